```python
import math, functools
import jax, jax.numpy as jnp
from jax import lax
import numpy as np

D_MODEL = 1024
BATCH = 16
SEQ = 256
DEPTH = 2
DEC_BATCH = 8
DEC_SEQ = 2048
PAST_LEN = 256

GRID_W = 64
HEAD_DIM = 64
ROPE_THETA = 10000.0
EPS = 1e-6
NEG = -1e30
Q_BLOCK = 128

MLA_HEADS = 8
MLA_Q_RANK = 384
MLA_KV_RANK = 256
MLA_NOPE = 64
MLA_ROPE = 32
MLA_V = 64
DIFF_HEADS = 4
DIFF_DH = 64
SWA_HEADS = 8
SWA_KV_HEADS = 2
SWA_WINDOW = 128
NA_HEADS = 8
NA_WIN_ROWS = 8
NA_WIN_COLS = 16
NA_QCOLS = 32
NA_KCOLS = NA_QCOLS + NA_WIN_COLS
N_EXPERTS = 32
TOP_K = 4
D_EXPERT = 1024
SWIGLU_LIMIT = 7.0
SWIGLU_ALPHA = 1.702
MOE_BLOCK = 128

IN0 = MLA_Q_RANK + MLA_KV_RANK + MLA_ROPE + 3 * DIFF_HEADS * 2 * DIFF_DH
MIX0 = MLA_HEADS * MLA_V + DIFF_HEADS * 2 * DIFF_DH
IN1 = SWA_HEADS * HEAD_DIM + 2 * SWA_KV_HEADS * HEAD_DIM + 3 * NA_HEADS * HEAD_DIM
MIX1 = SWA_HEADS * HEAD_DIM + NA_HEADS * HEAD_DIM

kernel_name = 'hybrid_diffusion_prefix_trunk_step'


def rms_norm(x, g):
    xf = x.astype(jnp.float32)
    y = xf * lax.rsqrt(jnp.mean(xf * xf, axis=-1, keepdims=True) + EPS)
    return (y * g.astype(jnp.float32)).astype(x.dtype)


def modulation(cond, w_mod, b_mod):
    m = jax.nn.silu(cond) @ w_mod + b_mod
    return [t[:, None, :] for t in jnp.split(m, 6, axis=-1)]


def modulate(h, shift, scale):
    return h * (1 + scale) + shift


def axial_rope(x):
    T, R = x.shape[1], x.shape[-1]
    n = R // 4
    inv = ROPE_THETA ** (-jnp.arange(n, dtype=jnp.float32) / n)
    t = jnp.arange(T)
    rows = (t // GRID_W).astype(jnp.float32)
    cols = (t % GRID_W).astype(jnp.float32)
    ang = jnp.concatenate([rows[:, None] * inv[None], cols[:, None] * inv[None]], axis=-1)
    ang = ang.reshape((1, T) + (1,) * (x.ndim - 3) + (R // 2,))
    cos, sin = jnp.cos(ang), jnp.sin(ang)
    xf = x.astype(jnp.float32)
    x1, x2 = xf[..., : R // 2], xf[..., R // 2:]
    return jnp.concatenate([x1 * cos - x2 * sin, x1 * sin + x2 * cos], axis=-1).astype(x.dtype)


def block_attention(q, k, v, scale, sink=None):
    B, T, H, dq = q.shape
    Hk = k.shape[2]
    G = H // Hk
    nb = T // Q_BLOCK
    qb = jnp.moveaxis(q.reshape(B, nb, Q_BLOCK, Hk, G, dq), 1, 0)

    def one(qi):
        s = jnp.einsum('bqhgd,bkhd->bhgqk', qi, k).astype(jnp.float32) * scale
        if sink is not None:
            s_sink = jnp.broadcast_to(sink.astype(jnp.float32)[None, :, :, None, None], s.shape[:-1] + (1,))
            p = jax.nn.softmax(jnp.concatenate([s_sink, s], axis=-1), axis=-1)[..., 1:]
        else:
            p = jax.nn.softmax(s, axis=-1)
        return jnp.einsum('bhgqk,bkhd->bqhgd', p.astype(v.dtype), v)

    out = lax.map(one, qb)
    return jnp.moveaxis(out, 0, 1).reshape(B, T, H, v.shape[-1])


def window_attention(q, k, v, k_ctx, v_ctx, sink, scale):
    B, T, H, d = q.shape
    Hk = k.shape[2]
    G = H // Hk
    W = SWA_WINDOW
    nb = T // Q_BLOCK
    KB = Q_BLOCK + 2 * W
    L = k_ctx.shape[1]
    pad = ((0, 0), (W, W), (0, 0), (0, 0))
    idx = (jnp.arange(nb) * Q_BLOCK)[:, None] + jnp.arange(KB)[None, :]
    kb = jnp.pad(k, pad)[:, idx]
    vb = jnp.pad(v, pad)[:, idx]
    key_pos = idx - W
    q_pos = jnp.arange(T).reshape(nb, Q_BLOCK)
    kp = key_pos[:, None, :]
    mask = (jnp.abs(q_pos[:, :, None] - kp) <= W) & (kp >= 0) & (kp < T)
    qb = q.reshape(B, nb, Q_BLOCK, Hk, G, d)
    s_lat = jnp.einsum('bnqhgd,bnkhd->bnhgqk', qb, kb).astype(jnp.float32) * scale
    s_lat = jnp.where(mask[None, :, None, None], s_lat, NEG)
    s_ctx = jnp.einsum('bnqhgd,bkhd->bnhgqk', qb, k_ctx).astype(jnp.float32) * scale
    s_sink = jnp.broadcast_to(sink.astype(jnp.float32)[None, None, :, :, None, None], s_ctx.shape[:-1] + (1,))
    p = jax.nn.softmax(jnp.concatenate([s_sink, s_ctx, s_lat], axis=-1), axis=-1).astype(v.dtype)
    o = (jnp.einsum('bnhgqk,bkhd->bnqhgd', p[..., 1:1 + L], v_ctx)
         + jnp.einsum('bnhgqk,bnkhd->bnqhgd', p[..., 1 + L:], vb))
    return o.reshape(B, T, H, d)


def neighbourhood_attention(q, k, v, k_ctx, v_ctx, rpb, scale):
    B, T, H, d = q.shape
    L = k_ctx.shape[1]
    rows_n = T // GRID_W
    kr = min(NA_WIN_ROWS, rows_n)
    ncb = GRID_W // NA_QCOLS
    col_start = jnp.clip(jnp.arange(GRID_W) - NA_WIN_COLS // 2, 0, GRID_W - NA_WIN_COLS)
    q_cols = (jnp.arange(ncb) * NA_QCOLS)[:, None] + jnp.arange(NA_QCOLS)[None, :]
    kc0 = jnp.clip(col_start[q_cols[:, 0]], 0, GRID_W - NA_KCOLS)
    key_cols = kc0[:, None] + jnp.arange(NA_KCOLS)[None, :]
    qs = col_start[q_cols]
    kc = key_cols[:, None, :]
    col_ok = (kc >= qs[:, :, None]) & (kc < qs[:, :, None] + NA_WIN_COLS)
    dc_idx = jnp.clip(kc - q_cols[:, :, None] + NA_WIN_COLS - 1, 0, 2 * NA_WIN_COLS - 2)
    qg = q.reshape(B, rows_n, ncb, NA_QCOLS, H, d)
    kg = k.reshape(B, rows_n, GRID_W, H, d)
    vg = v.reshape(B, rows_n, GRID_W, H, d)

    def one_row(r):
        rs = jnp.clip(r - kr // 2, 0, rows_n - kr)
        kb = lax.dynamic_slice_in_dim(kg, rs, kr, axis=1)[:, :, key_cols]
        vb = lax.dynamic_slice_in_dim(vg, rs, kr, axis=1)[:, :, key_cols]
        qr = lax.dynamic_index_in_dim(qg, r, axis=1, keepdims=False)
        dr_idx = rs + jnp.arange(kr) - r + NA_WIN_ROWS - 1
        bias = rpb[:, dr_idx[None, None, :, None], dc_idx[:, :, None, :]]
        s_lat = jnp.einsum('bcqhd,bicjhd->bchqij', qr, kb).astype(jnp.float32) * scale
        s_lat = s_lat + jnp.moveaxis(bias, 0, 1).astype(jnp.float32)[None]
        s_lat = jnp.where(col_ok[:, None, :, None, :], s_lat, NEG)
        s_lat = s_lat.reshape(B, ncb, H, NA_QCOLS, kr * NA_KCOLS)
        s_ctx = jnp.einsum('bcqhd,bkhd->bchqk', qr, k_ctx).astype(jnp.float32) * scale
        p = jax.nn.softmax(jnp.concatenate([s_ctx, s_lat], axis=-1), axis=-1).astype(v.dtype)
        p_lat = p[..., L:].reshape(B, ncb, H, NA_QCOLS, kr, NA_KCOLS)
        return (jnp.einsum('bchqk,bkhd->bcqhd', p[..., :L], v_ctx)
                + jnp.einsum('bchqij,bicjhd->bcqhd', p_lat, vb))

    out = lax.map(one_row, jnp.arange(rows_n))
    return jnp.moveaxis(out, 0, 1).reshape(B, T, H, d)


def mixer_ab(h, ctx, w_in, q_norm, w_uq, kv_norm, w_ukv, lam, sub_g, w_out, lam_init):
    B, T, _ = h.shape
    dsz = DIFF_HEADS * 2 * DIFF_DH
    sizes = [MLA_Q_RANK, MLA_KV_RANK, MLA_ROPE, dsz, dsz, dsz]
    q_a, kv_a, k_pe, dq, dk, dv = jnp.split(h @ w_in, np.cumsum(sizes)[:-1].tolist(), axis=-1)
    q = (rms_norm(q_a, q_norm) @ w_uq).reshape(B, T, MLA_HEADS, MLA_NOPE + MLA_ROPE)
    q_nope, q_pe = q[..., :MLA_NOPE], q[..., MLA_NOPE:]
    c_kv = rms_norm(kv_a, kv_norm)
    dq = dq.reshape(B, T, DIFF_HEADS, 2, DIFF_DH)
    dk = dk.reshape(B, T, DIFF_HEADS, 2, DIFF_DH)
    dv = dv.reshape(B, T, DIFF_HEADS, 2 * DIFF_DH)
    if ctx is None:
        new_state = (c_kv, k_pe, dk.reshape(B, T, DIFF_HEADS, 2 * DIFF_DH), dv)
        kv_c, kv_pe, kv_dk, kv_dv = new_state
    else:
        q_pe = axial_rope(q_pe)
        dq = axial_rope(dq)
        k_pe = axial_rope(k_pe[:, :, None, :])[:, :, 0]
        dk = axial_rope(dk).reshape(B, T, DIFF_HEADS, 2 * DIFF_DH)
        ckv_c, kpe_c, dk_c, dv_c = ctx
        kv_c = jnp.concatenate([ckv_c, c_kv], axis=1)
        kv_pe = jnp.concatenate([kpe_c, k_pe], axis=1)
        kv_dk = jnp.concatenate([dk_c, dk], axis=1)
        kv_dv = jnp.concatenate([dv_c, dv], axis=1)
        new_state = None
    S = kv_c.shape[1]
    kv = (kv_c @ w_ukv).reshape(B, S, MLA_HEADS, MLA_NOPE + MLA_V)
    k_mla = jnp.concatenate(
        [kv[..., :MLA_NOPE], jnp.broadcast_to(kv_pe[:, :, None, :], (B, S, MLA_HEADS, MLA_ROPE))], axis=-1)
    q_mla = jnp.concatenate([q_nope, q_pe], axis=-1)
    o_mla = block_attention(q_mla, k_mla, kv[..., MLA_NOPE:], (MLA_NOPE + MLA_ROPE) ** -0.5)
    kd = kv_dk.reshape(B, S, DIFF_HEADS, 2, DIFF_DH)
    sc = DIFF_DH ** -0.5
    o1 = block_attention(dq[:, :, :, 0], kd[:, :, :, 0], kv_dv, sc)
    o2 = block_attention(dq[:, :, :, 1], kd[:, :, :, 1], kv_dv, sc)
    lf = lam.astype(jnp.float32)
    lam_full = jnp.exp(jnp.sum(lf[0] * lf[1])) - jnp.exp(jnp.sum(lf[2] * lf[3])) + lam_init
    o_diff = rms_norm(o1 - lam_full.astype(o1.dtype) * o2, sub_g) * (1.0 - lam_init)
    o = jnp.concatenate([o_mla.reshape(B, T, -1), o_diff.reshape(B, T, -1)], axis=-1) @ w_out
    return o, new_state


def mixer_cd(h, ctx, w_in, sink, rpb, w_out):
    B, T, _ = h.shape
    sizes = [SWA_HEADS * HEAD_DIM, SWA_KV_HEADS * HEAD_DIM, SWA_KV_HEADS * HEAD_DIM,
             NA_HEADS * HEAD_DIM, NA_HEADS * HEAD_DIM, NA_HEADS * HEAD_DIM]
    sq, sk, sv, nq, nk, nv = jnp.split(h @ w_in, np.cumsum(sizes)[:-1].tolist(), axis=-1)
    sq = sq.reshape(B, T, SWA_HEADS, HEAD_DIM)
    sk = sk.reshape(B, T, SWA_KV_HEADS, HEAD_DIM)
    sv = sv.reshape(B, T, SWA_KV_HEADS, HEAD_DIM)
    nq = nq.reshape(B, T, NA_HEADS, HEAD_DIM)
    nk = nk.reshape(B, T, NA_HEADS, HEAD_DIM)
    nv = nv.reshape(B, T, NA_HEADS, HEAD_DIM)
    scale = HEAD_DIM ** -0.5
    sink_hg = sink.reshape(SWA_KV_HEADS, SWA_HEADS // SWA_KV_HEADS)
    if ctx is None:
        o_swa = block_attention(sq, sk, sv, scale, sink_hg)
        o_na = block_attention(nq, nk, nv, scale)
        new_state = (sk, sv, nk, nv)
    else:
        sk_c, sv_c, nk_c, nv_c = ctx
        o_swa = window_attention(axial_rope(sq), axial_rope(sk), sv, sk_c, sv_c, sink_hg, scale)
        o_na = neighbourhood_attention(nq, nk, nv, nk_c, nv_c, rpb, scale)
        new_state = None
    o = jnp.concatenate([o_swa.reshape(B, T, -1), o_na.reshape(B, T, -1)], axis=-1) @ w_out
    return o, new_state


def moe_ffn(h, w_router, b_router, w_gu, b_gu, w_down, b_down):
    B, T, D = h.shape
    xt = h.reshape(-1, D)
    N = xt.shape[0]
    logits = (xt @ w_router + b_router).astype(jnp.float32)
    top_v, top_i = lax.top_k(logits, TOP_K)
    gates = jax.nn.softmax(top_v, axis=-1)
    NK = N * TOP_K
    flat_e = top_i.reshape(-1)
    order = jnp.argsort(flat_e)
    sorted_e = flat_e[order]
    counts = jnp.zeros((N_EXPERTS,), jnp.int32).at[flat_e].add(1)
    padded = (counts + MOE_BLOCK - 1) // MOE_BLOCK * MOE_BLOCK
    pad_end = jnp.cumsum(padded)
    pad_start = pad_end - padded
    grp_start = jnp.cumsum(counts) - counts
    dest = pad_start[sorted_e] + (jnp.arange(NK) - grp_start[sorted_e])
    n_blocks = (NK + N_EXPERTS * (MOE_BLOCK - 1) + MOE_BLOCK - 1) // MOE_BLOCK
    P = n_blocks * MOE_BLOCK
    tok = jnp.full((P,), N, jnp.int32).at[dest].set((order // TOP_K).astype(jnp.int32))
    gate_buf = jnp.zeros((P,), jnp.float32).at[dest].set(gates.reshape(-1)[order])
    blk_e = jnp.minimum(jnp.searchsorted(pad_end, jnp.arange(n_blocks) * MOE_BLOCK, side='right'), N_EXPERTS - 1)
    x_pad = jnp.concatenate([xt, jnp.zeros((1, D), xt.dtype)], axis=0)
    xs = x_pad[tok].reshape(n_blocks, MOE_BLOCK, D)

    def expert_block(args):
        xb, e = args
        gu = xb @ w_gu[e] + b_gu[e]
        g, u = gu[..., :D_EXPERT], gu[..., D_EXPERT:]
        g = jnp.minimum(g, SWIGLU_LIMIT)
        u = jnp.clip(u, -SWIGLU_LIMIT, SWIGLU_LIMIT)
        a = g * jax.nn.sigmoid(SWIGLU_ALPHA * g) * (u + 1)
        return a @ w_down[e] + b_down[e]

    ys = lax.map(expert_block, (xs, blk_e)).reshape(P, D)
    out = jax.ops.segment_sum(ys * gate_buf[:, None].astype(ys.dtype), tok, num_segments=N + 1)[:N]
    return out.reshape(B, T, D)


def setup_inputs(seed: int = 0) -> dict:
    key = jax.random.key(seed)
    ks = iter(jax.random.split(key, 40))
    ne, no = (DEPTH + 1) // 2, DEPTH // 2
    D = D_MODEL

    def nrm(shape, scale=1.0):
        return jax.random.normal(next(ks), shape, jnp.float32) * scale

    def gain(shape):
        return 1.0 + nrm(shape, 0.05)

    return {
        'x_prompt': nrm((BATCH, SEQ, D)),
        'x_sample': nrm((DEC_BATCH, DEC_SEQ, D)),
        'cache_mla_ckv': nrm((DEC_BATCH, ne, PAST_LEN, MLA_KV_RANK)),
        'cache_mla_krope': nrm((DEC_BATCH, ne, PAST_LEN, MLA_ROPE)),
        'cache_diff_k': nrm((DEC_BATCH, ne, PAST_LEN, DIFF_HEADS, 2 * DIFF_DH)),
        'cache_diff_v': nrm((DEC_BATCH, ne, PAST_LEN, DIFF_HEADS, 2 * DIFF_DH)),
        'cache_swa_k': nrm((DEC_BATCH, no, PAST_LEN, SWA_KV_HEADS, HEAD_DIM)),
        'cache_swa_v': nrm((DEC_BATCH, no, PAST_LEN, SWA_KV_HEADS, HEAD_DIM)),
        'cache_na_k': nrm((DEC_BATCH, no, PAST_LEN, NA_HEADS, HEAD_DIM)),
        'cache_na_v': nrm((DEC_BATCH, no, PAST_LEN, NA_HEADS, HEAD_DIM)),
        'c': nrm((DEC_BATCH, D)),
        'c_ctx': nrm((D,)),
        'w_mod': nrm((DEPTH, D, 6 * D), 0.5 * D ** -0.5),
        'b_mod': nrm((DEPTH, 6 * D), 0.02),
        'norm_g': gain((DEPTH, 4, D)),
        'w_in0': nrm((ne, D, IN0), D ** -0.5),
        'mla_q_norm': gain((ne, MLA_Q_RANK)),
        'w_uq': nrm((ne, MLA_Q_RANK, MLA_HEADS * (MLA_NOPE + MLA_ROPE)), MLA_Q_RANK ** -0.5),
        'mla_kv_norm': gain((ne, MLA_KV_RANK)),
        'w_ukv': nrm((ne, MLA_KV_RANK, MLA_HEADS * (MLA_NOPE + MLA_V)), MLA_KV_RANK ** -0.5),
        'diff_lambda': nrm((ne, 4, DIFF_DH), 0.1),
        'diff_norm': gain((ne, 2 * DIFF_DH)),
        'w_out0': nrm((ne, MIX0, D), MIX0 ** -0.5),
        'w_in1': nrm((no, D, IN1), D ** -0.5),
        'swa_sink': nrm((no, SWA_HEADS), 0.5),
        'na_rpb': nrm((no, NA_HEADS, 2 * NA_WIN_ROWS - 1, 2 * NA_WIN_COLS - 1), 0.5),
        'w_out1': nrm((no, MIX1, D), MIX1 ** -0.5),
        'w_router': nrm((DEPTH, D, N_EXPERTS), D ** -0.5),
        'b_router': nrm((DEPTH, N_EXPERTS), 0.01),
        'w_gu': nrm((DEPTH, N_EXPERTS, D, 2 * D_EXPERT), D ** -0.5),
        'b_gu': nrm((DEPTH, N_EXPERTS, 2 * D_EXPERT), 0.02),
        'w_down': nrm((DEPTH, N_EXPERTS, D_EXPERT, D), D_EXPERT ** -0.5),
        'b_down': nrm((DEPTH, N_EXPERTS, D), 0.02),
    }


def reference(x_prompt, x_sample, cache_mla_ckv, cache_mla_krope, cache_diff_k, cache_diff_v,
              cache_swa_k, cache_swa_v, cache_na_k, cache_na_v, c, c_ctx,
              w_mod, b_mod, norm_g, w_in0, mla_q_norm, w_uq, mla_kv_norm, w_ukv,
              diff_lambda, diff_norm, w_out0, w_in1, swa_sink, na_rpb, w_out1,
              w_router, b_router, w_gu, b_gu, w_down, b_down):
    B = x_prompt.shape[0]
    c_ctx_b = jnp.broadcast_to(c_ctx[None, :], (B, D_MODEL))
    xp, xs = x_prompt, x_sample
    st_ab = ([], [], [], [])
    st_cd = ([], [], [], [])
    for l in range(DEPTH):
        i = l // 2
        mp = modulation(c_ctx_b, w_mod[l], b_mod[l])
        ms = modulation(c, w_mod[l], b_mod[l])
        g = norm_g[l]
        if l % 2 == 0:
            mix = functools.partial(
                mixer_ab, w_in=w_in0[i], q_norm=mla_q_norm[i], w_uq=w_uq[i], kv_norm=mla_kv_norm[i],
                w_ukv=w_ukv[i], lam=diff_lambda[i], sub_g=diff_norm[i], w_out=w_out0[i],
                lam_init=0.8 - 0.6 * math.exp(-0.3 * l))
            ctx = (cache_mla_ckv[:, i], cache_mla_krope[:, i], cache_diff_k[:, i], cache_diff_v[:, i])
            sts = st_ab
        else:
            mix = functools.partial(mixer_cd, w_in=w_in1[i], sink=swa_sink[i], rpb=na_rpb[i], w_out=w_out1[i])
            ctx = (cache_swa_k[:, i], cache_swa_v[:, i], cache_na_k[:, i], cache_na_v[:, i])
            sts = st_cd
        yp, new = mix(modulate(rms_norm(xp, g[0]), mp[0], mp[1]), ctx=None)
        xp = xp + mp[2] * rms_norm(yp, g[1])
        ys, _ = mix(modulate(rms_norm(xs, g[0]), ms[0], ms[1]), ctx=ctx)
        xs = xs + ms[2] * rms_norm(ys, g[1])
        for lst, t in zip(sts, new):
            lst.append(t)
        ffn = functools.partial(moe_ffn, w_router=w_router[l], b_router=b_router[l], w_gu=w_gu[l],
                                b_gu=b_gu[l], w_down=w_down[l], b_down=b_down[l])
        xp = xp + mp[5] * rms_norm(ffn(modulate(rms_norm(xp, g[2]), mp[3], mp[4])), g[3])
        xs = xs + ms[5] * rms_norm(ffn(modulate(rms_norm(xs, g[2]), ms[3], ms[4])), g[3])
    state_mla_ckv = jnp.stack(st_ab[0], axis=1)
    state_mla_krope = jnp.stack(st_ab[1], axis=1)
    state_diff_k = jnp.stack(st_ab[2], axis=1)
    state_diff_v = jnp.stack(st_ab[3], axis=1)
    state_swa_k = jnp.stack(st_cd[0], axis=1)
    state_swa_v = jnp.stack(st_cd[1], axis=1)
    state_na_k = jnp.stack(st_cd[2], axis=1)
    state_na_v = jnp.stack(st_cd[3], axis=1)
    return (xp, xs, state_mla_ckv, state_mla_krope, state_diff_k, state_diff_v,
            state_swa_k, state_swa_v, state_na_k, state_na_v)
```

```python
import functools
import math

import numpy as np
import jax
import jax.numpy as jnp
from jax import lax
from jax.experimental import pallas as pl
from jax.experimental.pallas import tpu as pltpu

F32 = jnp.float32
BF16 = jnp.bfloat16

D_MODEL = 1024
BATCH = 16
SEQ = 256
DEPTH = 2
DEC_BATCH = 8
DEC_SEQ = 2048
PAST_LEN = 256
GRID_W = 64
HEAD_DIM = 64
ROPE_THETA = 10000.0
EPS = 1e-6
NEG = -1e30

MLA_HEADS = 8
MLA_Q_RANK = 384
MLA_KV_RANK = 256
MLA_NOPE = 64
MLA_ROPE = 32
MLA_V = 64
DIFF_HEADS = 4
DIFF_DH = 64
SWA_HEADS = 8
SWA_KV_HEADS = 2
SWA_WINDOW = 128
NA_HEADS = 8
NA_WIN_ROWS = 8
NA_WIN_COLS = 16
N_EXPERTS = 32
TOP_K = 4
D_EXPERT = 1024
SWIGLU_LIMIT = 7.0
SWIGLU_ALPHA = 1.702

LANES = 128
NP_TOK = BATCH * SEQ
NS_TOK = DEC_BATCH * DEC_SEQ
N_TOK = NP_TOK + NS_TOK
TM = 512
NPT = NP_TOK // TM
TILES_PER_SAMPLE = DEC_SEQ // TM
N_TILES = N_TOK // TM
TQ = 256
MOE_TM = 256
MOE_ROWS = ((N_TOK * TOP_K + N_EXPERTS * (MOE_TM - 1)) // MOE_TM + 1) * MOE_TM
MOE_BLOCKS = MOE_ROWS // MOE_TM
NA_TILE_ROWS = TQ // GRID_W
NA_KEY_ROWS = 12
VMEM_LIMIT = 56 * 1024 * 1024


def _cparams(n_axes, vmem=VMEM_LIMIT):
    return pltpu.CompilerParams(dimension_semantics=("arbitrary",) * n_axes,
                                vmem_limit_bytes=vmem)


def _rms(x, g):
    return x * lax.rsqrt(jnp.mean(x * x, axis=-1, keepdims=True) + EPS) * g


def _dot(a, b):
    return jnp.dot(a, b, preferred_element_type=F32)


def _dot_nt(a, b):
    return lax.dot_general(a, b, (((1,), (1,)), ((), ())), preferred_element_type=F32)


def _rope(x, cos, sin_a, sin_b, half):
    return (x * cos + pltpu.roll(x, LANES - half, 1) * sin_a + pltpu.roll(x, half, 1) * sin_b)


def _mod_row(i):
    return jnp.where(i < NPT, 0, 1 + (i - NPT) // TILES_PER_SAMPLE)


def _rope_blk(i):
    return jnp.where(i < NPT, TILES_PER_SAMPLE, (i - NPT) % TILES_PER_SAMPLE)


def _mod_kernel(c_ref, w_ref, b_ref, o_ref):
    c = c_ref[...]
    s = (c * jax.nn.sigmoid(c)).astype(BF16)
    o_ref[0] = _dot(s, w_ref[0].astype(BF16)) + b_ref[0]


def _modulation(cond, w_mod, b_mod):
    nb = 1024
    return pl.pallas_call(
        _mod_kernel,
        grid=(DEPTH, 6 * D_MODEL // nb),
        in_specs=[
            pl.BlockSpec((16, D_MODEL), lambda l, n: (0, 0)),
            pl.BlockSpec((1, D_MODEL, nb), lambda l, n: (l, 0, n)),
            pl.BlockSpec((1, 1, nb), lambda l, n: (l, 0, n)),
        ],
        out_specs=pl.BlockSpec((1, 16, nb), lambda l, n: (l, 0, n)),
        out_shape=jax.ShapeDtypeStruct((DEPTH, 16, 6 * D_MODEL), F32),
        compiler_params=_cparams(2),
        name="modulation",
    )(cond, w_mod, b_mod.reshape(DEPTH, 1, 6 * D_MODEL))


def _rope_tables():
    t = jnp.arange(DEC_SEQ)
    rows = (t // GRID_W).astype(F32)
    cols = (t % GRID_W).astype(F32)

    def angles(r):
        n = r // 4
        inv = ROPE_THETA ** (-jnp.arange(n, dtype=F32) / n)
        return jnp.concatenate([rows[:, None] * inv[None], cols[:, None] * inv[None]], axis=-1)

    def finish(cos, sa, sb):
        ident = (jnp.ones((TM, LANES), F32), jnp.zeros((TM, LANES), F32), jnp.zeros((TM, LANES), F32))
        return tuple(jnp.concatenate([a, b], axis=0) for a, b in zip((cos, sa, sb), ident))

    a64 = angles(64)
    c, s, z = jnp.cos(a64), jnp.sin(a64), jnp.zeros_like(a64)
    t64 = finish(jnp.concatenate([c, c, c, c], -1), jnp.concatenate([-s, z, -s, z], -1),
                 jnp.concatenate([z, s, z, s], -1))
    a32 = angles(32)
    c, s, z = jnp.cos(a32), jnp.sin(a32), jnp.zeros_like(a32)
    one64 = jnp.ones((DEC_SEQ, 64), F32)
    z64 = jnp.zeros((DEC_SEQ, 64), F32)
    z32 = jnp.zeros((DEC_SEQ, 32), F32)
    t32 = finish(jnp.concatenate([one64, c, c, z32], -1), jnp.concatenate([z64, -s, z, z32], -1),
                 jnp.concatenate([z64, z, s, z32], -1))
    return t64, t32


_MLA_SCALE = (MLA_NOPE + MLA_ROPE) ** -0.5
_QSCALE = HEAD_DIM ** -0.5


def _premix0_kernel(x_ref, mod_ref, g_ref, win_ref, qn_ref, wuq_ref, kvn_ref, wk_ref, wv_ref,
                    c32_ref, sa32_ref, sb32_ref, c64_ref, sa64_ref, sb64_ref,
                    q_ref, k_ref, v_ref, dq_ref, dk_ref, dv_ref,
                    ckv_st, kpe_st, dk_st, dv_st):
    i = pl.program_id(0)
    m = mod_ref[0]
    h = _rms(x_ref[...], g_ref[...]) * (1.0 + m[1:2]) + m[0:1]
    proj = _dot(h.astype(BF16), win_ref[...])
    q_a = proj[:, 0:384]
    kv_a = proj[:, 384:640]
    dq = proj[:, 640:1152]
    dk = proj[:, 1152:1664]
    dv = proj[:, 1664:2176]
    pe = proj[:, 2176:2304]
    q = _dot(_rms(q_a, qn_ref[...]).astype(BF16), wuq_ref[...])
    ckv = _rms(kv_a, kvn_ref[...])
    ckv_b = ckv.astype(BF16)
    kn = _dot(ckv_b, wk_ref[...])
    v_ref[...] = _dot(ckv_b, wv_ref[...]).astype(BF16)
    c32, sa32, sb32 = c32_ref[...], sa32_ref[...], sb32_ref[...]
    c64, sa64, sb64 = c64_ref[...], sa64_ref[...], sb64_ref[...]
    pe_r = _rope(pe, c32, sa32, sb32, MLA_ROPE // 2)
    for hd in range(MLA_HEADS):
        sl = slice(LANES * hd, LANES * (hd + 1))
        q_ref[:, sl] = (_rope(q[:, sl], c32, sa32, sb32, MLA_ROPE // 2) * _MLA_SCALE).astype(BF16)
        k_ref[:, sl] = (kn[:, sl] + pe_r).astype(BF16)
    for hd in range(DIFF_HEADS):
        sl = slice(LANES * hd, LANES * (hd + 1))
        dq_ref[:, sl] = (_rope(dq[:, sl], c64, sa64, sb64, DIFF_DH // 2) * _QSCALE).astype(BF16)
        dk_ref[:, sl] = _rope(dk[:, sl], c64, sa64, sb64, DIFF_DH // 2).astype(BF16)
    dv_ref[...] = dv.astype(BF16)

    @pl.when(i < NPT)
    def _():
        ckv_st[...] = ckv
        kpe_st[...] = pe
        dk_st[...] = dk
        dv_st[...] = dv


def _premix1_kernel(x_ref, mod_ref, g_ref, win_ref, c64_ref, sa64_ref, sb64_ref,
                    sq_ref, sk_ref, sv_ref, nq_ref, nk_ref, nv_ref,
                    sk_st, sv_st, nk_st, nv_st):
    i = pl.program_id(0)
    m = mod_ref[0]
    h = _rms(x_ref[...], g_ref[...]) * (1.0 + m[1:2]) + m[0:1]
    proj = _dot(h.astype(BF16), win_ref[...])
    sq = proj[:, 0:512]
    sk = proj[:, 512:640]
    sv = proj[:, 640:768]
    nq = proj[:, 768:1280]
    nk = proj[:, 1280:1792]
    nv = proj[:, 1792:2304]
    c64, sa64, sb64 = c64_ref[...], sa64_ref[...], sb64_ref[...]
    for hd in range(4):
        sl = slice(LANES * hd, LANES * (hd + 1))
        sq_ref[:, sl] = (_rope(sq[:, sl], c64, sa64, sb64, HEAD_DIM // 2) * _QSCALE).astype(BF16)
    sk_ref[...] = _rope(sk, c64, sa64, sb64, HEAD_DIM // 2).astype(BF16)
    sv_ref[...] = sv.astype(BF16)
    nq_ref[...] = (nq * _QSCALE).astype(BF16)
    nk_ref[...] = nk.astype(BF16)
    nv_ref[...] = nv.astype(BF16)

    @pl.when(i < NPT)
    def _():
        sk_st[...] = sk
        sv_st[...] = sv
        nk_st[...] = nk
        nv_st[...] = nv


def _tok_spec(width):
    return pl.BlockSpec((TM, width), lambda i: (i, 0))


def _state_spec(width):
    return pl.BlockSpec((TM, width), lambda i: (jnp.minimum(i, NPT - 1), 0))


def _const_spec(shape):
    return pl.BlockSpec(shape, lambda i: (0,) * len(shape))


_MOD_SPEC = pl.BlockSpec((1, 6, D_MODEL), lambda i: (_mod_row(i), 0, 0))
_ROPE_SPEC = pl.BlockSpec((TM, LANES), lambda i: (_rope_blk(i), 0))


def _premix0(x, mod_l, g0, w_in_p, q_norm, w_uq_p, kv_norm, w_k_p, w_v, t32, t64):
    outs = [(N_TOK, 1024, BF16), (N_TOK, 1024, BF16), (N_TOK, 512, BF16), (N_TOK, 512, BF16),
            (N_TOK, 512, BF16), (N_TOK, 512, BF16),
            (NP_TOK, 256, F32), (NP_TOK, 128, F32), (NP_TOK, 512, F32), (NP_TOK, 512, F32)]
    return pl.pallas_call(
        _premix0_kernel,
        grid=(N_TILES,),
        in_specs=[_tok_spec(D_MODEL), _MOD_SPEC, _const_spec((1, D_MODEL)),
                  _const_spec((D_MODEL, 2304)), _const_spec((1, MLA_Q_RANK)),
                  _const_spec((MLA_Q_RANK, 1024)), _const_spec((1, MLA_KV_RANK)),
                  _const_spec((MLA_KV_RANK, 1024)), _const_spec((MLA_KV_RANK, 512))]
                 + [_ROPE_SPEC] * 6,
        out_specs=[_tok_spec(w) for (_, w, _) in outs[:6]] + [_state_spec(w) for (_, w, _) in outs[6:]],
        out_shape=[jax.ShapeDtypeStruct((n, w), dt) for (n, w, dt) in outs],
        compiler_params=_cparams(1),
        name="premix_ab",
    )(x, mod_l, g0, w_in_p, q_norm, w_uq_p, kv_norm, w_k_p, w_v, *t32, *t64)


def _premix1(x, mod_l, g0, w_in_p, t64):
    outs = [(N_TOK, 512, BF16), (N_TOK, 128, BF16), (N_TOK, 128, BF16), (N_TOK, 512, BF16),
            (N_TOK, 512, BF16), (N_TOK, 512, BF16),
            (NP_TOK, 128, F32), (NP_TOK, 128, F32), (NP_TOK, 512, F32), (NP_TOK, 512, F32)]
    return pl.pallas_call(
        _premix1_kernel,
        grid=(N_TILES,),
        in_specs=[_tok_spec(D_MODEL), _MOD_SPEC, _const_spec((1, D_MODEL)),
                  _const_spec((D_MODEL, 2304))] + [_ROPE_SPEC] * 3,
        out_specs=[_tok_spec(w) for (_, w, _) in outs[:6]] + [_state_spec(w) for (_, w, _) in outs[6:]],
        out_shape=[jax.ShapeDtypeStruct((n, w), dt) for (n, w, dt) in outs],
        compiler_params=_cparams(1),
        name="premix_cd",
    )(x, mod_l, g0, w_in_p, *t64)


def _mla_cache_kernel(ckv_ref, pe_ref, wk_ref, wv_ref, k_ref, v_ref):
    c = ckv_ref[...].astype(BF16)
    kn = _dot(c, wk_ref[...])
    v_ref[...] = _dot(c, wv_ref[...]).astype(BF16)
    pe = pe_ref[...]
    for hd in range(MLA_HEADS):
        sl = slice(LANES * hd, LANES * (hd + 1))
        k_ref[:, sl] = (kn[:, sl] + pe).astype(BF16)


def _mla_cache(ckv, pe_slab, w_k_p, w_v):
    n = ckv.shape[0]
    tm = 512
    return pl.pallas_call(
        _mla_cache_kernel,
        grid=(n // tm,),
        in_specs=[pl.BlockSpec((tm, MLA_KV_RANK), lambda i: (i, 0)),
                  pl.BlockSpec((tm, LANES), lambda i: (i, 0)),
                  _const_spec((MLA_KV_RANK, 1024)), _const_spec((MLA_KV_RANK, 512))],
        out_specs=[pl.BlockSpec((tm, 1024), lambda i: (i, 0)), pl.BlockSpec((tm, 512), lambda i: (i, 0))],
        out_shape=[jax.ShapeDtypeStruct((n, 1024), BF16), jax.ShapeDtypeStruct((n, 512), BF16)],
        compiler_params=_cparams(1),
        name="mla_cache",
    )(ckv, pe_slab, w_k_p, w_v)


def _softmax_pv(scores, values, sink=None):
    m = jnp.max(scores[0], axis=-1, keepdims=True)
    for s in scores[1:]:
        m = jnp.maximum(m, jnp.max(s, axis=-1, keepdims=True))
    if sink is not None:
        m = jnp.maximum(m, sink)
    l = None
    o = None
    for s, v in zip(scores, values):
        p = jnp.exp(s - m)
        ls = jnp.sum(p, axis=-1, keepdims=True)
        os_ = _dot(p.astype(BF16), v)
        l = ls if l is None else l + ls
        o = os_ if o is None else o + os_
    if sink is not None:
        l = l + jnp.exp(sink - m)
    return o * (1.0 / l)


def _lane_lo(shape):
    return lax.broadcasted_iota(jnp.int32, shape, 1) < (LANES // 2)


def _split_halves(qb):
    lo = _lane_lo(qb.shape)
    zero = jnp.zeros_like(qb)
    return jnp.where(lo, qb, zero), jnp.where(lo, zero, qb)


def _attn_ab_kernel(*refs, n_pieces, lam_init):
    q_ref, dq_ref = refs[0], refs[1]
    pieces = [refs[2 + 4 * p: 6 + 4 * p] for p in range(n_pieces)]
    lam_ref, subg_ref, o_ref = refs[2 + 4 * n_pieces:]
    lam = lam_ref[...]
    lam_full = (jnp.exp(jnp.sum(lam[0:1] * lam[1:2], axis=-1, keepdims=True))
                - jnp.exp(jnp.sum(lam[2:3] * lam[3:4], axis=-1, keepdims=True)) + lam_init)
    lo = _lane_lo((q_ref.shape[0], LANES))
    for j in range(MLA_HEADS // 2):
        pair = []
        for half in range(2):
            hd = 2 * j + half
            sl = slice(LANES * hd, LANES * (hd + 1))
            qh = q_ref[:, sl]
            scores = [_dot_nt(qh, k_ref[:, sl]) for (k_ref, _, _, _) in pieces]
            vals = [v_ref[:, LANES * j: LANES * (j + 1)] for (_, v_ref, _, _) in pieces]
            pair.append(_softmax_pv(scores, vals))
        o_ref[:, LANES * j: LANES * (j + 1)] = jnp.where(lo, pair[0], pair[1]).astype(BF16)
    subg = subg_ref[...]
    for hd in range(DIFF_HEADS):
        sl = slice(LANES * hd, LANES * (hd + 1))
        q1, q2 = _split_halves(dq_ref[:, sl])
        ks = [dk_ref[:, sl] for (_, _, dk_ref, _) in pieces]
        vs = [dv_ref[:, sl] for (_, _, _, dv_ref) in pieces]
        o1 = _softmax_pv([_dot_nt(q1, k) for k in ks], vs)
        o2 = _softmax_pv([_dot_nt(q2, k) for k in ks], vs)
        od = _rms(o1 - lam_full * o2, subg) * (1.0 - lam_init)
        o_ref[:, 512 + LANES * hd: 512 + LANES * (hd + 1)] = od.astype(BF16)


def _attn_ab(q, dq, new_kv, cache_kv, lam, sub_g, lam_init, *, n_batch, t_len, tq, tok_off):
    nq = t_len // tq
    q_off = tok_off // tq
    b_off = tok_off // t_len
    widths = (1024, 512, 512, 512)
    in_specs = [pl.BlockSpec((tq, 1024), lambda b, i: (q_off + b * nq + i, 0)),
                pl.BlockSpec((tq, 512), lambda b, i: (q_off + b * nq + i, 0))]
    args = [q, dq]
    for w, a in zip(widths, new_kv):
        in_specs.append(pl.BlockSpec((t_len, w), lambda b, i: (b_off + b, 0)))
        args.append(a)
    n_pieces = 1
    if cache_kv is not None:
        n_pieces = 2
        for w, a in zip(widths, cache_kv):
            in_specs.append(pl.BlockSpec((PAST_LEN, w), lambda b, i: (b, 0)))
            args.append(a)
    in_specs += [pl.BlockSpec((4, DIFF_DH), lambda b, i: (0, 0)),
                 pl.BlockSpec((1, 2 * DIFF_DH), lambda b, i: (0, 0))]
    args += [lam, sub_g]
    return pl.pallas_call(
        functools.partial(_attn_ab_kernel, n_pieces=n_pieces, lam_init=lam_init),
        grid=(n_batch, nq),
        in_specs=in_specs,
        out_specs=pl.BlockSpec((tq, 1024), lambda b, i: (b * nq + i, 0)),
        out_shape=jax.ShapeDtypeStruct((n_batch * t_len, 1024), BF16),
        compiler_params=_cparams(2),
        name="attn_ab_%d" % n_pieces,
    )(*args)


def _attn_cd_prompt_kernel(sink_ref, sq_ref, sk_ref, sv_ref, nq_ref, nk_ref, nv_ref, o_ref):
    lo = _lane_lo((sq_ref.shape[0], LANES))
    sk = sk_ref[...]
    sv = sv_ref[...]
    for j in range(4):
        sl = slice(LANES * j, LANES * (j + 1))
        q_lo, q_hi = _split_halves(sq_ref[:, sl])
        o_lo = _softmax_pv([_dot_nt(q_lo, sk)], [sv], sink=sink_ref[j])
        o_hi = _softmax_pv([_dot_nt(q_hi, sk)], [sv], sink=sink_ref[j + 4])
        o_ref[:, sl] = jnp.where(lo, o_lo, o_hi).astype(BF16)
    for j in range(4):
        sl = slice(LANES * j, LANES * (j + 1))
        q_lo, q_hi = _split_halves(nq_ref[:, sl])
        k = nk_ref[:, sl]
        v = nv_ref[:, sl]
        o_lo = _softmax_pv([_dot_nt(q_lo, k)], [v])
        o_hi = _softmax_pv([_dot_nt(q_hi, k)], [v])
        o_ref[:, 512 + LANES * j: 512 + LANES * (j + 1)] = jnp.where(lo, o_lo, o_hi).astype(BF16)


def _attn_cd_prompt(sink, sq, sk, sv, nq, nk, nv):
    def spec(w):
        return pl.BlockSpec((SEQ, w), lambda b: (b, 0))
    return pl.pallas_call(
        _attn_cd_prompt_kernel,
        grid=(BATCH,),
        in_specs=[pl.BlockSpec(memory_space=pltpu.SMEM), spec(512), spec(128), spec(128),
                  spec(512), spec(512), spec(512)],
        out_specs=spec(1024),
        out_shape=jax.ShapeDtypeStruct((NP_TOK, 1024), BF16),
        compiler_params=_cparams(1),
        name="attn_cd_prompt",
    )(sink, sq, sk, sv, nq, nk, nv)


_SWA_KEYS = TQ + 2 * SWA_WINDOW


def _attn_cd_sample_kernel(sink_ref, sq_ref, nq_ref, sk_ref, sv_ref, nk_ref, nv_ref,
                           skc_ref, svc_ref, nkc_ref, nvc_ref, bias_ref, o_ref):
    qi = pl.program_id(1)
    lo = _lane_lo((TQ, LANES))
    ks = pl.multiple_of(jnp.clip(qi * TQ - SWA_WINDOW, 0, DEC_SEQ - _SWA_KEYS), SWA_WINDOW)
    k_win = sk_ref[pl.ds(ks, _SWA_KEYS), :]
    v_win = sv_ref[pl.ds(ks, _SWA_KEYS), :]
    q_pos = qi * TQ + lax.broadcasted_iota(jnp.int32, (TQ, _SWA_KEYS), 0)
    k_pos = ks + lax.broadcasted_iota(jnp.int32, (TQ, _SWA_KEYS), 1)
    in_win = jnp.abs(q_pos - k_pos) <= SWA_WINDOW
    skc = skc_ref[...]
    svc = svc_ref[...]
    for j in range(4):
        sl = slice(LANES * j, LANES * (j + 1))
        outs = []
        for q_half, hd in zip(_split_halves(sq_ref[:, sl]), (j, j + 4)):
            s_lat = jnp.where(in_win, _dot_nt(q_half, k_win), NEG)
            outs.append(_softmax_pv([_dot_nt(q_half, skc), s_lat], [svc, v_win], sink=sink_ref[hd]))
        o_ref[:, sl] = jnp.where(lo, outs[0], outs[1]).astype(BF16)
    n_rows = DEC_SEQ // GRID_W
    r0 = jnp.clip(qi * NA_TILE_ROWS - NA_WIN_ROWS // 2, 0, n_rows - NA_KEY_ROWS)
    kn = pl.multiple_of(r0 * GRID_W, GRID_W)
    for j in range(4):
        sl = slice(LANES * j, LANES * (j + 1))
        k_win = nk_ref[pl.ds(kn, NA_KEY_ROWS * GRID_W), sl]
        v_win = nv_ref[pl.ds(kn, NA_KEY_ROWS * GRID_W), sl]
        kc = nkc_ref[:, sl]
        vc = nvc_ref[:, sl]
        outs = []
        for q_half, hd in zip(_split_halves(nq_ref[:, sl]), (2 * j, 2 * j + 1)):
            s_lat = _dot_nt(q_half, k_win) + bias_ref[0, hd]
            outs.append(_softmax_pv([_dot_nt(q_half, kc), s_lat], [vc, v_win]))
        o_ref[:, 512 + LANES * j: 512 + LANES * (j + 1)] = jnp.where(lo, outs[0], outs[1]).astype(BF16)


def _attn_cd_sample(sink, sq, sk, sv, nq, nk, nv, skc, svc, nkc, nvc, bias):
    nq_t = DEC_SEQ // TQ
    q_off = NP_TOK // TQ
    b_off = NP_TOK // DEC_SEQ

    def qspec(w):
        return pl.BlockSpec((TQ, w), lambda b, i: (q_off + b * nq_t + i, 0))

    def kspec(w):
        return pl.BlockSpec((DEC_SEQ, w), lambda b, i: (b_off + b, 0))

    def cspec(w):
        return pl.BlockSpec((PAST_LEN, w), lambda b, i: (b, 0))

    n_keys = NA_KEY_ROWS * GRID_W
    bias_spec = pl.BlockSpec(
        (1, NA_HEADS, TQ, n_keys),
        lambda b, i: (jnp.where(i == 0, 0, jnp.where(i == nq_t - 1, 2, 1)), 0, 0, 0))
    return pl.pallas_call(
        _attn_cd_sample_kernel,
        grid=(DEC_BATCH, nq_t),
        in_specs=[pl.BlockSpec(memory_space=pltpu.SMEM), qspec(512), qspec(512),
                  kspec(128), kspec(128), kspec(512), kspec(512),
                  cspec(128), cspec(128), cspec(512), cspec(512), bias_spec],
        out_specs=pl.BlockSpec((TQ, 1024), lambda b, i: (b * nq_t + i, 0)),
        out_shape=jax.ShapeDtypeStruct((NS_TOK, 1024), BF16),
        compiler_params=_cparams(2),
        name="attn_cd_sample",
    )(sink, sq, nq, sk, sv, nk, nv, skc, svc, nkc, nvc, bias)


def _na_bias(rpb):
    n_rows = DEC_SEQ // GRID_W
    last = n_rows - NA_TILE_ROWS
    variants = []
    for q_row0 in (0, NA_TILE_ROWS, last):
        k_row0 = min(max(q_row0 - NA_WIN_ROWS // 2, 0), n_rows - NA_KEY_ROWS)
        ql = np.arange(TQ)
        kl = np.arange(NA_KEY_ROWS * GRID_W)
        r = (q_row0 + ql // GRID_W)[:, None]
        c = (ql % GRID_W)[:, None]
        kr = (k_row0 + kl // GRID_W)[None, :]
        kc = (kl % GRID_W)[None, :]
        rs = np.clip(r - NA_WIN_ROWS // 2, 0, n_rows - NA_WIN_ROWS)
        qs = np.clip(c - NA_WIN_COLS // 2, 0, GRID_W - NA_WIN_COLS)
        valid = (kr >= rs) & (kr < rs + NA_WIN_ROWS) & (kc >= qs) & (kc < qs + NA_WIN_COLS)
        dr = np.clip(kr - r + NA_WIN_ROWS - 1, 0, 2 * NA_WIN_ROWS - 2)
        dc = np.clip(kc - c + NA_WIN_COLS - 1, 0, 2 * NA_WIN_COLS - 2)
        dr, dc = np.broadcast_arrays(dr, dc)
        b = rpb[:, dr, dc].astype(F32)
        variants.append(jnp.where(valid[None], b, NEG))
    return jnp.stack(variants, axis=0)


def _postmix_kernel(o_ref, x_ref, mod_ref, g_ref, wout_ref, wr_ref, br_ref,
                    x1_ref, h2_ref, idx_ref, gate_ref):
    m = mod_ref[0]
    g = g_ref[...]
    y = _dot(o_ref[...], wout_ref[...])
    x1 = x_ref[...] + m[2:3] * _rms(y, g[1:2])
    x1_ref[...] = x1
    h2 = (_rms(x1, g[2:3]) * (1.0 + m[4:5]) + m[3:4]).astype(BF16)
    h2_ref[...] = h2
    logits = _dot(h2, wr_ref[...]) + br_ref[...]
    lane = lax.broadcasted_iota(jnp.int32, logits.shape, 1).astype(F32)
    cur = jnp.where(lane < N_EXPERTS, logits, -jnp.inf)
    tops, idxs = [], []
    for _ in range(TOP_K):
        mx = jnp.max(cur, axis=-1, keepdims=True)
        ix = jnp.min(jnp.where(cur == mx, lane, float(LANES)), axis=-1, keepdims=True)
        tops.append(mx)
        idxs.append(ix)
        cur = jnp.where(lane == ix, -jnp.inf, cur)
    es = [jnp.exp(t - tops[0]) for t in tops]
    inv = 1.0 / (es[0] + es[1] + es[2] + es[3])
    idx_out = jnp.zeros_like(logits)
    gate_out = jnp.zeros_like(logits)
    for k in range(TOP_K):
        idx_out = jnp.where(lane == float(k), idxs[k], idx_out)
        gate_out = jnp.where(lane == float(k), es[k] * inv, gate_out)
    idx_ref[...] = idx_out.astype(jnp.int32)
    gate_ref[...] = gate_out


def _postmix(o_cat, x, mod_l, g_l, w_out, w_r, b_r):
    return pl.pallas_call(
        _postmix_kernel,
        grid=(N_TILES,),
        in_specs=[_tok_spec(1024), _tok_spec(D_MODEL), _MOD_SPEC, _const_spec((4, D_MODEL)),
                  _const_spec((1024, D_MODEL)), _const_spec((D_MODEL, LANES)), _const_spec((1, LANES))],
        out_specs=[_tok_spec(D_MODEL), _tok_spec(D_MODEL), _tok_spec(LANES), _tok_spec(LANES)],
        out_shape=[jax.ShapeDtypeStruct((N_TOK, D_MODEL), F32), jax.ShapeDtypeStruct((N_TOK, D_MODEL), BF16),
                   jax.ShapeDtypeStruct((N_TOK, LANES), jnp.int32), jax.ShapeDtypeStruct((N_TOK, LANES), F32)],
        compiler_params=_cparams(1),
        name="postmix",
    )(o_cat, x, mod_l, g_l, w_out, w_r, b_r)


def _moe_kernel(blk_e_ref, blk_first_ref, blk_on_ref, x_ref, wgu_ref, bgu_ref, wd_ref, bd_ref,
                y_ref, wgu_b, wd_b):
    i = pl.program_id(0)

    @pl.when(blk_first_ref[i] == 1)
    def _():
        wgu_b[...] = wgu_ref[0].astype(BF16)
        wd_b[...] = wd_ref[0].astype(BF16)

    @pl.when(blk_on_ref[i] == 1)
    def _():
        gu = _dot(x_ref[...], wgu_b[...]) + bgu_ref[0]
        g = jnp.minimum(gu[:, :D_EXPERT], SWIGLU_LIMIT)
        u = jnp.clip(gu[:, D_EXPERT:], -SWIGLU_LIMIT, SWIGLU_LIMIT)
        a = g * jax.nn.sigmoid(SWIGLU_ALPHA * g) * (u + 1.0)
        y_ref[...] = (_dot(a.astype(BF16), wd_b[...]) + bd_ref[0]).astype(BF16)

    @pl.when(blk_on_ref[i] == 0)
    def _():
        y_ref[...] = jnp.zeros_like(y_ref)


def _moe(blk_e, blk_first, blk_on, xs, w_gu, b_gu, w_down, b_down):
    grid_spec = pltpu.PrefetchScalarGridSpec(
        num_scalar_prefetch=3,
        grid=(MOE_BLOCKS,),
        in_specs=[
            pl.BlockSpec((MOE_TM, D_MODEL), lambda i, e, f, o: (i, 0)),
            pl.BlockSpec((1, D_MODEL, 2 * D_EXPERT), lambda i, e, f, o: (e[i], 0, 0)),
            pl.BlockSpec((1, 1, 2 * D_EXPERT), lambda i, e, f, o: (e[i], 0, 0)),
            pl.BlockSpec((1, D_EXPERT, D_MODEL), lambda i, e, f, o: (e[i], 0, 0)),
            pl.BlockSpec((1, 1, D_MODEL), lambda i, e, f, o: (e[i], 0, 0)),
        ],
        out_specs=pl.BlockSpec((MOE_TM, D_MODEL), lambda i, e, f, o: (i, 0)),
        scratch_shapes=[pltpu.VMEM((D_MODEL, 2 * D_EXPERT), BF16), pltpu.VMEM((D_EXPERT, D_MODEL), BF16)],
    )
    return pl.pallas_call(
        _moe_kernel,
        grid_spec=grid_spec,
        out_shape=jax.ShapeDtypeStruct((MOE_ROWS, D_MODEL), BF16),
        compiler_params=_cparams(1),
        name="moe_experts",
    )(blk_e, blk_first, blk_on, xs, w_gu, b_gu.reshape(N_EXPERTS, 1, 2 * D_EXPERT),
      w_down, b_down.reshape(N_EXPERTS, 1, D_MODEL))


def _route(idx):
    flat_e = idx.reshape(-1)
    onehot = (flat_e[:, None] == jnp.arange(N_EXPERTS, dtype=jnp.int32)[None, :]).astype(jnp.int32)
    rank = jnp.take_along_axis(jnp.cumsum(onehot, axis=0), flat_e[:, None], axis=1)[:, 0] - 1
    counts = jnp.sum(onehot, axis=0)
    padded = (counts + MOE_TM - 1) // MOE_TM * MOE_TM
    pad_end = jnp.cumsum(padded)
    pad_start = pad_end - padded
    dest = pad_start[flat_e] + rank
    row_tok = jnp.zeros((MOE_ROWS,), jnp.int32).at[dest].set(
        jnp.arange(N_TOK * TOP_K, dtype=jnp.int32) // TOP_K)
    blk_row0 = jnp.arange(MOE_BLOCKS, dtype=jnp.int32) * MOE_TM
    blk_on = (blk_row0 < pad_end[-1]).astype(jnp.int32)
    blk_e = jnp.minimum(jnp.searchsorted(pad_end, blk_row0, side='right'), N_EXPERTS - 1).astype(jnp.int32)
    last_e = blk_e[jnp.maximum(jnp.sum(blk_on) - 1, 0)]
    blk_e = jnp.where(blk_on == 1, blk_e, last_e)
    prev = jnp.concatenate([jnp.full((1,), -1, jnp.int32), blk_e[:-1]])
    blk_first = (blk_e != prev).astype(jnp.int32)
    return dest.reshape(N_TOK, TOP_K), row_tok, blk_e, blk_first, blk_on


def _combine_kernel(y_ref, gate_ref, x1_ref, mod_ref, g_ref, o_ref):
    m = mod_ref[0]
    gate = gate_ref[...]
    acc = gate[:, 0:1] * y_ref[0].astype(F32)
    for k in range(1, TOP_K):
        acc = acc + gate[:, k:k + 1] * y_ref[k].astype(F32)
    o_ref[...] = x1_ref[...] + m[5:6] * _rms(acc, g_ref[...][3:4])


def _combine(yg, gates, x1, mod_l, g_l):
    return pl.pallas_call(
        _combine_kernel,
        grid=(N_TILES,),
        in_specs=[pl.BlockSpec((TOP_K, TM, D_MODEL), lambda i: (0, i, 0)), _tok_spec(LANES),
                  _tok_spec(D_MODEL), _MOD_SPEC, _const_spec((4, D_MODEL))],
        out_specs=_tok_spec(D_MODEL),
        out_shape=jax.ShapeDtypeStruct((N_TOK, D_MODEL), F32),
        compiler_params=_cparams(1),
        name="combine",
    )(yg, gates, x1, mod_l, g_l)


def _ffn(o_cat, x, mod_l, g_l, w_out, w_router, b_router, w_gu, b_gu, w_down, b_down):
    w_r = jnp.pad(w_router, ((0, 0), (0, LANES - N_EXPERTS))).astype(BF16)
    b_r = jnp.pad(b_router, (0, LANES - N_EXPERTS)).reshape(1, LANES)
    x1, h2, idx_slab, gate_slab = _postmix(o_cat, x, mod_l, g_l, w_out.astype(BF16), w_r, b_r)
    pos, row_tok, blk_e, blk_first, blk_on = _route(idx_slab[:, :TOP_K])
    xs = jnp.take(h2, row_tok, axis=0)
    ys = _moe(blk_e, blk_first, blk_on, xs, w_gu, b_gu, w_down, b_down)
    yg = jnp.take(ys, pos.T, axis=0)
    return _combine(yg, gate_slab, x1, mod_l, g_l)


def _pad_heads(w, n_heads, width, keep):
    k = w.shape[0]
    w = w.reshape(k, n_heads, width)[:, :, :keep]
    return jnp.pad(w, ((0, 0), (0, 0), (0, LANES - keep))).reshape(k, n_heads * LANES)


def _pe_slab(x):
    return jnp.pad(x, [(0, 0)] * (x.ndim - 1) + [(MLA_NOPE, LANES - MLA_NOPE - MLA_ROPE)])


_SWA_ORDER = np.array([0, 4, 1, 5, 2, 6, 3, 7])


def kernel(x_prompt, x_sample, cache_mla_ckv, cache_mla_krope, cache_diff_k, cache_diff_v, cache_swa_k, cache_swa_v, cache_na_k, cache_na_v, c, c_ctx, w_mod, b_mod, norm_g, w_in0, mla_q_norm, w_uq, mla_kv_norm, w_ukv, diff_lambda, diff_norm, w_out0, w_in1, swa_sink, na_rpb, w_out1, w_router, b_router, w_gu, b_gu, w_down, b_down):
    x = jnp.concatenate([x_prompt.reshape(NP_TOK, D_MODEL), x_sample.reshape(NS_TOK, D_MODEL)], axis=0)
    cond = jnp.concatenate([c_ctx[None, :], c, jnp.zeros((16 - 1 - DEC_BATCH, D_MODEL), F32)], axis=0)
    mod = _modulation(cond, w_mod, b_mod).reshape(DEPTH, 16, 6, D_MODEL)
    t64, t32 = _rope_tables()
    states = {}
    for l in range(DEPTH):
        i = l // 2
        g_l = norm_g[l]
        mod_l = mod[l]
        if l % 2 == 0:
            lam_init = 0.8 - 0.6 * math.exp(-0.3 * l)
            wi = w_in0[i]
            w_in_p = jnp.concatenate(
                [wi[:, 0:640], wi[:, 672:2208], _pe_slab(wi[:, 640:672])], axis=1).astype(BF16)
            w_uq_p = _pad_heads(w_uq[i], MLA_HEADS, MLA_NOPE + MLA_ROPE, MLA_NOPE + MLA_ROPE).astype(BF16)
            w_k_p = _pad_heads(w_ukv[i], MLA_HEADS, MLA_NOPE + MLA_V, MLA_NOPE).astype(BF16)
            w_v = w_ukv[i].reshape(MLA_KV_RANK, MLA_HEADS, MLA_NOPE + MLA_V)[:, :, MLA_NOPE:].reshape(
                MLA_KV_RANK, MLA_HEADS * MLA_V).astype(BF16)
            (q, k, v, dq, dk, dv, ckv_st, kpe_st, dk_st, dv_st) = _premix0(
                x, mod_l, g_l[0:1], w_in_p, mla_q_norm[i][None, :], w_uq_p, mla_kv_norm[i][None, :],
                w_k_p, w_v, t32, t64)
            states['mla_ckv'] = ckv_st.reshape(BATCH, 1, SEQ, MLA_KV_RANK)
            states['mla_krope'] = kpe_st[:, MLA_NOPE:MLA_NOPE + MLA_ROPE].reshape(BATCH, 1, SEQ, MLA_ROPE)
            states['diff_k'] = dk_st.reshape(BATCH, 1, SEQ, DIFF_HEADS, 2 * DIFF_DH)
            states['diff_v'] = dv_st.reshape(BATCH, 1, SEQ, DIFF_HEADS, 2 * DIFF_DH)
            kc, vc = _mla_cache(cache_mla_ckv[:, i].reshape(DEC_BATCH * PAST_LEN, MLA_KV_RANK),
                                _pe_slab(cache_mla_krope[:, i].reshape(DEC_BATCH * PAST_LEN, MLA_ROPE)),
                                w_k_p, w_v)
            dkc = cache_diff_k[:, i].reshape(DEC_BATCH * PAST_LEN, 512).astype(BF16)
            dvc = cache_diff_v[:, i].reshape(DEC_BATCH * PAST_LEN, 512).astype(BF16)
            lam = diff_lambda[i]
            sub_g = diff_norm[i][None, :]
            o_p = _attn_ab(q, dq, (k, v, dk, dv), None, lam, sub_g, lam_init,
                           n_batch=BATCH, t_len=SEQ, tq=SEQ, tok_off=0)
            o_s = _attn_ab(q, dq, (k, v, dk, dv), (kc, vc, dkc, dvc), lam, sub_g, lam_init,
                           n_batch=DEC_BATCH, t_len=DEC_SEQ, tq=TQ, tok_off=NP_TOK)
            w_out = w_out0[i]
        else:
            wi = w_in1[i]
            sq_cols = (_SWA_ORDER[:, None] * HEAD_DIM + np.arange(HEAD_DIM)[None, :]).reshape(-1)
            w_in_p = jnp.concatenate([wi[:, sq_cols], wi[:, 512:]], axis=1).astype(BF16)
            (sq, sk, sv, nq, nk, nv, sk_st, sv_st, nk_st, nv_st) = _premix1(x, mod_l, g_l[0:1], w_in_p, t64)
            states['swa_k'] = sk_st.reshape(BATCH, 1, SEQ, SWA_KV_HEADS, HEAD_DIM)
            states['swa_v'] = sv_st.reshape(BATCH, 1, SEQ, SWA_KV_HEADS, HEAD_DIM)
            states['na_k'] = nk_st.reshape(BATCH, 1, SEQ, NA_HEADS, HEAD_DIM)
            states['na_v'] = nv_st.reshape(BATCH, 1, SEQ, NA_HEADS, HEAD_DIM)
            skc = cache_swa_k[:, i].reshape(DEC_BATCH * PAST_LEN, 128).astype(BF16)
            svc = cache_swa_v[:, i].reshape(DEC_BATCH * PAST_LEN, 128).astype(BF16)
            nkc = cache_na_k[:, i].reshape(DEC_BATCH * PAST_LEN, 512).astype(BF16)
            nvc = cache_na_v[:, i].reshape(DEC_BATCH * PAST_LEN, 512).astype(BF16)
            sink = swa_sink[i]
            o_p = _attn_cd_prompt(sink, sq, sk, sv, nq, nk, nv)
            o_s = _attn_cd_sample(sink, sq, sk, sv, nq, nk, nv, skc, svc, nkc, nvc, _na_bias(na_rpb[i]))
            wo = w_out1[i]
            w_out = jnp.concatenate([wo[sq_cols], wo[512:]], axis=0)
        o_cat = jnp.concatenate([o_p, o_s], axis=0)
        x = _ffn(o_cat, x, mod_l, g_l, w_out, w_router[l], b_router[l], w_gu[l], b_gu[l], w_down[l], b_down[l])
    return (x[:NP_TOK].reshape(BATCH, SEQ, D_MODEL), x[NP_TOK:].reshape(DEC_BATCH, DEC_SEQ, D_MODEL),
            states['mla_ckv'], states['mla_krope'], states['diff_k'], states['diff_v'],
            states['swa_k'], states['swa_v'], states['na_k'], states['na_v'])
```

```python
import functools
import math

import numpy as np
import jax
import jax.numpy as jnp
from jax import lax
from jax.experimental import pallas as pl
from jax.experimental.pallas import tpu as pltpu
from jax.experimental.pallas import tpu_sc as plsc

F32 = jnp.float32
BF16 = jnp.bfloat16

D_MODEL = 1024
BATCH = 16
SEQ = 256
DEPTH = 2
DEC_BATCH = 8
DEC_SEQ = 2048
PAST_LEN = 256
GRID_W = 64
HEAD_DIM = 64
ROPE_THETA = 10000.0
EPS = 1e-6
NEG = -1e30

MLA_HEADS = 8
MLA_Q_RANK = 384
MLA_KV_RANK = 256
MLA_NOPE = 64
MLA_ROPE = 32
MLA_V = 64
DIFF_HEADS = 4
DIFF_DH = 64
SWA_HEADS = 8
SWA_KV_HEADS = 2
SWA_WINDOW = 128
NA_HEADS = 8
NA_WIN_ROWS = 8
NA_WIN_COLS = 16
N_EXPERTS = 32
TOP_K = 4
D_EXPERT = 1024
SWIGLU_LIMIT = 7.0
SWIGLU_ALPHA = 1.702

LANES = 128
NP_TOK = BATCH * SEQ
NS_TOK = DEC_BATCH * DEC_SEQ
N_TOK = NP_TOK + NS_TOK
TM = 512
NPT = NP_TOK // TM
TILES_PER_SAMPLE = DEC_SEQ // TM
N_TILES = N_TOK // TM
TQ = 256
MOE_TM = 256
MOE_ROWS = ((N_TOK * TOP_K + N_EXPERTS * (MOE_TM - 1)) // MOE_TM + 1) * MOE_TM
MOE_BLOCKS = MOE_ROWS // MOE_TM
NA_TILE_ROWS = TQ // GRID_W
NA_KEY_ROWS = 12
VMEM_LIMIT = 56 * 1024 * 1024


def _cparams(n_axes, vmem=VMEM_LIMIT):
    return pltpu.CompilerParams(dimension_semantics=("arbitrary",) * n_axes,
                                vmem_limit_bytes=vmem)


def _rms(x, g):
    return x * lax.rsqrt(jnp.mean(x * x, axis=-1, keepdims=True) + EPS) * g


def _dot(a, b):
    return jnp.dot(a, b, preferred_element_type=F32)


def _dot_nt(a, b):
    return lax.dot_general(a, b, (((1,), (1,)), ((), ())), preferred_element_type=F32)


def _rope(x, cos, sin_a, sin_b, half):
    return (x * cos + pltpu.roll(x, LANES - half, 1) * sin_a + pltpu.roll(x, half, 1) * sin_b)


def _mod_row(i):
    return jnp.where(i < NPT, 0, 1 + (i - NPT) // TILES_PER_SAMPLE)


def _rope_blk(i):
    return jnp.where(i < NPT, TILES_PER_SAMPLE, (i - NPT) % TILES_PER_SAMPLE)


def _mod_kernel(c_ref, w_ref, b_ref, o_ref):
    c = c_ref[...]
    s = (c * jax.nn.sigmoid(c)).astype(BF16)
    o_ref[0] = _dot(s, w_ref[0].astype(BF16)) + b_ref[0]


def _modulation(cond, w_mod, b_mod):
    nb = 1024
    return pl.pallas_call(
        _mod_kernel,
        grid=(DEPTH, 6 * D_MODEL // nb),
        in_specs=[
            pl.BlockSpec((16, D_MODEL), lambda l, n: (0, 0)),
            pl.BlockSpec((1, D_MODEL, nb), lambda l, n: (l, 0, n)),
            pl.BlockSpec((1, 1, nb), lambda l, n: (l, 0, n)),
        ],
        out_specs=pl.BlockSpec((1, 16, nb), lambda l, n: (l, 0, n)),
        out_shape=jax.ShapeDtypeStruct((DEPTH, 16, 6 * D_MODEL), F32),
        compiler_params=_cparams(2),
        name="modulation",
    )(cond, w_mod, b_mod.reshape(DEPTH, 1, 6 * D_MODEL))


def _rope_tables():
    t = jnp.arange(DEC_SEQ)
    rows = (t // GRID_W).astype(F32)
    cols = (t % GRID_W).astype(F32)

    def angles(r):
        n = r // 4
        inv = ROPE_THETA ** (-jnp.arange(n, dtype=F32) / n)
        return jnp.concatenate([rows[:, None] * inv[None], cols[:, None] * inv[None]], axis=-1)

    def finish(cos, sa, sb):
        ident = (jnp.ones((TM, LANES), F32), jnp.zeros((TM, LANES), F32), jnp.zeros((TM, LANES), F32))
        return tuple(jnp.concatenate([a, b], axis=0) for a, b in zip((cos, sa, sb), ident))

    a64 = angles(64)
    c, s, z = jnp.cos(a64), jnp.sin(a64), jnp.zeros_like(a64)
    t64 = finish(jnp.concatenate([c, c, c, c], -1), jnp.concatenate([-s, z, -s, z], -1),
                 jnp.concatenate([z, s, z, s], -1))
    a32 = angles(32)
    c, s, z = jnp.cos(a32), jnp.sin(a32), jnp.zeros_like(a32)
    one64 = jnp.ones((DEC_SEQ, 64), F32)
    z64 = jnp.zeros((DEC_SEQ, 64), F32)
    z32 = jnp.zeros((DEC_SEQ, 32), F32)
    t32 = finish(jnp.concatenate([one64, c, c, z32], -1), jnp.concatenate([z64, -s, z, z32], -1),
                 jnp.concatenate([z64, z, s, z32], -1))
    return t64, t32


_MLA_SCALE = (MLA_NOPE + MLA_ROPE) ** -0.5
_QSCALE = HEAD_DIM ** -0.5


def _premix0_kernel(xp_ref, xs_ref, mod_ref, g_ref, win_ref, qn_ref, wuq_ref, kvn_ref, wk_ref, wv_ref,
                    c32_ref, sa32_ref, sb32_ref, c64_ref, sa64_ref, sb64_ref,
                    q_ref, k_ref, v_ref, dq_ref, dk_ref, dv_ref,
                    ckv_st, kpe_st, dk_st, dv_st):
    i = pl.program_id(0)
    m = mod_ref[0]
    h = _rms(_pick_x(i, xp_ref, xs_ref), g_ref[...]) * (1.0 + m[1:2]) + m[0:1]
    proj = _dot(h.astype(BF16), win_ref[...])
    q_a = proj[:, 0:384]
    kv_a = proj[:, 384:640]
    dq = proj[:, 640:1152]
    dk = proj[:, 1152:1664]
    dv = proj[:, 1664:2176]
    pe = proj[:, 2176:2304]
    q = _dot(_rms(q_a, qn_ref[...]).astype(BF16), wuq_ref[...])
    ckv = _rms(kv_a, kvn_ref[...])
    ckv_b = ckv.astype(BF16)
    kn = _dot(ckv_b, wk_ref[...])
    v_ref[...] = _dot(ckv_b, wv_ref[...]).astype(BF16)
    c32, sa32, sb32 = c32_ref[...], sa32_ref[...], sb32_ref[...]
    c64, sa64, sb64 = c64_ref[...], sa64_ref[...], sb64_ref[...]
    pe_r = _rope(pe, c32, sa32, sb32, MLA_ROPE // 2)
    for hd in range(MLA_HEADS):
        sl = slice(LANES * hd, LANES * (hd + 1))
        q_ref[:, sl] = (_rope(q[:, sl], c32, sa32, sb32, MLA_ROPE // 2) * _MLA_SCALE).astype(BF16)
        k_ref[:, sl] = (kn[:, sl] + pe_r).astype(BF16)
    for hd in range(DIFF_HEADS):
        sl = slice(LANES * hd, LANES * (hd + 1))
        dq_ref[:, sl] = (_rope(dq[:, sl], c64, sa64, sb64, DIFF_DH // 2) * _QSCALE).astype(BF16)
        dk_ref[:, sl] = _rope(dk[:, sl], c64, sa64, sb64, DIFF_DH // 2).astype(BF16)
    dv_ref[...] = dv.astype(BF16)

    @pl.when(i < NPT)
    def _():
        ckv_st[...] = ckv
        kpe_st[...] = pe
        dk_st[...] = dk
        dv_st[...] = dv


def _premix1_kernel(x_ref, mod_ref, g_ref, win_ref, c64_ref, sa64_ref, sb64_ref,
                    sq_ref, sk_ref, sv_ref, nq_ref, nk_ref, nv_ref,
                    sk_st, sv_st, nk_st, nv_st):
    i = pl.program_id(0)
    m = mod_ref[0]
    h = _rms(x_ref[...], g_ref[...]) * (1.0 + m[1:2]) + m[0:1]
    proj = _dot(h.astype(BF16), win_ref[...])
    sq = proj[:, 0:512]
    sk = proj[:, 512:640]
    sv = proj[:, 640:768]
    nq = proj[:, 768:1280]
    nk = proj[:, 1280:1792]
    nv = proj[:, 1792:2304]
    c64, sa64, sb64 = c64_ref[...], sa64_ref[...], sb64_ref[...]
    for hd in range(4):
        sl = slice(LANES * hd, LANES * (hd + 1))
        sq_ref[:, sl] = (_rope(sq[:, sl], c64, sa64, sb64, HEAD_DIM // 2) * _QSCALE).astype(BF16)
    sk_ref[...] = _rope(sk, c64, sa64, sb64, HEAD_DIM // 2).astype(BF16)
    sv_ref[...] = sv.astype(BF16)
    nq_ref[...] = (nq * _QSCALE).astype(BF16)
    nk_ref[...] = nk.astype(BF16)
    nv_ref[...] = nv.astype(BF16)

    @pl.when(i < NPT)
    def _():
        sk_st[...] = sk
        sv_st[...] = sv
        nk_st[...] = nk
        nv_st[...] = nv


def _tok_spec(width):
    return pl.BlockSpec((TM, width), lambda i: (i, 0))


_PROMPT_SPEC = pl.BlockSpec((TM, D_MODEL), lambda i: (jnp.minimum(i, NPT - 1), 0))
_SAMPLE_SPEC = pl.BlockSpec((TM, D_MODEL), lambda i: (jnp.maximum(i - NPT, 0), 0))


def _pick_x(i, xp_ref, xs_ref):
    return jnp.where(i < NPT, xp_ref[...], xs_ref[...])


def _state_spec(width):
    return pl.BlockSpec((TM, width), lambda i: (jnp.minimum(i, NPT - 1), 0))


def _const_spec(shape):
    return pl.BlockSpec(shape, lambda i: (0,) * len(shape))


_MOD_SPEC = pl.BlockSpec((1, 6, D_MODEL), lambda i: (_mod_row(i), 0, 0))
_ROPE_SPEC = pl.BlockSpec((TM, LANES), lambda i: (_rope_blk(i), 0))


def _premix0(xp, xs, mod_l, g0, w_in_p, q_norm, w_uq_p, kv_norm, w_k_p, w_v, t32, t64):
    outs = [(N_TOK, 1024, BF16), (N_TOK, 1024, BF16), (N_TOK, 512, BF16), (N_TOK, 512, BF16),
            (N_TOK, 512, BF16), (N_TOK, 512, BF16),
            (NP_TOK, 256, F32), (NP_TOK, 128, F32), (NP_TOK, 512, F32), (NP_TOK, 512, F32)]
    return pl.pallas_call(
        _premix0_kernel,
        grid=(N_TILES,),
        in_specs=[_PROMPT_SPEC, _SAMPLE_SPEC, _MOD_SPEC, _const_spec((1, D_MODEL)),
                  _const_spec((D_MODEL, 2304)), _const_spec((1, MLA_Q_RANK)),
                  _const_spec((MLA_Q_RANK, 1024)), _const_spec((1, MLA_KV_RANK)),
                  _const_spec((MLA_KV_RANK, 1024)), _const_spec((MLA_KV_RANK, 512))]
                 + [_ROPE_SPEC] * 6,
        out_specs=[_tok_spec(w) for (_, w, _) in outs[:6]] + [_state_spec(w) for (_, w, _) in outs[6:]],
        out_shape=[jax.ShapeDtypeStruct((n, w), dt) for (n, w, dt) in outs],
        compiler_params=_cparams(1),
        name="premix_ab",
    )(xp, xs, mod_l, g0, w_in_p, q_norm, w_uq_p, kv_norm, w_k_p, w_v, *t32, *t64)


def _premix1(x, mod_l, g0, w_in_p, t64):
    outs = [(N_TOK, 512, BF16), (N_TOK, 128, BF16), (N_TOK, 128, BF16), (N_TOK, 512, BF16),
            (N_TOK, 512, BF16), (N_TOK, 512, BF16),
            (NP_TOK, 128, F32), (NP_TOK, 128, F32), (NP_TOK, 512, F32), (NP_TOK, 512, F32)]
    return pl.pallas_call(
        _premix1_kernel,
        grid=(N_TILES,),
        in_specs=[_tok_spec(D_MODEL), _MOD_SPEC, _const_spec((1, D_MODEL)),
                  _const_spec((D_MODEL, 2304))] + [_ROPE_SPEC] * 3,
        out_specs=[_tok_spec(w) for (_, w, _) in outs[:6]] + [_state_spec(w) for (_, w, _) in outs[6:]],
        out_shape=[jax.ShapeDtypeStruct((n, w), dt) for (n, w, dt) in outs],
        compiler_params=_cparams(1),
        name="premix_cd",
    )(x, mod_l, g0, w_in_p, *t64)


def _mla_cache_kernel(ckv_ref, pe_ref, wk_ref, wv_ref, k_ref, v_ref):
    c = ckv_ref[...].astype(BF16)
    kn = _dot(c, wk_ref[...])
    v_ref[...] = _dot(c, wv_ref[...]).astype(BF16)
    pe = pe_ref[...]
    for hd in range(MLA_HEADS):
        sl = slice(LANES * hd, LANES * (hd + 1))
        k_ref[:, sl] = (kn[:, sl] + pe).astype(BF16)


def _mla_cache(ckv, pe_slab, w_k_p, w_v):
    n = ckv.shape[0]
    tm = 512
    return pl.pallas_call(
        _mla_cache_kernel,
        grid=(n // tm,),
        in_specs=[pl.BlockSpec((tm, MLA_KV_RANK), lambda i: (i, 0)),
                  pl.BlockSpec((tm, LANES), lambda i: (i, 0)),
                  _const_spec((MLA_KV_RANK, 1024)), _const_spec((MLA_KV_RANK, 512))],
        out_specs=[pl.BlockSpec((tm, 1024), lambda i: (i, 0)), pl.BlockSpec((tm, 512), lambda i: (i, 0))],
        out_shape=[jax.ShapeDtypeStruct((n, 1024), BF16), jax.ShapeDtypeStruct((n, 512), BF16)],
        compiler_params=_cparams(1),
        name="mla_cache",
    )(ckv, pe_slab, w_k_p, w_v)


def _softmax_pv(scores, values, sink=None):
    m = jnp.max(scores[0], axis=-1, keepdims=True)
    for s in scores[1:]:
        m = jnp.maximum(m, jnp.max(s, axis=-1, keepdims=True))
    if sink is not None:
        m = jnp.maximum(m, sink)
    l = None
    o = None
    for s, v in zip(scores, values):
        p = jnp.exp(s - m)
        ls = jnp.sum(p, axis=-1, keepdims=True)
        os_ = _dot(p.astype(BF16), v)
        l = ls if l is None else l + ls
        o = os_ if o is None else o + os_
    if sink is not None:
        l = l + jnp.exp(sink - m)
    return o * (1.0 / l)


def _lane_lo(shape):
    return lax.broadcasted_iota(jnp.int32, shape, 1) < (LANES // 2)


def _split_halves(qb):
    lo = _lane_lo(qb.shape)
    zero = jnp.zeros_like(qb)
    return jnp.where(lo, qb, zero), jnp.where(lo, zero, qb)


def _attn_ab_kernel(*refs, n_pieces, lam_init, aliased):
    if aliased:
        refs = refs[1:]
    q_ref, dq_ref = refs[0], refs[1]
    pieces = [refs[2 + 4 * p: 6 + 4 * p] for p in range(n_pieces)]
    lam_ref, subg_ref, o_ref = refs[2 + 4 * n_pieces:]
    lam = lam_ref[...]
    lam_full = (jnp.exp(jnp.sum(lam[0:1] * lam[1:2], axis=-1, keepdims=True))
                - jnp.exp(jnp.sum(lam[2:3] * lam[3:4], axis=-1, keepdims=True)) + lam_init)
    lo = _lane_lo((q_ref.shape[0], LANES))
    for j in range(MLA_HEADS // 2):
        pair = []
        for half in range(2):
            hd = 2 * j + half
            sl = slice(LANES * hd, LANES * (hd + 1))
            qh = q_ref[:, sl]
            scores = [_dot_nt(qh, k_ref[:, sl]) for (k_ref, _, _, _) in pieces]
            vals = [v_ref[:, LANES * j: LANES * (j + 1)] for (_, v_ref, _, _) in pieces]
            pair.append(_softmax_pv(scores, vals))
        o_ref[:, LANES * j: LANES * (j + 1)] = jnp.where(lo, pair[0], pair[1]).astype(BF16)
    subg = subg_ref[...]
    for hd in range(DIFF_HEADS):
        sl = slice(LANES * hd, LANES * (hd + 1))
        q1, q2 = _split_halves(dq_ref[:, sl])
        ks = [dk_ref[:, sl] for (_, _, dk_ref, _) in pieces]
        vs = [dv_ref[:, sl] for (_, _, _, dv_ref) in pieces]
        o1 = _softmax_pv([_dot_nt(q1, k) for k in ks], vs)
        o2 = _softmax_pv([_dot_nt(q2, k) for k in ks], vs)
        od = _rms(o1 - lam_full * o2, subg) * (1.0 - lam_init)
        o_ref[:, 512 + LANES * hd: 512 + LANES * (hd + 1)] = od.astype(BF16)


def _attn_ab(q, dq, new_kv, cache_kv, lam, sub_g, lam_init, *, n_batch, t_len, tq, tok_off, out_init=None):
    nq = t_len // tq
    q_off = tok_off // tq
    b_off = tok_off // t_len
    widths = (1024, 512, 512, 512)
    in_specs = [pl.BlockSpec((tq, 1024), lambda b, i: (q_off + b * nq + i, 0)),
                pl.BlockSpec((tq, 512), lambda b, i: (q_off + b * nq + i, 0))]
    args = [q, dq]
    for w, a in zip(widths, new_kv):
        in_specs.append(pl.BlockSpec((t_len, w), lambda b, i: (b_off + b, 0)))
        args.append(a)
    n_pieces = 1
    if cache_kv is not None:
        n_pieces = 2
        for w, a in zip(widths, cache_kv):
            in_specs.append(pl.BlockSpec((PAST_LEN, w), lambda b, i: (b, 0)))
            args.append(a)
    in_specs += [pl.BlockSpec((4, DIFF_DH), lambda b, i: (0, 0)),
                 pl.BlockSpec((1, 2 * DIFF_DH), lambda b, i: (0, 0))]
    args += [lam, sub_g]
    aliases = {}
    if out_init is not None:
        in_specs = [pl.BlockSpec(memory_space=pl.ANY)] + in_specs
        args = [out_init] + args
        aliases = {0: 0}
    return pl.pallas_call(
        functools.partial(_attn_ab_kernel, n_pieces=n_pieces, lam_init=lam_init, aliased=out_init is not None),
        grid=(n_batch, nq),
        in_specs=in_specs,
        out_specs=pl.BlockSpec((tq, 1024), lambda b, i: (q_off + b * nq + i, 0)),
        out_shape=jax.ShapeDtypeStruct((N_TOK, 1024), BF16),
        input_output_aliases=aliases,
        compiler_params=_cparams(2),
        name="attn_ab_%d" % n_pieces,
    )(*args)


def _attn_cd_prompt_kernel(sink_ref, sq_ref, sk_ref, sv_ref, nq_ref, nk_ref, nv_ref, o_ref):
    lo = _lane_lo((sq_ref.shape[0], LANES))
    sk = sk_ref[...]
    sv = sv_ref[...]
    for j in range(4):
        sl = slice(LANES * j, LANES * (j + 1))
        q_lo, q_hi = _split_halves(sq_ref[:, sl])
        o_lo = _softmax_pv([_dot_nt(q_lo, sk)], [sv], sink=sink_ref[j])
        o_hi = _softmax_pv([_dot_nt(q_hi, sk)], [sv], sink=sink_ref[j + 4])
        o_ref[:, sl] = jnp.where(lo, o_lo, o_hi).astype(BF16)
    for j in range(4):
        sl = slice(LANES * j, LANES * (j + 1))
        q_lo, q_hi = _split_halves(nq_ref[:, sl])
        k = nk_ref[:, sl]
        v = nv_ref[:, sl]
        o_lo = _softmax_pv([_dot_nt(q_lo, k)], [v])
        o_hi = _softmax_pv([_dot_nt(q_hi, k)], [v])
        o_ref[:, 512 + LANES * j: 512 + LANES * (j + 1)] = jnp.where(lo, o_lo, o_hi).astype(BF16)


def _attn_cd_prompt(sink, sq, sk, sv, nq, nk, nv):
    def spec(w):
        return pl.BlockSpec((SEQ, w), lambda b: (b, 0))
    return pl.pallas_call(
        _attn_cd_prompt_kernel,
        grid=(BATCH,),
        in_specs=[pl.BlockSpec(memory_space=pltpu.SMEM), spec(512), spec(128), spec(128),
                  spec(512), spec(512), spec(512)],
        out_specs=spec(1024),
        out_shape=jax.ShapeDtypeStruct((N_TOK, 1024), BF16),
        compiler_params=_cparams(1),
        name="attn_cd_prompt",
    )(sink, sq, sk, sv, nq, nk, nv)


_SWA_KEYS = TQ + 2 * SWA_WINDOW


def _attn_cd_sample_kernel(init_ref, sink_ref, sq_ref, nq_ref, sk_ref, sv_ref, nk_ref, nv_ref,
                           skc_ref, svc_ref, nkc_ref, nvc_ref, bias_ref, o_ref):
    del init_ref
    qi = pl.program_id(1)
    lo = _lane_lo((TQ, LANES))
    ks = pl.multiple_of(jnp.clip(qi * TQ - SWA_WINDOW, 0, DEC_SEQ - _SWA_KEYS), SWA_WINDOW)
    k_win = sk_ref[pl.ds(ks, _SWA_KEYS), :]
    v_win = sv_ref[pl.ds(ks, _SWA_KEYS), :]
    q_pos = qi * TQ + lax.broadcasted_iota(jnp.int32, (TQ, _SWA_KEYS), 0)
    k_pos = ks + lax.broadcasted_iota(jnp.int32, (TQ, _SWA_KEYS), 1)
    in_win = jnp.abs(q_pos - k_pos) <= SWA_WINDOW
    skc = skc_ref[...]
    svc = svc_ref[...]
    for j in range(4):
        sl = slice(LANES * j, LANES * (j + 1))
        outs = []
        for q_half, hd in zip(_split_halves(sq_ref[:, sl]), (j, j + 4)):
            s_lat = jnp.where(in_win, _dot_nt(q_half, k_win), NEG)
            outs.append(_softmax_pv([_dot_nt(q_half, skc), s_lat], [svc, v_win], sink=sink_ref[hd]))
        o_ref[:, sl] = jnp.where(lo, outs[0], outs[1]).astype(BF16)
    n_rows = DEC_SEQ // GRID_W
    r0 = jnp.clip(qi * NA_TILE_ROWS - NA_WIN_ROWS // 2, 0, n_rows - NA_KEY_ROWS)
    kn = pl.multiple_of(r0 * GRID_W, GRID_W)
    for j in range(4):
        sl = slice(LANES * j, LANES * (j + 1))
        k_win = nk_ref[pl.ds(kn, NA_KEY_ROWS * GRID_W), sl]
        v_win = nv_ref[pl.ds(kn, NA_KEY_ROWS * GRID_W), sl]
        kc = nkc_ref[:, sl]
        vc = nvc_ref[:, sl]
        outs = []
        for q_half, hd in zip(_split_halves(nq_ref[:, sl]), (2 * j, 2 * j + 1)):
            s_lat = _dot_nt(q_half, k_win) + bias_ref[0, hd]
            outs.append(_softmax_pv([_dot_nt(q_half, kc), s_lat], [vc, v_win]))
        o_ref[:, 512 + LANES * j: 512 + LANES * (j + 1)] = jnp.where(lo, outs[0], outs[1]).astype(BF16)


def _attn_cd_sample(out_init, sink, sq, sk, sv, nq, nk, nv, skc, svc, nkc, nvc, bias):
    nq_t = DEC_SEQ // TQ
    q_off = NP_TOK // TQ
    b_off = NP_TOK // DEC_SEQ

    def qspec(w):
        return pl.BlockSpec((TQ, w), lambda b, i: (q_off + b * nq_t + i, 0))

    def kspec(w):
        return pl.BlockSpec((DEC_SEQ, w), lambda b, i: (b_off + b, 0))

    def cspec(w):
        return pl.BlockSpec((PAST_LEN, w), lambda b, i: (b, 0))

    n_keys = NA_KEY_ROWS * GRID_W
    bias_spec = pl.BlockSpec(
        (1, NA_HEADS, TQ, n_keys),
        lambda b, i: (jnp.where(i == 0, 0, jnp.where(i == nq_t - 1, 2, 1)), 0, 0, 0))
    return pl.pallas_call(
        _attn_cd_sample_kernel,
        grid=(DEC_BATCH, nq_t),
        in_specs=[pl.BlockSpec(memory_space=pl.ANY), pl.BlockSpec(memory_space=pltpu.SMEM),
                  qspec(512), qspec(512), kspec(128), kspec(128), kspec(512), kspec(512),
                  cspec(128), cspec(128), cspec(512), cspec(512), bias_spec],
        out_specs=pl.BlockSpec((TQ, 1024), lambda b, i: (q_off + b * nq_t + i, 0)),
        out_shape=jax.ShapeDtypeStruct((N_TOK, 1024), BF16),
        input_output_aliases={0: 0},
        compiler_params=_cparams(2),
        name="attn_cd_sample",
    )(out_init, sink, sq, nq, sk, sv, nk, nv, skc, svc, nkc, nvc, bias)


def _na_bias(rpb):
    n_rows = DEC_SEQ // GRID_W
    n_dr = 2 * NA_WIN_ROWS - 1
    n_dc = 2 * NA_WIN_COLS - 1
    c = np.arange(GRID_W)[:, None]
    kc = np.arange(GRID_W)[None, :]
    qs = np.clip(c - NA_WIN_COLS // 2, 0, GRID_W - NA_WIN_COLS)
    col_ok = (kc >= qs) & (kc < qs + NA_WIN_COLS)
    dc = np.clip(kc - c + NA_WIN_COLS - 1, 0, n_dc - 1)
    onehot = ((dc[None] == np.arange(n_dc)[:, None, None]) & col_ok[None]).astype(np.float32)
    blocks = jnp.einsum('hrd,dck->hrck', rpb.astype(F32), onehot, precision=lax.Precision.HIGHEST)
    blocks = jnp.where(col_ok[None, None], blocks, NEG)
    blocks = jnp.concatenate([blocks, jnp.full((NA_HEADS, 1, GRID_W, GRID_W), NEG, F32)], axis=1)
    blk_idx = np.zeros((3, NA_TILE_ROWS, NA_KEY_ROWS), np.int32)
    for v, q_row0 in enumerate((0, NA_TILE_ROWS, n_rows - NA_TILE_ROWS)):
        k_row0 = min(max(q_row0 - NA_WIN_ROWS // 2, 0), n_rows - NA_KEY_ROWS)
        for ri in range(NA_TILE_ROWS):
            r = q_row0 + ri
            rs = min(max(r - NA_WIN_ROWS // 2, 0), n_rows - NA_WIN_ROWS)
            for kj in range(NA_KEY_ROWS):
                kr = k_row0 + kj
                blk_idx[v, ri, kj] = kr - r + NA_WIN_ROWS - 1 if rs <= kr < rs + NA_WIN_ROWS else n_dr
    tiles = jnp.take(blocks, blk_idx.reshape(-1), axis=1)
    tiles = tiles.reshape(NA_HEADS, 3, NA_TILE_ROWS, NA_KEY_ROWS, GRID_W, GRID_W)
    return tiles.transpose(1, 0, 2, 4, 3, 5).reshape(3, NA_HEADS, TQ, NA_KEY_ROWS * GRID_W)


_HI_MASK = -65536


def _pack_pairs(x):
    w = x.shape[1] // 2
    r = x.astype(BF16).astype(F32)
    lo = lax.bitcast_convert_type(r[:, :w], jnp.int32)
    hi = lax.bitcast_convert_type(r[:, w:], jnp.int32)
    return (hi & _HI_MASK) | lax.shift_right_logical(lo, 16)


def _unpack_pairs(p):
    lo = lax.bitcast_convert_type(lax.shift_left(p, 16), F32)
    hi = lax.bitcast_convert_type(p & _HI_MASK, F32)
    return lo, hi


def _postmix_kernel(*refs, split_x):
    if split_x:
        o_ref, xp_ref, xs_ref = refs[:3]
        refs = refs[3:]
    else:
        o_ref, x_ref = refs[:2]
        refs = refs[2:]
    (mod_ref, g_ref, wout_ref, wr_ref, br_ref, tri_ref,
     x1_ref, h2_ref, route_ref, gate_ref, cnt_ref, run_ref) = refs
    i = pl.program_id(0)
    x = _pick_x(i, xp_ref, xs_ref) if split_x else x_ref[...]

    @pl.when(i == 0)
    def _():
        run_ref[...] = jnp.zeros_like(run_ref)

    m = mod_ref[0]
    g = g_ref[...]
    y = _dot(o_ref[...], wout_ref[...])
    x1 = x + m[2:3] * _rms(y, g[1:2])
    x1_ref[...] = x1
    h2 = _rms(x1, g[2:3]) * (1.0 + m[4:5]) + m[3:4]
    h2_ref[...] = _pack_pairs(h2)
    logits = _dot(h2.astype(BF16), wr_ref[...]) + br_ref[...]
    lane = lax.broadcasted_iota(jnp.int32, logits.shape, 1).astype(F32)
    cur = jnp.where(lane < N_EXPERTS, logits, -jnp.inf)
    tops, idxs = [], []
    for _ in range(TOP_K):
        mx = jnp.max(cur, axis=-1, keepdims=True)
        ix = jnp.min(jnp.where(cur == mx, lane, float(LANES)), axis=-1, keepdims=True)
        tops.append(mx)
        idxs.append(ix)
        cur = jnp.where(lane == ix, -jnp.inf, cur)
    es = [jnp.exp(t - tops[0]) for t in tops]
    inv = 1.0 / (es[0] + es[1] + es[2] + es[3])
    picked = jnp.zeros_like(logits)
    for k in range(TOP_K):
        picked = jnp.where(lane == idxs[k], 1.0, picked)
    before = _dot(tri_ref[...], picked.astype(BF16)) + run_ref[0:1, :]
    route = jnp.zeros_like(logits)
    gate_out = jnp.zeros_like(logits)
    for k in range(TOP_K):
        rank = jnp.sum(jnp.where(lane == idxs[k], before, 0.0), axis=-1, keepdims=True)
        route = jnp.where(lane == float(k), idxs[k], route)
        route = jnp.where(lane == float(TOP_K + k), rank, route)
        gate_out = jnp.where(lane == float(k), es[k] * inv, gate_out)
    route_ref[...] = route.astype(jnp.int32)
    gate_ref[...] = gate_out
    run_ref[...] = run_ref[...] + jnp.sum(picked, axis=0, keepdims=True)
    cnt_ref[...] = run_ref[...].astype(jnp.int32)


def _postmix(o_cat, xs, mod_l, g_l, w_out, w_r, b_r):
    tri = jnp.asarray(np.tril(np.ones((TM, TM), np.float32), -1), BF16)
    split_x = len(xs) == 2
    x_specs = [_PROMPT_SPEC, _SAMPLE_SPEC] if split_x else [_tok_spec(D_MODEL)]
    return pl.pallas_call(
        functools.partial(_postmix_kernel, split_x=split_x),
        grid=(N_TILES,),
        in_specs=[_tok_spec(1024)] + x_specs + [_MOD_SPEC, _const_spec((4, D_MODEL)),
                  _const_spec((1024, D_MODEL)), _const_spec((D_MODEL, LANES)), _const_spec((1, LANES)),
                  _const_spec((TM, TM))],
        out_specs=[_tok_spec(D_MODEL), _tok_spec(D_MODEL // 2), _tok_spec(LANES), _tok_spec(LANES),
                   _const_spec((8, LANES))],
        out_shape=[jax.ShapeDtypeStruct((N_TOK, D_MODEL), F32),
                   jax.ShapeDtypeStruct((N_TOK, D_MODEL // 2), jnp.int32),
                   jax.ShapeDtypeStruct((N_TOK, LANES), jnp.int32), jax.ShapeDtypeStruct((N_TOK, LANES), F32),
                   jax.ShapeDtypeStruct((8, LANES), jnp.int32)],
        scratch_shapes=[pltpu.VMEM((8, LANES), F32)],
        compiler_params=_cparams(1),
        name="postmix",
    )(o_cat, *xs, mod_l, g_l, w_out, w_r, b_r, tri)


SC_WORKERS = 32
SC_ROWS = 128
ROW_WORDS = D_MODEL // 2


def _sc_worker_id():
    return lax.axis_index("s") * 2 + lax.axis_index("c")


def _sc_dispatch(src, idx):
    n_chunks = N_TOK // (SC_WORKERS * SC_ROWS)
    mesh = plsc.VectorSubcoreMesh(core_axis_name="c", subcore_axis_name="s")

    @functools.partial(
        pl.kernel, mesh=mesh,
        out_type=jax.ShapeDtypeStruct((MOE_ROWS, ROW_WORDS), jnp.int32),
        scratch_types=[pltpu.VMEM((n_chunks * TOP_K, SC_ROWS), jnp.int32),
                       pltpu.VMEM((SC_ROWS, ROW_WORDS), jnp.int32), pltpu.SemaphoreType.DMA])
    def k(src_hbm, idx_hbm, out_hbm, idx_v, rows_v, sem):
        wid = _sc_worker_id()
        pltpu.sync_copy(idx_hbm.at[wid], idx_v)

        @pl.loop(0, n_chunks)
        def _(g):
            pltpu.sync_copy(src_hbm.at[pl.ds((wid * n_chunks + g) * SC_ROWS, SC_ROWS)], rows_v)
            copies = [pltpu.async_copy(rows_v, out_hbm.at[idx_v.at[g * TOP_K + kk]], sem)
                      for kk in range(TOP_K)]
            for cp in copies:
                cp.wait()

    return k(src, idx)


def _sc_collect(table, idx):
    n_chunks = idx.shape[1]
    mesh = plsc.VectorSubcoreMesh(core_axis_name="c", subcore_axis_name="s")

    @functools.partial(
        pl.kernel, mesh=mesh,
        out_type=jax.ShapeDtypeStruct((SC_WORKERS * n_chunks * SC_ROWS, ROW_WORDS), jnp.int32),
        scratch_types=[pltpu.VMEM((n_chunks, SC_ROWS), jnp.int32),
                       pltpu.VMEM((SC_ROWS, ROW_WORDS), jnp.int32), pltpu.SemaphoreType.DMA])
    def k(table_hbm, idx_hbm, out_hbm, idx_v, rows_v, sem):
        wid = _sc_worker_id()
        pltpu.sync_copy(idx_hbm.at[wid], idx_v)

        @pl.loop(0, n_chunks)
        def _(g):
            pltpu.async_copy(table_hbm.at[idx_v.at[g]], rows_v, sem).wait()
            pltpu.sync_copy(rows_v, out_hbm.at[pl.ds((wid * n_chunks + g) * SC_ROWS, SC_ROWS)])

    return k(table, idx)


def _moe_kernel(blk_e_ref, blk_first_ref, blk_rows_ref, x_ref, wgu_ref, bgu_ref, wd_ref, bd_ref,
                y_ref, wgu_b, wd_b):
    i = pl.program_id(0)
    n_valid = blk_rows_ref[i]

    @pl.when(blk_first_ref[i] == 1)
    def _():
        wgu_b[...] = wgu_ref[0].astype(BF16)
        wd_b[...] = wd_ref[0].astype(BF16)

    @pl.when(n_valid > 0)
    def _():
        live = lax.broadcasted_iota(jnp.int32, x_ref.shape, 0) < n_valid
        lo, hi = _unpack_pairs(jnp.where(live, x_ref[...], 0))
        x = jnp.concatenate([lo, hi], axis=1).astype(BF16)
        gu = _dot(x, wgu_b[...]) + bgu_ref[0]
        g = jnp.minimum(gu[:, :D_EXPERT], SWIGLU_LIMIT)
        u = jnp.clip(gu[:, D_EXPERT:], -SWIGLU_LIMIT, SWIGLU_LIMIT)
        a = g * jax.nn.sigmoid(SWIGLU_ALPHA * g) * (u + 1.0)
        y_ref[...] = _pack_pairs(_dot(a.astype(BF16), wd_b[...]) + bd_ref[0])

    @pl.when(n_valid == 0)
    def _():
        y_ref[...] = jnp.zeros_like(y_ref)


def _moe(blk_e, blk_first, blk_rows, xs, w_gu, b_gu, w_down, b_down):
    grid_spec = pltpu.PrefetchScalarGridSpec(
        num_scalar_prefetch=3,
        grid=(MOE_BLOCKS,),
        in_specs=[
            pl.BlockSpec((MOE_TM, ROW_WORDS), lambda i, e, f, o: (i, 0)),
            pl.BlockSpec((1, D_MODEL, 2 * D_EXPERT), lambda i, e, f, o: (e[i], 0, 0)),
            pl.BlockSpec((1, 1, 2 * D_EXPERT), lambda i, e, f, o: (e[i], 0, 0)),
            pl.BlockSpec((1, D_EXPERT, D_MODEL), lambda i, e, f, o: (e[i], 0, 0)),
            pl.BlockSpec((1, 1, D_MODEL), lambda i, e, f, o: (e[i], 0, 0)),
        ],
        out_specs=pl.BlockSpec((MOE_TM, ROW_WORDS), lambda i, e, f, o: (i, 0)),
        scratch_shapes=[pltpu.VMEM((D_MODEL, 2 * D_EXPERT), BF16), pltpu.VMEM((D_EXPERT, D_MODEL), BF16)],
    )
    return pl.pallas_call(
        _moe_kernel,
        grid_spec=grid_spec,
        out_shape=jax.ShapeDtypeStruct((MOE_ROWS, ROW_WORDS), jnp.int32),
        compiler_params=_cparams(1),
        name="moe_experts",
    )(blk_e, blk_first, blk_rows, xs, w_gu, b_gu.reshape(N_EXPERTS, 1, 2 * D_EXPERT),
      w_down, b_down.reshape(N_EXPERTS, 1, D_MODEL))


def _route(route, counts):
    experts = jnp.arange(N_EXPERTS, dtype=jnp.int32)
    padded = (counts + MOE_TM - 1) // MOE_TM * MOE_TM
    pad_end = jnp.cumsum(padded)
    pad_start = pad_end - padded
    e = route[:, 0:TOP_K]
    onehot = e[:, :, None] == experts[None, None, :]
    dest = jnp.sum(jnp.where(onehot, pad_start[None, None, :], 0), axis=-1) + route[:, TOP_K:2 * TOP_K]
    blk_row0 = jnp.arange(MOE_BLOCKS, dtype=jnp.int32) * MOE_TM
    blk_e = jnp.minimum(jnp.sum((pad_end[None, :] <= blk_row0[:, None]).astype(jnp.int32), axis=1),
                        N_EXPERTS - 1)
    on = blk_row0 < pad_end[-1]
    n_on = jnp.sum(on.astype(jnp.int32))
    blk_onehot = blk_e[:, None] == experts[None, :]
    row_end = jnp.sum(jnp.where(blk_onehot, (pad_start + counts)[None, :], 0), axis=1)
    blk_rows = jnp.where(on, jnp.clip(row_end - blk_row0, 0, MOE_TM), 0).astype(jnp.int32)
    last_e = jnp.sum(jnp.where(jnp.arange(MOE_BLOCKS) == n_on - 1, blk_e, 0))
    blk_e = jnp.where(on, blk_e, last_e).astype(jnp.int32)
    prev = jnp.concatenate([jnp.full((1,), -1, jnp.int32), blk_e[:-1]])
    blk_first = (blk_e != prev).astype(jnp.int32)
    return dest, blk_e, blk_first, blk_rows


def _combine_kernel(y_ref, gate_ref, x1_ref, mod_ref, g_ref, *o_refs):
    i = pl.program_id(0)
    m = mod_ref[0]
    gate = gate_ref[...]
    acc_lo = None
    for k in range(TOP_K):
        lo, hi = _unpack_pairs(y_ref[k])
        gk = gate[:, k:k + 1]
        acc_lo = gk * lo if acc_lo is None else acc_lo + gk * lo
        acc_hi = gk * hi if k == 0 else acc_hi + gk * hi
    acc = jnp.concatenate([acc_lo, acc_hi], axis=1)
    out = x1_ref[...] + m[5:6] * _rms(acc, g_ref[...][3:4])
    if len(o_refs) == 1:
        o_refs[0][...] = out
    else:
        @pl.when(i < NPT)
        def _():
            o_refs[0][...] = out

        @pl.when(i >= NPT)
        def _():
            o_refs[1][...] = out


def _combine(yg, gates, x1, mod_l, g_l, split_out):
    if split_out:
        out_specs = [_PROMPT_SPEC, _SAMPLE_SPEC]
        out_shape = [jax.ShapeDtypeStruct((NP_TOK, D_MODEL), F32), jax.ShapeDtypeStruct((NS_TOK, D_MODEL), F32)]
    else:
        out_specs = [_tok_spec(D_MODEL)]
        out_shape = [jax.ShapeDtypeStruct((N_TOK, D_MODEL), F32)]
    return pl.pallas_call(
        _combine_kernel,
        grid=(N_TILES,),
        in_specs=[pl.BlockSpec((TOP_K, TM, ROW_WORDS), lambda i: (0, i, 0)), _tok_spec(LANES),
                  _tok_spec(D_MODEL), _MOD_SPEC, _const_spec((4, D_MODEL))],
        out_specs=out_specs,
        out_shape=out_shape,
        compiler_params=_cparams(1),
        name="combine",
    )(yg, gates, x1, mod_l, g_l)


def _ffn(o_cat, xs, mod_l, g_l, w_out, w_router, b_router, w_gu, b_gu, w_down, b_down, split_out):
    w_r = jnp.pad(w_router, ((0, 0), (0, LANES - N_EXPERTS))).astype(BF16)
    b_r = jnp.pad(b_router, (0, LANES - N_EXPERTS)).reshape(1, LANES)
    x1, h2p, route, gate_slab, counts = _postmix(o_cat, xs, mod_l, g_l, w_out.astype(BF16), w_r, b_r)
    dest, blk_e, blk_first, blk_rows = _route(route, counts[0, :N_EXPERTS])
    n_chunks = N_TOK // (SC_WORKERS * SC_ROWS)
    idx_d = dest.reshape(SC_WORKERS, n_chunks, SC_ROWS, TOP_K).transpose(0, 1, 3, 2).reshape(
        SC_WORKERS, n_chunks * TOP_K, SC_ROWS)
    rows = _sc_dispatch(h2p, idx_d)
    ys = _moe(blk_e, blk_first, blk_rows, rows, w_gu, b_gu, w_down, b_down)
    idx_c = dest.T.reshape(SC_WORKERS, TOP_K * n_chunks, SC_ROWS)
    yg = _sc_collect(ys, idx_c).reshape(TOP_K, N_TOK, ROW_WORDS)
    return _combine(yg, gate_slab, x1, mod_l, g_l, split_out)


def _pad_heads(w, n_heads, width, keep):
    k = w.shape[0]
    w = w.reshape(k, n_heads, width)[:, :, :keep]
    return jnp.pad(w, ((0, 0), (0, 0), (0, LANES - keep))).reshape(k, n_heads * LANES)


def _pe_slab(x):
    return jnp.pad(x, [(0, 0)] * (x.ndim - 1) + [(MLA_NOPE, LANES - MLA_NOPE - MLA_ROPE)])


_SWA_ORDER = np.array([0, 4, 1, 5, 2, 6, 3, 7])


def kernel(x_prompt, x_sample, cache_mla_ckv, cache_mla_krope, cache_diff_k, cache_diff_v, cache_swa_k, cache_swa_v, cache_na_k, cache_na_v, c, c_ctx, w_mod, b_mod, norm_g, w_in0, mla_q_norm, w_uq, mla_kv_norm, w_ukv, diff_lambda, diff_norm, w_out0, w_in1, swa_sink, na_rpb, w_out1, w_router, b_router, w_gu, b_gu, w_down, b_down):
    xs = (x_prompt.reshape(NP_TOK, D_MODEL), x_sample.reshape(NS_TOK, D_MODEL))
    cond = jnp.concatenate([c_ctx[None, :], c, jnp.zeros((16 - 1 - DEC_BATCH, D_MODEL), F32)], axis=0)
    mod = _modulation(cond, w_mod, b_mod).reshape(DEPTH, 16, 6, D_MODEL)
    t64, t32 = _rope_tables()
    states = {}
    for l in range(DEPTH):
        i = l // 2
        g_l = norm_g[l]
        mod_l = mod[l]
        if l % 2 == 0:
            lam_init = 0.8 - 0.6 * math.exp(-0.3 * l)
            wi = w_in0[i]
            w_in_p = jnp.concatenate(
                [wi[:, 0:640], wi[:, 672:2208], _pe_slab(wi[:, 640:672])], axis=1).astype(BF16)
            w_uq_p = _pad_heads(w_uq[i], MLA_HEADS, MLA_NOPE + MLA_ROPE, MLA_NOPE + MLA_ROPE).astype(BF16)
            w_k_p = _pad_heads(w_ukv[i], MLA_HEADS, MLA_NOPE + MLA_V, MLA_NOPE).astype(BF16)
            w_v = w_ukv[i].reshape(MLA_KV_RANK, MLA_HEADS, MLA_NOPE + MLA_V)[:, :, MLA_NOPE:].reshape(
                MLA_KV_RANK, MLA_HEADS * MLA_V).astype(BF16)
            (q, k, v, dq, dk, dv, ckv_st, kpe_st, dk_st, dv_st) = _premix0(
                *xs, mod_l, g_l[0:1], w_in_p, mla_q_norm[i][None, :], w_uq_p, mla_kv_norm[i][None, :],
                w_k_p, w_v, t32, t64)
            states['mla_ckv'] = ckv_st.reshape(BATCH, 1, SEQ, MLA_KV_RANK)
            states['mla_krope'] = kpe_st[:, MLA_NOPE:MLA_NOPE + MLA_ROPE].reshape(BATCH, 1, SEQ, MLA_ROPE)
            states['diff_k'] = dk_st.reshape(BATCH, 1, SEQ, DIFF_HEADS, 2 * DIFF_DH)
            states['diff_v'] = dv_st.reshape(BATCH, 1, SEQ, DIFF_HEADS, 2 * DIFF_DH)
            kc, vc = _mla_cache(cache_mla_ckv[:, i].reshape(DEC_BATCH * PAST_LEN, MLA_KV_RANK),
                                _pe_slab(cache_mla_krope[:, i].reshape(DEC_BATCH * PAST_LEN, MLA_ROPE)),
                                w_k_p, w_v)
            dkc = cache_diff_k[:, i].reshape(DEC_BATCH * PAST_LEN, 512).astype(BF16)
            dvc = cache_diff_v[:, i].reshape(DEC_BATCH * PAST_LEN, 512).astype(BF16)
            lam = diff_lambda[i]
            sub_g = diff_norm[i][None, :]
            o_p = _attn_ab(q, dq, (k, v, dk, dv), None, lam, sub_g, lam_init,
                           n_batch=BATCH, t_len=SEQ, tq=SEQ, tok_off=0)
            o_cat = _attn_ab(q, dq, (k, v, dk, dv), (kc, vc, dkc, dvc), lam, sub_g, lam_init,
                             n_batch=DEC_BATCH, t_len=DEC_SEQ, tq=TQ, tok_off=NP_TOK, out_init=o_p)
            w_out = w_out0[i]
        else:
            wi = w_in1[i]
            sq_cols = (_SWA_ORDER[:, None] * HEAD_DIM + np.arange(HEAD_DIM)[None, :]).reshape(-1)
            w_in_p = jnp.concatenate([wi[:, sq_cols], wi[:, 512:]], axis=1).astype(BF16)
            (sq, sk, sv, nq, nk, nv, sk_st, sv_st, nk_st, nv_st) = _premix1(xs[0], mod_l, g_l[0:1], w_in_p, t64)
            states['swa_k'] = sk_st.reshape(BATCH, 1, SEQ, SWA_KV_HEADS, HEAD_DIM)
            states['swa_v'] = sv_st.reshape(BATCH, 1, SEQ, SWA_KV_HEADS, HEAD_DIM)
            states['na_k'] = nk_st.reshape(BATCH, 1, SEQ, NA_HEADS, HEAD_DIM)
            states['na_v'] = nv_st.reshape(BATCH, 1, SEQ, NA_HEADS, HEAD_DIM)
            skc = cache_swa_k[:, i].reshape(DEC_BATCH * PAST_LEN, 128).astype(BF16)
            svc = cache_swa_v[:, i].reshape(DEC_BATCH * PAST_LEN, 128).astype(BF16)
            nkc = cache_na_k[:, i].reshape(DEC_BATCH * PAST_LEN, 512).astype(BF16)
            nvc = cache_na_v[:, i].reshape(DEC_BATCH * PAST_LEN, 512).astype(BF16)
            sink = swa_sink[i]
            o_p = _attn_cd_prompt(sink, sq, sk, sv, nq, nk, nv)
            o_cat = _attn_cd_sample(o_p, sink, sq, sk, sv, nq, nk, nv, skc, svc, nkc, nvc, _na_bias(na_rpb[i]))
            wo = w_out1[i]
            w_out = jnp.concatenate([wo[sq_cols], wo[512:]], axis=0)
        xs = _ffn(o_cat, xs, mod_l, g_l, w_out, w_router[l], b_router[l], w_gu[l], b_gu[l], w_down[l],
                  b_down[l], split_out=(l == DEPTH - 1))
    return (xs[0].reshape(BATCH, SEQ, D_MODEL), xs[1].reshape(DEC_BATCH, DEC_SEQ, D_MODEL),
            states['mla_ckv'], states['mla_krope'], states['diff_k'], states['diff_v'],
            states['swa_k'], states['swa_v'], states['na_k'], states['na_v'])
```

```python
import functools
import math

import numpy as np
import jax
import jax.numpy as jnp
from jax import lax
from jax.experimental import pallas as pl
from jax.experimental.pallas import tpu as pltpu
from jax.experimental.pallas import tpu_sc as plsc

F32 = jnp.float32
BF16 = jnp.bfloat16

D_MODEL = 1024
BATCH = 16
SEQ = 256
DEPTH = 2
DEC_BATCH = 8
DEC_SEQ = 2048
PAST_LEN = 256
GRID_W = 64
HEAD_DIM = 64
ROPE_THETA = 10000.0
EPS = 1e-6
NEG = -1e30

MLA_HEADS = 8
MLA_Q_RANK = 384
MLA_KV_RANK = 256
MLA_NOPE = 64
MLA_ROPE = 32
MLA_V = 64
DIFF_HEADS = 4
DIFF_DH = 64
SWA_HEADS = 8
SWA_KV_HEADS = 2
SWA_WINDOW = 128
NA_HEADS = 8
NA_WIN_ROWS = 8
NA_WIN_COLS = 16
N_EXPERTS = 32
TOP_K = 4
D_EXPERT = 1024
SWIGLU_LIMIT = 7.0
SWIGLU_ALPHA = 1.702

LANES = 128
NP_TOK = BATCH * SEQ
NS_TOK = DEC_BATCH * DEC_SEQ
N_TOK = NP_TOK + NS_TOK
TM = 512
NPT = NP_TOK // TM
TILES_PER_SAMPLE = DEC_SEQ // TM
N_TILES = N_TOK // TM
TQ = 256
TQ_AB = 256
MOE_TM = 512
MOE_ROWS = ((N_TOK * TOP_K + N_EXPERTS * (MOE_TM - 1)) // MOE_TM + 1) * MOE_TM
MOE_BLOCKS = MOE_ROWS // MOE_TM
NA_TILE_ROWS = TQ // GRID_W
NA_KEY_ROWS = 12
VMEM_LIMIT = 56 * 1024 * 1024


def _cparams(n_axes, vmem=VMEM_LIMIT):
    return pltpu.CompilerParams(dimension_semantics=("arbitrary",) * n_axes,
                                vmem_limit_bytes=vmem)


def _rms(x, g):
    return x * lax.rsqrt(jnp.mean(x * x, axis=-1, keepdims=True) + EPS) * g


def _dot(a, b):
    return jnp.dot(a, b, preferred_element_type=F32)


def _dot_nt(a, b):
    return lax.dot_general(a, b, (((1,), (1,)), ((), ())), preferred_element_type=F32)


def _rope(x, cos, sin_a, sin_b, half):
    return (x * cos + pltpu.roll(x, LANES - half, 1) * sin_a + pltpu.roll(x, half, 1) * sin_b)


def _mod_row(i):
    return jnp.where(i < NPT, 0, 1 + (i - NPT) // TILES_PER_SAMPLE)


def _rope_blk(i):
    return jnp.where(i < NPT, TILES_PER_SAMPLE, (i - NPT) % TILES_PER_SAMPLE)


def _mod_kernel(c_ref, w_ref, b_ref, o_ref):
    c = c_ref[...]
    s = (c * jax.nn.sigmoid(c)).astype(BF16)
    o_ref[0] = _dot(s, w_ref[0].astype(BF16)) + b_ref[0]


def _modulation(cond, w_mod, b_mod):
    nb = 1024
    return pl.pallas_call(
        _mod_kernel,
        grid=(DEPTH, 6 * D_MODEL // nb),
        in_specs=[
            pl.BlockSpec((16, D_MODEL), lambda l, n: (0, 0)),
            pl.BlockSpec((1, D_MODEL, nb), lambda l, n: (l, 0, n)),
            pl.BlockSpec((1, 1, nb), lambda l, n: (l, 0, n)),
        ],
        out_specs=pl.BlockSpec((1, 16, nb), lambda l, n: (l, 0, n)),
        out_shape=jax.ShapeDtypeStruct((DEPTH, 16, 6 * D_MODEL), F32),
        compiler_params=_cparams(2),
        name="modulation",
    )(cond, w_mod, b_mod.reshape(DEPTH, 1, 6 * D_MODEL))


def _rope_tables():
    t = jnp.arange(DEC_SEQ)
    rows = (t // GRID_W).astype(F32)
    cols = (t % GRID_W).astype(F32)

    def angles(r):
        n = r // 4
        inv = ROPE_THETA ** (-jnp.arange(n, dtype=F32) / n)
        return jnp.concatenate([rows[:, None] * inv[None], cols[:, None] * inv[None]], axis=-1)

    def finish(cos, sa, sb):
        ident = (jnp.ones((TM, LANES), F32), jnp.zeros((TM, LANES), F32), jnp.zeros((TM, LANES), F32))
        return tuple(jnp.concatenate([a, b], axis=0) for a, b in zip((cos, sa, sb), ident))

    a64 = angles(64)
    c, s, z = jnp.cos(a64), jnp.sin(a64), jnp.zeros_like(a64)
    t64 = finish(jnp.concatenate([c, c, c, c], -1), jnp.concatenate([-s, z, -s, z], -1),
                 jnp.concatenate([z, s, z, s], -1))
    a32 = angles(32)
    c, s, z = jnp.cos(a32), jnp.sin(a32), jnp.zeros_like(a32)
    one64 = jnp.ones((DEC_SEQ, 64), F32)
    z64 = jnp.zeros((DEC_SEQ, 64), F32)
    z32 = jnp.zeros((DEC_SEQ, 32), F32)
    t32 = finish(jnp.concatenate([one64, c, c, z32], -1), jnp.concatenate([z64, -s, z, z32], -1),
                 jnp.concatenate([z64, z, s, z32], -1))
    return t64, t32


_MLA_SCALE = (MLA_NOPE + MLA_ROPE) ** -0.5
_QSCALE = HEAD_DIM ** -0.5


def _premix0_kernel(xp_ref, xs_ref, mod_ref, g_ref, win_ref, qn_ref, wuq_ref, kvn_ref, wk_ref, wv_ref,
                    c32_ref, sa32_ref, sb32_ref, c64_ref, sa64_ref, sb64_ref,
                    q_ref, k_ref, v_ref, dq_ref, dk_ref, dv_ref,
                    ckv_st, kpe_st, dk_st, dv_st):
    i = pl.program_id(0)
    m = mod_ref[0]
    h = _rms(_pick_x(i, xp_ref, xs_ref), g_ref[...]) * (1.0 + m[1:2]) + m[0:1]
    proj = _dot(h.astype(BF16), win_ref[...])
    q_a = proj[:, 0:384]
    kv_a = proj[:, 384:640]
    dq = proj[:, 640:1152]
    dk = proj[:, 1152:1664]
    dv = proj[:, 1664:2176]
    pe = proj[:, 2176:2304]
    q = _dot(_rms(q_a, qn_ref[...]).astype(BF16), wuq_ref[...])
    ckv = _rms(kv_a, kvn_ref[...])
    ckv_b = ckv.astype(BF16)
    kn = _dot(ckv_b, wk_ref[...])
    v_ref[...] = _dot(ckv_b, wv_ref[...]).astype(BF16)
    c32, sa32, sb32 = c32_ref[...], sa32_ref[...], sb32_ref[...]
    c64, sa64, sb64 = c64_ref[...], sa64_ref[...], sb64_ref[...]
    pe_r = _rope(pe, c32, sa32, sb32, MLA_ROPE // 2)
    for hd in range(MLA_HEADS):
        sl = slice(LANES * hd, LANES * (hd + 1))
        q_ref[:, sl] = (_rope(q[:, sl], c32, sa32, sb32, MLA_ROPE // 2) * _MLA_SCALE).astype(BF16)
        k_ref[:, sl] = (kn[:, sl] + pe_r).astype(BF16)
    for hd in range(DIFF_HEADS):
        sl = slice(LANES * hd, LANES * (hd + 1))
        dq_ref[:, sl] = (_rope(dq[:, sl], c64, sa64, sb64, DIFF_DH // 2) * _QSCALE).astype(BF16)
        dk_ref[:, sl] = _rope(dk[:, sl], c64, sa64, sb64, DIFF_DH // 2).astype(BF16)
    dv_ref[...] = dv.astype(BF16)

    @pl.when(i < NPT)
    def _():
        ckv_st[...] = ckv
        kpe_st[...] = pe
        dk_st[...] = dk
        dv_st[...] = dv


def _premix1_kernel(x_ref, mod_ref, g_ref, win_ref, c64_ref, sa64_ref, sb64_ref,
                    sq_ref, sk_ref, sv_ref, nq_ref, nk_ref, nv_ref,
                    sk_st, sv_st, nk_st, nv_st):
    i = pl.program_id(0)
    m = mod_ref[0]
    h = _rms(x_ref[...], g_ref[...]) * (1.0 + m[1:2]) + m[0:1]
    proj = _dot(h.astype(BF16), win_ref[...])
    sq = proj[:, 0:512]
    sk = proj[:, 512:640]
    sv = proj[:, 640:768]
    nq = proj[:, 768:1280]
    nk = proj[:, 1280:1792]
    nv = proj[:, 1792:2304]
    c64, sa64, sb64 = c64_ref[...], sa64_ref[...], sb64_ref[...]
    for hd in range(4):
        sl = slice(LANES * hd, LANES * (hd + 1))
        sq_ref[:, sl] = (_rope(sq[:, sl], c64, sa64, sb64, HEAD_DIM // 2) * _QSCALE).astype(BF16)
    sk_ref[...] = _rope(sk, c64, sa64, sb64, HEAD_DIM // 2).astype(BF16)
    sv_ref[...] = sv.astype(BF16)
    nq_ref[...] = (nq * _QSCALE).astype(BF16)
    nk_ref[...] = nk.astype(BF16)
    nv_ref[...] = nv.astype(BF16)

    @pl.when(i < NPT)
    def _():
        sk_st[...] = sk
        sv_st[...] = sv
        nk_st[...] = nk
        nv_st[...] = nv


def _tok_spec(width):
    return pl.BlockSpec((TM, width), lambda i: (i, 0))


_PROMPT_SPEC = pl.BlockSpec((TM, D_MODEL), lambda i: (jnp.minimum(i, NPT - 1), 0))
_SAMPLE_SPEC = pl.BlockSpec((TM, D_MODEL), lambda i: (jnp.maximum(i - NPT, 0), 0))


def _pick_x(i, xp_ref, xs_ref):
    return jnp.where(i < NPT, xp_ref[...], xs_ref[...])


def _state_spec(width):
    return pl.BlockSpec((TM, width), lambda i: (jnp.minimum(i, NPT - 1), 0))


def _const_spec(shape):
    return pl.BlockSpec(shape, lambda i: (0,) * len(shape))


_MOD_SPEC = pl.BlockSpec((1, 6, D_MODEL), lambda i: (_mod_row(i), 0, 0))
_ROPE_SPEC = pl.BlockSpec((TM, LANES), lambda i: (_rope_blk(i), 0))


def _premix0(xp, xs, mod_l, g0, w_in_p, q_norm, w_uq_p, kv_norm, w_k_p, w_v, t32, t64):
    outs = [(N_TOK, 1024, BF16), (N_TOK, 1024, BF16), (N_TOK, 512, BF16), (N_TOK, 512, BF16),
            (N_TOK, 512, BF16), (N_TOK, 512, BF16),
            (NP_TOK, 256, F32), (NP_TOK, 128, F32), (NP_TOK, 512, F32), (NP_TOK, 512, F32)]
    return pl.pallas_call(
        _premix0_kernel,
        grid=(N_TILES,),
        in_specs=[_PROMPT_SPEC, _SAMPLE_SPEC, _MOD_SPEC, _const_spec((1, D_MODEL)),
                  _const_spec((D_MODEL, 2304)), _const_spec((1, MLA_Q_RANK)),
                  _const_spec((MLA_Q_RANK, 1024)), _const_spec((1, MLA_KV_RANK)),
                  _const_spec((MLA_KV_RANK, 1024)), _const_spec((MLA_KV_RANK, 512))]
                 + [_ROPE_SPEC] * 6,
        out_specs=[_tok_spec(w) for (_, w, _) in outs[:6]] + [_state_spec(w) for (_, w, _) in outs[6:]],
        out_shape=[jax.ShapeDtypeStruct((n, w), dt) for (n, w, dt) in outs],
        compiler_params=_cparams(1),
        name="premix_ab",
    )(xp, xs, mod_l, g0, w_in_p, q_norm, w_uq_p, kv_norm, w_k_p, w_v, *t32, *t64)


def _premix1(x, mod_l, g0, w_in_p, t64):
    outs = [(N_TOK, 512, BF16), (N_TOK, 128, BF16), (N_TOK, 128, BF16), (N_TOK, 512, BF16),
            (N_TOK, 512, BF16), (N_TOK, 512, BF16),
            (NP_TOK, 128, F32), (NP_TOK, 128, F32), (NP_TOK, 512, F32), (NP_TOK, 512, F32)]
    return pl.pallas_call(
        _premix1_kernel,
        grid=(N_TILES,),
        in_specs=[_tok_spec(D_MODEL), _MOD_SPEC, _const_spec((1, D_MODEL)),
                  _const_spec((D_MODEL, 2304))] + [_ROPE_SPEC] * 3,
        out_specs=[_tok_spec(w) for (_, w, _) in outs[:6]] + [_state_spec(w) for (_, w, _) in outs[6:]],
        out_shape=[jax.ShapeDtypeStruct((n, w), dt) for (n, w, dt) in outs],
        compiler_params=_cparams(1),
        name="premix_cd",
    )(x, mod_l, g0, w_in_p, *t64)


def _mla_cache_kernel(ckv_ref, pe_ref, wk_ref, wv_ref, k_ref, v_ref):
    c = ckv_ref[...].astype(BF16)
    kn = _dot(c, wk_ref[...])
    v_ref[...] = _dot(c, wv_ref[...]).astype(BF16)
    pe = pe_ref[...]
    for hd in range(MLA_HEADS):
        sl = slice(LANES * hd, LANES * (hd + 1))
        k_ref[:, sl] = (kn[:, sl] + pe).astype(BF16)


def _mla_cache(ckv, pe_slab, w_k_p, w_v):
    n = ckv.shape[0]
    tm = 512
    return pl.pallas_call(
        _mla_cache_kernel,
        grid=(n // tm,),
        in_specs=[pl.BlockSpec((tm, MLA_KV_RANK), lambda i: (i, 0)),
                  pl.BlockSpec((tm, LANES), lambda i: (i, 0)),
                  _const_spec((MLA_KV_RANK, 1024)), _const_spec((MLA_KV_RANK, 512))],
        out_specs=[pl.BlockSpec((tm, 1024), lambda i: (i, 0)), pl.BlockSpec((tm, 512), lambda i: (i, 0))],
        out_shape=[jax.ShapeDtypeStruct((n, 1024), BF16), jax.ShapeDtypeStruct((n, 512), BF16)],
        compiler_params=_cparams(1),
        name="mla_cache",
    )(ckv, pe_slab, w_k_p, w_v)


def _softmax_pv(scores, values, sink=None):
    m = jnp.max(scores[0], axis=-1, keepdims=True)
    for s in scores[1:]:
        m = jnp.maximum(m, jnp.max(s, axis=-1, keepdims=True))
    if sink is not None:
        m = jnp.maximum(m, sink)
    l = None
    o = None
    for s, v in zip(scores, values):
        p = jnp.exp(s - m)
        ls = jnp.sum(p, axis=-1, keepdims=True)
        os_ = _dot(p.astype(BF16), v)
        l = ls if l is None else l + ls
        o = os_ if o is None else o + os_
    if sink is not None:
        l = l + jnp.exp(sink - m)
    return o * (1.0 / l)


def _lane_lo(shape):
    return lax.broadcasted_iota(jnp.int32, shape, 1) < (LANES // 2)


def _split_halves(qb):
    lo = _lane_lo(qb.shape)
    zero = jnp.zeros_like(qb)
    return jnp.where(lo, qb, zero), jnp.where(lo, zero, qb)


def _attn_ab_kernel(*refs, n_pieces, lam_init, aliased):
    if aliased:
        refs = refs[1:]
    q_ref, dq_ref = refs[0], refs[1]
    pieces = [refs[2 + 4 * p: 6 + 4 * p] for p in range(n_pieces)]
    lam_ref, subg_ref, o_ref = refs[2 + 4 * n_pieces:]
    lam = lam_ref[...]
    lam_full = (jnp.exp(jnp.sum(lam[0:1] * lam[1:2], axis=-1, keepdims=True))
                - jnp.exp(jnp.sum(lam[2:3] * lam[3:4], axis=-1, keepdims=True)) + lam_init)
    lo = _lane_lo((q_ref.shape[0], LANES))
    for j in range(MLA_HEADS // 2):
        pair = []
        for half in range(2):
            hd = 2 * j + half
            sl = slice(LANES * hd, LANES * (hd + 1))
            qh = q_ref[:, sl]
            scores = [_dot_nt(qh, k_ref[:, sl]) for (k_ref, _, _, _) in pieces]
            vals = [v_ref[:, LANES * j: LANES * (j + 1)] for (_, v_ref, _, _) in pieces]
            pair.append(_softmax_pv(scores, vals))
        o_ref[:, LANES * j: LANES * (j + 1)] = jnp.where(lo, pair[0], pair[1]).astype(BF16)
    subg = subg_ref[...]
    for hd in range(DIFF_HEADS):
        sl = slice(LANES * hd, LANES * (hd + 1))
        q1, q2 = _split_halves(dq_ref[:, sl])
        ks = [dk_ref[:, sl] for (_, _, dk_ref, _) in pieces]
        vs = [dv_ref[:, sl] for (_, _, _, dv_ref) in pieces]
        o1 = _softmax_pv([_dot_nt(q1, k) for k in ks], vs)
        o2 = _softmax_pv([_dot_nt(q2, k) for k in ks], vs)
        od = _rms(o1 - lam_full * o2, subg) * (1.0 - lam_init)
        o_ref[:, 512 + LANES * hd: 512 + LANES * (hd + 1)] = od.astype(BF16)


def _attn_ab(q, dq, new_kv, cache_kv, lam, sub_g, lam_init, *, n_batch, t_len, tq, tok_off, out_init=None):
    nq = t_len // tq
    q_off = tok_off // tq
    b_off = tok_off // t_len
    widths = (1024, 512, 512, 512)
    in_specs = [pl.BlockSpec((tq, 1024), lambda b, i: (q_off + b * nq + i, 0)),
                pl.BlockSpec((tq, 512), lambda b, i: (q_off + b * nq + i, 0))]
    args = [q, dq]
    for w, a in zip(widths, new_kv):
        in_specs.append(pl.BlockSpec((t_len, w), lambda b, i: (b_off + b, 0)))
        args.append(a)
    n_pieces = 1
    if cache_kv is not None:
        n_pieces = 2
        for w, a in zip(widths, cache_kv):
            in_specs.append(pl.BlockSpec((PAST_LEN, w), lambda b, i: (b, 0)))
            args.append(a)
    in_specs += [pl.BlockSpec((4, DIFF_DH), lambda b, i: (0, 0)),
                 pl.BlockSpec((1, 2 * DIFF_DH), lambda b, i: (0, 0))]
    args += [lam, sub_g]
    aliases = {}
    if out_init is not None:
        in_specs = [pl.BlockSpec(memory_space=pl.ANY)] + in_specs
        args = [out_init] + args
        aliases = {0: 0}
    return pl.pallas_call(
        functools.partial(_attn_ab_kernel, n_pieces=n_pieces, lam_init=lam_init, aliased=out_init is not None),
        grid=(n_batch, nq),
        in_specs=in_specs,
        out_specs=pl.BlockSpec((tq, 1024), lambda b, i: (q_off + b * nq + i, 0)),
        out_shape=jax.ShapeDtypeStruct((N_TOK, 1024), BF16),
        input_output_aliases=aliases,
        compiler_params=_cparams(2),
        name="attn_ab_%d" % n_pieces,
    )(*args)


def _attn_cd_prompt_kernel(sink_ref, sq_ref, sk_ref, sv_ref, nq_ref, nk_ref, nv_ref, o_ref):
    lo = _lane_lo((sq_ref.shape[0], LANES))
    sk = sk_ref[...]
    sv = sv_ref[...]
    for j in range(4):
        sl = slice(LANES * j, LANES * (j + 1))
        q_lo, q_hi = _split_halves(sq_ref[:, sl])
        o_lo = _softmax_pv([_dot_nt(q_lo, sk)], [sv], sink=sink_ref[j])
        o_hi = _softmax_pv([_dot_nt(q_hi, sk)], [sv], sink=sink_ref[j + 4])
        o_ref[:, sl] = jnp.where(lo, o_lo, o_hi).astype(BF16)
    for j in range(4):
        sl = slice(LANES * j, LANES * (j + 1))
        q_lo, q_hi = _split_halves(nq_ref[:, sl])
        k = nk_ref[:, sl]
        v = nv_ref[:, sl]
        o_lo = _softmax_pv([_dot_nt(q_lo, k)], [v])
        o_hi = _softmax_pv([_dot_nt(q_hi, k)], [v])
        o_ref[:, 512 + LANES * j: 512 + LANES * (j + 1)] = jnp.where(lo, o_lo, o_hi).astype(BF16)


def _attn_cd_prompt(sink, sq, sk, sv, nq, nk, nv):
    def spec(w):
        return pl.BlockSpec((SEQ, w), lambda b: (b, 0))
    return pl.pallas_call(
        _attn_cd_prompt_kernel,
        grid=(BATCH,),
        in_specs=[pl.BlockSpec(memory_space=pltpu.SMEM), spec(512), spec(128), spec(128),
                  spec(512), spec(512), spec(512)],
        out_specs=spec(1024),
        out_shape=jax.ShapeDtypeStruct((N_TOK, 1024), BF16),
        compiler_params=_cparams(1),
        name="attn_cd_prompt",
    )(sink, sq, sk, sv, nq, nk, nv)


_SWA_KEYS = TQ + 2 * SWA_WINDOW


def _attn_cd_sample_kernel(init_ref, sink_ref, sq_ref, nq_ref, sk_ref, sv_ref, nk_ref, nv_ref,
                           skc_ref, svc_ref, nkc_ref, nvc_ref, bias_ref, o_ref):
    del init_ref
    qi = pl.program_id(1)
    lo = _lane_lo((TQ, LANES))
    ks = pl.multiple_of(jnp.clip(qi * TQ - SWA_WINDOW, 0, DEC_SEQ - _SWA_KEYS), SWA_WINDOW)
    k_win = sk_ref[pl.ds(ks, _SWA_KEYS), :]
    v_win = sv_ref[pl.ds(ks, _SWA_KEYS), :]
    q_pos = qi * TQ + lax.broadcasted_iota(jnp.int32, (TQ, _SWA_KEYS), 0)
    k_pos = ks + lax.broadcasted_iota(jnp.int32, (TQ, _SWA_KEYS), 1)
    in_win = jnp.abs(q_pos - k_pos) <= SWA_WINDOW
    skc = skc_ref[...]
    svc = svc_ref[...]
    for j in range(4):
        sl = slice(LANES * j, LANES * (j + 1))
        outs = []
        for q_half, hd in zip(_split_halves(sq_ref[:, sl]), (j, j + 4)):
            s_lat = jnp.where(in_win, _dot_nt(q_half, k_win), NEG)
            outs.append(_softmax_pv([_dot_nt(q_half, skc), s_lat], [svc, v_win], sink=sink_ref[hd]))
        o_ref[:, sl] = jnp.where(lo, outs[0], outs[1]).astype(BF16)
    n_rows = DEC_SEQ // GRID_W
    r0 = jnp.clip(qi * NA_TILE_ROWS - NA_WIN_ROWS // 2, 0, n_rows - NA_KEY_ROWS)
    kn = pl.multiple_of(r0 * GRID_W, GRID_W)
    for j in range(4):
        sl = slice(LANES * j, LANES * (j + 1))
        k_win = nk_ref[pl.ds(kn, NA_KEY_ROWS * GRID_W), sl]
        v_win = nv_ref[pl.ds(kn, NA_KEY_ROWS * GRID_W), sl]
        kc = nkc_ref[:, sl]
        vc = nvc_ref[:, sl]
        outs = []
        for q_half, hd in zip(_split_halves(nq_ref[:, sl]), (2 * j, 2 * j + 1)):
            s_lat = _dot_nt(q_half, k_win) + bias_ref[0, hd]
            outs.append(_softmax_pv([_dot_nt(q_half, kc), s_lat], [vc, v_win]))
        o_ref[:, 512 + LANES * j: 512 + LANES * (j + 1)] = jnp.where(lo, outs[0], outs[1]).astype(BF16)


def _attn_cd_sample(out_init, sink, sq, sk, sv, nq, nk, nv, skc, svc, nkc, nvc, bias):
    nq_t = DEC_SEQ // TQ
    q_off = NP_TOK // TQ
    b_off = NP_TOK // DEC_SEQ

    def qspec(w):
        return pl.BlockSpec((TQ, w), lambda b, i: (q_off + b * nq_t + i, 0))

    def kspec(w):
        return pl.BlockSpec((DEC_SEQ, w), lambda b, i: (b_off + b, 0))

    def cspec(w):
        return pl.BlockSpec((PAST_LEN, w), lambda b, i: (b, 0))

    n_keys = NA_KEY_ROWS * GRID_W
    bias_spec = pl.BlockSpec(
        (1, NA_HEADS, TQ, n_keys),
        lambda b, i: (jnp.where(i == 0, 0, jnp.where(i == nq_t - 1, 2, 1)), 0, 0, 0))
    return pl.pallas_call(
        _attn_cd_sample_kernel,
        grid=(DEC_BATCH, nq_t),
        in_specs=[pl.BlockSpec(memory_space=pl.ANY), pl.BlockSpec(memory_space=pltpu.SMEM),
                  qspec(512), qspec(512), kspec(128), kspec(128), kspec(512), kspec(512),
                  cspec(128), cspec(128), cspec(512), cspec(512), bias_spec],
        out_specs=pl.BlockSpec((TQ, 1024), lambda b, i: (q_off + b * nq_t + i, 0)),
        out_shape=jax.ShapeDtypeStruct((N_TOK, 1024), BF16),
        input_output_aliases={0: 0},
        compiler_params=_cparams(2),
        name="attn_cd_sample",
    )(out_init, sink, sq, nq, sk, sv, nk, nv, skc, svc, nkc, nvc, bias)


def _na_bias(rpb):
    n_rows = DEC_SEQ // GRID_W
    n_dr = 2 * NA_WIN_ROWS - 1
    n_dc = 2 * NA_WIN_COLS - 1
    c = np.arange(GRID_W)[:, None]
    kc = np.arange(GRID_W)[None, :]
    qs = np.clip(c - NA_WIN_COLS // 2, 0, GRID_W - NA_WIN_COLS)
    col_ok = (kc >= qs) & (kc < qs + NA_WIN_COLS)
    dc = np.clip(kc - c + NA_WIN_COLS - 1, 0, n_dc - 1)
    onehot = ((dc[None] == np.arange(n_dc)[:, None, None]) & col_ok[None]).astype(np.float32)
    blocks = jnp.einsum('hrd,dck->hrck', rpb.astype(F32), onehot, precision=lax.Precision.HIGHEST)
    blocks = jnp.where(col_ok[None, None], blocks, NEG)
    blocks = jnp.concatenate([blocks, jnp.full((NA_HEADS, 1, GRID_W, GRID_W), NEG, F32)], axis=1)
    blk_idx = np.zeros((3, NA_TILE_ROWS, NA_KEY_ROWS), np.int32)
    for v, q_row0 in enumerate((0, NA_TILE_ROWS, n_rows - NA_TILE_ROWS)):
        k_row0 = min(max(q_row0 - NA_WIN_ROWS // 2, 0), n_rows - NA_KEY_ROWS)
        for ri in range(NA_TILE_ROWS):
            r = q_row0 + ri
            rs = min(max(r - NA_WIN_ROWS // 2, 0), n_rows - NA_WIN_ROWS)
            for kj in range(NA_KEY_ROWS):
                kr = k_row0 + kj
                blk_idx[v, ri, kj] = kr - r + NA_WIN_ROWS - 1 if rs <= kr < rs + NA_WIN_ROWS else n_dr
    tiles = jnp.take(blocks, blk_idx.reshape(-1), axis=1)
    tiles = tiles.reshape(NA_HEADS, 3, NA_TILE_ROWS, NA_KEY_ROWS, GRID_W, GRID_W)
    return tiles.transpose(1, 0, 2, 4, 3, 5).reshape(3, NA_HEADS, TQ, NA_KEY_ROWS * GRID_W)


_HI_MASK = -65536


def _pack_pairs(x):
    w = x.shape[1] // 2
    r = x.astype(BF16).astype(F32)
    lo = lax.bitcast_convert_type(r[:, :w], jnp.int32)
    hi = lax.bitcast_convert_type(r[:, w:], jnp.int32)
    return (hi & _HI_MASK) | lax.shift_right_logical(lo, 16)


def _unpack_pairs(p):
    lo = lax.bitcast_convert_type(lax.shift_left(p, 16), F32)
    hi = lax.bitcast_convert_type(p & _HI_MASK, F32)
    return lo, hi


def _postmix_kernel(*refs, split_x):
    if split_x:
        o_ref, xp_ref, xs_ref = refs[:3]
        refs = refs[3:]
    else:
        o_ref, x_ref = refs[:2]
        refs = refs[2:]
    (mod_ref, g_ref, wout_ref, wr_ref, br_ref, tri_ref,
     x1_ref, h2_ref, route_ref, gate_ref, cnt_ref, run_ref) = refs
    i = pl.program_id(0)
    x = _pick_x(i, xp_ref, xs_ref) if split_x else x_ref[...]

    @pl.when(i == 0)
    def _():
        run_ref[...] = jnp.zeros_like(run_ref)

    m = mod_ref[0]
    g = g_ref[...]
    y = _dot(o_ref[...], wout_ref[...])
    x1 = x + m[2:3] * _rms(y, g[1:2])
    x1_ref[...] = x1
    h2 = _rms(x1, g[2:3]) * (1.0 + m[4:5]) + m[3:4]
    h2_ref[...] = _pack_pairs(h2)
    logits = _dot(h2.astype(BF16), wr_ref[...]) + br_ref[...]
    lane = lax.broadcasted_iota(jnp.int32, logits.shape, 1).astype(F32)
    cur = jnp.where(lane < N_EXPERTS, logits, -jnp.inf)
    tops, idxs = [], []
    for _ in range(TOP_K):
        mx = jnp.max(cur, axis=-1, keepdims=True)
        ix = jnp.min(jnp.where(cur == mx, lane, float(LANES)), axis=-1, keepdims=True)
        tops.append(mx)
        idxs.append(ix)
        cur = jnp.where(lane == ix, -jnp.inf, cur)
    es = [jnp.exp(t - tops[0]) for t in tops]
    inv = 1.0 / (es[0] + es[1] + es[2] + es[3])
    picked = jnp.zeros_like(logits)
    for k in range(TOP_K):
        picked = jnp.where(lane == idxs[k], 1.0, picked)
    before = _dot(tri_ref[...], picked.astype(BF16)) + run_ref[0:1, :]
    route = jnp.zeros_like(logits)
    gate_out = jnp.zeros_like(logits)
    for k in range(TOP_K):
        rank = jnp.sum(jnp.where(lane == idxs[k], before, 0.0), axis=-1, keepdims=True)
        route = jnp.where(lane == float(k), idxs[k], route)
        route = jnp.where(lane == float(TOP_K + k), rank, route)
        gate_out = jnp.where(lane == float(k), es[k] * inv, gate_out)
    route_ref[...] = route.astype(jnp.int32)
    gate_ref[...] = gate_out
    run_ref[...] = run_ref[...] + jnp.sum(picked, axis=0, keepdims=True)
    cnt_ref[...] = run_ref[...].astype(jnp.int32)


def _postmix(o_cat, xs, mod_l, g_l, w_out, w_r, b_r):
    tri = jnp.asarray(np.tril(np.ones((TM, TM), np.float32), -1), BF16)
    split_x = len(xs) == 2
    x_specs = [_PROMPT_SPEC, _SAMPLE_SPEC] if split_x else [_tok_spec(D_MODEL)]
    return pl.pallas_call(
        functools.partial(_postmix_kernel, split_x=split_x),
        grid=(N_TILES,),
        in_specs=[_tok_spec(1024)] + x_specs + [_MOD_SPEC, _const_spec((4, D_MODEL)),
                  _const_spec((1024, D_MODEL)), _const_spec((D_MODEL, LANES)), _const_spec((1, LANES)),
                  _const_spec((TM, TM))],
        out_specs=[_tok_spec(D_MODEL), _tok_spec(D_MODEL // 2), _tok_spec(LANES), _tok_spec(LANES),
                   _const_spec((8, LANES))],
        out_shape=[jax.ShapeDtypeStruct((N_TOK, D_MODEL), F32),
                   jax.ShapeDtypeStruct((N_TOK, D_MODEL // 2), jnp.int32),
                   jax.ShapeDtypeStruct((N_TOK, LANES), jnp.int32), jax.ShapeDtypeStruct((N_TOK, LANES), F32),
                   jax.ShapeDtypeStruct((8, LANES), jnp.int32)],
        scratch_shapes=[pltpu.VMEM((8, LANES), F32)],
        compiler_params=_cparams(1),
        name="postmix",
    )(o_cat, *xs, mod_l, g_l, w_out, w_r, b_r, tri)


SC_WORKERS = 32
SC_ROWS = 128
ROW_WORDS = D_MODEL // 2


def _sc_worker_id():
    return lax.axis_index("s") * 2 + lax.axis_index("c")


def _sc_dispatch(src, idx):
    n_chunks = N_TOK // (SC_WORKERS * SC_ROWS)
    mesh = plsc.VectorSubcoreMesh(core_axis_name="c", subcore_axis_name="s")

    @functools.partial(
        pl.kernel, mesh=mesh,
        out_type=jax.ShapeDtypeStruct((MOE_ROWS, ROW_WORDS), jnp.int32),
        scratch_types=[pltpu.VMEM((n_chunks * TOP_K, SC_ROWS), jnp.int32),
                       pltpu.VMEM((SC_ROWS, ROW_WORDS), jnp.int32), pltpu.SemaphoreType.DMA])
    def k(src_hbm, idx_hbm, out_hbm, idx_v, rows_v, sem):
        wid = _sc_worker_id()
        pltpu.sync_copy(idx_hbm.at[wid], idx_v)

        @pl.loop(0, n_chunks)
        def _(g):
            pltpu.sync_copy(src_hbm.at[pl.ds((wid * n_chunks + g) * SC_ROWS, SC_ROWS)], rows_v)
            copies = [pltpu.async_copy(rows_v, out_hbm.at[idx_v.at[g * TOP_K + kk]], sem)
                      for kk in range(TOP_K)]
            for cp in copies:
                cp.wait()

    return k(src, idx)


def _sc_collect(table, idx):
    n_chunks = idx.shape[1]
    mesh = plsc.VectorSubcoreMesh(core_axis_name="c", subcore_axis_name="s")

    @functools.partial(
        pl.kernel, mesh=mesh,
        out_type=jax.ShapeDtypeStruct((SC_WORKERS * n_chunks * SC_ROWS, ROW_WORDS), jnp.int32),
        scratch_types=[pltpu.VMEM((n_chunks, SC_ROWS), jnp.int32),
                       pltpu.VMEM((SC_ROWS, ROW_WORDS), jnp.int32), pltpu.SemaphoreType.DMA])
    def k(table_hbm, idx_hbm, out_hbm, idx_v, rows_v, sem):
        wid = _sc_worker_id()
        pltpu.sync_copy(idx_hbm.at[wid], idx_v)

        @pl.loop(0, n_chunks)
        def _(g):
            pltpu.async_copy(table_hbm.at[idx_v.at[g]], rows_v, sem).wait()
            pltpu.sync_copy(rows_v, out_hbm.at[pl.ds((wid * n_chunks + g) * SC_ROWS, SC_ROWS)])

    return k(table, idx)


def _expert_rows(words, n_valid, wgu_b, wd_b, bgu, bd):
    live = lax.broadcasted_iota(jnp.int32, words.shape, 0) < n_valid
    lo, hi = _unpack_pairs(jnp.where(live, words, 0))
    x = jnp.concatenate([lo, hi], axis=1).astype(BF16)
    gu = _dot(x, wgu_b[...]) + bgu
    g = jnp.minimum(gu[:, :D_EXPERT], SWIGLU_LIMIT)
    u = jnp.clip(gu[:, D_EXPERT:], -SWIGLU_LIMIT, SWIGLU_LIMIT)
    a = g * jax.nn.sigmoid(SWIGLU_ALPHA * g) * (u + 1.0)
    return _pack_pairs(_dot(a.astype(BF16), wd_b[...]) + bd)


def _moe_kernel(blk_e_ref, blk_first_ref, blk_rows_ref, x_ref, wgu_ref, bgu_ref, wd_ref, bd_ref,
                y_ref, wgu_b, wd_b):
    i = pl.program_id(0)
    n_valid = blk_rows_ref[i]
    half = MOE_TM // 2

    @pl.when(blk_first_ref[i] == 1)
    def _():
        wgu_b[...] = wgu_ref[0, 0].astype(BF16)
        wd_b[...] = wd_ref[0, 0].astype(BF16)

    @pl.when(n_valid > half)
    def _():
        y_ref[...] = _expert_rows(x_ref[...], n_valid, wgu_b, wd_b, bgu_ref[0, 0], bd_ref[0, 0])

    @pl.when((n_valid > 0) & (n_valid <= half))
    def _():
        y_ref[:half] = _expert_rows(x_ref[:half], n_valid, wgu_b, wd_b, bgu_ref[0, 0], bd_ref[0, 0])
        y_ref[half:] = jnp.zeros((half, ROW_WORDS), jnp.int32)

    @pl.when(n_valid == 0)
    def _():
        y_ref[...] = jnp.zeros_like(y_ref)


def _moe(layer, blk_e, blk_first, blk_rows, xs, w_gu, b_gu, w_down, b_down):
    grid_spec = pltpu.PrefetchScalarGridSpec(
        num_scalar_prefetch=3,
        grid=(MOE_BLOCKS,),
        in_specs=[
            pl.BlockSpec((MOE_TM, ROW_WORDS), lambda i, e, f, o: (i, 0)),
            pl.BlockSpec((1, 1, D_MODEL, 2 * D_EXPERT), lambda i, e, f, o: (layer, e[i], 0, 0)),
            pl.BlockSpec((1, 1, 1, 2 * D_EXPERT), lambda i, e, f, o: (layer, e[i], 0, 0)),
            pl.BlockSpec((1, 1, D_EXPERT, D_MODEL), lambda i, e, f, o: (layer, e[i], 0, 0)),
            pl.BlockSpec((1, 1, 1, D_MODEL), lambda i, e, f, o: (layer, e[i], 0, 0)),
        ],
        out_specs=pl.BlockSpec((MOE_TM, ROW_WORDS), lambda i, e, f, o: (i, 0)),
        scratch_shapes=[pltpu.VMEM((D_MODEL, 2 * D_EXPERT), BF16), pltpu.VMEM((D_EXPERT, D_MODEL), BF16)],
    )
    return pl.pallas_call(
        _moe_kernel,
        grid_spec=grid_spec,
        out_shape=jax.ShapeDtypeStruct((MOE_ROWS, ROW_WORDS), jnp.int32),
        compiler_params=_cparams(1),
        name="moe_experts",
    )(blk_e, blk_first, blk_rows, xs, w_gu, b_gu.reshape(DEPTH, N_EXPERTS, 1, 2 * D_EXPERT),
      w_down, b_down.reshape(DEPTH, N_EXPERTS, 1, D_MODEL))


def _route(route, counts):
    experts = jnp.arange(N_EXPERTS, dtype=jnp.int32)
    padded = (counts + MOE_TM - 1) // MOE_TM * MOE_TM
    pad_end = jnp.cumsum(padded)
    pad_start = pad_end - padded
    e = route[:, 0:TOP_K]
    onehot = e[:, :, None] == experts[None, None, :]
    dest = jnp.sum(jnp.where(onehot, pad_start[None, None, :], 0), axis=-1) + route[:, TOP_K:2 * TOP_K]
    blk_row0 = jnp.arange(MOE_BLOCKS, dtype=jnp.int32) * MOE_TM
    blk_e = jnp.minimum(jnp.sum((pad_end[None, :] <= blk_row0[:, None]).astype(jnp.int32), axis=1),
                        N_EXPERTS - 1)
    on = blk_row0 < pad_end[-1]
    n_on = jnp.sum(on.astype(jnp.int32))
    blk_onehot = blk_e[:, None] == experts[None, :]
    row_end = jnp.sum(jnp.where(blk_onehot, (pad_start + counts)[None, :], 0), axis=1)
    blk_rows = jnp.where(on, jnp.clip(row_end - blk_row0, 0, MOE_TM), 0).astype(jnp.int32)
    last_e = jnp.sum(jnp.where(jnp.arange(MOE_BLOCKS) == n_on - 1, blk_e, 0))
    blk_e = jnp.where(on, blk_e, last_e).astype(jnp.int32)
    prev = jnp.concatenate([jnp.full((1,), -1, jnp.int32), blk_e[:-1]])
    blk_first = (blk_e != prev).astype(jnp.int32)
    return dest, blk_e, blk_first, blk_rows


def _combine_kernel(y_ref, gate_ref, x1_ref, mod_ref, g_ref, *o_refs):
    i = pl.program_id(0)
    m = mod_ref[0]
    gate = gate_ref[...]
    acc_lo = None
    for k in range(TOP_K):
        lo, hi = _unpack_pairs(y_ref[k])
        gk = gate[:, k:k + 1]
        acc_lo = gk * lo if acc_lo is None else acc_lo + gk * lo
        acc_hi = gk * hi if k == 0 else acc_hi + gk * hi
    acc = jnp.concatenate([acc_lo, acc_hi], axis=1)
    out = x1_ref[...] + m[5:6] * _rms(acc, g_ref[...][3:4])
    if len(o_refs) == 1:
        o_refs[0][...] = out
    else:
        @pl.when(i < NPT)
        def _():
            o_refs[0][...] = out

        @pl.when(i >= NPT)
        def _():
            o_refs[1][...] = out


def _combine(yg, gates, x1, mod_l, g_l, split_out):
    if split_out:
        out_specs = [_PROMPT_SPEC, _SAMPLE_SPEC]
        out_shape = [jax.ShapeDtypeStruct((NP_TOK, D_MODEL), F32), jax.ShapeDtypeStruct((NS_TOK, D_MODEL), F32)]
    else:
        out_specs = [_tok_spec(D_MODEL)]
        out_shape = [jax.ShapeDtypeStruct((N_TOK, D_MODEL), F32)]
    return pl.pallas_call(
        _combine_kernel,
        grid=(N_TILES,),
        in_specs=[pl.BlockSpec((TOP_K, TM, ROW_WORDS), lambda i: (0, i, 0)), _tok_spec(LANES),
                  _tok_spec(D_MODEL), _MOD_SPEC, _const_spec((4, D_MODEL))],
        out_specs=out_specs,
        out_shape=out_shape,
        compiler_params=_cparams(1),
        name="combine",
    )(yg, gates, x1, mod_l, g_l)


def _ffn(layer, o_cat, xs, mod_l, g_l, w_out, w_router, b_router, w_gu, b_gu, w_down, b_down, split_out):
    w_r = jnp.pad(w_router, ((0, 0), (0, LANES - N_EXPERTS))).astype(BF16)
    b_r = jnp.pad(b_router, (0, LANES - N_EXPERTS)).reshape(1, LANES)
    x1, h2p, route, gate_slab, counts = _postmix(o_cat, xs, mod_l, g_l, w_out.astype(BF16), w_r, b_r)
    dest, blk_e, blk_first, blk_rows = _route(route, counts[0, :N_EXPERTS])
    n_chunks = N_TOK // (SC_WORKERS * SC_ROWS)
    idx_d = dest.reshape(SC_WORKERS, n_chunks, SC_ROWS, TOP_K).transpose(0, 1, 3, 2).reshape(
        SC_WORKERS, n_chunks * TOP_K, SC_ROWS)
    rows = _sc_dispatch(h2p, idx_d)
    ys = _moe(layer, blk_e, blk_first, blk_rows, rows, w_gu, b_gu, w_down, b_down)
    idx_c = dest.T.reshape(SC_WORKERS, TOP_K * n_chunks, SC_ROWS)
    yg = _sc_collect(ys, idx_c).reshape(TOP_K, N_TOK, ROW_WORDS)
    return _combine(yg, gate_slab, x1, mod_l, g_l, split_out)


def _pad_heads(w, n_heads, width, keep):
    k = w.shape[0]
    w = w.reshape(k, n_heads, width)[:, :, :keep]
    return jnp.pad(w, ((0, 0), (0, 0), (0, LANES - keep))).reshape(k, n_heads * LANES)


def _pe_slab(x):
    return jnp.pad(x, [(0, 0)] * (x.ndim - 1) + [(MLA_NOPE, LANES - MLA_NOPE - MLA_ROPE)])


_SWA_ORDER = np.array([0, 4, 1, 5, 2, 6, 3, 7])


def kernel(x_prompt, x_sample, cache_mla_ckv, cache_mla_krope, cache_diff_k, cache_diff_v, cache_swa_k, cache_swa_v, cache_na_k, cache_na_v, c, c_ctx, w_mod, b_mod, norm_g, w_in0, mla_q_norm, w_uq, mla_kv_norm, w_ukv, diff_lambda, diff_norm, w_out0, w_in1, swa_sink, na_rpb, w_out1, w_router, b_router, w_gu, b_gu, w_down, b_down):
    xs = (x_prompt.reshape(NP_TOK, D_MODEL), x_sample.reshape(NS_TOK, D_MODEL))
    cond = jnp.concatenate([c_ctx[None, :], c, jnp.zeros((16 - 1 - DEC_BATCH, D_MODEL), F32)], axis=0)
    mod = _modulation(cond, w_mod, b_mod).reshape(DEPTH, 16, 6, D_MODEL)
    t64, t32 = _rope_tables()
    states = {}
    for l in range(DEPTH):
        i = l // 2
        g_l = norm_g[l]
        mod_l = mod[l]
        if l % 2 == 0:
            lam_init = 0.8 - 0.6 * math.exp(-0.3 * l)
            wi = w_in0[i]
            w_in_p = jnp.concatenate(
                [wi[:, 0:640], wi[:, 672:2208], _pe_slab(wi[:, 640:672])], axis=1).astype(BF16)
            w_uq_p = _pad_heads(w_uq[i], MLA_HEADS, MLA_NOPE + MLA_ROPE, MLA_NOPE + MLA_ROPE).astype(BF16)
            w_k_p = _pad_heads(w_ukv[i], MLA_HEADS, MLA_NOPE + MLA_V, MLA_NOPE).astype(BF16)
            w_v = w_ukv[i].reshape(MLA_KV_RANK, MLA_HEADS, MLA_NOPE + MLA_V)[:, :, MLA_NOPE:].reshape(
                MLA_KV_RANK, MLA_HEADS * MLA_V).astype(BF16)
            (q, k, v, dq, dk, dv, ckv_st, kpe_st, dk_st, dv_st) = _premix0(
                *xs, mod_l, g_l[0:1], w_in_p, mla_q_norm[i][None, :], w_uq_p, mla_kv_norm[i][None, :],
                w_k_p, w_v, t32, t64)
            states['mla_ckv'] = ckv_st.reshape(BATCH, 1, SEQ, MLA_KV_RANK)
            states['mla_krope'] = kpe_st[:, MLA_NOPE:MLA_NOPE + MLA_ROPE].reshape(BATCH, 1, SEQ, MLA_ROPE)
            states['diff_k'] = dk_st.reshape(BATCH, 1, SEQ, DIFF_HEADS, 2 * DIFF_DH)
            states['diff_v'] = dv_st.reshape(BATCH, 1, SEQ, DIFF_HEADS, 2 * DIFF_DH)
            kc, vc = _mla_cache(cache_mla_ckv[:, i].reshape(DEC_BATCH * PAST_LEN, MLA_KV_RANK),
                                _pe_slab(cache_mla_krope[:, i].reshape(DEC_BATCH * PAST_LEN, MLA_ROPE)),
                                w_k_p, w_v)
            dkc = cache_diff_k[:, i].reshape(DEC_BATCH * PAST_LEN, 512).astype(BF16)
            dvc = cache_diff_v[:, i].reshape(DEC_BATCH * PAST_LEN, 512).astype(BF16)
            lam = diff_lambda[i]
            sub_g = diff_norm[i][None, :]
            o_p = _attn_ab(q, dq, (k, v, dk, dv), None, lam, sub_g, lam_init,
                           n_batch=BATCH, t_len=SEQ, tq=SEQ, tok_off=0)
            o_cat = _attn_ab(q, dq, (k, v, dk, dv), (kc, vc, dkc, dvc), lam, sub_g, lam_init,
                             n_batch=DEC_BATCH, t_len=DEC_SEQ, tq=TQ_AB, tok_off=NP_TOK, out_init=o_p)
            w_out = w_out0[i]
        else:
            wi = w_in1[i]
            sq_cols = (_SWA_ORDER[:, None] * HEAD_DIM + np.arange(HEAD_DIM)[None, :]).reshape(-1)
            w_in_p = jnp.concatenate([wi[:, sq_cols], wi[:, 512:]], axis=1).astype(BF16)
            (sq, sk, sv, nq, nk, nv, sk_st, sv_st, nk_st, nv_st) = _premix1(xs[0], mod_l, g_l[0:1], w_in_p, t64)
            states['swa_k'] = sk_st.reshape(BATCH, 1, SEQ, SWA_KV_HEADS, HEAD_DIM)
            states['swa_v'] = sv_st.reshape(BATCH, 1, SEQ, SWA_KV_HEADS, HEAD_DIM)
            states['na_k'] = nk_st.reshape(BATCH, 1, SEQ, NA_HEADS, HEAD_DIM)
            states['na_v'] = nv_st.reshape(BATCH, 1, SEQ, NA_HEADS, HEAD_DIM)
            skc = cache_swa_k[:, i].reshape(DEC_BATCH * PAST_LEN, 128).astype(BF16)
            svc = cache_swa_v[:, i].reshape(DEC_BATCH * PAST_LEN, 128).astype(BF16)
            nkc = cache_na_k[:, i].reshape(DEC_BATCH * PAST_LEN, 512).astype(BF16)
            nvc = cache_na_v[:, i].reshape(DEC_BATCH * PAST_LEN, 512).astype(BF16)
            sink = swa_sink[i]
            o_p = _attn_cd_prompt(sink, sq, sk, sv, nq, nk, nv)
            o_cat = _attn_cd_sample(o_p, sink, sq, sk, sv, nq, nk, nv, skc, svc, nkc, nvc, _na_bias(na_rpb[i]))
            wo = w_out1[i]
            w_out = jnp.concatenate([wo[sq_cols], wo[512:]], axis=0)
        xs = _ffn(l, o_cat, xs, mod_l, g_l, w_out, w_router[l], b_router[l], w_gu, b_gu, w_down, b_down,
                  split_out=(l == DEPTH - 1))
    return (xs[0].reshape(BATCH, SEQ, D_MODEL), xs[1].reshape(DEC_BATCH, DEC_SEQ, D_MODEL),
            states['mla_ckv'], states['mla_krope'], states['diff_k'], states['diff_v'],
            states['swa_k'], states['swa_v'], states['na_k'], states['na_v'])
```

```python
import functools
import math

import numpy as np
import jax
import jax.numpy as jnp
from jax import lax
from jax.experimental import pallas as pl
from jax.experimental.pallas import tpu as pltpu
from jax.experimental.pallas import tpu_sc as plsc

F32 = jnp.float32
BF16 = jnp.bfloat16

D_MODEL = 1024
BATCH = 16
SEQ = 256
DEPTH = 2
DEC_BATCH = 8
DEC_SEQ = 2048
PAST_LEN = 256
GRID_W = 64
HEAD_DIM = 64
ROPE_THETA = 10000.0
EPS = 1e-6
NEG = -1e30

MLA_HEADS = 8
MLA_Q_RANK = 384
MLA_KV_RANK = 256
MLA_NOPE = 64
MLA_ROPE = 32
MLA_V = 64
DIFF_HEADS = 4
DIFF_DH = 64
SWA_HEADS = 8
SWA_KV_HEADS = 2
SWA_WINDOW = 128
NA_HEADS = 8
NA_WIN_ROWS = 8
NA_WIN_COLS = 16
N_EXPERTS = 32
TOP_K = 4
D_EXPERT = 1024
SWIGLU_LIMIT = 7.0
SWIGLU_ALPHA = 1.702

LANES = 128
NP_TOK = BATCH * SEQ
NS_TOK = DEC_BATCH * DEC_SEQ
N_TOK = NP_TOK + NS_TOK
TM = 512
NPT = NP_TOK // TM
TILES_PER_SAMPLE = DEC_SEQ // TM
N_TILES = N_TOK // TM
TQ = 256
TQ_AB = 256
MOE_TM = 512
MOE_ROWS = ((N_TOK * TOP_K + N_EXPERTS * (MOE_TM - 1)) // MOE_TM + 1) * MOE_TM
MOE_BLOCKS = MOE_ROWS // MOE_TM
NA_TILE_ROWS = TQ // GRID_W
NA_KEY_ROWS = 12
VMEM_LIMIT = 56 * 1024 * 1024


def _cparams(n_axes, vmem=VMEM_LIMIT):
    return pltpu.CompilerParams(dimension_semantics=("arbitrary",) * n_axes,
                                vmem_limit_bytes=vmem)


def _rms(x, g):
    return x * lax.rsqrt(jnp.mean(x * x, axis=-1, keepdims=True) + EPS) * g


def _dot(a, b):
    return jnp.dot(a, b, preferred_element_type=F32)


def _dot_nt(a, b):
    return lax.dot_general(a, b, (((1,), (1,)), ((), ())), preferred_element_type=F32)


def _rope(x, cos, sin_a, sin_b, half):
    return (x * cos + pltpu.roll(x, LANES - half, 1) * sin_a + pltpu.roll(x, half, 1) * sin_b)


def _mod_row(i):
    return jnp.where(i < NPT, 0, 1 + (i - NPT) // TILES_PER_SAMPLE)


def _rope_blk(i):
    return jnp.where(i < NPT, TILES_PER_SAMPLE, (i - NPT) % TILES_PER_SAMPLE)


def _mod_kernel(c_ref, w_ref, b_ref, o_ref):
    c = c_ref[...]
    s = (c * jax.nn.sigmoid(c)).astype(BF16)
    o_ref[0] = _dot(s, w_ref[0].astype(BF16)) + b_ref[0]


def _modulation(cond, w_mod, b_mod):
    nb = 1024
    return pl.pallas_call(
        _mod_kernel,
        grid=(DEPTH, 6 * D_MODEL // nb),
        in_specs=[
            pl.BlockSpec((16, D_MODEL), lambda l, n: (0, 0)),
            pl.BlockSpec((1, D_MODEL, nb), lambda l, n: (l, 0, n)),
            pl.BlockSpec((1, 1, nb), lambda l, n: (l, 0, n)),
        ],
        out_specs=pl.BlockSpec((1, 16, nb), lambda l, n: (l, 0, n)),
        out_shape=jax.ShapeDtypeStruct((DEPTH, 16, 6 * D_MODEL), F32),
        compiler_params=_cparams(2),
        name="modulation",
    )(cond, w_mod, b_mod.reshape(DEPTH, 1, 6 * D_MODEL))


def _rope_tables():
    t = jnp.arange(DEC_SEQ)
    rows = (t // GRID_W).astype(F32)
    cols = (t % GRID_W).astype(F32)

    def angles(r):
        n = r // 4
        inv = ROPE_THETA ** (-jnp.arange(n, dtype=F32) / n)
        return jnp.concatenate([rows[:, None] * inv[None], cols[:, None] * inv[None]], axis=-1)

    def finish(cos, sa, sb):
        ident = (jnp.ones((TM, LANES), F32), jnp.zeros((TM, LANES), F32), jnp.zeros((TM, LANES), F32))
        return tuple(jnp.concatenate([a, b], axis=0) for a, b in zip((cos, sa, sb), ident))

    a64 = angles(64)
    c, s, z = jnp.cos(a64), jnp.sin(a64), jnp.zeros_like(a64)
    t64 = finish(jnp.concatenate([c, c, c, c], -1), jnp.concatenate([-s, z, -s, z], -1),
                 jnp.concatenate([z, s, z, s], -1))
    a32 = angles(32)
    c, s, z = jnp.cos(a32), jnp.sin(a32), jnp.zeros_like(a32)
    one64 = jnp.ones((DEC_SEQ, 64), F32)
    z64 = jnp.zeros((DEC_SEQ, 64), F32)
    z32 = jnp.zeros((DEC_SEQ, 32), F32)
    t32 = finish(jnp.concatenate([one64, c, c, z32], -1), jnp.concatenate([z64, -s, z, z32], -1),
                 jnp.concatenate([z64, z, s, z32], -1))
    return t64, t32


LOG2E = math.log2(math.e)
_MLA_SCALE = (MLA_NOPE + MLA_ROPE) ** -0.5 * LOG2E
_QSCALE = HEAD_DIM ** -0.5 * LOG2E


def _premix0_kernel(xp_ref, xs_ref, mod_ref, g_ref, win_ref, qn_ref, wuq_ref, kvn_ref, wk_ref, wv_ref,
                    c32_ref, sa32_ref, sb32_ref, c64_ref, sa64_ref, sb64_ref,
                    q_ref, k_ref, v_ref, dq_ref, dk_ref, dv_ref,
                    ckv_st, kpe_st, dk_st, dv_st):
    i = pl.program_id(0)
    m = mod_ref[0]
    h = _rms(_pick_x(i, xp_ref, xs_ref), g_ref[...]) * (1.0 + m[1:2]) + m[0:1]
    proj = _dot(h.astype(BF16), win_ref[...])
    q_a = proj[:, 0:384]
    kv_a = proj[:, 384:640]
    dq = proj[:, 640:1152]
    dk = proj[:, 1152:1664]
    dv = proj[:, 1664:2176]
    pe = proj[:, 2176:2304]
    q = _dot(_rms(q_a, qn_ref[...]).astype(BF16), wuq_ref[...])
    ckv = _rms(kv_a, kvn_ref[...])
    ckv_b = ckv.astype(BF16)
    kn = _dot(ckv_b, wk_ref[...])
    v_ref[...] = _dot(ckv_b, wv_ref[...]).astype(BF16)
    c32, sa32, sb32 = c32_ref[...], sa32_ref[...], sb32_ref[...]
    c64, sa64, sb64 = c64_ref[...], sa64_ref[...], sb64_ref[...]
    pe_r = _rope(pe, c32, sa32, sb32, MLA_ROPE // 2)
    for hd in range(MLA_HEADS):
        sl = slice(LANES * hd, LANES * (hd + 1))
        q_ref[:, sl] = (_rope(q[:, sl], c32, sa32, sb32, MLA_ROPE // 2) * _MLA_SCALE).astype(BF16)
        k_ref[:, sl] = (kn[:, sl] + pe_r).astype(BF16)
    for hd in range(DIFF_HEADS):
        sl = slice(LANES * hd, LANES * (hd + 1))
        dq_ref[:, sl] = (_rope(dq[:, sl], c64, sa64, sb64, DIFF_DH // 2) * _QSCALE).astype(BF16)
        dk_ref[:, sl] = _rope(dk[:, sl], c64, sa64, sb64, DIFF_DH // 2).astype(BF16)
    dv_ref[...] = dv.astype(BF16)

    @pl.when(i < NPT)
    def _():
        ckv_st[...] = ckv
        kpe_st[...] = pe
        dk_st[...] = dk
        dv_st[...] = dv


def _premix1_kernel(x_ref, mod_ref, g_ref, win_ref, c64_ref, sa64_ref, sb64_ref,
                    sq_ref, sk_ref, sv_ref, nq_ref, nk_ref, nv_ref,
                    sk_st, sv_st, nk_st, nv_st):
    i = pl.program_id(0)
    m = mod_ref[0]
    h = _rms(x_ref[...], g_ref[...]) * (1.0 + m[1:2]) + m[0:1]
    proj = _dot(h.astype(BF16), win_ref[...])
    sq = proj[:, 0:512]
    sk = proj[:, 512:640]
    sv = proj[:, 640:768]
    nq = proj[:, 768:1280]
    nk = proj[:, 1280:1792]
    nv = proj[:, 1792:2304]
    c64, sa64, sb64 = c64_ref[...], sa64_ref[...], sb64_ref[...]
    for hd in range(4):
        sl = slice(LANES * hd, LANES * (hd + 1))
        sq_ref[:, sl] = (_rope(sq[:, sl], c64, sa64, sb64, HEAD_DIM // 2) * _QSCALE).astype(BF16)
    sk_ref[...] = _rope(sk, c64, sa64, sb64, HEAD_DIM // 2).astype(BF16)
    sv_ref[...] = sv.astype(BF16)
    nq_ref[...] = (nq * _QSCALE).astype(BF16)
    nk_ref[...] = nk.astype(BF16)
    nv_ref[...] = nv.astype(BF16)

    @pl.when(i < NPT)
    def _():
        sk_st[...] = sk
        sv_st[...] = sv
        nk_st[...] = nk
        nv_st[...] = nv


def _tok_spec(width):
    return pl.BlockSpec((TM, width), lambda i: (i, 0))


_PROMPT_SPEC = pl.BlockSpec((TM, D_MODEL), lambda i: (jnp.minimum(i, NPT - 1), 0))
_SAMPLE_SPEC = pl.BlockSpec((TM, D_MODEL), lambda i: (jnp.maximum(i - NPT, 0), 0))


def _pick_x(i, xp_ref, xs_ref):
    return jnp.where(i < NPT, xp_ref[...], xs_ref[...])


def _state_spec(width):
    return pl.BlockSpec((TM, width), lambda i: (jnp.minimum(i, NPT - 1), 0))


def _const_spec(shape):
    return pl.BlockSpec(shape, lambda i: (0,) * len(shape))


_MOD_SPEC = pl.BlockSpec((1, 6, D_MODEL), lambda i: (_mod_row(i), 0, 0))
_ROPE_SPEC = pl.BlockSpec((TM, LANES), lambda i: (_rope_blk(i), 0))


def _premix0(xp, xs, mod_l, g0, w_in_p, q_norm, w_uq_p, kv_norm, w_k_p, w_v, t32, t64):
    outs = [(N_TOK, 1024, BF16), (N_TOK, 1024, BF16), (N_TOK, 512, BF16), (N_TOK, 512, BF16),
            (N_TOK, 512, BF16), (N_TOK, 512, BF16),
            (NP_TOK, 256, F32), (NP_TOK, 128, F32), (NP_TOK, 512, F32), (NP_TOK, 512, F32)]
    return pl.pallas_call(
        _premix0_kernel,
        grid=(N_TILES,),
        in_specs=[_PROMPT_SPEC, _SAMPLE_SPEC, _MOD_SPEC, _const_spec((1, D_MODEL)),
                  _const_spec((D_MODEL, 2304)), _const_spec((1, MLA_Q_RANK)),
                  _const_spec((MLA_Q_RANK, 1024)), _const_spec((1, MLA_KV_RANK)),
                  _const_spec((MLA_KV_RANK, 1024)), _const_spec((MLA_KV_RANK, 512))]
                 + [_ROPE_SPEC] * 6,
        out_specs=[_tok_spec(w) for (_, w, _) in outs[:6]] + [_state_spec(w) for (_, w, _) in outs[6:]],
        out_shape=[jax.ShapeDtypeStruct((n, w), dt) for (n, w, dt) in outs],
        compiler_params=_cparams(1),
        name="premix_ab",
    )(xp, xs, mod_l, g0, w_in_p, q_norm, w_uq_p, kv_norm, w_k_p, w_v, *t32, *t64)


def _premix1(x, mod_l, g0, w_in_p, t64):
    outs = [(N_TOK, 512, BF16), (N_TOK, 128, BF16), (N_TOK, 128, BF16), (N_TOK, 512, BF16),
            (N_TOK, 512, BF16), (N_TOK, 512, BF16),
            (NP_TOK, 128, F32), (NP_TOK, 128, F32), (NP_TOK, 512, F32), (NP_TOK, 512, F32)]
    return pl.pallas_call(
        _premix1_kernel,
        grid=(N_TILES,),
        in_specs=[_tok_spec(D_MODEL), _MOD_SPEC, _const_spec((1, D_MODEL)),
                  _const_spec((D_MODEL, 2304))] + [_ROPE_SPEC] * 3,
        out_specs=[_tok_spec(w) for (_, w, _) in outs[:6]] + [_state_spec(w) for (_, w, _) in outs[6:]],
        out_shape=[jax.ShapeDtypeStruct((n, w), dt) for (n, w, dt) in outs],
        compiler_params=_cparams(1),
        name="premix_cd",
    )(x, mod_l, g0, w_in_p, *t64)


def _mla_cache_kernel(ckv_ref, pe_ref, wk_ref, wv_ref, k_ref, v_ref):
    c = ckv_ref[...].astype(BF16)
    kn = _dot(c, wk_ref[...])
    v_ref[...] = _dot(c, wv_ref[...]).astype(BF16)
    pe = pe_ref[...]
    for hd in range(MLA_HEADS):
        sl = slice(LANES * hd, LANES * (hd + 1))
        k_ref[:, sl] = (kn[:, sl] + pe).astype(BF16)


def _mla_cache(ckv, pe_slab, w_k_p, w_v):
    n = ckv.shape[0]
    tm = 512
    return pl.pallas_call(
        _mla_cache_kernel,
        grid=(n // tm,),
        in_specs=[pl.BlockSpec((tm, MLA_KV_RANK), lambda i: (i, 0)),
                  pl.BlockSpec((tm, LANES), lambda i: (i, 0)),
                  _const_spec((MLA_KV_RANK, 1024)), _const_spec((MLA_KV_RANK, 512))],
        out_specs=[pl.BlockSpec((tm, 1024), lambda i: (i, 0)), pl.BlockSpec((tm, 512), lambda i: (i, 0))],
        out_shape=[jax.ShapeDtypeStruct((n, 1024), BF16), jax.ShapeDtypeStruct((n, 512), BF16)],
        compiler_params=_cparams(1),
        name="mla_cache",
    )(ckv, pe_slab, w_k_p, w_v)


def _softmax_pv(scores, values, sink=None):
    m = jnp.max(scores[0], axis=-1, keepdims=True)
    for s in scores[1:]:
        m = jnp.maximum(m, jnp.max(s, axis=-1, keepdims=True))
    if sink is not None:
        m = jnp.maximum(m, sink)
    l = None
    o = None
    for s, v in zip(scores, values):
        p = jnp.exp2(s - m)
        ls = jnp.sum(p, axis=-1, keepdims=True)
        os_ = _dot(p.astype(BF16), v)
        l = ls if l is None else l + ls
        o = os_ if o is None else o + os_
    if sink is not None:
        l = l + jnp.exp2(sink - m)
    return o * (1.0 / l)


def _lane_lo(shape):
    return lax.broadcasted_iota(jnp.int32, shape, 1) < (LANES // 2)


def _split_halves(qb):
    lo = _lane_lo(qb.shape)
    zero = jnp.zeros_like(qb)
    return jnp.where(lo, qb, zero), jnp.where(lo, zero, qb)


def _attn_ab_kernel(*refs, n_pieces, lam_init, aliased):
    if aliased:
        refs = refs[1:]
    q_ref, dq_ref = refs[0], refs[1]
    pieces = [refs[2 + 4 * p: 6 + 4 * p] for p in range(n_pieces)]
    lam_ref, subg_ref, o_ref = refs[2 + 4 * n_pieces:]
    lam = lam_ref[...]
    lam_full = (jnp.exp(jnp.sum(lam[0:1] * lam[1:2], axis=-1, keepdims=True))
                - jnp.exp(jnp.sum(lam[2:3] * lam[3:4], axis=-1, keepdims=True)) + lam_init)
    lo = _lane_lo((q_ref.shape[0], LANES))
    for j in range(MLA_HEADS // 2):
        pair = []
        for half in range(2):
            hd = 2 * j + half
            sl = slice(LANES * hd, LANES * (hd + 1))
            qh = q_ref[:, sl]
            scores = [_dot_nt(qh, k_ref[:, sl]) for (k_ref, _, _, _) in pieces]
            vals = [v_ref[:, LANES * j: LANES * (j + 1)] for (_, v_ref, _, _) in pieces]
            pair.append(_softmax_pv(scores, vals))
        o_ref[:, LANES * j: LANES * (j + 1)] = jnp.where(lo, pair[0], pair[1]).astype(BF16)
    subg = subg_ref[...]
    tq = q_ref.shape[0]
    for hd in range(DIFF_HEADS):
        sl = slice(LANES * hd, LANES * (hd + 1))
        qq = jnp.concatenate(_split_halves(dq_ref[:, sl]), axis=0)
        ks = [dk_ref[:, sl] for (_, _, dk_ref, _) in pieces]
        vs = [dv_ref[:, sl] for (_, _, _, dv_ref) in pieces]
        oo = _softmax_pv([_dot_nt(qq, k) for k in ks], vs)
        od = _rms(oo[:tq] - lam_full * oo[tq:], subg) * (1.0 - lam_init)
        o_ref[:, 512 + LANES * hd: 512 + LANES * (hd + 1)] = od.astype(BF16)


def _attn_ab(q, dq, new_kv, cache_kv, lam, sub_g, lam_init, *, n_batch, t_len, tq, tok_off, out_init=None):
    nq = t_len // tq
    q_off = tok_off // tq
    b_off = tok_off // t_len
    widths = (1024, 512, 512, 512)
    in_specs = [pl.BlockSpec((tq, 1024), lambda b, i: (q_off + b * nq + i, 0)),
                pl.BlockSpec((tq, 512), lambda b, i: (q_off + b * nq + i, 0))]
    args = [q, dq]
    for w, a in zip(widths, new_kv):
        in_specs.append(pl.BlockSpec((t_len, w), lambda b, i: (b_off + b, 0)))
        args.append(a)
    n_pieces = 1
    if cache_kv is not None:
        n_pieces = 2
        for w, a in zip(widths, cache_kv):
            in_specs.append(pl.BlockSpec((PAST_LEN, w), lambda b, i: (b, 0)))
            args.append(a)
    in_specs += [pl.BlockSpec((4, DIFF_DH), lambda b, i: (0, 0)),
                 pl.BlockSpec((1, 2 * DIFF_DH), lambda b, i: (0, 0))]
    args += [lam, sub_g]
    aliases = {}
    if out_init is not None:
        in_specs = [pl.BlockSpec(memory_space=pl.ANY)] + in_specs
        args = [out_init] + args
        aliases = {0: 0}
    return pl.pallas_call(
        functools.partial(_attn_ab_kernel, n_pieces=n_pieces, lam_init=lam_init, aliased=out_init is not None),
        grid=(n_batch, nq),
        in_specs=in_specs,
        out_specs=pl.BlockSpec((tq, 1024), lambda b, i: (q_off + b * nq + i, 0)),
        out_shape=jax.ShapeDtypeStruct((N_TOK, 1024), BF16),
        input_output_aliases=aliases,
        compiler_params=_cparams(2),
        name="attn_ab_%d" % n_pieces,
    )(*args)


def _gqa_stacks(sq_ref, sink_ref):
    tq = sq_ref.shape[0]
    halves = [_split_halves(sq_ref[:, LANES * j: LANES * (j + 1)]) for j in range(4)]
    q_stacks = [jnp.concatenate([halves[j][kvh] for j in range(4)], axis=0) for kvh in range(SWA_KV_HEADS)]
    sinks = [jnp.concatenate([jnp.full((tq, 1), sink_ref[4 * kvh + j] * LOG2E, F32) for j in range(4)], axis=0)
             for kvh in range(SWA_KV_HEADS)]
    return q_stacks, sinks


def _attn_cd_prompt_kernel(sink_ref, sq_ref, sk_ref, sv_ref, nq_ref, nk_ref, nv_ref, o_ref):
    tq = sq_ref.shape[0]
    lo = _lane_lo((tq, LANES))
    sk = sk_ref[...]
    sv = sv_ref[...]
    q_stacks, sinks = _gqa_stacks(sq_ref, sink_ref)
    o_kv = [_softmax_pv([_dot_nt(q, sk)], [sv], sink=s) for q, s in zip(q_stacks, sinks)]
    for j in range(4):
        rows = slice(tq * j, tq * (j + 1))
        o_ref[:, LANES * j: LANES * (j + 1)] = jnp.where(lo, o_kv[0][rows], o_kv[1][rows]).astype(BF16)
    for j in range(4):
        sl = slice(LANES * j, LANES * (j + 1))
        q_lo, q_hi = _split_halves(nq_ref[:, sl])
        k = nk_ref[:, sl]
        v = nv_ref[:, sl]
        o_lo = _softmax_pv([_dot_nt(q_lo, k)], [v])
        o_hi = _softmax_pv([_dot_nt(q_hi, k)], [v])
        o_ref[:, 512 + LANES * j: 512 + LANES * (j + 1)] = jnp.where(lo, o_lo, o_hi).astype(BF16)


def _attn_cd_prompt(sink, sq, sk, sv, nq, nk, nv):
    def spec(w):
        return pl.BlockSpec((SEQ, w), lambda b: (b, 0))
    return pl.pallas_call(
        _attn_cd_prompt_kernel,
        grid=(BATCH,),
        in_specs=[pl.BlockSpec(memory_space=pltpu.SMEM), spec(512), spec(128), spec(128),
                  spec(512), spec(512), spec(512)],
        out_specs=spec(1024),
        out_shape=jax.ShapeDtypeStruct((N_TOK, 1024), BF16),
        compiler_params=_cparams(1),
        name="attn_cd_prompt",
    )(sink, sq, sk, sv, nq, nk, nv)


_SWA_KEYS = TQ + 2 * SWA_WINDOW


def _attn_cd_sample_kernel(init_ref, sink_ref, sq_ref, nq_ref, sk_ref, sv_ref, nk_ref, nv_ref,
                           skc_ref, svc_ref, nkc_ref, nvc_ref, bias_ref, o_ref):
    del init_ref
    qi = pl.program_id(1)
    lo = _lane_lo((TQ, LANES))
    ks = pl.multiple_of(jnp.clip(qi * TQ - SWA_WINDOW, 0, DEC_SEQ - _SWA_KEYS), SWA_WINDOW)
    k_win = sk_ref[pl.ds(ks, _SWA_KEYS), :]
    v_win = sv_ref[pl.ds(ks, _SWA_KEYS), :]
    q_pos = qi * TQ + (lax.broadcasted_iota(jnp.int32, (4 * TQ, _SWA_KEYS), 0) & (TQ - 1))
    k_pos = ks + lax.broadcasted_iota(jnp.int32, (4 * TQ, _SWA_KEYS), 1)
    in_win = jnp.abs(q_pos - k_pos) <= SWA_WINDOW
    skc = skc_ref[...]
    svc = svc_ref[...]
    q_stacks, sinks = _gqa_stacks(sq_ref, sink_ref)
    o_kv = []
    for q, s in zip(q_stacks, sinks):
        s_lat = jnp.where(in_win, _dot_nt(q, k_win), NEG)
        o_kv.append(_softmax_pv([_dot_nt(q, skc), s_lat], [svc, v_win], sink=s))
    for j in range(4):
        rows = slice(TQ * j, TQ * (j + 1))
        o_ref[:, LANES * j: LANES * (j + 1)] = jnp.where(lo, o_kv[0][rows], o_kv[1][rows]).astype(BF16)
    n_rows = DEC_SEQ // GRID_W
    r0 = jnp.clip(qi * NA_TILE_ROWS - NA_WIN_ROWS // 2, 0, n_rows - NA_KEY_ROWS)
    kn = pl.multiple_of(r0 * GRID_W, GRID_W)
    for j in range(4):
        sl = slice(LANES * j, LANES * (j + 1))
        k_win = nk_ref[pl.ds(kn, NA_KEY_ROWS * GRID_W), sl]
        v_win = nv_ref[pl.ds(kn, NA_KEY_ROWS * GRID_W), sl]
        kc = nkc_ref[:, sl]
        vc = nvc_ref[:, sl]
        outs = []
        for q_half, hd in zip(_split_halves(nq_ref[:, sl]), (2 * j, 2 * j + 1)):
            s_lat = _dot_nt(q_half, k_win) + bias_ref[0, hd]
            outs.append(_softmax_pv([_dot_nt(q_half, kc), s_lat], [vc, v_win]))
        o_ref[:, 512 + LANES * j: 512 + LANES * (j + 1)] = jnp.where(lo, outs[0], outs[1]).astype(BF16)


def _attn_cd_sample(out_init, sink, sq, sk, sv, nq, nk, nv, skc, svc, nkc, nvc, bias):
    nq_t = DEC_SEQ // TQ
    q_off = NP_TOK // TQ
    b_off = NP_TOK // DEC_SEQ

    def qspec(w):
        return pl.BlockSpec((TQ, w), lambda b, i: (q_off + b * nq_t + i, 0))

    def kspec(w):
        return pl.BlockSpec((DEC_SEQ, w), lambda b, i: (b_off + b, 0))

    def cspec(w):
        return pl.BlockSpec((PAST_LEN, w), lambda b, i: (b, 0))

    n_keys = NA_KEY_ROWS * GRID_W
    bias_spec = pl.BlockSpec(
        (1, NA_HEADS, TQ, n_keys),
        lambda b, i: (jnp.where(i == 0, 0, jnp.where(i == nq_t - 1, 2, 1)), 0, 0, 0))
    return pl.pallas_call(
        _attn_cd_sample_kernel,
        grid=(DEC_BATCH, nq_t),
        in_specs=[pl.BlockSpec(memory_space=pl.ANY), pl.BlockSpec(memory_space=pltpu.SMEM),
                  qspec(512), qspec(512), kspec(128), kspec(128), kspec(512), kspec(512),
                  cspec(128), cspec(128), cspec(512), cspec(512), bias_spec],
        out_specs=pl.BlockSpec((TQ, 1024), lambda b, i: (q_off + b * nq_t + i, 0)),
        out_shape=jax.ShapeDtypeStruct((N_TOK, 1024), BF16),
        input_output_aliases={0: 0},
        compiler_params=_cparams(2),
        name="attn_cd_sample",
    )(out_init, sink, sq, nq, sk, sv, nk, nv, skc, svc, nkc, nvc, bias)


def _na_bias(rpb):
    n_rows = DEC_SEQ // GRID_W
    n_dr = 2 * NA_WIN_ROWS - 1
    n_dc = 2 * NA_WIN_COLS - 1
    c = np.arange(GRID_W)[:, None]
    kc = np.arange(GRID_W)[None, :]
    qs = np.clip(c - NA_WIN_COLS // 2, 0, GRID_W - NA_WIN_COLS)
    col_ok = (kc >= qs) & (kc < qs + NA_WIN_COLS)
    dc = np.clip(kc - c + NA_WIN_COLS - 1, 0, n_dc - 1)
    onehot = ((dc[None] == np.arange(n_dc)[:, None, None]) & col_ok[None]).astype(np.float32)
    blocks = jnp.einsum('hrd,dck->hrck', rpb.astype(F32) * LOG2E, onehot, precision=lax.Precision.HIGHEST)
    row_sel = np.zeros((3, n_dr, NA_TILE_ROWS, NA_KEY_ROWS), np.float32)
    for v, q_row0 in enumerate((0, NA_TILE_ROWS, n_rows - NA_TILE_ROWS)):
        k_row0 = min(max(q_row0 - NA_WIN_ROWS // 2, 0), n_rows - NA_KEY_ROWS)
        for ri in range(NA_TILE_ROWS):
            r = q_row0 + ri
            rs = min(max(r - NA_WIN_ROWS // 2, 0), n_rows - NA_WIN_ROWS)
            for kj in range(NA_KEY_ROWS):
                kr = k_row0 + kj
                if rs <= kr < rs + NA_WIN_ROWS:
                    row_sel[v, kr - r + NA_WIN_ROWS - 1, ri, kj] = 1.0
    tiles = jnp.einsum('vdrj,hdck->vhrcjk', row_sel, blocks, precision=lax.Precision.HIGHEST)
    valid = (row_sel.sum(axis=1) > 0)[:, :, None, :, None] & col_ok[None, None, :, None, :]
    tiles = jnp.where(valid[:, None], tiles, NEG)
    return tiles.reshape(3, NA_HEADS, TQ, NA_KEY_ROWS * GRID_W)


_HI_MASK = -65536


def _pack_pairs(x):
    w = x.shape[1] // 2
    r = x.astype(BF16).astype(F32)
    lo = lax.bitcast_convert_type(r[:, :w], jnp.int32)
    hi = lax.bitcast_convert_type(r[:, w:], jnp.int32)
    return (hi & _HI_MASK) | lax.shift_right_logical(lo, 16)


def _unpack_pairs(p):
    lo = lax.bitcast_convert_type(lax.shift_left(p, 16), F32)
    hi = lax.bitcast_convert_type(p & _HI_MASK, F32)
    return lo, hi


def _postmix_kernel(*refs, split_x):
    if split_x:
        o_ref, xp_ref, xs_ref = refs[:3]
        refs = refs[3:]
    else:
        o_ref, x_ref = refs[:2]
        refs = refs[2:]
    (mod_ref, g_ref, wout_ref, wr_ref, br_ref, tri_ref,
     x1_ref, h2_ref, route_ref, gate_ref, cnt_ref, run_ref) = refs
    i = pl.program_id(0)
    x = _pick_x(i, xp_ref, xs_ref) if split_x else x_ref[...]

    @pl.when(i == 0)
    def _():
        run_ref[...] = jnp.zeros_like(run_ref)

    m = mod_ref[0]
    g = g_ref[...]
    y = _dot(o_ref[...], wout_ref[...])
    x1 = x + m[2:3] * _rms(y, g[1:2])
    x1_ref[...] = x1
    h2 = _rms(x1, g[2:3]) * (1.0 + m[4:5]) + m[3:4]
    h2_ref[...] = _pack_pairs(h2)
    logits = _dot(h2.astype(BF16), wr_ref[...]) + br_ref[...]
    lane = lax.broadcasted_iota(jnp.int32, logits.shape, 1).astype(F32)
    cur = jnp.where(lane < N_EXPERTS, logits, -jnp.inf)
    tops, idxs = [], []
    for _ in range(TOP_K):
        mx = jnp.max(cur, axis=-1, keepdims=True)
        ix = jnp.min(jnp.where(cur == mx, lane, float(LANES)), axis=-1, keepdims=True)
        tops.append(mx)
        idxs.append(ix)
        cur = jnp.where(lane == ix, -jnp.inf, cur)
    es = [jnp.exp(t - tops[0]) for t in tops]
    inv = 1.0 / (es[0] + es[1] + es[2] + es[3])
    picked = jnp.zeros_like(logits)
    for k in range(TOP_K):
        picked = jnp.where(lane == idxs[k], 1.0, picked)
    before = _dot(tri_ref[...], picked.astype(BF16)) + run_ref[0:1, :]
    route = jnp.zeros_like(logits)
    gate_out = jnp.zeros_like(logits)
    for k in range(TOP_K):
        rank = jnp.sum(jnp.where(lane == idxs[k], before, 0.0), axis=-1, keepdims=True)
        route = jnp.where(lane == float(k), idxs[k], route)
        route = jnp.where(lane == float(TOP_K + k), rank, route)
        gate_out = jnp.where(lane == float(k), es[k] * inv, gate_out)
    route_ref[...] = route.astype(jnp.int32)
    gate_ref[...] = gate_out
    run_ref[...] = run_ref[...] + jnp.sum(picked, axis=0, keepdims=True)
    cnt_ref[...] = run_ref[...].astype(jnp.int32)


def _postmix(o_cat, xs, mod_l, g_l, w_out, w_r, b_r):
    tri = jnp.asarray(np.tril(np.ones((TM, TM), np.float32), -1), BF16)
    split_x = len(xs) == 2
    x_specs = [_PROMPT_SPEC, _SAMPLE_SPEC] if split_x else [_tok_spec(D_MODEL)]
    return pl.pallas_call(
        functools.partial(_postmix_kernel, split_x=split_x),
        grid=(N_TILES,),
        in_specs=[_tok_spec(1024)] + x_specs + [_MOD_SPEC, _const_spec((4, D_MODEL)),
                  _const_spec((1024, D_MODEL)), _const_spec((D_MODEL, LANES)), _const_spec((1, LANES)),
                  _const_spec((TM, TM))],
        out_specs=[_tok_spec(D_MODEL), _tok_spec(D_MODEL // 2), _tok_spec(LANES), _tok_spec(LANES),
                   _const_spec((8, LANES))],
        out_shape=[jax.ShapeDtypeStruct((N_TOK, D_MODEL), F32),
                   jax.ShapeDtypeStruct((N_TOK, D_MODEL // 2), jnp.int32),
                   jax.ShapeDtypeStruct((N_TOK, LANES), jnp.int32), jax.ShapeDtypeStruct((N_TOK, LANES), F32),
                   jax.ShapeDtypeStruct((8, LANES), jnp.int32)],
        scratch_shapes=[pltpu.VMEM((8, LANES), F32)],
        compiler_params=_cparams(1),
        name="postmix",
    )(o_cat, *xs, mod_l, g_l, w_out, w_r, b_r, tri)


SC_WORKERS = 32
SC_ROWS = 128
ROW_WORDS = D_MODEL // 2


def _sc_worker_id():
    return lax.axis_index("s") * 2 + lax.axis_index("c")


def _sc_dispatch(src, idx):
    n_chunks = N_TOK // (SC_WORKERS * SC_ROWS)
    mesh = plsc.VectorSubcoreMesh(core_axis_name="c", subcore_axis_name="s")

    @functools.partial(
        pl.kernel, mesh=mesh,
        out_type=jax.ShapeDtypeStruct((MOE_ROWS, ROW_WORDS), jnp.int32),
        scratch_types=[pltpu.VMEM((n_chunks * TOP_K, SC_ROWS), jnp.int32),
                       pltpu.VMEM((SC_ROWS, ROW_WORDS), jnp.int32), pltpu.SemaphoreType.DMA])
    def k(src_hbm, idx_hbm, out_hbm, idx_v, rows_v, sem):
        wid = _sc_worker_id()
        pltpu.sync_copy(idx_hbm.at[wid], idx_v)

        @pl.loop(0, n_chunks)
        def _(g):
            pltpu.sync_copy(src_hbm.at[pl.ds((wid * n_chunks + g) * SC_ROWS, SC_ROWS)], rows_v)
            copies = [pltpu.async_copy(rows_v, out_hbm.at[idx_v.at[g * TOP_K + kk]], sem)
                      for kk in range(TOP_K)]
            for cp in copies:
                cp.wait()

    return k(src, idx)


def _sc_collect(table, idx):
    n_chunks = idx.shape[1]
    mesh = plsc.VectorSubcoreMesh(core_axis_name="c", subcore_axis_name="s")

    @functools.partial(
        pl.kernel, mesh=mesh,
        out_type=jax.ShapeDtypeStruct((SC_WORKERS * n_chunks * SC_ROWS, ROW_WORDS), jnp.int32),
        scratch_types=[pltpu.VMEM((n_chunks, SC_ROWS), jnp.int32),
                       pltpu.VMEM((SC_ROWS, ROW_WORDS), jnp.int32), pltpu.SemaphoreType.DMA])
    def k(table_hbm, idx_hbm, out_hbm, idx_v, rows_v, sem):
        wid = _sc_worker_id()
        pltpu.sync_copy(idx_hbm.at[wid], idx_v)

        @pl.loop(0, n_chunks)
        def _(g):
            pltpu.async_copy(table_hbm.at[idx_v.at[g]], rows_v, sem).wait()
            pltpu.sync_copy(rows_v, out_hbm.at[pl.ds((wid * n_chunks + g) * SC_ROWS, SC_ROWS)])

    return k(table, idx)


def _expert_rows(words, n_valid, wgu_b, wd_b, bgu, bd):
    live = lax.broadcasted_iota(jnp.int32, words.shape, 0) < n_valid
    lo, hi = _unpack_pairs(jnp.where(live, words, 0))
    x = jnp.concatenate([lo, hi], axis=1).astype(BF16)
    gu = _dot(x, wgu_b[...]) + bgu
    g = jnp.minimum(gu[:, :D_EXPERT], SWIGLU_LIMIT)
    u = jnp.clip(gu[:, D_EXPERT:], -SWIGLU_LIMIT, SWIGLU_LIMIT)
    a = g * jax.nn.sigmoid(SWIGLU_ALPHA * g) * (u + 1.0)
    return _pack_pairs(_dot(a.astype(BF16), wd_b[...]) + bd)


def _moe_kernel(blk_e_ref, blk_first_ref, blk_rows_ref, x_ref, wgu_ref, bgu_ref, wd_ref, bd_ref,
                y_ref, wgu_b, wd_b):
    i = pl.program_id(0)
    n_valid = blk_rows_ref[i]
    half = MOE_TM // 2

    @pl.when(blk_first_ref[i] == 1)
    def _():
        wgu_b[...] = wgu_ref[0, 0].astype(BF16)
        wd_b[...] = wd_ref[0, 0].astype(BF16)

    @pl.when(n_valid > half)
    def _():
        y_ref[...] = _expert_rows(x_ref[...], n_valid, wgu_b, wd_b, bgu_ref[0, 0], bd_ref[0, 0])

    @pl.when((n_valid > 0) & (n_valid <= half))
    def _():
        y_ref[:half] = _expert_rows(x_ref[:half], n_valid, wgu_b, wd_b, bgu_ref[0, 0], bd_ref[0, 0])
        y_ref[half:] = jnp.zeros((half, ROW_WORDS), jnp.int32)

    @pl.when(n_valid == 0)
    def _():
        y_ref[...] = jnp.zeros_like(y_ref)


def _moe(layer, blk_e, blk_first, blk_rows, xs, w_gu, b_gu, w_down, b_down):
    grid_spec = pltpu.PrefetchScalarGridSpec(
        num_scalar_prefetch=3,
        grid=(MOE_BLOCKS,),
        in_specs=[
            pl.BlockSpec((MOE_TM, ROW_WORDS), lambda i, e, f, o: (i, 0)),
            pl.BlockSpec((1, 1, D_MODEL, 2 * D_EXPERT), lambda i, e, f, o: (layer, e[i], 0, 0)),
            pl.BlockSpec((1, 1, 1, 2 * D_EXPERT), lambda i, e, f, o: (layer, e[i], 0, 0)),
            pl.BlockSpec((1, 1, D_EXPERT, D_MODEL), lambda i, e, f, o: (layer, e[i], 0, 0)),
            pl.BlockSpec((1, 1, 1, D_MODEL), lambda i, e, f, o: (layer, e[i], 0, 0)),
        ],
        out_specs=pl.BlockSpec((MOE_TM, ROW_WORDS), lambda i, e, f, o: (i, 0)),
        scratch_shapes=[pltpu.VMEM((D_MODEL, 2 * D_EXPERT), BF16), pltpu.VMEM((D_EXPERT, D_MODEL), BF16)],
    )
    return pl.pallas_call(
        _moe_kernel,
        grid_spec=grid_spec,
        out_shape=jax.ShapeDtypeStruct((MOE_ROWS, ROW_WORDS), jnp.int32),
        compiler_params=_cparams(1),
        name="moe_experts",
    )(blk_e, blk_first, blk_rows, xs, w_gu, b_gu.reshape(DEPTH, N_EXPERTS, 1, 2 * D_EXPERT),
      w_down, b_down.reshape(DEPTH, N_EXPERTS, 1, D_MODEL))


def _route(route, counts):
    experts = jnp.arange(N_EXPERTS, dtype=jnp.int32)
    padded = (counts + MOE_TM - 1) // MOE_TM * MOE_TM
    pad_end = jnp.cumsum(padded)
    pad_start = pad_end - padded
    e = route[:, 0:TOP_K]
    onehot = e[:, :, None] == experts[None, None, :]
    dest = jnp.sum(jnp.where(onehot, pad_start[None, None, :], 0), axis=-1) + route[:, TOP_K:2 * TOP_K]
    blk_row0 = jnp.arange(MOE_BLOCKS, dtype=jnp.int32) * MOE_TM
    blk_e = jnp.minimum(jnp.sum((pad_end[None, :] <= blk_row0[:, None]).astype(jnp.int32), axis=1),
                        N_EXPERTS - 1)
    on = blk_row0 < pad_end[-1]
    n_on = jnp.sum(on.astype(jnp.int32))
    blk_onehot = blk_e[:, None] == experts[None, :]
    row_end = jnp.sum(jnp.where(blk_onehot, (pad_start + counts)[None, :], 0), axis=1)
    blk_rows = jnp.where(on, jnp.clip(row_end - blk_row0, 0, MOE_TM), 0).astype(jnp.int32)
    last_e = jnp.sum(jnp.where(jnp.arange(MOE_BLOCKS) == n_on - 1, blk_e, 0))
    blk_e = jnp.where(on, blk_e, last_e).astype(jnp.int32)
    prev = jnp.concatenate([jnp.full((1,), -1, jnp.int32), blk_e[:-1]])
    blk_first = (blk_e != prev).astype(jnp.int32)
    return dest, blk_e, blk_first, blk_rows


def _combine_kernel(y_ref, gate_ref, x1_ref, mod_ref, g_ref, *o_refs):
    i = pl.program_id(0)
    m = mod_ref[0]
    gate = gate_ref[...]
    acc_lo = None
    for k in range(TOP_K):
        lo, hi = _unpack_pairs(y_ref[k])
        gk = gate[:, k:k + 1]
        acc_lo = gk * lo if acc_lo is None else acc_lo + gk * lo
        acc_hi = gk * hi if k == 0 else acc_hi + gk * hi
    acc = jnp.concatenate([acc_lo, acc_hi], axis=1)
    out = x1_ref[...] + m[5:6] * _rms(acc, g_ref[...][3:4])
    if len(o_refs) == 1:
        o_refs[0][...] = out
    else:
        @pl.when(i < NPT)
        def _():
            o_refs[0][...] = out

        @pl.when(i >= NPT)
        def _():
            o_refs[1][...] = out


def _combine(yg, gates, x1, mod_l, g_l, split_out):
    if split_out:
        out_specs = [_PROMPT_SPEC, _SAMPLE_SPEC]
        out_shape = [jax.ShapeDtypeStruct((NP_TOK, D_MODEL), F32), jax.ShapeDtypeStruct((NS_TOK, D_MODEL), F32)]
    else:
        out_specs = [_tok_spec(D_MODEL)]
        out_shape = [jax.ShapeDtypeStruct((N_TOK, D_MODEL), F32)]
    return pl.pallas_call(
        _combine_kernel,
        grid=(N_TILES,),
        in_specs=[pl.BlockSpec((TOP_K, TM, ROW_WORDS), lambda i: (0, i, 0)), _tok_spec(LANES),
                  _tok_spec(D_MODEL), _MOD_SPEC, _const_spec((4, D_MODEL))],
        out_specs=out_specs,
        out_shape=out_shape,
        compiler_params=_cparams(1),
        name="combine",
    )(yg, gates, x1, mod_l, g_l)


def _ffn(layer, o_cat, xs, mod_l, g_l, w_out, w_router, b_router, w_gu, b_gu, w_down, b_down, split_out):
    w_r = jnp.pad(w_router, ((0, 0), (0, LANES - N_EXPERTS))).astype(BF16)
    b_r = jnp.pad(b_router, (0, LANES - N_EXPERTS)).reshape(1, LANES)
    x1, h2p, route, gate_slab, counts = _postmix(o_cat, xs, mod_l, g_l, w_out.astype(BF16), w_r, b_r)
    dest, blk_e, blk_first, blk_rows = _route(route, counts[0, :N_EXPERTS])
    n_chunks = N_TOK // (SC_WORKERS * SC_ROWS)
    idx_d = dest.reshape(SC_WORKERS, n_chunks, SC_ROWS, TOP_K).transpose(0, 1, 3, 2).reshape(
        SC_WORKERS, n_chunks * TOP_K, SC_ROWS)
    rows = _sc_dispatch(h2p, idx_d)
    ys = _moe(layer, blk_e, blk_first, blk_rows, rows, w_gu, b_gu, w_down, b_down)
    idx_c = dest.T.reshape(SC_WORKERS, TOP_K * n_chunks, SC_ROWS)
    yg = _sc_collect(ys, idx_c).reshape(TOP_K, N_TOK, ROW_WORDS)
    return _combine(yg, gate_slab, x1, mod_l, g_l, split_out)


def _pad_heads(w, n_heads, width, keep):
    k = w.shape[0]
    w = w.reshape(k, n_heads, width)[:, :, :keep]
    return jnp.pad(w, ((0, 0), (0, 0), (0, LANES - keep))).reshape(k, n_heads * LANES)


def _pe_slab(x):
    return jnp.pad(x, [(0, 0)] * (x.ndim - 1) + [(MLA_NOPE, LANES - MLA_NOPE - MLA_ROPE)])


_SWA_ORDER = np.array([0, 4, 1, 5, 2, 6, 3, 7])


def kernel(x_prompt, x_sample, cache_mla_ckv, cache_mla_krope, cache_diff_k, cache_diff_v, cache_swa_k, cache_swa_v, cache_na_k, cache_na_v, c, c_ctx, w_mod, b_mod, norm_g, w_in0, mla_q_norm, w_uq, mla_kv_norm, w_ukv, diff_lambda, diff_norm, w_out0, w_in1, swa_sink, na_rpb, w_out1, w_router, b_router, w_gu, b_gu, w_down, b_down):
    xs = (x_prompt.reshape(NP_TOK, D_MODEL), x_sample.reshape(NS_TOK, D_MODEL))
    cond = jnp.concatenate([c_ctx[None, :], c, jnp.zeros((16 - 1 - DEC_BATCH, D_MODEL), F32)], axis=0)
    mod = _modulation(cond, w_mod, b_mod).reshape(DEPTH, 16, 6, D_MODEL)
    t64, t32 = _rope_tables()
    states = {}
    for l in range(DEPTH):
        i = l // 2
        g_l = norm_g[l]
        mod_l = mod[l]
        if l % 2 == 0:
            lam_init = 0.8 - 0.6 * math.exp(-0.3 * l)
            wi = w_in0[i]
            w_in_p = jnp.concatenate(
                [wi[:, 0:640], wi[:, 672:2208], _pe_slab(wi[:, 640:672])], axis=1).astype(BF16)
            w_uq_p = _pad_heads(w_uq[i], MLA_HEADS, MLA_NOPE + MLA_ROPE, MLA_NOPE + MLA_ROPE).astype(BF16)
            w_k_p = _pad_heads(w_ukv[i], MLA_HEADS, MLA_NOPE + MLA_V, MLA_NOPE).astype(BF16)
            w_v = w_ukv[i].reshape(MLA_KV_RANK, MLA_HEADS, MLA_NOPE + MLA_V)[:, :, MLA_NOPE:].reshape(
                MLA_KV_RANK, MLA_HEADS * MLA_V).astype(BF16)
            (q, k, v, dq, dk, dv, ckv_st, kpe_st, dk_st, dv_st) = _premix0(
                *xs, mod_l, g_l[0:1], w_in_p, mla_q_norm[i][None, :], w_uq_p, mla_kv_norm[i][None, :],
                w_k_p, w_v, t32, t64)
            states['mla_ckv'] = ckv_st.reshape(BATCH, 1, SEQ, MLA_KV_RANK)
            states['mla_krope'] = kpe_st[:, MLA_NOPE:MLA_NOPE + MLA_ROPE].reshape(BATCH, 1, SEQ, MLA_ROPE)
            states['diff_k'] = dk_st.reshape(BATCH, 1, SEQ, DIFF_HEADS, 2 * DIFF_DH)
            states['diff_v'] = dv_st.reshape(BATCH, 1, SEQ, DIFF_HEADS, 2 * DIFF_DH)
            kc, vc = _mla_cache(cache_mla_ckv[:, i].reshape(DEC_BATCH * PAST_LEN, MLA_KV_RANK),
                                _pe_slab(cache_mla_krope[:, i].reshape(DEC_BATCH * PAST_LEN, MLA_ROPE)),
                                w_k_p, w_v)
            dkc = cache_diff_k[:, i].reshape(DEC_BATCH * PAST_LEN, 512).astype(BF16)
            dvc = cache_diff_v[:, i].reshape(DEC_BATCH * PAST_LEN, 512).astype(BF16)
            lam = diff_lambda[i]
            sub_g = diff_norm[i][None, :]
            o_p = _attn_ab(q, dq, (k, v, dk, dv), None, lam, sub_g, lam_init,
                           n_batch=BATCH, t_len=SEQ, tq=SEQ, tok_off=0)
            o_cat = _attn_ab(q, dq, (k, v, dk, dv), (kc, vc, dkc, dvc), lam, sub_g, lam_init,
                             n_batch=DEC_BATCH, t_len=DEC_SEQ, tq=TQ_AB, tok_off=NP_TOK, out_init=o_p)
            w_out = w_out0[i]
        else:
            wi = w_in1[i]
            sq_cols = (_SWA_ORDER[:, None] * HEAD_DIM + np.arange(HEAD_DIM)[None, :]).reshape(-1)
            w_in_p = jnp.concatenate([wi[:, sq_cols], wi[:, 512:]], axis=1).astype(BF16)
            (sq, sk, sv, nq, nk, nv, sk_st, sv_st, nk_st, nv_st) = _premix1(xs[0], mod_l, g_l[0:1], w_in_p, t64)
            states['swa_k'] = sk_st.reshape(BATCH, 1, SEQ, SWA_KV_HEADS, HEAD_DIM)
            states['swa_v'] = sv_st.reshape(BATCH, 1, SEQ, SWA_KV_HEADS, HEAD_DIM)
            states['na_k'] = nk_st.reshape(BATCH, 1, SEQ, NA_HEADS, HEAD_DIM)
            states['na_v'] = nv_st.reshape(BATCH, 1, SEQ, NA_HEADS, HEAD_DIM)
            skc = cache_swa_k[:, i].reshape(DEC_BATCH * PAST_LEN, 128).astype(BF16)
            svc = cache_swa_v[:, i].reshape(DEC_BATCH * PAST_LEN, 128).astype(BF16)
            nkc = cache_na_k[:, i].reshape(DEC_BATCH * PAST_LEN, 512).astype(BF16)
            nvc = cache_na_v[:, i].reshape(DEC_BATCH * PAST_LEN, 512).astype(BF16)
            sink = swa_sink[i]
            o_p = _attn_cd_prompt(sink, sq, sk, sv, nq, nk, nv)
            o_cat = _attn_cd_sample(o_p, sink, sq, sk, sv, nq, nk, nv, skc, svc, nkc, nvc, _na_bias(na_rpb[i]))
            wo = w_out1[i]
            w_out = jnp.concatenate([wo[sq_cols], wo[512:]], axis=0)
        xs = _ffn(l, o_cat, xs, mod_l, g_l, w_out, w_router[l], b_router[l], w_gu, b_gu, w_down, b_down,
                  split_out=(l == DEPTH - 1))
    return (xs[0].reshape(BATCH, SEQ, D_MODEL), xs[1].reshape(DEC_BATCH, DEC_SEQ, D_MODEL),
            states['mla_ckv'], states['mla_krope'], states['diff_k'], states['diff_v'],
            states['swa_k'], states['swa_v'], states['na_k'], states['na_v'])
```

```python
import functools
import math

import numpy as np
import jax
import jax.numpy as jnp
from jax import lax
from jax.experimental import pallas as pl
from jax.experimental.pallas import tpu as pltpu
from jax.experimental.pallas import tpu_sc as plsc

F32 = jnp.float32
BF16 = jnp.bfloat16

D_MODEL = 1024
BATCH = 16
SEQ = 256
DEPTH = 2
DEC_BATCH = 8
DEC_SEQ = 2048
PAST_LEN = 256
GRID_W = 64
HEAD_DIM = 64
ROPE_THETA = 10000.0
EPS = 1e-6
NEG = -1e30

MLA_HEADS = 8
MLA_Q_RANK = 384
MLA_KV_RANK = 256
MLA_NOPE = 64
MLA_ROPE = 32
MLA_V = 64
DIFF_HEADS = 4
DIFF_DH = 64
SWA_HEADS = 8
SWA_KV_HEADS = 2
SWA_WINDOW = 128
NA_HEADS = 8
NA_WIN_ROWS = 8
NA_WIN_COLS = 16
N_EXPERTS = 32
TOP_K = 4
D_EXPERT = 1024
SWIGLU_LIMIT = 7.0
SWIGLU_ALPHA = 1.702

LANES = 128
NP_TOK = BATCH * SEQ
NS_TOK = DEC_BATCH * DEC_SEQ
N_TOK = NP_TOK + NS_TOK
TM = 512
NPT = NP_TOK // TM
TILES_PER_SAMPLE = DEC_SEQ // TM
N_TILES = N_TOK // TM
TQ = 256
TQ_AB = 256
MOE_TM = 512
MOE_ROWS = ((N_TOK * TOP_K + N_EXPERTS * (MOE_TM - 1)) // MOE_TM + 1) * MOE_TM
MOE_BLOCKS = MOE_ROWS // MOE_TM
NA_TILE_ROWS = TQ // GRID_W
NA_KEY_ROWS = 12
VMEM_LIMIT = 56 * 1024 * 1024


def _cparams(n_axes, vmem=VMEM_LIMIT):
    return pltpu.CompilerParams(dimension_semantics=("arbitrary",) * n_axes,
                                vmem_limit_bytes=vmem)


def _rms(x, g):
    return x * lax.rsqrt(jnp.mean(x * x, axis=-1, keepdims=True) + EPS) * g


def _dot(a, b):
    return jnp.dot(a, b, preferred_element_type=F32)


def _dot_nt(a, b):
    return lax.dot_general(a, b, (((1,), (1,)), ((), ())), preferred_element_type=F32)


def _rope(x, cos, sin_a, sin_b, half):
    return (x * cos + pltpu.roll(x, LANES - half, 1) * sin_a + pltpu.roll(x, half, 1) * sin_b)


def _mod_row(i):
    return jnp.where(i < NPT, 0, 1 + (i - NPT) // TILES_PER_SAMPLE)


def _rope_blk(i):
    return jnp.where(i < NPT, TILES_PER_SAMPLE, (i - NPT) % TILES_PER_SAMPLE)


def _mod_kernel(c_ref, w_ref, b_ref, o_ref):
    c = c_ref[...]
    s = (c * jax.nn.sigmoid(c)).astype(BF16)
    o_ref[0] = _dot(s, w_ref[0].astype(BF16)) + b_ref[0]


def _modulation(cond, w_mod, b_mod):
    nb = 1024
    return pl.pallas_call(
        _mod_kernel,
        grid=(DEPTH, 6 * D_MODEL // nb),
        in_specs=[
            pl.BlockSpec((16, D_MODEL), lambda l, n: (0, 0)),
            pl.BlockSpec((1, D_MODEL, nb), lambda l, n: (l, 0, n)),
            pl.BlockSpec((1, 1, nb), lambda l, n: (l, 0, n)),
        ],
        out_specs=pl.BlockSpec((1, 16, nb), lambda l, n: (l, 0, n)),
        out_shape=jax.ShapeDtypeStruct((DEPTH, 16, 6 * D_MODEL), F32),
        compiler_params=_cparams(2),
        name="modulation",
    )(cond, w_mod, b_mod.reshape(DEPTH, 1, 6 * D_MODEL))


def _rope_tables():
    t = jnp.arange(DEC_SEQ)
    rows = (t // GRID_W).astype(F32)
    cols = (t % GRID_W).astype(F32)

    def angles(r):
        n = r // 4
        inv = ROPE_THETA ** (-jnp.arange(n, dtype=F32) / n)
        return jnp.concatenate([rows[:, None] * inv[None], cols[:, None] * inv[None]], axis=-1)

    def finish(cos, sa, sb):
        ident = (jnp.ones((TM, LANES), F32), jnp.zeros((TM, LANES), F32), jnp.zeros((TM, LANES), F32))
        return tuple(jnp.concatenate([a, b], axis=0) for a, b in zip((cos, sa, sb), ident))

    a64 = angles(64)
    c, s, z = jnp.cos(a64), jnp.sin(a64), jnp.zeros_like(a64)
    t64 = finish(jnp.concatenate([c, c, c, c], -1), jnp.concatenate([-s, z, -s, z], -1),
                 jnp.concatenate([z, s, z, s], -1))
    a32 = angles(32)
    c, s, z = jnp.cos(a32), jnp.sin(a32), jnp.zeros_like(a32)
    one64 = jnp.ones((DEC_SEQ, 64), F32)
    z64 = jnp.zeros((DEC_SEQ, 64), F32)
    z32 = jnp.zeros((DEC_SEQ, 32), F32)
    t32 = finish(jnp.concatenate([one64, c, c, z32], -1), jnp.concatenate([z64, -s, z, z32], -1),
                 jnp.concatenate([z64, z, s, z32], -1))
    return t64, t32


LOG2E = math.log2(math.e)
_MLA_SCALE = (MLA_NOPE + MLA_ROPE) ** -0.5 * LOG2E
_QSCALE = HEAD_DIM ** -0.5 * LOG2E


def _premix0_kernel(xp_ref, xs_ref, mod_ref, g_ref, win_ref, qn_ref, wuq_ref, kvn_ref, wk_ref, wv_ref,
                    c32_ref, sa32_ref, sb32_ref, c64_ref, sa64_ref, sb64_ref,
                    q_ref, k_ref, v_ref, dq_ref, dk_ref, dv_ref,
                    ckv_st, kpe_st, dk_st, dv_st):
    i = pl.program_id(0)
    m = mod_ref[0]
    h = _rms(_pick_x(i, xp_ref, xs_ref), g_ref[...]) * (1.0 + m[1:2]) + m[0:1]
    proj = _dot(h.astype(BF16), win_ref[...])
    q_a = proj[:, 0:384]
    kv_a = proj[:, 384:640]
    dq = proj[:, 640:1152]
    dk = proj[:, 1152:1664]
    dv = proj[:, 1664:2176]
    pe = proj[:, 2176:2304]
    q = _dot(_rms(q_a, qn_ref[...]).astype(BF16), wuq_ref[...])
    ckv = _rms(kv_a, kvn_ref[...])
    ckv_b = ckv.astype(BF16)
    kn = _dot(ckv_b, wk_ref[...])
    v_ref[...] = _dot(ckv_b, wv_ref[...]).astype(BF16)
    c32, sa32, sb32 = c32_ref[...], sa32_ref[...], sb32_ref[...]
    c64, sa64, sb64 = c64_ref[...], sa64_ref[...], sb64_ref[...]
    pe_r = _rope(pe, c32, sa32, sb32, MLA_ROPE // 2)
    for hd in range(MLA_HEADS):
        sl = slice(LANES * hd, LANES * (hd + 1))
        q_ref[:, sl] = (_rope(q[:, sl], c32, sa32, sb32, MLA_ROPE // 2) * _MLA_SCALE).astype(BF16)
        k_ref[:, sl] = (kn[:, sl] + pe_r).astype(BF16)
    for hd in range(DIFF_HEADS):
        sl = slice(LANES * hd, LANES * (hd + 1))
        dq_ref[:, sl] = (_rope(dq[:, sl], c64, sa64, sb64, DIFF_DH // 2) * _QSCALE).astype(BF16)
        dk_ref[:, sl] = _rope(dk[:, sl], c64, sa64, sb64, DIFF_DH // 2).astype(BF16)
    dv_ref[...] = dv.astype(BF16)

    @pl.when(i < NPT)
    def _():
        ckv_st[...] = ckv
        kpe_st[...] = pe
        dk_st[...] = dk
        dv_st[...] = dv


def _premix1_kernel(x_ref, mod_ref, g_ref, win_ref, c64_ref, sa64_ref, sb64_ref,
                    sq_ref, sk_ref, sv_ref, nq_ref, nk_ref, nv_ref,
                    sk_st, sv_st, nk_st, nv_st):
    i = pl.program_id(0)
    m = mod_ref[0]
    h = _rms(x_ref[...], g_ref[...]) * (1.0 + m[1:2]) + m[0:1]
    proj = _dot(h.astype(BF16), win_ref[...])
    sq = proj[:, 0:512]
    sk = proj[:, 512:640]
    sv = proj[:, 640:768]
    nq = proj[:, 768:1280]
    nk = proj[:, 1280:1792]
    nv = proj[:, 1792:2304]
    c64, sa64, sb64 = c64_ref[...], sa64_ref[...], sb64_ref[...]
    for hd in range(4):
        sl = slice(LANES * hd, LANES * (hd + 1))
        sq_ref[:, sl] = (_rope(sq[:, sl], c64, sa64, sb64, HEAD_DIM // 2) * _QSCALE).astype(BF16)
    sk_ref[...] = _rope(sk, c64, sa64, sb64, HEAD_DIM // 2).astype(BF16)
    sv_ref[...] = sv.astype(BF16)
    nq_ref[...] = (nq * _QSCALE).astype(BF16)
    nk_ref[...] = nk.astype(BF16)
    nv_ref[...] = nv.astype(BF16)

    @pl.when(i < NPT)
    def _():
        sk_st[...] = sk
        sv_st[...] = sv
        nk_st[...] = nk
        nv_st[...] = nv


def _tok_spec(width):
    return pl.BlockSpec((TM, width), lambda i: (i, 0))


_PROMPT_SPEC = pl.BlockSpec((TM, D_MODEL), lambda i: (jnp.minimum(i, NPT - 1), 0))
_SAMPLE_SPEC = pl.BlockSpec((TM, D_MODEL), lambda i: (jnp.maximum(i - NPT, 0), 0))


def _pick_x(i, xp_ref, xs_ref):
    return jnp.where(i < NPT, xp_ref[...], xs_ref[...])


def _state_spec(width):
    return pl.BlockSpec((TM, width), lambda i: (jnp.minimum(i, NPT - 1), 0))


def _const_spec(shape):
    return pl.BlockSpec(shape, lambda i: (0,) * len(shape))


_MOD_SPEC = pl.BlockSpec((1, 6, D_MODEL), lambda i: (_mod_row(i), 0, 0))
_ROPE_SPEC = pl.BlockSpec((TM, LANES), lambda i: (_rope_blk(i), 0))


def _premix0(xp, xs, mod_l, g0, w_in_p, q_norm, w_uq_p, kv_norm, w_k_p, w_v, t32, t64):
    outs = [(N_TOK, 1024, BF16), (N_TOK, 1024, BF16), (N_TOK, 512, BF16), (N_TOK, 512, BF16),
            (N_TOK, 512, BF16), (N_TOK, 512, BF16),
            (NP_TOK, 256, F32), (NP_TOK, 128, F32), (NP_TOK, 512, F32), (NP_TOK, 512, F32)]
    return pl.pallas_call(
        _premix0_kernel,
        grid=(N_TILES,),
        in_specs=[_PROMPT_SPEC, _SAMPLE_SPEC, _MOD_SPEC, _const_spec((1, D_MODEL)),
                  _const_spec((D_MODEL, 2304)), _const_spec((1, MLA_Q_RANK)),
                  _const_spec((MLA_Q_RANK, 1024)), _const_spec((1, MLA_KV_RANK)),
                  _const_spec((MLA_KV_RANK, 1024)), _const_spec((MLA_KV_RANK, 512))]
                 + [_ROPE_SPEC] * 6,
        out_specs=[_tok_spec(w) for (_, w, _) in outs[:6]] + [_state_spec(w) for (_, w, _) in outs[6:]],
        out_shape=[jax.ShapeDtypeStruct((n, w), dt) for (n, w, dt) in outs],
        compiler_params=_cparams(1),
        name="premix_ab",
    )(xp, xs, mod_l, g0, w_in_p, q_norm, w_uq_p, kv_norm, w_k_p, w_v, *t32, *t64)


def _premix1(x, mod_l, g0, w_in_p, t64):
    outs = [(N_TOK, 512, BF16), (N_TOK, 128, BF16), (N_TOK, 128, BF16), (N_TOK, 512, BF16),
            (N_TOK, 512, BF16), (N_TOK, 512, BF16),
            (NP_TOK, 128, F32), (NP_TOK, 128, F32), (NP_TOK, 512, F32), (NP_TOK, 512, F32)]
    return pl.pallas_call(
        _premix1_kernel,
        grid=(N_TILES,),
        in_specs=[_tok_spec(D_MODEL), _MOD_SPEC, _const_spec((1, D_MODEL)),
                  _const_spec((D_MODEL, 2304))] + [_ROPE_SPEC] * 3,
        out_specs=[_tok_spec(w) for (_, w, _) in outs[:6]] + [_state_spec(w) for (_, w, _) in outs[6:]],
        out_shape=[jax.ShapeDtypeStruct((n, w), dt) for (n, w, dt) in outs],
        compiler_params=_cparams(1),
        name="premix_cd",
    )(x, mod_l, g0, w_in_p, *t64)


def _mla_cache_kernel(ckv_ref, pe_ref, wk_ref, wv_ref, k_ref, v_ref):
    c = ckv_ref[...].astype(BF16)
    kn = _dot(c, wk_ref[...])
    v_ref[...] = _dot(c, wv_ref[...]).astype(BF16)
    pe = pe_ref[...]
    for hd in range(MLA_HEADS):
        sl = slice(LANES * hd, LANES * (hd + 1))
        k_ref[:, sl] = (kn[:, sl] + pe).astype(BF16)


def _mla_cache(ckv, pe_slab, w_k_p, w_v):
    n = ckv.shape[0]
    tm = 512
    return pl.pallas_call(
        _mla_cache_kernel,
        grid=(n // tm,),
        in_specs=[pl.BlockSpec((tm, MLA_KV_RANK), lambda i: (i, 0)),
                  pl.BlockSpec((tm, LANES), lambda i: (i, 0)),
                  _const_spec((MLA_KV_RANK, 1024)), _const_spec((MLA_KV_RANK, 512))],
        out_specs=[pl.BlockSpec((tm, 1024), lambda i: (i, 0)), pl.BlockSpec((tm, 512), lambda i: (i, 0))],
        out_shape=[jax.ShapeDtypeStruct((n, 1024), BF16), jax.ShapeDtypeStruct((n, 512), BF16)],
        compiler_params=_cparams(1),
        name="mla_cache",
    )(ckv, pe_slab, w_k_p, w_v)


def _softmax_pv(scores, values, sink=None):
    m = jnp.max(scores[0], axis=-1, keepdims=True)
    for s in scores[1:]:
        m = jnp.maximum(m, jnp.max(s, axis=-1, keepdims=True))
    if sink is not None:
        m = jnp.maximum(m, sink)
    l = None
    o = None
    for s, v in zip(scores, values):
        p = jnp.exp2(s - m)
        ls = jnp.sum(p, axis=-1, keepdims=True)
        os_ = _dot(p.astype(BF16), v)
        l = ls if l is None else l + ls
        o = os_ if o is None else o + os_
    if sink is not None:
        l = l + jnp.exp2(sink - m)
    return o * (1.0 / l)


def _lane_lo(shape):
    return lax.broadcasted_iota(jnp.int32, shape, 1) < (LANES // 2)


def _split_halves(qb):
    lo = _lane_lo(qb.shape)
    zero = jnp.zeros_like(qb)
    return jnp.where(lo, qb, zero), jnp.where(lo, zero, qb)


def _attn_ab_kernel(*refs, n_pieces, lam_init, aliased):
    if aliased:
        refs = refs[1:]
    q_ref, dq_ref = refs[0], refs[1]
    pieces = [refs[2 + 4 * p: 6 + 4 * p] for p in range(n_pieces)]
    lam_ref, subg_ref, o_ref = refs[2 + 4 * n_pieces:]
    lam = lam_ref[...]
    lam_full = (jnp.exp(jnp.sum(lam[0:1] * lam[1:2], axis=-1, keepdims=True))
                - jnp.exp(jnp.sum(lam[2:3] * lam[3:4], axis=-1, keepdims=True)) + lam_init)
    lo = _lane_lo((q_ref.shape[0], LANES))
    for j in range(MLA_HEADS // 2):
        pair = []
        for half in range(2):
            hd = 2 * j + half
            sl = slice(LANES * hd, LANES * (hd + 1))
            qh = q_ref[:, sl]
            scores = [_dot_nt(qh, k_ref[:, sl]) for (k_ref, _, _, _) in pieces]
            vals = [v_ref[:, LANES * j: LANES * (j + 1)] for (_, v_ref, _, _) in pieces]
            pair.append(_softmax_pv(scores, vals))
        o_ref[:, LANES * j: LANES * (j + 1)] = jnp.where(lo, pair[0], pair[1]).astype(BF16)
    subg = subg_ref[...]
    tq = q_ref.shape[0]
    for hd in range(DIFF_HEADS):
        sl = slice(LANES * hd, LANES * (hd + 1))
        qq = jnp.concatenate(_split_halves(dq_ref[:, sl]), axis=0)
        ks = [dk_ref[:, sl] for (_, _, dk_ref, _) in pieces]
        vs = [dv_ref[:, sl] for (_, _, _, dv_ref) in pieces]
        oo = _softmax_pv([_dot_nt(qq, k) for k in ks], vs)
        od = _rms(oo[:tq] - lam_full * oo[tq:], subg) * (1.0 - lam_init)
        o_ref[:, 512 + LANES * hd: 512 + LANES * (hd + 1)] = od.astype(BF16)


def _attn_ab(q, dq, new_kv, cache_kv, lam, sub_g, lam_init, *, n_batch, t_len, tq, tok_off, out_init=None):
    nq = t_len // tq
    q_off = tok_off // tq
    b_off = tok_off // t_len
    widths = (1024, 512, 512, 512)
    in_specs = [pl.BlockSpec((tq, 1024), lambda b, i: (q_off + b * nq + i, 0)),
                pl.BlockSpec((tq, 512), lambda b, i: (q_off + b * nq + i, 0))]
    args = [q, dq]
    for w, a in zip(widths, new_kv):
        in_specs.append(pl.BlockSpec((t_len, w), lambda b, i: (b_off + b, 0)))
        args.append(a)
    n_pieces = 1
    if cache_kv is not None:
        n_pieces = 2
        for w, a in zip(widths, cache_kv):
            in_specs.append(pl.BlockSpec((PAST_LEN, w), lambda b, i: (b, 0)))
            args.append(a)
    in_specs += [pl.BlockSpec((4, DIFF_DH), lambda b, i: (0, 0)),
                 pl.BlockSpec((1, 2 * DIFF_DH), lambda b, i: (0, 0))]
    args += [lam, sub_g]
    aliases = {}
    if out_init is not None:
        in_specs = [pl.BlockSpec(memory_space=pl.ANY)] + in_specs
        args = [out_init] + args
        aliases = {0: 0}
    return pl.pallas_call(
        functools.partial(_attn_ab_kernel, n_pieces=n_pieces, lam_init=lam_init, aliased=out_init is not None),
        grid=(n_batch, nq),
        in_specs=in_specs,
        out_specs=pl.BlockSpec((tq, 1024), lambda b, i: (q_off + b * nq + i, 0)),
        out_shape=jax.ShapeDtypeStruct((N_TOK, 1024), BF16),
        input_output_aliases=aliases,
        compiler_params=_cparams(2),
        name="attn_ab_%d" % n_pieces,
    )(*args)


def _gqa_stacks(sq_ref, sink_ref):
    tq = sq_ref.shape[0]
    halves = [_split_halves(sq_ref[:, LANES * j: LANES * (j + 1)]) for j in range(4)]
    q_stacks = [jnp.concatenate([halves[j][kvh] for j in range(4)], axis=0) for kvh in range(SWA_KV_HEADS)]
    sinks = [jnp.concatenate([jnp.full((tq, 1), sink_ref[4 * kvh + j] * LOG2E, F32) for j in range(4)], axis=0)
             for kvh in range(SWA_KV_HEADS)]
    return q_stacks, sinks


def _attn_cd_prompt_kernel(sink_ref, sq_ref, sk_ref, sv_ref, nq_ref, nk_ref, nv_ref, o_ref):
    tq = sq_ref.shape[0]
    lo = _lane_lo((tq, LANES))
    sk = sk_ref[...]
    sv = sv_ref[...]
    q_stacks, sinks = _gqa_stacks(sq_ref, sink_ref)
    o_kv = [_softmax_pv([_dot_nt(q, sk)], [sv], sink=s) for q, s in zip(q_stacks, sinks)]
    for j in range(4):
        rows = slice(tq * j, tq * (j + 1))
        o_ref[:, LANES * j: LANES * (j + 1)] = jnp.where(lo, o_kv[0][rows], o_kv[1][rows]).astype(BF16)
    for j in range(4):
        sl = slice(LANES * j, LANES * (j + 1))
        q_lo, q_hi = _split_halves(nq_ref[:, sl])
        k = nk_ref[:, sl]
        v = nv_ref[:, sl]
        o_lo = _softmax_pv([_dot_nt(q_lo, k)], [v])
        o_hi = _softmax_pv([_dot_nt(q_hi, k)], [v])
        o_ref[:, 512 + LANES * j: 512 + LANES * (j + 1)] = jnp.where(lo, o_lo, o_hi).astype(BF16)


def _attn_cd_prompt(sink, sq, sk, sv, nq, nk, nv):
    def spec(w):
        return pl.BlockSpec((SEQ, w), lambda b: (b, 0))
    return pl.pallas_call(
        _attn_cd_prompt_kernel,
        grid=(BATCH,),
        in_specs=[pl.BlockSpec(memory_space=pltpu.SMEM), spec(512), spec(128), spec(128),
                  spec(512), spec(512), spec(512)],
        out_specs=spec(1024),
        out_shape=jax.ShapeDtypeStruct((N_TOK, 1024), BF16),
        compiler_params=_cparams(1),
        name="attn_cd_prompt",
    )(sink, sq, sk, sv, nq, nk, nv)


_SWA_KEYS = TQ + 2 * SWA_WINDOW


def _attn_cd_sample_kernel(init_ref, sink_ref, sq_ref, nq_ref, sk_ref, sv_ref, nk_ref, nv_ref,
                           skc_ref, svc_ref, nkc_ref, nvc_ref, bias_ref, o_ref):
    del init_ref
    qi = pl.program_id(1)
    lo = _lane_lo((TQ, LANES))
    ks = pl.multiple_of(jnp.clip(qi * TQ - SWA_WINDOW, 0, DEC_SEQ - _SWA_KEYS), SWA_WINDOW)
    k_win = sk_ref[pl.ds(ks, _SWA_KEYS), :]
    v_win = sv_ref[pl.ds(ks, _SWA_KEYS), :]
    q_pos = qi * TQ + (lax.broadcasted_iota(jnp.int32, (4 * TQ, _SWA_KEYS), 0) & (TQ - 1))
    k_pos = ks + lax.broadcasted_iota(jnp.int32, (4 * TQ, _SWA_KEYS), 1)
    in_win = jnp.abs(q_pos - k_pos) <= SWA_WINDOW
    skc = skc_ref[...]
    svc = svc_ref[...]
    q_stacks, sinks = _gqa_stacks(sq_ref, sink_ref)
    o_kv = []
    for q, s in zip(q_stacks, sinks):
        s_lat = jnp.where(in_win, _dot_nt(q, k_win), NEG)
        o_kv.append(_softmax_pv([_dot_nt(q, skc), s_lat], [svc, v_win], sink=s))
    for j in range(4):
        rows = slice(TQ * j, TQ * (j + 1))
        o_ref[:, LANES * j: LANES * (j + 1)] = jnp.where(lo, o_kv[0][rows], o_kv[1][rows]).astype(BF16)
    n_rows = DEC_SEQ // GRID_W
    r0 = jnp.clip(qi * NA_TILE_ROWS - NA_WIN_ROWS // 2, 0, n_rows - NA_KEY_ROWS)
    kn = pl.multiple_of(r0 * GRID_W, GRID_W)
    for j in range(4):
        sl = slice(LANES * j, LANES * (j + 1))
        k_win = nk_ref[pl.ds(kn, NA_KEY_ROWS * GRID_W), sl]
        v_win = nv_ref[pl.ds(kn, NA_KEY_ROWS * GRID_W), sl]
        kc = nkc_ref[:, sl]
        vc = nvc_ref[:, sl]
        outs = []
        for q_half, hd in zip(_split_halves(nq_ref[:, sl]), (2 * j, 2 * j + 1)):
            s_lat = _dot_nt(q_half, k_win) + bias_ref[0, hd]
            outs.append(_softmax_pv([_dot_nt(q_half, kc), s_lat], [vc, v_win]))
        o_ref[:, 512 + LANES * j: 512 + LANES * (j + 1)] = jnp.where(lo, outs[0], outs[1]).astype(BF16)


def _attn_cd_sample(out_init, sink, sq, sk, sv, nq, nk, nv, skc, svc, nkc, nvc, bias):
    nq_t = DEC_SEQ // TQ
    q_off = NP_TOK // TQ
    b_off = NP_TOK // DEC_SEQ

    def qspec(w):
        return pl.BlockSpec((TQ, w), lambda b, i: (q_off + b * nq_t + i, 0))

    def kspec(w):
        return pl.BlockSpec((DEC_SEQ, w), lambda b, i: (b_off + b, 0))

    def cspec(w):
        return pl.BlockSpec((PAST_LEN, w), lambda b, i: (b, 0))

    n_keys = NA_KEY_ROWS * GRID_W
    bias_spec = pl.BlockSpec(
        (1, NA_HEADS, TQ, n_keys),
        lambda b, i: (jnp.where(i == 0, 0, jnp.where(i == nq_t - 1, 2, 1)), 0, 0, 0))
    return pl.pallas_call(
        _attn_cd_sample_kernel,
        grid=(DEC_BATCH, nq_t),
        in_specs=[pl.BlockSpec(memory_space=pl.ANY), pl.BlockSpec(memory_space=pltpu.SMEM),
                  qspec(512), qspec(512), kspec(128), kspec(128), kspec(512), kspec(512),
                  cspec(128), cspec(128), cspec(512), cspec(512), bias_spec],
        out_specs=pl.BlockSpec((TQ, 1024), lambda b, i: (q_off + b * nq_t + i, 0)),
        out_shape=jax.ShapeDtypeStruct((N_TOK, 1024), BF16),
        input_output_aliases={0: 0},
        compiler_params=_cparams(2),
        name="attn_cd_sample",
    )(out_init, sink, sq, nq, sk, sv, nk, nv, skc, svc, nkc, nvc, bias)


def _na_bias(rpb):
    n_rows = DEC_SEQ // GRID_W
    n_dr = 2 * NA_WIN_ROWS - 1
    n_dc = 2 * NA_WIN_COLS - 1
    c = np.arange(GRID_W)[:, None]
    kc = np.arange(GRID_W)[None, :]
    qs = np.clip(c - NA_WIN_COLS // 2, 0, GRID_W - NA_WIN_COLS)
    col_ok = (kc >= qs) & (kc < qs + NA_WIN_COLS)
    dc = np.clip(kc - c + NA_WIN_COLS - 1, 0, n_dc - 1)
    onehot = ((dc[None] == np.arange(n_dc)[:, None, None]) & col_ok[None]).astype(np.float32)
    blocks = jnp.einsum('hrd,dck->hrck', rpb.astype(F32) * LOG2E, onehot, precision=lax.Precision.HIGHEST)
    row_sel = np.zeros((3, n_dr, NA_TILE_ROWS, NA_KEY_ROWS), np.float32)
    for v, q_row0 in enumerate((0, NA_TILE_ROWS, n_rows - NA_TILE_ROWS)):
        k_row0 = min(max(q_row0 - NA_WIN_ROWS // 2, 0), n_rows - NA_KEY_ROWS)
        for ri in range(NA_TILE_ROWS):
            r = q_row0 + ri
            rs = min(max(r - NA_WIN_ROWS // 2, 0), n_rows - NA_WIN_ROWS)
            for kj in range(NA_KEY_ROWS):
                kr = k_row0 + kj
                if rs <= kr < rs + NA_WIN_ROWS:
                    row_sel[v, kr - r + NA_WIN_ROWS - 1, ri, kj] = 1.0
    tiles = jnp.einsum('vdrj,hdck->vhrcjk', row_sel, blocks, precision=lax.Precision.HIGHEST)
    valid = (row_sel.sum(axis=1) > 0)[:, :, None, :, None] & col_ok[None, None, :, None, :]
    tiles = jnp.where(valid[:, None], tiles, NEG)
    return tiles.reshape(3, NA_HEADS, TQ, NA_KEY_ROWS * GRID_W)


_HI_MASK = -65536


def _pack_pairs(x):
    w = x.shape[1] // 2
    r = x.astype(BF16).astype(F32)
    lo = lax.bitcast_convert_type(r[:, :w], jnp.int32)
    hi = lax.bitcast_convert_type(r[:, w:], jnp.int32)
    return (hi & _HI_MASK) | lax.shift_right_logical(lo, 16)


def _unpack_pairs(p):
    lo = lax.bitcast_convert_type(lax.shift_left(p, 16), F32)
    hi = lax.bitcast_convert_type(p & _HI_MASK, F32)
    return lo, hi


def _postmix_kernel(*refs, split_x):
    if split_x:
        o_ref, xp_ref, xs_ref = refs[:3]
        refs = refs[3:]
    else:
        o_ref, x_ref = refs[:2]
        refs = refs[2:]
    (mod_ref, g_ref, wout_ref, wr_ref, br_ref, tri_ref,
     x1_ref, h2_ref, route_ref, gate_ref, cnt_ref, run_ref) = refs
    i = pl.program_id(0)
    x = _pick_x(i, xp_ref, xs_ref) if split_x else x_ref[...]

    @pl.when(i == 0)
    def _():
        run_ref[...] = jnp.zeros_like(run_ref)

    m = mod_ref[0]
    g = g_ref[...]
    y = _dot(o_ref[...], wout_ref[...])
    x1 = x + m[2:3] * _rms(y, g[1:2])
    x1_ref[...] = x1
    h2 = _rms(x1, g[2:3]) * (1.0 + m[4:5]) + m[3:4]
    h2_ref[...] = _pack_pairs(h2)
    logits = _dot(h2.astype(BF16), wr_ref[...]) + br_ref[...]
    lane = lax.broadcasted_iota(jnp.int32, logits.shape, 1).astype(F32)
    cur = jnp.where(lane < N_EXPERTS, logits, -jnp.inf)
    tops, idxs = [], []
    for _ in range(TOP_K):
        mx = jnp.max(cur, axis=-1, keepdims=True)
        ix = jnp.min(jnp.where(cur == mx, lane, float(LANES)), axis=-1, keepdims=True)
        tops.append(mx)
        idxs.append(ix)
        cur = jnp.where(lane == ix, -jnp.inf, cur)
    es = [jnp.exp(t - tops[0]) for t in tops]
    inv = 1.0 / (es[0] + es[1] + es[2] + es[3])
    picked = jnp.zeros_like(logits)
    for k in range(TOP_K):
        picked = jnp.where(lane == idxs[k], 1.0, picked)
    before = _dot(tri_ref[...], picked.astype(BF16)) + run_ref[0:1, :]
    route = jnp.zeros_like(logits)
    gate_out = jnp.zeros_like(logits)
    for k in range(TOP_K):
        rank = jnp.sum(jnp.where(lane == idxs[k], before, 0.0), axis=-1, keepdims=True)
        route = jnp.where(lane == float(k), idxs[k], route)
        route = jnp.where(lane == float(TOP_K + k), rank, route)
        gate_out = jnp.where(lane == float(k), es[k] * inv, gate_out)
    route_ref[...] = route.astype(jnp.int32)
    gate_ref[...] = gate_out
    run_ref[...] = run_ref[...] + jnp.sum(picked, axis=0, keepdims=True)
    cnt_ref[...] = run_ref[...].astype(jnp.int32)


def _postmix(o_cat, xs, mod_l, g_l, w_out, w_r, b_r):
    tri = jnp.asarray(np.tril(np.ones((TM, TM), np.float32), -1), BF16)
    split_x = len(xs) == 2
    x_specs = [_PROMPT_SPEC, _SAMPLE_SPEC] if split_x else [_tok_spec(D_MODEL)]
    return pl.pallas_call(
        functools.partial(_postmix_kernel, split_x=split_x),
        grid=(N_TILES,),
        in_specs=[_tok_spec(1024)] + x_specs + [_MOD_SPEC, _const_spec((4, D_MODEL)),
                  _const_spec((1024, D_MODEL)), _const_spec((D_MODEL, LANES)), _const_spec((1, LANES)),
                  _const_spec((TM, TM))],
        out_specs=[_tok_spec(D_MODEL), _tok_spec(D_MODEL // 2), _tok_spec(LANES), _tok_spec(LANES),
                   _const_spec((8, LANES))],
        out_shape=[jax.ShapeDtypeStruct((N_TOK, D_MODEL), F32),
                   jax.ShapeDtypeStruct((N_TOK, D_MODEL // 2), jnp.int32),
                   jax.ShapeDtypeStruct((N_TOK, LANES), jnp.int32), jax.ShapeDtypeStruct((N_TOK, LANES), F32),
                   jax.ShapeDtypeStruct((8, LANES), jnp.int32)],
        scratch_shapes=[pltpu.VMEM((8, LANES), F32)],
        compiler_params=_cparams(1),
        name="postmix",
    )(o_cat, *xs, mod_l, g_l, w_out, w_r, b_r, tri)


SC_WORKERS = 32
SC_ROWS = 128
ROW_WORDS = D_MODEL // 2


def _sc_worker_id():
    return lax.axis_index("s") * 2 + lax.axis_index("c")


def _sc_dispatch(src, idx):
    n_chunks = N_TOK // (SC_WORKERS * SC_ROWS)
    mesh = plsc.VectorSubcoreMesh(core_axis_name="c", subcore_axis_name="s")

    @functools.partial(
        pl.kernel, mesh=mesh,
        out_type=jax.ShapeDtypeStruct((MOE_ROWS, ROW_WORDS), jnp.int32),
        scratch_types=[pltpu.VMEM((n_chunks * TOP_K, SC_ROWS), jnp.int32),
                       pltpu.VMEM((SC_ROWS, ROW_WORDS), jnp.int32), pltpu.SemaphoreType.DMA])
    def k(src_hbm, idx_hbm, out_hbm, idx_v, rows_v, sem):
        wid = _sc_worker_id()
        pltpu.sync_copy(idx_hbm.at[wid], idx_v)

        @pl.loop(0, n_chunks)
        def _(g):
            pltpu.sync_copy(src_hbm.at[pl.ds((wid * n_chunks + g) * SC_ROWS, SC_ROWS)], rows_v)
            copies = [pltpu.async_copy(rows_v, out_hbm.at[idx_v.at[g * TOP_K + kk]], sem)
                      for kk in range(TOP_K)]
            for cp in copies:
                cp.wait()

    return k(src, idx)


def _sc_collect(table, idx):
    n_chunks = idx.shape[1]
    mesh = plsc.VectorSubcoreMesh(core_axis_name="c", subcore_axis_name="s")

    @functools.partial(
        pl.kernel, mesh=mesh,
        out_type=jax.ShapeDtypeStruct((SC_WORKERS * n_chunks * SC_ROWS, ROW_WORDS), jnp.int32),
        scratch_types=[pltpu.VMEM((n_chunks, SC_ROWS), jnp.int32),
                       pltpu.VMEM((SC_ROWS, ROW_WORDS), jnp.int32), pltpu.SemaphoreType.DMA])
    def k(table_hbm, idx_hbm, out_hbm, idx_v, rows_v, sem):
        wid = _sc_worker_id()
        pltpu.sync_copy(idx_hbm.at[wid], idx_v)

        @pl.loop(0, n_chunks)
        def _(g):
            pltpu.async_copy(table_hbm.at[idx_v.at[g]], rows_v, sem).wait()
            pltpu.sync_copy(rows_v, out_hbm.at[pl.ds((wid * n_chunks + g) * SC_ROWS, SC_ROWS)])

    return k(table, idx)


def _expert_rows(words, n_valid, wgu_b, wd_b, bgu, bd):
    live = lax.broadcasted_iota(jnp.int32, words.shape, 0) < n_valid
    lo, hi = _unpack_pairs(jnp.where(live, words, 0))
    x = jnp.concatenate([lo, hi], axis=1).astype(BF16)
    gu = _dot(x, wgu_b[...]) + bgu
    g = jnp.minimum(gu[:, :D_EXPERT], SWIGLU_LIMIT)
    u = jnp.clip(gu[:, D_EXPERT:], -SWIGLU_LIMIT, SWIGLU_LIMIT)
    a = g * jax.nn.sigmoid(SWIGLU_ALPHA * g) * (u + 1.0)
    return _pack_pairs(_dot(a.astype(BF16), wd_b[...]) + bd)


def _moe_kernel(blk_e_ref, blk_first_ref, blk_rows_ref, blk_slot_ref, blk_next_ref,
                x_ref, wgu_hbm, bgu_ref, wd_hbm, bd_ref, y_ref,
                wgu_f, wd_f, wgu_b, wd_b, sem, *, layer):
    i = pl.program_id(0)
    n_valid = blk_rows_ref[i]
    half = MOE_TM // 2

    def weight_copies(e, slot):
        return (pltpu.make_async_copy(wgu_hbm.at[layer, e], wgu_f.at[slot], sem.at[0, slot]),
                pltpu.make_async_copy(wd_hbm.at[layer, e], wd_f.at[slot], sem.at[1, slot]))

    @pl.when(i == 0)
    def _():
        for cp in weight_copies(blk_e_ref[0], blk_slot_ref[0]):
            cp.start()

    @pl.when(blk_first_ref[i] == 1)
    def _():
        slot = blk_slot_ref[i]
        for cp in weight_copies(blk_e_ref[i], slot):
            cp.wait()
        nxt = blk_next_ref[i]

        @pl.when(nxt >= 0)
        def _():
            for cp in weight_copies(nxt, 1 - slot):
                cp.start()

        wgu_b[...] = wgu_f[slot].astype(BF16)
        wd_b[...] = wd_f[slot].astype(BF16)

    @pl.when(n_valid > half)
    def _():
        y_ref[...] = _expert_rows(x_ref[...], n_valid, wgu_b, wd_b, bgu_ref[0, 0], bd_ref[0, 0])

    @pl.when((n_valid > 0) & (n_valid <= half))
    def _():
        y_ref[:half] = _expert_rows(x_ref[:half], n_valid, wgu_b, wd_b, bgu_ref[0, 0], bd_ref[0, 0])
        y_ref[half:] = jnp.zeros((half, ROW_WORDS), jnp.int32)

    @pl.when(n_valid == 0)
    def _():
        y_ref[...] = jnp.zeros_like(y_ref)


def _moe(layer, blk_meta, xs, w_gu, b_gu, w_down, b_down):
    def row_map(i, *_):
        return (i, 0)

    def bias_map(i, e, *_):
        return (layer, e[i], 0, 0)

    grid_spec = pltpu.PrefetchScalarGridSpec(
        num_scalar_prefetch=5,
        grid=(MOE_BLOCKS,),
        in_specs=[
            pl.BlockSpec((MOE_TM, ROW_WORDS), row_map),
            pl.BlockSpec(memory_space=pl.ANY),
            pl.BlockSpec((1, 1, 1, 2 * D_EXPERT), bias_map),
            pl.BlockSpec(memory_space=pl.ANY),
            pl.BlockSpec((1, 1, 1, D_MODEL), bias_map),
        ],
        out_specs=pl.BlockSpec((MOE_TM, ROW_WORDS), row_map),
        scratch_shapes=[pltpu.VMEM((2, D_MODEL, 2 * D_EXPERT), F32), pltpu.VMEM((2, D_EXPERT, D_MODEL), F32),
                        pltpu.VMEM((D_MODEL, 2 * D_EXPERT), BF16), pltpu.VMEM((D_EXPERT, D_MODEL), BF16),
                        pltpu.SemaphoreType.DMA((2, 2))],
    )
    return pl.pallas_call(
        functools.partial(_moe_kernel, layer=layer),
        grid_spec=grid_spec,
        out_shape=jax.ShapeDtypeStruct((MOE_ROWS, ROW_WORDS), jnp.int32),
        compiler_params=_cparams(1),
        name="moe_experts",
    )(*blk_meta, xs, w_gu, b_gu.reshape(DEPTH, N_EXPERTS, 1, 2 * D_EXPERT),
      w_down, b_down.reshape(DEPTH, N_EXPERTS, 1, D_MODEL))


def _route(route, counts):
    experts = jnp.arange(N_EXPERTS, dtype=jnp.int32)
    padded = (counts + MOE_TM - 1) // MOE_TM * MOE_TM
    pad_end = jnp.cumsum(padded)
    pad_start = pad_end - padded
    e = route[:, 0:TOP_K]
    onehot = e[:, :, None] == experts[None, None, :]
    dest = jnp.sum(jnp.where(onehot, pad_start[None, None, :], 0), axis=-1) + route[:, TOP_K:2 * TOP_K]
    blk_row0 = jnp.arange(MOE_BLOCKS, dtype=jnp.int32) * MOE_TM
    blk_e = jnp.minimum(jnp.sum((pad_end[None, :] <= blk_row0[:, None]).astype(jnp.int32), axis=1),
                        N_EXPERTS - 1)
    on = blk_row0 < pad_end[-1]
    n_on = jnp.sum(on.astype(jnp.int32))
    blk_onehot = blk_e[:, None] == experts[None, :]
    row_end = jnp.sum(jnp.where(blk_onehot, (pad_start + counts)[None, :], 0), axis=1)
    blk_rows = jnp.where(on, jnp.clip(row_end - blk_row0, 0, MOE_TM), 0).astype(jnp.int32)
    last_e = jnp.sum(jnp.where(jnp.arange(MOE_BLOCKS) == n_on - 1, blk_e, 0))
    blk_e = jnp.where(on, blk_e, last_e).astype(jnp.int32)
    prev = jnp.concatenate([jnp.full((1,), -1, jnp.int32), blk_e[:-1]])
    blk_first = (blk_e != prev).astype(jnp.int32)
    blk_slot = ((jnp.cumsum(blk_first) - 1) % 2).astype(jnp.int32)
    later_used = (experts[None, :] > experts[:, None]) & (counts[None, :] > 0)
    next_used = jnp.min(jnp.where(later_used, experts[None, :], N_EXPERTS), axis=1)
    next_used = jnp.where(next_used == N_EXPERTS, -1, next_used)
    blk_next = jnp.sum(jnp.where(blk_e[:, None] == experts[None, :], next_used[None, :], 0), axis=1)
    return dest, (blk_e, blk_first, blk_rows, blk_slot, blk_next.astype(jnp.int32))


def _combine_kernel(y_ref, gate_ref, x1_ref, mod_ref, g_ref, *o_refs):
    i = pl.program_id(0)
    m = mod_ref[0]
    gate = gate_ref[...]
    acc_lo = None
    for k in range(TOP_K):
        lo, hi = _unpack_pairs(y_ref[k])
        gk = gate[:, k:k + 1]
        acc_lo = gk * lo if acc_lo is None else acc_lo + gk * lo
        acc_hi = gk * hi if k == 0 else acc_hi + gk * hi
    acc = jnp.concatenate([acc_lo, acc_hi], axis=1)
    out = x1_ref[...] + m[5:6] * _rms(acc, g_ref[...][3:4])
    if len(o_refs) == 1:
        o_refs[0][...] = out
    else:
        @pl.when(i < NPT)
        def _():
            o_refs[0][...] = out

        @pl.when(i >= NPT)
        def _():
            o_refs[1][...] = out


def _combine(yg, gates, x1, mod_l, g_l, split_out):
    if split_out:
        out_specs = [_PROMPT_SPEC, _SAMPLE_SPEC]
        out_shape = [jax.ShapeDtypeStruct((NP_TOK, D_MODEL), F32), jax.ShapeDtypeStruct((NS_TOK, D_MODEL), F32)]
    else:
        out_specs = [_tok_spec(D_MODEL)]
        out_shape = [jax.ShapeDtypeStruct((N_TOK, D_MODEL), F32)]
    return pl.pallas_call(
        _combine_kernel,
        grid=(N_TILES,),
        in_specs=[pl.BlockSpec((TOP_K, TM, ROW_WORDS), lambda i: (0, i, 0)), _tok_spec(LANES),
                  _tok_spec(D_MODEL), _MOD_SPEC, _const_spec((4, D_MODEL))],
        out_specs=out_specs,
        out_shape=out_shape,
        compiler_params=_cparams(1),
        name="combine",
    )(yg, gates, x1, mod_l, g_l)


def _ffn(layer, o_cat, xs, mod_l, g_l, w_out, w_router, b_router, w_gu, b_gu, w_down, b_down, split_out):
    w_r = jnp.pad(w_router, ((0, 0), (0, LANES - N_EXPERTS))).astype(BF16)
    b_r = jnp.pad(b_router, (0, LANES - N_EXPERTS)).reshape(1, LANES)
    x1, h2p, route, gate_slab, counts = _postmix(o_cat, xs, mod_l, g_l, w_out.astype(BF16), w_r, b_r)
    dest, blk_meta = _route(route, counts[0, :N_EXPERTS])
    n_chunks = N_TOK // (SC_WORKERS * SC_ROWS)
    idx_d = dest.reshape(SC_WORKERS, n_chunks, SC_ROWS, TOP_K).transpose(0, 1, 3, 2).reshape(
        SC_WORKERS, n_chunks * TOP_K, SC_ROWS)
    rows = _sc_dispatch(h2p, idx_d)
    ys = _moe(layer, blk_meta, rows, w_gu, b_gu, w_down, b_down)
    idx_c = dest.T.reshape(SC_WORKERS, TOP_K * n_chunks, SC_ROWS)
    yg = _sc_collect(ys, idx_c).reshape(TOP_K, N_TOK, ROW_WORDS)
    return _combine(yg, gate_slab, x1, mod_l, g_l, split_out)


def _pad_heads(w, n_heads, width, keep):
    k = w.shape[0]
    w = w.reshape(k, n_heads, width)[:, :, :keep]
    return jnp.pad(w, ((0, 0), (0, 0), (0, LANES - keep))).reshape(k, n_heads * LANES)


def _pe_slab(x):
    return jnp.pad(x, [(0, 0)] * (x.ndim - 1) + [(MLA_NOPE, LANES - MLA_NOPE - MLA_ROPE)])


_SWA_ORDER = np.array([0, 4, 1, 5, 2, 6, 3, 7])


def kernel(x_prompt, x_sample, cache_mla_ckv, cache_mla_krope, cache_diff_k, cache_diff_v, cache_swa_k, cache_swa_v, cache_na_k, cache_na_v, c, c_ctx, w_mod, b_mod, norm_g, w_in0, mla_q_norm, w_uq, mla_kv_norm, w_ukv, diff_lambda, diff_norm, w_out0, w_in1, swa_sink, na_rpb, w_out1, w_router, b_router, w_gu, b_gu, w_down, b_down):
    xs = (x_prompt.reshape(NP_TOK, D_MODEL), x_sample.reshape(NS_TOK, D_MODEL))
    cond = jnp.concatenate([c_ctx[None, :], c, jnp.zeros((16 - 1 - DEC_BATCH, D_MODEL), F32)], axis=0)
    mod = _modulation(cond, w_mod, b_mod).reshape(DEPTH, 16, 6, D_MODEL)
    t64, t32 = _rope_tables()
    states = {}
    for l in range(DEPTH):
        i = l // 2
        g_l = norm_g[l]
        mod_l = mod[l]
        if l % 2 == 0:
            lam_init = 0.8 - 0.6 * math.exp(-0.3 * l)
            wi = w_in0[i]
            w_in_p = jnp.concatenate(
                [wi[:, 0:640], wi[:, 672:2208], _pe_slab(wi[:, 640:672])], axis=1).astype(BF16)
            w_uq_p = _pad_heads(w_uq[i], MLA_HEADS, MLA_NOPE + MLA_ROPE, MLA_NOPE + MLA_ROPE).astype(BF16)
            w_k_p = _pad_heads(w_ukv[i], MLA_HEADS, MLA_NOPE + MLA_V, MLA_NOPE).astype(BF16)
            w_v = w_ukv[i].reshape(MLA_KV_RANK, MLA_HEADS, MLA_NOPE + MLA_V)[:, :, MLA_NOPE:].reshape(
                MLA_KV_RANK, MLA_HEADS * MLA_V).astype(BF16)
            (q, k, v, dq, dk, dv, ckv_st, kpe_st, dk_st, dv_st) = _premix0(
                *xs, mod_l, g_l[0:1], w_in_p, mla_q_norm[i][None, :], w_uq_p, mla_kv_norm[i][None, :],
                w_k_p, w_v, t32, t64)
            states['mla_ckv'] = ckv_st.reshape(BATCH, 1, SEQ, MLA_KV_RANK)
            states['mla_krope'] = kpe_st[:, MLA_NOPE:MLA_NOPE + MLA_ROPE].reshape(BATCH, 1, SEQ, MLA_ROPE)
            states['diff_k'] = dk_st.reshape(BATCH, 1, SEQ, DIFF_HEADS, 2 * DIFF_DH)
            states['diff_v'] = dv_st.reshape(BATCH, 1, SEQ, DIFF_HEADS, 2 * DIFF_DH)
            kc, vc = _mla_cache(cache_mla_ckv[:, i].reshape(DEC_BATCH * PAST_LEN, MLA_KV_RANK),
                                _pe_slab(cache_mla_krope[:, i].reshape(DEC_BATCH * PAST_LEN, MLA_ROPE)),
                                w_k_p, w_v)
            dkc = cache_diff_k[:, i].reshape(DEC_BATCH * PAST_LEN, 512).astype(BF16)
            dvc = cache_diff_v[:, i].reshape(DEC_BATCH * PAST_LEN, 512).astype(BF16)
            lam = diff_lambda[i]
            sub_g = diff_norm[i][None, :]
            o_p = _attn_ab(q, dq, (k, v, dk, dv), None, lam, sub_g, lam_init,
                           n_batch=BATCH, t_len=SEQ, tq=SEQ, tok_off=0)
            o_cat = _attn_ab(q, dq, (k, v, dk, dv), (kc, vc, dkc, dvc), lam, sub_g, lam_init,
                             n_batch=DEC_BATCH, t_len=DEC_SEQ, tq=TQ_AB, tok_off=NP_TOK, out_init=o_p)
            w_out = w_out0[i]
        else:
            wi = w_in1[i]
            sq_cols = (_SWA_ORDER[:, None] * HEAD_DIM + np.arange(HEAD_DIM)[None, :]).reshape(-1)
            w_in_p = jnp.concatenate([wi[:, sq_cols], wi[:, 512:]], axis=1).astype(BF16)
            (sq, sk, sv, nq, nk, nv, sk_st, sv_st, nk_st, nv_st) = _premix1(xs[0], mod_l, g_l[0:1], w_in_p, t64)
            states['swa_k'] = sk_st.reshape(BATCH, 1, SEQ, SWA_KV_HEADS, HEAD_DIM)
            states['swa_v'] = sv_st.reshape(BATCH, 1, SEQ, SWA_KV_HEADS, HEAD_DIM)
            states['na_k'] = nk_st.reshape(BATCH, 1, SEQ, NA_HEADS, HEAD_DIM)
            states['na_v'] = nv_st.reshape(BATCH, 1, SEQ, NA_HEADS, HEAD_DIM)
            skc = cache_swa_k[:, i].reshape(DEC_BATCH * PAST_LEN, 128).astype(BF16)
            svc = cache_swa_v[:, i].reshape(DEC_BATCH * PAST_LEN, 128).astype(BF16)
            nkc = cache_na_k[:, i].reshape(DEC_BATCH * PAST_LEN, 512).astype(BF16)
            nvc = cache_na_v[:, i].reshape(DEC_BATCH * PAST_LEN, 512).astype(BF16)
            sink = swa_sink[i]
            o_p = _attn_cd_prompt(sink, sq, sk, sv, nq, nk, nv)
            o_cat = _attn_cd_sample(o_p, sink, sq, sk, sv, nq, nk, nv, skc, svc, nkc, nvc, _na_bias(na_rpb[i]))
            wo = w_out1[i]
            w_out = jnp.concatenate([wo[sq_cols], wo[512:]], axis=0)
        xs = _ffn(l, o_cat, xs, mod_l, g_l, w_out, w_router[l], b_router[l], w_gu, b_gu, w_down, b_down,
                  split_out=(l == DEPTH - 1))
    return (xs[0].reshape(BATCH, SEQ, D_MODEL), xs[1].reshape(DEC_BATCH, DEC_SEQ, D_MODEL),
            states['mla_ckv'], states['mla_krope'], states['diff_k'], states['diff_v'],
            states['swa_k'], states['swa_v'], states['na_k'], states['na_v'])
```

```python
import functools
import math

import numpy as np
import jax
import jax.numpy as jnp
from jax import lax
from jax.experimental import pallas as pl
from jax.experimental.pallas import tpu as pltpu
from jax.experimental.pallas import tpu_sc as plsc

F32 = jnp.float32
BF16 = jnp.bfloat16

D_MODEL = 1024
BATCH = 16
SEQ = 256
DEPTH = 2
DEC_BATCH = 8
DEC_SEQ = 2048
PAST_LEN = 256
GRID_W = 64
HEAD_DIM = 64
ROPE_THETA = 10000.0
EPS = 1e-6
NEG = -1e30

MLA_HEADS = 8
MLA_Q_RANK = 384
MLA_KV_RANK = 256
MLA_NOPE = 64
MLA_ROPE = 32
MLA_V = 64
DIFF_HEADS = 4
DIFF_DH = 64
SWA_HEADS = 8
SWA_KV_HEADS = 2
SWA_WINDOW = 128
NA_HEADS = 8
NA_WIN_ROWS = 8
NA_WIN_COLS = 16
N_EXPERTS = 32
TOP_K = 4
D_EXPERT = 1024
SWIGLU_LIMIT = 7.0
SWIGLU_ALPHA = 1.702

LANES = 128
NP_TOK = BATCH * SEQ
NS_TOK = DEC_BATCH * DEC_SEQ
N_TOK = NP_TOK + NS_TOK
TM = 512
NPT = NP_TOK // TM
TILES_PER_SAMPLE = DEC_SEQ // TM
N_TILES = N_TOK // TM
TQ = 256
TQ_AB = 256
MOE_TM = 512
MOE_TAIL_PARTS = 4
MOE_ROWS = ((N_TOK * TOP_K + N_EXPERTS * (MOE_TM - 1)) // MOE_TM + 1) * MOE_TM
MOE_BLOCKS = MOE_ROWS // MOE_TM
NA_TILE_ROWS = TQ // GRID_W
NA_KEY_ROWS = 12
VMEM_LIMIT = 56 * 1024 * 1024


def _cparams(n_axes, vmem=VMEM_LIMIT):
    return pltpu.CompilerParams(dimension_semantics=("arbitrary",) * n_axes,
                                vmem_limit_bytes=vmem)


def _rms(x, g):
    return x * lax.rsqrt(jnp.mean(x * x, axis=-1, keepdims=True) + EPS) * g


def _dot(a, b):
    return jnp.dot(a, b, preferred_element_type=F32)


def _dot_nt(a, b):
    return lax.dot_general(a, b, (((1,), (1,)), ((), ())), preferred_element_type=F32)


def _rope(x, cos, sin_a, sin_b, half):
    return (x * cos + pltpu.roll(x, LANES - half, 1) * sin_a + pltpu.roll(x, half, 1) * sin_b)


def _mod_row(i):
    return jnp.where(i < NPT, 0, 1 + (i - NPT) // TILES_PER_SAMPLE)


def _rope_blk(i):
    return jnp.where(i < NPT, TILES_PER_SAMPLE, (i - NPT) % TILES_PER_SAMPLE)


def _mod_kernel(c_ref, w_ref, b_ref, o_ref):
    c = c_ref[...]
    s = (c * jax.nn.sigmoid(c)).astype(BF16)
    o_ref[0] = _dot(s, w_ref[0].astype(BF16)) + b_ref[0]


def _modulation(cond, w_mod, b_mod):
    nb = 1024
    return pl.pallas_call(
        _mod_kernel,
        grid=(DEPTH, 6 * D_MODEL // nb),
        in_specs=[
            pl.BlockSpec((16, D_MODEL), lambda l, n: (0, 0)),
            pl.BlockSpec((1, D_MODEL, nb), lambda l, n: (l, 0, n)),
            pl.BlockSpec((1, 1, nb), lambda l, n: (l, 0, n)),
        ],
        out_specs=pl.BlockSpec((1, 16, nb), lambda l, n: (l, 0, n)),
        out_shape=jax.ShapeDtypeStruct((DEPTH, 16, 6 * D_MODEL), F32),
        compiler_params=_cparams(2),
        name="modulation",
    )(cond, w_mod, b_mod.reshape(DEPTH, 1, 6 * D_MODEL))


def _rope_tables():
    t = jnp.arange(DEC_SEQ)
    rows = (t // GRID_W).astype(F32)
    cols = (t % GRID_W).astype(F32)

    def angles(r):
        n = r // 4
        inv = ROPE_THETA ** (-jnp.arange(n, dtype=F32) / n)
        return jnp.concatenate([rows[:, None] * inv[None], cols[:, None] * inv[None]], axis=-1)

    def finish(cos, sa, sb):
        ident = (jnp.ones((TM, LANES), F32), jnp.zeros((TM, LANES), F32), jnp.zeros((TM, LANES), F32))
        return tuple(jnp.concatenate([a, b], axis=0) for a, b in zip((cos, sa, sb), ident))

    a64 = angles(64)
    c, s, z = jnp.cos(a64), jnp.sin(a64), jnp.zeros_like(a64)
    t64 = finish(jnp.concatenate([c, c, c, c], -1), jnp.concatenate([-s, z, -s, z], -1),
                 jnp.concatenate([z, s, z, s], -1))
    a32 = angles(32)
    c, s, z = jnp.cos(a32), jnp.sin(a32), jnp.zeros_like(a32)
    one64 = jnp.ones((DEC_SEQ, 64), F32)
    z64 = jnp.zeros((DEC_SEQ, 64), F32)
    z32 = jnp.zeros((DEC_SEQ, 32), F32)
    t32 = finish(jnp.concatenate([one64, c, c, z32], -1), jnp.concatenate([z64, -s, z, z32], -1),
                 jnp.concatenate([z64, z, s, z32], -1))
    return t64, t32


LOG2E = math.log2(math.e)
_MLA_SCALE = (MLA_NOPE + MLA_ROPE) ** -0.5 * LOG2E
_QSCALE = HEAD_DIM ** -0.5 * LOG2E


def _premix0_kernel(xp_ref, xs_ref, mod_ref, g_ref, win_ref, qn_ref, wuq_ref, kvn_ref, wk_ref, wv_ref,
                    c32_ref, sa32_ref, sb32_ref, c64_ref, sa64_ref, sb64_ref,
                    q_ref, k_ref, v_ref, dq_ref, dk_ref, dv_ref,
                    ckv_st, kpe_st, dk_st, dv_st):
    i = pl.program_id(0)
    m = mod_ref[0]
    h = _rms(_pick_x(i, xp_ref, xs_ref), g_ref[...]) * (1.0 + m[1:2]) + m[0:1]
    proj = _dot(h.astype(BF16), win_ref[...])
    q_a = proj[:, 0:384]
    kv_a = proj[:, 384:640]
    dq = proj[:, 640:1152]
    dk = proj[:, 1152:1664]
    dv = proj[:, 1664:2176]
    pe = proj[:, 2176:2304]
    q = _dot(_rms(q_a, qn_ref[...]).astype(BF16), wuq_ref[...])
    ckv = _rms(kv_a, kvn_ref[...])
    ckv_b = ckv.astype(BF16)
    kn = _dot(ckv_b, wk_ref[...])
    v_ref[...] = _dot(ckv_b, wv_ref[...]).astype(BF16)
    c32, sa32, sb32 = c32_ref[...], sa32_ref[...], sb32_ref[...]
    c64, sa64, sb64 = c64_ref[...], sa64_ref[...], sb64_ref[...]
    pe_r = _rope(pe, c32, sa32, sb32, MLA_ROPE // 2)
    for hd in range(MLA_HEADS):
        sl = slice(LANES * hd, LANES * (hd + 1))
        q_ref[:, sl] = (_rope(q[:, sl], c32, sa32, sb32, MLA_ROPE // 2) * _MLA_SCALE).astype(BF16)
        k_ref[:, sl] = (kn[:, sl] + pe_r).astype(BF16)
    for hd in range(DIFF_HEADS):
        sl = slice(LANES * hd, LANES * (hd + 1))
        dq_ref[:, sl] = (_rope(dq[:, sl], c64, sa64, sb64, DIFF_DH // 2) * _QSCALE).astype(BF16)
        dk_ref[:, sl] = _rope(dk[:, sl], c64, sa64, sb64, DIFF_DH // 2).astype(BF16)
    dv_ref[...] = dv.astype(BF16)

    @pl.when(i < NPT)
    def _():
        ckv_st[...] = ckv
        kpe_st[...] = pe
        dk_st[...] = dk
        dv_st[...] = dv


def _premix1_kernel(x_ref, mod_ref, g_ref, win_ref, c64_ref, sa64_ref, sb64_ref,
                    sq_ref, sk_ref, sv_ref, nq_ref, nk_ref, nv_ref,
                    sk_st, sv_st, nk_st, nv_st):
    i = pl.program_id(0)
    m = mod_ref[0]
    h = _rms(x_ref[...], g_ref[...]) * (1.0 + m[1:2]) + m[0:1]
    proj = _dot(h.astype(BF16), win_ref[...])
    sq = proj[:, 0:512]
    sk = proj[:, 512:640]
    sv = proj[:, 640:768]
    nq = proj[:, 768:1280]
    nk = proj[:, 1280:1792]
    nv = proj[:, 1792:2304]
    c64, sa64, sb64 = c64_ref[...], sa64_ref[...], sb64_ref[...]
    for hd in range(4):
        sl = slice(LANES * hd, LANES * (hd + 1))
        sq_ref[:, sl] = (_rope(sq[:, sl], c64, sa64, sb64, HEAD_DIM // 2) * _QSCALE).astype(BF16)
    sk_ref[...] = _rope(sk, c64, sa64, sb64, HEAD_DIM // 2).astype(BF16)
    sv_ref[...] = sv.astype(BF16)
    nq_ref[...] = (nq * _QSCALE).astype(BF16)
    nk_ref[...] = nk.astype(BF16)
    nv_ref[...] = nv.astype(BF16)

    @pl.when(i < NPT)
    def _():
        sk_st[...] = sk
        sv_st[...] = sv
        nk_st[...] = nk
        nv_st[...] = nv


def _tok_spec(width):
    return pl.BlockSpec((TM, width), lambda i: (i, 0))


_PROMPT_SPEC = pl.BlockSpec((TM, D_MODEL), lambda i: (jnp.minimum(i, NPT - 1), 0))
_SAMPLE_SPEC = pl.BlockSpec((TM, D_MODEL), lambda i: (jnp.maximum(i - NPT, 0), 0))


def _pick_x(i, xp_ref, xs_ref):
    return jnp.where(i < NPT, xp_ref[...], xs_ref[...])


def _state_spec(width):
    return pl.BlockSpec((TM, width), lambda i: (jnp.minimum(i, NPT - 1), 0))


def _const_spec(shape):
    return pl.BlockSpec(shape, lambda i: (0,) * len(shape))


_MOD_SPEC = pl.BlockSpec((1, 6, D_MODEL), lambda i: (_mod_row(i), 0, 0))
_ROPE_SPEC = pl.BlockSpec((TM, LANES), lambda i: (_rope_blk(i), 0))


def _premix0(xp, xs, mod_l, g0, w_in_p, q_norm, w_uq_p, kv_norm, w_k_p, w_v, t32, t64):
    outs = [(N_TOK, 1024, BF16), (N_TOK, 1024, BF16), (N_TOK, 512, BF16), (N_TOK, 512, BF16),
            (N_TOK, 512, BF16), (N_TOK, 512, BF16),
            (NP_TOK, 256, F32), (NP_TOK, 128, F32), (NP_TOK, 512, F32), (NP_TOK, 512, F32)]
    return pl.pallas_call(
        _premix0_kernel,
        grid=(N_TILES,),
        in_specs=[_PROMPT_SPEC, _SAMPLE_SPEC, _MOD_SPEC, _const_spec((1, D_MODEL)),
                  _const_spec((D_MODEL, 2304)), _const_spec((1, MLA_Q_RANK)),
                  _const_spec((MLA_Q_RANK, 1024)), _const_spec((1, MLA_KV_RANK)),
                  _const_spec((MLA_KV_RANK, 1024)), _const_spec((MLA_KV_RANK, 512))]
                 + [_ROPE_SPEC] * 6,
        out_specs=[_tok_spec(w) for (_, w, _) in outs[:6]] + [_state_spec(w) for (_, w, _) in outs[6:]],
        out_shape=[jax.ShapeDtypeStruct((n, w), dt) for (n, w, dt) in outs],
        compiler_params=_cparams(1),
        name="premix_ab",
    )(xp, xs, mod_l, g0, w_in_p, q_norm, w_uq_p, kv_norm, w_k_p, w_v, *t32, *t64)


def _premix1(x, mod_l, g0, w_in_p, t64):
    outs = [(N_TOK, 512, BF16), (N_TOK, 128, BF16), (N_TOK, 128, BF16), (N_TOK, 512, BF16),
            (N_TOK, 512, BF16), (N_TOK, 512, BF16),
            (NP_TOK, 128, F32), (NP_TOK, 128, F32), (NP_TOK, 512, F32), (NP_TOK, 512, F32)]
    return pl.pallas_call(
        _premix1_kernel,
        grid=(N_TILES,),
        in_specs=[_tok_spec(D_MODEL), _MOD_SPEC, _const_spec((1, D_MODEL)),
                  _const_spec((D_MODEL, 2304))] + [_ROPE_SPEC] * 3,
        out_specs=[_tok_spec(w) for (_, w, _) in outs[:6]] + [_state_spec(w) for (_, w, _) in outs[6:]],
        out_shape=[jax.ShapeDtypeStruct((n, w), dt) for (n, w, dt) in outs],
        compiler_params=_cparams(1),
        name="premix_cd",
    )(x, mod_l, g0, w_in_p, *t64)


def _mla_cache_kernel(ckv_ref, pe_ref, wk_ref, wv_ref, k_ref, v_ref):
    c = ckv_ref[...].astype(BF16)
    kn = _dot(c, wk_ref[...])
    v_ref[...] = _dot(c, wv_ref[...]).astype(BF16)
    pe = pe_ref[...]
    for hd in range(MLA_HEADS):
        sl = slice(LANES * hd, LANES * (hd + 1))
        k_ref[:, sl] = (kn[:, sl] + pe).astype(BF16)


def _mla_cache(ckv, pe_slab, w_k_p, w_v):
    n = ckv.shape[0]
    tm = 512
    return pl.pallas_call(
        _mla_cache_kernel,
        grid=(n // tm,),
        in_specs=[pl.BlockSpec((tm, MLA_KV_RANK), lambda i: (i, 0)),
                  pl.BlockSpec((tm, LANES), lambda i: (i, 0)),
                  _const_spec((MLA_KV_RANK, 1024)), _const_spec((MLA_KV_RANK, 512))],
        out_specs=[pl.BlockSpec((tm, 1024), lambda i: (i, 0)), pl.BlockSpec((tm, 512), lambda i: (i, 0))],
        out_shape=[jax.ShapeDtypeStruct((n, 1024), BF16), jax.ShapeDtypeStruct((n, 512), BF16)],
        compiler_params=_cparams(1),
        name="mla_cache",
    )(ckv, pe_slab, w_k_p, w_v)


def _softmax_pv(scores, values, sink=None):
    m = jnp.max(scores[0], axis=-1, keepdims=True)
    for s in scores[1:]:
        m = jnp.maximum(m, jnp.max(s, axis=-1, keepdims=True))
    if sink is not None:
        m = jnp.maximum(m, sink)
    l = None
    o = None
    for s, v in zip(scores, values):
        p = jnp.exp2(s - m)
        ls = jnp.sum(p, axis=-1, keepdims=True)
        os_ = _dot(p.astype(BF16), v)
        l = ls if l is None else l + ls
        o = os_ if o is None else o + os_
    if sink is not None:
        l = l + jnp.exp2(sink - m)
    return o * (1.0 / l)


def _lane_lo(shape):
    return lax.broadcasted_iota(jnp.int32, shape, 1) < (LANES // 2)


def _split_halves(qb):
    lo = _lane_lo(qb.shape)
    zero = jnp.zeros_like(qb)
    return jnp.where(lo, qb, zero), jnp.where(lo, zero, qb)


def _attn_ab_kernel(*refs, n_pieces, lam_init, aliased):
    if aliased:
        refs = refs[1:]
    q_ref, dq_ref = refs[0], refs[1]
    pieces = [refs[2 + 4 * p: 6 + 4 * p] for p in range(n_pieces)]
    lam_ref, subg_ref, o_ref = refs[2 + 4 * n_pieces:]
    lam = lam_ref[...]
    lam_full = (jnp.exp(jnp.sum(lam[0:1] * lam[1:2], axis=-1, keepdims=True))
                - jnp.exp(jnp.sum(lam[2:3] * lam[3:4], axis=-1, keepdims=True)) + lam_init)
    lo = _lane_lo((q_ref.shape[0], LANES))
    subg = subg_ref[...]
    tq = q_ref.shape[0]
    for j in range(MLA_HEADS // 2):
        pair = []
        for half in range(2):
            hd = 2 * j + half
            sl = slice(LANES * hd, LANES * (hd + 1))
            qh = q_ref[:, sl]
            scores = [_dot_nt(qh, k_ref[:, sl]) for (k_ref, _, _, _) in pieces]
            vals = [v_ref[:, LANES * j: LANES * (j + 1)] for (_, v_ref, _, _) in pieces]
            pair.append(_softmax_pv(scores, vals))
        o_ref[:, LANES * j: LANES * (j + 1)] = jnp.where(lo, pair[0], pair[1]).astype(BF16)
        hd = j
        sl = slice(LANES * hd, LANES * (hd + 1))
        qq = jnp.concatenate(_split_halves(dq_ref[:, sl]), axis=0)
        ks = [dk_ref[:, sl] for (_, _, dk_ref, _) in pieces]
        vs = [dv_ref[:, sl] for (_, _, _, dv_ref) in pieces]
        oo = _softmax_pv([_dot_nt(qq, k) for k in ks], vs)
        od = _rms(oo[:tq] - lam_full * oo[tq:], subg) * (1.0 - lam_init)
        o_ref[:, 512 + LANES * hd: 512 + LANES * (hd + 1)] = od.astype(BF16)


def _attn_ab(q, dq, new_kv, cache_kv, lam, sub_g, lam_init, *, n_batch, t_len, tq, tok_off, out_init=None):
    nq = t_len // tq
    q_off = tok_off // tq
    b_off = tok_off // t_len
    widths = (1024, 512, 512, 512)
    in_specs = [pl.BlockSpec((tq, 1024), lambda b, i: (q_off + b * nq + i, 0)),
                pl.BlockSpec((tq, 512), lambda b, i: (q_off + b * nq + i, 0))]
    args = [q, dq]
    for w, a in zip(widths, new_kv):
        in_specs.append(pl.BlockSpec((t_len, w), lambda b, i: (b_off + b, 0)))
        args.append(a)
    n_pieces = 1
    if cache_kv is not None:
        n_pieces = 2
        for w, a in zip(widths, cache_kv):
            in_specs.append(pl.BlockSpec((PAST_LEN, w), lambda b, i: (b, 0)))
            args.append(a)
    in_specs += [pl.BlockSpec((4, DIFF_DH), lambda b, i: (0, 0)),
                 pl.BlockSpec((1, 2 * DIFF_DH), lambda b, i: (0, 0))]
    args += [lam, sub_g]
    aliases = {}
    if out_init is not None:
        in_specs = [pl.BlockSpec(memory_space=pl.ANY)] + in_specs
        args = [out_init] + args
        aliases = {0: 0}
    return pl.pallas_call(
        functools.partial(_attn_ab_kernel, n_pieces=n_pieces, lam_init=lam_init, aliased=out_init is not None),
        grid=(n_batch, nq),
        in_specs=in_specs,
        out_specs=pl.BlockSpec((tq, 1024), lambda b, i: (q_off + b * nq + i, 0)),
        out_shape=jax.ShapeDtypeStruct((N_TOK, 1024), BF16),
        input_output_aliases=aliases,
        compiler_params=_cparams(2),
        name="attn_ab_%d" % n_pieces,
    )(*args)


def _gqa_stacks(sq_ref, sink_ref):
    tq = sq_ref.shape[0]
    halves = [_split_halves(sq_ref[:, LANES * j: LANES * (j + 1)]) for j in range(4)]
    q_stacks = [jnp.concatenate([halves[j][kvh] for j in range(4)], axis=0) for kvh in range(SWA_KV_HEADS)]
    sinks = [jnp.concatenate([jnp.full((tq, 1), sink_ref[4 * kvh + j] * LOG2E, F32) for j in range(4)], axis=0)
             for kvh in range(SWA_KV_HEADS)]
    return q_stacks, sinks


def _attn_cd_prompt_kernel(sink_ref, sq_ref, sk_ref, sv_ref, nq_ref, nk_ref, nv_ref, o_ref):
    tq = sq_ref.shape[0]
    lo = _lane_lo((tq, LANES))
    sk = sk_ref[...]
    sv = sv_ref[...]
    for j in range(4):
        sl = slice(LANES * j, LANES * (j + 1))
        q_lo, q_hi = _split_halves(sq_ref[:, sl])
        o_lo = _softmax_pv([_dot_nt(q_lo, sk)], [sv], sink=sink_ref[j] * LOG2E)
        o_hi = _softmax_pv([_dot_nt(q_hi, sk)], [sv], sink=sink_ref[j + 4] * LOG2E)
        o_ref[:, sl] = jnp.where(lo, o_lo, o_hi).astype(BF16)
    for j in range(4):
        sl = slice(LANES * j, LANES * (j + 1))
        q_lo, q_hi = _split_halves(nq_ref[:, sl])
        k = nk_ref[:, sl]
        v = nv_ref[:, sl]
        o_lo = _softmax_pv([_dot_nt(q_lo, k)], [v])
        o_hi = _softmax_pv([_dot_nt(q_hi, k)], [v])
        o_ref[:, 512 + LANES * j: 512 + LANES * (j + 1)] = jnp.where(lo, o_lo, o_hi).astype(BF16)


def _attn_cd_prompt(sink, sq, sk, sv, nq, nk, nv):
    def spec(w):
        return pl.BlockSpec((SEQ, w), lambda b: (b, 0))
    return pl.pallas_call(
        _attn_cd_prompt_kernel,
        grid=(BATCH,),
        in_specs=[pl.BlockSpec(memory_space=pltpu.SMEM), spec(512), spec(128), spec(128),
                  spec(512), spec(512), spec(512)],
        out_specs=spec(1024),
        out_shape=jax.ShapeDtypeStruct((N_TOK, 1024), BF16),
        compiler_params=_cparams(1),
        name="attn_cd_prompt",
    )(sink, sq, sk, sv, nq, nk, nv)


_SWA_KEYS = TQ + 2 * SWA_WINDOW


def _attn_cd_sample_kernel(init_ref, sink_ref, sq_ref, nq_ref, sk_ref, sv_ref, nk_ref, nv_ref,
                           skc_ref, svc_ref, nkc_ref, nvc_ref, bias_ref, o_ref):
    del init_ref
    qi = pl.program_id(1)
    lo = _lane_lo((TQ, LANES))
    ks = pl.multiple_of(jnp.clip(qi * TQ - SWA_WINDOW, 0, DEC_SEQ - _SWA_KEYS), SWA_WINDOW)
    k_win = sk_ref[pl.ds(ks, _SWA_KEYS), :]
    v_win = sv_ref[pl.ds(ks, _SWA_KEYS), :]
    q_pos = qi * TQ + (lax.broadcasted_iota(jnp.int32, (4 * TQ, _SWA_KEYS), 0) & (TQ - 1))
    k_pos = ks + lax.broadcasted_iota(jnp.int32, (4 * TQ, _SWA_KEYS), 1)
    in_win = jnp.abs(q_pos - k_pos) <= SWA_WINDOW
    skc = skc_ref[...]
    svc = svc_ref[...]
    q_stacks, sinks = _gqa_stacks(sq_ref, sink_ref)

    def window_stack(kvh):
        s_lat = jnp.where(in_win, _dot_nt(q_stacks[kvh], k_win), NEG)
        return _softmax_pv([_dot_nt(q_stacks[kvh], skc), s_lat], [svc, v_win], sink=sinks[kvh])

    n_rows = DEC_SEQ // GRID_W
    r0 = jnp.clip(qi * NA_TILE_ROWS - NA_WIN_ROWS // 2, 0, n_rows - NA_KEY_ROWS)
    kn = pl.multiple_of(r0 * GRID_W, GRID_W)
    o_kv = []
    for j in range(4):
        if j % 2 == 0:
            o_kv.append(window_stack(j // 2))
        sl = slice(LANES * j, LANES * (j + 1))
        nk_win = nk_ref[pl.ds(kn, NA_KEY_ROWS * GRID_W), sl]
        nv_win = nv_ref[pl.ds(kn, NA_KEY_ROWS * GRID_W), sl]
        kc = nkc_ref[:, sl]
        vc = nvc_ref[:, sl]
        outs = []
        for q_half, hd in zip(_split_halves(nq_ref[:, sl]), (2 * j, 2 * j + 1)):
            s_lat = _dot_nt(q_half, nk_win) + bias_ref[0, hd]
            outs.append(_softmax_pv([_dot_nt(q_half, kc), s_lat], [vc, nv_win]))
        o_ref[:, 512 + LANES * j: 512 + LANES * (j + 1)] = jnp.where(lo, outs[0], outs[1]).astype(BF16)
    for j in range(4):
        rows = slice(TQ * j, TQ * (j + 1))
        o_ref[:, LANES * j: LANES * (j + 1)] = jnp.where(lo, o_kv[0][rows], o_kv[1][rows]).astype(BF16)


def _attn_cd_sample(out_init, sink, sq, sk, sv, nq, nk, nv, skc, svc, nkc, nvc, bias):
    nq_t = DEC_SEQ // TQ
    q_off = NP_TOK // TQ
    b_off = NP_TOK // DEC_SEQ

    def qspec(w):
        return pl.BlockSpec((TQ, w), lambda b, i: (q_off + b * nq_t + i, 0))

    def kspec(w):
        return pl.BlockSpec((DEC_SEQ, w), lambda b, i: (b_off + b, 0))

    def cspec(w):
        return pl.BlockSpec((PAST_LEN, w), lambda b, i: (b, 0))

    n_keys = NA_KEY_ROWS * GRID_W
    bias_spec = pl.BlockSpec(
        (1, NA_HEADS, TQ, n_keys),
        lambda b, i: (jnp.where(i == 0, 0, jnp.where(i == nq_t - 1, 2, 1)), 0, 0, 0))
    return pl.pallas_call(
        _attn_cd_sample_kernel,
        grid=(DEC_BATCH, nq_t),
        in_specs=[pl.BlockSpec(memory_space=pl.ANY), pl.BlockSpec(memory_space=pltpu.SMEM),
                  qspec(512), qspec(512), kspec(128), kspec(128), kspec(512), kspec(512),
                  cspec(128), cspec(128), cspec(512), cspec(512), bias_spec],
        out_specs=pl.BlockSpec((TQ, 1024), lambda b, i: (q_off + b * nq_t + i, 0)),
        out_shape=jax.ShapeDtypeStruct((N_TOK, 1024), BF16),
        input_output_aliases={0: 0},
        compiler_params=_cparams(2),
        name="attn_cd_sample",
    )(out_init, sink, sq, nq, sk, sv, nk, nv, skc, svc, nkc, nvc, bias)


def _na_bias(rpb):
    n_rows = DEC_SEQ // GRID_W
    n_dr = 2 * NA_WIN_ROWS - 1
    n_dc = 2 * NA_WIN_COLS - 1
    c = np.arange(GRID_W)[:, None]
    kc = np.arange(GRID_W)[None, :]
    qs = np.clip(c - NA_WIN_COLS // 2, 0, GRID_W - NA_WIN_COLS)
    col_ok = (kc >= qs) & (kc < qs + NA_WIN_COLS)
    dc = np.clip(kc - c + NA_WIN_COLS - 1, 0, n_dc - 1)
    onehot = ((dc[None] == np.arange(n_dc)[:, None, None]) & col_ok[None]).astype(np.float32)
    blocks = jnp.einsum('hrd,dck->hrck', rpb.astype(F32) * LOG2E, onehot, precision=lax.Precision.HIGHEST)
    row_sel = np.zeros((3, n_dr, NA_TILE_ROWS, NA_KEY_ROWS), np.float32)
    for v, q_row0 in enumerate((0, NA_TILE_ROWS, n_rows - NA_TILE_ROWS)):
        k_row0 = min(max(q_row0 - NA_WIN_ROWS // 2, 0), n_rows - NA_KEY_ROWS)
        for ri in range(NA_TILE_ROWS):
            r = q_row0 + ri
            rs = min(max(r - NA_WIN_ROWS // 2, 0), n_rows - NA_WIN_ROWS)
            for kj in range(NA_KEY_ROWS):
                kr = k_row0 + kj
                if rs <= kr < rs + NA_WIN_ROWS:
                    row_sel[v, kr - r + NA_WIN_ROWS - 1, ri, kj] = 1.0
    tiles = jnp.einsum('vdrj,hdck->vhrcjk', row_sel, blocks, precision=lax.Precision.HIGHEST)
    valid = (row_sel.sum(axis=1) > 0)[:, :, None, :, None] & col_ok[None, None, :, None, :]
    tiles = jnp.where(valid[:, None], tiles, NEG)
    return tiles.reshape(3, NA_HEADS, TQ, NA_KEY_ROWS * GRID_W)


_HI_MASK = -65536


def _pack_pairs(x):
    w = x.shape[1] // 2
    r = x.astype(BF16).astype(F32)
    lo = lax.bitcast_convert_type(r[:, :w], jnp.int32)
    hi = lax.bitcast_convert_type(r[:, w:], jnp.int32)
    return (hi & _HI_MASK) | lax.shift_right_logical(lo, 16)


def _unpack_pairs(p):
    lo = lax.bitcast_convert_type(lax.shift_left(p, 16), F32)
    hi = lax.bitcast_convert_type(p & _HI_MASK, F32)
    return lo, hi


def _postmix_kernel(*refs, split_x):
    if split_x:
        o_ref, xp_ref, xs_ref = refs[:3]
        refs = refs[3:]
    else:
        o_ref, x_ref = refs[:2]
        refs = refs[2:]
    (mod_ref, g_ref, wout_ref, wr_ref, br_ref, tri_ref,
     x1_ref, h2_ref, route_ref, gate_ref, cnt_ref, run_ref) = refs
    i = pl.program_id(0)
    x = _pick_x(i, xp_ref, xs_ref) if split_x else x_ref[...]

    @pl.when(i == 0)
    def _():
        run_ref[...] = jnp.zeros_like(run_ref)

    m = mod_ref[0]
    g = g_ref[...]
    y = _dot(o_ref[...], wout_ref[...])
    x1 = x + m[2:3] * _rms(y, g[1:2])
    x1_ref[...] = x1
    h2 = _rms(x1, g[2:3]) * (1.0 + m[4:5]) + m[3:4]
    h2_ref[...] = _pack_pairs(h2)
    logits = _dot(h2.astype(BF16), wr_ref[...]) + br_ref[...]
    lane = lax.broadcasted_iota(jnp.int32, logits.shape, 1).astype(F32)
    cur = jnp.where(lane < N_EXPERTS, logits, -jnp.inf)
    tops, idxs = [], []
    for _ in range(TOP_K):
        mx = jnp.max(cur, axis=-1, keepdims=True)
        ix = jnp.min(jnp.where(cur == mx, lane, float(LANES)), axis=-1, keepdims=True)
        tops.append(mx)
        idxs.append(ix)
        cur = jnp.where(lane == ix, -jnp.inf, cur)
    es = [jnp.exp(t - tops[0]) for t in tops]
    inv = 1.0 / (es[0] + es[1] + es[2] + es[3])
    picked = jnp.zeros_like(logits)
    for k in range(TOP_K):
        picked = jnp.where(lane == idxs[k], 1.0, picked)
    before = _dot(tri_ref[...], picked.astype(BF16)) + run_ref[0:1, :]
    route = jnp.zeros_like(logits)
    gate_out = jnp.zeros_like(logits)
    for k in range(TOP_K):
        rank = jnp.sum(jnp.where(lane == idxs[k], before, 0.0), axis=-1, keepdims=True)
        route = jnp.where(lane == float(k), idxs[k], route)
        route = jnp.where(lane == float(TOP_K + k), rank, route)
        gate_out = jnp.where(lane == float(k), es[k] * inv, gate_out)
    route_ref[...] = route.astype(jnp.int32)
    gate_ref[...] = gate_out
    run_ref[...] = run_ref[...] + jnp.sum(picked, axis=0, keepdims=True)
    cnt_ref[...] = run_ref[...].astype(jnp.int32)


def _postmix(o_cat, xs, mod_l, g_l, w_out, w_r, b_r):
    tri = jnp.asarray(np.tril(np.ones((TM, TM), np.float32), -1), BF16)
    split_x = len(xs) == 2
    x_specs = [_PROMPT_SPEC, _SAMPLE_SPEC] if split_x else [_tok_spec(D_MODEL)]
    return pl.pallas_call(
        functools.partial(_postmix_kernel, split_x=split_x),
        grid=(N_TILES,),
        in_specs=[_tok_spec(1024)] + x_specs + [_MOD_SPEC, _const_spec((4, D_MODEL)),
                  _const_spec((1024, D_MODEL)), _const_spec((D_MODEL, LANES)), _const_spec((1, LANES)),
                  _const_spec((TM, TM))],
        out_specs=[_tok_spec(D_MODEL), _tok_spec(D_MODEL // 2), _tok_spec(LANES), _tok_spec(LANES),
                   _const_spec((8, LANES))],
        out_shape=[jax.ShapeDtypeStruct((N_TOK, D_MODEL), F32),
                   jax.ShapeDtypeStruct((N_TOK, D_MODEL // 2), jnp.int32),
                   jax.ShapeDtypeStruct((N_TOK, LANES), jnp.int32), jax.ShapeDtypeStruct((N_TOK, LANES), F32),
                   jax.ShapeDtypeStruct((8, LANES), jnp.int32)],
        scratch_shapes=[pltpu.VMEM((8, LANES), F32)],
        compiler_params=_cparams(1),
        name="postmix",
    )(o_cat, *xs, mod_l, g_l, w_out, w_r, b_r, tri)


SC_WORKERS = 32
SC_ROWS = 128
ROW_WORDS = D_MODEL // 2


def _sc_worker_id():
    return lax.axis_index("s") * 2 + lax.axis_index("c")


def _sc_dispatch(src, idx):
    n_chunks = N_TOK // (SC_WORKERS * SC_ROWS)
    mesh = plsc.VectorSubcoreMesh(core_axis_name="c", subcore_axis_name="s")

    @functools.partial(
        pl.kernel, mesh=mesh,
        out_type=jax.ShapeDtypeStruct((MOE_ROWS, ROW_WORDS), jnp.int32),
        scratch_types=[pltpu.VMEM((n_chunks * TOP_K, SC_ROWS), jnp.int32),
                       pltpu.VMEM((SC_ROWS, ROW_WORDS), jnp.int32), pltpu.SemaphoreType.DMA])
    def k(src_hbm, idx_hbm, out_hbm, idx_v, rows_v, sem):
        wid = _sc_worker_id()
        pltpu.sync_copy(idx_hbm.at[wid], idx_v)

        @pl.loop(0, n_chunks)
        def _(g):
            pltpu.sync_copy(src_hbm.at[pl.ds((wid * n_chunks + g) * SC_ROWS, SC_ROWS)], rows_v)
            copies = [pltpu.async_copy(rows_v, out_hbm.at[idx_v.at[g * TOP_K + kk]], sem)
                      for kk in range(TOP_K)]
            for cp in copies:
                cp.wait()

    return k(src, idx)


def _sc_collect(table, idx):
    n_chunks = idx.shape[1]
    mesh = plsc.VectorSubcoreMesh(core_axis_name="c", subcore_axis_name="s")

    @functools.partial(
        pl.kernel, mesh=mesh,
        out_type=jax.ShapeDtypeStruct((SC_WORKERS * n_chunks * SC_ROWS, ROW_WORDS), jnp.int32),
        scratch_types=[pltpu.VMEM((n_chunks, SC_ROWS), jnp.int32),
                       pltpu.VMEM((SC_ROWS, ROW_WORDS), jnp.int32), pltpu.SemaphoreType.DMA])
    def k(table_hbm, idx_hbm, out_hbm, idx_v, rows_v, sem):
        wid = _sc_worker_id()
        pltpu.sync_copy(idx_hbm.at[wid], idx_v)

        @pl.loop(0, n_chunks)
        def _(g):
            pltpu.async_copy(table_hbm.at[idx_v.at[g]], rows_v, sem).wait()
            pltpu.sync_copy(rows_v, out_hbm.at[pl.ds((wid * n_chunks + g) * SC_ROWS, SC_ROWS)])

    return k(table, idx)


def _expert_rows(words, n_valid, wgu_b, wd_b, bgu, bd):
    live = lax.broadcasted_iota(jnp.int32, words.shape, 0) < n_valid
    lo, hi = _unpack_pairs(jnp.where(live, words, 0))
    x = jnp.concatenate([lo, hi], axis=1).astype(BF16)
    gu = _dot(x, wgu_b[...]) + bgu
    g = jnp.minimum(gu[:, :D_EXPERT], SWIGLU_LIMIT)
    u = jnp.clip(gu[:, D_EXPERT:], -SWIGLU_LIMIT, SWIGLU_LIMIT)
    a = g * jax.nn.sigmoid(SWIGLU_ALPHA * g) * (u + 1.0)
    return _pack_pairs(_dot(a.astype(BF16), wd_b[...]) + bd)


def _moe_kernel(blk_e_ref, blk_first_ref, blk_rows_ref, blk_slot_ref, blk_next_ref,
                x_ref, wgu_hbm, bgu_ref, wd_hbm, bd_ref, y_ref,
                wgu_f, wd_f, wgu_b, wd_b, sem, *, layer):
    i = pl.program_id(0)
    n_valid = blk_rows_ref[i]
    quantum = MOE_TM // MOE_TAIL_PARTS

    def weight_copies(e, slot):
        return (pltpu.make_async_copy(wgu_hbm.at[layer, e], wgu_f.at[slot], sem.at[0, slot]),
                pltpu.make_async_copy(wd_hbm.at[layer, e], wd_f.at[slot], sem.at[1, slot]))

    @pl.when(i == 0)
    def _():
        for cp in weight_copies(blk_e_ref[0], blk_slot_ref[0]):
            cp.start()

    @pl.when(blk_first_ref[i] == 1)
    def _():
        slot = blk_slot_ref[i]
        for cp in weight_copies(blk_e_ref[i], slot):
            cp.wait()
        nxt = blk_next_ref[i]

        @pl.when(nxt >= 0)
        def _():
            for cp in weight_copies(nxt, 1 - slot):
                cp.start()

        wgu_b[...] = wgu_f[slot].astype(BF16)
        wd_b[...] = wd_f[slot].astype(BF16)

    for parts in range(1, MOE_TAIL_PARTS + 1):
        rows = parts * quantum

        @pl.when((n_valid > rows - quantum) & (n_valid <= rows))
        def _(rows=rows):
            y_ref[:rows] = _expert_rows(x_ref[:rows], n_valid, wgu_b, wd_b, bgu_ref[0, 0], bd_ref[0, 0])
            if rows < MOE_TM:
                y_ref[rows:] = jnp.zeros((MOE_TM - rows, ROW_WORDS), jnp.int32)

    @pl.when(n_valid == 0)
    def _():
        y_ref[...] = jnp.zeros_like(y_ref)


def _moe(layer, blk_meta, xs, w_gu, b_gu, w_down, b_down):
    def row_map(i, *_):
        return (i, 0)

    def bias_map(i, e, *_):
        return (layer, e[i], 0, 0)

    grid_spec = pltpu.PrefetchScalarGridSpec(
        num_scalar_prefetch=5,
        grid=(MOE_BLOCKS,),
        in_specs=[
            pl.BlockSpec((MOE_TM, ROW_WORDS), row_map),
            pl.BlockSpec(memory_space=pl.ANY),
            pl.BlockSpec((1, 1, 1, 2 * D_EXPERT), bias_map),
            pl.BlockSpec(memory_space=pl.ANY),
            pl.BlockSpec((1, 1, 1, D_MODEL), bias_map),
        ],
        out_specs=pl.BlockSpec((MOE_TM, ROW_WORDS), row_map),
        scratch_shapes=[pltpu.VMEM((2, D_MODEL, 2 * D_EXPERT), F32), pltpu.VMEM((2, D_EXPERT, D_MODEL), F32),
                        pltpu.VMEM((D_MODEL, 2 * D_EXPERT), BF16), pltpu.VMEM((D_EXPERT, D_MODEL), BF16),
                        pltpu.SemaphoreType.DMA((2, 2))],
    )
    return pl.pallas_call(
        functools.partial(_moe_kernel, layer=layer),
        grid_spec=grid_spec,
        out_shape=jax.ShapeDtypeStruct((MOE_ROWS, ROW_WORDS), jnp.int32),
        compiler_params=_cparams(1),
        name="moe_experts",
    )(*blk_meta, xs, w_gu, b_gu.reshape(DEPTH, N_EXPERTS, 1, 2 * D_EXPERT),
      w_down, b_down.reshape(DEPTH, N_EXPERTS, 1, D_MODEL))


def _route(route, counts):
    experts = jnp.arange(N_EXPERTS, dtype=jnp.int32)
    padded = (counts + MOE_TM - 1) // MOE_TM * MOE_TM
    pad_end = jnp.cumsum(padded)
    pad_start = pad_end - padded
    e = route[:, 0:TOP_K]
    onehot = e[:, :, None] == experts[None, None, :]
    dest = jnp.sum(jnp.where(onehot, pad_start[None, None, :], 0), axis=-1) + route[:, TOP_K:2 * TOP_K]
    blk_row0 = jnp.arange(MOE_BLOCKS, dtype=jnp.int32) * MOE_TM
    blk_e = jnp.minimum(jnp.sum((pad_end[None, :] <= blk_row0[:, None]).astype(jnp.int32), axis=1),
                        N_EXPERTS - 1)
    on = blk_row0 < pad_end[-1]
    n_on = jnp.sum(on.astype(jnp.int32))
    blk_onehot = blk_e[:, None] == experts[None, :]
    row_end = jnp.sum(jnp.where(blk_onehot, (pad_start + counts)[None, :], 0), axis=1)
    blk_rows = jnp.where(on, jnp.clip(row_end - blk_row0, 0, MOE_TM), 0).astype(jnp.int32)
    last_e = jnp.sum(jnp.where(jnp.arange(MOE_BLOCKS) == n_on - 1, blk_e, 0))
    blk_e = jnp.where(on, blk_e, last_e).astype(jnp.int32)
    prev = jnp.concatenate([jnp.full((1,), -1, jnp.int32), blk_e[:-1]])
    blk_first = (blk_e != prev).astype(jnp.int32)
    blk_slot = ((jnp.cumsum(blk_first) - 1) % 2).astype(jnp.int32)
    later_used = (experts[None, :] > experts[:, None]) & (counts[None, :] > 0)
    next_used = jnp.min(jnp.where(later_used, experts[None, :], N_EXPERTS), axis=1)
    next_used = jnp.where(next_used == N_EXPERTS, -1, next_used)
    blk_next = jnp.sum(jnp.where(blk_e[:, None] == experts[None, :], next_used[None, :], 0), axis=1)
    return dest, (blk_e, blk_first, blk_rows, blk_slot, blk_next.astype(jnp.int32))


def _combine_kernel(y_ref, gate_ref, x1_ref, mod_ref, g_ref, *o_refs):
    i = pl.program_id(0)
    m = mod_ref[0]
    gate = gate_ref[...]
    acc_lo = None
    for k in range(TOP_K):
        lo, hi = _unpack_pairs(y_ref[k])
        gk = gate[:, k:k + 1]
        acc_lo = gk * lo if acc_lo is None else acc_lo + gk * lo
        acc_hi = gk * hi if k == 0 else acc_hi + gk * hi
    acc = jnp.concatenate([acc_lo, acc_hi], axis=1)
    out = x1_ref[...] + m[5:6] * _rms(acc, g_ref[...][3:4])
    if len(o_refs) == 1:
        o_refs[0][...] = out
    else:
        @pl.when(i < NPT)
        def _():
            o_refs[0][...] = out

        @pl.when(i >= NPT)
        def _():
            o_refs[1][...] = out


def _combine(yg, gates, x1, mod_l, g_l, split_out):
    if split_out:
        out_specs = [_PROMPT_SPEC, _SAMPLE_SPEC]
        out_shape = [jax.ShapeDtypeStruct((NP_TOK, D_MODEL), F32), jax.ShapeDtypeStruct((NS_TOK, D_MODEL), F32)]
    else:
        out_specs = [_tok_spec(D_MODEL)]
        out_shape = [jax.ShapeDtypeStruct((N_TOK, D_MODEL), F32)]
    return pl.pallas_call(
        _combine_kernel,
        grid=(N_TILES,),
        in_specs=[pl.BlockSpec((TOP_K, TM, ROW_WORDS), lambda i: (0, i, 0)), _tok_spec(LANES),
                  _tok_spec(D_MODEL), _MOD_SPEC, _const_spec((4, D_MODEL))],
        out_specs=out_specs,
        out_shape=out_shape,
        compiler_params=_cparams(1),
        name="combine",
    )(yg, gates, x1, mod_l, g_l)


def _ffn(layer, o_cat, xs, mod_l, g_l, w_out, w_router, b_router, w_gu, b_gu, w_down, b_down, split_out):
    w_r = jnp.pad(w_router, ((0, 0), (0, LANES - N_EXPERTS))).astype(BF16)
    b_r = jnp.pad(b_router, (0, LANES - N_EXPERTS)).reshape(1, LANES)
    x1, h2p, route, gate_slab, counts = _postmix(o_cat, xs, mod_l, g_l, w_out.astype(BF16), w_r, b_r)
    dest, blk_meta = _route(route, counts[0, :N_EXPERTS])
    n_chunks = N_TOK // (SC_WORKERS * SC_ROWS)
    idx_d = dest.reshape(SC_WORKERS, n_chunks, SC_ROWS, TOP_K).transpose(0, 1, 3, 2).reshape(
        SC_WORKERS, n_chunks * TOP_K, SC_ROWS)
    rows = _sc_dispatch(h2p, idx_d)
    ys = _moe(layer, blk_meta, rows, w_gu, b_gu, w_down, b_down)
    idx_c = dest.T.reshape(SC_WORKERS, TOP_K * n_chunks, SC_ROWS)
    yg = _sc_collect(ys, idx_c).reshape(TOP_K, N_TOK, ROW_WORDS)
    return _combine(yg, gate_slab, x1, mod_l, g_l, split_out)


def _pad_heads(w, n_heads, width, keep):
    k = w.shape[0]
    w = w.reshape(k, n_heads, width)[:, :, :keep]
    return jnp.pad(w, ((0, 0), (0, 0), (0, LANES - keep))).reshape(k, n_heads * LANES)


def _pe_slab(x):
    return jnp.pad(x, [(0, 0)] * (x.ndim - 1) + [(MLA_NOPE, LANES - MLA_NOPE - MLA_ROPE)])


_SWA_ORDER = np.array([0, 4, 1, 5, 2, 6, 3, 7])


def kernel(x_prompt, x_sample, cache_mla_ckv, cache_mla_krope, cache_diff_k, cache_diff_v, cache_swa_k, cache_swa_v, cache_na_k, cache_na_v, c, c_ctx, w_mod, b_mod, norm_g, w_in0, mla_q_norm, w_uq, mla_kv_norm, w_ukv, diff_lambda, diff_norm, w_out0, w_in1, swa_sink, na_rpb, w_out1, w_router, b_router, w_gu, b_gu, w_down, b_down):
    xs = (x_prompt.reshape(NP_TOK, D_MODEL), x_sample.reshape(NS_TOK, D_MODEL))
    cond = jnp.concatenate([c_ctx[None, :], c, jnp.zeros((16 - 1 - DEC_BATCH, D_MODEL), F32)], axis=0)
    mod = _modulation(cond, w_mod, b_mod).reshape(DEPTH, 16, 6, D_MODEL)
    t64, t32 = _rope_tables()
    states = {}
    for l in range(DEPTH):
        i = l // 2
        g_l = norm_g[l]
        mod_l = mod[l]
        if l % 2 == 0:
            lam_init = 0.8 - 0.6 * math.exp(-0.3 * l)
            wi = w_in0[i]
            w_in_p = jnp.concatenate(
                [wi[:, 0:640], wi[:, 672:2208], _pe_slab(wi[:, 640:672])], axis=1).astype(BF16)
            w_uq_p = _pad_heads(w_uq[i], MLA_HEADS, MLA_NOPE + MLA_ROPE, MLA_NOPE + MLA_ROPE).astype(BF16)
            w_k_p = _pad_heads(w_ukv[i], MLA_HEADS, MLA_NOPE + MLA_V, MLA_NOPE).astype(BF16)
            w_v = w_ukv[i].reshape(MLA_KV_RANK, MLA_HEADS, MLA_NOPE + MLA_V)[:, :, MLA_NOPE:].reshape(
                MLA_KV_RANK, MLA_HEADS * MLA_V).astype(BF16)
            (q, k, v, dq, dk, dv, ckv_st, kpe_st, dk_st, dv_st) = _premix0(
                *xs, mod_l, g_l[0:1], w_in_p, mla_q_norm[i][None, :], w_uq_p, mla_kv_norm[i][None, :],
                w_k_p, w_v, t32, t64)
            states['mla_ckv'] = ckv_st.reshape(BATCH, 1, SEQ, MLA_KV_RANK)
            states['mla_krope'] = kpe_st[:, MLA_NOPE:MLA_NOPE + MLA_ROPE].reshape(BATCH, 1, SEQ, MLA_ROPE)
            states['diff_k'] = dk_st.reshape(BATCH, 1, SEQ, DIFF_HEADS, 2 * DIFF_DH)
            states['diff_v'] = dv_st.reshape(BATCH, 1, SEQ, DIFF_HEADS, 2 * DIFF_DH)
            kc, vc = _mla_cache(cache_mla_ckv[:, i].reshape(DEC_BATCH * PAST_LEN, MLA_KV_RANK),
                                _pe_slab(cache_mla_krope[:, i].reshape(DEC_BATCH * PAST_LEN, MLA_ROPE)),
                                w_k_p, w_v)
            dkc = cache_diff_k[:, i].reshape(DEC_BATCH * PAST_LEN, 512).astype(BF16)
            dvc = cache_diff_v[:, i].reshape(DEC_BATCH * PAST_LEN, 512).astype(BF16)
            lam = diff_lambda[i]
            sub_g = diff_norm[i][None, :]
            o_p = _attn_ab(q, dq, (k, v, dk, dv), None, lam, sub_g, lam_init,
                           n_batch=BATCH, t_len=SEQ, tq=SEQ, tok_off=0)
            o_cat = _attn_ab(q, dq, (k, v, dk, dv), (kc, vc, dkc, dvc), lam, sub_g, lam_init,
                             n_batch=DEC_BATCH, t_len=DEC_SEQ, tq=TQ_AB, tok_off=NP_TOK, out_init=o_p)
            w_out = w_out0[i]
        else:
            wi = w_in1[i]
            sq_cols = (_SWA_ORDER[:, None] * HEAD_DIM + np.arange(HEAD_DIM)[None, :]).reshape(-1)
            w_in_p = jnp.concatenate([wi[:, sq_cols], wi[:, 512:]], axis=1).astype(BF16)
            (sq, sk, sv, nq, nk, nv, sk_st, sv_st, nk_st, nv_st) = _premix1(xs[0], mod_l, g_l[0:1], w_in_p, t64)
            states['swa_k'] = sk_st.reshape(BATCH, 1, SEQ, SWA_KV_HEADS, HEAD_DIM)
            states['swa_v'] = sv_st.reshape(BATCH, 1, SEQ, SWA_KV_HEADS, HEAD_DIM)
            states['na_k'] = nk_st.reshape(BATCH, 1, SEQ, NA_HEADS, HEAD_DIM)
            states['na_v'] = nv_st.reshape(BATCH, 1, SEQ, NA_HEADS, HEAD_DIM)
            skc = cache_swa_k[:, i].reshape(DEC_BATCH * PAST_LEN, 128).astype(BF16)
            svc = cache_swa_v[:, i].reshape(DEC_BATCH * PAST_LEN, 128).astype(BF16)
            nkc = cache_na_k[:, i].reshape(DEC_BATCH * PAST_LEN, 512).astype(BF16)
            nvc = cache_na_v[:, i].reshape(DEC_BATCH * PAST_LEN, 512).astype(BF16)
            sink = swa_sink[i]
            o_p = _attn_cd_prompt(sink, sq, sk, sv, nq, nk, nv)
            o_cat = _attn_cd_sample(o_p, sink, sq, sk, sv, nq, nk, nv, skc, svc, nkc, nvc, _na_bias(na_rpb[i]))
            wo = w_out1[i]
            w_out = jnp.concatenate([wo[sq_cols], wo[512:]], axis=0)
        xs = _ffn(l, o_cat, xs, mod_l, g_l, w_out, w_router[l], b_router[l], w_gu, b_gu, w_down, b_down,
                  split_out=(l == DEPTH - 1))
    return (xs[0].reshape(BATCH, SEQ, D_MODEL), xs[1].reshape(DEC_BATCH, DEC_SEQ, D_MODEL),
            states['mla_ckv'], states['mla_krope'], states['diff_k'], states['diff_v'],
            states['swa_k'], states['swa_v'], states['na_k'], states['na_v'])
```

```python
import functools
import math

import numpy as np
import jax
import jax.numpy as jnp
from jax import lax
from jax.experimental import pallas as pl
from jax.experimental.pallas import tpu as pltpu
from jax.experimental.pallas import tpu_sc as plsc

F32 = jnp.float32
BF16 = jnp.bfloat16

D_MODEL = 1024
BATCH = 16
SEQ = 256
DEPTH = 2
DEC_BATCH = 8
DEC_SEQ = 2048
PAST_LEN = 256
GRID_W = 64
HEAD_DIM = 64
ROPE_THETA = 10000.0
EPS = 1e-6
NEG = -1e30

MLA_HEADS = 8
MLA_Q_RANK = 384
MLA_KV_RANK = 256
MLA_NOPE = 64
MLA_ROPE = 32
MLA_V = 64
DIFF_HEADS = 4
DIFF_DH = 64
SWA_HEADS = 8
SWA_KV_HEADS = 2
SWA_WINDOW = 128
NA_HEADS = 8
NA_WIN_ROWS = 8
NA_WIN_COLS = 16
N_EXPERTS = 32
TOP_K = 4
D_EXPERT = 1024
SWIGLU_LIMIT = 7.0
SWIGLU_ALPHA = 1.702

LANES = 128
NP_TOK = BATCH * SEQ
NS_TOK = DEC_BATCH * DEC_SEQ
N_TOK = NP_TOK + NS_TOK
TM = 512
NPT = NP_TOK // TM
TILES_PER_SAMPLE = DEC_SEQ // TM
N_TILES = N_TOK // TM
TQ = 256
TQ_AB = 256
MOE_TM = 512
MOE_TAIL_PARTS = 4
MOE_ROWS = ((N_TOK * TOP_K + N_EXPERTS * (MOE_TM - 1)) // MOE_TM + 1) * MOE_TM
MOE_BLOCKS = MOE_ROWS // MOE_TM
NA_TILE_ROWS = TQ // GRID_W
NA_KEY_ROWS = 12
VMEM_LIMIT = 56 * 1024 * 1024


def _cparams(n_axes, vmem=VMEM_LIMIT):
    return pltpu.CompilerParams(dimension_semantics=("arbitrary",) * n_axes,
                                vmem_limit_bytes=vmem)


def _rms(x, g):
    return x * lax.rsqrt(jnp.mean(x * x, axis=-1, keepdims=True) + EPS) * g


def _dot(a, b):
    return jnp.dot(a, b, preferred_element_type=F32)


def _dot_nt(a, b):
    return lax.dot_general(a, b, (((1,), (1,)), ((), ())), preferred_element_type=F32)


def _rope(x, cos, sin_a, sin_b, half):
    return (x * cos + pltpu.roll(x, LANES - half, 1) * sin_a + pltpu.roll(x, half, 1) * sin_b)


def _mod_row(i):
    return jnp.where(i < NPT, 0, 1 + (i - NPT) // TILES_PER_SAMPLE)


def _rope_blk(i):
    return jnp.where(i < NPT, TILES_PER_SAMPLE, (i - NPT) % TILES_PER_SAMPLE)


def _mod_kernel(c_ref, w_ref, b_ref, o_ref):
    c = c_ref[...]
    s = (c * jax.nn.sigmoid(c)).astype(BF16)
    o_ref[0] = _dot(s, w_ref[0].astype(BF16)) + b_ref[0]


def _modulation(cond, w_mod, b_mod):
    nb = 1024
    return pl.pallas_call(
        _mod_kernel,
        grid=(DEPTH, 6 * D_MODEL // nb),
        in_specs=[
            pl.BlockSpec((16, D_MODEL), lambda l, n: (0, 0)),
            pl.BlockSpec((1, D_MODEL, nb), lambda l, n: (l, 0, n)),
            pl.BlockSpec((1, 1, nb), lambda l, n: (l, 0, n)),
        ],
        out_specs=pl.BlockSpec((1, 16, nb), lambda l, n: (l, 0, n)),
        out_shape=jax.ShapeDtypeStruct((DEPTH, 16, 6 * D_MODEL), F32),
        compiler_params=_cparams(2),
        name="modulation",
    )(cond, w_mod, b_mod.reshape(DEPTH, 1, 6 * D_MODEL))


def _rope_tables():
    t = jnp.arange(DEC_SEQ)
    rows = (t // GRID_W).astype(F32)
    cols = (t % GRID_W).astype(F32)

    def angles(r):
        n = r // 4
        inv = ROPE_THETA ** (-jnp.arange(n, dtype=F32) / n)
        return jnp.concatenate([rows[:, None] * inv[None], cols[:, None] * inv[None]], axis=-1)

    def finish(cos, sa, sb):
        ident = (jnp.ones((TM, LANES), F32), jnp.zeros((TM, LANES), F32), jnp.zeros((TM, LANES), F32))
        return tuple(jnp.concatenate([a, b], axis=0) for a, b in zip((cos, sa, sb), ident))

    a64 = angles(64)
    c, s, z = jnp.cos(a64), jnp.sin(a64), jnp.zeros_like(a64)
    t64 = finish(jnp.concatenate([c, c, c, c], -1), jnp.concatenate([-s, z, -s, z], -1),
                 jnp.concatenate([z, s, z, s], -1))
    a32 = angles(32)
    c, s, z = jnp.cos(a32), jnp.sin(a32), jnp.zeros_like(a32)
    one64 = jnp.ones((DEC_SEQ, 64), F32)
    z64 = jnp.zeros((DEC_SEQ, 64), F32)
    z32 = jnp.zeros((DEC_SEQ, 32), F32)
    t32 = finish(jnp.concatenate([one64, c, c, z32], -1), jnp.concatenate([z64, -s, z, z32], -1),
                 jnp.concatenate([z64, z, s, z32], -1))
    return t64, t32


LOG2E = math.log2(math.e)
_MLA_SCALE = (MLA_NOPE + MLA_ROPE) ** -0.5 * LOG2E
_QSCALE = HEAD_DIM ** -0.5 * LOG2E


def _premix0_kernel(xp_ref, xs_ref, mod_ref, g_ref, win_ref, qn_ref, wuq_ref, kvn_ref, wk_ref, wv_ref,
                    c32_ref, sa32_ref, sb32_ref, c64_ref, sa64_ref, sb64_ref,
                    q_ref, k_ref, v_ref, dq_ref, dk_ref, dv_ref,
                    ckv_st, kpe_st, dk_st, dv_st):
    i = pl.program_id(0)
    m = mod_ref[0]
    h = _rms(_pick_x(i, xp_ref, xs_ref), g_ref[...]) * (1.0 + m[1:2]) + m[0:1]
    proj = _dot(h.astype(BF16), win_ref[...])
    q_a = proj[:, 0:384]
    kv_a = proj[:, 384:640]
    dq = proj[:, 640:1152]
    dk = proj[:, 1152:1664]
    dv = proj[:, 1664:2176]
    pe = proj[:, 2176:2304]
    q = _dot(_rms(q_a, qn_ref[...]).astype(BF16), wuq_ref[...])
    ckv = _rms(kv_a, kvn_ref[...])
    ckv_b = ckv.astype(BF16)
    kn = _dot(ckv_b, wk_ref[...])
    v_ref[...] = _dot(ckv_b, wv_ref[...]).astype(BF16)
    c32, sa32, sb32 = c32_ref[...], sa32_ref[...], sb32_ref[...]
    c64, sa64, sb64 = c64_ref[...], sa64_ref[...], sb64_ref[...]
    pe_r = _rope(pe, c32, sa32, sb32, MLA_ROPE // 2)
    for hd in range(MLA_HEADS):
        sl = slice(LANES * hd, LANES * (hd + 1))
        q_ref[:, sl] = (_rope(q[:, sl], c32, sa32, sb32, MLA_ROPE // 2) * _MLA_SCALE).astype(BF16)
        k_ref[:, sl] = (kn[:, sl] + pe_r).astype(BF16)
    for hd in range(DIFF_HEADS):
        sl = slice(LANES * hd, LANES * (hd + 1))
        dq_ref[:, sl] = (_rope(dq[:, sl], c64, sa64, sb64, DIFF_DH // 2) * _QSCALE).astype(BF16)
        dk_ref[:, sl] = _rope(dk[:, sl], c64, sa64, sb64, DIFF_DH // 2).astype(BF16)
    dv_ref[...] = dv.astype(BF16)

    @pl.when(i < NPT)
    def _():
        ckv_st[...] = ckv
        kpe_st[...] = pe
        dk_st[...] = dk
        dv_st[...] = dv


def _premix1_kernel(x_ref, mod_ref, g_ref, win_ref, c64_ref, sa64_ref, sb64_ref,
                    sq_ref, sk_ref, sv_ref, nq_ref, nk_ref, nv_ref,
                    sk_st, sv_st, nk_st, nv_st):
    i = pl.program_id(0)
    m = mod_ref[0]
    h = _rms(x_ref[...], g_ref[...]) * (1.0 + m[1:2]) + m[0:1]
    proj = _dot(h.astype(BF16), win_ref[...])
    sq = proj[:, 0:512]
    sk = proj[:, 512:640]
    sv = proj[:, 640:768]
    nq = proj[:, 768:1280]
    nk = proj[:, 1280:1792]
    nv = proj[:, 1792:2304]
    c64, sa64, sb64 = c64_ref[...], sa64_ref[...], sb64_ref[...]
    for hd in range(4):
        sl = slice(LANES * hd, LANES * (hd + 1))
        sq_ref[:, sl] = (_rope(sq[:, sl], c64, sa64, sb64, HEAD_DIM // 2) * _QSCALE).astype(BF16)
    sk_ref[...] = _rope(sk, c64, sa64, sb64, HEAD_DIM // 2).astype(BF16)
    sv_ref[...] = sv.astype(BF16)
    nq_ref[...] = (nq * _QSCALE).astype(BF16)
    nk_ref[...] = nk.astype(BF16)
    nv_ref[...] = nv.astype(BF16)

    @pl.when(i < NPT)
    def _():
        sk_st[...] = sk
        sv_st[...] = sv
        nk_st[...] = nk
        nv_st[...] = nv


def _tok_spec(width):
    return pl.BlockSpec((TM, width), lambda i: (i, 0))


_PROMPT_SPEC = pl.BlockSpec((TM, D_MODEL), lambda i: (jnp.minimum(i, NPT - 1), 0))
_SAMPLE_SPEC = pl.BlockSpec((TM, D_MODEL), lambda i: (jnp.maximum(i - NPT, 0), 0))


def _pick_x(i, xp_ref, xs_ref):
    return jnp.where(i < NPT, xp_ref[...], xs_ref[...])


def _state_spec(width):
    return pl.BlockSpec((TM, width), lambda i: (jnp.minimum(i, NPT - 1), 0))


def _const_spec(shape):
    return pl.BlockSpec(shape, lambda i: (0,) * len(shape))


_MOD_SPEC = pl.BlockSpec((1, 6, D_MODEL), lambda i: (_mod_row(i), 0, 0))
_ROPE_SPEC = pl.BlockSpec((TM, LANES), lambda i: (_rope_blk(i), 0))


def _premix0(xp, xs, mod_l, g0, w_in_p, q_norm, w_uq_p, kv_norm, w_k_p, w_v, t32, t64):
    outs = [(N_TOK, 1024, BF16), (N_TOK, 1024, BF16), (N_TOK, 512, BF16), (N_TOK, 512, BF16),
            (N_TOK, 512, BF16), (N_TOK, 512, BF16),
            (NP_TOK, 256, F32), (NP_TOK, 128, F32), (NP_TOK, 512, F32), (NP_TOK, 512, F32)]
    return pl.pallas_call(
        _premix0_kernel,
        grid=(N_TILES,),
        in_specs=[_PROMPT_SPEC, _SAMPLE_SPEC, _MOD_SPEC, _const_spec((1, D_MODEL)),
                  _const_spec((D_MODEL, 2304)), _const_spec((1, MLA_Q_RANK)),
                  _const_spec((MLA_Q_RANK, 1024)), _const_spec((1, MLA_KV_RANK)),
                  _const_spec((MLA_KV_RANK, 1024)), _const_spec((MLA_KV_RANK, 512))]
                 + [_ROPE_SPEC] * 6,
        out_specs=[_tok_spec(w) for (_, w, _) in outs[:6]] + [_state_spec(w) for (_, w, _) in outs[6:]],
        out_shape=[jax.ShapeDtypeStruct((n, w), dt) for (n, w, dt) in outs],
        compiler_params=_cparams(1),
        name="premix_ab",
    )(xp, xs, mod_l, g0, w_in_p, q_norm, w_uq_p, kv_norm, w_k_p, w_v, *t32, *t64)


def _premix1(x, mod_l, g0, w_in_p, t64):
    outs = [(N_TOK, 512, BF16), (N_TOK, 128, BF16), (N_TOK, 128, BF16), (N_TOK, 512, BF16),
            (N_TOK, 512, BF16), (N_TOK, 512, BF16),
            (NP_TOK, 128, F32), (NP_TOK, 128, F32), (NP_TOK, 512, F32), (NP_TOK, 512, F32)]
    return pl.pallas_call(
        _premix1_kernel,
        grid=(N_TILES,),
        in_specs=[_tok_spec(D_MODEL), _MOD_SPEC, _const_spec((1, D_MODEL)),
                  _const_spec((D_MODEL, 2304))] + [_ROPE_SPEC] * 3,
        out_specs=[_tok_spec(w) for (_, w, _) in outs[:6]] + [_state_spec(w) for (_, w, _) in outs[6:]],
        out_shape=[jax.ShapeDtypeStruct((n, w), dt) for (n, w, dt) in outs],
        compiler_params=_cparams(1),
        name="premix_cd",
    )(x, mod_l, g0, w_in_p, *t64)


def _mla_cache_kernel(ckv_ref, pe_ref, wk_ref, wv_ref, k_ref, v_ref):
    c = ckv_ref[...].astype(BF16)
    kn = _dot(c, wk_ref[...])
    v_ref[...] = _dot(c, wv_ref[...]).astype(BF16)
    pe = pe_ref[...]
    for hd in range(MLA_HEADS):
        sl = slice(LANES * hd, LANES * (hd + 1))
        k_ref[:, sl] = (kn[:, sl] + pe).astype(BF16)


def _mla_cache(ckv, pe_slab, w_k_p, w_v):
    n = ckv.shape[0]
    tm = 512
    return pl.pallas_call(
        _mla_cache_kernel,
        grid=(n // tm,),
        in_specs=[pl.BlockSpec((tm, MLA_KV_RANK), lambda i: (i, 0)),
                  pl.BlockSpec((tm, LANES), lambda i: (i, 0)),
                  _const_spec((MLA_KV_RANK, 1024)), _const_spec((MLA_KV_RANK, 512))],
        out_specs=[pl.BlockSpec((tm, 1024), lambda i: (i, 0)), pl.BlockSpec((tm, 512), lambda i: (i, 0))],
        out_shape=[jax.ShapeDtypeStruct((n, 1024), BF16), jax.ShapeDtypeStruct((n, 512), BF16)],
        compiler_params=_cparams(1),
        name="mla_cache",
    )(ckv, pe_slab, w_k_p, w_v)


def _softmax_pv(scores, values, sink=None):
    m = jnp.max(scores[0], axis=-1, keepdims=True)
    for s in scores[1:]:
        m = jnp.maximum(m, jnp.max(s, axis=-1, keepdims=True))
    if sink is not None:
        m = jnp.maximum(m, sink)
    l = None
    o = None
    for s, v in zip(scores, values):
        p = jnp.exp2(s - m)
        ls = jnp.sum(p, axis=-1, keepdims=True)
        os_ = _dot(p.astype(BF16), v)
        l = ls if l is None else l + ls
        o = os_ if o is None else o + os_
    if sink is not None:
        l = l + jnp.exp2(sink - m)
    return o * (1.0 / l)


def _pipelined_units(units):
    pending = units[0][0](units[0][2])
    for n, (_, finish, arg) in enumerate(units):
        following = units[n + 1][0](units[n + 1][2]) if n + 1 < len(units) else None
        finish(arg, pending)
        pending = following


def _lane_lo(shape):
    return lax.broadcasted_iota(jnp.int32, shape, 1) < (LANES // 2)


def _split_halves(qb):
    lo = _lane_lo(qb.shape)
    zero = jnp.zeros_like(qb)
    return jnp.where(lo, qb, zero), jnp.where(lo, zero, qb)


def _attn_ab_kernel(*refs, n_pieces, lam_init, aliased):
    if aliased:
        refs = refs[1:]
    q_ref, dq_ref = refs[0], refs[1]
    pieces = [refs[2 + 4 * p: 6 + 4 * p] for p in range(n_pieces)]
    lam_ref, subg_ref, o_ref = refs[2 + 4 * n_pieces:]
    lam = lam_ref[...]
    lam_full = (jnp.exp(jnp.sum(lam[0:1] * lam[1:2], axis=-1, keepdims=True))
                - jnp.exp(jnp.sum(lam[2:3] * lam[3:4], axis=-1, keepdims=True)) + lam_init)
    lo = _lane_lo((q_ref.shape[0], LANES))
    subg = subg_ref[...]
    tq = q_ref.shape[0]
    def mla_scores(hd):
        sl = slice(LANES * hd, LANES * (hd + 1))
        qh = q_ref[:, sl]
        return [_dot_nt(qh, k_ref[:, sl]) for (k_ref, _, _, _) in pieces]

    def diff_scores(hd):
        sl = slice(LANES * hd, LANES * (hd + 1))
        qq = jnp.concatenate(_split_halves(dq_ref[:, sl]), axis=0)
        return [_dot_nt(qq, dk_ref[:, sl]) for (_, _, dk_ref, _) in pieces]

    pair = {}

    def mla_finish(hd, scores):
        j = hd // 2
        vals = [v_ref[:, LANES * j: LANES * (j + 1)] for (_, v_ref, _, _) in pieces]
        pair[hd % 2] = _softmax_pv(scores, vals)
        if hd % 2 == 1:
            o_ref[:, LANES * j: LANES * (j + 1)] = jnp.where(lo, pair[0], pair[1]).astype(BF16)

    def diff_finish(hd, scores):
        sl = slice(LANES * hd, LANES * (hd + 1))
        oo = _softmax_pv(scores, [dv_ref[:, sl] for (_, _, _, dv_ref) in pieces])
        od = _rms(oo[:tq] - lam_full * oo[tq:], subg) * (1.0 - lam_init)
        o_ref[:, 512 + LANES * hd: 512 + LANES * (hd + 1)] = od.astype(BF16)

    units = []
    for j in range(DIFF_HEADS):
        units += [(mla_scores, mla_finish, 2 * j), (mla_scores, mla_finish, 2 * j + 1),
                  (diff_scores, diff_finish, j)]
    _pipelined_units(units)


def _attn_ab(q, dq, new_kv, cache_kv, lam, sub_g, lam_init, *, n_batch, t_len, tq, tok_off, out_init=None):
    nq = t_len // tq
    q_off = tok_off // tq
    b_off = tok_off // t_len
    widths = (1024, 512, 512, 512)
    in_specs = [pl.BlockSpec((tq, 1024), lambda b, i: (q_off + b * nq + i, 0)),
                pl.BlockSpec((tq, 512), lambda b, i: (q_off + b * nq + i, 0))]
    args = [q, dq]
    for w, a in zip(widths, new_kv):
        in_specs.append(pl.BlockSpec((t_len, w), lambda b, i: (b_off + b, 0)))
        args.append(a)
    n_pieces = 1
    if cache_kv is not None:
        n_pieces = 2
        for w, a in zip(widths, cache_kv):
            in_specs.append(pl.BlockSpec((PAST_LEN, w), lambda b, i: (b, 0)))
            args.append(a)
    in_specs += [pl.BlockSpec((4, DIFF_DH), lambda b, i: (0, 0)),
                 pl.BlockSpec((1, 2 * DIFF_DH), lambda b, i: (0, 0))]
    args += [lam, sub_g]
    aliases = {}
    if out_init is not None:
        in_specs = [pl.BlockSpec(memory_space=pl.ANY)] + in_specs
        args = [out_init] + args
        aliases = {0: 0}
    return pl.pallas_call(
        functools.partial(_attn_ab_kernel, n_pieces=n_pieces, lam_init=lam_init, aliased=out_init is not None),
        grid=(n_batch, nq),
        in_specs=in_specs,
        out_specs=pl.BlockSpec((tq, 1024), lambda b, i: (q_off + b * nq + i, 0)),
        out_shape=jax.ShapeDtypeStruct((N_TOK, 1024), BF16),
        input_output_aliases=aliases,
        compiler_params=_cparams(2),
        name="attn_ab_%d" % n_pieces,
    )(*args)


def _gqa_stacks(sq_ref, sink_ref):
    tq = sq_ref.shape[0]
    halves = [_split_halves(sq_ref[:, LANES * j: LANES * (j + 1)]) for j in range(4)]
    q_stacks = [jnp.concatenate([halves[j][kvh] for j in range(4)], axis=0) for kvh in range(SWA_KV_HEADS)]
    sinks = [jnp.concatenate([jnp.full((tq, 1), sink_ref[4 * kvh + j] * LOG2E, F32) for j in range(4)], axis=0)
             for kvh in range(SWA_KV_HEADS)]
    return q_stacks, sinks


def _attn_cd_prompt_kernel(sink_ref, sq_ref, sk_ref, sv_ref, nq_ref, nk_ref, nv_ref, o_ref):
    tq = sq_ref.shape[0]
    lo = _lane_lo((tq, LANES))
    sk = sk_ref[...]
    sv = sv_ref[...]
    for j in range(4):
        sl = slice(LANES * j, LANES * (j + 1))
        q_lo, q_hi = _split_halves(sq_ref[:, sl])
        o_lo = _softmax_pv([_dot_nt(q_lo, sk)], [sv], sink=sink_ref[j] * LOG2E)
        o_hi = _softmax_pv([_dot_nt(q_hi, sk)], [sv], sink=sink_ref[j + 4] * LOG2E)
        o_ref[:, sl] = jnp.where(lo, o_lo, o_hi).astype(BF16)
    for j in range(4):
        sl = slice(LANES * j, LANES * (j + 1))
        q_lo, q_hi = _split_halves(nq_ref[:, sl])
        k = nk_ref[:, sl]
        v = nv_ref[:, sl]
        o_lo = _softmax_pv([_dot_nt(q_lo, k)], [v])
        o_hi = _softmax_pv([_dot_nt(q_hi, k)], [v])
        o_ref[:, 512 + LANES * j: 512 + LANES * (j + 1)] = jnp.where(lo, o_lo, o_hi).astype(BF16)


def _attn_cd_prompt(sink, sq, sk, sv, nq, nk, nv):
    def spec(w):
        return pl.BlockSpec((SEQ, w), lambda b: (b, 0))
    return pl.pallas_call(
        _attn_cd_prompt_kernel,
        grid=(BATCH,),
        in_specs=[pl.BlockSpec(memory_space=pltpu.SMEM), spec(512), spec(128), spec(128),
                  spec(512), spec(512), spec(512)],
        out_specs=spec(1024),
        out_shape=jax.ShapeDtypeStruct((N_TOK, 1024), BF16),
        compiler_params=_cparams(1),
        name="attn_cd_prompt",
    )(sink, sq, sk, sv, nq, nk, nv)


_SWA_KEYS = TQ + 2 * SWA_WINDOW


def _attn_cd_sample_kernel(init_ref, sink_ref, sq_ref, nq_ref, sk_ref, sv_ref, nk_ref, nv_ref,
                           skc_ref, svc_ref, nkc_ref, nvc_ref, bias_ref, o_ref):
    del init_ref
    qi = pl.program_id(1)
    lo = _lane_lo((TQ, LANES))
    ks = pl.multiple_of(jnp.clip(qi * TQ - SWA_WINDOW, 0, DEC_SEQ - _SWA_KEYS), SWA_WINDOW)
    k_win = sk_ref[pl.ds(ks, _SWA_KEYS), :]
    v_win = sv_ref[pl.ds(ks, _SWA_KEYS), :]
    q_pos = qi * TQ + (lax.broadcasted_iota(jnp.int32, (4 * TQ, _SWA_KEYS), 0) & (TQ - 1))
    k_pos = ks + lax.broadcasted_iota(jnp.int32, (4 * TQ, _SWA_KEYS), 1)
    in_win = jnp.abs(q_pos - k_pos) <= SWA_WINDOW
    skc = skc_ref[...]
    svc = svc_ref[...]
    q_stacks, sinks = _gqa_stacks(sq_ref, sink_ref)

    def window_scores(kvh):
        return [_dot_nt(q_stacks[kvh], skc), jnp.where(in_win, _dot_nt(q_stacks[kvh], k_win), NEG)]

    o_kv = {}

    def window_finish(kvh, scores):
        o_kv[kvh] = _softmax_pv(scores, [svc, v_win], sink=sinks[kvh])
        if kvh == SWA_KV_HEADS - 1:
            for j in range(4):
                rows = slice(TQ * j, TQ * (j + 1))
                o_ref[:, LANES * j: LANES * (j + 1)] = jnp.where(lo, o_kv[0][rows], o_kv[1][rows]).astype(BF16)

    n_rows = DEC_SEQ // GRID_W
    r0 = jnp.clip(qi * NA_TILE_ROWS - NA_WIN_ROWS // 2, 0, n_rows - NA_KEY_ROWS)
    kn = pl.multiple_of(r0 * GRID_W, GRID_W)

    def na_scores(hd):
        sl = slice(LANES * (hd // 2), LANES * (hd // 2 + 1))
        q_half = _split_halves(nq_ref[:, sl])[hd % 2]
        nk_win = nk_ref[pl.ds(kn, NA_KEY_ROWS * GRID_W), sl]
        return [_dot_nt(q_half, nkc_ref[:, sl]), _dot_nt(q_half, nk_win) + bias_ref[0, hd]]

    pair = {}

    def na_finish(hd, scores):
        j = hd // 2
        sl = slice(LANES * j, LANES * (j + 1))
        nv_win = nv_ref[pl.ds(kn, NA_KEY_ROWS * GRID_W), sl]
        pair[hd % 2] = _softmax_pv(scores, [nvc_ref[:, sl], nv_win])
        if hd % 2 == 1:
            o_ref[:, 512 + LANES * j: 512 + LANES * (j + 1)] = jnp.where(lo, pair[0], pair[1]).astype(BF16)

    _pipelined_units([(window_scores, window_finish, kvh) for kvh in range(SWA_KV_HEADS)]
                     + [(na_scores, na_finish, hd) for hd in range(NA_HEADS)])


def _attn_cd_sample(out_init, sink, sq, sk, sv, nq, nk, nv, skc, svc, nkc, nvc, bias):
    nq_t = DEC_SEQ // TQ
    q_off = NP_TOK // TQ
    b_off = NP_TOK // DEC_SEQ

    def qspec(w):
        return pl.BlockSpec((TQ, w), lambda b, i: (q_off + b * nq_t + i, 0))

    def kspec(w):
        return pl.BlockSpec((DEC_SEQ, w), lambda b, i: (b_off + b, 0))

    def cspec(w):
        return pl.BlockSpec((PAST_LEN, w), lambda b, i: (b, 0))

    n_keys = NA_KEY_ROWS * GRID_W
    bias_spec = pl.BlockSpec(
        (1, NA_HEADS, TQ, n_keys),
        lambda b, i: (jnp.where(i == 0, 0, jnp.where(i == nq_t - 1, 2, 1)), 0, 0, 0))
    return pl.pallas_call(
        _attn_cd_sample_kernel,
        grid=(DEC_BATCH, nq_t),
        in_specs=[pl.BlockSpec(memory_space=pl.ANY), pl.BlockSpec(memory_space=pltpu.SMEM),
                  qspec(512), qspec(512), kspec(128), kspec(128), kspec(512), kspec(512),
                  cspec(128), cspec(128), cspec(512), cspec(512), bias_spec],
        out_specs=pl.BlockSpec((TQ, 1024), lambda b, i: (q_off + b * nq_t + i, 0)),
        out_shape=jax.ShapeDtypeStruct((N_TOK, 1024), BF16),
        input_output_aliases={0: 0},
        compiler_params=_cparams(2),
        name="attn_cd_sample",
    )(out_init, sink, sq, nq, sk, sv, nk, nv, skc, svc, nkc, nvc, bias)


def _na_bias(rpb):
    n_rows = DEC_SEQ // GRID_W
    n_dr = 2 * NA_WIN_ROWS - 1
    n_dc = 2 * NA_WIN_COLS - 1
    c = np.arange(GRID_W)[:, None]
    kc = np.arange(GRID_W)[None, :]
    qs = np.clip(c - NA_WIN_COLS // 2, 0, GRID_W - NA_WIN_COLS)
    col_ok = (kc >= qs) & (kc < qs + NA_WIN_COLS)
    dc = np.clip(kc - c + NA_WIN_COLS - 1, 0, n_dc - 1)
    onehot = ((dc[None] == np.arange(n_dc)[:, None, None]) & col_ok[None]).astype(np.float32)
    blocks = jnp.einsum('hrd,dck->hrck', rpb.astype(F32) * LOG2E, onehot, precision=lax.Precision.HIGHEST)
    row_sel = np.zeros((3, n_dr, NA_TILE_ROWS, NA_KEY_ROWS), np.float32)
    for v, q_row0 in enumerate((0, NA_TILE_ROWS, n_rows - NA_TILE_ROWS)):
        k_row0 = min(max(q_row0 - NA_WIN_ROWS // 2, 0), n_rows - NA_KEY_ROWS)
        for ri in range(NA_TILE_ROWS):
            r = q_row0 + ri
            rs = min(max(r - NA_WIN_ROWS // 2, 0), n_rows - NA_WIN_ROWS)
            for kj in range(NA_KEY_ROWS):
                kr = k_row0 + kj
                if rs <= kr < rs + NA_WIN_ROWS:
                    row_sel[v, kr - r + NA_WIN_ROWS - 1, ri, kj] = 1.0
    tiles = jnp.einsum('vdrj,hdck->vhrcjk', row_sel, blocks, precision=lax.Precision.HIGHEST)
    valid = (row_sel.sum(axis=1) > 0)[:, :, None, :, None] & col_ok[None, None, :, None, :]
    tiles = jnp.where(valid[:, None], tiles, NEG)
    return tiles.reshape(3, NA_HEADS, TQ, NA_KEY_ROWS * GRID_W)


_HI_MASK = -65536


def _pack_pairs(x):
    w = x.shape[1] // 2
    r = x.astype(BF16).astype(F32)
    lo = lax.bitcast_convert_type(r[:, :w], jnp.int32)
    hi = lax.bitcast_convert_type(r[:, w:], jnp.int32)
    return (hi & _HI_MASK) | lax.shift_right_logical(lo, 16)


def _unpack_pairs(p):
    lo = lax.bitcast_convert_type(lax.shift_left(p, 16), F32)
    hi = lax.bitcast_convert_type(p & _HI_MASK, F32)
    return lo, hi


def _postmix_kernel(*refs, split_x):
    if split_x:
        o_ref, xp_ref, xs_ref = refs[:3]
        refs = refs[3:]
    else:
        o_ref, x_ref = refs[:2]
        refs = refs[2:]
    (mod_ref, g_ref, wout_ref, wr_ref, br_ref, tri_ref,
     x1_ref, h2_ref, route_ref, gate_ref, cnt_ref, run_ref) = refs
    i = pl.program_id(0)
    x = _pick_x(i, xp_ref, xs_ref) if split_x else x_ref[...]

    @pl.when(i == 0)
    def _():
        run_ref[...] = jnp.zeros_like(run_ref)

    m = mod_ref[0]
    g = g_ref[...]
    y = _dot(o_ref[...], wout_ref[...])
    x1 = x + m[2:3] * _rms(y, g[1:2])
    x1_ref[...] = x1
    h2 = _rms(x1, g[2:3]) * (1.0 + m[4:5]) + m[3:4]
    h2_ref[...] = _pack_pairs(h2)
    logits = _dot(h2.astype(BF16), wr_ref[...]) + br_ref[...]
    lane = lax.broadcasted_iota(jnp.int32, logits.shape, 1).astype(F32)
    cur = jnp.where(lane < N_EXPERTS, logits, -jnp.inf)
    tops, idxs = [], []
    for _ in range(TOP_K):
        mx = jnp.max(cur, axis=-1, keepdims=True)
        ix = jnp.min(jnp.where(cur == mx, lane, float(LANES)), axis=-1, keepdims=True)
        tops.append(mx)
        idxs.append(ix)
        cur = jnp.where(lane == ix, -jnp.inf, cur)
    es = [jnp.exp(t - tops[0]) for t in tops]
    inv = 1.0 / (es[0] + es[1] + es[2] + es[3])
    picked = jnp.zeros_like(logits)
    for k in range(TOP_K):
        picked = jnp.where(lane == idxs[k], 1.0, picked)
    before = _dot(tri_ref[...], picked.astype(BF16)) + run_ref[0:1, :]
    route = jnp.zeros_like(logits)
    gate_out = jnp.zeros_like(logits)
    for k in range(TOP_K):
        rank = jnp.sum(jnp.where(lane == idxs[k], before, 0.0), axis=-1, keepdims=True)
        route = jnp.where(lane == float(k), idxs[k], route)
        route = jnp.where(lane == float(TOP_K + k), rank, route)
        gate_out = jnp.where(lane == float(k), es[k] * inv, gate_out)
    route_ref[...] = route.astype(jnp.int32)
    gate_ref[...] = gate_out
    run_ref[...] = run_ref[...] + jnp.sum(picked, axis=0, keepdims=True)
    cnt_ref[...] = run_ref[...].astype(jnp.int32)


def _postmix(o_cat, xs, mod_l, g_l, w_out, w_r, b_r):
    tri = jnp.asarray(np.tril(np.ones((TM, TM), np.float32), -1), BF16)
    split_x = len(xs) == 2
    x_specs = [_PROMPT_SPEC, _SAMPLE_SPEC] if split_x else [_tok_spec(D_MODEL)]
    return pl.pallas_call(
        functools.partial(_postmix_kernel, split_x=split_x),
        grid=(N_TILES,),
        in_specs=[_tok_spec(1024)] + x_specs + [_MOD_SPEC, _const_spec((4, D_MODEL)),
                  _const_spec((1024, D_MODEL)), _const_spec((D_MODEL, LANES)), _const_spec((1, LANES)),
                  _const_spec((TM, TM))],
        out_specs=[_tok_spec(D_MODEL), _tok_spec(D_MODEL // 2), _tok_spec(LANES), _tok_spec(LANES),
                   _const_spec((8, LANES))],
        out_shape=[jax.ShapeDtypeStruct((N_TOK, D_MODEL), F32),
                   jax.ShapeDtypeStruct((N_TOK, D_MODEL // 2), jnp.int32),
                   jax.ShapeDtypeStruct((N_TOK, LANES), jnp.int32), jax.ShapeDtypeStruct((N_TOK, LANES), F32),
                   jax.ShapeDtypeStruct((8, LANES), jnp.int32)],
        scratch_shapes=[pltpu.VMEM((8, LANES), F32)],
        compiler_params=_cparams(1),
        name="postmix",
    )(o_cat, *xs, mod_l, g_l, w_out, w_r, b_r, tri)


SC_WORKERS = 32
SC_ROWS = 128
ROW_WORDS = D_MODEL // 2


def _sc_worker_id():
    return lax.axis_index("s") * 2 + lax.axis_index("c")


def _sc_dispatch(src, idx):
    n_chunks = N_TOK // (SC_WORKERS * SC_ROWS)
    mesh = plsc.VectorSubcoreMesh(core_axis_name="c", subcore_axis_name="s")

    @functools.partial(
        pl.kernel, mesh=mesh,
        out_type=jax.ShapeDtypeStruct((MOE_ROWS, ROW_WORDS), jnp.int32),
        scratch_types=[pltpu.VMEM((n_chunks * TOP_K, SC_ROWS), jnp.int32),
                       pltpu.VMEM((SC_ROWS, ROW_WORDS), jnp.int32), pltpu.SemaphoreType.DMA])
    def k(src_hbm, idx_hbm, out_hbm, idx_v, rows_v, sem):
        wid = _sc_worker_id()
        pltpu.sync_copy(idx_hbm.at[wid], idx_v)

        @pl.loop(0, n_chunks)
        def _(g):
            pltpu.sync_copy(src_hbm.at[pl.ds((wid * n_chunks + g) * SC_ROWS, SC_ROWS)], rows_v)
            copies = [pltpu.async_copy(rows_v, out_hbm.at[idx_v.at[g * TOP_K + kk]], sem)
                      for kk in range(TOP_K)]
            for cp in copies:
                cp.wait()

    return k(src, idx)


def _sc_collect(table, idx):
    n_chunks = idx.shape[1]
    mesh = plsc.VectorSubcoreMesh(core_axis_name="c", subcore_axis_name="s")

    @functools.partial(
        pl.kernel, mesh=mesh,
        out_type=jax.ShapeDtypeStruct((SC_WORKERS * n_chunks * SC_ROWS, ROW_WORDS), jnp.int32),
        scratch_types=[pltpu.VMEM((n_chunks, SC_ROWS), jnp.int32),
                       pltpu.VMEM((SC_ROWS, ROW_WORDS), jnp.int32), pltpu.SemaphoreType.DMA])
    def k(table_hbm, idx_hbm, out_hbm, idx_v, rows_v, sem):
        wid = _sc_worker_id()
        pltpu.sync_copy(idx_hbm.at[wid], idx_v)

        @pl.loop(0, n_chunks)
        def _(g):
            pltpu.async_copy(table_hbm.at[idx_v.at[g]], rows_v, sem).wait()
            pltpu.sync_copy(rows_v, out_hbm.at[pl.ds((wid * n_chunks + g) * SC_ROWS, SC_ROWS)])

    return k(table, idx)


def _expert_rows(words, n_valid, wgu_b, wd_b, bgu, bd):
    live = lax.broadcasted_iota(jnp.int32, words.shape, 0) < n_valid
    lo, hi = _unpack_pairs(jnp.where(live, words, 0))
    x = jnp.concatenate([lo, hi], axis=1).astype(BF16)
    gu = _dot(x, wgu_b[...]) + bgu
    g = jnp.minimum(gu[:, :D_EXPERT], SWIGLU_LIMIT)
    u = jnp.clip(gu[:, D_EXPERT:], -SWIGLU_LIMIT, SWIGLU_LIMIT)
    a = g * jax.nn.sigmoid(SWIGLU_ALPHA * g) * (u + 1.0)
    return _pack_pairs(_dot(a.astype(BF16), wd_b[...]) + bd)


def _moe_kernel(blk_e_ref, blk_first_ref, blk_rows_ref, blk_slot_ref, blk_next_ref,
                x_ref, wgu_hbm, bgu_ref, wd_hbm, bd_ref, y_ref,
                wgu_f, wd_f, wgu_b, wd_b, sem, *, layer):
    i = pl.program_id(0)
    n_valid = blk_rows_ref[i]
    quantum = MOE_TM // MOE_TAIL_PARTS

    def weight_copies(e, slot):
        return (pltpu.make_async_copy(wgu_hbm.at[layer, e], wgu_f.at[slot], sem.at[0, slot]),
                pltpu.make_async_copy(wd_hbm.at[layer, e], wd_f.at[slot], sem.at[1, slot]))

    @pl.when(i == 0)
    def _():
        for cp in weight_copies(blk_e_ref[0], blk_slot_ref[0]):
            cp.start()

    @pl.when(blk_first_ref[i] == 1)
    def _():
        slot = blk_slot_ref[i]
        for cp in weight_copies(blk_e_ref[i], slot):
            cp.wait()
        nxt = blk_next_ref[i]

        @pl.when(nxt >= 0)
        def _():
            for cp in weight_copies(nxt, 1 - slot):
                cp.start()

        wgu_b[...] = wgu_f[slot].astype(BF16)
        wd_b[...] = wd_f[slot].astype(BF16)

    for parts in range(1, MOE_TAIL_PARTS + 1):
        rows = parts * quantum

        @pl.when((n_valid > rows - quantum) & (n_valid <= rows))
        def _(rows=rows):
            y_ref[:rows] = _expert_rows(x_ref[:rows], n_valid, wgu_b, wd_b, bgu_ref[0, 0], bd_ref[0, 0])
            if rows < MOE_TM:
                y_ref[rows:] = jnp.zeros((MOE_TM - rows, ROW_WORDS), jnp.int32)

    @pl.when(n_valid == 0)
    def _():
        y_ref[...] = jnp.zeros_like(y_ref)


def _moe(layer, blk_meta, xs, w_gu, b_gu, w_down, b_down):
    def row_map(i, *_):
        return (i, 0)

    def bias_map(i, e, *_):
        return (layer, e[i], 0, 0)

    grid_spec = pltpu.PrefetchScalarGridSpec(
        num_scalar_prefetch=5,
        grid=(MOE_BLOCKS,),
        in_specs=[
            pl.BlockSpec((MOE_TM, ROW_WORDS), row_map),
            pl.BlockSpec(memory_space=pl.ANY),
            pl.BlockSpec((1, 1, 1, 2 * D_EXPERT), bias_map),
            pl.BlockSpec(memory_space=pl.ANY),
            pl.BlockSpec((1, 1, 1, D_MODEL), bias_map),
        ],
        out_specs=pl.BlockSpec((MOE_TM, ROW_WORDS), row_map),
        scratch_shapes=[pltpu.VMEM((2, D_MODEL, 2 * D_EXPERT), F32), pltpu.VMEM((2, D_EXPERT, D_MODEL), F32),
                        pltpu.VMEM((D_MODEL, 2 * D_EXPERT), BF16), pltpu.VMEM((D_EXPERT, D_MODEL), BF16),
                        pltpu.SemaphoreType.DMA((2, 2))],
    )
    return pl.pallas_call(
        functools.partial(_moe_kernel, layer=layer),
        grid_spec=grid_spec,
        out_shape=jax.ShapeDtypeStruct((MOE_ROWS, ROW_WORDS), jnp.int32),
        compiler_params=_cparams(1),
        name="moe_experts",
    )(*blk_meta, xs, w_gu, b_gu.reshape(DEPTH, N_EXPERTS, 1, 2 * D_EXPERT),
      w_down, b_down.reshape(DEPTH, N_EXPERTS, 1, D_MODEL))


def _route(route, counts):
    experts = jnp.arange(N_EXPERTS, dtype=jnp.int32)
    padded = (counts + MOE_TM - 1) // MOE_TM * MOE_TM
    pad_end = jnp.cumsum(padded)
    pad_start = pad_end - padded
    e = route[:, 0:TOP_K]
    onehot = e[:, :, None] == experts[None, None, :]
    dest = jnp.sum(jnp.where(onehot, pad_start[None, None, :], 0), axis=-1) + route[:, TOP_K:2 * TOP_K]
    blk_row0 = jnp.arange(MOE_BLOCKS, dtype=jnp.int32) * MOE_TM
    blk_e = jnp.minimum(jnp.sum((pad_end[None, :] <= blk_row0[:, None]).astype(jnp.int32), axis=1),
                        N_EXPERTS - 1)
    on = blk_row0 < pad_end[-1]
    n_on = jnp.sum(on.astype(jnp.int32))
    blk_onehot = blk_e[:, None] == experts[None, :]
    row_end = jnp.sum(jnp.where(blk_onehot, (pad_start + counts)[None, :], 0), axis=1)
    blk_rows = jnp.where(on, jnp.clip(row_end - blk_row0, 0, MOE_TM), 0).astype(jnp.int32)
    last_e = jnp.sum(jnp.where(jnp.arange(MOE_BLOCKS) == n_on - 1, blk_e, 0))
    blk_e = jnp.where(on, blk_e, last_e).astype(jnp.int32)
    prev = jnp.concatenate([jnp.full((1,), -1, jnp.int32), blk_e[:-1]])
    blk_first = (blk_e != prev).astype(jnp.int32)
    blk_slot = ((jnp.cumsum(blk_first) - 1) % 2).astype(jnp.int32)
    later_used = (experts[None, :] > experts[:, None]) & (counts[None, :] > 0)
    next_used = jnp.min(jnp.where(later_used, experts[None, :], N_EXPERTS), axis=1)
    next_used = jnp.where(next_used == N_EXPERTS, -1, next_used)
    blk_next = jnp.sum(jnp.where(blk_e[:, None] == experts[None, :], next_used[None, :], 0), axis=1)
    return dest, (blk_e, blk_first, blk_rows, blk_slot, blk_next.astype(jnp.int32))


def _combine_kernel(y_ref, gate_ref, x1_ref, mod_ref, g_ref, *o_refs):
    i = pl.program_id(0)
    m = mod_ref[0]
    gate = gate_ref[...]
    acc_lo = None
    for k in range(TOP_K):
        lo, hi = _unpack_pairs(y_ref[k])
        gk = gate[:, k:k + 1]
        acc_lo = gk * lo if acc_lo is None else acc_lo + gk * lo
        acc_hi = gk * hi if k == 0 else acc_hi + gk * hi
    acc = jnp.concatenate([acc_lo, acc_hi], axis=1)
    out = x1_ref[...] + m[5:6] * _rms(acc, g_ref[...][3:4])
    if len(o_refs) == 1:
        o_refs[0][...] = out
    else:
        @pl.when(i < NPT)
        def _():
            o_refs[0][...] = out

        @pl.when(i >= NPT)
        def _():
            o_refs[1][...] = out


def _combine(yg, gates, x1, mod_l, g_l, split_out):
    if split_out:
        out_specs = [_PROMPT_SPEC, _SAMPLE_SPEC]
        out_shape = [jax.ShapeDtypeStruct((NP_TOK, D_MODEL), F32), jax.ShapeDtypeStruct((NS_TOK, D_MODEL), F32)]
    else:
        out_specs = [_tok_spec(D_MODEL)]
        out_shape = [jax.ShapeDtypeStruct((N_TOK, D_MODEL), F32)]
    return pl.pallas_call(
        _combine_kernel,
        grid=(N_TILES,),
        in_specs=[pl.BlockSpec((TOP_K, TM, ROW_WORDS), lambda i: (0, i, 0)), _tok_spec(LANES),
                  _tok_spec(D_MODEL), _MOD_SPEC, _const_spec((4, D_MODEL))],
        out_specs=out_specs,
        out_shape=out_shape,
        compiler_params=_cparams(1),
        name="combine",
    )(yg, gates, x1, mod_l, g_l)


def _ffn(layer, o_cat, xs, mod_l, g_l, w_out, w_router, b_router, w_gu, b_gu, w_down, b_down, split_out):
    w_r = jnp.pad(w_router, ((0, 0), (0, LANES - N_EXPERTS))).astype(BF16)
    b_r = jnp.pad(b_router, (0, LANES - N_EXPERTS)).reshape(1, LANES)
    x1, h2p, route, gate_slab, counts = _postmix(o_cat, xs, mod_l, g_l, w_out.astype(BF16), w_r, b_r)
    dest, blk_meta = _route(route, counts[0, :N_EXPERTS])
    n_chunks = N_TOK // (SC_WORKERS * SC_ROWS)
    idx_d = dest.reshape(SC_WORKERS, n_chunks, SC_ROWS, TOP_K).transpose(0, 1, 3, 2).reshape(
        SC_WORKERS, n_chunks * TOP_K, SC_ROWS)
    rows = _sc_dispatch(h2p, idx_d)
    ys = _moe(layer, blk_meta, rows, w_gu, b_gu, w_down, b_down)
    idx_c = dest.T.reshape(SC_WORKERS, TOP_K * n_chunks, SC_ROWS)
    yg = _sc_collect(ys, idx_c).reshape(TOP_K, N_TOK, ROW_WORDS)
    return _combine(yg, gate_slab, x1, mod_l, g_l, split_out)


def _pad_heads(w, n_heads, width, keep):
    k = w.shape[0]
    w = w.reshape(k, n_heads, width)[:, :, :keep]
    return jnp.pad(w, ((0, 0), (0, 0), (0, LANES - keep))).reshape(k, n_heads * LANES)


def _pe_slab(x):
    return jnp.pad(x, [(0, 0)] * (x.ndim - 1) + [(MLA_NOPE, LANES - MLA_NOPE - MLA_ROPE)])


_SWA_ORDER = np.array([0, 4, 1, 5, 2, 6, 3, 7])


def kernel(x_prompt, x_sample, cache_mla_ckv, cache_mla_krope, cache_diff_k, cache_diff_v, cache_swa_k, cache_swa_v, cache_na_k, cache_na_v, c, c_ctx, w_mod, b_mod, norm_g, w_in0, mla_q_norm, w_uq, mla_kv_norm, w_ukv, diff_lambda, diff_norm, w_out0, w_in1, swa_sink, na_rpb, w_out1, w_router, b_router, w_gu, b_gu, w_down, b_down):
    xs = (x_prompt.reshape(NP_TOK, D_MODEL), x_sample.reshape(NS_TOK, D_MODEL))
    cond = jnp.concatenate([c_ctx[None, :], c, jnp.zeros((16 - 1 - DEC_BATCH, D_MODEL), F32)], axis=0)
    mod = _modulation(cond, w_mod, b_mod).reshape(DEPTH, 16, 6, D_MODEL)
    t64, t32 = _rope_tables()
    states = {}
    for l in range(DEPTH):
        i = l // 2
        g_l = norm_g[l]
        mod_l = mod[l]
        if l % 2 == 0:
            lam_init = 0.8 - 0.6 * math.exp(-0.3 * l)
            wi = w_in0[i]
            w_in_p = jnp.concatenate(
                [wi[:, 0:640], wi[:, 672:2208], _pe_slab(wi[:, 640:672])], axis=1).astype(BF16)
            w_uq_p = _pad_heads(w_uq[i], MLA_HEADS, MLA_NOPE + MLA_ROPE, MLA_NOPE + MLA_ROPE).astype(BF16)
            w_k_p = _pad_heads(w_ukv[i], MLA_HEADS, MLA_NOPE + MLA_V, MLA_NOPE).astype(BF16)
            w_v = w_ukv[i].reshape(MLA_KV_RANK, MLA_HEADS, MLA_NOPE + MLA_V)[:, :, MLA_NOPE:].reshape(
                MLA_KV_RANK, MLA_HEADS * MLA_V).astype(BF16)
            (q, k, v, dq, dk, dv, ckv_st, kpe_st, dk_st, dv_st) = _premix0(
                *xs, mod_l, g_l[0:1], w_in_p, mla_q_norm[i][None, :], w_uq_p, mla_kv_norm[i][None, :],
                w_k_p, w_v, t32, t64)
            states['mla_ckv'] = ckv_st.reshape(BATCH, 1, SEQ, MLA_KV_RANK)
            states['mla_krope'] = kpe_st[:, MLA_NOPE:MLA_NOPE + MLA_ROPE].reshape(BATCH, 1, SEQ, MLA_ROPE)
            states['diff_k'] = dk_st.reshape(BATCH, 1, SEQ, DIFF_HEADS, 2 * DIFF_DH)
            states['diff_v'] = dv_st.reshape(BATCH, 1, SEQ, DIFF_HEADS, 2 * DIFF_DH)
            kc, vc = _mla_cache(cache_mla_ckv[:, i].reshape(DEC_BATCH * PAST_LEN, MLA_KV_RANK),
                                _pe_slab(cache_mla_krope[:, i].reshape(DEC_BATCH * PAST_LEN, MLA_ROPE)),
                                w_k_p, w_v)
            dkc = cache_diff_k[:, i].reshape(DEC_BATCH * PAST_LEN, 512).astype(BF16)
            dvc = cache_diff_v[:, i].reshape(DEC_BATCH * PAST_LEN, 512).astype(BF16)
            lam = diff_lambda[i]
            sub_g = diff_norm[i][None, :]
            o_p = _attn_ab(q, dq, (k, v, dk, dv), None, lam, sub_g, lam_init,
                           n_batch=BATCH, t_len=SEQ, tq=SEQ, tok_off=0)
            o_cat = _attn_ab(q, dq, (k, v, dk, dv), (kc, vc, dkc, dvc), lam, sub_g, lam_init,
                             n_batch=DEC_BATCH, t_len=DEC_SEQ, tq=TQ_AB, tok_off=NP_TOK, out_init=o_p)
            w_out = w_out0[i]
        else:
            wi = w_in1[i]
            sq_cols = (_SWA_ORDER[:, None] * HEAD_DIM + np.arange(HEAD_DIM)[None, :]).reshape(-1)
            w_in_p = jnp.concatenate([wi[:, sq_cols], wi[:, 512:]], axis=1).astype(BF16)
            (sq, sk, sv, nq, nk, nv, sk_st, sv_st, nk_st, nv_st) = _premix1(xs[0], mod_l, g_l[0:1], w_in_p, t64)
            states['swa_k'] = sk_st.reshape(BATCH, 1, SEQ, SWA_KV_HEADS, HEAD_DIM)
            states['swa_v'] = sv_st.reshape(BATCH, 1, SEQ, SWA_KV_HEADS, HEAD_DIM)
            states['na_k'] = nk_st.reshape(BATCH, 1, SEQ, NA_HEADS, HEAD_DIM)
            states['na_v'] = nv_st.reshape(BATCH, 1, SEQ, NA_HEADS, HEAD_DIM)
            skc = cache_swa_k[:, i].reshape(DEC_BATCH * PAST_LEN, 128).astype(BF16)
            svc = cache_swa_v[:, i].reshape(DEC_BATCH * PAST_LEN, 128).astype(BF16)
            nkc = cache_na_k[:, i].reshape(DEC_BATCH * PAST_LEN, 512).astype(BF16)
            nvc = cache_na_v[:, i].reshape(DEC_BATCH * PAST_LEN, 512).astype(BF16)
            sink = swa_sink[i]
            o_p = _attn_cd_prompt(sink, sq, sk, sv, nq, nk, nv)
            o_cat = _attn_cd_sample(o_p, sink, sq, sk, sv, nq, nk, nv, skc, svc, nkc, nvc, _na_bias(na_rpb[i]))
            wo = w_out1[i]
            w_out = jnp.concatenate([wo[sq_cols], wo[512:]], axis=0)
        xs = _ffn(l, o_cat, xs, mod_l, g_l, w_out, w_router[l], b_router[l], w_gu, b_gu, w_down, b_down,
                  split_out=(l == DEPTH - 1))
    return (xs[0].reshape(BATCH, SEQ, D_MODEL), xs[1].reshape(DEC_BATCH, DEC_SEQ, D_MODEL),
            states['mla_ckv'], states['mla_krope'], states['diff_k'], states['diff_v'],
            states['swa_k'], states['swa_v'], states['na_k'], states['na_v'])
```

```python
import functools
import math

import numpy as np
import jax
import jax.numpy as jnp
from jax import lax
from jax.experimental import pallas as pl
from jax.experimental.pallas import tpu as pltpu
from jax.experimental.pallas import tpu_sc as plsc

F32 = jnp.float32
BF16 = jnp.bfloat16

D_MODEL = 1024
BATCH = 16
SEQ = 256
DEPTH = 2
DEC_BATCH = 8
DEC_SEQ = 2048
PAST_LEN = 256
GRID_W = 64
HEAD_DIM = 64
ROPE_THETA = 10000.0
EPS = 1e-6
NEG = -1e30

MLA_HEADS = 8
MLA_Q_RANK = 384
MLA_KV_RANK = 256
MLA_NOPE = 64
MLA_ROPE = 32
MLA_V = 64
DIFF_HEADS = 4
DIFF_DH = 64
SWA_HEADS = 8
SWA_KV_HEADS = 2
SWA_WINDOW = 128
NA_HEADS = 8
NA_WIN_ROWS = 8
NA_WIN_COLS = 16
N_EXPERTS = 32
TOP_K = 4
D_EXPERT = 1024
SWIGLU_LIMIT = 7.0
SWIGLU_ALPHA = 1.702

LANES = 128
NP_TOK = BATCH * SEQ
NS_TOK = DEC_BATCH * DEC_SEQ
N_TOK = NP_TOK + NS_TOK
TM = 512
NPT = NP_TOK // TM
TILES_PER_SAMPLE = DEC_SEQ // TM
N_TILES = N_TOK // TM
N_SUB = 2
SUB_TM = TM // N_SUB
TQ = 256
TQ_AB = 256
MOE_TM = 512
MOE_TAIL_PARTS = 4
MOE_ROWS = ((N_TOK * TOP_K + N_EXPERTS * (MOE_TM - 1)) // MOE_TM + 1) * MOE_TM
MOE_BLOCKS = MOE_ROWS // MOE_TM
NA_TILE_ROWS = TQ // GRID_W
NA_KEY_ROWS = 12
VMEM_LIMIT = 56 * 1024 * 1024


def _cparams(n_axes, vmem=VMEM_LIMIT):
    return pltpu.CompilerParams(dimension_semantics=("arbitrary",) * n_axes,
                                vmem_limit_bytes=vmem)


def _rms(x, g):
    return x * lax.rsqrt(jnp.mean(x * x, axis=-1, keepdims=True) + EPS) * g


def _dot(a, b):
    return jnp.dot(a, b, preferred_element_type=F32)


def _dot_nt(a, b):
    return lax.dot_general(a, b, (((1,), (1,)), ((), ())), preferred_element_type=F32)


def _rope(x, cos, sin_a, sin_b, half):
    return (x * cos + pltpu.roll(x, LANES - half, 1) * sin_a + pltpu.roll(x, half, 1) * sin_b)


def _pipelined_units(units):
    pending = units[0][0](units[0][2])
    for n, (_, finish, arg) in enumerate(units):
        following = units[n + 1][0](units[n + 1][2]) if n + 1 < len(units) else None
        finish(arg, pending)
        pending = following


def _sub_rows(n):
    return slice(SUB_TM * n, SUB_TM * (n + 1))


def _mod_row(i):
    return jnp.where(i < NPT, 0, 1 + (i - NPT) // TILES_PER_SAMPLE)


def _rope_blk(i):
    return jnp.where(i < NPT, TILES_PER_SAMPLE, (i - NPT) % TILES_PER_SAMPLE)


def _mod_kernel(c_ref, w_ref, b_ref, o_ref):
    c = c_ref[...]
    s = (c * jax.nn.sigmoid(c)).astype(BF16)
    o_ref[0] = _dot(s, w_ref[0].astype(BF16)) + b_ref[0]


def _modulation(cond, w_mod, b_mod):
    nb = 1024
    return pl.pallas_call(
        _mod_kernel,
        grid=(DEPTH, 6 * D_MODEL // nb),
        in_specs=[
            pl.BlockSpec((16, D_MODEL), lambda l, n: (0, 0)),
            pl.BlockSpec((1, D_MODEL, nb), lambda l, n: (l, 0, n)),
            pl.BlockSpec((1, 1, nb), lambda l, n: (l, 0, n)),
        ],
        out_specs=pl.BlockSpec((1, 16, nb), lambda l, n: (l, 0, n)),
        out_shape=jax.ShapeDtypeStruct((DEPTH, 16, 6 * D_MODEL), F32),
        compiler_params=_cparams(2),
        name="modulation",
    )(cond, w_mod, b_mod.reshape(DEPTH, 1, 6 * D_MODEL))


def _rope_tables():
    t = jnp.arange(DEC_SEQ)
    rows = (t // GRID_W).astype(F32)
    cols = (t % GRID_W).astype(F32)

    def angles(r):
        n = r // 4
        inv = ROPE_THETA ** (-jnp.arange(n, dtype=F32) / n)
        return jnp.concatenate([rows[:, None] * inv[None], cols[:, None] * inv[None]], axis=-1)

    def finish(cos, sa, sb):
        ident = (jnp.ones((TM, LANES), F32), jnp.zeros((TM, LANES), F32), jnp.zeros((TM, LANES), F32))
        return tuple(jnp.concatenate([a, b], axis=0) for a, b in zip((cos, sa, sb), ident))

    a64 = angles(64)
    c, s, z = jnp.cos(a64), jnp.sin(a64), jnp.zeros_like(a64)
    t64 = finish(jnp.concatenate([c, c, c, c], -1), jnp.concatenate([-s, z, -s, z], -1),
                 jnp.concatenate([z, s, z, s], -1))
    a32 = angles(32)
    c, s, z = jnp.cos(a32), jnp.sin(a32), jnp.zeros_like(a32)
    one64 = jnp.ones((DEC_SEQ, 64), F32)
    z64 = jnp.zeros((DEC_SEQ, 64), F32)
    z32 = jnp.zeros((DEC_SEQ, 32), F32)
    t32 = finish(jnp.concatenate([one64, c, c, z32], -1), jnp.concatenate([z64, -s, z, z32], -1),
                 jnp.concatenate([z64, z, s, z32], -1))
    return t64, t32


LOG2E = math.log2(math.e)
_MLA_SCALE = (MLA_NOPE + MLA_ROPE) ** -0.5 * LOG2E
_QSCALE = HEAD_DIM ** -0.5 * LOG2E


def _premix0_kernel(xp_ref, xs_ref, mod_ref, g_ref, win_ref, qn_ref, wuq_ref, kvn_ref, wk_ref, wv_ref,
                    c32_ref, sa32_ref, sb32_ref, c64_ref, sa64_ref, sb64_ref,
                    q_ref, k_ref, v_ref, dq_ref, dk_ref, dv_ref,
                    ckv_st, kpe_st, dk_st, dv_st):
    i = pl.program_id(0)
    m = mod_ref[0]
    g = g_ref[...]

    def project(n):
        r = _sub_rows(n)
        x = jnp.where(i < NPT, xp_ref[r, :], xs_ref[r, :])
        h = _rms(x, g) * (1.0 + m[1:2]) + m[0:1]
        return _dot(h.astype(BF16), win_ref[...])

    states = {}

    def finish(n, proj):
        r = _sub_rows(n)
        q_a = proj[:, 0:384]
        kv_a = proj[:, 384:640]
        dq = proj[:, 640:1152]
        dk = proj[:, 1152:1664]
        dv = proj[:, 1664:2176]
        pe = proj[:, 2176:2304]
        q = _dot(_rms(q_a, qn_ref[...]).astype(BF16), wuq_ref[...])
        ckv = _rms(kv_a, kvn_ref[...])
        ckv_b = ckv.astype(BF16)
        kn = _dot(ckv_b, wk_ref[...])
        v_ref[r, :] = _dot(ckv_b, wv_ref[...]).astype(BF16)
        c32, sa32, sb32 = c32_ref[r, :], sa32_ref[r, :], sb32_ref[r, :]
        c64, sa64, sb64 = c64_ref[r, :], sa64_ref[r, :], sb64_ref[r, :]
        pe_r = _rope(pe, c32, sa32, sb32, MLA_ROPE // 2)
        for hd in range(MLA_HEADS):
            sl = slice(LANES * hd, LANES * (hd + 1))
            q_ref[r, sl] = (_rope(q[:, sl], c32, sa32, sb32, MLA_ROPE // 2) * _MLA_SCALE).astype(BF16)
            k_ref[r, sl] = (kn[:, sl] + pe_r).astype(BF16)
        for hd in range(DIFF_HEADS):
            sl = slice(LANES * hd, LANES * (hd + 1))
            dq_ref[r, sl] = (_rope(dq[:, sl], c64, sa64, sb64, DIFF_DH // 2) * _QSCALE).astype(BF16)
            dk_ref[r, sl] = _rope(dk[:, sl], c64, sa64, sb64, DIFF_DH // 2).astype(BF16)
        dv_ref[r, :] = dv.astype(BF16)
        states[n] = (ckv, pe, dk, dv)

    _pipelined_units([(project, finish, n) for n in range(N_SUB)])

    @pl.when(i < NPT)
    def _():
        for n in range(N_SUB):
            r = _sub_rows(n)
            ckv_st[r, :], kpe_st[r, :], dk_st[r, :], dv_st[r, :] = states[n]


def _premix1_kernel(x_ref, mod_ref, g_ref, win_ref, c64_ref, sa64_ref, sb64_ref,
                    sq_ref, sk_ref, sv_ref, nq_ref, nk_ref, nv_ref,
                    sk_st, sv_st, nk_st, nv_st):
    i = pl.program_id(0)
    m = mod_ref[0]
    g = g_ref[...]

    def project(n):
        h = _rms(x_ref[_sub_rows(n), :], g) * (1.0 + m[1:2]) + m[0:1]
        return _dot(h.astype(BF16), win_ref[...])

    states = {}

    def finish(n, proj):
        r = _sub_rows(n)
        sq = proj[:, 0:512]
        sk = proj[:, 512:640]
        sv = proj[:, 640:768]
        nq = proj[:, 768:1280]
        nk = proj[:, 1280:1792]
        nv = proj[:, 1792:2304]
        c64, sa64, sb64 = c64_ref[r, :], sa64_ref[r, :], sb64_ref[r, :]
        for hd in range(4):
            sl = slice(LANES * hd, LANES * (hd + 1))
            sq_ref[r, sl] = (_rope(sq[:, sl], c64, sa64, sb64, HEAD_DIM // 2) * _QSCALE).astype(BF16)
        sk_ref[r, :] = _rope(sk, c64, sa64, sb64, HEAD_DIM // 2).astype(BF16)
        sv_ref[r, :] = sv.astype(BF16)
        nq_ref[r, :] = (nq * _QSCALE).astype(BF16)
        nk_ref[r, :] = nk.astype(BF16)
        nv_ref[r, :] = nv.astype(BF16)
        states[n] = (sk, sv, nk, nv)

    _pipelined_units([(project, finish, n) for n in range(N_SUB)])

    @pl.when(i < NPT)
    def _():
        for n in range(N_SUB):
            r = _sub_rows(n)
            sk_st[r, :], sv_st[r, :], nk_st[r, :], nv_st[r, :] = states[n]


def _tok_spec(width):
    return pl.BlockSpec((TM, width), lambda i: (i, 0))


_PROMPT_SPEC = pl.BlockSpec((TM, D_MODEL), lambda i: (jnp.minimum(i, NPT - 1), 0))
_SAMPLE_SPEC = pl.BlockSpec((TM, D_MODEL), lambda i: (jnp.maximum(i - NPT, 0), 0))


def _state_spec(width):
    return pl.BlockSpec((TM, width), lambda i: (jnp.minimum(i, NPT - 1), 0))


def _const_spec(shape):
    return pl.BlockSpec(shape, lambda i: (0,) * len(shape))


_MOD_SPEC = pl.BlockSpec((1, 6, D_MODEL), lambda i: (_mod_row(i), 0, 0))
_ROPE_SPEC = pl.BlockSpec((TM, LANES), lambda i: (_rope_blk(i), 0))


def _premix0(xp, xs, mod_l, g0, w_in_p, q_norm, w_uq_p, kv_norm, w_k_p, w_v, t32, t64):
    outs = [(N_TOK, 1024, BF16), (N_TOK, 1024, BF16), (N_TOK, 512, BF16), (N_TOK, 512, BF16),
            (N_TOK, 512, BF16), (N_TOK, 512, BF16),
            (NP_TOK, 256, F32), (NP_TOK, 128, F32), (NP_TOK, 512, F32), (NP_TOK, 512, F32)]
    return pl.pallas_call(
        _premix0_kernel,
        grid=(N_TILES,),
        in_specs=[_PROMPT_SPEC, _SAMPLE_SPEC, _MOD_SPEC, _const_spec((1, D_MODEL)),
                  _const_spec((D_MODEL, 2304)), _const_spec((1, MLA_Q_RANK)),
                  _const_spec((MLA_Q_RANK, 1024)), _const_spec((1, MLA_KV_RANK)),
                  _const_spec((MLA_KV_RANK, 1024)), _const_spec((MLA_KV_RANK, 512))]
                 + [_ROPE_SPEC] * 6,
        out_specs=[_tok_spec(w) for (_, w, _) in outs[:6]] + [_state_spec(w) for (_, w, _) in outs[6:]],
        out_shape=[jax.ShapeDtypeStruct((n, w), dt) for (n, w, dt) in outs],
        compiler_params=_cparams(1),
        name="premix_ab",
    )(xp, xs, mod_l, g0, w_in_p, q_norm, w_uq_p, kv_norm, w_k_p, w_v, *t32, *t64)


def _premix1(x, mod_l, g0, w_in_p, t64):
    outs = [(N_TOK, 512, BF16), (N_TOK, 128, BF16), (N_TOK, 128, BF16), (N_TOK, 512, BF16),
            (N_TOK, 512, BF16), (N_TOK, 512, BF16),
            (NP_TOK, 128, F32), (NP_TOK, 128, F32), (NP_TOK, 512, F32), (NP_TOK, 512, F32)]
    return pl.pallas_call(
        _premix1_kernel,
        grid=(N_TILES,),
        in_specs=[_tok_spec(D_MODEL), _MOD_SPEC, _const_spec((1, D_MODEL)),
                  _const_spec((D_MODEL, 2304))] + [_ROPE_SPEC] * 3,
        out_specs=[_tok_spec(w) for (_, w, _) in outs[:6]] + [_state_spec(w) for (_, w, _) in outs[6:]],
        out_shape=[jax.ShapeDtypeStruct((n, w), dt) for (n, w, dt) in outs],
        compiler_params=_cparams(1),
        name="premix_cd",
    )(x, mod_l, g0, w_in_p, *t64)


def _mla_cache_kernel(ckv_ref, pe_ref, wk_ref, wv_ref, k_ref, v_ref):
    c = ckv_ref[...].astype(BF16)
    kn = _dot(c, wk_ref[...])
    v_ref[...] = _dot(c, wv_ref[...]).astype(BF16)
    pe = pe_ref[...]
    for hd in range(MLA_HEADS):
        sl = slice(LANES * hd, LANES * (hd + 1))
        k_ref[:, sl] = (kn[:, sl] + pe).astype(BF16)


def _mla_cache(ckv, pe_slab, w_k_p, w_v):
    n = ckv.shape[0]
    tm = 512
    return pl.pallas_call(
        _mla_cache_kernel,
        grid=(n // tm,),
        in_specs=[pl.BlockSpec((tm, MLA_KV_RANK), lambda i: (i, 0)),
                  pl.BlockSpec((tm, LANES), lambda i: (i, 0)),
                  _const_spec((MLA_KV_RANK, 1024)), _const_spec((MLA_KV_RANK, 512))],
        out_specs=[pl.BlockSpec((tm, 1024), lambda i: (i, 0)), pl.BlockSpec((tm, 512), lambda i: (i, 0))],
        out_shape=[jax.ShapeDtypeStruct((n, 1024), BF16), jax.ShapeDtypeStruct((n, 512), BF16)],
        compiler_params=_cparams(1),
        name="mla_cache",
    )(ckv, pe_slab, w_k_p, w_v)


def _softmax_pv(scores, values, sink=None):
    m = jnp.max(scores[0], axis=-1, keepdims=True)
    for s in scores[1:]:
        m = jnp.maximum(m, jnp.max(s, axis=-1, keepdims=True))
    if sink is not None:
        m = jnp.maximum(m, sink)
    l = None
    o = None
    for s, v in zip(scores, values):
        p = jnp.exp2(s - m)
        ls = jnp.sum(p, axis=-1, keepdims=True)
        os_ = _dot(p.astype(BF16), v)
        l = ls if l is None else l + ls
        o = os_ if o is None else o + os_
    if sink is not None:
        l = l + jnp.exp2(sink - m)
    return o * (1.0 / l)


def _lane_lo(shape):
    return lax.broadcasted_iota(jnp.int32, shape, 1) < (LANES // 2)


def _split_halves(qb):
    lo = _lane_lo(qb.shape)
    zero = jnp.zeros_like(qb)
    return jnp.where(lo, qb, zero), jnp.where(lo, zero, qb)


def _attn_ab_kernel(*refs, n_pieces, lam_init, aliased):
    if aliased:
        refs = refs[1:]
    q_ref, dq_ref = refs[0], refs[1]
    pieces = [refs[2 + 4 * p: 6 + 4 * p] for p in range(n_pieces)]
    lam_ref, subg_ref, o_ref = refs[2 + 4 * n_pieces:]
    lam = lam_ref[...]
    lam_full = (jnp.exp(jnp.sum(lam[0:1] * lam[1:2], axis=-1, keepdims=True))
                - jnp.exp(jnp.sum(lam[2:3] * lam[3:4], axis=-1, keepdims=True)) + lam_init)
    lo = _lane_lo((q_ref.shape[0], LANES))
    subg = subg_ref[...]
    tq = q_ref.shape[0]
    def mla_scores(hd):
        sl = slice(LANES * hd, LANES * (hd + 1))
        qh = q_ref[:, sl]
        return [_dot_nt(qh, k_ref[:, sl]) for (k_ref, _, _, _) in pieces]

    def diff_scores(hd):
        sl = slice(LANES * hd, LANES * (hd + 1))
        qq = jnp.concatenate(_split_halves(dq_ref[:, sl]), axis=0)
        return [_dot_nt(qq, dk_ref[:, sl]) for (_, _, dk_ref, _) in pieces]

    pair = {}

    def mla_finish(hd, scores):
        j = hd // 2
        vals = [v_ref[:, LANES * j: LANES * (j + 1)] for (_, v_ref, _, _) in pieces]
        pair[hd % 2] = _softmax_pv(scores, vals)
        if hd % 2 == 1:
            o_ref[:, LANES * j: LANES * (j + 1)] = jnp.where(lo, pair[0], pair[1]).astype(BF16)

    def diff_finish(hd, scores):
        sl = slice(LANES * hd, LANES * (hd + 1))
        oo = _softmax_pv(scores, [dv_ref[:, sl] for (_, _, _, dv_ref) in pieces])
        od = _rms(oo[:tq] - lam_full * oo[tq:], subg) * (1.0 - lam_init)
        o_ref[:, 512 + LANES * hd: 512 + LANES * (hd + 1)] = od.astype(BF16)

    units = []
    for j in range(DIFF_HEADS):
        units += [(mla_scores, mla_finish, 2 * j), (mla_scores, mla_finish, 2 * j + 1),
                  (diff_scores, diff_finish, j)]
    _pipelined_units(units)


def _attn_ab(q, dq, new_kv, cache_kv, lam, sub_g, lam_init, *, n_batch, t_len, tq, tok_off, out_init=None):
    nq = t_len // tq
    q_off = tok_off // tq
    b_off = tok_off // t_len
    widths = (1024, 512, 512, 512)
    in_specs = [pl.BlockSpec((tq, 1024), lambda b, i: (q_off + b * nq + i, 0)),
                pl.BlockSpec((tq, 512), lambda b, i: (q_off + b * nq + i, 0))]
    args = [q, dq]
    for w, a in zip(widths, new_kv):
        in_specs.append(pl.BlockSpec((t_len, w), lambda b, i: (b_off + b, 0)))
        args.append(a)
    n_pieces = 1
    if cache_kv is not None:
        n_pieces = 2
        for w, a in zip(widths, cache_kv):
            in_specs.append(pl.BlockSpec((PAST_LEN, w), lambda b, i: (b, 0)))
            args.append(a)
    in_specs += [pl.BlockSpec((4, DIFF_DH), lambda b, i: (0, 0)),
                 pl.BlockSpec((1, 2 * DIFF_DH), lambda b, i: (0, 0))]
    args += [lam, sub_g]
    aliases = {}
    if out_init is not None:
        in_specs = [pl.BlockSpec(memory_space=pl.ANY)] + in_specs
        args = [out_init] + args
        aliases = {0: 0}
    return pl.pallas_call(
        functools.partial(_attn_ab_kernel, n_pieces=n_pieces, lam_init=lam_init, aliased=out_init is not None),
        grid=(n_batch, nq),
        in_specs=in_specs,
        out_specs=pl.BlockSpec((tq, 1024), lambda b, i: (q_off + b * nq + i, 0)),
        out_shape=jax.ShapeDtypeStruct((N_TOK, 1024), BF16),
        input_output_aliases=aliases,
        compiler_params=_cparams(2),
        name="attn_ab_%d" % n_pieces,
    )(*args)


def _gqa_stacks(sq_ref, sink_ref):
    tq = sq_ref.shape[0]
    halves = [_split_halves(sq_ref[:, LANES * j: LANES * (j + 1)]) for j in range(4)]
    q_stacks = [jnp.concatenate([halves[j][kvh] for j in range(4)], axis=0) for kvh in range(SWA_KV_HEADS)]
    sinks = [jnp.concatenate([jnp.full((tq, 1), sink_ref[4 * kvh + j] * LOG2E, F32) for j in range(4)], axis=0)
             for kvh in range(SWA_KV_HEADS)]
    return q_stacks, sinks


def _attn_cd_prompt_kernel(sink_ref, sq_ref, sk_ref, sv_ref, nq_ref, nk_ref, nv_ref, o_ref):
    tq = sq_ref.shape[0]
    lo = _lane_lo((tq, LANES))
    sk = sk_ref[...]
    sv = sv_ref[...]
    for j in range(4):
        sl = slice(LANES * j, LANES * (j + 1))
        q_lo, q_hi = _split_halves(sq_ref[:, sl])
        o_lo = _softmax_pv([_dot_nt(q_lo, sk)], [sv], sink=sink_ref[j] * LOG2E)
        o_hi = _softmax_pv([_dot_nt(q_hi, sk)], [sv], sink=sink_ref[j + 4] * LOG2E)
        o_ref[:, sl] = jnp.where(lo, o_lo, o_hi).astype(BF16)
    for j in range(4):
        sl = slice(LANES * j, LANES * (j + 1))
        q_lo, q_hi = _split_halves(nq_ref[:, sl])
        k = nk_ref[:, sl]
        v = nv_ref[:, sl]
        o_lo = _softmax_pv([_dot_nt(q_lo, k)], [v])
        o_hi = _softmax_pv([_dot_nt(q_hi, k)], [v])
        o_ref[:, 512 + LANES * j: 512 + LANES * (j + 1)] = jnp.where(lo, o_lo, o_hi).astype(BF16)


def _attn_cd_prompt(sink, sq, sk, sv, nq, nk, nv):
    def spec(w):
        return pl.BlockSpec((SEQ, w), lambda b: (b, 0))
    return pl.pallas_call(
        _attn_cd_prompt_kernel,
        grid=(BATCH,),
        in_specs=[pl.BlockSpec(memory_space=pltpu.SMEM), spec(512), spec(128), spec(128),
                  spec(512), spec(512), spec(512)],
        out_specs=spec(1024),
        out_shape=jax.ShapeDtypeStruct((N_TOK, 1024), BF16),
        compiler_params=_cparams(1),
        name="attn_cd_prompt",
    )(sink, sq, sk, sv, nq, nk, nv)


_SWA_KEYS = TQ + 2 * SWA_WINDOW


def _attn_cd_sample_kernel(init_ref, sink_ref, sq_ref, nq_ref, sk_ref, sv_ref, nk_ref, nv_ref,
                           skc_ref, svc_ref, nkc_ref, nvc_ref, bias_ref, o_ref):
    del init_ref
    qi = pl.program_id(1)
    lo = _lane_lo((TQ, LANES))
    ks = pl.multiple_of(jnp.clip(qi * TQ - SWA_WINDOW, 0, DEC_SEQ - _SWA_KEYS), SWA_WINDOW)
    k_win = sk_ref[pl.ds(ks, _SWA_KEYS), :]
    v_win = sv_ref[pl.ds(ks, _SWA_KEYS), :]
    q_pos = qi * TQ + (lax.broadcasted_iota(jnp.int32, (4 * TQ, _SWA_KEYS), 0) & (TQ - 1))
    k_pos = ks + lax.broadcasted_iota(jnp.int32, (4 * TQ, _SWA_KEYS), 1)
    in_win = jnp.abs(q_pos - k_pos) <= SWA_WINDOW
    skc = skc_ref[...]
    svc = svc_ref[...]
    q_stacks, sinks = _gqa_stacks(sq_ref, sink_ref)

    def window_scores(kvh):
        return [_dot_nt(q_stacks[kvh], skc), jnp.where(in_win, _dot_nt(q_stacks[kvh], k_win), NEG)]

    o_kv = {}

    def window_finish(kvh, scores):
        o_kv[kvh] = _softmax_pv(scores, [svc, v_win], sink=sinks[kvh])
        if kvh == SWA_KV_HEADS - 1:
            for j in range(4):
                rows = slice(TQ * j, TQ * (j + 1))
                o_ref[:, LANES * j: LANES * (j + 1)] = jnp.where(lo, o_kv[0][rows], o_kv[1][rows]).astype(BF16)

    n_rows = DEC_SEQ // GRID_W
    r0 = jnp.clip(qi * NA_TILE_ROWS - NA_WIN_ROWS // 2, 0, n_rows - NA_KEY_ROWS)
    kn = pl.multiple_of(r0 * GRID_W, GRID_W)

    def na_scores(hd):
        sl = slice(LANES * (hd // 2), LANES * (hd // 2 + 1))
        q_half = _split_halves(nq_ref[:, sl])[hd % 2]
        nk_win = nk_ref[pl.ds(kn, NA_KEY_ROWS * GRID_W), sl]
        return [_dot_nt(q_half, nkc_ref[:, sl]), _dot_nt(q_half, nk_win) + bias_ref[0, hd]]

    pair = {}

    def na_finish(hd, scores):
        j = hd // 2
        sl = slice(LANES * j, LANES * (j + 1))
        nv_win = nv_ref[pl.ds(kn, NA_KEY_ROWS * GRID_W), sl]
        pair[hd % 2] = _softmax_pv(scores, [nvc_ref[:, sl], nv_win])
        if hd % 2 == 1:
            o_ref[:, 512 + LANES * j: 512 + LANES * (j + 1)] = jnp.where(lo, pair[0], pair[1]).astype(BF16)

    _pipelined_units([(window_scores, window_finish, kvh) for kvh in range(SWA_KV_HEADS)]
                     + [(na_scores, na_finish, hd) for hd in range(NA_HEADS)])


def _attn_cd_sample(out_init, sink, sq, sk, sv, nq, nk, nv, skc, svc, nkc, nvc, bias):
    nq_t = DEC_SEQ // TQ
    q_off = NP_TOK // TQ
    b_off = NP_TOK // DEC_SEQ

    def qspec(w):
        return pl.BlockSpec((TQ, w), lambda b, i: (q_off + b * nq_t + i, 0))

    def kspec(w):
        return pl.BlockSpec((DEC_SEQ, w), lambda b, i: (b_off + b, 0))

    def cspec(w):
        return pl.BlockSpec((PAST_LEN, w), lambda b, i: (b, 0))

    n_keys = NA_KEY_ROWS * GRID_W
    bias_spec = pl.BlockSpec(
        (1, NA_HEADS, TQ, n_keys),
        lambda b, i: (jnp.where(i == 0, 0, jnp.where(i == nq_t - 1, 2, 1)), 0, 0, 0))
    return pl.pallas_call(
        _attn_cd_sample_kernel,
        grid=(DEC_BATCH, nq_t),
        in_specs=[pl.BlockSpec(memory_space=pl.ANY), pl.BlockSpec(memory_space=pltpu.SMEM),
                  qspec(512), qspec(512), kspec(128), kspec(128), kspec(512), kspec(512),
                  cspec(128), cspec(128), cspec(512), cspec(512), bias_spec],
        out_specs=pl.BlockSpec((TQ, 1024), lambda b, i: (q_off + b * nq_t + i, 0)),
        out_shape=jax.ShapeDtypeStruct((N_TOK, 1024), BF16),
        input_output_aliases={0: 0},
        compiler_params=_cparams(2),
        name="attn_cd_sample",
    )(out_init, sink, sq, nq, sk, sv, nk, nv, skc, svc, nkc, nvc, bias)


def _na_bias(rpb):
    n_rows = DEC_SEQ // GRID_W
    n_dr = 2 * NA_WIN_ROWS - 1
    n_dc = 2 * NA_WIN_COLS - 1
    c = np.arange(GRID_W)[:, None]
    kc = np.arange(GRID_W)[None, :]
    qs = np.clip(c - NA_WIN_COLS // 2, 0, GRID_W - NA_WIN_COLS)
    col_ok = (kc >= qs) & (kc < qs + NA_WIN_COLS)
    dc = np.clip(kc - c + NA_WIN_COLS - 1, 0, n_dc - 1)
    onehot = ((dc[None] == np.arange(n_dc)[:, None, None]) & col_ok[None]).astype(np.float32)
    blocks = jnp.einsum('hrd,dck->hrck', rpb.astype(F32) * LOG2E, onehot, precision=lax.Precision.HIGHEST)
    row_sel = np.zeros((3, n_dr, NA_TILE_ROWS, NA_KEY_ROWS), np.float32)
    for v, q_row0 in enumerate((0, NA_TILE_ROWS, n_rows - NA_TILE_ROWS)):
        k_row0 = min(max(q_row0 - NA_WIN_ROWS // 2, 0), n_rows - NA_KEY_ROWS)
        for ri in range(NA_TILE_ROWS):
            r = q_row0 + ri
            rs = min(max(r - NA_WIN_ROWS // 2, 0), n_rows - NA_WIN_ROWS)
            for kj in range(NA_KEY_ROWS):
                kr = k_row0 + kj
                if rs <= kr < rs + NA_WIN_ROWS:
                    row_sel[v, kr - r + NA_WIN_ROWS - 1, ri, kj] = 1.0
    tiles = jnp.einsum('vdrj,hdck->vhrcjk', row_sel, blocks, precision=lax.Precision.HIGHEST)
    valid = (row_sel.sum(axis=1) > 0)[:, :, None, :, None] & col_ok[None, None, :, None, :]
    tiles = jnp.where(valid[:, None], tiles, NEG)
    return tiles.reshape(3, NA_HEADS, TQ, NA_KEY_ROWS * GRID_W)


_HI_MASK = -65536


def _pack_pairs(x):
    w = x.shape[1] // 2
    r = x.astype(BF16).astype(F32)
    lo = lax.bitcast_convert_type(r[:, :w], jnp.int32)
    hi = lax.bitcast_convert_type(r[:, w:], jnp.int32)
    return (hi & _HI_MASK) | lax.shift_right_logical(lo, 16)


def _unpack_pairs(p):
    lo = lax.bitcast_convert_type(lax.shift_left(p, 16), F32)
    hi = lax.bitcast_convert_type(p & _HI_MASK, F32)
    return lo, hi


def _postmix_kernel(*refs, split_x):
    if split_x:
        o_ref, xp_ref, xs_ref = refs[:3]
        refs = refs[3:]
    else:
        o_ref, x_ref = refs[:2]
        refs = refs[2:]
    (mod_ref, g_ref, wout_ref, wr_ref, br_ref, tri_ref,
     x1_ref, h2_ref, route_ref, gate_ref, cnt_ref, run_ref) = refs
    i = pl.program_id(0)

    @pl.when(i == 0)
    def _():
        run_ref[...] = jnp.zeros_like(run_ref)

    m = mod_ref[0]
    g = g_ref[...]
    tri = tri_ref[...]
    counts = {-1: run_ref[0:1, :]}

    def project(n):
        return _dot(o_ref[_sub_rows(n), :], wout_ref[...])

    def finish(n, y):
        r = _sub_rows(n)
        x = jnp.where(i < NPT, xp_ref[r, :], xs_ref[r, :]) if split_x else x_ref[r, :]
        x1 = x + m[2:3] * _rms(y, g[1:2])
        x1_ref[r, :] = x1
        h2 = _rms(x1, g[2:3]) * (1.0 + m[4:5]) + m[3:4]
        h2_ref[r, :] = _pack_pairs(h2)
        logits = _dot(h2.astype(BF16), wr_ref[...]) + br_ref[...]
        lane = lax.broadcasted_iota(jnp.int32, logits.shape, 1).astype(F32)
        cur = jnp.where(lane < N_EXPERTS, logits, -jnp.inf)
        tops, idxs = [], []
        for _ in range(TOP_K):
            mx = jnp.max(cur, axis=-1, keepdims=True)
            ix = jnp.min(jnp.where(cur == mx, lane, float(LANES)), axis=-1, keepdims=True)
            tops.append(mx)
            idxs.append(ix)
            cur = jnp.where(lane == ix, -jnp.inf, cur)
        es = [jnp.exp(t - tops[0]) for t in tops]
        inv = 1.0 / (es[0] + es[1] + es[2] + es[3])
        picked = jnp.zeros_like(logits)
        for k in range(TOP_K):
            picked = jnp.where(lane == idxs[k], 1.0, picked)
        before = _dot(tri, picked.astype(BF16)) + counts[n - 1]
        route = jnp.zeros_like(logits)
        gate_out = jnp.zeros_like(logits)
        for k in range(TOP_K):
            rank = jnp.sum(jnp.where(lane == idxs[k], before, 0.0), axis=-1, keepdims=True)
            route = jnp.where(lane == float(k), idxs[k], route)
            route = jnp.where(lane == float(TOP_K + k), rank, route)
            gate_out = jnp.where(lane == float(k), es[k] * inv, gate_out)
        route_ref[r, :] = route.astype(jnp.int32)
        gate_ref[r, :] = gate_out
        counts[n] = counts[n - 1] + jnp.sum(picked, axis=0, keepdims=True)

    _pipelined_units([(project, finish, n) for n in range(N_SUB)])
    run_ref[...] = jnp.broadcast_to(counts[N_SUB - 1], run_ref.shape)
    cnt_ref[...] = run_ref[...].astype(jnp.int32)


def _postmix(o_cat, xs, mod_l, g_l, w_out, w_r, b_r):
    tri = jnp.asarray(np.tril(np.ones((SUB_TM, SUB_TM), np.float32), -1), BF16)
    split_x = len(xs) == 2
    x_specs = [_PROMPT_SPEC, _SAMPLE_SPEC] if split_x else [_tok_spec(D_MODEL)]
    return pl.pallas_call(
        functools.partial(_postmix_kernel, split_x=split_x),
        grid=(N_TILES,),
        in_specs=[_tok_spec(1024)] + x_specs + [_MOD_SPEC, _const_spec((4, D_MODEL)),
                  _const_spec((1024, D_MODEL)), _const_spec((D_MODEL, LANES)), _const_spec((1, LANES)),
                  _const_spec((SUB_TM, SUB_TM))],
        out_specs=[_tok_spec(D_MODEL), _tok_spec(D_MODEL // 2), _tok_spec(LANES), _tok_spec(LANES),
                   _const_spec((8, LANES))],
        out_shape=[jax.ShapeDtypeStruct((N_TOK, D_MODEL), F32),
                   jax.ShapeDtypeStruct((N_TOK, D_MODEL // 2), jnp.int32),
                   jax.ShapeDtypeStruct((N_TOK, LANES), jnp.int32), jax.ShapeDtypeStruct((N_TOK, LANES), F32),
                   jax.ShapeDtypeStruct((8, LANES), jnp.int32)],
        scratch_shapes=[pltpu.VMEM((8, LANES), F32)],
        compiler_params=_cparams(1),
        name="postmix",
    )(o_cat, *xs, mod_l, g_l, w_out, w_r, b_r, tri)


SC_WORKERS = 32
SC_ROWS = 128
ROW_WORDS = D_MODEL // 2


def _sc_worker_id():
    return lax.axis_index("s") * 2 + lax.axis_index("c")


def _sc_dispatch(src, idx):
    n_chunks = N_TOK // (SC_WORKERS * SC_ROWS)
    mesh = plsc.VectorSubcoreMesh(core_axis_name="c", subcore_axis_name="s")

    @functools.partial(
        pl.kernel, mesh=mesh,
        out_type=jax.ShapeDtypeStruct((MOE_ROWS, ROW_WORDS), jnp.int32),
        scratch_types=[pltpu.VMEM((n_chunks * TOP_K, SC_ROWS), jnp.int32),
                       pltpu.VMEM((SC_ROWS, ROW_WORDS), jnp.int32), pltpu.SemaphoreType.DMA])
    def k(src_hbm, idx_hbm, out_hbm, idx_v, rows_v, sem):
        wid = _sc_worker_id()
        pltpu.sync_copy(idx_hbm.at[wid], idx_v)

        @pl.loop(0, n_chunks)
        def _(g):
            pltpu.sync_copy(src_hbm.at[pl.ds((wid * n_chunks + g) * SC_ROWS, SC_ROWS)], rows_v)
            copies = [pltpu.async_copy(rows_v, out_hbm.at[idx_v.at[g * TOP_K + kk]], sem)
                      for kk in range(TOP_K)]
            for cp in copies:
                cp.wait()

    return k(src, idx)


def _sc_collect(table, idx):
    n_chunks = idx.shape[1]
    mesh = plsc.VectorSubcoreMesh(core_axis_name="c", subcore_axis_name="s")

    @functools.partial(
        pl.kernel, mesh=mesh,
        out_type=jax.ShapeDtypeStruct((SC_WORKERS * n_chunks * SC_ROWS, ROW_WORDS), jnp.int32),
        scratch_types=[pltpu.VMEM((n_chunks, SC_ROWS), jnp.int32),
                       pltpu.VMEM((SC_ROWS, ROW_WORDS), jnp.int32), pltpu.SemaphoreType.DMA])
    def k(table_hbm, idx_hbm, out_hbm, idx_v, rows_v, sem):
        wid = _sc_worker_id()
        pltpu.sync_copy(idx_hbm.at[wid], idx_v)

        @pl.loop(0, n_chunks)
        def _(g):
            pltpu.async_copy(table_hbm.at[idx_v.at[g]], rows_v, sem).wait()
            pltpu.sync_copy(rows_v, out_hbm.at[pl.ds((wid * n_chunks + g) * SC_ROWS, SC_ROWS)])

    return k(table, idx)


def _expert_rows(words, n_valid, wgu_b, wd_b, bgu, bd):
    live = lax.broadcasted_iota(jnp.int32, words.shape, 0) < n_valid
    lo, hi = _unpack_pairs(jnp.where(live, words, 0))
    x = jnp.concatenate([lo, hi], axis=1).astype(BF16)
    gu = _dot(x, wgu_b[...]) + bgu
    g = jnp.minimum(gu[:, :D_EXPERT], SWIGLU_LIMIT)
    u = jnp.clip(gu[:, D_EXPERT:], -SWIGLU_LIMIT, SWIGLU_LIMIT)
    a = g * jax.nn.sigmoid(SWIGLU_ALPHA * g) * (u + 1.0)
    return _pack_pairs(_dot(a.astype(BF16), wd_b[...]) + bd)


def _moe_kernel(blk_e_ref, blk_first_ref, blk_rows_ref, blk_slot_ref, blk_next_ref,
                x_ref, wgu_hbm, bgu_ref, wd_hbm, bd_ref, y_ref,
                wgu_f, wd_f, wgu_b, wd_b, sem, *, layer):
    i = pl.program_id(0)
    n_valid = blk_rows_ref[i]
    quantum = MOE_TM // MOE_TAIL_PARTS

    def weight_copies(e, slot):
        return (pltpu.make_async_copy(wgu_hbm.at[layer, e], wgu_f.at[slot], sem.at[0, slot]),
                pltpu.make_async_copy(wd_hbm.at[layer, e], wd_f.at[slot], sem.at[1, slot]))

    @pl.when(i == 0)
    def _():
        for cp in weight_copies(blk_e_ref[0], blk_slot_ref[0]):
            cp.start()

    @pl.when(blk_first_ref[i] == 1)
    def _():
        slot = blk_slot_ref[i]
        for cp in weight_copies(blk_e_ref[i], slot):
            cp.wait()
        nxt = blk_next_ref[i]

        @pl.when(nxt >= 0)
        def _():
            for cp in weight_copies(nxt, 1 - slot):
                cp.start()

        wgu_b[...] = wgu_f[slot].astype(BF16)
        wd_b[...] = wd_f[slot].astype(BF16)

    for parts in range(1, MOE_TAIL_PARTS + 1):
        rows = parts * quantum

        @pl.when((n_valid > rows - quantum) & (n_valid <= rows))
        def _(rows=rows):
            y_ref[:rows] = _expert_rows(x_ref[:rows], n_valid, wgu_b, wd_b, bgu_ref[0, 0], bd_ref[0, 0])
            if rows < MOE_TM:
                y_ref[rows:] = jnp.zeros((MOE_TM - rows, ROW_WORDS), jnp.int32)

    @pl.when(n_valid == 0)
    def _():
        y_ref[...] = jnp.zeros_like(y_ref)


def _moe(layer, blk_meta, xs, w_gu, b_gu, w_down, b_down):
    def row_map(i, *_):
        return (i, 0)

    def bias_map(i, e, *_):
        return (layer, e[i], 0, 0)

    grid_spec = pltpu.PrefetchScalarGridSpec(
        num_scalar_prefetch=5,
        grid=(MOE_BLOCKS,),
        in_specs=[
            pl.BlockSpec((MOE_TM, ROW_WORDS), row_map),
            pl.BlockSpec(memory_space=pl.ANY),
            pl.BlockSpec((1, 1, 1, 2 * D_EXPERT), bias_map),
            pl.BlockSpec(memory_space=pl.ANY),
            pl.BlockSpec((1, 1, 1, D_MODEL), bias_map),
        ],
        out_specs=pl.BlockSpec((MOE_TM, ROW_WORDS), row_map),
        scratch_shapes=[pltpu.VMEM((2, D_MODEL, 2 * D_EXPERT), F32), pltpu.VMEM((2, D_EXPERT, D_MODEL), F32),
                        pltpu.VMEM((D_MODEL, 2 * D_EXPERT), BF16), pltpu.VMEM((D_EXPERT, D_MODEL), BF16),
                        pltpu.SemaphoreType.DMA((2, 2))],
    )
    return pl.pallas_call(
        functools.partial(_moe_kernel, layer=layer),
        grid_spec=grid_spec,
        out_shape=jax.ShapeDtypeStruct((MOE_ROWS, ROW_WORDS), jnp.int32),
        compiler_params=_cparams(1),
        name="moe_experts",
    )(*blk_meta, xs, w_gu, b_gu.reshape(DEPTH, N_EXPERTS, 1, 2 * D_EXPERT),
      w_down, b_down.reshape(DEPTH, N_EXPERTS, 1, D_MODEL))


def _route(route, counts):
    experts = jnp.arange(N_EXPERTS, dtype=jnp.int32)
    padded = (counts + MOE_TM - 1) // MOE_TM * MOE_TM
    pad_end = jnp.cumsum(padded)
    pad_start = pad_end - padded
    e = route[:, 0:TOP_K]
    onehot = e[:, :, None] == experts[None, None, :]
    dest = jnp.sum(jnp.where(onehot, pad_start[None, None, :], 0), axis=-1) + route[:, TOP_K:2 * TOP_K]
    blk_row0 = jnp.arange(MOE_BLOCKS, dtype=jnp.int32) * MOE_TM
    blk_e = jnp.minimum(jnp.sum((pad_end[None, :] <= blk_row0[:, None]).astype(jnp.int32), axis=1),
                        N_EXPERTS - 1)
    on = blk_row0 < pad_end[-1]
    n_on = jnp.sum(on.astype(jnp.int32))
    blk_onehot = blk_e[:, None] == experts[None, :]
    row_end = jnp.sum(jnp.where(blk_onehot, (pad_start + counts)[None, :], 0), axis=1)
    blk_rows = jnp.where(on, jnp.clip(row_end - blk_row0, 0, MOE_TM), 0).astype(jnp.int32)
    last_e = jnp.sum(jnp.where(jnp.arange(MOE_BLOCKS) == n_on - 1, blk_e, 0))
    blk_e = jnp.where(on, blk_e, last_e).astype(jnp.int32)
    prev = jnp.concatenate([jnp.full((1,), -1, jnp.int32), blk_e[:-1]])
    blk_first = (blk_e != prev).astype(jnp.int32)
    blk_slot = ((jnp.cumsum(blk_first) - 1) % 2).astype(jnp.int32)
    later_used = (experts[None, :] > experts[:, None]) & (counts[None, :] > 0)
    next_used = jnp.min(jnp.where(later_used, experts[None, :], N_EXPERTS), axis=1)
    next_used = jnp.where(next_used == N_EXPERTS, -1, next_used)
    blk_next = jnp.sum(jnp.where(blk_e[:, None] == experts[None, :], next_used[None, :], 0), axis=1)
    return dest, (blk_e, blk_first, blk_rows, blk_slot, blk_next.astype(jnp.int32))


def _combine_kernel(y_ref, gate_ref, x1_ref, mod_ref, g_ref, *o_refs):
    i = pl.program_id(0)
    m = mod_ref[0]
    gate = gate_ref[...]
    acc_lo = None
    for k in range(TOP_K):
        lo, hi = _unpack_pairs(y_ref[k])
        gk = gate[:, k:k + 1]
        acc_lo = gk * lo if acc_lo is None else acc_lo + gk * lo
        acc_hi = gk * hi if k == 0 else acc_hi + gk * hi
    acc = jnp.concatenate([acc_lo, acc_hi], axis=1)
    out = x1_ref[...] + m[5:6] * _rms(acc, g_ref[...][3:4])
    if len(o_refs) == 1:
        o_refs[0][...] = out
    else:
        @pl.when(i < NPT)
        def _():
            o_refs[0][...] = out

        @pl.when(i >= NPT)
        def _():
            o_refs[1][...] = out


def _combine(yg, gates, x1, mod_l, g_l, split_out):
    if split_out:
        out_specs = [_PROMPT_SPEC, _SAMPLE_SPEC]
        out_shape = [jax.ShapeDtypeStruct((NP_TOK, D_MODEL), F32), jax.ShapeDtypeStruct((NS_TOK, D_MODEL), F32)]
    else:
        out_specs = [_tok_spec(D_MODEL)]
        out_shape = [jax.ShapeDtypeStruct((N_TOK, D_MODEL), F32)]
    return pl.pallas_call(
        _combine_kernel,
        grid=(N_TILES,),
        in_specs=[pl.BlockSpec((TOP_K, TM, ROW_WORDS), lambda i: (0, i, 0)), _tok_spec(LANES),
                  _tok_spec(D_MODEL), _MOD_SPEC, _const_spec((4, D_MODEL))],
        out_specs=out_specs,
        out_shape=out_shape,
        compiler_params=_cparams(1),
        name="combine",
    )(yg, gates, x1, mod_l, g_l)


def _ffn(layer, o_cat, xs, mod_l, g_l, w_out, w_router, b_router, w_gu, b_gu, w_down, b_down, split_out):
    w_r = jnp.pad(w_router, ((0, 0), (0, LANES - N_EXPERTS))).astype(BF16)
    b_r = jnp.pad(b_router, (0, LANES - N_EXPERTS)).reshape(1, LANES)
    x1, h2p, route, gate_slab, counts = _postmix(o_cat, xs, mod_l, g_l, w_out.astype(BF16), w_r, b_r)
    dest, blk_meta = _route(route, counts[0, :N_EXPERTS])
    n_chunks = N_TOK // (SC_WORKERS * SC_ROWS)
    idx_d = dest.reshape(SC_WORKERS, n_chunks, SC_ROWS, TOP_K).transpose(0, 1, 3, 2).reshape(
        SC_WORKERS, n_chunks * TOP_K, SC_ROWS)
    rows = _sc_dispatch(h2p, idx_d)
    ys = _moe(layer, blk_meta, rows, w_gu, b_gu, w_down, b_down)
    idx_c = dest.T.reshape(SC_WORKERS, TOP_K * n_chunks, SC_ROWS)
    yg = _sc_collect(ys, idx_c).reshape(TOP_K, N_TOK, ROW_WORDS)
    return _combine(yg, gate_slab, x1, mod_l, g_l, split_out)


def _pad_heads(w, n_heads, width, keep):
    k = w.shape[0]
    w = w.reshape(k, n_heads, width)[:, :, :keep]
    return jnp.pad(w, ((0, 0), (0, 0), (0, LANES - keep))).reshape(k, n_heads * LANES)


def _pe_slab(x):
    return jnp.pad(x, [(0, 0)] * (x.ndim - 1) + [(MLA_NOPE, LANES - MLA_NOPE - MLA_ROPE)])


_SWA_ORDER = np.array([0, 4, 1, 5, 2, 6, 3, 7])


def kernel(x_prompt, x_sample, cache_mla_ckv, cache_mla_krope, cache_diff_k, cache_diff_v, cache_swa_k, cache_swa_v, cache_na_k, cache_na_v, c, c_ctx, w_mod, b_mod, norm_g, w_in0, mla_q_norm, w_uq, mla_kv_norm, w_ukv, diff_lambda, diff_norm, w_out0, w_in1, swa_sink, na_rpb, w_out1, w_router, b_router, w_gu, b_gu, w_down, b_down):
    xs = (x_prompt.reshape(NP_TOK, D_MODEL), x_sample.reshape(NS_TOK, D_MODEL))
    cond = jnp.concatenate([c_ctx[None, :], c, jnp.zeros((16 - 1 - DEC_BATCH, D_MODEL), F32)], axis=0)
    mod = _modulation(cond, w_mod, b_mod).reshape(DEPTH, 16, 6, D_MODEL)
    t64, t32 = _rope_tables()
    states = {}
    for l in range(DEPTH):
        i = l // 2
        g_l = norm_g[l]
        mod_l = mod[l]
        if l % 2 == 0:
            lam_init = 0.8 - 0.6 * math.exp(-0.3 * l)
            wi = w_in0[i]
            w_in_p = jnp.concatenate(
                [wi[:, 0:640], wi[:, 672:2208], _pe_slab(wi[:, 640:672])], axis=1).astype(BF16)
            w_uq_p = _pad_heads(w_uq[i], MLA_HEADS, MLA_NOPE + MLA_ROPE, MLA_NOPE + MLA_ROPE).astype(BF16)
            w_k_p = _pad_heads(w_ukv[i], MLA_HEADS, MLA_NOPE + MLA_V, MLA_NOPE).astype(BF16)
            w_v = w_ukv[i].reshape(MLA_KV_RANK, MLA_HEADS, MLA_NOPE + MLA_V)[:, :, MLA_NOPE:].reshape(
                MLA_KV_RANK, MLA_HEADS * MLA_V).astype(BF16)
            (q, k, v, dq, dk, dv, ckv_st, kpe_st, dk_st, dv_st) = _premix0(
                *xs, mod_l, g_l[0:1], w_in_p, mla_q_norm[i][None, :], w_uq_p, mla_kv_norm[i][None, :],
                w_k_p, w_v, t32, t64)
            states['mla_ckv'] = ckv_st.reshape(BATCH, 1, SEQ, MLA_KV_RANK)
            states['mla_krope'] = kpe_st[:, MLA_NOPE:MLA_NOPE + MLA_ROPE].reshape(BATCH, 1, SEQ, MLA_ROPE)
            states['diff_k'] = dk_st.reshape(BATCH, 1, SEQ, DIFF_HEADS, 2 * DIFF_DH)
            states['diff_v'] = dv_st.reshape(BATCH, 1, SEQ, DIFF_HEADS, 2 * DIFF_DH)
            kc, vc = _mla_cache(cache_mla_ckv[:, i].reshape(DEC_BATCH * PAST_LEN, MLA_KV_RANK),
                                _pe_slab(cache_mla_krope[:, i].reshape(DEC_BATCH * PAST_LEN, MLA_ROPE)),
                                w_k_p, w_v)
            dkc = cache_diff_k[:, i].reshape(DEC_BATCH * PAST_LEN, 512).astype(BF16)
            dvc = cache_diff_v[:, i].reshape(DEC_BATCH * PAST_LEN, 512).astype(BF16)
            lam = diff_lambda[i]
            sub_g = diff_norm[i][None, :]
            o_p = _attn_ab(q, dq, (k, v, dk, dv), None, lam, sub_g, lam_init,
                           n_batch=BATCH, t_len=SEQ, tq=SEQ, tok_off=0)
            o_cat = _attn_ab(q, dq, (k, v, dk, dv), (kc, vc, dkc, dvc), lam, sub_g, lam_init,
                             n_batch=DEC_BATCH, t_len=DEC_SEQ, tq=TQ_AB, tok_off=NP_TOK, out_init=o_p)
            w_out = w_out0[i]
        else:
            wi = w_in1[i]
            sq_cols = (_SWA_ORDER[:, None] * HEAD_DIM + np.arange(HEAD_DIM)[None, :]).reshape(-1)
            w_in_p = jnp.concatenate([wi[:, sq_cols], wi[:, 512:]], axis=1).astype(BF16)
            (sq, sk, sv, nq, nk, nv, sk_st, sv_st, nk_st, nv_st) = _premix1(xs[0], mod_l, g_l[0:1], w_in_p, t64)
            states['swa_k'] = sk_st.reshape(BATCH, 1, SEQ, SWA_KV_HEADS, HEAD_DIM)
            states['swa_v'] = sv_st.reshape(BATCH, 1, SEQ, SWA_KV_HEADS, HEAD_DIM)
            states['na_k'] = nk_st.reshape(BATCH, 1, SEQ, NA_HEADS, HEAD_DIM)
            states['na_v'] = nv_st.reshape(BATCH, 1, SEQ, NA_HEADS, HEAD_DIM)
            skc = cache_swa_k[:, i].reshape(DEC_BATCH * PAST_LEN, 128).astype(BF16)
            svc = cache_swa_v[:, i].reshape(DEC_BATCH * PAST_LEN, 128).astype(BF16)
            nkc = cache_na_k[:, i].reshape(DEC_BATCH * PAST_LEN, 512).astype(BF16)
            nvc = cache_na_v[:, i].reshape(DEC_BATCH * PAST_LEN, 512).astype(BF16)
            sink = swa_sink[i]
            o_p = _attn_cd_prompt(sink, sq, sk, sv, nq, nk, nv)
            o_cat = _attn_cd_sample(o_p, sink, sq, sk, sv, nq, nk, nv, skc, svc, nkc, nvc, _na_bias(na_rpb[i]))
            wo = w_out1[i]
            w_out = jnp.concatenate([wo[sq_cols], wo[512:]], axis=0)
        xs = _ffn(l, o_cat, xs, mod_l, g_l, w_out, w_router[l], b_router[l], w_gu, b_gu, w_down, b_down,
                  split_out=(l == DEPTH - 1))
    return (xs[0].reshape(BATCH, SEQ, D_MODEL), xs[1].reshape(DEC_BATCH, DEC_SEQ, D_MODEL),
            states['mla_ckv'], states['mla_krope'], states['diff_k'], states['diff_v'],
            states['swa_k'], states['swa_v'], states['na_k'], states['na_v'])
```

```python
import functools
import math

import numpy as np
import jax
import jax.numpy as jnp
from jax import lax
from jax.experimental import pallas as pl
from jax.experimental.pallas import tpu as pltpu
from jax.experimental.pallas import tpu_sc as plsc

F32 = jnp.float32
BF16 = jnp.bfloat16

D_MODEL = 1024
BATCH = 16
SEQ = 256
DEPTH = 2
DEC_BATCH = 8
DEC_SEQ = 2048
PAST_LEN = 256
GRID_W = 64
HEAD_DIM = 64
ROPE_THETA = 10000.0
EPS = 1e-6
NEG = -1e30

MLA_HEADS = 8
MLA_Q_RANK = 384
MLA_KV_RANK = 256
MLA_NOPE = 64
MLA_ROPE = 32
MLA_V = 64
DIFF_HEADS = 4
DIFF_DH = 64
SWA_HEADS = 8
SWA_KV_HEADS = 2
SWA_WINDOW = 128
NA_HEADS = 8
NA_WIN_ROWS = 8
NA_WIN_COLS = 16
N_EXPERTS = 32
TOP_K = 4
D_EXPERT = 1024
SWIGLU_LIMIT = 7.0
SWIGLU_ALPHA = 1.702

LANES = 128
NP_TOK = BATCH * SEQ
NS_TOK = DEC_BATCH * DEC_SEQ
N_TOK = NP_TOK + NS_TOK
TM = 512
NPT = NP_TOK // TM
TILES_PER_SAMPLE = DEC_SEQ // TM
N_TILES = N_TOK // TM
N_SUB = 2
SUB_TM = TM // N_SUB
TQ = 256
TQ_AB = 256
MOE_TM = 512
MOE_TAIL_PARTS = 4
MOE_ROWS = ((N_TOK * TOP_K + N_EXPERTS * (MOE_TM - 1)) // MOE_TM + 1) * MOE_TM
MOE_BLOCKS = MOE_ROWS // MOE_TM
NA_TILE_ROWS = TQ // GRID_W
NA_KEY_ROWS = 12
VMEM_LIMIT = 56 * 1024 * 1024


def _cparams(n_axes, vmem=VMEM_LIMIT):
    return pltpu.CompilerParams(dimension_semantics=("arbitrary",) * n_axes,
                                vmem_limit_bytes=vmem)


def _rms(x, g):
    return x * lax.rsqrt(jnp.mean(x * x, axis=-1, keepdims=True) + EPS) * g


def _dot(a, b):
    return jnp.dot(a, b, preferred_element_type=F32)


def _dot_nt(a, b):
    return lax.dot_general(a, b, (((1,), (1,)), ((), ())), preferred_element_type=F32)


def _rope(x, cos, sin_a, sin_b, half):
    return (x * cos + pltpu.roll(x, LANES - half, 1) * sin_a + pltpu.roll(x, half, 1) * sin_b)


def _pipelined_units(units):
    pending = units[0][0](units[0][2])
    for n, (_, finish, arg) in enumerate(units):
        following = units[n + 1][0](units[n + 1][2]) if n + 1 < len(units) else None
        finish(arg, pending)
        pending = following


def _sub_rows(n):
    return slice(SUB_TM * n, SUB_TM * (n + 1))


def _mod_row(i):
    return jnp.where(i < NPT, 0, 1 + (i - NPT) // TILES_PER_SAMPLE)


def _rope_blk(i):
    return jnp.where(i < NPT, TILES_PER_SAMPLE, (i - NPT) % TILES_PER_SAMPLE)


def _mod_kernel(c_ref, w_ref, b_ref, o_ref):
    c = c_ref[...]
    s = (c * jax.nn.sigmoid(c)).astype(BF16)
    o_ref[0] = _dot(s, w_ref[0].astype(BF16)) + b_ref[0]


def _modulation(cond, w_mod, b_mod):
    nb = 1024
    return pl.pallas_call(
        _mod_kernel,
        grid=(DEPTH, 6 * D_MODEL // nb),
        in_specs=[
            pl.BlockSpec((16, D_MODEL), lambda l, n: (0, 0)),
            pl.BlockSpec((1, D_MODEL, nb), lambda l, n: (l, 0, n)),
            pl.BlockSpec((1, 1, nb), lambda l, n: (l, 0, n)),
        ],
        out_specs=pl.BlockSpec((1, 16, nb), lambda l, n: (l, 0, n)),
        out_shape=jax.ShapeDtypeStruct((DEPTH, 16, 6 * D_MODEL), F32),
        compiler_params=_cparams(2),
        name="modulation",
    )(cond, w_mod, b_mod.reshape(DEPTH, 1, 6 * D_MODEL))


def _rope_tables():
    t = jnp.arange(DEC_SEQ)
    rows = (t // GRID_W).astype(F32)
    cols = (t % GRID_W).astype(F32)

    def angles(r):
        n = r // 4
        inv = ROPE_THETA ** (-jnp.arange(n, dtype=F32) / n)
        return jnp.concatenate([rows[:, None] * inv[None], cols[:, None] * inv[None]], axis=-1)

    def finish(cos, sa, sb):
        ident = (jnp.ones((TM, LANES), F32), jnp.zeros((TM, LANES), F32), jnp.zeros((TM, LANES), F32))
        return tuple(jnp.concatenate([a, b], axis=0) for a, b in zip((cos, sa, sb), ident))

    a64 = angles(64)
    c, s, z = jnp.cos(a64), jnp.sin(a64), jnp.zeros_like(a64)
    t64 = finish(jnp.concatenate([c, c, c, c], -1), jnp.concatenate([-s, z, -s, z], -1),
                 jnp.concatenate([z, s, z, s], -1))
    a32 = angles(32)
    c, s, z = jnp.cos(a32), jnp.sin(a32), jnp.zeros_like(a32)
    one64 = jnp.ones((DEC_SEQ, 64), F32)
    z64 = jnp.zeros((DEC_SEQ, 64), F32)
    z32 = jnp.zeros((DEC_SEQ, 32), F32)
    t32 = finish(jnp.concatenate([one64, c, c, z32], -1), jnp.concatenate([z64, -s, z, z32], -1),
                 jnp.concatenate([z64, z, s, z32], -1))
    return t64, t32


LOG2E = math.log2(math.e)
_MLA_SCALE = (MLA_NOPE + MLA_ROPE) ** -0.5 * LOG2E
_QSCALE = HEAD_DIM ** -0.5 * LOG2E


def _premix0_kernel(xp_ref, xs_ref, mod_ref, g_ref, win_ref, qn_ref, wuq_ref, kvn_ref, wk_ref, wv_ref,
                    c32_ref, sa32_ref, sb32_ref, c64_ref, sa64_ref, sb64_ref,
                    q_ref, k_ref, v_ref, dq_ref, dk_ref, dv_ref,
                    ckv_st, kpe_st, dk_st, dv_st):
    i = pl.program_id(0)
    m = mod_ref[0]
    g = g_ref[...]

    def project(n):
        r = _sub_rows(n)
        x = jnp.where(i < NPT, xp_ref[r, :], xs_ref[r, :])
        h = _rms(x, g) * (1.0 + m[1:2]) + m[0:1]
        return _dot(h.astype(BF16), win_ref[...])

    states = {}

    def finish(n, proj):
        r = _sub_rows(n)
        q_a = proj[:, 0:384]
        kv_a = proj[:, 384:640]
        dq = proj[:, 640:1152]
        dk = proj[:, 1152:1664]
        dv = proj[:, 1664:2176]
        pe = proj[:, 2176:2304]
        q = _dot(_rms(q_a, qn_ref[...]).astype(BF16), wuq_ref[...])
        ckv = _rms(kv_a, kvn_ref[...])
        ckv_b = ckv.astype(BF16)
        kn = _dot(ckv_b, wk_ref[...])
        v_ref[r, :] = _dot(ckv_b, wv_ref[...]).astype(BF16)
        c32, sa32, sb32 = c32_ref[r, :], sa32_ref[r, :], sb32_ref[r, :]
        c64, sa64, sb64 = c64_ref[r, :], sa64_ref[r, :], sb64_ref[r, :]
        pe_r = _rope(pe, c32, sa32, sb32, MLA_ROPE // 2)
        for hd in range(MLA_HEADS):
            sl = slice(LANES * hd, LANES * (hd + 1))
            q_ref[r, sl] = (_rope(q[:, sl], c32, sa32, sb32, MLA_ROPE // 2) * _MLA_SCALE).astype(BF16)
            k_ref[r, sl] = (kn[:, sl] + pe_r).astype(BF16)
        for hd in range(DIFF_HEADS):
            sl = slice(LANES * hd, LANES * (hd + 1))
            dq_ref[r, sl] = (_rope(dq[:, sl], c64, sa64, sb64, DIFF_DH // 2) * _QSCALE).astype(BF16)
            dk_ref[r, sl] = _rope(dk[:, sl], c64, sa64, sb64, DIFF_DH // 2).astype(BF16)
        dv_ref[r, :] = dv.astype(BF16)
        states[n] = (ckv, pe, dk, dv)

    _pipelined_units([(project, finish, n) for n in range(N_SUB)])

    @pl.when(i < NPT)
    def _():
        for n in range(N_SUB):
            r = _sub_rows(n)
            ckv_st[r, :], kpe_st[r, :], dk_st[r, :], dv_st[r, :] = states[n]


def _premix1_kernel(x_ref, mod_ref, g_ref, win_ref, c64_ref, sa64_ref, sb64_ref,
                    sq_ref, sk_ref, sv_ref, nq_ref, nk_ref, nv_ref,
                    sk_st, sv_st, nk_st, nv_st):
    i = pl.program_id(0)
    m = mod_ref[0]
    g = g_ref[...]

    def project(n):
        h = _rms(x_ref[_sub_rows(n), :], g) * (1.0 + m[1:2]) + m[0:1]
        return _dot(h.astype(BF16), win_ref[...])

    states = {}

    def finish(n, proj):
        r = _sub_rows(n)
        sq = proj[:, 0:512]
        sk = proj[:, 512:640]
        sv = proj[:, 640:768]
        nq = proj[:, 768:1280]
        nk = proj[:, 1280:1792]
        nv = proj[:, 1792:2304]
        c64, sa64, sb64 = c64_ref[r, :], sa64_ref[r, :], sb64_ref[r, :]
        for hd in range(4):
            sl = slice(LANES * hd, LANES * (hd + 1))
            sq_ref[r, sl] = (_rope(sq[:, sl], c64, sa64, sb64, HEAD_DIM // 2) * _QSCALE).astype(BF16)
        sk_ref[r, :] = _rope(sk, c64, sa64, sb64, HEAD_DIM // 2).astype(BF16)
        sv_ref[r, :] = sv.astype(BF16)
        nq_ref[r, :] = (nq * _QSCALE).astype(BF16)
        nk_ref[r, :] = nk.astype(BF16)
        nv_ref[r, :] = nv.astype(BF16)
        states[n] = (sk, sv, nk, nv)

    _pipelined_units([(project, finish, n) for n in range(N_SUB)])

    @pl.when(i < NPT)
    def _():
        for n in range(N_SUB):
            r = _sub_rows(n)
            sk_st[r, :], sv_st[r, :], nk_st[r, :], nv_st[r, :] = states[n]


def _tok_spec(width):
    return pl.BlockSpec((TM, width), lambda i: (i, 0))


_PROMPT_SPEC = pl.BlockSpec((TM, D_MODEL), lambda i: (jnp.minimum(i, NPT - 1), 0))
_SAMPLE_SPEC = pl.BlockSpec((TM, D_MODEL), lambda i: (jnp.maximum(i - NPT, 0), 0))


def _state_spec(width):
    return pl.BlockSpec((TM, width), lambda i: (jnp.minimum(i, NPT - 1), 0))


def _const_spec(shape):
    return pl.BlockSpec(shape, lambda i: (0,) * len(shape))


_MOD_SPEC = pl.BlockSpec((1, 6, D_MODEL), lambda i: (_mod_row(i), 0, 0))
_ROPE_SPEC = pl.BlockSpec((TM, LANES), lambda i: (_rope_blk(i), 0))


def _premix0(xp, xs, mod_l, g0, w_in_p, q_norm, w_uq_p, kv_norm, w_k_p, w_v, t32, t64):
    outs = [(N_TOK, 1024, BF16), (N_TOK, 1024, BF16), (N_TOK, 512, BF16), (N_TOK, 512, BF16),
            (N_TOK, 512, BF16), (N_TOK, 512, BF16),
            (NP_TOK, 256, F32), (NP_TOK, 128, F32), (NP_TOK, 512, F32), (NP_TOK, 512, F32)]
    return pl.pallas_call(
        _premix0_kernel,
        grid=(N_TILES,),
        in_specs=[_PROMPT_SPEC, _SAMPLE_SPEC, _MOD_SPEC, _const_spec((1, D_MODEL)),
                  _const_spec((D_MODEL, 2304)), _const_spec((1, MLA_Q_RANK)),
                  _const_spec((MLA_Q_RANK, 1024)), _const_spec((1, MLA_KV_RANK)),
                  _const_spec((MLA_KV_RANK, 1024)), _const_spec((MLA_KV_RANK, 512))]
                 + [_ROPE_SPEC] * 6,
        out_specs=[_tok_spec(w) for (_, w, _) in outs[:6]] + [_state_spec(w) for (_, w, _) in outs[6:]],
        out_shape=[jax.ShapeDtypeStruct((n, w), dt) for (n, w, dt) in outs],
        compiler_params=_cparams(1),
        name="premix_ab",
    )(xp, xs, mod_l, g0, w_in_p, q_norm, w_uq_p, kv_norm, w_k_p, w_v, *t32, *t64)


def _premix1(x, mod_l, g0, w_in_p, t64):
    outs = [(N_TOK, 512, BF16), (N_TOK, 128, BF16), (N_TOK, 128, BF16), (N_TOK, 512, BF16),
            (N_TOK, 512, BF16), (N_TOK, 512, BF16),
            (NP_TOK, 128, F32), (NP_TOK, 128, F32), (NP_TOK, 512, F32), (NP_TOK, 512, F32)]
    return pl.pallas_call(
        _premix1_kernel,
        grid=(N_TILES,),
        in_specs=[_tok_spec(D_MODEL), _MOD_SPEC, _const_spec((1, D_MODEL)),
                  _const_spec((D_MODEL, 2304))] + [_ROPE_SPEC] * 3,
        out_specs=[_tok_spec(w) for (_, w, _) in outs[:6]] + [_state_spec(w) for (_, w, _) in outs[6:]],
        out_shape=[jax.ShapeDtypeStruct((n, w), dt) for (n, w, dt) in outs],
        compiler_params=_cparams(1),
        name="premix_cd",
    )(x, mod_l, g0, w_in_p, *t64)


def _mla_cache_kernel(ckv_ref, pe_ref, wk_ref, wv_ref, k_ref, v_ref):
    c = ckv_ref[...].astype(BF16)
    kn = _dot(c, wk_ref[...])
    v_ref[...] = _dot(c, wv_ref[...]).astype(BF16)
    pe = pe_ref[...]
    for hd in range(MLA_HEADS):
        sl = slice(LANES * hd, LANES * (hd + 1))
        k_ref[:, sl] = (kn[:, sl] + pe).astype(BF16)


def _mla_cache(ckv, pe_slab, w_k_p, w_v):
    n = ckv.shape[0]
    tm = 512
    return pl.pallas_call(
        _mla_cache_kernel,
        grid=(n // tm,),
        in_specs=[pl.BlockSpec((tm, MLA_KV_RANK), lambda i: (i, 0)),
                  pl.BlockSpec((tm, LANES), lambda i: (i, 0)),
                  _const_spec((MLA_KV_RANK, 1024)), _const_spec((MLA_KV_RANK, 512))],
        out_specs=[pl.BlockSpec((tm, 1024), lambda i: (i, 0)), pl.BlockSpec((tm, 512), lambda i: (i, 0))],
        out_shape=[jax.ShapeDtypeStruct((n, 1024), BF16), jax.ShapeDtypeStruct((n, 512), BF16)],
        compiler_params=_cparams(1),
        name="mla_cache",
    )(ckv, pe_slab, w_k_p, w_v)


def _softmax_pv(scores, values, sink=None):
    m = jnp.max(scores[0], axis=-1, keepdims=True)
    for s in scores[1:]:
        m = jnp.maximum(m, jnp.max(s, axis=-1, keepdims=True))
    if sink is not None:
        m = jnp.maximum(m, sink)
    l = None
    o = None
    for s, v in zip(scores, values):
        p = jnp.exp2(s - m)
        ls = jnp.sum(p, axis=-1, keepdims=True)
        os_ = _dot(p.astype(BF16), v)
        l = ls if l is None else l + ls
        o = os_ if o is None else o + os_
    if sink is not None:
        l = l + jnp.exp2(sink - m)
    return o * (1.0 / l)


def _lane_lo(shape):
    return lax.broadcasted_iota(jnp.int32, shape, 1) < (LANES // 2)


def _split_halves(qb):
    lo = _lane_lo(qb.shape)
    zero = jnp.zeros_like(qb)
    return jnp.where(lo, qb, zero), jnp.where(lo, zero, qb)


def _attn_ab_kernel(*refs, n_pieces, lam_init, aliased):
    if aliased:
        refs = refs[1:]
    q_ref, dq_ref = refs[0], refs[1]
    pieces = [refs[2 + 4 * p: 6 + 4 * p] for p in range(n_pieces)]
    lam_ref, subg_ref, o_ref = refs[2 + 4 * n_pieces:]
    lam = lam_ref[...]
    lam_full = (jnp.exp(jnp.sum(lam[0:1] * lam[1:2], axis=-1, keepdims=True))
                - jnp.exp(jnp.sum(lam[2:3] * lam[3:4], axis=-1, keepdims=True)) + lam_init)
    lo = _lane_lo((q_ref.shape[0], LANES))
    subg = subg_ref[...]
    tq = q_ref.shape[0]
    def mla_scores(hd):
        sl = slice(LANES * hd, LANES * (hd + 1))
        qh = q_ref[:, sl]
        return [_dot_nt(qh, k_ref[:, sl]) for (k_ref, _, _, _) in pieces]

    def diff_scores(hd):
        sl = slice(LANES * hd, LANES * (hd + 1))
        qq = jnp.concatenate(_split_halves(dq_ref[:, sl]), axis=0)
        return [_dot_nt(qq, dk_ref[:, sl]) for (_, _, dk_ref, _) in pieces]

    pair = {}

    def mla_finish(hd, scores):
        j = hd // 2
        vals = [v_ref[:, LANES * j: LANES * (j + 1)] for (_, v_ref, _, _) in pieces]
        pair[hd % 2] = _softmax_pv(scores, vals)
        if hd % 2 == 1:
            o_ref[:, LANES * j: LANES * (j + 1)] = jnp.where(lo, pair[0], pair[1]).astype(BF16)

    def diff_finish(hd, scores):
        sl = slice(LANES * hd, LANES * (hd + 1))
        oo = _softmax_pv(scores, [dv_ref[:, sl] for (_, _, _, dv_ref) in pieces])
        od = _rms(oo[:tq] - lam_full * oo[tq:], subg) * (1.0 - lam_init)
        o_ref[:, 512 + LANES * hd: 512 + LANES * (hd + 1)] = od.astype(BF16)

    units = []
    for j in range(DIFF_HEADS):
        units += [(diff_scores, diff_finish, j), (mla_scores, mla_finish, 2 * j),
                  (mla_scores, mla_finish, 2 * j + 1)]
    _pipelined_units(units)


def _attn_ab(q, dq, new_kv, cache_kv, lam, sub_g, lam_init, *, n_batch, t_len, tq, tok_off, out_init=None):
    nq = t_len // tq
    q_off = tok_off // tq
    b_off = tok_off // t_len
    widths = (1024, 512, 512, 512)
    in_specs = [pl.BlockSpec((tq, 1024), lambda b, i: (q_off + b * nq + i, 0)),
                pl.BlockSpec((tq, 512), lambda b, i: (q_off + b * nq + i, 0))]
    args = [q, dq]
    for w, a in zip(widths, new_kv):
        in_specs.append(pl.BlockSpec((t_len, w), lambda b, i: (b_off + b, 0)))
        args.append(a)
    n_pieces = 1
    if cache_kv is not None:
        n_pieces = 2
        for w, a in zip(widths, cache_kv):
            in_specs.append(pl.BlockSpec((PAST_LEN, w), lambda b, i: (b, 0)))
            args.append(a)
    in_specs += [pl.BlockSpec((4, DIFF_DH), lambda b, i: (0, 0)),
                 pl.BlockSpec((1, 2 * DIFF_DH), lambda b, i: (0, 0))]
    args += [lam, sub_g]
    aliases = {}
    if out_init is not None:
        in_specs = [pl.BlockSpec(memory_space=pl.ANY)] + in_specs
        args = [out_init] + args
        aliases = {0: 0}
    return pl.pallas_call(
        functools.partial(_attn_ab_kernel, n_pieces=n_pieces, lam_init=lam_init, aliased=out_init is not None),
        grid=(n_batch, nq),
        in_specs=in_specs,
        out_specs=pl.BlockSpec((tq, 1024), lambda b, i: (q_off + b * nq + i, 0)),
        out_shape=jax.ShapeDtypeStruct((N_TOK, 1024), BF16),
        input_output_aliases=aliases,
        compiler_params=_cparams(2),
        name="attn_ab_%d" % n_pieces,
    )(*args)


def _gqa_stacks(sq_ref, sink_ref):
    tq = sq_ref.shape[0]
    halves = [_split_halves(sq_ref[:, LANES * j: LANES * (j + 1)]) for j in range(4)]
    q_stacks = [jnp.concatenate([halves[j][kvh] for j in range(4)], axis=0) for kvh in range(SWA_KV_HEADS)]
    sinks = [jnp.concatenate([jnp.full((tq, 1), sink_ref[4 * kvh + j] * LOG2E, F32) for j in range(4)], axis=0)
             for kvh in range(SWA_KV_HEADS)]
    return q_stacks, sinks


def _attn_cd_prompt_kernel(sink_ref, sq_ref, sk_ref, sv_ref, nq_ref, nk_ref, nv_ref, o_ref):
    tq = sq_ref.shape[0]
    lo = _lane_lo((tq, LANES))
    sk = sk_ref[...]
    sv = sv_ref[...]
    for j in range(4):
        sl = slice(LANES * j, LANES * (j + 1))
        q_lo, q_hi = _split_halves(sq_ref[:, sl])
        o_lo = _softmax_pv([_dot_nt(q_lo, sk)], [sv], sink=sink_ref[j] * LOG2E)
        o_hi = _softmax_pv([_dot_nt(q_hi, sk)], [sv], sink=sink_ref[j + 4] * LOG2E)
        o_ref[:, sl] = jnp.where(lo, o_lo, o_hi).astype(BF16)
    for j in range(4):
        sl = slice(LANES * j, LANES * (j + 1))
        q_lo, q_hi = _split_halves(nq_ref[:, sl])
        k = nk_ref[:, sl]
        v = nv_ref[:, sl]
        o_lo = _softmax_pv([_dot_nt(q_lo, k)], [v])
        o_hi = _softmax_pv([_dot_nt(q_hi, k)], [v])
        o_ref[:, 512 + LANES * j: 512 + LANES * (j + 1)] = jnp.where(lo, o_lo, o_hi).astype(BF16)


def _attn_cd_prompt(sink, sq, sk, sv, nq, nk, nv):
    def spec(w):
        return pl.BlockSpec((SEQ, w), lambda b: (b, 0))
    return pl.pallas_call(
        _attn_cd_prompt_kernel,
        grid=(BATCH,),
        in_specs=[pl.BlockSpec(memory_space=pltpu.SMEM), spec(512), spec(128), spec(128),
                  spec(512), spec(512), spec(512)],
        out_specs=spec(1024),
        out_shape=jax.ShapeDtypeStruct((N_TOK, 1024), BF16),
        compiler_params=_cparams(1),
        name="attn_cd_prompt",
    )(sink, sq, sk, sv, nq, nk, nv)


_SWA_KEYS = TQ + 2 * SWA_WINDOW


def _attn_cd_sample_kernel(init_ref, sink_ref, sq_ref, nq_ref, sk_ref, sv_ref, nk_ref, nv_ref,
                           skc_ref, svc_ref, nkc_ref, nvc_ref, bias_ref, o_ref):
    del init_ref
    qi = pl.program_id(1)
    lo = _lane_lo((TQ, LANES))
    ks = pl.multiple_of(jnp.clip(qi * TQ - SWA_WINDOW, 0, DEC_SEQ - _SWA_KEYS), SWA_WINDOW)
    k_win = sk_ref[pl.ds(ks, _SWA_KEYS), :]
    v_win = sv_ref[pl.ds(ks, _SWA_KEYS), :]
    q_pos = qi * TQ + (lax.broadcasted_iota(jnp.int32, (4 * TQ, _SWA_KEYS), 0) & (TQ - 1))
    k_pos = ks + lax.broadcasted_iota(jnp.int32, (4 * TQ, _SWA_KEYS), 1)
    in_win = jnp.abs(q_pos - k_pos) <= SWA_WINDOW
    skc = skc_ref[...]
    svc = svc_ref[...]
    q_stacks, sinks = _gqa_stacks(sq_ref, sink_ref)

    def window_scores(kvh):
        return [_dot_nt(q_stacks[kvh], skc), jnp.where(in_win, _dot_nt(q_stacks[kvh], k_win), NEG)]

    o_kv = {}

    def window_finish(kvh, scores):
        o_kv[kvh] = _softmax_pv(scores, [svc, v_win], sink=sinks[kvh])
        if kvh == SWA_KV_HEADS - 1:
            for j in range(4):
                rows = slice(TQ * j, TQ * (j + 1))
                o_ref[:, LANES * j: LANES * (j + 1)] = jnp.where(lo, o_kv[0][rows], o_kv[1][rows]).astype(BF16)

    n_rows = DEC_SEQ // GRID_W
    r0 = jnp.clip(qi * NA_TILE_ROWS - NA_WIN_ROWS // 2, 0, n_rows - NA_KEY_ROWS)
    kn = pl.multiple_of(r0 * GRID_W, GRID_W)

    lane_lo = _lane_lo((1, LANES))
    pieces = []
    for ri in range(NA_TILE_ROWS):
        r = qi * NA_TILE_ROWS + ri
        rs = jnp.clip(r - NA_WIN_ROWS // 2, 0, n_rows - NA_WIN_ROWS)
        row = []
        for mm in range(NA_KEY_ROWS // 2):
            kr = r0 + 2 * mm
            ok = [(kr + t >= rs) & (kr + t < rs + NA_WIN_ROWS) for t in range(2)]
            mask = jnp.where(lane_lo, jnp.where(ok[0], 0.0, NEG), jnp.where(ok[1], 0.0, NEG))
            row.append((jnp.clip(kr - r + NA_WIN_ROWS, 0, 2 * NA_WIN_ROWS - 1), mask))
        pieces.append(row)

    def na_bias(hd):
        return jnp.concatenate(
            [jnp.concatenate([bias_ref[hd, d] + mask for d, mask in row], axis=1) for row in pieces], axis=0)

    def na_scores(hd):
        sl = slice(LANES * (hd // 2), LANES * (hd // 2 + 1))
        q_half = _split_halves(nq_ref[:, sl])[hd % 2]
        nk_win = nk_ref[pl.ds(kn, NA_KEY_ROWS * GRID_W), sl]
        return [_dot_nt(q_half, nkc_ref[:, sl]), _dot_nt(q_half, nk_win) + na_bias(hd)]

    pair = {}

    def na_finish(hd, scores):
        j = hd // 2
        sl = slice(LANES * j, LANES * (j + 1))
        nv_win = nv_ref[pl.ds(kn, NA_KEY_ROWS * GRID_W), sl]
        pair[hd % 2] = _softmax_pv(scores, [nvc_ref[:, sl], nv_win])
        if hd % 2 == 1:
            o_ref[:, 512 + LANES * j: 512 + LANES * (j + 1)] = jnp.where(lo, pair[0], pair[1]).astype(BF16)

    _pipelined_units([(window_scores, window_finish, kvh) for kvh in range(SWA_KV_HEADS)]
                     + [(na_scores, na_finish, hd) for hd in range(NA_HEADS)])


def _attn_cd_sample(out_init, sink, sq, sk, sv, nq, nk, nv, skc, svc, nkc, nvc, bias):
    nq_t = DEC_SEQ // TQ
    q_off = NP_TOK // TQ
    b_off = NP_TOK // DEC_SEQ

    def qspec(w):
        return pl.BlockSpec((TQ, w), lambda b, i: (q_off + b * nq_t + i, 0))

    def kspec(w):
        return pl.BlockSpec((DEC_SEQ, w), lambda b, i: (b_off + b, 0))

    def cspec(w):
        return pl.BlockSpec((PAST_LEN, w), lambda b, i: (b, 0))

    bias_spec = pl.BlockSpec((NA_HEADS, 2 * NA_WIN_ROWS, GRID_W, LANES), lambda b, i: (0, 0, 0, 0))
    return pl.pallas_call(
        _attn_cd_sample_kernel,
        grid=(DEC_BATCH, nq_t),
        in_specs=[pl.BlockSpec(memory_space=pl.ANY), pl.BlockSpec(memory_space=pltpu.SMEM),
                  qspec(512), qspec(512), kspec(128), kspec(128), kspec(512), kspec(512),
                  cspec(128), cspec(128), cspec(512), cspec(512), bias_spec],
        out_specs=pl.BlockSpec((TQ, 1024), lambda b, i: (q_off + b * nq_t + i, 0)),
        out_shape=jax.ShapeDtypeStruct((N_TOK, 1024), BF16),
        input_output_aliases={0: 0},
        compiler_params=_cparams(2),
        name="attn_cd_sample",
    )(out_init, sink, sq, nq, sk, sv, nk, nv, skc, svc, nkc, nvc, bias)


def _na_bias(rpb):
    n_dc = 2 * NA_WIN_COLS - 1
    c = np.arange(GRID_W)[:, None]
    kc = np.arange(GRID_W)[None, :]
    qs = np.clip(c - NA_WIN_COLS // 2, 0, GRID_W - NA_WIN_COLS)
    col_ok = (kc >= qs) & (kc < qs + NA_WIN_COLS)
    dc = np.clip(kc - c + NA_WIN_COLS - 1, 0, n_dc - 1)
    onehot = ((dc[None] == np.arange(n_dc)[:, None, None]) & col_ok[None]).astype(np.float32)
    blocks = jnp.einsum('hrd,dck->hrck', rpb.astype(F32) * LOG2E, onehot, precision=lax.Precision.HIGHEST)
    blocks = jnp.where(col_ok[None, None], blocks, NEG)
    none = jnp.full((NA_HEADS, 1, GRID_W, GRID_W), NEG, F32)
    return jnp.concatenate([jnp.concatenate([none, blocks], axis=1),
                            jnp.concatenate([blocks, none], axis=1)], axis=-1)


_HI_MASK = -65536


def _pack_pairs(x):
    w = x.shape[1] // 2
    r = x.astype(BF16).astype(F32)
    lo = lax.bitcast_convert_type(r[:, :w], jnp.int32)
    hi = lax.bitcast_convert_type(r[:, w:], jnp.int32)
    return (hi & _HI_MASK) | lax.shift_right_logical(lo, 16)


def _unpack_pairs(p):
    lo = lax.bitcast_convert_type(lax.shift_left(p, 16), F32)
    hi = lax.bitcast_convert_type(p & _HI_MASK, F32)
    return lo, hi


def _postmix_kernel(*refs, split_x):
    if split_x:
        o_ref, xp_ref, xs_ref = refs[:3]
        refs = refs[3:]
    else:
        o_ref, x_ref = refs[:2]
        refs = refs[2:]
    (mod_ref, g_ref, wout_ref, wr_ref, br_ref, tri_ref,
     x1_ref, h2_ref, route_ref, gate_ref, cnt_ref, run_ref) = refs
    i = pl.program_id(0)

    @pl.when(i == 0)
    def _():
        run_ref[...] = jnp.zeros_like(run_ref)

    m = mod_ref[0]
    g = g_ref[...]
    tri = tri_ref[...]
    counts = {-1: run_ref[0:1, :]}

    def project(n):
        return _dot(o_ref[_sub_rows(n), :], wout_ref[...])

    def finish(n, y):
        r = _sub_rows(n)
        x = jnp.where(i < NPT, xp_ref[r, :], xs_ref[r, :]) if split_x else x_ref[r, :]
        x1 = x + m[2:3] * _rms(y, g[1:2])
        x1_ref[r, :] = x1
        h2 = _rms(x1, g[2:3]) * (1.0 + m[4:5]) + m[3:4]
        h2_ref[r, :] = _pack_pairs(h2)
        logits = _dot(h2.astype(BF16), wr_ref[...]) + br_ref[...]
        lane = lax.broadcasted_iota(jnp.int32, logits.shape, 1).astype(F32)
        cur = jnp.where(lane < N_EXPERTS, logits, -jnp.inf)
        tops, idxs = [], []
        for _ in range(TOP_K):
            mx = jnp.max(cur, axis=-1, keepdims=True)
            ix = jnp.min(jnp.where(cur == mx, lane, float(LANES)), axis=-1, keepdims=True)
            tops.append(mx)
            idxs.append(ix)
            cur = jnp.where(lane == ix, -jnp.inf, cur)
        es = [jnp.exp(t - tops[0]) for t in tops]
        inv = 1.0 / (es[0] + es[1] + es[2] + es[3])
        picked = jnp.zeros_like(logits)
        for k in range(TOP_K):
            picked = jnp.where(lane == idxs[k], 1.0, picked)
        before = _dot(tri, picked.astype(BF16)) + counts[n - 1]
        route = jnp.zeros_like(logits)
        gate_out = jnp.zeros_like(logits)
        for k in range(TOP_K):
            rank = jnp.sum(jnp.where(lane == idxs[k], before, 0.0), axis=-1, keepdims=True)
            route = jnp.where(lane == float(k), idxs[k], route)
            route = jnp.where(lane == float(TOP_K + k), rank, route)
            gate_out = jnp.where(lane == float(k), es[k] * inv, gate_out)
        route_ref[r, :] = route.astype(jnp.int32)
        gate_ref[r, :] = gate_out
        counts[n] = counts[n - 1] + jnp.sum(picked, axis=0, keepdims=True)

    _pipelined_units([(project, finish, n) for n in range(N_SUB)])
    run_ref[...] = jnp.broadcast_to(counts[N_SUB - 1], run_ref.shape)
    cnt_ref[...] = run_ref[...].astype(jnp.int32)


def _postmix(o_cat, xs, mod_l, g_l, w_out, w_r, b_r):
    tri = jnp.asarray(np.tril(np.ones((SUB_TM, SUB_TM), np.float32), -1), BF16)
    split_x = len(xs) == 2
    x_specs = [_PROMPT_SPEC, _SAMPLE_SPEC] if split_x else [_tok_spec(D_MODEL)]
    return pl.pallas_call(
        functools.partial(_postmix_kernel, split_x=split_x),
        grid=(N_TILES,),
        in_specs=[_tok_spec(1024)] + x_specs + [_MOD_SPEC, _const_spec((4, D_MODEL)),
                  _const_spec((1024, D_MODEL)), _const_spec((D_MODEL, LANES)), _const_spec((1, LANES)),
                  _const_spec((SUB_TM, SUB_TM))],
        out_specs=[_tok_spec(D_MODEL), _tok_spec(D_MODEL // 2), _tok_spec(LANES), _tok_spec(LANES),
                   _const_spec((8, LANES))],
        out_shape=[jax.ShapeDtypeStruct((N_TOK, D_MODEL), F32),
                   jax.ShapeDtypeStruct((N_TOK, D_MODEL // 2), jnp.int32),
                   jax.ShapeDtypeStruct((N_TOK, LANES), jnp.int32), jax.ShapeDtypeStruct((N_TOK, LANES), F32),
                   jax.ShapeDtypeStruct((8, LANES), jnp.int32)],
        scratch_shapes=[pltpu.VMEM((8, LANES), F32)],
        compiler_params=_cparams(1),
        name="postmix",
    )(o_cat, *xs, mod_l, g_l, w_out, w_r, b_r, tri)


SC_WORKERS = 32
SC_ROWS = 128
ROW_WORDS = D_MODEL // 2


def _sc_worker_id():
    return lax.axis_index("s") * 2 + lax.axis_index("c")


def _sc_dispatch(src, idx):
    n_chunks = N_TOK // (SC_WORKERS * SC_ROWS)
    mesh = plsc.VectorSubcoreMesh(core_axis_name="c", subcore_axis_name="s")

    @functools.partial(
        pl.kernel, mesh=mesh,
        out_type=jax.ShapeDtypeStruct((MOE_ROWS, ROW_WORDS), jnp.int32),
        scratch_types=[pltpu.VMEM((n_chunks * TOP_K, SC_ROWS), jnp.int32),
                       pltpu.VMEM((SC_ROWS, ROW_WORDS), jnp.int32), pltpu.SemaphoreType.DMA])
    def k(src_hbm, idx_hbm, out_hbm, idx_v, rows_v, sem):
        wid = _sc_worker_id()
        pltpu.sync_copy(idx_hbm.at[wid], idx_v)

        @pl.loop(0, n_chunks)
        def _(g):
            pltpu.sync_copy(src_hbm.at[pl.ds((wid * n_chunks + g) * SC_ROWS, SC_ROWS)], rows_v)
            copies = [pltpu.async_copy(rows_v, out_hbm.at[idx_v.at[g * TOP_K + kk]], sem)
                      for kk in range(TOP_K)]
            for cp in copies:
                cp.wait()

    return k(src, idx)


def _sc_collect(table, idx):
    n_chunks = idx.shape[1]
    mesh = plsc.VectorSubcoreMesh(core_axis_name="c", subcore_axis_name="s")

    @functools.partial(
        pl.kernel, mesh=mesh,
        out_type=jax.ShapeDtypeStruct((SC_WORKERS * n_chunks * SC_ROWS, ROW_WORDS), jnp.int32),
        scratch_types=[pltpu.VMEM((n_chunks, SC_ROWS), jnp.int32),
                       pltpu.VMEM((SC_ROWS, ROW_WORDS), jnp.int32), pltpu.SemaphoreType.DMA])
    def k(table_hbm, idx_hbm, out_hbm, idx_v, rows_v, sem):
        wid = _sc_worker_id()
        pltpu.sync_copy(idx_hbm.at[wid], idx_v)

        @pl.loop(0, n_chunks)
        def _(g):
            pltpu.async_copy(table_hbm.at[idx_v.at[g]], rows_v, sem).wait()
            pltpu.sync_copy(rows_v, out_hbm.at[pl.ds((wid * n_chunks + g) * SC_ROWS, SC_ROWS)])

    return k(table, idx)


def _expert_rows(words, n_valid, wgu_b, wd_b, bgu, bd):
    live = lax.broadcasted_iota(jnp.int32, words.shape, 0) < n_valid
    lo, hi = _unpack_pairs(jnp.where(live, words, 0))
    x = jnp.concatenate([lo, hi], axis=1).astype(BF16)
    gu = _dot(x, wgu_b[...]) + bgu
    g = jnp.minimum(gu[:, :D_EXPERT], SWIGLU_LIMIT)
    u = jnp.clip(gu[:, D_EXPERT:], -SWIGLU_LIMIT, SWIGLU_LIMIT)
    a = g * jax.nn.sigmoid(SWIGLU_ALPHA * g) * (u + 1.0)
    return _pack_pairs(_dot(a.astype(BF16), wd_b[...]) + bd)


def _moe_kernel(blk_e_ref, blk_first_ref, blk_rows_ref, blk_slot_ref, blk_next_ref,
                x_ref, wgu_hbm, bgu_ref, wd_hbm, bd_ref, y_ref,
                wgu_f, wd_f, wgu_b, wd_b, sem, *, layer):
    i = pl.program_id(0)
    n_valid = blk_rows_ref[i]
    quantum = MOE_TM // MOE_TAIL_PARTS

    def weight_copies(e, slot):
        return (pltpu.make_async_copy(wgu_hbm.at[layer, e], wgu_f.at[slot], sem.at[0, slot]),
                pltpu.make_async_copy(wd_hbm.at[layer, e], wd_f.at[slot], sem.at[1, slot]))

    @pl.when(i == 0)
    def _():
        for cp in weight_copies(blk_e_ref[0], blk_slot_ref[0]):
            cp.start()

    @pl.when(blk_first_ref[i] == 1)
    def _():
        slot = blk_slot_ref[i]
        for cp in weight_copies(blk_e_ref[i], slot):
            cp.wait()
        nxt = blk_next_ref[i]

        @pl.when(nxt >= 0)
        def _():
            for cp in weight_copies(nxt, 1 - slot):
                cp.start()

        wgu_b[...] = wgu_f[slot].astype(BF16)
        wd_b[...] = wd_f[slot].astype(BF16)

    for parts in range(1, MOE_TAIL_PARTS + 1):
        rows = parts * quantum

        @pl.when((n_valid > rows - quantum) & (n_valid <= rows))
        def _(rows=rows):
            y_ref[:rows] = _expert_rows(x_ref[:rows], n_valid, wgu_b, wd_b, bgu_ref[0, 0], bd_ref[0, 0])
            if rows < MOE_TM:
                y_ref[rows:] = jnp.zeros((MOE_TM - rows, ROW_WORDS), jnp.int32)

    @pl.when(n_valid == 0)
    def _():
        y_ref[...] = jnp.zeros_like(y_ref)


def _moe(layer, blk_meta, xs, w_gu, b_gu, w_down, b_down):
    def row_map(i, *_):
        return (i, 0)

    def bias_map(i, e, *_):
        return (layer, e[i], 0, 0)

    grid_spec = pltpu.PrefetchScalarGridSpec(
        num_scalar_prefetch=5,
        grid=(MOE_BLOCKS,),
        in_specs=[
            pl.BlockSpec((MOE_TM, ROW_WORDS), row_map),
            pl.BlockSpec(memory_space=pl.ANY),
            pl.BlockSpec((1, 1, 1, 2 * D_EXPERT), bias_map),
            pl.BlockSpec(memory_space=pl.ANY),
            pl.BlockSpec((1, 1, 1, D_MODEL), bias_map),
        ],
        out_specs=pl.BlockSpec((MOE_TM, ROW_WORDS), row_map),
        scratch_shapes=[pltpu.VMEM((2, D_MODEL, 2 * D_EXPERT), F32), pltpu.VMEM((2, D_EXPERT, D_MODEL), F32),
                        pltpu.VMEM((D_MODEL, 2 * D_EXPERT), BF16), pltpu.VMEM((D_EXPERT, D_MODEL), BF16),
                        pltpu.SemaphoreType.DMA((2, 2))],
    )
    return pl.pallas_call(
        functools.partial(_moe_kernel, layer=layer),
        grid_spec=grid_spec,
        out_shape=jax.ShapeDtypeStruct((MOE_ROWS, ROW_WORDS), jnp.int32),
        compiler_params=_cparams(1),
        name="moe_experts",
    )(*blk_meta, xs, w_gu, b_gu.reshape(DEPTH, N_EXPERTS, 1, 2 * D_EXPERT),
      w_down, b_down.reshape(DEPTH, N_EXPERTS, 1, D_MODEL))


def _route(route, counts):
    experts = jnp.arange(N_EXPERTS, dtype=jnp.int32)
    padded = (counts + MOE_TM - 1) // MOE_TM * MOE_TM
    pad_end = jnp.cumsum(padded)
    pad_start = pad_end - padded
    e = route[:, 0:TOP_K]
    onehot = e[:, :, None] == experts[None, None, :]
    dest = jnp.sum(jnp.where(onehot, pad_start[None, None, :], 0), axis=-1) + route[:, TOP_K:2 * TOP_K]
    blk_row0 = jnp.arange(MOE_BLOCKS, dtype=jnp.int32) * MOE_TM
    blk_e = jnp.minimum(jnp.sum((pad_end[None, :] <= blk_row0[:, None]).astype(jnp.int32), axis=1),
                        N_EXPERTS - 1)
    on = blk_row0 < pad_end[-1]
    n_on = jnp.sum(on.astype(jnp.int32))
    blk_onehot = blk_e[:, None] == experts[None, :]
    row_end = jnp.sum(jnp.where(blk_onehot, (pad_start + counts)[None, :], 0), axis=1)
    blk_rows = jnp.where(on, jnp.clip(row_end - blk_row0, 0, MOE_TM), 0).astype(jnp.int32)
    last_e = jnp.sum(jnp.where(jnp.arange(MOE_BLOCKS) == n_on - 1, blk_e, 0))
    blk_e = jnp.where(on, blk_e, last_e).astype(jnp.int32)
    prev = jnp.concatenate([jnp.full((1,), -1, jnp.int32), blk_e[:-1]])
    blk_first = (blk_e != prev).astype(jnp.int32)
    blk_slot = ((jnp.cumsum(blk_first) - 1) % 2).astype(jnp.int32)
    later_used = (experts[None, :] > experts[:, None]) & (counts[None, :] > 0)
    next_used = jnp.min(jnp.where(later_used, experts[None, :], N_EXPERTS), axis=1)
    next_used = jnp.where(next_used == N_EXPERTS, -1, next_used)
    blk_next = jnp.sum(jnp.where(blk_e[:, None] == experts[None, :], next_used[None, :], 0), axis=1)
    return dest, (blk_e, blk_first, blk_rows, blk_slot, blk_next.astype(jnp.int32))


def _combine_kernel(y_ref, gate_ref, x1_ref, mod_ref, g_ref, *o_refs):
    i = pl.program_id(0)
    m = mod_ref[0]
    gate = gate_ref[...]
    acc_lo = None
    for k in range(TOP_K):
        lo, hi = _unpack_pairs(y_ref[k])
        gk = gate[:, k:k + 1]
        acc_lo = gk * lo if acc_lo is None else acc_lo + gk * lo
        acc_hi = gk * hi if k == 0 else acc_hi + gk * hi
    acc = jnp.concatenate([acc_lo, acc_hi], axis=1)
    out = x1_ref[...] + m[5:6] * _rms(acc, g_ref[...][3:4])
    if len(o_refs) == 1:
        o_refs[0][...] = out
    else:
        @pl.when(i < NPT)
        def _():
            o_refs[0][...] = out

        @pl.when(i >= NPT)
        def _():
            o_refs[1][...] = out


def _combine(yg, gates, x1, mod_l, g_l, split_out):
    if split_out:
        out_specs = [_PROMPT_SPEC, _SAMPLE_SPEC]
        out_shape = [jax.ShapeDtypeStruct((NP_TOK, D_MODEL), F32), jax.ShapeDtypeStruct((NS_TOK, D_MODEL), F32)]
    else:
        out_specs = [_tok_spec(D_MODEL)]
        out_shape = [jax.ShapeDtypeStruct((N_TOK, D_MODEL), F32)]
    return pl.pallas_call(
        _combine_kernel,
        grid=(N_TILES,),
        in_specs=[pl.BlockSpec((TOP_K, TM, ROW_WORDS), lambda i: (0, i, 0)), _tok_spec(LANES),
                  _tok_spec(D_MODEL), _MOD_SPEC, _const_spec((4, D_MODEL))],
        out_specs=out_specs,
        out_shape=out_shape,
        compiler_params=_cparams(1),
        name="combine",
    )(yg, gates, x1, mod_l, g_l)


def _ffn(layer, o_cat, xs, mod_l, g_l, w_out, w_router, b_router, w_gu, b_gu, w_down, b_down, split_out):
    w_r = jnp.pad(w_router, ((0, 0), (0, LANES - N_EXPERTS))).astype(BF16)
    b_r = jnp.pad(b_router, (0, LANES - N_EXPERTS)).reshape(1, LANES)
    x1, h2p, route, gate_slab, counts = _postmix(o_cat, xs, mod_l, g_l, w_out.astype(BF16), w_r, b_r)
    dest, blk_meta = _route(route, counts[0, :N_EXPERTS])
    n_chunks = N_TOK // (SC_WORKERS * SC_ROWS)
    idx_d = dest.reshape(SC_WORKERS, n_chunks, SC_ROWS, TOP_K).transpose(0, 1, 3, 2).reshape(
        SC_WORKERS, n_chunks * TOP_K, SC_ROWS)
    rows = _sc_dispatch(h2p, idx_d)
    ys = _moe(layer, blk_meta, rows, w_gu, b_gu, w_down, b_down)
    idx_c = dest.T.reshape(SC_WORKERS, TOP_K * n_chunks, SC_ROWS)
    yg = _sc_collect(ys, idx_c).reshape(TOP_K, N_TOK, ROW_WORDS)
    return _combine(yg, gate_slab, x1, mod_l, g_l, split_out)


def _pad_heads(w, n_heads, width, keep):
    k = w.shape[0]
    w = w.reshape(k, n_heads, width)[:, :, :keep]
    return jnp.pad(w, ((0, 0), (0, 0), (0, LANES - keep))).reshape(k, n_heads * LANES)


def _pe_slab(x):
    return jnp.pad(x, [(0, 0)] * (x.ndim - 1) + [(MLA_NOPE, LANES - MLA_NOPE - MLA_ROPE)])


_SWA_ORDER = np.array([0, 4, 1, 5, 2, 6, 3, 7])


def kernel(x_prompt, x_sample, cache_mla_ckv, cache_mla_krope, cache_diff_k, cache_diff_v, cache_swa_k, cache_swa_v, cache_na_k, cache_na_v, c, c_ctx, w_mod, b_mod, norm_g, w_in0, mla_q_norm, w_uq, mla_kv_norm, w_ukv, diff_lambda, diff_norm, w_out0, w_in1, swa_sink, na_rpb, w_out1, w_router, b_router, w_gu, b_gu, w_down, b_down):
    xs = (x_prompt.reshape(NP_TOK, D_MODEL), x_sample.reshape(NS_TOK, D_MODEL))
    cond = jnp.concatenate([c_ctx[None, :], c, jnp.zeros((16 - 1 - DEC_BATCH, D_MODEL), F32)], axis=0)
    mod = _modulation(cond, w_mod, b_mod).reshape(DEPTH, 16, 6, D_MODEL)
    t64, t32 = _rope_tables()
    states = {}
    for l in range(DEPTH):
        i = l // 2
        g_l = norm_g[l]
        mod_l = mod[l]
        if l % 2 == 0:
            lam_init = 0.8 - 0.6 * math.exp(-0.3 * l)
            wi = w_in0[i]
            w_in_p = jnp.concatenate(
                [wi[:, 0:640], wi[:, 672:2208], _pe_slab(wi[:, 640:672])], axis=1).astype(BF16)
            w_uq_p = _pad_heads(w_uq[i], MLA_HEADS, MLA_NOPE + MLA_ROPE, MLA_NOPE + MLA_ROPE).astype(BF16)
            w_k_p = _pad_heads(w_ukv[i], MLA_HEADS, MLA_NOPE + MLA_V, MLA_NOPE).astype(BF16)
            w_v = w_ukv[i].reshape(MLA_KV_RANK, MLA_HEADS, MLA_NOPE + MLA_V)[:, :, MLA_NOPE:].reshape(
                MLA_KV_RANK, MLA_HEADS * MLA_V).astype(BF16)
            (q, k, v, dq, dk, dv, ckv_st, kpe_st, dk_st, dv_st) = _premix0(
                *xs, mod_l, g_l[0:1], w_in_p, mla_q_norm[i][None, :], w_uq_p, mla_kv_norm[i][None, :],
                w_k_p, w_v, t32, t64)
            states['mla_ckv'] = ckv_st.reshape(BATCH, 1, SEQ, MLA_KV_RANK)
            states['mla_krope'] = kpe_st[:, MLA_NOPE:MLA_NOPE + MLA_ROPE].reshape(BATCH, 1, SEQ, MLA_ROPE)
            states['diff_k'] = dk_st.reshape(BATCH, 1, SEQ, DIFF_HEADS, 2 * DIFF_DH)
            states['diff_v'] = dv_st.reshape(BATCH, 1, SEQ, DIFF_HEADS, 2 * DIFF_DH)
            kc, vc = _mla_cache(cache_mla_ckv[:, i].reshape(DEC_BATCH * PAST_LEN, MLA_KV_RANK),
                                _pe_slab(cache_mla_krope[:, i].reshape(DEC_BATCH * PAST_LEN, MLA_ROPE)),
                                w_k_p, w_v)
            dkc = cache_diff_k[:, i].reshape(DEC_BATCH * PAST_LEN, 512).astype(BF16)
            dvc = cache_diff_v[:, i].reshape(DEC_BATCH * PAST_LEN, 512).astype(BF16)
            lam = diff_lambda[i]
            sub_g = diff_norm[i][None, :]
            o_p = _attn_ab(q, dq, (k, v, dk, dv), None, lam, sub_g, lam_init,
                           n_batch=BATCH, t_len=SEQ, tq=SEQ, tok_off=0)
            o_cat = _attn_ab(q, dq, (k, v, dk, dv), (kc, vc, dkc, dvc), lam, sub_g, lam_init,
                             n_batch=DEC_BATCH, t_len=DEC_SEQ, tq=TQ_AB, tok_off=NP_TOK, out_init=o_p)
            w_out = w_out0[i]
        else:
            wi = w_in1[i]
            sq_cols = (_SWA_ORDER[:, None] * HEAD_DIM + np.arange(HEAD_DIM)[None, :]).reshape(-1)
            w_in_p = jnp.concatenate([wi[:, sq_cols], wi[:, 512:]], axis=1).astype(BF16)
            (sq, sk, sv, nq, nk, nv, sk_st, sv_st, nk_st, nv_st) = _premix1(xs[0], mod_l, g_l[0:1], w_in_p, t64)
            states['swa_k'] = sk_st.reshape(BATCH, 1, SEQ, SWA_KV_HEADS, HEAD_DIM)
            states['swa_v'] = sv_st.reshape(BATCH, 1, SEQ, SWA_KV_HEADS, HEAD_DIM)
            states['na_k'] = nk_st.reshape(BATCH, 1, SEQ, NA_HEADS, HEAD_DIM)
            states['na_v'] = nv_st.reshape(BATCH, 1, SEQ, NA_HEADS, HEAD_DIM)
            skc = cache_swa_k[:, i].reshape(DEC_BATCH * PAST_LEN, 128).astype(BF16)
            svc = cache_swa_v[:, i].reshape(DEC_BATCH * PAST_LEN, 128).astype(BF16)
            nkc = cache_na_k[:, i].reshape(DEC_BATCH * PAST_LEN, 512).astype(BF16)
            nvc = cache_na_v[:, i].reshape(DEC_BATCH * PAST_LEN, 512).astype(BF16)
            sink = swa_sink[i]
            o_p = _attn_cd_prompt(sink, sq, sk, sv, nq, nk, nv)
            o_cat = _attn_cd_sample(o_p, sink, sq, sk, sv, nq, nk, nv, skc, svc, nkc, nvc, _na_bias(na_rpb[i]))
            wo = w_out1[i]
            w_out = jnp.concatenate([wo[sq_cols], wo[512:]], axis=0)
        xs = _ffn(l, o_cat, xs, mod_l, g_l, w_out, w_router[l], b_router[l], w_gu, b_gu, w_down, b_down,
                  split_out=(l == DEPTH - 1))
    return (xs[0].reshape(BATCH, SEQ, D_MODEL), xs[1].reshape(DEC_BATCH, DEC_SEQ, D_MODEL),
            states['mla_ckv'], states['mla_krope'], states['diff_k'], states['diff_v'],
            states['swa_k'], states['swa_v'], states['na_k'], states['na_v'])
```

```python
import functools
import math

import numpy as np
import jax
import jax.numpy as jnp
from jax import lax
from jax.experimental import pallas as pl
from jax.experimental.pallas import tpu as pltpu
from jax.experimental.pallas import tpu_sc as plsc

F32 = jnp.float32
BF16 = jnp.bfloat16

D_MODEL = 1024
BATCH = 16
SEQ = 256
DEPTH = 2
DEC_BATCH = 8
DEC_SEQ = 2048
PAST_LEN = 256
GRID_W = 64
HEAD_DIM = 64
ROPE_THETA = 10000.0
EPS = 1e-6
NEG = -1e30

MLA_HEADS = 8
MLA_Q_RANK = 384
MLA_KV_RANK = 256
MLA_NOPE = 64
MLA_ROPE = 32
MLA_V = 64
DIFF_HEADS = 4
DIFF_DH = 64
SWA_HEADS = 8
SWA_KV_HEADS = 2
SWA_WINDOW = 128
NA_HEADS = 8
NA_WIN_ROWS = 8
NA_WIN_COLS = 16
N_EXPERTS = 32
TOP_K = 4
D_EXPERT = 1024
SWIGLU_LIMIT = 7.0
SWIGLU_ALPHA = 1.702

LANES = 128
NP_TOK = BATCH * SEQ
NS_TOK = DEC_BATCH * DEC_SEQ
N_TOK = NP_TOK + NS_TOK
TM = 512
NPT = NP_TOK // TM
TILES_PER_SAMPLE = DEC_SEQ // TM
N_TILES = N_TOK // TM
N_SUB = 2
SUB_TM = TM // N_SUB
TQ = 256
TQ_AB = 256
MOE_TM = 512
MOE_TAIL_PARTS = 4
MOE_ROWS = ((N_TOK * TOP_K + N_EXPERTS * (MOE_TM - 1)) // MOE_TM + 1) * MOE_TM
MOE_BLOCKS = MOE_ROWS // MOE_TM
NA_TILE_ROWS = TQ // GRID_W
NA_KEY_ROWS = 12
VMEM_LIMIT = 56 * 1024 * 1024


def _cparams(n_axes, vmem=VMEM_LIMIT):
    return pltpu.CompilerParams(dimension_semantics=("arbitrary",) * n_axes,
                                vmem_limit_bytes=vmem)


def _rms(x, g):
    return x * lax.rsqrt(jnp.mean(x * x, axis=-1, keepdims=True) + EPS) * g


def _dot(a, b):
    return jnp.dot(a, b, preferred_element_type=F32)


def _dot_nt(a, b):
    return lax.dot_general(a, b, (((1,), (1,)), ((), ())), preferred_element_type=F32)


def _rope(x, cos, sin_a, sin_b, half):
    return (x * cos + pltpu.roll(x, LANES - half, 1) * sin_a + pltpu.roll(x, half, 1) * sin_b)


def _pipelined_units(units):
    pending = units[0][0](units[0][2])
    for n, (_, finish, arg) in enumerate(units):
        following = units[n + 1][0](units[n + 1][2]) if n + 1 < len(units) else None
        finish(arg, pending)
        pending = following


def _sub_rows(n):
    return slice(SUB_TM * n, SUB_TM * (n + 1))


def _store_head_rows(st_ref, n, hd, n_heads, x):
    st_ref[pl.ds(SUB_TM * n * n_heads + hd, SUB_TM, stride=n_heads), :] = x


def _mod_row(i):
    return jnp.where(i < NPT, 0, 1 + (i - NPT) // TILES_PER_SAMPLE)


def _rope_blk(i):
    return jnp.where(i < NPT, TILES_PER_SAMPLE, (i - NPT) % TILES_PER_SAMPLE)


def _mod_kernel(c_ref, w_ref, b_ref, o_ref):
    c = c_ref[...]
    s = (c * jax.nn.sigmoid(c)).astype(BF16)
    o_ref[0] = _dot(s, w_ref[0].astype(BF16)) + b_ref[0]


def _modulation(cond, w_mod, b_mod):
    nb = 1024
    return pl.pallas_call(
        _mod_kernel,
        grid=(DEPTH, 6 * D_MODEL // nb),
        in_specs=[
            pl.BlockSpec((16, D_MODEL), lambda l, n: (0, 0)),
            pl.BlockSpec((1, D_MODEL, nb), lambda l, n: (l, 0, n)),
            pl.BlockSpec((1, 1, nb), lambda l, n: (l, 0, n)),
        ],
        out_specs=pl.BlockSpec((1, 16, nb), lambda l, n: (l, 0, n)),
        out_shape=jax.ShapeDtypeStruct((DEPTH, 16, 6 * D_MODEL), F32),
        compiler_params=_cparams(2),
        name="modulation",
    )(cond, w_mod, b_mod.reshape(DEPTH, 1, 6 * D_MODEL))


def _rope_tables():
    t = jnp.arange(DEC_SEQ)
    rows = (t // GRID_W).astype(F32)
    cols = (t % GRID_W).astype(F32)

    def angles(r):
        n = r // 4
        inv = ROPE_THETA ** (-jnp.arange(n, dtype=F32) / n)
        return jnp.concatenate([rows[:, None] * inv[None], cols[:, None] * inv[None]], axis=-1)

    def finish(cos, sa, sb):
        ident = (jnp.ones((TM, LANES), F32), jnp.zeros((TM, LANES), F32), jnp.zeros((TM, LANES), F32))
        return tuple(jnp.concatenate([a, b], axis=0) for a, b in zip((cos, sa, sb), ident))

    a64 = angles(64)
    c, s, z = jnp.cos(a64), jnp.sin(a64), jnp.zeros_like(a64)
    t64 = finish(jnp.concatenate([c, c, c, c], -1), jnp.concatenate([-s, z, -s, z], -1),
                 jnp.concatenate([z, s, z, s], -1))
    a32 = angles(32)
    c, s, z = jnp.cos(a32), jnp.sin(a32), jnp.zeros_like(a32)
    one64 = jnp.ones((DEC_SEQ, 64), F32)
    z64 = jnp.zeros((DEC_SEQ, 64), F32)
    z32 = jnp.zeros((DEC_SEQ, 32), F32)
    t32 = finish(jnp.concatenate([one64, c, c, z32], -1), jnp.concatenate([z64, -s, z, z32], -1),
                 jnp.concatenate([z64, z, s, z32], -1))
    return t64, t32


LOG2E = math.log2(math.e)
_MLA_SCALE = (MLA_NOPE + MLA_ROPE) ** -0.5 * LOG2E
_QSCALE = HEAD_DIM ** -0.5 * LOG2E


def _premix0_kernel(xp_ref, xs_ref, mod_ref, g_ref, win_ref, qn_ref, wuq_ref, kvn_ref, wk_ref, wv_ref,
                    c32_ref, sa32_ref, sb32_ref, c64_ref, sa64_ref, sb64_ref,
                    q_ref, k_ref, v_ref, dq_ref, dk_ref, dv_ref,
                    ckv_st, kpe_st, dk_st, dv_st):
    i = pl.program_id(0)
    m = mod_ref[0]
    g = g_ref[...]

    def project(n):
        r = _sub_rows(n)
        x = jnp.where(i < NPT, xp_ref[r, :], xs_ref[r, :])
        h = _rms(x, g) * (1.0 + m[1:2]) + m[0:1]
        return _dot(h.astype(BF16), win_ref[...])

    states = {}

    def finish(n, proj):
        r = _sub_rows(n)
        q_a = proj[:, 0:384]
        kv_a = proj[:, 384:640]
        dq = proj[:, 640:1152]
        dk = proj[:, 1152:1664]
        dv = proj[:, 1664:2176]
        pe = proj[:, 2176:2304]
        q = _dot(_rms(q_a, qn_ref[...]).astype(BF16), wuq_ref[...])
        ckv = _rms(kv_a, kvn_ref[...])
        ckv_b = ckv.astype(BF16)
        kn = _dot(ckv_b, wk_ref[...])
        v_ref[r, :] = _dot(ckv_b, wv_ref[...]).astype(BF16)
        c32, sa32, sb32 = c32_ref[r, :], sa32_ref[r, :], sb32_ref[r, :]
        c64, sa64, sb64 = c64_ref[r, :], sa64_ref[r, :], sb64_ref[r, :]
        pe_r = _rope(pe, c32, sa32, sb32, MLA_ROPE // 2)
        for hd in range(MLA_HEADS):
            sl = slice(LANES * hd, LANES * (hd + 1))
            q_ref[r, sl] = (_rope(q[:, sl], c32, sa32, sb32, MLA_ROPE // 2) * _MLA_SCALE).astype(BF16)
            k_ref[r, sl] = (kn[:, sl] + pe_r).astype(BF16)
        for hd in range(DIFF_HEADS):
            sl = slice(LANES * hd, LANES * (hd + 1))
            dq_ref[r, sl] = (_rope(dq[:, sl], c64, sa64, sb64, DIFF_DH // 2) * _QSCALE).astype(BF16)
            dk_ref[r, sl] = _rope(dk[:, sl], c64, sa64, sb64, DIFF_DH // 2).astype(BF16)
        dv_ref[r, :] = dv.astype(BF16)
        states[n] = (ckv, pe, dk, dv)

    _pipelined_units([(project, finish, n) for n in range(N_SUB)])

    @pl.when(i < NPT)
    def _():
        for n in range(N_SUB):
            r = _sub_rows(n)
            ckv, pe, dk, dv = states[n]
            ckv_st[r, :] = ckv
            kpe_st[r, :] = pe
            for hd in range(DIFF_HEADS):
                _store_head_rows(dk_st, n, hd, DIFF_HEADS, dk[:, LANES * hd: LANES * (hd + 1)])
                _store_head_rows(dv_st, n, hd, DIFF_HEADS, dv[:, LANES * hd: LANES * (hd + 1)])


def _premix1_kernel(x_ref, mod_ref, g_ref, win_ref, c64_ref, sa64_ref, sb64_ref,
                    sq_ref, sk_ref, sv_ref, nq_ref, nk_ref, nv_ref,
                    sk_st, sv_st, nk_st, nv_st):
    i = pl.program_id(0)
    m = mod_ref[0]
    g = g_ref[...]

    def project(n):
        h = _rms(x_ref[_sub_rows(n), :], g) * (1.0 + m[1:2]) + m[0:1]
        return _dot(h.astype(BF16), win_ref[...])

    states = {}

    def finish(n, proj):
        r = _sub_rows(n)
        sq = proj[:, 0:512]
        sk = proj[:, 512:640]
        sv = proj[:, 640:768]
        nq = proj[:, 768:1280]
        nk = proj[:, 1280:1792]
        nv = proj[:, 1792:2304]
        c64, sa64, sb64 = c64_ref[r, :], sa64_ref[r, :], sb64_ref[r, :]
        for hd in range(4):
            sl = slice(LANES * hd, LANES * (hd + 1))
            sq_ref[r, sl] = (_rope(sq[:, sl], c64, sa64, sb64, HEAD_DIM // 2) * _QSCALE).astype(BF16)
        sk_ref[r, :] = _rope(sk, c64, sa64, sb64, HEAD_DIM // 2).astype(BF16)
        sv_ref[r, :] = sv.astype(BF16)
        nq_ref[r, :] = (nq * _QSCALE).astype(BF16)
        nk_ref[r, :] = nk.astype(BF16)
        nv_ref[r, :] = nv.astype(BF16)
        states[n] = (sk, sv, nk, nv)

    _pipelined_units([(project, finish, n) for n in range(N_SUB)])

    @pl.when(i < NPT)
    def _():
        for n in range(N_SUB):
            r = _sub_rows(n)
            sk, sv, nk, nv = states[n]
            sk_st[r, :] = sk
            sv_st[r, :] = sv
            for hd in range(NA_HEADS):
                _store_head_rows(nk_st, n, hd, NA_HEADS, nk[:, HEAD_DIM * hd: HEAD_DIM * (hd + 1)])
                _store_head_rows(nv_st, n, hd, NA_HEADS, nv[:, HEAD_DIM * hd: HEAD_DIM * (hd + 1)])


def _tok_spec(width):
    return pl.BlockSpec((TM, width), lambda i: (i, 0))


_PROMPT_SPEC = pl.BlockSpec((TM, D_MODEL), lambda i: (jnp.minimum(i, NPT - 1), 0))
_SAMPLE_SPEC = pl.BlockSpec((TM, D_MODEL), lambda i: (jnp.maximum(i - NPT, 0), 0))


def _state_spec(width, rows_per_token=1):
    return pl.BlockSpec((TM * rows_per_token, width), lambda i: (jnp.minimum(i, NPT - 1), 0))


def _const_spec(shape):
    return pl.BlockSpec(shape, lambda i: (0,) * len(shape))


_MOD_SPEC = pl.BlockSpec((1, 6, D_MODEL), lambda i: (_mod_row(i), 0, 0))
_ROPE_SPEC = pl.BlockSpec((TM, LANES), lambda i: (_rope_blk(i), 0))


def _premix0(xp, xs, mod_l, g0, w_in_p, q_norm, w_uq_p, kv_norm, w_k_p, w_v, t32, t64):
    outs = [(N_TOK, 1024, BF16), (N_TOK, 1024, BF16), (N_TOK, 512, BF16), (N_TOK, 512, BF16),
            (N_TOK, 512, BF16), (N_TOK, 512, BF16),
            (NP_TOK, 256, F32), (NP_TOK, 128, F32),
            (NP_TOK * DIFF_HEADS, 2 * DIFF_DH, F32), (NP_TOK * DIFF_HEADS, 2 * DIFF_DH, F32)]
    return pl.pallas_call(
        _premix0_kernel,
        grid=(N_TILES,),
        in_specs=[_PROMPT_SPEC, _SAMPLE_SPEC, _MOD_SPEC, _const_spec((1, D_MODEL)),
                  _const_spec((D_MODEL, 2304)), _const_spec((1, MLA_Q_RANK)),
                  _const_spec((MLA_Q_RANK, 1024)), _const_spec((1, MLA_KV_RANK)),
                  _const_spec((MLA_KV_RANK, 1024)), _const_spec((MLA_KV_RANK, 512))]
                 + [_ROPE_SPEC] * 6,
        out_specs=([_tok_spec(w) for (_, w, _) in outs[:6]]
                   + [_state_spec(w, n // NP_TOK) for (n, w, _) in outs[6:]]),
        out_shape=[jax.ShapeDtypeStruct((n, w), dt) for (n, w, dt) in outs],
        compiler_params=_cparams(1),
        name="premix_ab",
    )(xp, xs, mod_l, g0, w_in_p, q_norm, w_uq_p, kv_norm, w_k_p, w_v, *t32, *t64)


def _premix1(x, mod_l, g0, w_in_p, t64):
    outs = [(N_TOK, 512, BF16), (N_TOK, 128, BF16), (N_TOK, 128, BF16), (N_TOK, 512, BF16),
            (N_TOK, 512, BF16), (N_TOK, 512, BF16),
            (NP_TOK, 128, F32), (NP_TOK, 128, F32),
            (NP_TOK * NA_HEADS, HEAD_DIM, F32), (NP_TOK * NA_HEADS, HEAD_DIM, F32)]
    return pl.pallas_call(
        _premix1_kernel,
        grid=(N_TILES,),
        in_specs=[_tok_spec(D_MODEL), _MOD_SPEC, _const_spec((1, D_MODEL)),
                  _const_spec((D_MODEL, 2304))] + [_ROPE_SPEC] * 3,
        out_specs=([_tok_spec(w) for (_, w, _) in outs[:6]]
                   + [_state_spec(w, n // NP_TOK) for (n, w, _) in outs[6:]]),
        out_shape=[jax.ShapeDtypeStruct((n, w), dt) for (n, w, dt) in outs],
        compiler_params=_cparams(1),
        name="premix_cd",
    )(x, mod_l, g0, w_in_p, *t64)


def _mla_cache_kernel(ckv_ref, pe_ref, wk_ref, wv_ref, k_ref, v_ref):
    c = ckv_ref[...].astype(BF16)
    kn = _dot(c, wk_ref[...])
    v_ref[...] = _dot(c, wv_ref[...]).astype(BF16)
    pe = pe_ref[...]
    for hd in range(MLA_HEADS):
        sl = slice(LANES * hd, LANES * (hd + 1))
        k_ref[:, sl] = (kn[:, sl] + pe).astype(BF16)


def _mla_cache(ckv, pe_slab, w_k_p, w_v):
    n = ckv.shape[0]
    tm = 512
    return pl.pallas_call(
        _mla_cache_kernel,
        grid=(n // tm,),
        in_specs=[pl.BlockSpec((tm, MLA_KV_RANK), lambda i: (i, 0)),
                  pl.BlockSpec((tm, LANES), lambda i: (i, 0)),
                  _const_spec((MLA_KV_RANK, 1024)), _const_spec((MLA_KV_RANK, 512))],
        out_specs=[pl.BlockSpec((tm, 1024), lambda i: (i, 0)), pl.BlockSpec((tm, 512), lambda i: (i, 0))],
        out_shape=[jax.ShapeDtypeStruct((n, 1024), BF16), jax.ShapeDtypeStruct((n, 512), BF16)],
        compiler_params=_cparams(1),
        name="mla_cache",
    )(ckv, pe_slab, w_k_p, w_v)


def _softmax_pv(scores, values, sink=None):
    m = jnp.max(scores[0], axis=-1, keepdims=True)
    for s in scores[1:]:
        m = jnp.maximum(m, jnp.max(s, axis=-1, keepdims=True))
    if sink is not None:
        m = jnp.maximum(m, sink)
    l = None
    o = None
    for s, v in zip(scores, values):
        p = jnp.exp2(s - m)
        ls = jnp.sum(p, axis=-1, keepdims=True)
        os_ = _dot(p.astype(BF16), v)
        l = ls if l is None else l + ls
        o = os_ if o is None else o + os_
    if sink is not None:
        l = l + jnp.exp2(sink - m)
    return o * (1.0 / l)


def _lane_lo(shape):
    return lax.broadcasted_iota(jnp.int32, shape, 1) < (LANES // 2)


def _split_halves(qb):
    lo = _lane_lo(qb.shape)
    zero = jnp.zeros_like(qb)
    return jnp.where(lo, qb, zero), jnp.where(lo, zero, qb)


def _attn_ab_kernel(*refs, n_pieces, lam_init, aliased):
    if aliased:
        refs = refs[1:]
    q_ref, dq_ref = refs[0], refs[1]
    pieces = [refs[2 + 4 * p: 6 + 4 * p] for p in range(n_pieces)]
    lam_ref, subg_ref, o_ref = refs[2 + 4 * n_pieces:]
    lam = lam_ref[...]
    lam_full = (jnp.exp(jnp.sum(lam[0:1] * lam[1:2], axis=-1, keepdims=True))
                - jnp.exp(jnp.sum(lam[2:3] * lam[3:4], axis=-1, keepdims=True)) + lam_init)
    lo = _lane_lo((q_ref.shape[0], LANES))
    subg = subg_ref[...]
    tq = q_ref.shape[0]
    def mla_scores(hd):
        sl = slice(LANES * hd, LANES * (hd + 1))
        qh = q_ref[:, sl]
        return [_dot_nt(qh, k_ref[:, sl]) for (k_ref, _, _, _) in pieces]

    def diff_scores(hd):
        sl = slice(LANES * hd, LANES * (hd + 1))
        qq = jnp.concatenate(_split_halves(dq_ref[:, sl]), axis=0)
        return [_dot_nt(qq, dk_ref[:, sl]) for (_, _, dk_ref, _) in pieces]

    pair = {}

    def mla_finish(hd, scores):
        j = hd // 2
        vals = [v_ref[:, LANES * j: LANES * (j + 1)] for (_, v_ref, _, _) in pieces]
        pair[hd % 2] = _softmax_pv(scores, vals)
        if hd % 2 == 1:
            o_ref[:, LANES * j: LANES * (j + 1)] = jnp.where(lo, pair[0], pair[1]).astype(BF16)

    def diff_finish(hd, scores):
        sl = slice(LANES * hd, LANES * (hd + 1))
        oo = _softmax_pv(scores, [dv_ref[:, sl] for (_, _, _, dv_ref) in pieces])
        od = _rms(oo[:tq] - lam_full * oo[tq:], subg) * (1.0 - lam_init)
        o_ref[:, 512 + LANES * hd: 512 + LANES * (hd + 1)] = od.astype(BF16)

    units = []
    for j in range(DIFF_HEADS):
        units += [(diff_scores, diff_finish, j), (mla_scores, mla_finish, 2 * j),
                  (mla_scores, mla_finish, 2 * j + 1)]
    _pipelined_units(units)


def _attn_ab(q, dq, new_kv, cache_kv, lam, sub_g, lam_init, *, n_batch, t_len, tq, tok_off, out_init=None):
    nq = t_len // tq
    q_off = tok_off // tq
    b_off = tok_off // t_len
    widths = (1024, 512, 512, 512)
    in_specs = [pl.BlockSpec((tq, 1024), lambda b, i: (q_off + b * nq + i, 0)),
                pl.BlockSpec((tq, 512), lambda b, i: (q_off + b * nq + i, 0))]
    args = [q, dq]
    for w, a in zip(widths, new_kv):
        in_specs.append(pl.BlockSpec((t_len, w), lambda b, i: (b_off + b, 0)))
        args.append(a)
    n_pieces = 1
    if cache_kv is not None:
        n_pieces = 2
        for w, a in zip(widths, cache_kv):
            in_specs.append(pl.BlockSpec((PAST_LEN, w), lambda b, i: (b, 0)))
            args.append(a)
    in_specs += [pl.BlockSpec((4, DIFF_DH), lambda b, i: (0, 0)),
                 pl.BlockSpec((1, 2 * DIFF_DH), lambda b, i: (0, 0))]
    args += [lam, sub_g]
    aliases = {}
    if out_init is not None:
        in_specs = [pl.BlockSpec(memory_space=pl.ANY)] + in_specs
        args = [out_init] + args
        aliases = {0: 0}
    return pl.pallas_call(
        functools.partial(_attn_ab_kernel, n_pieces=n_pieces, lam_init=lam_init, aliased=out_init is not None),
        grid=(n_batch, nq),
        in_specs=in_specs,
        out_specs=pl.BlockSpec((tq, 1024), lambda b, i: (q_off + b * nq + i, 0)),
        out_shape=jax.ShapeDtypeStruct((N_TOK, 1024), BF16),
        input_output_aliases=aliases,
        compiler_params=_cparams(2),
        name="attn_ab_%d" % n_pieces,
    )(*args)


def _gqa_stacks(sq_ref, sink_ref):
    tq = sq_ref.shape[0]
    halves = [_split_halves(sq_ref[:, LANES * j: LANES * (j + 1)]) for j in range(4)]
    q_stacks = [jnp.concatenate([halves[j][kvh] for j in range(4)], axis=0) for kvh in range(SWA_KV_HEADS)]
    sinks = [jnp.concatenate([jnp.full((tq, 1), sink_ref[4 * kvh + j] * LOG2E, F32) for j in range(4)], axis=0)
             for kvh in range(SWA_KV_HEADS)]
    return q_stacks, sinks


def _attn_cd_prompt_kernel(sink_ref, sq_ref, sk_ref, sv_ref, nq_ref, nk_ref, nv_ref, o_ref):
    tq = sq_ref.shape[0]
    lo = _lane_lo((tq, LANES))
    sk = sk_ref[...]
    sv = sv_ref[...]
    for j in range(4):
        sl = slice(LANES * j, LANES * (j + 1))
        q_lo, q_hi = _split_halves(sq_ref[:, sl])
        o_lo = _softmax_pv([_dot_nt(q_lo, sk)], [sv], sink=sink_ref[j] * LOG2E)
        o_hi = _softmax_pv([_dot_nt(q_hi, sk)], [sv], sink=sink_ref[j + 4] * LOG2E)
        o_ref[:, sl] = jnp.where(lo, o_lo, o_hi).astype(BF16)
    for j in range(4):
        sl = slice(LANES * j, LANES * (j + 1))
        q_lo, q_hi = _split_halves(nq_ref[:, sl])
        k = nk_ref[:, sl]
        v = nv_ref[:, sl]
        o_lo = _softmax_pv([_dot_nt(q_lo, k)], [v])
        o_hi = _softmax_pv([_dot_nt(q_hi, k)], [v])
        o_ref[:, 512 + LANES * j: 512 + LANES * (j + 1)] = jnp.where(lo, o_lo, o_hi).astype(BF16)


def _attn_cd_prompt(sink, sq, sk, sv, nq, nk, nv):
    def spec(w):
        return pl.BlockSpec((SEQ, w), lambda b: (b, 0))
    return pl.pallas_call(
        _attn_cd_prompt_kernel,
        grid=(BATCH,),
        in_specs=[pl.BlockSpec(memory_space=pltpu.SMEM), spec(512), spec(128), spec(128),
                  spec(512), spec(512), spec(512)],
        out_specs=spec(1024),
        out_shape=jax.ShapeDtypeStruct((N_TOK, 1024), BF16),
        compiler_params=_cparams(1),
        name="attn_cd_prompt",
    )(sink, sq, sk, sv, nq, nk, nv)


_SWA_KEYS = TQ + 2 * SWA_WINDOW


def _attn_cd_sample_kernel(init_ref, sink_ref, sq_ref, nq_ref, sk_ref, sv_ref, nk_ref, nv_ref,
                           skc_ref, svc_ref, nkc_ref, nvc_ref, bias_ref, o_ref):
    del init_ref
    qi = pl.program_id(1)
    lo = _lane_lo((TQ, LANES))
    ks = pl.multiple_of(jnp.clip(qi * TQ - SWA_WINDOW, 0, DEC_SEQ - _SWA_KEYS), SWA_WINDOW)
    k_win = sk_ref[pl.ds(ks, _SWA_KEYS), :]
    v_win = sv_ref[pl.ds(ks, _SWA_KEYS), :]
    q_pos = qi * TQ + (lax.broadcasted_iota(jnp.int32, (4 * TQ, _SWA_KEYS), 0) & (TQ - 1))
    k_pos = ks + lax.broadcasted_iota(jnp.int32, (4 * TQ, _SWA_KEYS), 1)
    in_win = jnp.abs(q_pos - k_pos) <= SWA_WINDOW
    skc = skc_ref[...]
    svc = svc_ref[...]
    q_stacks, sinks = _gqa_stacks(sq_ref, sink_ref)

    def window_scores(kvh):
        return [_dot_nt(q_stacks[kvh], skc), jnp.where(in_win, _dot_nt(q_stacks[kvh], k_win), NEG)]

    o_kv = {}

    def window_finish(kvh, scores):
        o_kv[kvh] = _softmax_pv(scores, [svc, v_win], sink=sinks[kvh])
        if kvh == SWA_KV_HEADS - 1:
            for j in range(4):
                rows = slice(TQ * j, TQ * (j + 1))
                o_ref[:, LANES * j: LANES * (j + 1)] = jnp.where(lo, o_kv[0][rows], o_kv[1][rows]).astype(BF16)

    n_rows = DEC_SEQ // GRID_W
    r0 = jnp.clip(qi * NA_TILE_ROWS - NA_WIN_ROWS // 2, 0, n_rows - NA_KEY_ROWS)
    kn = pl.multiple_of(r0 * GRID_W, GRID_W)

    lane_lo = _lane_lo((1, LANES))
    pieces = []
    for ri in range(NA_TILE_ROWS):
        r = qi * NA_TILE_ROWS + ri
        rs = jnp.clip(r - NA_WIN_ROWS // 2, 0, n_rows - NA_WIN_ROWS)
        row = []
        for mm in range(NA_KEY_ROWS // 2):
            kr = r0 + 2 * mm
            ok = [(kr + t >= rs) & (kr + t < rs + NA_WIN_ROWS) for t in range(2)]
            mask = jnp.where(lane_lo, jnp.where(ok[0], 0.0, NEG), jnp.where(ok[1], 0.0, NEG))
            row.append((jnp.clip(kr - r + NA_WIN_ROWS, 0, 2 * NA_WIN_ROWS - 1), mask))
        pieces.append(row)

    def na_bias(hd):
        return jnp.concatenate(
            [jnp.concatenate([bias_ref[hd, d] + mask for d, mask in row], axis=1) for row in pieces], axis=0)

    def na_scores(hd):
        sl = slice(LANES * (hd // 2), LANES * (hd // 2 + 1))
        q_half = _split_halves(nq_ref[:, sl])[hd % 2]
        nk_win = nk_ref[pl.ds(kn, NA_KEY_ROWS * GRID_W), sl]
        return [_dot_nt(q_half, nkc_ref[:, sl]), _dot_nt(q_half, nk_win) + na_bias(hd)]

    pair = {}

    def na_finish(hd, scores):
        j = hd // 2
        sl = slice(LANES * j, LANES * (j + 1))
        nv_win = nv_ref[pl.ds(kn, NA_KEY_ROWS * GRID_W), sl]
        pair[hd % 2] = _softmax_pv(scores, [nvc_ref[:, sl], nv_win])
        if hd % 2 == 1:
            o_ref[:, 512 + LANES * j: 512 + LANES * (j + 1)] = jnp.where(lo, pair[0], pair[1]).astype(BF16)

    _pipelined_units([(window_scores, window_finish, kvh) for kvh in range(SWA_KV_HEADS)]
                     + [(na_scores, na_finish, hd) for hd in range(NA_HEADS)])


def _attn_cd_sample(out_init, sink, sq, sk, sv, nq, nk, nv, skc, svc, nkc, nvc, bias):
    nq_t = DEC_SEQ // TQ
    q_off = NP_TOK // TQ
    b_off = NP_TOK // DEC_SEQ

    def qspec(w):
        return pl.BlockSpec((TQ, w), lambda b, i: (q_off + b * nq_t + i, 0))

    def kspec(w):
        return pl.BlockSpec((DEC_SEQ, w), lambda b, i: (b_off + b, 0))

    def cspec(w):
        return pl.BlockSpec((PAST_LEN, w), lambda b, i: (b, 0))

    bias_spec = pl.BlockSpec((NA_HEADS, 2 * NA_WIN_ROWS, GRID_W, LANES), lambda b, i: (0, 0, 0, 0))
    return pl.pallas_call(
        _attn_cd_sample_kernel,
        grid=(DEC_BATCH, nq_t),
        in_specs=[pl.BlockSpec(memory_space=pl.ANY), pl.BlockSpec(memory_space=pltpu.SMEM),
                  qspec(512), qspec(512), kspec(128), kspec(128), kspec(512), kspec(512),
                  cspec(128), cspec(128), cspec(512), cspec(512), bias_spec],
        out_specs=pl.BlockSpec((TQ, 1024), lambda b, i: (q_off + b * nq_t + i, 0)),
        out_shape=jax.ShapeDtypeStruct((N_TOK, 1024), BF16),
        input_output_aliases={0: 0},
        compiler_params=_cparams(2),
        name="attn_cd_sample",
    )(out_init, sink, sq, nq, sk, sv, nk, nv, skc, svc, nkc, nvc, bias)


def _na_bias(rpb):
    n_dc = 2 * NA_WIN_COLS - 1
    c = np.arange(GRID_W)[:, None]
    kc = np.arange(GRID_W)[None, :]
    qs = np.clip(c - NA_WIN_COLS // 2, 0, GRID_W - NA_WIN_COLS)
    col_ok = (kc >= qs) & (kc < qs + NA_WIN_COLS)
    dc = np.clip(kc - c + NA_WIN_COLS - 1, 0, n_dc - 1)
    onehot = ((dc[None] == np.arange(n_dc)[:, None, None]) & col_ok[None]).astype(np.float32)
    blocks = jnp.einsum('hrd,dck->hrck', rpb.astype(F32) * LOG2E, onehot, precision=lax.Precision.HIGHEST)
    blocks = jnp.where(col_ok[None, None], blocks, NEG)
    none = jnp.full((NA_HEADS, 1, GRID_W, GRID_W), NEG, F32)
    return jnp.concatenate([jnp.concatenate([none, blocks], axis=1),
                            jnp.concatenate([blocks, none], axis=1)], axis=-1)


_HI_MASK = -65536


def _pack_pairs(x):
    w = x.shape[1] // 2
    r = x.astype(BF16).astype(F32)
    lo = lax.bitcast_convert_type(r[:, :w], jnp.int32)
    hi = lax.bitcast_convert_type(r[:, w:], jnp.int32)
    return (hi & _HI_MASK) | lax.shift_right_logical(lo, 16)


def _unpack_pairs(p):
    lo = lax.bitcast_convert_type(lax.shift_left(p, 16), F32)
    hi = lax.bitcast_convert_type(p & _HI_MASK, F32)
    return lo, hi


def _postmix_kernel(*refs, split_x):
    if split_x:
        o_ref, xp_ref, xs_ref = refs[:3]
        refs = refs[3:]
    else:
        o_ref, x_ref = refs[:2]
        refs = refs[2:]
    (mod_ref, g_ref, wout_ref, wr_ref, br_ref, tri_ref,
     x1_ref, h2_ref, route_ref, gate_ref, cnt_ref, run_ref) = refs
    i = pl.program_id(0)

    @pl.when(i == 0)
    def _():
        run_ref[...] = jnp.zeros_like(run_ref)

    m = mod_ref[0]
    g = g_ref[...]
    sub_logits = {}

    def project(n):
        return _dot(o_ref[_sub_rows(n), :], wout_ref[...])

    def finish(n, y):
        r = _sub_rows(n)
        x = jnp.where(i < NPT, xp_ref[r, :], xs_ref[r, :]) if split_x else x_ref[r, :]
        x1 = x + m[2:3] * _rms(y, g[1:2])
        x1_ref[r, :] = x1
        h2 = _rms(x1, g[2:3]) * (1.0 + m[4:5]) + m[3:4]
        h2_ref[r, :] = _pack_pairs(h2)
        sub_logits[n] = _dot(h2.astype(BF16), wr_ref[...]) + br_ref[...]

    _pipelined_units([(project, finish, n) for n in range(N_SUB)])

    logits = jnp.concatenate([sub_logits[n] for n in range(N_SUB)], axis=0)
    lane = lax.broadcasted_iota(jnp.int32, logits.shape, 1).astype(F32)
    cur = jnp.where(lane < N_EXPERTS, logits, -jnp.inf)
    tops, idxs = [], []
    for _ in range(TOP_K):
        mx = jnp.max(cur, axis=-1, keepdims=True)
        ix = jnp.min(jnp.where(cur == mx, lane, float(LANES)), axis=-1, keepdims=True)
        tops.append(mx)
        idxs.append(ix)
        cur = jnp.where(lane == ix, -jnp.inf, cur)
    es = [jnp.exp(t - tops[0]) for t in tops]
    inv = 1.0 / (es[0] + es[1] + es[2] + es[3])
    picked = jnp.zeros_like(logits)
    for k in range(TOP_K):
        picked = jnp.where(lane == idxs[k], 1.0, picked)
    before = _dot(tri_ref[...], picked.astype(BF16)) + run_ref[0:1, :]
    route = jnp.zeros_like(logits)
    gate_out = jnp.zeros_like(logits)
    for k in range(TOP_K):
        rank = jnp.sum(jnp.where(lane == idxs[k], before, 0.0), axis=-1, keepdims=True)
        route = jnp.where(lane == float(k), idxs[k], route)
        route = jnp.where(lane == float(TOP_K + k), rank, route)
        gate_out = jnp.where(lane == float(k), es[k] * inv, gate_out)
    route_ref[...] = route.T[:2 * TOP_K].astype(jnp.int32)
    gate_ref[...] = gate_out
    run_ref[...] = run_ref[...] + jnp.sum(picked, axis=0, keepdims=True)
    cnt_ref[...] = run_ref[...].astype(jnp.int32)


def _postmix(o_cat, xs, mod_l, g_l, w_out, w_r, b_r):
    tri = jnp.asarray(np.tril(np.ones((TM, TM), np.float32), -1), BF16)
    split_x = len(xs) == 2
    x_specs = [_PROMPT_SPEC, _SAMPLE_SPEC] if split_x else [_tok_spec(D_MODEL)]
    return pl.pallas_call(
        functools.partial(_postmix_kernel, split_x=split_x),
        grid=(N_TILES,),
        in_specs=[_tok_spec(1024)] + x_specs + [_MOD_SPEC, _const_spec((4, D_MODEL)),
                  _const_spec((1024, D_MODEL)), _const_spec((D_MODEL, LANES)), _const_spec((1, LANES)),
                  _const_spec((TM, TM))],
        out_specs=[_tok_spec(D_MODEL), _tok_spec(D_MODEL // 2), pl.BlockSpec((2 * TOP_K, TM), lambda i: (0, i)),
                   _tok_spec(LANES), _const_spec((8, LANES))],
        out_shape=[jax.ShapeDtypeStruct((N_TOK, D_MODEL), F32),
                   jax.ShapeDtypeStruct((N_TOK, D_MODEL // 2), jnp.int32),
                   jax.ShapeDtypeStruct((2 * TOP_K, N_TOK), jnp.int32), jax.ShapeDtypeStruct((N_TOK, LANES), F32),
                   jax.ShapeDtypeStruct((8, LANES), jnp.int32)],
        scratch_shapes=[pltpu.VMEM((8, LANES), F32)],
        compiler_params=_cparams(1),
        name="postmix",
    )(o_cat, *xs, mod_l, g_l, w_out, w_r, b_r, tri)


SC_WORKERS = 32
SC_ROWS = 128
ROW_WORDS = D_MODEL // 2


def _sc_worker_id():
    return lax.axis_index("s") * 2 + lax.axis_index("c")


def _sc_dispatch(src, idx):
    n_chunks = N_TOK // (SC_WORKERS * SC_ROWS)
    mesh = plsc.VectorSubcoreMesh(core_axis_name="c", subcore_axis_name="s")

    @functools.partial(
        pl.kernel, mesh=mesh,
        out_type=jax.ShapeDtypeStruct((MOE_ROWS, ROW_WORDS), jnp.int32),
        scratch_types=[pltpu.VMEM((n_chunks * TOP_K, SC_ROWS), jnp.int32),
                       pltpu.VMEM((SC_ROWS, ROW_WORDS), jnp.int32), pltpu.SemaphoreType.DMA])
    def k(src_hbm, idx_hbm, out_hbm, idx_v, rows_v, sem):
        wid = _sc_worker_id()
        pltpu.sync_copy(idx_hbm.at[wid], idx_v)

        @pl.loop(0, n_chunks)
        def _(g):
            pltpu.sync_copy(src_hbm.at[pl.ds((wid * n_chunks + g) * SC_ROWS, SC_ROWS)], rows_v)
            copies = [pltpu.async_copy(rows_v, out_hbm.at[idx_v.at[g * TOP_K + kk]], sem)
                      for kk in range(TOP_K)]
            for cp in copies:
                cp.wait()

    return k(src, idx)


def _sc_collect(table, idx):
    n_chunks = idx.shape[1]
    mesh = plsc.VectorSubcoreMesh(core_axis_name="c", subcore_axis_name="s")

    @functools.partial(
        pl.kernel, mesh=mesh,
        out_type=jax.ShapeDtypeStruct((SC_WORKERS * n_chunks * SC_ROWS, ROW_WORDS), jnp.int32),
        scratch_types=[pltpu.VMEM((n_chunks, SC_ROWS), jnp.int32),
                       pltpu.VMEM((SC_ROWS, ROW_WORDS), jnp.int32), pltpu.SemaphoreType.DMA])
    def k(table_hbm, idx_hbm, out_hbm, idx_v, rows_v, sem):
        wid = _sc_worker_id()
        pltpu.sync_copy(idx_hbm.at[wid], idx_v)

        @pl.loop(0, n_chunks)
        def _(g):
            pltpu.async_copy(table_hbm.at[idx_v.at[g]], rows_v, sem).wait()
            pltpu.sync_copy(rows_v, out_hbm.at[pl.ds((wid * n_chunks + g) * SC_ROWS, SC_ROWS)])

    return k(table, idx)


def _expert_rows(words, n_valid, wgu_b, wd_b, bgu, bd):
    live = lax.broadcasted_iota(jnp.int32, words.shape, 0) < n_valid
    lo, hi = _unpack_pairs(jnp.where(live, words, 0))
    x = jnp.concatenate([lo, hi], axis=1).astype(BF16)
    gu = _dot(x, wgu_b[...]) + bgu
    g = jnp.minimum(gu[:, :D_EXPERT], SWIGLU_LIMIT)
    u = jnp.clip(gu[:, D_EXPERT:], -SWIGLU_LIMIT, SWIGLU_LIMIT)
    a = g * jax.nn.sigmoid(SWIGLU_ALPHA * g) * (u + 1.0)
    return _pack_pairs(_dot(a.astype(BF16), wd_b[...]) + bd)


def _moe_kernel(blk_e_ref, blk_first_ref, blk_rows_ref, blk_slot_ref, blk_next_ref,
                x_ref, wgu_hbm, bgu_ref, wd_hbm, bd_ref, y_ref,
                wgu_f, wd_f, wgu_b, wd_b, sem, *, layer):
    i = pl.program_id(0)
    n_valid = blk_rows_ref[i]
    quantum = MOE_TM // MOE_TAIL_PARTS

    def weight_copies(e, slot):
        return (pltpu.make_async_copy(wgu_hbm.at[layer, e], wgu_f.at[slot], sem.at[0, slot]),
                pltpu.make_async_copy(wd_hbm.at[layer, e], wd_f.at[slot], sem.at[1, slot]))

    @pl.when(i == 0)
    def _():
        for cp in weight_copies(blk_e_ref[0], blk_slot_ref[0]):
            cp.start()

    @pl.when(blk_first_ref[i] == 1)
    def _():
        slot = blk_slot_ref[i]
        for cp in weight_copies(blk_e_ref[i], slot):
            cp.wait()
        nxt = blk_next_ref[i]

        @pl.when(nxt >= 0)
        def _():
            for cp in weight_copies(nxt, 1 - slot):
                cp.start()

        wgu_b[...] = wgu_f[slot].astype(BF16)
        wd_b[...] = wd_f[slot].astype(BF16)

    for parts in range(1, MOE_TAIL_PARTS + 1):
        rows = parts * quantum

        @pl.when((n_valid > rows - quantum) & (n_valid <= rows))
        def _(rows=rows):
            y_ref[:rows] = _expert_rows(x_ref[:rows], n_valid, wgu_b, wd_b, bgu_ref[0, 0], bd_ref[0, 0])
            if rows < MOE_TM:
                y_ref[rows:] = jnp.zeros((MOE_TM - rows, ROW_WORDS), jnp.int32)

    @pl.when(n_valid == 0)
    def _():
        y_ref[...] = jnp.zeros_like(y_ref)


def _moe(layer, blk_meta, xs, w_gu, b_gu, w_down, b_down):
    def row_map(i, *_):
        return (i, 0)

    def bias_map(i, e, *_):
        return (layer, e[i], 0, 0)

    grid_spec = pltpu.PrefetchScalarGridSpec(
        num_scalar_prefetch=5,
        grid=(MOE_BLOCKS,),
        in_specs=[
            pl.BlockSpec((MOE_TM, ROW_WORDS), row_map),
            pl.BlockSpec(memory_space=pl.ANY),
            pl.BlockSpec((1, 1, 1, 2 * D_EXPERT), bias_map),
            pl.BlockSpec(memory_space=pl.ANY),
            pl.BlockSpec((1, 1, 1, D_MODEL), bias_map),
        ],
        out_specs=pl.BlockSpec((MOE_TM, ROW_WORDS), row_map),
        scratch_shapes=[pltpu.VMEM((2, D_MODEL, 2 * D_EXPERT), F32), pltpu.VMEM((2, D_EXPERT, D_MODEL), F32),
                        pltpu.VMEM((D_MODEL, 2 * D_EXPERT), BF16), pltpu.VMEM((D_EXPERT, D_MODEL), BF16),
                        pltpu.SemaphoreType.DMA((2, 2))],
    )
    return pl.pallas_call(
        functools.partial(_moe_kernel, layer=layer),
        grid_spec=grid_spec,
        out_shape=jax.ShapeDtypeStruct((MOE_ROWS, ROW_WORDS), jnp.int32),
        compiler_params=_cparams(1),
        name="moe_experts",
    )(*blk_meta, xs, w_gu, b_gu.reshape(DEPTH, N_EXPERTS, 1, 2 * D_EXPERT),
      w_down, b_down.reshape(DEPTH, N_EXPERTS, 1, D_MODEL))


def _route(route, counts):
    experts = jnp.arange(N_EXPERTS, dtype=jnp.int32)
    padded = (counts + MOE_TM - 1) // MOE_TM * MOE_TM
    pad_end = jnp.cumsum(padded)
    pad_start = pad_end - padded
    e = route[0:TOP_K]
    onehot = e[:, :, None] == experts[None, None, :]
    dest = jnp.sum(jnp.where(onehot, pad_start[None, None, :], 0), axis=-1) + route[TOP_K:2 * TOP_K]
    blk_row0 = jnp.arange(MOE_BLOCKS, dtype=jnp.int32) * MOE_TM
    blk_e = jnp.minimum(jnp.sum((pad_end[None, :] <= blk_row0[:, None]).astype(jnp.int32), axis=1),
                        N_EXPERTS - 1)
    on = blk_row0 < pad_end[-1]
    n_on = jnp.sum(on.astype(jnp.int32))
    blk_onehot = blk_e[:, None] == experts[None, :]
    row_end = jnp.sum(jnp.where(blk_onehot, (pad_start + counts)[None, :], 0), axis=1)
    blk_rows = jnp.where(on, jnp.clip(row_end - blk_row0, 0, MOE_TM), 0).astype(jnp.int32)
    last_e = jnp.sum(jnp.where(jnp.arange(MOE_BLOCKS) == n_on - 1, blk_e, 0))
    blk_e = jnp.where(on, blk_e, last_e).astype(jnp.int32)
    prev = jnp.concatenate([jnp.full((1,), -1, jnp.int32), blk_e[:-1]])
    blk_first = (blk_e != prev).astype(jnp.int32)
    blk_slot = ((jnp.cumsum(blk_first) - 1) % 2).astype(jnp.int32)
    later_used = (experts[None, :] > experts[:, None]) & (counts[None, :] > 0)
    next_used = jnp.min(jnp.where(later_used, experts[None, :], N_EXPERTS), axis=1)
    next_used = jnp.where(next_used == N_EXPERTS, -1, next_used)
    blk_next = jnp.sum(jnp.where(blk_e[:, None] == experts[None, :], next_used[None, :], 0), axis=1)
    return dest, (blk_e, blk_first, blk_rows, blk_slot, blk_next.astype(jnp.int32))


def _combine_kernel(y_ref, gate_ref, x1_ref, mod_ref, g_ref, *o_refs):
    i = pl.program_id(0)
    m = mod_ref[0]
    gate = gate_ref[...]
    acc_lo = None
    for k in range(TOP_K):
        lo, hi = _unpack_pairs(y_ref[k])
        gk = gate[:, k:k + 1]
        acc_lo = gk * lo if acc_lo is None else acc_lo + gk * lo
        acc_hi = gk * hi if k == 0 else acc_hi + gk * hi
    acc = jnp.concatenate([acc_lo, acc_hi], axis=1)
    out = x1_ref[...] + m[5:6] * _rms(acc, g_ref[...][3:4])
    if len(o_refs) == 1:
        o_refs[0][...] = out
    else:
        @pl.when(i < NPT)
        def _():
            o_refs[0][...] = out

        @pl.when(i >= NPT)
        def _():
            o_refs[1][...] = out


def _combine(yg, gates, x1, mod_l, g_l, split_out):
    if split_out:
        out_specs = [_PROMPT_SPEC, _SAMPLE_SPEC]
        out_shape = [jax.ShapeDtypeStruct((NP_TOK, D_MODEL), F32), jax.ShapeDtypeStruct((NS_TOK, D_MODEL), F32)]
    else:
        out_specs = [_tok_spec(D_MODEL)]
        out_shape = [jax.ShapeDtypeStruct((N_TOK, D_MODEL), F32)]
    return pl.pallas_call(
        _combine_kernel,
        grid=(N_TILES,),
        in_specs=[pl.BlockSpec((TOP_K, TM, ROW_WORDS), lambda i: (0, i, 0)), _tok_spec(LANES),
                  _tok_spec(D_MODEL), _MOD_SPEC, _const_spec((4, D_MODEL))],
        out_specs=out_specs,
        out_shape=out_shape,
        compiler_params=_cparams(1),
        name="combine",
    )(yg, gates, x1, mod_l, g_l)


def _ffn(layer, o_cat, xs, mod_l, g_l, w_out, w_router, b_router, w_gu, b_gu, w_down, b_down, split_out):
    w_r = jnp.pad(w_router, ((0, 0), (0, LANES - N_EXPERTS))).astype(BF16)
    b_r = jnp.pad(b_router, (0, LANES - N_EXPERTS)).reshape(1, LANES)
    x1, h2p, route, gate_slab, counts = _postmix(o_cat, xs, mod_l, g_l, w_out.astype(BF16), w_r, b_r)
    dest, blk_meta = _route(route, counts[0, :N_EXPERTS])
    n_chunks = N_TOK // (SC_WORKERS * SC_ROWS)
    idx_d = dest.reshape(TOP_K, SC_WORKERS, n_chunks, SC_ROWS).transpose(1, 2, 0, 3).reshape(
        SC_WORKERS, n_chunks * TOP_K, SC_ROWS)
    rows = _sc_dispatch(h2p, idx_d)
    ys = _moe(layer, blk_meta, rows, w_gu, b_gu, w_down, b_down)
    idx_c = dest.reshape(SC_WORKERS, TOP_K * n_chunks, SC_ROWS)
    yg = _sc_collect(ys, idx_c).reshape(TOP_K, N_TOK, ROW_WORDS)
    return _combine(yg, gate_slab, x1, mod_l, g_l, split_out)


def _pad_heads(w, n_heads, width, keep):
    k = w.shape[0]
    w = w.reshape(k, n_heads, width)[:, :, :keep]
    return jnp.pad(w, ((0, 0), (0, 0), (0, LANES - keep))).reshape(k, n_heads * LANES)


def _pe_slab(x):
    return jnp.pad(x, [(0, 0)] * (x.ndim - 1) + [(MLA_NOPE, LANES - MLA_NOPE - MLA_ROPE)])


_SWA_ORDER = np.array([0, 4, 1, 5, 2, 6, 3, 7])


def kernel(x_prompt, x_sample, cache_mla_ckv, cache_mla_krope, cache_diff_k, cache_diff_v, cache_swa_k, cache_swa_v, cache_na_k, cache_na_v, c, c_ctx, w_mod, b_mod, norm_g, w_in0, mla_q_norm, w_uq, mla_kv_norm, w_ukv, diff_lambda, diff_norm, w_out0, w_in1, swa_sink, na_rpb, w_out1, w_router, b_router, w_gu, b_gu, w_down, b_down):
    xs = (x_prompt.reshape(NP_TOK, D_MODEL), x_sample.reshape(NS_TOK, D_MODEL))
    cond = jnp.concatenate([c_ctx[None, :], c, jnp.zeros((16 - 1 - DEC_BATCH, D_MODEL), F32)], axis=0)
    mod = _modulation(cond, w_mod, b_mod).reshape(DEPTH, 16, 6, D_MODEL)
    t64, t32 = _rope_tables()
    states = {}
    for l in range(DEPTH):
        i = l // 2
        g_l = norm_g[l]
        mod_l = mod[l]
        if l % 2 == 0:
            lam_init = 0.8 - 0.6 * math.exp(-0.3 * l)
            wi = w_in0[i]
            w_in_p = jnp.concatenate(
                [wi[:, 0:640], wi[:, 672:2208], _pe_slab(wi[:, 640:672])], axis=1).astype(BF16)
            w_uq_p = _pad_heads(w_uq[i], MLA_HEADS, MLA_NOPE + MLA_ROPE, MLA_NOPE + MLA_ROPE).astype(BF16)
            w_k_p = _pad_heads(w_ukv[i], MLA_HEADS, MLA_NOPE + MLA_V, MLA_NOPE).astype(BF16)
            w_v = w_ukv[i].reshape(MLA_KV_RANK, MLA_HEADS, MLA_NOPE + MLA_V)[:, :, MLA_NOPE:].reshape(
                MLA_KV_RANK, MLA_HEADS * MLA_V).astype(BF16)
            (q, k, v, dq, dk, dv, ckv_st, kpe_st, dk_st, dv_st) = _premix0(
                *xs, mod_l, g_l[0:1], w_in_p, mla_q_norm[i][None, :], w_uq_p, mla_kv_norm[i][None, :],
                w_k_p, w_v, t32, t64)
            states['mla_ckv'] = ckv_st.reshape(BATCH, 1, SEQ, MLA_KV_RANK)
            states['mla_krope'] = kpe_st[:, MLA_NOPE:MLA_NOPE + MLA_ROPE].reshape(BATCH, 1, SEQ, MLA_ROPE)
            states['diff_k'] = dk_st.reshape(BATCH, 1, SEQ, DIFF_HEADS, 2 * DIFF_DH)
            states['diff_v'] = dv_st.reshape(BATCH, 1, SEQ, DIFF_HEADS, 2 * DIFF_DH)
            kc, vc = _mla_cache(cache_mla_ckv[:, i].reshape(DEC_BATCH * PAST_LEN, MLA_KV_RANK),
                                _pe_slab(cache_mla_krope[:, i].reshape(DEC_BATCH * PAST_LEN, MLA_ROPE)),
                                w_k_p, w_v)
            dkc = cache_diff_k[:, i].reshape(DEC_BATCH * PAST_LEN, 512).astype(BF16)
            dvc = cache_diff_v[:, i].reshape(DEC_BATCH * PAST_LEN, 512).astype(BF16)
            lam = diff_lambda[i]
            sub_g = diff_norm[i][None, :]
            o_p = _attn_ab(q, dq, (k, v, dk, dv), None, lam, sub_g, lam_init,
                           n_batch=BATCH, t_len=SEQ, tq=SEQ, tok_off=0)
            o_cat = _attn_ab(q, dq, (k, v, dk, dv), (kc, vc, dkc, dvc), lam, sub_g, lam_init,
                             n_batch=DEC_BATCH, t_len=DEC_SEQ, tq=TQ_AB, tok_off=NP_TOK, out_init=o_p)
            w_out = w_out0[i]
        else:
            wi = w_in1[i]
            sq_cols = (_SWA_ORDER[:, None] * HEAD_DIM + np.arange(HEAD_DIM)[None, :]).reshape(-1)
            w_in_p = jnp.concatenate([wi[:, sq_cols], wi[:, 512:]], axis=1).astype(BF16)
            (sq, sk, sv, nq, nk, nv, sk_st, sv_st, nk_st, nv_st) = _premix1(xs[0], mod_l, g_l[0:1], w_in_p, t64)
            states['swa_k'] = sk_st.reshape(BATCH, 1, SEQ, SWA_KV_HEADS, HEAD_DIM)
            states['swa_v'] = sv_st.reshape(BATCH, 1, SEQ, SWA_KV_HEADS, HEAD_DIM)
            states['na_k'] = nk_st.reshape(BATCH, 1, SEQ, NA_HEADS, HEAD_DIM)
            states['na_v'] = nv_st.reshape(BATCH, 1, SEQ, NA_HEADS, HEAD_DIM)
            skc = cache_swa_k[:, i].reshape(DEC_BATCH * PAST_LEN, 128).astype(BF16)
            svc = cache_swa_v[:, i].reshape(DEC_BATCH * PAST_LEN, 128).astype(BF16)
            nkc = cache_na_k[:, i].reshape(DEC_BATCH * PAST_LEN, 512).astype(BF16)
            nvc = cache_na_v[:, i].reshape(DEC_BATCH * PAST_LEN, 512).astype(BF16)
            sink = swa_sink[i]
            o_p = _attn_cd_prompt(sink, sq, sk, sv, nq, nk, nv)
            o_cat = _attn_cd_sample(o_p, sink, sq, sk, sv, nq, nk, nv, skc, svc, nkc, nvc, _na_bias(na_rpb[i]))
            wo = w_out1[i]
            w_out = jnp.concatenate([wo[sq_cols], wo[512:]], axis=0)
        xs = _ffn(l, o_cat, xs, mod_l, g_l, w_out, w_router[l], b_router[l], w_gu, b_gu, w_down, b_down,
                  split_out=(l == DEPTH - 1))
    return (xs[0].reshape(BATCH, SEQ, D_MODEL), xs[1].reshape(DEC_BATCH, DEC_SEQ, D_MODEL),
            states['mla_ckv'], states['mla_krope'], states['diff_k'], states['diff_v'],
            states['swa_k'], states['swa_v'], states['na_k'], states['na_v'])
```

```python
import functools
import math

import numpy as np
import jax
import jax.numpy as jnp
from jax import lax
from jax.experimental import pallas as pl
from jax.experimental.pallas import tpu as pltpu
from jax.experimental.pallas import tpu_sc as plsc

F32 = jnp.float32
BF16 = jnp.bfloat16

D_MODEL = 1024
BATCH = 16
SEQ = 256
DEPTH = 2
DEC_BATCH = 8
DEC_SEQ = 2048
PAST_LEN = 256
GRID_W = 64
HEAD_DIM = 64
ROPE_THETA = 10000.0
EPS = 1e-6
NEG = -1e30

MLA_HEADS = 8
MLA_Q_RANK = 384
MLA_KV_RANK = 256
MLA_NOPE = 64
MLA_ROPE = 32
MLA_V = 64
DIFF_HEADS = 4
DIFF_DH = 64
SWA_HEADS = 8
SWA_KV_HEADS = 2
SWA_WINDOW = 128
NA_HEADS = 8
NA_WIN_ROWS = 8
NA_WIN_COLS = 16
N_EXPERTS = 32
TOP_K = 4
D_EXPERT = 1024
SWIGLU_LIMIT = 7.0
SWIGLU_ALPHA = 1.702

LANES = 128
NP_TOK = BATCH * SEQ
NS_TOK = DEC_BATCH * DEC_SEQ
N_TOK = NP_TOK + NS_TOK
TM = 512
NPT = NP_TOK // TM
TILES_PER_SAMPLE = DEC_SEQ // TM
N_TILES = N_TOK // TM
N_SUB = 2
SUB_TM = TM // N_SUB
TQ = 256
TQ_AB = 256
MOE_TM = 512
MOE_TAIL_PARTS = 4
MOE_ROWS = ((N_TOK * TOP_K + N_EXPERTS * (MOE_TM - 1)) // MOE_TM + 1) * MOE_TM
MOE_BLOCKS = MOE_ROWS // MOE_TM
NA_TILE_ROWS = TQ // GRID_W
NA_KEY_ROWS = 12
VMEM_LIMIT = 56 * 1024 * 1024


def _cparams(n_axes, vmem=VMEM_LIMIT):
    return pltpu.CompilerParams(dimension_semantics=("arbitrary",) * n_axes,
                                vmem_limit_bytes=vmem)


def _rms(x, g):
    return x * lax.rsqrt(jnp.mean(x * x, axis=-1, keepdims=True) + EPS) * g


def _dot(a, b):
    return jnp.dot(a, b, preferred_element_type=F32)


def _dot_nt(a, b):
    return lax.dot_general(a, b, (((1,), (1,)), ((), ())), preferred_element_type=F32)


def _rope(x, cos, sin_a, sin_b, half):
    return (x * cos + pltpu.roll(x, LANES - half, 1) * sin_a + pltpu.roll(x, half, 1) * sin_b)


def _pipelined_units(units):
    pending = units[0][0](units[0][2])
    for n, (_, finish, arg) in enumerate(units):
        following = units[n + 1][0](units[n + 1][2]) if n + 1 < len(units) else None
        finish(arg, pending)
        pending = following


def _sub_rows(n):
    return slice(SUB_TM * n, SUB_TM * (n + 1))


def _store_head_rows(st_ref, n, hd, n_heads, x):
    st_ref[pl.ds(SUB_TM * n * n_heads + hd, SUB_TM, stride=n_heads), :] = x


def _mod_row(i):
    return jnp.where(i < NPT, 0, 1 + (i - NPT) // TILES_PER_SAMPLE)


def _rope_blk(i):
    return jnp.where(i < NPT, TILES_PER_SAMPLE, (i - NPT) % TILES_PER_SAMPLE)


def _mod_kernel(c_ref, w_ref, b_ref, o_ref):
    c = c_ref[...]
    s = (c * jax.nn.sigmoid(c)).astype(BF16)
    o_ref[0] = _dot(s, w_ref[0].astype(BF16)) + b_ref[0]


def _modulation(cond, w_mod, b_mod):
    nb = 1024
    return pl.pallas_call(
        _mod_kernel,
        grid=(DEPTH, 6 * D_MODEL // nb),
        in_specs=[
            pl.BlockSpec((16, D_MODEL), lambda l, n: (0, 0)),
            pl.BlockSpec((1, D_MODEL, nb), lambda l, n: (l, 0, n)),
            pl.BlockSpec((1, 1, nb), lambda l, n: (l, 0, n)),
        ],
        out_specs=pl.BlockSpec((1, 16, nb), lambda l, n: (l, 0, n)),
        out_shape=jax.ShapeDtypeStruct((DEPTH, 16, 6 * D_MODEL), F32),
        compiler_params=_cparams(2),
        name="modulation",
    )(cond, w_mod, b_mod.reshape(DEPTH, 1, 6 * D_MODEL))


def _rope_tables():
    t = jnp.arange(DEC_SEQ)
    rows = (t // GRID_W).astype(F32)
    cols = (t % GRID_W).astype(F32)

    def angles(r):
        n = r // 4
        inv = ROPE_THETA ** (-jnp.arange(n, dtype=F32) / n)
        return jnp.concatenate([rows[:, None] * inv[None], cols[:, None] * inv[None]], axis=-1)

    def finish(cos, sa, sb):
        ident = (jnp.ones((TM, LANES), F32), jnp.zeros((TM, LANES), F32), jnp.zeros((TM, LANES), F32))
        return tuple(jnp.concatenate([a, b], axis=0) for a, b in zip((cos, sa, sb), ident))

    a64 = angles(64)
    c, s, z = jnp.cos(a64), jnp.sin(a64), jnp.zeros_like(a64)
    t64 = finish(jnp.concatenate([c, c, c, c], -1), jnp.concatenate([-s, z, -s, z], -1),
                 jnp.concatenate([z, s, z, s], -1))
    a32 = angles(32)
    c, s, z = jnp.cos(a32), jnp.sin(a32), jnp.zeros_like(a32)
    one64 = jnp.ones((DEC_SEQ, 64), F32)
    z64 = jnp.zeros((DEC_SEQ, 64), F32)
    z32 = jnp.zeros((DEC_SEQ, 32), F32)
    t32 = finish(jnp.concatenate([one64, c, c, z32], -1), jnp.concatenate([z64, -s, z, z32], -1),
                 jnp.concatenate([z64, z, s, z32], -1))
    return t64, t32


LOG2E = math.log2(math.e)
_MLA_SCALE = (MLA_NOPE + MLA_ROPE) ** -0.5 * LOG2E
_QSCALE = HEAD_DIM ** -0.5 * LOG2E


def _premix0_kernel(xp_ref, xs_ref, mod_ref, g_ref, win_ref, qn_ref, wuq_ref, kvn_ref, wk_ref, wv_ref,
                    c32_ref, sa32_ref, sb32_ref, c64_ref, sa64_ref, sb64_ref,
                    q_ref, k_ref, v_ref, dq_ref, dk_ref, dv_ref,
                    ckv_st, kpe_st, dk_st, dv_st):
    i = pl.program_id(0)
    m = mod_ref[0]
    g = g_ref[...]

    def project(n):
        r = _sub_rows(n)
        x = jnp.where(i < NPT, xp_ref[r, :], xs_ref[r, :])
        h = _rms(x, g) * (1.0 + m[1:2]) + m[0:1]
        return _dot(h.astype(BF16), win_ref[...])

    states = {}

    def finish(n, proj):
        r = _sub_rows(n)
        q_a = proj[:, 0:384]
        kv_a = proj[:, 384:640]
        dq = proj[:, 640:1152]
        dk = proj[:, 1152:1664]
        dv = proj[:, 1664:2176]
        pe = proj[:, 2176:2304]
        q = _dot(_rms(q_a, qn_ref[...]).astype(BF16), wuq_ref[...])
        ckv = _rms(kv_a, kvn_ref[...])
        ckv_b = ckv.astype(BF16)
        kn = _dot(ckv_b, wk_ref[...])
        v_ref[r, :] = _dot(ckv_b, wv_ref[...]).astype(BF16)
        c32, sa32, sb32 = c32_ref[r, :], sa32_ref[r, :], sb32_ref[r, :]
        c64, sa64, sb64 = c64_ref[r, :], sa64_ref[r, :], sb64_ref[r, :]
        pe_r = _rope(pe, c32, sa32, sb32, MLA_ROPE // 2)
        for hd in range(MLA_HEADS):
            sl = slice(LANES * hd, LANES * (hd + 1))
            q_ref[r, sl] = (_rope(q[:, sl], c32, sa32, sb32, MLA_ROPE // 2) * _MLA_SCALE).astype(BF16)
            k_ref[r, sl] = (kn[:, sl] + pe_r).astype(BF16)
        for hd in range(DIFF_HEADS):
            sl = slice(LANES * hd, LANES * (hd + 1))
            dq_ref[r, sl] = (_rope(dq[:, sl], c64, sa64, sb64, DIFF_DH // 2) * _QSCALE).astype(BF16)
            dk_ref[r, sl] = _rope(dk[:, sl], c64, sa64, sb64, DIFF_DH // 2).astype(BF16)
        dv_ref[r, :] = dv.astype(BF16)
        states[n] = (ckv, pe, dk, dv)

    _pipelined_units([(project, finish, n) for n in range(N_SUB)])

    @pl.when(i < NPT)
    def _():
        for n in range(N_SUB):
            r = _sub_rows(n)
            ckv, pe, dk, dv = states[n]
            ckv_st[r, :] = ckv
            kpe_st[r, :] = pe
            for hd in range(DIFF_HEADS):
                _store_head_rows(dk_st, n, hd, DIFF_HEADS, dk[:, LANES * hd: LANES * (hd + 1)])
                _store_head_rows(dv_st, n, hd, DIFF_HEADS, dv[:, LANES * hd: LANES * (hd + 1)])


def _premix1_kernel(x_ref, mod_ref, g_ref, win_ref, c64_ref, sa64_ref, sb64_ref,
                    sq_ref, sk_ref, sv_ref, nq_ref, nk_ref, nv_ref,
                    sk_st, sv_st, nk_st, nv_st):
    i = pl.program_id(0)
    m = mod_ref[0]
    g = g_ref[...]

    def project(n):
        h = _rms(x_ref[_sub_rows(n), :], g) * (1.0 + m[1:2]) + m[0:1]
        return _dot(h.astype(BF16), win_ref[...])

    states = {}

    def finish(n, proj):
        r = _sub_rows(n)
        sq = proj[:, 0:512]
        sk = proj[:, 512:640]
        sv = proj[:, 640:768]
        nq = proj[:, 768:1280]
        nk = proj[:, 1280:1792]
        nv = proj[:, 1792:2304]
        c64, sa64, sb64 = c64_ref[r, :], sa64_ref[r, :], sb64_ref[r, :]
        for hd in range(4):
            sl = slice(LANES * hd, LANES * (hd + 1))
            sq_ref[r, sl] = (_rope(sq[:, sl], c64, sa64, sb64, HEAD_DIM // 2) * _QSCALE).astype(BF16)
        sk_ref[r, :] = _rope(sk, c64, sa64, sb64, HEAD_DIM // 2).astype(BF16)
        sv_ref[r, :] = sv.astype(BF16)
        nq_ref[r, :] = (nq * _QSCALE).astype(BF16)
        nk_ref[r, :] = nk.astype(BF16)
        nv_ref[r, :] = nv.astype(BF16)
        states[n] = (sk, sv, nk, nv)

    _pipelined_units([(project, finish, n) for n in range(N_SUB)])

    @pl.when(i < NPT)
    def _():
        for n in range(N_SUB):
            r = _sub_rows(n)
            sk, sv, nk, nv = states[n]
            sk_st[r, :] = sk
            sv_st[r, :] = sv
            for hd in range(NA_HEADS):
                _store_head_rows(nk_st, n, hd, NA_HEADS, nk[:, HEAD_DIM * hd: HEAD_DIM * (hd + 1)])
                _store_head_rows(nv_st, n, hd, NA_HEADS, nv[:, HEAD_DIM * hd: HEAD_DIM * (hd + 1)])


def _tok_spec(width):
    return pl.BlockSpec((TM, width), lambda i: (i, 0))


_PROMPT_SPEC = pl.BlockSpec((TM, D_MODEL), lambda i: (jnp.minimum(i, NPT - 1), 0))
_SAMPLE_SPEC = pl.BlockSpec((TM, D_MODEL), lambda i: (jnp.maximum(i - NPT, 0), 0))


def _state_spec(width, rows_per_token=1):
    return pl.BlockSpec((TM * rows_per_token, width), lambda i: (jnp.minimum(i, NPT - 1), 0))


def _const_spec(shape):
    return pl.BlockSpec(shape, lambda i: (0,) * len(shape))


_MOD_SPEC = pl.BlockSpec((1, 6, D_MODEL), lambda i: (_mod_row(i), 0, 0))
_ROPE_SPEC = pl.BlockSpec((TM, LANES), lambda i: (_rope_blk(i), 0))


def _premix0(xp, xs, mod_l, g0, w_in_p, q_norm, w_uq_p, kv_norm, w_k_p, w_v, t32, t64):
    outs = [(N_TOK, 1024, BF16), (N_TOK, 1024, BF16), (N_TOK, 512, BF16), (N_TOK, 512, BF16),
            (N_TOK, 512, BF16), (N_TOK, 512, BF16),
            (NP_TOK, 256, F32), (NP_TOK, 128, F32),
            (NP_TOK * DIFF_HEADS, 2 * DIFF_DH, F32), (NP_TOK * DIFF_HEADS, 2 * DIFF_DH, F32)]
    return pl.pallas_call(
        _premix0_kernel,
        grid=(N_TILES,),
        in_specs=[_PROMPT_SPEC, _SAMPLE_SPEC, _MOD_SPEC, _const_spec((1, D_MODEL)),
                  _const_spec((D_MODEL, 2304)), _const_spec((1, MLA_Q_RANK)),
                  _const_spec((MLA_Q_RANK, 1024)), _const_spec((1, MLA_KV_RANK)),
                  _const_spec((MLA_KV_RANK, 1024)), _const_spec((MLA_KV_RANK, 512))]
                 + [_ROPE_SPEC] * 6,
        out_specs=([_tok_spec(w) for (_, w, _) in outs[:6]]
                   + [_state_spec(w, n // NP_TOK) for (n, w, _) in outs[6:]]),
        out_shape=[jax.ShapeDtypeStruct((n, w), dt) for (n, w, dt) in outs],
        compiler_params=_cparams(1),
        name="premix_ab",
    )(xp, xs, mod_l, g0, w_in_p, q_norm, w_uq_p, kv_norm, w_k_p, w_v, *t32, *t64)


def _premix1(x, mod_l, g0, w_in_p, t64):
    outs = [(N_TOK, 512, BF16), (N_TOK, 128, BF16), (N_TOK, 128, BF16), (N_TOK, 512, BF16),
            (N_TOK, 512, BF16), (N_TOK, 512, BF16),
            (NP_TOK, 128, F32), (NP_TOK, 128, F32),
            (NP_TOK * NA_HEADS, HEAD_DIM, F32), (NP_TOK * NA_HEADS, HEAD_DIM, F32)]
    return pl.pallas_call(
        _premix1_kernel,
        grid=(N_TILES,),
        in_specs=[_tok_spec(D_MODEL), _MOD_SPEC, _const_spec((1, D_MODEL)),
                  _const_spec((D_MODEL, 2304))] + [_ROPE_SPEC] * 3,
        out_specs=([_tok_spec(w) for (_, w, _) in outs[:6]]
                   + [_state_spec(w, n // NP_TOK) for (n, w, _) in outs[6:]]),
        out_shape=[jax.ShapeDtypeStruct((n, w), dt) for (n, w, dt) in outs],
        compiler_params=_cparams(1),
        name="premix_cd",
    )(x, mod_l, g0, w_in_p, *t64)


def _mla_cache_kernel(ckv_ref, pe_ref, wk_ref, wv_ref, k_ref, v_ref):
    c = ckv_ref[...].astype(BF16)
    kn = _dot(c, wk_ref[...])
    v_ref[...] = _dot(c, wv_ref[...]).astype(BF16)
    pe = pe_ref[...]
    for hd in range(MLA_HEADS):
        sl = slice(LANES * hd, LANES * (hd + 1))
        k_ref[:, sl] = (kn[:, sl] + pe).astype(BF16)


def _mla_cache(ckv, pe_slab, w_k_p, w_v):
    n = ckv.shape[0]
    tm = 512
    return pl.pallas_call(
        _mla_cache_kernel,
        grid=(n // tm,),
        in_specs=[pl.BlockSpec((tm, MLA_KV_RANK), lambda i: (i, 0)),
                  pl.BlockSpec((tm, LANES), lambda i: (i, 0)),
                  _const_spec((MLA_KV_RANK, 1024)), _const_spec((MLA_KV_RANK, 512))],
        out_specs=[pl.BlockSpec((tm, 1024), lambda i: (i, 0)), pl.BlockSpec((tm, 512), lambda i: (i, 0))],
        out_shape=[jax.ShapeDtypeStruct((n, 1024), BF16), jax.ShapeDtypeStruct((n, 512), BF16)],
        compiler_params=_cparams(1),
        name="mla_cache",
    )(ckv, pe_slab, w_k_p, w_v)


def _softmax_pv(scores, values, sink=None):
    m = jnp.max(scores[0], axis=-1, keepdims=True)
    for s in scores[1:]:
        m = jnp.maximum(m, jnp.max(s, axis=-1, keepdims=True))
    if sink is not None:
        m = jnp.maximum(m, sink)
    l = None
    o = None
    for s, v in zip(scores, values):
        p = jnp.exp2(s - m)
        ls = jnp.sum(p, axis=-1, keepdims=True)
        os_ = _dot(p.astype(BF16), v)
        l = ls if l is None else l + ls
        o = os_ if o is None else o + os_
    if sink is not None:
        l = l + jnp.exp2(sink - m)
    return o * (1.0 / l)


def _lane_lo(shape):
    return lax.broadcasted_iota(jnp.int32, shape, 1) < (LANES // 2)


def _split_halves(qb):
    lo = _lane_lo(qb.shape)
    zero = jnp.zeros_like(qb)
    return jnp.where(lo, qb, zero), jnp.where(lo, zero, qb)


def _attn_ab_kernel(*refs, n_pieces, lam_init, aliased):
    if aliased:
        refs = refs[1:]
    q_ref, dq_ref = refs[0], refs[1]
    pieces = [refs[2 + 4 * p: 6 + 4 * p] for p in range(n_pieces)]
    lam_ref, subg_ref, o_ref = refs[2 + 4 * n_pieces:]
    lam = lam_ref[...]
    lam_full = (jnp.exp(jnp.sum(lam[0:1] * lam[1:2], axis=-1, keepdims=True))
                - jnp.exp(jnp.sum(lam[2:3] * lam[3:4], axis=-1, keepdims=True)) + lam_init)
    lo = _lane_lo((q_ref.shape[0], LANES))
    subg = subg_ref[...]
    tq = q_ref.shape[0]
    def mla_scores(hd):
        sl = slice(LANES * hd, LANES * (hd + 1))
        qh = q_ref[:, sl]
        return [_dot_nt(qh, k_ref[:, sl]) for (k_ref, _, _, _) in pieces]

    def diff_scores(hd):
        sl = slice(LANES * hd, LANES * (hd + 1))
        qq = jnp.concatenate(_split_halves(dq_ref[:, sl]), axis=0)
        return [_dot_nt(qq, dk_ref[:, sl]) for (_, _, dk_ref, _) in pieces]

    pair = {}

    def mla_finish(hd, scores):
        j = hd // 2
        vals = [v_ref[:, LANES * j: LANES * (j + 1)] for (_, v_ref, _, _) in pieces]
        pair[hd % 2] = _softmax_pv(scores, vals)
        if hd % 2 == 1:
            o_ref[:, LANES * j: LANES * (j + 1)] = jnp.where(lo, pair[0], pair[1]).astype(BF16)

    def diff_finish(hd, scores):
        sl = slice(LANES * hd, LANES * (hd + 1))
        oo = _softmax_pv(scores, [dv_ref[:, sl] for (_, _, _, dv_ref) in pieces])
        od = _rms(oo[:tq] - lam_full * oo[tq:], subg) * (1.0 - lam_init)
        o_ref[:, 512 + LANES * hd: 512 + LANES * (hd + 1)] = od.astype(BF16)

    units = []
    for j in range(DIFF_HEADS):
        units += [(diff_scores, diff_finish, j), (mla_scores, mla_finish, 2 * j),
                  (mla_scores, mla_finish, 2 * j + 1)]
    _pipelined_units(units)


def _attn_ab(q, dq, new_kv, cache_kv, lam, sub_g, lam_init, *, n_batch, t_len, tq, tok_off, out_init=None):
    nq = t_len // tq
    q_off = tok_off // tq
    b_off = tok_off // t_len
    widths = (1024, 512, 512, 512)
    in_specs = [pl.BlockSpec((tq, 1024), lambda b, i: (q_off + b * nq + i, 0)),
                pl.BlockSpec((tq, 512), lambda b, i: (q_off + b * nq + i, 0))]
    args = [q, dq]
    for w, a in zip(widths, new_kv):
        in_specs.append(pl.BlockSpec((t_len, w), lambda b, i: (b_off + b, 0)))
        args.append(a)
    n_pieces = 1
    if cache_kv is not None:
        n_pieces = 2
        for w, a in zip(widths, cache_kv):
            in_specs.append(pl.BlockSpec((PAST_LEN, w), lambda b, i: (b, 0)))
            args.append(a)
    in_specs += [pl.BlockSpec((4, DIFF_DH), lambda b, i: (0, 0)),
                 pl.BlockSpec((1, 2 * DIFF_DH), lambda b, i: (0, 0))]
    args += [lam, sub_g]
    aliases = {}
    if out_init is not None:
        in_specs = [pl.BlockSpec(memory_space=pl.ANY)] + in_specs
        args = [out_init] + args
        aliases = {0: 0}
    return pl.pallas_call(
        functools.partial(_attn_ab_kernel, n_pieces=n_pieces, lam_init=lam_init, aliased=out_init is not None),
        grid=(n_batch, nq),
        in_specs=in_specs,
        out_specs=pl.BlockSpec((tq, 1024), lambda b, i: (q_off + b * nq + i, 0)),
        out_shape=jax.ShapeDtypeStruct((N_TOK, 1024), BF16),
        input_output_aliases=aliases,
        compiler_params=_cparams(2),
        name="attn_ab_%d" % n_pieces,
    )(*args)


def _gqa_stacks(sq_ref, sink_ref):
    tq = sq_ref.shape[0]
    halves = [_split_halves(sq_ref[:, LANES * j: LANES * (j + 1)]) for j in range(4)]
    q_stacks = [jnp.concatenate([halves[j][kvh] for j in range(4)], axis=0) for kvh in range(SWA_KV_HEADS)]
    sinks = [jnp.concatenate([jnp.full((tq, 1), sink_ref[4 * kvh + j] * LOG2E, F32) for j in range(4)], axis=0)
             for kvh in range(SWA_KV_HEADS)]
    return q_stacks, sinks


def _attn_cd_prompt_kernel(sink_ref, sq_ref, sk_ref, sv_ref, nq_ref, nk_ref, nv_ref, o_ref):
    tq = sq_ref.shape[0]
    lo = _lane_lo((tq, LANES))
    sk = sk_ref[...]
    sv = sv_ref[...]
    for j in range(4):
        sl = slice(LANES * j, LANES * (j + 1))
        q_lo, q_hi = _split_halves(sq_ref[:, sl])
        o_lo = _softmax_pv([_dot_nt(q_lo, sk)], [sv], sink=sink_ref[j] * LOG2E)
        o_hi = _softmax_pv([_dot_nt(q_hi, sk)], [sv], sink=sink_ref[j + 4] * LOG2E)
        o_ref[:, sl] = jnp.where(lo, o_lo, o_hi).astype(BF16)
    for j in range(4):
        sl = slice(LANES * j, LANES * (j + 1))
        q_lo, q_hi = _split_halves(nq_ref[:, sl])
        k = nk_ref[:, sl]
        v = nv_ref[:, sl]
        o_lo = _softmax_pv([_dot_nt(q_lo, k)], [v])
        o_hi = _softmax_pv([_dot_nt(q_hi, k)], [v])
        o_ref[:, 512 + LANES * j: 512 + LANES * (j + 1)] = jnp.where(lo, o_lo, o_hi).astype(BF16)


def _attn_cd_prompt(sink, sq, sk, sv, nq, nk, nv):
    def spec(w):
        return pl.BlockSpec((SEQ, w), lambda b: (b, 0))
    return pl.pallas_call(
        _attn_cd_prompt_kernel,
        grid=(BATCH,),
        in_specs=[pl.BlockSpec(memory_space=pltpu.SMEM), spec(512), spec(128), spec(128),
                  spec(512), spec(512), spec(512)],
        out_specs=spec(1024),
        out_shape=jax.ShapeDtypeStruct((N_TOK, 1024), BF16),
        compiler_params=_cparams(1),
        name="attn_cd_prompt",
    )(sink, sq, sk, sv, nq, nk, nv)


_SWA_KEYS = TQ + 2 * SWA_WINDOW


def _attn_cd_sample_kernel(init_ref, sink_ref, sq_ref, nq_ref, sk_ref, sv_ref, nk_ref, nv_ref,
                           skc_ref, svc_ref, nkc_ref, nvc_ref, bias_ref, o_ref):
    del init_ref
    qi = pl.program_id(1)
    lo = _lane_lo((TQ, LANES))
    ks = pl.multiple_of(jnp.clip(qi * TQ - SWA_WINDOW, 0, DEC_SEQ - _SWA_KEYS), SWA_WINDOW)
    k_win = sk_ref[pl.ds(ks, _SWA_KEYS), :]
    v_win = sv_ref[pl.ds(ks, _SWA_KEYS), :]
    q_pos = qi * TQ + (lax.broadcasted_iota(jnp.int32, (4 * TQ, _SWA_KEYS), 0) & (TQ - 1))
    k_pos = ks + lax.broadcasted_iota(jnp.int32, (4 * TQ, _SWA_KEYS), 1)
    in_win = jnp.abs(q_pos - k_pos) <= SWA_WINDOW
    skc = skc_ref[...]
    svc = svc_ref[...]
    q_stacks, sinks = _gqa_stacks(sq_ref, sink_ref)

    def window_scores(kvh):
        return [_dot_nt(q_stacks[kvh], skc), jnp.where(in_win, _dot_nt(q_stacks[kvh], k_win), NEG)]

    o_kv = {}

    def window_finish(kvh, scores):
        o_kv[kvh] = _softmax_pv(scores, [svc, v_win], sink=sinks[kvh])
        if kvh == SWA_KV_HEADS - 1:
            for j in range(4):
                rows = slice(TQ * j, TQ * (j + 1))
                o_ref[:, LANES * j: LANES * (j + 1)] = jnp.where(lo, o_kv[0][rows], o_kv[1][rows]).astype(BF16)

    n_rows = DEC_SEQ // GRID_W
    r0 = jnp.clip(qi * NA_TILE_ROWS - NA_WIN_ROWS // 2, 0, n_rows - NA_KEY_ROWS)
    kn = pl.multiple_of(r0 * GRID_W, GRID_W)

    lane_lo = _lane_lo((1, LANES))
    pieces = []
    for ri in range(NA_TILE_ROWS):
        r = qi * NA_TILE_ROWS + ri
        rs = jnp.clip(r - NA_WIN_ROWS // 2, 0, n_rows - NA_WIN_ROWS)
        row = []
        for mm in range(NA_KEY_ROWS // 2):
            kr = r0 + 2 * mm
            ok = [(kr + t >= rs) & (kr + t < rs + NA_WIN_ROWS) for t in range(2)]
            mask = jnp.where(lane_lo, jnp.where(ok[0], 0.0, NEG), jnp.where(ok[1], 0.0, NEG))
            row.append((jnp.clip(kr - r + NA_WIN_ROWS, 0, 2 * NA_WIN_ROWS - 1), mask))
        pieces.append(row)

    def na_bias(hd):
        return jnp.concatenate(
            [jnp.concatenate([bias_ref[hd, d] + mask for d, mask in row], axis=1) for row in pieces], axis=0)

    def na_scores(hd):
        sl = slice(LANES * (hd // 2), LANES * (hd // 2 + 1))
        q_half = _split_halves(nq_ref[:, sl])[hd % 2]
        nk_win = nk_ref[pl.ds(kn, NA_KEY_ROWS * GRID_W), sl]
        return [_dot_nt(q_half, nkc_ref[:, sl]), _dot_nt(q_half, nk_win) + na_bias(hd)]

    pair = {}

    def na_finish(hd, scores):
        j = hd // 2
        sl = slice(LANES * j, LANES * (j + 1))
        nv_win = nv_ref[pl.ds(kn, NA_KEY_ROWS * GRID_W), sl]
        pair[hd % 2] = _softmax_pv(scores, [nvc_ref[:, sl], nv_win])
        if hd % 2 == 1:
            o_ref[:, 512 + LANES * j: 512 + LANES * (j + 1)] = jnp.where(lo, pair[0], pair[1]).astype(BF16)

    _pipelined_units([(window_scores, window_finish, kvh) for kvh in range(SWA_KV_HEADS)]
                     + [(na_scores, na_finish, hd) for hd in range(NA_HEADS)])


def _attn_cd_sample(out_init, sink, sq, sk, sv, nq, nk, nv, skc, svc, nkc, nvc, bias):
    nq_t = DEC_SEQ // TQ
    q_off = NP_TOK // TQ
    b_off = NP_TOK // DEC_SEQ

    def qspec(w):
        return pl.BlockSpec((TQ, w), lambda b, i: (q_off + b * nq_t + i, 0))

    def kspec(w):
        return pl.BlockSpec((DEC_SEQ, w), lambda b, i: (b_off + b, 0))

    def cspec(w):
        return pl.BlockSpec((PAST_LEN, w), lambda b, i: (b, 0))

    bias_spec = pl.BlockSpec((NA_HEADS, 2 * NA_WIN_ROWS, GRID_W, LANES), lambda b, i: (0, 0, 0, 0))
    return pl.pallas_call(
        _attn_cd_sample_kernel,
        grid=(DEC_BATCH, nq_t),
        in_specs=[pl.BlockSpec(memory_space=pl.ANY), pl.BlockSpec(memory_space=pltpu.SMEM),
                  qspec(512), qspec(512), kspec(128), kspec(128), kspec(512), kspec(512),
                  cspec(128), cspec(128), cspec(512), cspec(512), bias_spec],
        out_specs=pl.BlockSpec((TQ, 1024), lambda b, i: (q_off + b * nq_t + i, 0)),
        out_shape=jax.ShapeDtypeStruct((N_TOK, 1024), BF16),
        input_output_aliases={0: 0},
        compiler_params=_cparams(2),
        name="attn_cd_sample",
    )(out_init, sink, sq, nq, sk, sv, nk, nv, skc, svc, nkc, nvc, bias)


def _na_bias(rpb):
    n_dc = 2 * NA_WIN_COLS - 1
    c = np.arange(GRID_W)[:, None]
    kc = np.arange(GRID_W)[None, :]
    qs = np.clip(c - NA_WIN_COLS // 2, 0, GRID_W - NA_WIN_COLS)
    col_ok = (kc >= qs) & (kc < qs + NA_WIN_COLS)
    dc = np.clip(kc - c + NA_WIN_COLS - 1, 0, n_dc - 1)
    onehot = ((dc[None] == np.arange(n_dc)[:, None, None]) & col_ok[None]).astype(np.float32)
    blocks = jnp.einsum('hrd,dck->hrck', rpb.astype(F32) * LOG2E, onehot, precision=lax.Precision.HIGHEST)
    blocks = jnp.where(col_ok[None, None], blocks, NEG)
    none = jnp.full((NA_HEADS, 1, GRID_W, GRID_W), NEG, F32)
    return jnp.concatenate([jnp.concatenate([none, blocks], axis=1),
                            jnp.concatenate([blocks, none], axis=1)], axis=-1)


_HI_MASK = -65536


def _pack_pairs(x):
    w = x.shape[1] // 2
    r = x.astype(BF16).astype(F32)
    lo = lax.bitcast_convert_type(r[:, :w], jnp.int32)
    hi = lax.bitcast_convert_type(r[:, w:], jnp.int32)
    return (hi & _HI_MASK) | lax.shift_right_logical(lo, 16)


def _unpack_pairs(p):
    lo = lax.bitcast_convert_type(lax.shift_left(p, 16), F32)
    hi = lax.bitcast_convert_type(p & _HI_MASK, F32)
    return lo, hi


def _postmix_kernel(*refs, split_x):
    if split_x:
        o_ref, xp_ref, xs_ref = refs[:3]
        refs = refs[3:]
    else:
        o_ref, x_ref = refs[:2]
        refs = refs[2:]
    (mod_ref, g_ref, wout_ref, wr_ref, br_ref, tri_ref,
     x1_ref, h2_ref, route_ref, gate_ref, cnt_ref, run_ref) = refs
    i = pl.program_id(0)

    @pl.when(i == 0)
    def _():
        run_ref[...] = jnp.zeros_like(run_ref)

    m = mod_ref[0]
    g = g_ref[...]
    sub_logits = {}

    def project(n):
        return _dot(o_ref[_sub_rows(n), :], wout_ref[...])

    def finish(n, y):
        r = _sub_rows(n)
        x = jnp.where(i < NPT, xp_ref[r, :], xs_ref[r, :]) if split_x else x_ref[r, :]
        x1 = x + m[2:3] * _rms(y, g[1:2])
        x1_ref[r, :] = x1
        h2 = _rms(x1, g[2:3]) * (1.0 + m[4:5]) + m[3:4]
        h2_ref[r, :] = _pack_pairs(h2)
        sub_logits[n] = _dot(h2.astype(BF16), wr_ref[...]) + br_ref[...]

    _pipelined_units([(project, finish, n) for n in range(N_SUB)])

    logits = jnp.concatenate([sub_logits[n] for n in range(N_SUB)], axis=0)
    lane = lax.broadcasted_iota(jnp.int32, logits.shape, 1).astype(F32)
    cur = jnp.where(lane < N_EXPERTS, logits, -jnp.inf)
    tops, idxs = [], []
    for _ in range(TOP_K):
        mx = jnp.max(cur, axis=-1, keepdims=True)
        ix = jnp.min(jnp.where(cur == mx, lane, float(LANES)), axis=-1, keepdims=True)
        tops.append(mx)
        idxs.append(ix)
        cur = jnp.where(lane == ix, -jnp.inf, cur)
    es = [jnp.exp(t - tops[0]) for t in tops]
    inv = 1.0 / (es[0] + es[1] + es[2] + es[3])
    picked = jnp.zeros_like(logits)
    for k in range(TOP_K):
        picked = jnp.where(lane == idxs[k], 1.0, picked)
    before = _dot(tri_ref[...], picked.astype(BF16)) + run_ref[0:1, :]
    route = jnp.zeros_like(logits)
    gate_out = jnp.zeros_like(logits)
    for k in range(TOP_K):
        rank = jnp.sum(jnp.where(lane == idxs[k], before, 0.0), axis=-1, keepdims=True)
        route = jnp.where(lane == float(k), idxs[k], route)
        route = jnp.where(lane == float(TOP_K + k), rank, route)
        gate_out = jnp.where(lane == float(k), es[k] * inv, gate_out)
    route_ref[...] = route.T[:2 * TOP_K].astype(jnp.int32)
    gate_ref[...] = gate_out
    run_ref[...] = run_ref[...] + jnp.sum(picked, axis=0, keepdims=True)
    cnt_ref[...] = run_ref[...].astype(jnp.int32)


def _postmix(o_cat, xs, mod_l, g_l, w_out, w_r, b_r):
    tri = jnp.asarray(np.tril(np.ones((TM, TM), np.float32), -1), BF16)
    split_x = len(xs) == 2
    x_specs = [_PROMPT_SPEC, _SAMPLE_SPEC] if split_x else [_tok_spec(D_MODEL)]
    return pl.pallas_call(
        functools.partial(_postmix_kernel, split_x=split_x),
        grid=(N_TILES,),
        in_specs=[_tok_spec(1024)] + x_specs + [_MOD_SPEC, _const_spec((4, D_MODEL)),
                  _const_spec((1024, D_MODEL)), _const_spec((D_MODEL, LANES)), _const_spec((1, LANES)),
                  _const_spec((TM, TM))],
        out_specs=[_tok_spec(D_MODEL), _tok_spec(D_MODEL // 2), pl.BlockSpec((2 * TOP_K, TM), lambda i: (0, i)),
                   _tok_spec(LANES), _const_spec((8, LANES))],
        out_shape=[jax.ShapeDtypeStruct((N_TOK, D_MODEL), F32),
                   jax.ShapeDtypeStruct((N_TOK, D_MODEL // 2), jnp.int32),
                   jax.ShapeDtypeStruct((2 * TOP_K, N_TOK), jnp.int32), jax.ShapeDtypeStruct((N_TOK, LANES), F32),
                   jax.ShapeDtypeStruct((8, LANES), jnp.int32)],
        scratch_shapes=[pltpu.VMEM((8, LANES), F32)],
        compiler_params=_cparams(1),
        name="postmix",
    )(o_cat, *xs, mod_l, g_l, w_out, w_r, b_r, tri)


SC_WORKERS = 32
SC_ROWS = 64
ROW_WORDS = D_MODEL // 2

_SC_SCRATCH = [pltpu.VMEM((SC_ROWS, ROW_WORDS), jnp.int32), pltpu.VMEM((SC_ROWS, ROW_WORDS), jnp.int32),
               pltpu.SemaphoreType.DMA, pltpu.SemaphoreType.DMA, pltpu.SemaphoreType.DMA, pltpu.SemaphoreType.DMA]


def _sc_worker_id():
    return lax.axis_index("s") * 2 + lax.axis_index("c")


def _sc_double_buffered(n_chunks, load, store):
    for cp in load(0, 0):
        cp.start()

    @pl.loop(0, n_chunks, step=2)
    def _(g0):
        for b in range(2):
            g = g0 + b
            for cp in load(g, b):
                cp.wait()

            @pl.when(g >= 1)
            def _():
                for cp in store(g - 1, 1 - b):
                    cp.wait()

            @pl.when(g + 1 < n_chunks)
            def _():
                for cp in load(g + 1, 1 - b):
                    cp.start()

            for cp in store(g, b):
                cp.start()

    for cp in store(n_chunks - 1, (n_chunks - 1) % 2):
        cp.wait()


def _sc_dispatch(src, idx):
    n_chunks = N_TOK // (SC_WORKERS * SC_ROWS)
    assert n_chunks % 2 == 0
    mesh = plsc.VectorSubcoreMesh(core_axis_name="c", subcore_axis_name="s")

    @functools.partial(
        pl.kernel, mesh=mesh,
        out_type=jax.ShapeDtypeStruct((MOE_ROWS, ROW_WORDS), jnp.int32),
        scratch_types=[pltpu.VMEM((n_chunks * TOP_K, SC_ROWS), jnp.int32)] + _SC_SCRATCH)
    def k(src_hbm, idx_hbm, out_hbm, idx_v, buf0, buf1, in0, in1, out0, out1):
        wid = _sc_worker_id()
        pltpu.sync_copy(idx_hbm.at[wid], idx_v)
        bufs, in_sems, out_sems = (buf0, buf1), (in0, in1), (out0, out1)

        def load(g, b):
            rows = pl.ds((wid * n_chunks + g) * SC_ROWS, SC_ROWS)
            return [pltpu.make_async_copy(src_hbm.at[rows], bufs[b], in_sems[b])]

        def store(g, b):
            return [pltpu.make_async_copy(bufs[b], out_hbm.at[idx_v.at[g * TOP_K + kk]], out_sems[b])
                    for kk in range(TOP_K)]

        _sc_double_buffered(n_chunks, load, store)

    return k(src, idx)


def _sc_collect(table, idx):
    n_chunks = idx.shape[1]
    assert n_chunks % 2 == 0
    mesh = plsc.VectorSubcoreMesh(core_axis_name="c", subcore_axis_name="s")

    @functools.partial(
        pl.kernel, mesh=mesh,
        out_type=jax.ShapeDtypeStruct((SC_WORKERS * n_chunks * SC_ROWS, ROW_WORDS), jnp.int32),
        scratch_types=[pltpu.VMEM((n_chunks, SC_ROWS), jnp.int32)] + _SC_SCRATCH)
    def k(table_hbm, idx_hbm, out_hbm, idx_v, buf0, buf1, in0, in1, out0, out1):
        wid = _sc_worker_id()
        pltpu.sync_copy(idx_hbm.at[wid], idx_v)
        bufs, in_sems, out_sems = (buf0, buf1), (in0, in1), (out0, out1)

        def load(g, b):
            return [pltpu.make_async_copy(table_hbm.at[idx_v.at[g]], bufs[b], in_sems[b])]

        def store(g, b):
            rows = pl.ds((wid * n_chunks + g) * SC_ROWS, SC_ROWS)
            return [pltpu.make_async_copy(bufs[b], out_hbm.at[rows], out_sems[b])]

        _sc_double_buffered(n_chunks, load, store)

    return k(table, idx)


def _expert_rows(words, n_valid, wgu_b, wd_b, bgu, bd):
    live = lax.broadcasted_iota(jnp.int32, words.shape, 0) < n_valid
    lo, hi = _unpack_pairs(jnp.where(live, words, 0))
    x = jnp.concatenate([lo, hi], axis=1).astype(BF16)
    gu = _dot(x, wgu_b[...]) + bgu
    g = jnp.minimum(gu[:, :D_EXPERT], SWIGLU_LIMIT)
    u = jnp.clip(gu[:, D_EXPERT:], -SWIGLU_LIMIT, SWIGLU_LIMIT)
    a = g * jax.nn.sigmoid(SWIGLU_ALPHA * g) * (u + 1.0)
    return _pack_pairs(_dot(a.astype(BF16), wd_b[...]) + bd)


def _moe_kernel(blk_e_ref, blk_first_ref, blk_rows_ref, blk_slot_ref, blk_next_ref,
                x_ref, wgu_hbm, bgu_ref, wd_hbm, bd_ref, y_ref,
                wgu_f, wd_f, wgu_b, wd_b, sem, *, layer):
    i = pl.program_id(0)
    n_valid = blk_rows_ref[i]
    quantum = MOE_TM // MOE_TAIL_PARTS

    def weight_copies(e, slot):
        return (pltpu.make_async_copy(wgu_hbm.at[layer, e], wgu_f.at[slot], sem.at[0, slot]),
                pltpu.make_async_copy(wd_hbm.at[layer, e], wd_f.at[slot], sem.at[1, slot]))

    @pl.when(i == 0)
    def _():
        for cp in weight_copies(blk_e_ref[0], blk_slot_ref[0]):
            cp.start()

    @pl.when(blk_first_ref[i] == 1)
    def _():
        slot = blk_slot_ref[i]
        for cp in weight_copies(blk_e_ref[i], slot):
            cp.wait()
        nxt = blk_next_ref[i]

        @pl.when(nxt >= 0)
        def _():
            for cp in weight_copies(nxt, 1 - slot):
                cp.start()

        wgu_b[...] = wgu_f[slot].astype(BF16)
        wd_b[...] = wd_f[slot].astype(BF16)

    for parts in range(1, MOE_TAIL_PARTS + 1):
        rows = parts * quantum

        @pl.when((n_valid > rows - quantum) & (n_valid <= rows))
        def _(rows=rows):
            y_ref[:rows] = _expert_rows(x_ref[:rows], n_valid, wgu_b, wd_b, bgu_ref[0, 0], bd_ref[0, 0])
            if rows < MOE_TM:
                y_ref[rows:] = jnp.zeros((MOE_TM - rows, ROW_WORDS), jnp.int32)

    @pl.when(n_valid == 0)
    def _():
        y_ref[...] = jnp.zeros_like(y_ref)


def _moe(layer, blk_meta, xs, w_gu, b_gu, w_down, b_down):
    def row_map(i, *_):
        return (i, 0)

    def bias_map(i, e, *_):
        return (layer, e[i], 0, 0)

    grid_spec = pltpu.PrefetchScalarGridSpec(
        num_scalar_prefetch=5,
        grid=(MOE_BLOCKS,),
        in_specs=[
            pl.BlockSpec((MOE_TM, ROW_WORDS), row_map),
            pl.BlockSpec(memory_space=pl.ANY),
            pl.BlockSpec((1, 1, 1, 2 * D_EXPERT), bias_map),
            pl.BlockSpec(memory_space=pl.ANY),
            pl.BlockSpec((1, 1, 1, D_MODEL), bias_map),
        ],
        out_specs=pl.BlockSpec((MOE_TM, ROW_WORDS), row_map),
        scratch_shapes=[pltpu.VMEM((2, D_MODEL, 2 * D_EXPERT), F32), pltpu.VMEM((2, D_EXPERT, D_MODEL), F32),
                        pltpu.VMEM((D_MODEL, 2 * D_EXPERT), BF16), pltpu.VMEM((D_EXPERT, D_MODEL), BF16),
                        pltpu.SemaphoreType.DMA((2, 2))],
    )
    return pl.pallas_call(
        functools.partial(_moe_kernel, layer=layer),
        grid_spec=grid_spec,
        out_shape=jax.ShapeDtypeStruct((MOE_ROWS, ROW_WORDS), jnp.int32),
        compiler_params=_cparams(1),
        name="moe_experts",
    )(*blk_meta, xs, w_gu, b_gu.reshape(DEPTH, N_EXPERTS, 1, 2 * D_EXPERT),
      w_down, b_down.reshape(DEPTH, N_EXPERTS, 1, D_MODEL))


def _route(route, counts):
    experts = jnp.arange(N_EXPERTS, dtype=jnp.int32)
    padded = (counts + MOE_TM - 1) // MOE_TM * MOE_TM
    pad_end = jnp.cumsum(padded)
    pad_start = pad_end - padded
    e = route[0:TOP_K]
    onehot = e[:, :, None] == experts[None, None, :]
    dest = jnp.sum(jnp.where(onehot, pad_start[None, None, :], 0), axis=-1) + route[TOP_K:2 * TOP_K]
    blk_row0 = jnp.arange(MOE_BLOCKS, dtype=jnp.int32) * MOE_TM
    blk_e = jnp.minimum(jnp.sum((pad_end[None, :] <= blk_row0[:, None]).astype(jnp.int32), axis=1),
                        N_EXPERTS - 1)
    on = blk_row0 < pad_end[-1]
    n_on = jnp.sum(on.astype(jnp.int32))
    blk_onehot = blk_e[:, None] == experts[None, :]
    row_end = jnp.sum(jnp.where(blk_onehot, (pad_start + counts)[None, :], 0), axis=1)
    blk_rows = jnp.where(on, jnp.clip(row_end - blk_row0, 0, MOE_TM), 0).astype(jnp.int32)
    last_e = jnp.sum(jnp.where(jnp.arange(MOE_BLOCKS) == n_on - 1, blk_e, 0))
    blk_e = jnp.where(on, blk_e, last_e).astype(jnp.int32)
    prev = jnp.concatenate([jnp.full((1,), -1, jnp.int32), blk_e[:-1]])
    blk_first = (blk_e != prev).astype(jnp.int32)
    blk_slot = ((jnp.cumsum(blk_first) - 1) % 2).astype(jnp.int32)
    later_used = (experts[None, :] > experts[:, None]) & (counts[None, :] > 0)
    next_used = jnp.min(jnp.where(later_used, experts[None, :], N_EXPERTS), axis=1)
    next_used = jnp.where(next_used == N_EXPERTS, -1, next_used)
    blk_next = jnp.sum(jnp.where(blk_e[:, None] == experts[None, :], next_used[None, :], 0), axis=1)
    return dest, (blk_e, blk_first, blk_rows, blk_slot, blk_next.astype(jnp.int32))


def _combine_kernel(y_ref, gate_ref, x1_ref, mod_ref, g_ref, *o_refs):
    i = pl.program_id(0)
    m = mod_ref[0]
    gate = gate_ref[...]
    acc_lo = None
    for k in range(TOP_K):
        lo, hi = _unpack_pairs(y_ref[k])
        gk = gate[:, k:k + 1]
        acc_lo = gk * lo if acc_lo is None else acc_lo + gk * lo
        acc_hi = gk * hi if k == 0 else acc_hi + gk * hi
    acc = jnp.concatenate([acc_lo, acc_hi], axis=1)
    out = x1_ref[...] + m[5:6] * _rms(acc, g_ref[...][3:4])
    if len(o_refs) == 1:
        o_refs[0][...] = out
    else:
        @pl.when(i < NPT)
        def _():
            o_refs[0][...] = out

        @pl.when(i >= NPT)
        def _():
            o_refs[1][...] = out


def _combine(yg, gates, x1, mod_l, g_l, split_out):
    if split_out:
        out_specs = [_PROMPT_SPEC, _SAMPLE_SPEC]
        out_shape = [jax.ShapeDtypeStruct((NP_TOK, D_MODEL), F32), jax.ShapeDtypeStruct((NS_TOK, D_MODEL), F32)]
    else:
        out_specs = [_tok_spec(D_MODEL)]
        out_shape = [jax.ShapeDtypeStruct((N_TOK, D_MODEL), F32)]
    return pl.pallas_call(
        _combine_kernel,
        grid=(N_TILES,),
        in_specs=[pl.BlockSpec((TOP_K, TM, ROW_WORDS), lambda i: (0, i, 0)), _tok_spec(LANES),
                  _tok_spec(D_MODEL), _MOD_SPEC, _const_spec((4, D_MODEL))],
        out_specs=out_specs,
        out_shape=out_shape,
        compiler_params=_cparams(1),
        name="combine",
    )(yg, gates, x1, mod_l, g_l)


def _ffn(layer, o_cat, xs, mod_l, g_l, w_out, w_router, b_router, w_gu, b_gu, w_down, b_down, split_out):
    w_r = jnp.pad(w_router, ((0, 0), (0, LANES - N_EXPERTS))).astype(BF16)
    b_r = jnp.pad(b_router, (0, LANES - N_EXPERTS)).reshape(1, LANES)
    x1, h2p, route, gate_slab, counts = _postmix(o_cat, xs, mod_l, g_l, w_out.astype(BF16), w_r, b_r)
    dest, blk_meta = _route(route, counts[0, :N_EXPERTS])
    n_chunks = N_TOK // (SC_WORKERS * SC_ROWS)
    idx_d = dest.reshape(TOP_K, SC_WORKERS, n_chunks, SC_ROWS).transpose(1, 2, 0, 3).reshape(
        SC_WORKERS, n_chunks * TOP_K, SC_ROWS)
    rows = _sc_dispatch(h2p, idx_d)
    ys = _moe(layer, blk_meta, rows, w_gu, b_gu, w_down, b_down)
    idx_c = dest.reshape(SC_WORKERS, TOP_K * n_chunks, SC_ROWS)
    yg = _sc_collect(ys, idx_c).reshape(TOP_K, N_TOK, ROW_WORDS)
    return _combine(yg, gate_slab, x1, mod_l, g_l, split_out)


def _pad_heads(w, n_heads, width, keep):
    k = w.shape[0]
    w = w.reshape(k, n_heads, width)[:, :, :keep]
    return jnp.pad(w, ((0, 0), (0, 0), (0, LANES - keep))).reshape(k, n_heads * LANES)


def _pe_slab(x):
    return jnp.pad(x, [(0, 0)] * (x.ndim - 1) + [(MLA_NOPE, LANES - MLA_NOPE - MLA_ROPE)])


_SWA_ORDER = np.array([0, 4, 1, 5, 2, 6, 3, 7])


def kernel(x_prompt, x_sample, cache_mla_ckv, cache_mla_krope, cache_diff_k, cache_diff_v, cache_swa_k, cache_swa_v, cache_na_k, cache_na_v, c, c_ctx, w_mod, b_mod, norm_g, w_in0, mla_q_norm, w_uq, mla_kv_norm, w_ukv, diff_lambda, diff_norm, w_out0, w_in1, swa_sink, na_rpb, w_out1, w_router, b_router, w_gu, b_gu, w_down, b_down):
    xs = (x_prompt.reshape(NP_TOK, D_MODEL), x_sample.reshape(NS_TOK, D_MODEL))
    cond = jnp.concatenate([c_ctx[None, :], c, jnp.zeros((16 - 1 - DEC_BATCH, D_MODEL), F32)], axis=0)
    mod = _modulation(cond, w_mod, b_mod).reshape(DEPTH, 16, 6, D_MODEL)
    t64, t32 = _rope_tables()
    states = {}
    for l in range(DEPTH):
        i = l // 2
        g_l = norm_g[l]
        mod_l = mod[l]
        if l % 2 == 0:
            lam_init = 0.8 - 0.6 * math.exp(-0.3 * l)
            wi = w_in0[i]
            w_in_p = jnp.concatenate(
                [wi[:, 0:640], wi[:, 672:2208], _pe_slab(wi[:, 640:672])], axis=1).astype(BF16)
            w_uq_p = _pad_heads(w_uq[i], MLA_HEADS, MLA_NOPE + MLA_ROPE, MLA_NOPE + MLA_ROPE).astype(BF16)
            w_k_p = _pad_heads(w_ukv[i], MLA_HEADS, MLA_NOPE + MLA_V, MLA_NOPE).astype(BF16)
            w_v = w_ukv[i].reshape(MLA_KV_RANK, MLA_HEADS, MLA_NOPE + MLA_V)[:, :, MLA_NOPE:].reshape(
                MLA_KV_RANK, MLA_HEADS * MLA_V).astype(BF16)
            (q, k, v, dq, dk, dv, ckv_st, kpe_st, dk_st, dv_st) = _premix0(
                *xs, mod_l, g_l[0:1], w_in_p, mla_q_norm[i][None, :], w_uq_p, mla_kv_norm[i][None, :],
                w_k_p, w_v, t32, t64)
            states['mla_ckv'] = ckv_st.reshape(BATCH, 1, SEQ, MLA_KV_RANK)
            states['mla_krope'] = kpe_st[:, MLA_NOPE:MLA_NOPE + MLA_ROPE].reshape(BATCH, 1, SEQ, MLA_ROPE)
            states['diff_k'] = dk_st.reshape(BATCH, 1, SEQ, DIFF_HEADS, 2 * DIFF_DH)
            states['diff_v'] = dv_st.reshape(BATCH, 1, SEQ, DIFF_HEADS, 2 * DIFF_DH)
            kc, vc = _mla_cache(cache_mla_ckv[:, i].reshape(DEC_BATCH * PAST_LEN, MLA_KV_RANK),
                                _pe_slab(cache_mla_krope[:, i].reshape(DEC_BATCH * PAST_LEN, MLA_ROPE)),
                                w_k_p, w_v)
            dkc = cache_diff_k[:, i].reshape(DEC_BATCH * PAST_LEN, 512).astype(BF16)
            dvc = cache_diff_v[:, i].reshape(DEC_BATCH * PAST_LEN, 512).astype(BF16)
            lam = diff_lambda[i]
            sub_g = diff_norm[i][None, :]
            o_p = _attn_ab(q, dq, (k, v, dk, dv), None, lam, sub_g, lam_init,
                           n_batch=BATCH, t_len=SEQ, tq=SEQ, tok_off=0)
            o_cat = _attn_ab(q, dq, (k, v, dk, dv), (kc, vc, dkc, dvc), lam, sub_g, lam_init,
                             n_batch=DEC_BATCH, t_len=DEC_SEQ, tq=TQ_AB, tok_off=NP_TOK, out_init=o_p)
            w_out = w_out0[i]
        else:
            wi = w_in1[i]
            sq_cols = (_SWA_ORDER[:, None] * HEAD_DIM + np.arange(HEAD_DIM)[None, :]).reshape(-1)
            w_in_p = jnp.concatenate([wi[:, sq_cols], wi[:, 512:]], axis=1).astype(BF16)
            (sq, sk, sv, nq, nk, nv, sk_st, sv_st, nk_st, nv_st) = _premix1(xs[0], mod_l, g_l[0:1], w_in_p, t64)
            states['swa_k'] = sk_st.reshape(BATCH, 1, SEQ, SWA_KV_HEADS, HEAD_DIM)
            states['swa_v'] = sv_st.reshape(BATCH, 1, SEQ, SWA_KV_HEADS, HEAD_DIM)
            states['na_k'] = nk_st.reshape(BATCH, 1, SEQ, NA_HEADS, HEAD_DIM)
            states['na_v'] = nv_st.reshape(BATCH, 1, SEQ, NA_HEADS, HEAD_DIM)
            skc = cache_swa_k[:, i].reshape(DEC_BATCH * PAST_LEN, 128).astype(BF16)
            svc = cache_swa_v[:, i].reshape(DEC_BATCH * PAST_LEN, 128).astype(BF16)
            nkc = cache_na_k[:, i].reshape(DEC_BATCH * PAST_LEN, 512).astype(BF16)
            nvc = cache_na_v[:, i].reshape(DEC_BATCH * PAST_LEN, 512).astype(BF16)
            sink = swa_sink[i]
            o_p = _attn_cd_prompt(sink, sq, sk, sv, nq, nk, nv)
            o_cat = _attn_cd_sample(o_p, sink, sq, sk, sv, nq, nk, nv, skc, svc, nkc, nvc, _na_bias(na_rpb[i]))
            wo = w_out1[i]
            w_out = jnp.concatenate([wo[sq_cols], wo[512:]], axis=0)
        xs = _ffn(l, o_cat, xs, mod_l, g_l, w_out, w_router[l], b_router[l], w_gu, b_gu, w_down, b_down,
                  split_out=(l == DEPTH - 1))
    return (xs[0].reshape(BATCH, SEQ, D_MODEL), xs[1].reshape(DEC_BATCH, DEC_SEQ, D_MODEL),
            states['mla_ckv'], states['mla_krope'], states['diff_k'], states['diff_v'],
            states['swa_k'], states['swa_v'], states['na_k'], states['na_v'])
```

```python
import functools
import math

import numpy as np
import jax
import jax.numpy as jnp
from jax import lax
from jax.experimental import pallas as pl
from jax.experimental.pallas import tpu as pltpu
from jax.experimental.pallas import tpu_sc as plsc

F32 = jnp.float32
BF16 = jnp.bfloat16

D_MODEL = 1024
BATCH = 16
SEQ = 256
DEPTH = 2
DEC_BATCH = 8
DEC_SEQ = 2048
PAST_LEN = 256
GRID_W = 64
HEAD_DIM = 64
ROPE_THETA = 10000.0
EPS = 1e-6
NEG = -1e30

MLA_HEADS = 8
MLA_Q_RANK = 384
MLA_KV_RANK = 256
MLA_NOPE = 64
MLA_ROPE = 32
MLA_V = 64
DIFF_HEADS = 4
DIFF_DH = 64
SWA_HEADS = 8
SWA_KV_HEADS = 2
SWA_WINDOW = 128
NA_HEADS = 8
NA_WIN_ROWS = 8
NA_WIN_COLS = 16
N_EXPERTS = 32
TOP_K = 4
D_EXPERT = 1024
SWIGLU_LIMIT = 7.0
SWIGLU_ALPHA = 1.702

LANES = 128
NP_TOK = BATCH * SEQ
NS_TOK = DEC_BATCH * DEC_SEQ
N_TOK = NP_TOK + NS_TOK
TM = 512
NPT = NP_TOK // TM
TILES_PER_SAMPLE = DEC_SEQ // TM
N_TILES = N_TOK // TM
N_SUB = 2
SUB_TM = TM // N_SUB
TQ = 256
TQ_AB = 256
MOE_TM = 512
MOE_TAIL_PARTS = 4
MOE_ROWS = ((N_TOK * TOP_K + N_EXPERTS * (MOE_TM - 1)) // MOE_TM + 1) * MOE_TM
MOE_BLOCKS = MOE_ROWS // MOE_TM
NA_TILE_ROWS = TQ // GRID_W
NA_KEY_ROWS = 12
VMEM_LIMIT = 56 * 1024 * 1024


def _cparams(n_axes, vmem=VMEM_LIMIT):
    return pltpu.CompilerParams(dimension_semantics=("arbitrary",) * n_axes,
                                vmem_limit_bytes=vmem)


def _rms(x, g):
    return x * lax.rsqrt(jnp.mean(x * x, axis=-1, keepdims=True) + EPS) * g


def _dot(a, b):
    return jnp.dot(a, b, preferred_element_type=F32)


def _dot_nt(a, b):
    return lax.dot_general(a, b, (((1,), (1,)), ((), ())), preferred_element_type=F32)


def _rope(x, cos, sin_a, sin_b, half):
    return (x * cos + pltpu.roll(x, LANES - half, 1) * sin_a + pltpu.roll(x, half, 1) * sin_b)


def _pipelined_units(units):
    pending = units[0][0](units[0][2])
    for n, (_, finish, arg) in enumerate(units):
        following = units[n + 1][0](units[n + 1][2]) if n + 1 < len(units) else None
        finish(arg, pending)
        pending = following


def _sub_rows(n):
    return slice(SUB_TM * n, SUB_TM * (n + 1))


def _store_head_rows(st_ref, n, hd, n_heads, x):
    st_ref[pl.ds(SUB_TM * n * n_heads + hd, SUB_TM, stride=n_heads), :] = x


def _mod_row(i):
    return jnp.where(i < NPT, 0, 1 + (i - NPT) // TILES_PER_SAMPLE)


def _rope_blk(i):
    return jnp.where(i < NPT, TILES_PER_SAMPLE, (i - NPT) % TILES_PER_SAMPLE)


def _mod_kernel(c_ref, w_ref, b_ref, o_ref):
    c = c_ref[...]
    s = (c * jax.nn.sigmoid(c)).astype(BF16)
    o_ref[0] = _dot(s, w_ref[0].astype(BF16)) + b_ref[0]


def _modulation(cond, w_mod, b_mod):
    nb = 1024
    return pl.pallas_call(
        _mod_kernel,
        grid=(DEPTH, 6 * D_MODEL // nb),
        in_specs=[
            pl.BlockSpec((16, D_MODEL), lambda l, n: (0, 0)),
            pl.BlockSpec((1, D_MODEL, nb), lambda l, n: (l, 0, n)),
            pl.BlockSpec((1, 1, nb), lambda l, n: (l, 0, n)),
        ],
        out_specs=pl.BlockSpec((1, 16, nb), lambda l, n: (l, 0, n)),
        out_shape=jax.ShapeDtypeStruct((DEPTH, 16, 6 * D_MODEL), F32),
        compiler_params=_cparams(2),
        name="modulation",
    )(cond, w_mod, b_mod.reshape(DEPTH, 1, 6 * D_MODEL))


def _rope_tables():
    t = jnp.arange(DEC_SEQ)
    rows = (t // GRID_W).astype(F32)
    cols = (t % GRID_W).astype(F32)

    def angles(r):
        n = r // 4
        inv = ROPE_THETA ** (-jnp.arange(n, dtype=F32) / n)
        return jnp.concatenate([rows[:, None] * inv[None], cols[:, None] * inv[None]], axis=-1)

    def finish(cos, sa, sb):
        ident = (jnp.ones((TM, LANES), F32), jnp.zeros((TM, LANES), F32), jnp.zeros((TM, LANES), F32))
        return tuple(jnp.concatenate([a, b], axis=0) for a, b in zip((cos, sa, sb), ident))

    a64 = angles(64)
    c, s, z = jnp.cos(a64), jnp.sin(a64), jnp.zeros_like(a64)
    t64 = finish(jnp.concatenate([c, c, c, c], -1), jnp.concatenate([-s, z, -s, z], -1),
                 jnp.concatenate([z, s, z, s], -1))
    a32 = angles(32)
    c, s, z = jnp.cos(a32), jnp.sin(a32), jnp.zeros_like(a32)
    one64 = jnp.ones((DEC_SEQ, 64), F32)
    z64 = jnp.zeros((DEC_SEQ, 64), F32)
    z32 = jnp.zeros((DEC_SEQ, 32), F32)
    t32 = finish(jnp.concatenate([one64, c, c, z32], -1), jnp.concatenate([z64, -s, z, z32], -1),
                 jnp.concatenate([z64, z, s, z32], -1))
    return t64, t32


LOG2E = math.log2(math.e)
_MLA_SCALE = (MLA_NOPE + MLA_ROPE) ** -0.5 * LOG2E
_QSCALE = HEAD_DIM ** -0.5 * LOG2E


def _premix0_kernel(xp_ref, xs_ref, mod_ref, g_ref, win_ref, qn_ref, wuq_ref, kvn_ref, wk_ref, wv_ref,
                    c32_ref, sa32_ref, sb32_ref, c64_ref, sa64_ref, sb64_ref,
                    q_ref, k_ref, v_ref, dq_ref, dk_ref, dv_ref,
                    ckv_st, kpe_st, dk_st, dv_st):
    i = pl.program_id(0)
    m = mod_ref[0]
    g = g_ref[...]

    def project(n):
        r = _sub_rows(n)
        x = jnp.where(i < NPT, xp_ref[r, :], xs_ref[r, :])
        h = _rms(x, g) * (1.0 + m[1:2]) + m[0:1]
        return _dot(h.astype(BF16), win_ref[...])

    states = {}

    def finish(n, proj):
        r = _sub_rows(n)
        q_a = proj[:, 0:384]
        kv_a = proj[:, 384:640]
        dq = proj[:, 640:1152]
        dk = proj[:, 1152:1664]
        dv = proj[:, 1664:2176]
        pe = proj[:, 2176:2304]
        q = _dot(_rms(q_a, qn_ref[...]).astype(BF16), wuq_ref[...])
        ckv = _rms(kv_a, kvn_ref[...])
        ckv_b = ckv.astype(BF16)
        kn = _dot(ckv_b, wk_ref[...])
        v_ref[r, :] = _dot(ckv_b, wv_ref[...]).astype(BF16)
        c32, sa32, sb32 = c32_ref[r, :], sa32_ref[r, :], sb32_ref[r, :]
        c64, sa64, sb64 = c64_ref[r, :], sa64_ref[r, :], sb64_ref[r, :]
        pe_r = _rope(pe, c32, sa32, sb32, MLA_ROPE // 2)
        for hd in range(MLA_HEADS):
            sl = slice(LANES * hd, LANES * (hd + 1))
            q_ref[r, sl] = (_rope(q[:, sl], c32, sa32, sb32, MLA_ROPE // 2) * _MLA_SCALE).astype(BF16)
            k_ref[r, sl] = (kn[:, sl] + pe_r).astype(BF16)
        for hd in range(DIFF_HEADS):
            sl = slice(LANES * hd, LANES * (hd + 1))
            dq_ref[r, sl] = (_rope(dq[:, sl], c64, sa64, sb64, DIFF_DH // 2) * _QSCALE).astype(BF16)
            dk_ref[r, sl] = _rope(dk[:, sl], c64, sa64, sb64, DIFF_DH // 2).astype(BF16)
        dv_ref[r, :] = dv.astype(BF16)
        states[n] = (ckv, pe, dk, dv)

    _pipelined_units([(project, finish, n) for n in range(N_SUB)])

    @pl.when(i < NPT)
    def _():
        for n in range(N_SUB):
            r = _sub_rows(n)
            ckv, pe, dk, dv = states[n]
            ckv_st[r, :] = ckv
            kpe_st[r, :] = pe
            for hd in range(DIFF_HEADS):
                _store_head_rows(dk_st, n, hd, DIFF_HEADS, dk[:, LANES * hd: LANES * (hd + 1)])
                _store_head_rows(dv_st, n, hd, DIFF_HEADS, dv[:, LANES * hd: LANES * (hd + 1)])


def _premix1_kernel(x_ref, mod_ref, g_ref, win_ref, c64_ref, sa64_ref, sb64_ref,
                    sq_ref, sk_ref, sv_ref, nq_ref, nk_ref, nv_ref,
                    sk_st, sv_st, nk_st, nv_st):
    i = pl.program_id(0)
    m = mod_ref[0]
    g = g_ref[...]

    def project(n):
        h = _rms(x_ref[_sub_rows(n), :], g) * (1.0 + m[1:2]) + m[0:1]
        return _dot(h.astype(BF16), win_ref[...])

    states = {}

    def finish(n, proj):
        r = _sub_rows(n)
        sq = proj[:, 0:512]
        sk = proj[:, 512:640]
        sv = proj[:, 640:768]
        nq = proj[:, 768:1280]
        nk = proj[:, 1280:1792]
        nv = proj[:, 1792:2304]
        c64, sa64, sb64 = c64_ref[r, :], sa64_ref[r, :], sb64_ref[r, :]
        for hd in range(4):
            sl = slice(LANES * hd, LANES * (hd + 1))
            sq_ref[r, sl] = (_rope(sq[:, sl], c64, sa64, sb64, HEAD_DIM // 2) * _QSCALE).astype(BF16)
        sk_ref[r, :] = _rope(sk, c64, sa64, sb64, HEAD_DIM // 2).astype(BF16)
        sv_ref[r, :] = sv.astype(BF16)
        nq_ref[r, :] = (nq * _QSCALE).astype(BF16)
        nk_ref[r, :] = nk.astype(BF16)
        nv_ref[r, :] = nv.astype(BF16)
        states[n] = (sk, sv, nk, nv)

    _pipelined_units([(project, finish, n) for n in range(N_SUB)])

    @pl.when(i < NPT)
    def _():
        for n in range(N_SUB):
            r = _sub_rows(n)
            sk, sv, nk, nv = states[n]
            sk_st[r, :] = sk
            sv_st[r, :] = sv
            for hd in range(NA_HEADS):
                _store_head_rows(nk_st, n, hd, NA_HEADS, nk[:, HEAD_DIM * hd: HEAD_DIM * (hd + 1)])
                _store_head_rows(nv_st, n, hd, NA_HEADS, nv[:, HEAD_DIM * hd: HEAD_DIM * (hd + 1)])


def _tok_spec(width):
    return pl.BlockSpec((TM, width), lambda i: (i, 0))


_PROMPT_SPEC = pl.BlockSpec((TM, D_MODEL), lambda i: (jnp.minimum(i, NPT - 1), 0))
_SAMPLE_SPEC = pl.BlockSpec((TM, D_MODEL), lambda i: (jnp.maximum(i - NPT, 0), 0))


def _state_spec(width, rows_per_token=1):
    return pl.BlockSpec((TM * rows_per_token, width), lambda i: (jnp.minimum(i, NPT - 1), 0))


def _const_spec(shape):
    return pl.BlockSpec(shape, lambda i: (0,) * len(shape))


_MOD_SPEC = pl.BlockSpec((1, 6, D_MODEL), lambda i: (_mod_row(i), 0, 0))
_ROPE_SPEC = pl.BlockSpec((TM, LANES), lambda i: (_rope_blk(i), 0))


def _premix0(xp, xs, mod_l, g0, w_in_p, q_norm, w_uq_p, kv_norm, w_k_p, w_v, t32, t64):
    outs = [(N_TOK, 1024, BF16), (N_TOK, 1024, BF16), (N_TOK, 512, BF16), (N_TOK, 512, BF16),
            (N_TOK, 512, BF16), (N_TOK, 512, BF16),
            (NP_TOK, 256, F32), (NP_TOK, 128, F32),
            (NP_TOK * DIFF_HEADS, 2 * DIFF_DH, F32), (NP_TOK * DIFF_HEADS, 2 * DIFF_DH, F32)]
    return pl.pallas_call(
        _premix0_kernel,
        grid=(N_TILES,),
        in_specs=[_PROMPT_SPEC, _SAMPLE_SPEC, _MOD_SPEC, _const_spec((1, D_MODEL)),
                  _const_spec((D_MODEL, 2304)), _const_spec((1, MLA_Q_RANK)),
                  _const_spec((MLA_Q_RANK, 1024)), _const_spec((1, MLA_KV_RANK)),
                  _const_spec((MLA_KV_RANK, 1024)), _const_spec((MLA_KV_RANK, 512))]
                 + [_ROPE_SPEC] * 6,
        out_specs=([_tok_spec(w) for (_, w, _) in outs[:6]]
                   + [_state_spec(w, n // NP_TOK) for (n, w, _) in outs[6:]]),
        out_shape=[jax.ShapeDtypeStruct((n, w), dt) for (n, w, dt) in outs],
        compiler_params=_cparams(1),
        name="premix_ab",
    )(xp, xs, mod_l, g0, w_in_p, q_norm, w_uq_p, kv_norm, w_k_p, w_v, *t32, *t64)


def _premix1(x, mod_l, g0, w_in_p, t64):
    outs = [(N_TOK, 512, BF16), (N_TOK, 128, BF16), (N_TOK, 128, BF16), (N_TOK, 512, BF16),
            (N_TOK, 512, BF16), (N_TOK, 512, BF16),
            (NP_TOK, 128, F32), (NP_TOK, 128, F32),
            (NP_TOK * NA_HEADS, HEAD_DIM, F32), (NP_TOK * NA_HEADS, HEAD_DIM, F32)]
    return pl.pallas_call(
        _premix1_kernel,
        grid=(N_TILES,),
        in_specs=[_tok_spec(D_MODEL), _MOD_SPEC, _const_spec((1, D_MODEL)),
                  _const_spec((D_MODEL, 2304))] + [_ROPE_SPEC] * 3,
        out_specs=([_tok_spec(w) for (_, w, _) in outs[:6]]
                   + [_state_spec(w, n // NP_TOK) for (n, w, _) in outs[6:]]),
        out_shape=[jax.ShapeDtypeStruct((n, w), dt) for (n, w, dt) in outs],
        compiler_params=_cparams(1),
        name="premix_cd",
    )(x, mod_l, g0, w_in_p, *t64)


def _mla_cache_kernel(ckv_ref, pe_ref, wk_ref, wv_ref, k_ref, v_ref):
    c = ckv_ref[...].astype(BF16)
    kn = _dot(c, wk_ref[...])
    v_ref[...] = _dot(c, wv_ref[...]).astype(BF16)
    pe = pe_ref[...]
    for hd in range(MLA_HEADS):
        sl = slice(LANES * hd, LANES * (hd + 1))
        k_ref[:, sl] = (kn[:, sl] + pe).astype(BF16)


def _mla_cache(ckv, pe_slab, w_k_p, w_v):
    n = ckv.shape[0]
    tm = 512
    return pl.pallas_call(
        _mla_cache_kernel,
        grid=(n // tm,),
        in_specs=[pl.BlockSpec((tm, MLA_KV_RANK), lambda i: (i, 0)),
                  pl.BlockSpec((tm, LANES), lambda i: (i, 0)),
                  _const_spec((MLA_KV_RANK, 1024)), _const_spec((MLA_KV_RANK, 512))],
        out_specs=[pl.BlockSpec((tm, 1024), lambda i: (i, 0)), pl.BlockSpec((tm, 512), lambda i: (i, 0))],
        out_shape=[jax.ShapeDtypeStruct((n, 1024), BF16), jax.ShapeDtypeStruct((n, 512), BF16)],
        compiler_params=_cparams(1),
        name="mla_cache",
    )(ckv, pe_slab, w_k_p, w_v)


def _softmax_pv(scores, values, sink=None):
    m = jnp.max(scores[0], axis=-1, keepdims=True)
    for s in scores[1:]:
        m = jnp.maximum(m, jnp.max(s, axis=-1, keepdims=True))
    if sink is not None:
        m = jnp.maximum(m, sink)
    l = None
    o = None
    for s, v in zip(scores, values):
        p = jnp.exp2(s - m)
        ls = jnp.sum(p, axis=-1, keepdims=True)
        os_ = _dot(p.astype(BF16), v)
        l = ls if l is None else l + ls
        o = os_ if o is None else o + os_
    if sink is not None:
        l = l + jnp.exp2(sink - m)
    return o * (1.0 / l)


def _lane_lo(shape):
    return lax.broadcasted_iota(jnp.int32, shape, 1) < (LANES // 2)


def _split_halves(qb):
    lo = _lane_lo(qb.shape)
    zero = jnp.zeros_like(qb)
    return jnp.where(lo, qb, zero), jnp.where(lo, zero, qb)


def _attn_ab_kernel(*refs, n_pieces, lam_init, aliased):
    if aliased:
        refs = refs[1:]
    q_ref, dq_ref = refs[0], refs[1]
    pieces = [refs[2 + 4 * p: 6 + 4 * p] for p in range(n_pieces)]
    lam_ref, subg_ref, o_ref = refs[2 + 4 * n_pieces:]
    lam = lam_ref[...]
    lam_full = (jnp.exp(jnp.sum(lam[0:1] * lam[1:2], axis=-1, keepdims=True))
                - jnp.exp(jnp.sum(lam[2:3] * lam[3:4], axis=-1, keepdims=True)) + lam_init)
    lo = _lane_lo((q_ref.shape[0], LANES))
    subg = subg_ref[...]
    tq = q_ref.shape[0]
    def mla_scores(hd):
        sl = slice(LANES * hd, LANES * (hd + 1))
        qh = q_ref[:, sl]
        return [_dot_nt(qh, k_ref[:, sl]) for (k_ref, _, _, _) in pieces]

    def diff_scores(hd):
        sl = slice(LANES * hd, LANES * (hd + 1))
        qq = jnp.concatenate(_split_halves(dq_ref[:, sl]), axis=0)
        return [_dot_nt(qq, dk_ref[:, sl]) for (_, _, dk_ref, _) in pieces]

    pair = {}

    def mla_finish(hd, scores):
        j = hd // 2
        vals = [v_ref[:, LANES * j: LANES * (j + 1)] for (_, v_ref, _, _) in pieces]
        pair[hd % 2] = _softmax_pv(scores, vals)
        if hd % 2 == 1:
            o_ref[:, LANES * j: LANES * (j + 1)] = jnp.where(lo, pair[0], pair[1]).astype(BF16)

    def diff_finish(hd, scores):
        sl = slice(LANES * hd, LANES * (hd + 1))
        oo = _softmax_pv(scores, [dv_ref[:, sl] for (_, _, _, dv_ref) in pieces])
        od = _rms(oo[:tq] - lam_full * oo[tq:], subg) * (1.0 - lam_init)
        o_ref[:, 512 + LANES * hd: 512 + LANES * (hd + 1)] = od.astype(BF16)

    units = []
    for j in range(DIFF_HEADS):
        units += [(diff_scores, diff_finish, j), (mla_scores, mla_finish, 2 * j),
                  (mla_scores, mla_finish, 2 * j + 1)]
    _pipelined_units(units)


def _attn_ab(q, dq, new_kv, cache_kv, lam, sub_g, lam_init, *, n_batch, t_len, tq, tok_off, out_init=None):
    nq = t_len // tq
    q_off = tok_off // tq
    b_off = tok_off // t_len
    widths = (1024, 512, 512, 512)
    in_specs = [pl.BlockSpec((tq, 1024), lambda b, i: (q_off + b * nq + i, 0)),
                pl.BlockSpec((tq, 512), lambda b, i: (q_off + b * nq + i, 0))]
    args = [q, dq]
    for w, a in zip(widths, new_kv):
        in_specs.append(pl.BlockSpec((t_len, w), lambda b, i: (b_off + b, 0)))
        args.append(a)
    n_pieces = 1
    if cache_kv is not None:
        n_pieces = 2
        for w, a in zip(widths, cache_kv):
            in_specs.append(pl.BlockSpec((PAST_LEN, w), lambda b, i: (b, 0)))
            args.append(a)
    in_specs += [pl.BlockSpec((4, DIFF_DH), lambda b, i: (0, 0)),
                 pl.BlockSpec((1, 2 * DIFF_DH), lambda b, i: (0, 0))]
    args += [lam, sub_g]
    aliases = {}
    if out_init is not None:
        in_specs = [pl.BlockSpec(memory_space=pl.ANY)] + in_specs
        args = [out_init] + args
        aliases = {0: 0}
    return pl.pallas_call(
        functools.partial(_attn_ab_kernel, n_pieces=n_pieces, lam_init=lam_init, aliased=out_init is not None),
        grid=(n_batch, nq),
        in_specs=in_specs,
        out_specs=pl.BlockSpec((tq, 1024), lambda b, i: (q_off + b * nq + i, 0)),
        out_shape=jax.ShapeDtypeStruct((N_TOK, 1024), BF16),
        input_output_aliases=aliases,
        compiler_params=_cparams(2),
        name="attn_ab_%d" % n_pieces,
    )(*args)


def _gqa_stacks(sq_ref, sink_ref):
    tq = sq_ref.shape[0]
    halves = [_split_halves(sq_ref[:, LANES * j: LANES * (j + 1)]) for j in range(4)]
    q_stacks = [jnp.concatenate([halves[j][kvh] for j in range(4)], axis=0) for kvh in range(SWA_KV_HEADS)]
    sinks = [jnp.concatenate([jnp.full((tq, 1), sink_ref[4 * kvh + j] * LOG2E, F32) for j in range(4)], axis=0)
             for kvh in range(SWA_KV_HEADS)]
    return q_stacks, sinks


def _attn_cd_prompt_kernel(sink_ref, sq_ref, sk_ref, sv_ref, nq_ref, nk_ref, nv_ref, o_ref):
    tq = sq_ref.shape[0]
    lo = _lane_lo((tq, LANES))
    sk = sk_ref[...]
    sv = sv_ref[...]
    for j in range(4):
        sl = slice(LANES * j, LANES * (j + 1))
        q_lo, q_hi = _split_halves(sq_ref[:, sl])
        o_lo = _softmax_pv([_dot_nt(q_lo, sk)], [sv], sink=sink_ref[j] * LOG2E)
        o_hi = _softmax_pv([_dot_nt(q_hi, sk)], [sv], sink=sink_ref[j + 4] * LOG2E)
        o_ref[:, sl] = jnp.where(lo, o_lo, o_hi).astype(BF16)
    for j in range(4):
        sl = slice(LANES * j, LANES * (j + 1))
        q_lo, q_hi = _split_halves(nq_ref[:, sl])
        k = nk_ref[:, sl]
        v = nv_ref[:, sl]
        o_lo = _softmax_pv([_dot_nt(q_lo, k)], [v])
        o_hi = _softmax_pv([_dot_nt(q_hi, k)], [v])
        o_ref[:, 512 + LANES * j: 512 + LANES * (j + 1)] = jnp.where(lo, o_lo, o_hi).astype(BF16)


def _attn_cd_prompt(sink, sq, sk, sv, nq, nk, nv):
    def spec(w):
        return pl.BlockSpec((SEQ, w), lambda b: (b, 0))
    return pl.pallas_call(
        _attn_cd_prompt_kernel,
        grid=(BATCH,),
        in_specs=[pl.BlockSpec(memory_space=pltpu.SMEM), spec(512), spec(128), spec(128),
                  spec(512), spec(512), spec(512)],
        out_specs=spec(1024),
        out_shape=jax.ShapeDtypeStruct((N_TOK, 1024), BF16),
        compiler_params=_cparams(1),
        name="attn_cd_prompt",
    )(sink, sq, sk, sv, nq, nk, nv)


_SWA_KEYS = TQ + 2 * SWA_WINDOW


def _attn_cd_sample_kernel(init_ref, sink_ref, sq_ref, nq_ref, sk_ref, sv_ref, nk_ref, nv_ref,
                           skc_ref, svc_ref, nkc_ref, nvc_ref, bias_ref, o_ref):
    del init_ref
    qi = pl.program_id(1)
    lo = _lane_lo((TQ, LANES))
    ks = pl.multiple_of(jnp.clip(qi * TQ - SWA_WINDOW, 0, DEC_SEQ - _SWA_KEYS), SWA_WINDOW)
    k_win = sk_ref[pl.ds(ks, _SWA_KEYS), :]
    v_win = sv_ref[pl.ds(ks, _SWA_KEYS), :]
    q_pos = qi * TQ + (lax.broadcasted_iota(jnp.int32, (4 * TQ, _SWA_KEYS), 0) & (TQ - 1))
    k_pos = ks + lax.broadcasted_iota(jnp.int32, (4 * TQ, _SWA_KEYS), 1)
    in_win = jnp.abs(q_pos - k_pos) <= SWA_WINDOW
    skc = skc_ref[...]
    svc = svc_ref[...]
    q_stacks, sinks = _gqa_stacks(sq_ref, sink_ref)

    def window_scores(kvh):
        return [_dot_nt(q_stacks[kvh], skc), jnp.where(in_win, _dot_nt(q_stacks[kvh], k_win), NEG)]

    o_kv = {}

    def window_finish(kvh, scores):
        o_kv[kvh] = _softmax_pv(scores, [svc, v_win], sink=sinks[kvh])
        if kvh == SWA_KV_HEADS - 1:
            for j in range(4):
                rows = slice(TQ * j, TQ * (j + 1))
                o_ref[:, LANES * j: LANES * (j + 1)] = jnp.where(lo, o_kv[0][rows], o_kv[1][rows]).astype(BF16)

    n_rows = DEC_SEQ // GRID_W
    r0 = jnp.clip(qi * NA_TILE_ROWS - NA_WIN_ROWS // 2, 0, n_rows - NA_KEY_ROWS)
    kn = pl.multiple_of(r0 * GRID_W, GRID_W)

    lane_lo = _lane_lo((1, LANES))
    pieces = []
    for ri in range(NA_TILE_ROWS):
        r = qi * NA_TILE_ROWS + ri
        rs = jnp.clip(r - NA_WIN_ROWS // 2, 0, n_rows - NA_WIN_ROWS)
        row = []
        for mm in range(NA_KEY_ROWS // 2):
            kr = r0 + 2 * mm
            ok = [(kr + t >= rs) & (kr + t < rs + NA_WIN_ROWS) for t in range(2)]
            mask = jnp.where(lane_lo, jnp.where(ok[0], 0.0, NEG), jnp.where(ok[1], 0.0, NEG))
            row.append((jnp.clip(kr - r + NA_WIN_ROWS, 0, 2 * NA_WIN_ROWS - 1), mask))
        pieces.append(row)

    def na_bias(hd):
        return jnp.concatenate(
            [jnp.concatenate([bias_ref[hd, d] + mask for d, mask in row], axis=1) for row in pieces], axis=0)

    def na_scores(hd):
        sl = slice(LANES * (hd // 2), LANES * (hd // 2 + 1))
        q_half = _split_halves(nq_ref[:, sl])[hd % 2]
        nk_win = nk_ref[pl.ds(kn, NA_KEY_ROWS * GRID_W), sl]
        return [_dot_nt(q_half, nkc_ref[:, sl]), _dot_nt(q_half, nk_win) + na_bias(hd)]

    pair = {}

    def na_finish(hd, scores):
        j = hd // 2
        sl = slice(LANES * j, LANES * (j + 1))
        nv_win = nv_ref[pl.ds(kn, NA_KEY_ROWS * GRID_W), sl]
        pair[hd % 2] = _softmax_pv(scores, [nvc_ref[:, sl], nv_win])
        if hd % 2 == 1:
            o_ref[:, 512 + LANES * j: 512 + LANES * (j + 1)] = jnp.where(lo, pair[0], pair[1]).astype(BF16)

    _pipelined_units([(window_scores, window_finish, kvh) for kvh in range(SWA_KV_HEADS)]
                     + [(na_scores, na_finish, hd) for hd in range(NA_HEADS)])


def _attn_cd_sample(out_init, sink, sq, sk, sv, nq, nk, nv, skc, svc, nkc, nvc, bias):
    nq_t = DEC_SEQ // TQ
    q_off = NP_TOK // TQ
    b_off = NP_TOK // DEC_SEQ

    def qspec(w):
        return pl.BlockSpec((TQ, w), lambda b, i: (q_off + b * nq_t + i, 0))

    def kspec(w):
        return pl.BlockSpec((DEC_SEQ, w), lambda b, i: (b_off + b, 0))

    def cspec(w):
        return pl.BlockSpec((PAST_LEN, w), lambda b, i: (b, 0))

    bias_spec = pl.BlockSpec((NA_HEADS, 2 * NA_WIN_ROWS, GRID_W, LANES), lambda b, i: (0, 0, 0, 0))
    return pl.pallas_call(
        _attn_cd_sample_kernel,
        grid=(DEC_BATCH, nq_t),
        in_specs=[pl.BlockSpec(memory_space=pl.ANY), pl.BlockSpec(memory_space=pltpu.SMEM),
                  qspec(512), qspec(512), kspec(128), kspec(128), kspec(512), kspec(512),
                  cspec(128), cspec(128), cspec(512), cspec(512), bias_spec],
        out_specs=pl.BlockSpec((TQ, 1024), lambda b, i: (q_off + b * nq_t + i, 0)),
        out_shape=jax.ShapeDtypeStruct((N_TOK, 1024), BF16),
        input_output_aliases={0: 0},
        compiler_params=_cparams(2),
        name="attn_cd_sample",
    )(out_init, sink, sq, nq, sk, sv, nk, nv, skc, svc, nkc, nvc, bias)


def _na_bias(rpb):
    n_dc = 2 * NA_WIN_COLS - 1
    c = np.arange(GRID_W)[:, None]
    kc = np.arange(GRID_W)[None, :]
    qs = np.clip(c - NA_WIN_COLS // 2, 0, GRID_W - NA_WIN_COLS)
    col_ok = (kc >= qs) & (kc < qs + NA_WIN_COLS)
    dc = np.clip(kc - c + NA_WIN_COLS - 1, 0, n_dc - 1)
    onehot = ((dc[None] == np.arange(n_dc)[:, None, None]) & col_ok[None]).astype(np.float32)
    blocks = jnp.einsum('hrd,dck->hrck', rpb.astype(F32) * LOG2E, onehot, precision=lax.Precision.HIGHEST)
    blocks = jnp.where(col_ok[None, None], blocks, NEG)
    none = jnp.full((NA_HEADS, 1, GRID_W, GRID_W), NEG, F32)
    return jnp.concatenate([jnp.concatenate([none, blocks], axis=1),
                            jnp.concatenate([blocks, none], axis=1)], axis=-1)


_HI_MASK = -65536


def _pack_pairs(x):
    w = x.shape[1] // 2
    r = x.astype(BF16).astype(F32)
    lo = lax.bitcast_convert_type(r[:, :w], jnp.int32)
    hi = lax.bitcast_convert_type(r[:, w:], jnp.int32)
    return (hi & _HI_MASK) | lax.shift_right_logical(lo, 16)


def _unpack_pairs(p):
    lo = lax.bitcast_convert_type(lax.shift_left(p, 16), F32)
    hi = lax.bitcast_convert_type(p & _HI_MASK, F32)
    return lo, hi


def _postmix_kernel(*refs, split_x):
    if split_x:
        o_ref, xp_ref, xs_ref = refs[:3]
        refs = refs[3:]
    else:
        o_ref, x_ref = refs[:2]
        refs = refs[2:]
    (mod_ref, g_ref, wout_ref, wr_ref, br_ref, tri_ref,
     x1_ref, h2_ref, route_ref, gate_ref, cnt_ref, run_ref) = refs
    i = pl.program_id(0)

    @pl.when(i == 0)
    def _():
        run_ref[...] = jnp.zeros_like(run_ref)

    m = mod_ref[0]
    g = g_ref[...]
    sub_logits = {}

    def project(n):
        return _dot(o_ref[_sub_rows(n), :], wout_ref[...])

    def finish(n, y):
        r = _sub_rows(n)
        x = jnp.where(i < NPT, xp_ref[r, :], xs_ref[r, :]) if split_x else x_ref[r, :]
        x1 = x + m[2:3] * _rms(y, g[1:2])
        x1_ref[r, :] = x1
        h2 = _rms(x1, g[2:3]) * (1.0 + m[4:5]) + m[3:4]
        h2_ref[r, :] = _pack_pairs(h2)
        sub_logits[n] = _dot(h2.astype(BF16), wr_ref[...]) + br_ref[...]

    _pipelined_units([(project, finish, n) for n in range(N_SUB)])

    logits = jnp.concatenate([sub_logits[n] for n in range(N_SUB)], axis=0)
    lane = lax.broadcasted_iota(jnp.int32, logits.shape, 1).astype(F32)
    cur = jnp.where(lane < N_EXPERTS, logits, -jnp.inf)
    tops, idxs = [], []
    for _ in range(TOP_K):
        mx = jnp.max(cur, axis=-1, keepdims=True)
        ix = jnp.min(jnp.where(cur == mx, lane, float(LANES)), axis=-1, keepdims=True)
        tops.append(mx)
        idxs.append(ix)
        cur = jnp.where(lane == ix, -jnp.inf, cur)
    es = [jnp.exp(t - tops[0]) for t in tops]
    inv = 1.0 / (es[0] + es[1] + es[2] + es[3])
    picked = jnp.zeros_like(logits)
    for k in range(TOP_K):
        picked = jnp.where(lane == idxs[k], 1.0, picked)
    before = _dot(tri_ref[...], picked.astype(BF16)) + run_ref[0:1, :]
    route = jnp.zeros_like(logits)
    gate_out = jnp.zeros_like(logits)
    for k in range(TOP_K):
        rank = jnp.sum(jnp.where(lane == idxs[k], before, 0.0), axis=-1, keepdims=True)
        route = jnp.where(lane == float(k), idxs[k], route)
        route = jnp.where(lane == float(TOP_K + k), rank, route)
        gate_out = jnp.where(lane == float(k), es[k] * inv, gate_out)
    route_ref[...] = route.T[:2 * TOP_K].astype(jnp.int32)
    gate_ref[...] = gate_out
    run_ref[...] = run_ref[...] + jnp.sum(picked, axis=0, keepdims=True)
    cnt_ref[...] = run_ref[...].astype(jnp.int32)


def _postmix(o_cat, xs, mod_l, g_l, w_out, w_r, b_r):
    tri = jnp.asarray(np.tril(np.ones((TM, TM), np.float32), -1), BF16)
    split_x = len(xs) == 2
    x_specs = [_PROMPT_SPEC, _SAMPLE_SPEC] if split_x else [_tok_spec(D_MODEL)]
    return pl.pallas_call(
        functools.partial(_postmix_kernel, split_x=split_x),
        grid=(N_TILES,),
        in_specs=[_tok_spec(1024)] + x_specs + [_MOD_SPEC, _const_spec((4, D_MODEL)),
                  _const_spec((1024, D_MODEL)), _const_spec((D_MODEL, LANES)), _const_spec((1, LANES)),
                  _const_spec((TM, TM))],
        out_specs=[_tok_spec(D_MODEL), _tok_spec(D_MODEL // 2), pl.BlockSpec((2 * TOP_K, TM), lambda i: (0, i)),
                   _tok_spec(LANES), _const_spec((8, LANES))],
        out_shape=[jax.ShapeDtypeStruct((N_TOK, D_MODEL), F32),
                   jax.ShapeDtypeStruct((N_TOK, D_MODEL // 2), jnp.int32),
                   jax.ShapeDtypeStruct((2 * TOP_K, N_TOK), jnp.int32), jax.ShapeDtypeStruct((N_TOK, LANES), F32),
                   jax.ShapeDtypeStruct((8, LANES), jnp.int32)],
        scratch_shapes=[pltpu.VMEM((8, LANES), F32)],
        compiler_params=_cparams(1),
        name="postmix",
    )(o_cat, *xs, mod_l, g_l, w_out, w_r, b_r, tri)


SC_WORKERS = 32
SC_ROWS = 64
ROW_WORDS = D_MODEL // 2

_SC_SCRATCH = [pltpu.VMEM((SC_ROWS, ROW_WORDS), jnp.int32), pltpu.VMEM((SC_ROWS, ROW_WORDS), jnp.int32),
               pltpu.SemaphoreType.DMA, pltpu.SemaphoreType.DMA, pltpu.SemaphoreType.DMA, pltpu.SemaphoreType.DMA]


def _sc_worker_id():
    return lax.axis_index("s") * 2 + lax.axis_index("c")


def _sc_double_buffered(n_chunks, load, store):
    for cp in load(0, 0):
        cp.start()

    @pl.loop(0, n_chunks, step=2)
    def _(g0):
        for b in range(2):
            g = g0 + b
            for cp in load(g, b):
                cp.wait()

            @pl.when(g >= 1)
            def _():
                for cp in store(g - 1, 1 - b):
                    cp.wait()

            @pl.when(g + 1 < n_chunks)
            def _():
                for cp in load(g + 1, 1 - b):
                    cp.start()

            for cp in store(g, b):
                cp.start()

    for cp in store(n_chunks - 1, (n_chunks - 1) % 2):
        cp.wait()


def _sc_dispatch(src, idx):
    n_chunks = N_TOK // (SC_WORKERS * SC_ROWS)
    assert n_chunks % 2 == 0
    mesh = plsc.VectorSubcoreMesh(core_axis_name="c", subcore_axis_name="s")

    @functools.partial(
        pl.kernel, mesh=mesh,
        out_type=jax.ShapeDtypeStruct((MOE_ROWS, ROW_WORDS), jnp.int32),
        scratch_types=[pltpu.VMEM((n_chunks * TOP_K, SC_ROWS), jnp.int32)] + _SC_SCRATCH)
    def k(src_hbm, idx_hbm, out_hbm, idx_v, buf0, buf1, in0, in1, out0, out1):
        wid = _sc_worker_id()
        pltpu.sync_copy(idx_hbm.at[wid], idx_v)
        bufs, in_sems, out_sems = (buf0, buf1), (in0, in1), (out0, out1)

        def load(g, b):
            rows = pl.ds((wid * n_chunks + g) * SC_ROWS, SC_ROWS)
            return [pltpu.make_async_copy(src_hbm.at[rows], bufs[b], in_sems[b])]

        def store(g, b):
            return [pltpu.make_async_copy(bufs[b], out_hbm.at[idx_v.at[g * TOP_K + kk]], out_sems[b])
                    for kk in range(TOP_K)]

        _sc_double_buffered(n_chunks, load, store)

    return k(src, idx)


def _sc_collect(table, idx):
    n_chunks = idx.shape[1]
    assert n_chunks % 2 == 0
    mesh = plsc.VectorSubcoreMesh(core_axis_name="c", subcore_axis_name="s")

    @functools.partial(
        pl.kernel, mesh=mesh,
        out_type=jax.ShapeDtypeStruct((SC_WORKERS * n_chunks * SC_ROWS, ROW_WORDS), jnp.int32),
        scratch_types=[pltpu.VMEM((n_chunks, SC_ROWS), jnp.int32)] + _SC_SCRATCH)
    def k(table_hbm, idx_hbm, out_hbm, idx_v, buf0, buf1, in0, in1, out0, out1):
        wid = _sc_worker_id()
        pltpu.sync_copy(idx_hbm.at[wid], idx_v)
        bufs, in_sems, out_sems = (buf0, buf1), (in0, in1), (out0, out1)

        def load(g, b):
            return [pltpu.make_async_copy(table_hbm.at[idx_v.at[g]], bufs[b], in_sems[b])]

        def store(g, b):
            rows = pl.ds((wid * n_chunks + g) * SC_ROWS, SC_ROWS)
            return [pltpu.make_async_copy(bufs[b], out_hbm.at[rows], out_sems[b])]

        _sc_double_buffered(n_chunks, load, store)

    return k(table, idx)


def _expert_rows(words, n_valid, wgu_b, wd_b, bgu, bd):
    live = lax.broadcasted_iota(jnp.int32, words.shape, 0) < n_valid
    lo, hi = _unpack_pairs(jnp.where(live, words, 0))
    x = jnp.concatenate([lo, hi], axis=1).astype(BF16)
    gu = _dot(x, wgu_b[...]) + bgu
    g = jnp.minimum(gu[:, :D_EXPERT], SWIGLU_LIMIT)
    u = jnp.clip(gu[:, D_EXPERT:], -SWIGLU_LIMIT, SWIGLU_LIMIT)
    a = g * jax.nn.sigmoid(SWIGLU_ALPHA * g) * (u + 1.0)
    return _pack_pairs(_dot(a.astype(BF16), wd_b[...]) + bd)


def _moe_kernel(blk_e_ref, blk_first_ref, blk_rows_ref, blk_slot_ref, blk_next_ref,
                x_ref, wgu_hbm, bgu_ref, wd_hbm, bd_ref, y_ref,
                wgu_f, wd_f, wgu_b, wd_b, sem, *, layer):
    i = pl.program_id(0)
    n_valid = blk_rows_ref[i]
    quantum = MOE_TM // MOE_TAIL_PARTS

    def weight_copies(e, slot):
        return (pltpu.make_async_copy(wgu_hbm.at[layer, e], wgu_f.at[slot], sem.at[0, slot]),
                pltpu.make_async_copy(wd_hbm.at[layer, e], wd_f.at[slot], sem.at[1, slot]))

    @pl.when(i == 0)
    def _():
        for cp in weight_copies(blk_e_ref[0], blk_slot_ref[0]):
            cp.start()

    @pl.when(blk_first_ref[i] == 1)
    def _():
        slot = blk_slot_ref[i]
        for cp in weight_copies(blk_e_ref[i], slot):
            cp.wait()
        nxt = blk_next_ref[i]

        @pl.when(nxt >= 0)
        def _():
            for cp in weight_copies(nxt, 1 - slot):
                cp.start()

        wgu_b[...] = wgu_f[slot].astype(BF16)
        wd_b[...] = wd_f[slot].astype(BF16)

    for parts in range(1, MOE_TAIL_PARTS + 1):
        rows = parts * quantum

        @pl.when((n_valid > rows - quantum) & (n_valid <= rows))
        def _(rows=rows):
            y_ref[:rows] = _expert_rows(x_ref[:rows], n_valid, wgu_b, wd_b, bgu_ref[0, 0], bd_ref[0, 0])
            if rows < MOE_TM:
                y_ref[rows:] = jnp.zeros((MOE_TM - rows, ROW_WORDS), jnp.int32)

    @pl.when(n_valid == 0)
    def _():
        y_ref[...] = jnp.zeros_like(y_ref)


def _moe(layer, blk_meta, xs, w_gu, b_gu, w_down, b_down):
    def row_map(i, *_):
        return (i, 0)

    def bias_map(i, e, *_):
        return (layer, e[i], 0, 0)

    grid_spec = pltpu.PrefetchScalarGridSpec(
        num_scalar_prefetch=5,
        grid=(MOE_BLOCKS,),
        in_specs=[
            pl.BlockSpec((MOE_TM, ROW_WORDS), row_map),
            pl.BlockSpec(memory_space=pl.ANY),
            pl.BlockSpec((1, 1, 1, 2 * D_EXPERT), bias_map),
            pl.BlockSpec(memory_space=pl.ANY),
            pl.BlockSpec((1, 1, 1, D_MODEL), bias_map),
        ],
        out_specs=pl.BlockSpec((MOE_TM, ROW_WORDS), row_map),
        scratch_shapes=[pltpu.VMEM((2, D_MODEL, 2 * D_EXPERT), F32), pltpu.VMEM((2, D_EXPERT, D_MODEL), F32),
                        pltpu.VMEM((D_MODEL, 2 * D_EXPERT), BF16), pltpu.VMEM((D_EXPERT, D_MODEL), BF16),
                        pltpu.SemaphoreType.DMA((2, 2))],
    )
    return pl.pallas_call(
        functools.partial(_moe_kernel, layer=layer),
        grid_spec=grid_spec,
        out_shape=jax.ShapeDtypeStruct((MOE_ROWS, ROW_WORDS), jnp.int32),
        compiler_params=_cparams(1),
        name="moe_experts",
    )(*blk_meta, xs, w_gu, b_gu.reshape(DEPTH, N_EXPERTS, 1, 2 * D_EXPERT),
      w_down, b_down.reshape(DEPTH, N_EXPERTS, 1, D_MODEL))


def _route(route, counts):
    experts = jnp.arange(N_EXPERTS, dtype=jnp.int32)
    padded = (counts + MOE_TM - 1) // MOE_TM * MOE_TM
    pad_end = jnp.cumsum(padded)
    pad_start = pad_end - padded
    e = route[0:TOP_K]
    onehot = e[:, :, None] == experts[None, None, :]
    dest = jnp.sum(jnp.where(onehot, pad_start[None, None, :], 0), axis=-1) + route[TOP_K:2 * TOP_K]
    blk_row0 = jnp.arange(MOE_BLOCKS, dtype=jnp.int32) * MOE_TM
    blk_e = jnp.minimum(jnp.sum((pad_end[None, :] <= blk_row0[:, None]).astype(jnp.int32), axis=1),
                        N_EXPERTS - 1)
    on = blk_row0 < pad_end[-1]
    n_on = jnp.sum(on.astype(jnp.int32))
    blk_onehot = blk_e[:, None] == experts[None, :]
    row_end = jnp.sum(jnp.where(blk_onehot, (pad_start + counts)[None, :], 0), axis=1)
    blk_rows = jnp.where(on, jnp.clip(row_end - blk_row0, 0, MOE_TM), 0).astype(jnp.int32)
    last_e = jnp.sum(jnp.where(jnp.arange(MOE_BLOCKS) == n_on - 1, blk_e, 0))
    blk_e = jnp.where(on, blk_e, last_e).astype(jnp.int32)
    prev = jnp.concatenate([jnp.full((1,), -1, jnp.int32), blk_e[:-1]])
    blk_first = (blk_e != prev).astype(jnp.int32)
    blk_slot = ((jnp.cumsum(blk_first) - 1) % 2).astype(jnp.int32)
    later_used = (experts[None, :] > experts[:, None]) & (counts[None, :] > 0)
    next_used = jnp.min(jnp.where(later_used, experts[None, :], N_EXPERTS), axis=1)
    next_used = jnp.where(next_used == N_EXPERTS, -1, next_used)
    blk_next = jnp.sum(jnp.where(blk_e[:, None] == experts[None, :], next_used[None, :], 0), axis=1)
    return dest, (blk_e, blk_first, blk_rows, blk_slot, blk_next.astype(jnp.int32))


COMBINE_PARTS = 2
PART_TILES = N_TILES // COMBINE_PARTS


def _combine_kernel(*refs, tile0, aliased):
    if aliased:
        refs = refs[1:]
    y_ref, gate_ref, x1_ref, mod_ref, g_ref = refs[:5]
    o_refs = refs[5:]
    i = pl.program_id(0) + tile0
    m = mod_ref[0]
    gate = gate_ref[...]
    acc_lo = None
    for k in range(TOP_K):
        lo, hi = _unpack_pairs(y_ref[k])
        gk = gate[:, k:k + 1]
        acc_lo = gk * lo if acc_lo is None else acc_lo + gk * lo
        acc_hi = gk * hi if k == 0 else acc_hi + gk * hi
    acc = jnp.concatenate([acc_lo, acc_hi], axis=1)
    out = x1_ref[...] + m[5:6] * _rms(acc, g_ref[...][3:4])
    if len(o_refs) == 1:
        o_refs[0][...] = out
    else:
        @pl.when(i < NPT)
        def _():
            o_refs[0][...] = out

        @pl.when(i >= NPT)
        def _():
            o_refs[1][...] = out


def _combine(yg, gates, x1, mod_l, g_l, split_out, part, prev):
    tile0 = part * PART_TILES
    assert PART_TILES >= NPT

    def tspec(w):
        return pl.BlockSpec((TM, w), lambda i: (i + tile0, 0))

    mod_spec = pl.BlockSpec((1, 6, D_MODEL), lambda i: (_mod_row(i + tile0), 0, 0))
    if not split_out:
        out_specs = [tspec(D_MODEL)]
        out_shape = [jax.ShapeDtypeStruct((N_TOK, D_MODEL), F32)]
        alias_src = None if prev is None else prev[0]
    elif part == 0:
        out_specs = [_PROMPT_SPEC, _SAMPLE_SPEC]
        out_shape = [jax.ShapeDtypeStruct((NP_TOK, D_MODEL), F32), jax.ShapeDtypeStruct((NS_TOK, D_MODEL), F32)]
        alias_src = None
    else:
        out_specs = [pl.BlockSpec((TM, D_MODEL), lambda i: (i + tile0 - NPT, 0))]
        out_shape = [jax.ShapeDtypeStruct((NS_TOK, D_MODEL), F32)]
        alias_src = prev[1]
    in_specs = [pl.BlockSpec((TOP_K, TM, ROW_WORDS), lambda i: (0, i, 0)), tspec(LANES), tspec(D_MODEL),
                mod_spec, _const_spec((4, D_MODEL))]
    args = [yg, gates, x1, mod_l, g_l]
    aliases = {}
    if alias_src is not None:
        in_specs = [pl.BlockSpec(memory_space=pl.ANY)] + in_specs
        args = [alias_src] + args
        aliases = {0: 0}
    outs = pl.pallas_call(
        functools.partial(_combine_kernel, tile0=tile0, aliased=alias_src is not None),
        grid=(PART_TILES,),
        in_specs=in_specs,
        out_specs=out_specs,
        out_shape=out_shape,
        input_output_aliases=aliases,
        compiler_params=_cparams(1),
        name="combine",
    )(*args)
    if split_out and part > 0:
        return [prev[0], outs[0]]
    return outs


def _ffn(layer, o_cat, xs, mod_l, g_l, w_out, w_router, b_router, w_gu, b_gu, w_down, b_down, split_out):
    w_r = jnp.pad(w_router, ((0, 0), (0, LANES - N_EXPERTS))).astype(BF16)
    b_r = jnp.pad(b_router, (0, LANES - N_EXPERTS)).reshape(1, LANES)
    x1, h2p, route, gate_slab, counts = _postmix(o_cat, xs, mod_l, g_l, w_out.astype(BF16), w_r, b_r)
    dest, blk_meta = _route(route, counts[0, :N_EXPERTS])
    n_chunks = N_TOK // (SC_WORKERS * SC_ROWS)
    idx_d = dest.reshape(TOP_K, SC_WORKERS, n_chunks, SC_ROWS).transpose(1, 2, 0, 3).reshape(
        SC_WORKERS, n_chunks * TOP_K, SC_ROWS)
    rows = _sc_dispatch(h2p, idx_d)
    ys = _moe(layer, blk_meta, rows, w_gu, b_gu, w_down, b_down)
    part_tok = PART_TILES * TM
    outs = None
    for part in range(COMBINE_PARTS):
        idx_c = dest[:, part * part_tok:(part + 1) * part_tok].reshape(
            SC_WORKERS, TOP_K * n_chunks // COMBINE_PARTS, SC_ROWS)
        yg = _sc_collect(ys, idx_c).reshape(TOP_K, part_tok, ROW_WORDS)
        outs = _combine(yg, gate_slab, x1, mod_l, g_l, split_out, part, outs)
    return outs


def _pad_heads(w, n_heads, width, keep):
    k = w.shape[0]
    w = w.reshape(k, n_heads, width)[:, :, :keep]
    return jnp.pad(w, ((0, 0), (0, 0), (0, LANES - keep))).reshape(k, n_heads * LANES)


def _pe_slab(x):
    return jnp.pad(x, [(0, 0)] * (x.ndim - 1) + [(MLA_NOPE, LANES - MLA_NOPE - MLA_ROPE)])


_SWA_ORDER = np.array([0, 4, 1, 5, 2, 6, 3, 7])


def kernel(x_prompt, x_sample, cache_mla_ckv, cache_mla_krope, cache_diff_k, cache_diff_v, cache_swa_k, cache_swa_v, cache_na_k, cache_na_v, c, c_ctx, w_mod, b_mod, norm_g, w_in0, mla_q_norm, w_uq, mla_kv_norm, w_ukv, diff_lambda, diff_norm, w_out0, w_in1, swa_sink, na_rpb, w_out1, w_router, b_router, w_gu, b_gu, w_down, b_down):
    xs = (x_prompt.reshape(NP_TOK, D_MODEL), x_sample.reshape(NS_TOK, D_MODEL))
    cond = jnp.concatenate([c_ctx[None, :], c, jnp.zeros((16 - 1 - DEC_BATCH, D_MODEL), F32)], axis=0)
    mod = _modulation(cond, w_mod, b_mod).reshape(DEPTH, 16, 6, D_MODEL)
    t64, t32 = _rope_tables()
    states = {}
    for l in range(DEPTH):
        i = l // 2
        g_l = norm_g[l]
        mod_l = mod[l]
        if l % 2 == 0:
            lam_init = 0.8 - 0.6 * math.exp(-0.3 * l)
            wi = w_in0[i]
            w_in_p = jnp.concatenate(
                [wi[:, 0:640], wi[:, 672:2208], _pe_slab(wi[:, 640:672])], axis=1).astype(BF16)
            w_uq_p = _pad_heads(w_uq[i], MLA_HEADS, MLA_NOPE + MLA_ROPE, MLA_NOPE + MLA_ROPE).astype(BF16)
            w_k_p = _pad_heads(w_ukv[i], MLA_HEADS, MLA_NOPE + MLA_V, MLA_NOPE).astype(BF16)
            w_v = w_ukv[i].reshape(MLA_KV_RANK, MLA_HEADS, MLA_NOPE + MLA_V)[:, :, MLA_NOPE:].reshape(
                MLA_KV_RANK, MLA_HEADS * MLA_V).astype(BF16)
            (q, k, v, dq, dk, dv, ckv_st, kpe_st, dk_st, dv_st) = _premix0(
                *xs, mod_l, g_l[0:1], w_in_p, mla_q_norm[i][None, :], w_uq_p, mla_kv_norm[i][None, :],
                w_k_p, w_v, t32, t64)
            states['mla_ckv'] = ckv_st.reshape(BATCH, 1, SEQ, MLA_KV_RANK)
            states['mla_krope'] = kpe_st[:, MLA_NOPE:MLA_NOPE + MLA_ROPE].reshape(BATCH, 1, SEQ, MLA_ROPE)
            states['diff_k'] = dk_st.reshape(BATCH, 1, SEQ, DIFF_HEADS, 2 * DIFF_DH)
            states['diff_v'] = dv_st.reshape(BATCH, 1, SEQ, DIFF_HEADS, 2 * DIFF_DH)
            kc, vc = _mla_cache(cache_mla_ckv[:, i].reshape(DEC_BATCH * PAST_LEN, MLA_KV_RANK),
                                _pe_slab(cache_mla_krope[:, i].reshape(DEC_BATCH * PAST_LEN, MLA_ROPE)),
                                w_k_p, w_v)
            dkc = cache_diff_k[:, i].reshape(DEC_BATCH * PAST_LEN, 512).astype(BF16)
            dvc = cache_diff_v[:, i].reshape(DEC_BATCH * PAST_LEN, 512).astype(BF16)
            lam = diff_lambda[i]
            sub_g = diff_norm[i][None, :]
            o_p = _attn_ab(q, dq, (k, v, dk, dv), None, lam, sub_g, lam_init,
                           n_batch=BATCH, t_len=SEQ, tq=SEQ, tok_off=0)
            o_cat = _attn_ab(q, dq, (k, v, dk, dv), (kc, vc, dkc, dvc), lam, sub_g, lam_init,
                             n_batch=DEC_BATCH, t_len=DEC_SEQ, tq=TQ_AB, tok_off=NP_TOK, out_init=o_p)
            w_out = w_out0[i]
        else:
            wi = w_in1[i]
            sq_cols = (_SWA_ORDER[:, None] * HEAD_DIM + np.arange(HEAD_DIM)[None, :]).reshape(-1)
            w_in_p = jnp.concatenate([wi[:, sq_cols], wi[:, 512:]], axis=1).astype(BF16)
            (sq, sk, sv, nq, nk, nv, sk_st, sv_st, nk_st, nv_st) = _premix1(xs[0], mod_l, g_l[0:1], w_in_p, t64)
            states['swa_k'] = sk_st.reshape(BATCH, 1, SEQ, SWA_KV_HEADS, HEAD_DIM)
            states['swa_v'] = sv_st.reshape(BATCH, 1, SEQ, SWA_KV_HEADS, HEAD_DIM)
            states['na_k'] = nk_st.reshape(BATCH, 1, SEQ, NA_HEADS, HEAD_DIM)
            states['na_v'] = nv_st.reshape(BATCH, 1, SEQ, NA_HEADS, HEAD_DIM)
            skc = cache_swa_k[:, i].reshape(DEC_BATCH * PAST_LEN, 128).astype(BF16)
            svc = cache_swa_v[:, i].reshape(DEC_BATCH * PAST_LEN, 128).astype(BF16)
            nkc = cache_na_k[:, i].reshape(DEC_BATCH * PAST_LEN, 512).astype(BF16)
            nvc = cache_na_v[:, i].reshape(DEC_BATCH * PAST_LEN, 512).astype(BF16)
            sink = swa_sink[i]
            o_p = _attn_cd_prompt(sink, sq, sk, sv, nq, nk, nv)
            o_cat = _attn_cd_sample(o_p, sink, sq, sk, sv, nq, nk, nv, skc, svc, nkc, nvc, _na_bias(na_rpb[i]))
            wo = w_out1[i]
            w_out = jnp.concatenate([wo[sq_cols], wo[512:]], axis=0)
        xs = _ffn(l, o_cat, xs, mod_l, g_l, w_out, w_router[l], b_router[l], w_gu, b_gu, w_down, b_down,
                  split_out=(l == DEPTH - 1))
    return (xs[0].reshape(BATCH, SEQ, D_MODEL), xs[1].reshape(DEC_BATCH, DEC_SEQ, D_MODEL),
            states['mla_ckv'], states['mla_krope'], states['diff_k'], states['diff_v'],
            states['swa_k'], states['swa_v'], states['na_k'], states['na_v'])
```

```python
import functools
import math

import numpy as np
import jax
import jax.numpy as jnp
from jax import lax
from jax.experimental import pallas as pl
from jax.experimental.pallas import tpu as pltpu
from jax.experimental.pallas import tpu_sc as plsc

F32 = jnp.float32
BF16 = jnp.bfloat16

D_MODEL = 1024
BATCH = 16
SEQ = 256
DEPTH = 2
DEC_BATCH = 8
DEC_SEQ = 2048
PAST_LEN = 256
GRID_W = 64
HEAD_DIM = 64
ROPE_THETA = 10000.0
EPS = 1e-6
NEG = -1e30

MLA_HEADS = 8
MLA_Q_RANK = 384
MLA_KV_RANK = 256
MLA_NOPE = 64
MLA_ROPE = 32
MLA_V = 64
DIFF_HEADS = 4
DIFF_DH = 64
SWA_HEADS = 8
SWA_KV_HEADS = 2
SWA_WINDOW = 128
NA_HEADS = 8
NA_WIN_ROWS = 8
NA_WIN_COLS = 16
N_EXPERTS = 32
TOP_K = 4
D_EXPERT = 1024
SWIGLU_LIMIT = 7.0
SWIGLU_ALPHA = 1.702

LANES = 128
NP_TOK = BATCH * SEQ
NS_TOK = DEC_BATCH * DEC_SEQ
N_TOK = NP_TOK + NS_TOK
TM = 512
NPT = NP_TOK // TM
TILES_PER_SAMPLE = DEC_SEQ // TM
N_TILES = N_TOK // TM
N_SUB = 2
SUB_TM = TM // N_SUB
TQ = 256
TQ_AB = 256
MOE_TM = 512
MOE_TAIL_PARTS = 4
MOE_ROWS = ((N_TOK * TOP_K + N_EXPERTS * (MOE_TM - 1)) // MOE_TM + 1) * MOE_TM
MOE_BLOCKS = MOE_ROWS // MOE_TM
NA_TILE_ROWS = TQ // GRID_W
NA_KEY_ROWS = 12
VMEM_LIMIT = 56 * 1024 * 1024


def _cparams(n_axes, vmem=VMEM_LIMIT):
    return pltpu.CompilerParams(dimension_semantics=("arbitrary",) * n_axes,
                                vmem_limit_bytes=vmem)


def _rms(x, g):
    return x * lax.rsqrt(jnp.mean(x * x, axis=-1, keepdims=True) + EPS) * g


def _dot(a, b):
    return jnp.dot(a, b, preferred_element_type=F32)


def _dot_nt(a, b):
    return lax.dot_general(a, b, (((1,), (1,)), ((), ())), preferred_element_type=F32)


def _rope(x, cos, sin_a, sin_b, half):
    return (x * cos + pltpu.roll(x, LANES - half, 1) * sin_a + pltpu.roll(x, half, 1) * sin_b)


def _pipelined_units(units):
    pending = units[0][0](units[0][2])
    for n, (_, finish, arg) in enumerate(units):
        following = units[n + 1][0](units[n + 1][2]) if n + 1 < len(units) else None
        finish(arg, pending)
        pending = following


def _sub_rows(n):
    return slice(SUB_TM * n, SUB_TM * (n + 1))


def _store_head_rows(st_ref, n, hd, n_heads, x):
    st_ref[pl.ds(SUB_TM * n * n_heads + hd, SUB_TM, stride=n_heads), :] = x


def _mod_row(i):
    return jnp.where(i < NPT, 0, 1 + (i - NPT) // TILES_PER_SAMPLE)


def _rope_blk(i):
    return jnp.where(i < NPT, TILES_PER_SAMPLE, (i - NPT) % TILES_PER_SAMPLE)


def _mod_kernel(c_ref, w_ref, b_ref, o_ref):
    c = c_ref[...]
    s = (c * jax.nn.sigmoid(c)).astype(BF16)
    o_ref[0] = _dot(s, w_ref[0].astype(BF16)) + b_ref[0]


def _modulation(cond, w_mod, b_mod):
    nb = 1024
    return pl.pallas_call(
        _mod_kernel,
        grid=(DEPTH, 6 * D_MODEL // nb),
        in_specs=[
            pl.BlockSpec((16, D_MODEL), lambda l, n: (0, 0)),
            pl.BlockSpec((1, D_MODEL, nb), lambda l, n: (l, 0, n)),
            pl.BlockSpec((1, 1, nb), lambda l, n: (l, 0, n)),
        ],
        out_specs=pl.BlockSpec((1, 16, nb), lambda l, n: (l, 0, n)),
        out_shape=jax.ShapeDtypeStruct((DEPTH, 16, 6 * D_MODEL), F32),
        compiler_params=_cparams(2),
        name="modulation",
    )(cond, w_mod, b_mod.reshape(DEPTH, 1, 6 * D_MODEL))


def _rope_tables():
    t = jnp.arange(DEC_SEQ)
    rows = (t // GRID_W).astype(F32)
    cols = (t % GRID_W).astype(F32)

    def angles(r):
        n = r // 4
        inv = ROPE_THETA ** (-jnp.arange(n, dtype=F32) / n)
        return jnp.concatenate([rows[:, None] * inv[None], cols[:, None] * inv[None]], axis=-1)

    def finish(cos, sa, sb):
        ident = (jnp.ones((TM, LANES), F32), jnp.zeros((TM, LANES), F32), jnp.zeros((TM, LANES), F32))
        return tuple(jnp.concatenate([a, b], axis=0) for a, b in zip((cos, sa, sb), ident))

    a64 = angles(64)
    c, s, z = jnp.cos(a64), jnp.sin(a64), jnp.zeros_like(a64)
    t64 = finish(jnp.concatenate([c, c, c, c], -1), jnp.concatenate([-s, z, -s, z], -1),
                 jnp.concatenate([z, s, z, s], -1))
    a32 = angles(32)
    c, s, z = jnp.cos(a32), jnp.sin(a32), jnp.zeros_like(a32)
    one64 = jnp.ones((DEC_SEQ, 64), F32)
    z64 = jnp.zeros((DEC_SEQ, 64), F32)
    z32 = jnp.zeros((DEC_SEQ, 32), F32)
    t32 = finish(jnp.concatenate([one64, c, c, z32], -1), jnp.concatenate([z64, -s, z, z32], -1),
                 jnp.concatenate([z64, z, s, z32], -1))
    return t64, t32


LOG2E = math.log2(math.e)
_DIFF_COLS = DIFF_HEADS * 2 * DIFF_DH
_AB_COLS = tuple(np.cumsum([0, MLA_Q_RANK, MLA_KV_RANK, _DIFF_COLS, _DIFF_COLS, _DIFF_COLS, LANES]).tolist())
_CD_COLS = tuple(np.cumsum([0, SWA_HEADS * HEAD_DIM, SWA_KV_HEADS * HEAD_DIM, SWA_KV_HEADS * HEAD_DIM,
                            NA_HEADS * HEAD_DIM, NA_HEADS * HEAD_DIM, NA_HEADS * HEAD_DIM]).tolist())
IN_COLS = _AB_COLS[-1]
assert IN_COLS == _CD_COLS[-1]
_MLA_SCALE = (MLA_NOPE + MLA_ROPE) ** -0.5 * LOG2E
_QSCALE = HEAD_DIM ** -0.5 * LOG2E


def _premix0_kernel(xp_ref, xs_ref, mod_ref, g_ref, win_ref, qn_ref, wuq_ref, kvn_ref, wk_ref, wv_ref,
                    c32_ref, sa32_ref, sb32_ref, c64_ref, sa64_ref, sb64_ref,
                    q_ref, k_ref, v_ref, dq_ref, dk_ref, dv_ref,
                    ckv_st, kpe_st, dk_st, dv_st):
    i = pl.program_id(0)
    m = mod_ref[0]
    g = g_ref[...]

    def project(n):
        r = _sub_rows(n)
        x = jnp.where(i < NPT, xp_ref[r, :], xs_ref[r, :])
        h = _rms(x, g) * (1.0 + m[1:2]) + m[0:1]
        return _dot(h.astype(BF16), win_ref[...])

    states = {}

    def finish(n, proj):
        r = _sub_rows(n)
        q_a, kv_a, dq, dk, dv, pe = (proj[:, a:b] for a, b in zip(_AB_COLS[:-1], _AB_COLS[1:]))
        q = _dot(_rms(q_a, qn_ref[...]).astype(BF16), wuq_ref[...])
        ckv = _rms(kv_a, kvn_ref[...])
        ckv_b = ckv.astype(BF16)
        kn = _dot(ckv_b, wk_ref[...])
        v_ref[r, :] = _dot(ckv_b, wv_ref[...]).astype(BF16)
        c32, sa32, sb32 = c32_ref[r, :], sa32_ref[r, :], sb32_ref[r, :]
        c64, sa64, sb64 = c64_ref[r, :], sa64_ref[r, :], sb64_ref[r, :]
        pe_r = _rope(pe, c32, sa32, sb32, MLA_ROPE // 2)
        for hd in range(MLA_HEADS):
            sl = slice(LANES * hd, LANES * (hd + 1))
            q_ref[r, sl] = (_rope(q[:, sl], c32, sa32, sb32, MLA_ROPE // 2) * _MLA_SCALE).astype(BF16)
            k_ref[r, sl] = (kn[:, sl] + pe_r).astype(BF16)
        for hd in range(DIFF_HEADS):
            sl = slice(LANES * hd, LANES * (hd + 1))
            dq_ref[r, sl] = (_rope(dq[:, sl], c64, sa64, sb64, DIFF_DH // 2) * _QSCALE).astype(BF16)
            dk_ref[r, sl] = _rope(dk[:, sl], c64, sa64, sb64, DIFF_DH // 2).astype(BF16)
        dv_ref[r, :] = dv.astype(BF16)
        states[n] = (ckv, pe, dk, dv)

    _pipelined_units([(project, finish, n) for n in range(N_SUB)])

    @pl.when(i < NPT)
    def _():
        for n in range(N_SUB):
            r = _sub_rows(n)
            ckv, pe, dk, dv = states[n]
            ckv_st[r, :] = ckv
            kpe_st[r, :] = pe
            for hd in range(DIFF_HEADS):
                _store_head_rows(dk_st, n, hd, DIFF_HEADS, dk[:, LANES * hd: LANES * (hd + 1)])
                _store_head_rows(dv_st, n, hd, DIFF_HEADS, dv[:, LANES * hd: LANES * (hd + 1)])


def _premix1_kernel(x_ref, mod_ref, g_ref, win_ref, c64_ref, sa64_ref, sb64_ref,
                    sq_ref, sk_ref, sv_ref, nq_ref, nk_ref, nv_ref,
                    sk_st, sv_st, nk_st, nv_st):
    i = pl.program_id(0)
    m = mod_ref[0]
    g = g_ref[...]

    def project(n):
        h = _rms(x_ref[_sub_rows(n), :], g) * (1.0 + m[1:2]) + m[0:1]
        return _dot(h.astype(BF16), win_ref[...])

    states = {}

    def finish(n, proj):
        r = _sub_rows(n)
        sq, sk, sv, nq, nk, nv = (proj[:, a:b] for a, b in zip(_CD_COLS[:-1], _CD_COLS[1:]))
        c64, sa64, sb64 = c64_ref[r, :], sa64_ref[r, :], sb64_ref[r, :]
        for hd in range(4):
            sl = slice(LANES * hd, LANES * (hd + 1))
            sq_ref[r, sl] = (_rope(sq[:, sl], c64, sa64, sb64, HEAD_DIM // 2) * _QSCALE).astype(BF16)
        sk_ref[r, :] = _rope(sk, c64, sa64, sb64, HEAD_DIM // 2).astype(BF16)
        sv_ref[r, :] = sv.astype(BF16)
        nq_ref[r, :] = (nq * _QSCALE).astype(BF16)
        nk_ref[r, :] = nk.astype(BF16)
        nv_ref[r, :] = nv.astype(BF16)
        states[n] = (sk, sv, nk, nv)

    _pipelined_units([(project, finish, n) for n in range(N_SUB)])

    @pl.when(i < NPT)
    def _():
        for n in range(N_SUB):
            r = _sub_rows(n)
            sk, sv, nk, nv = states[n]
            sk_st[r, :] = sk
            sv_st[r, :] = sv
            for hd in range(NA_HEADS):
                _store_head_rows(nk_st, n, hd, NA_HEADS, nk[:, HEAD_DIM * hd: HEAD_DIM * (hd + 1)])
                _store_head_rows(nv_st, n, hd, NA_HEADS, nv[:, HEAD_DIM * hd: HEAD_DIM * (hd + 1)])


def _tok_spec(width):
    return pl.BlockSpec((TM, width), lambda i: (i, 0))


_PROMPT_SPEC = pl.BlockSpec((TM, D_MODEL), lambda i: (jnp.minimum(i, NPT - 1), 0))
_SAMPLE_SPEC = pl.BlockSpec((TM, D_MODEL), lambda i: (jnp.maximum(i - NPT, 0), 0))


def _state_spec(width, rows_per_token=1):
    return pl.BlockSpec((TM * rows_per_token, width), lambda i: (jnp.minimum(i, NPT - 1), 0))


def _const_spec(shape):
    return pl.BlockSpec(shape, lambda i: (0,) * len(shape))


_MOD_SPEC = pl.BlockSpec((1, 6, D_MODEL), lambda i: (_mod_row(i), 0, 0))
_ROPE_SPEC = pl.BlockSpec((TM, LANES), lambda i: (_rope_blk(i), 0))


def _premix0(xp, xs, mod_l, g0, w_in_p, q_norm, w_uq_p, kv_norm, w_k_p, w_v, t32, t64):
    outs = [(N_TOK, 1024, BF16), (N_TOK, 1024, BF16), (N_TOK, 512, BF16), (N_TOK, 512, BF16),
            (N_TOK, 512, BF16), (N_TOK, 512, BF16),
            (NP_TOK, 256, F32), (NP_TOK, 128, F32),
            (NP_TOK * DIFF_HEADS, 2 * DIFF_DH, F32), (NP_TOK * DIFF_HEADS, 2 * DIFF_DH, F32)]
    return pl.pallas_call(
        _premix0_kernel,
        grid=(N_TILES,),
        in_specs=[_PROMPT_SPEC, _SAMPLE_SPEC, _MOD_SPEC, _const_spec((1, D_MODEL)),
                  _const_spec((D_MODEL, IN_COLS)), _const_spec((1, MLA_Q_RANK)),
                  _const_spec((MLA_Q_RANK, 1024)), _const_spec((1, MLA_KV_RANK)),
                  _const_spec((MLA_KV_RANK, 1024)), _const_spec((MLA_KV_RANK, 512))]
                 + [_ROPE_SPEC] * 6,
        out_specs=([_tok_spec(w) for (_, w, _) in outs[:6]]
                   + [_state_spec(w, n // NP_TOK) for (n, w, _) in outs[6:]]),
        out_shape=[jax.ShapeDtypeStruct((n, w), dt) for (n, w, dt) in outs],
        compiler_params=_cparams(1),
        name="premix_ab",
    )(xp, xs, mod_l, g0, w_in_p, q_norm, w_uq_p, kv_norm, w_k_p, w_v, *t32, *t64)


def _premix1(x, mod_l, g0, w_in_p, t64):
    outs = [(N_TOK, 512, BF16), (N_TOK, 128, BF16), (N_TOK, 128, BF16), (N_TOK, 512, BF16),
            (N_TOK, 512, BF16), (N_TOK, 512, BF16),
            (NP_TOK, 128, F32), (NP_TOK, 128, F32),
            (NP_TOK * NA_HEADS, HEAD_DIM, F32), (NP_TOK * NA_HEADS, HEAD_DIM, F32)]
    return pl.pallas_call(
        _premix1_kernel,
        grid=(N_TILES,),
        in_specs=[_tok_spec(D_MODEL), _MOD_SPEC, _const_spec((1, D_MODEL)),
                  _const_spec((D_MODEL, IN_COLS))] + [_ROPE_SPEC] * 3,
        out_specs=([_tok_spec(w) for (_, w, _) in outs[:6]]
                   + [_state_spec(w, n // NP_TOK) for (n, w, _) in outs[6:]]),
        out_shape=[jax.ShapeDtypeStruct((n, w), dt) for (n, w, dt) in outs],
        compiler_params=_cparams(1),
        name="premix_cd",
    )(x, mod_l, g0, w_in_p, *t64)


def _mla_cache_kernel(ckv_ref, pe_ref, wk_ref, wv_ref, k_ref, v_ref):
    c = ckv_ref[...].astype(BF16)
    kn = _dot(c, wk_ref[...])
    v_ref[...] = _dot(c, wv_ref[...]).astype(BF16)
    pe = pe_ref[...]
    for hd in range(MLA_HEADS):
        sl = slice(LANES * hd, LANES * (hd + 1))
        k_ref[:, sl] = (kn[:, sl] + pe).astype(BF16)


def _mla_cache(ckv, pe_slab, w_k_p, w_v):
    n = ckv.shape[0]
    tm = 512
    return pl.pallas_call(
        _mla_cache_kernel,
        grid=(n // tm,),
        in_specs=[pl.BlockSpec((tm, MLA_KV_RANK), lambda i: (i, 0)),
                  pl.BlockSpec((tm, LANES), lambda i: (i, 0)),
                  _const_spec((MLA_KV_RANK, 1024)), _const_spec((MLA_KV_RANK, 512))],
        out_specs=[pl.BlockSpec((tm, 1024), lambda i: (i, 0)), pl.BlockSpec((tm, 512), lambda i: (i, 0))],
        out_shape=[jax.ShapeDtypeStruct((n, 1024), BF16), jax.ShapeDtypeStruct((n, 512), BF16)],
        compiler_params=_cparams(1),
        name="mla_cache",
    )(ckv, pe_slab, w_k_p, w_v)


def _softmax_pv(scores, values, sink=None):
    m = jnp.max(scores[0], axis=-1, keepdims=True)
    for s in scores[1:]:
        m = jnp.maximum(m, jnp.max(s, axis=-1, keepdims=True))
    if sink is not None:
        m = jnp.maximum(m, sink)
    l = None
    o = None
    for s, v in zip(scores, values):
        p = jnp.exp2(s - m)
        ls = jnp.sum(p, axis=-1, keepdims=True)
        os_ = _dot(p.astype(BF16), v)
        l = ls if l is None else l + ls
        o = os_ if o is None else o + os_
    if sink is not None:
        l = l + jnp.exp2(sink - m)
    return o * (1.0 / l)


def _lane_lo(shape):
    return lax.broadcasted_iota(jnp.int32, shape, 1) < (LANES // 2)


def _split_halves(qb):
    lo = _lane_lo(qb.shape)
    zero = jnp.zeros_like(qb)
    return jnp.where(lo, qb, zero), jnp.where(lo, zero, qb)


def _attn_ab_kernel(*refs, n_pieces, lam_init, aliased):
    if aliased:
        refs = refs[1:]
    q_ref, dq_ref = refs[0], refs[1]
    pieces = [refs[2 + 4 * p: 6 + 4 * p] for p in range(n_pieces)]
    lam_ref, subg_ref, o_ref = refs[2 + 4 * n_pieces:]
    lam = lam_ref[...]
    lam_full = (jnp.exp(jnp.sum(lam[0:1] * lam[1:2], axis=-1, keepdims=True))
                - jnp.exp(jnp.sum(lam[2:3] * lam[3:4], axis=-1, keepdims=True)) + lam_init)
    lo = _lane_lo((q_ref.shape[0], LANES))
    subg = subg_ref[...]
    tq = q_ref.shape[0]
    def mla_scores(hd):
        sl = slice(LANES * hd, LANES * (hd + 1))
        qh = q_ref[:, sl]
        return [_dot_nt(qh, k_ref[:, sl]) for (k_ref, _, _, _) in pieces]

    def diff_scores(hd):
        sl = slice(LANES * hd, LANES * (hd + 1))
        qq = jnp.concatenate(_split_halves(dq_ref[:, sl]), axis=0)
        return [_dot_nt(qq, dk_ref[:, sl]) for (_, _, dk_ref, _) in pieces]

    pair = {}

    def mla_finish(hd, scores):
        j = hd // 2
        vals = [v_ref[:, LANES * j: LANES * (j + 1)] for (_, v_ref, _, _) in pieces]
        pair[hd % 2] = _softmax_pv(scores, vals)
        if hd % 2 == 1:
            o_ref[:, LANES * j: LANES * (j + 1)] = jnp.where(lo, pair[0], pair[1]).astype(BF16)

    def diff_finish(hd, scores):
        sl = slice(LANES * hd, LANES * (hd + 1))
        oo = _softmax_pv(scores, [dv_ref[:, sl] for (_, _, _, dv_ref) in pieces])
        od = _rms(oo[:tq] - lam_full * oo[tq:], subg) * (1.0 - lam_init)
        o_ref[:, 512 + LANES * hd: 512 + LANES * (hd + 1)] = od.astype(BF16)

    units = []
    for j in range(DIFF_HEADS):
        units += [(diff_scores, diff_finish, j), (mla_scores, mla_finish, 2 * j),
                  (mla_scores, mla_finish, 2 * j + 1)]
    _pipelined_units(units)


def _attn_ab(q, dq, new_kv, cache_kv, lam, sub_g, lam_init, *, n_batch, t_len, tq, tok_off, out_init=None):
    nq = t_len // tq
    q_off = tok_off // tq
    b_off = tok_off // t_len
    widths = (1024, 512, 512, 512)
    in_specs = [pl.BlockSpec((tq, 1024), lambda b, i: (q_off + b * nq + i, 0)),
                pl.BlockSpec((tq, 512), lambda b, i: (q_off + b * nq + i, 0))]
    args = [q, dq]
    for w, a in zip(widths, new_kv):
        in_specs.append(pl.BlockSpec((t_len, w), lambda b, i: (b_off + b, 0)))
        args.append(a)
    n_pieces = 1
    if cache_kv is not None:
        n_pieces = 2
        for w, a in zip(widths, cache_kv):
            in_specs.append(pl.BlockSpec((PAST_LEN, w), lambda b, i: (b, 0)))
            args.append(a)
    in_specs += [pl.BlockSpec((4, DIFF_DH), lambda b, i: (0, 0)),
                 pl.BlockSpec((1, 2 * DIFF_DH), lambda b, i: (0, 0))]
    args += [lam, sub_g]
    aliases = {}
    if out_init is not None:
        in_specs = [pl.BlockSpec(memory_space=pl.ANY)] + in_specs
        args = [out_init] + args
        aliases = {0: 0}
    return pl.pallas_call(
        functools.partial(_attn_ab_kernel, n_pieces=n_pieces, lam_init=lam_init, aliased=out_init is not None),
        grid=(n_batch, nq),
        in_specs=in_specs,
        out_specs=pl.BlockSpec((tq, 1024), lambda b, i: (q_off + b * nq + i, 0)),
        out_shape=jax.ShapeDtypeStruct((N_TOK, 1024), BF16),
        input_output_aliases=aliases,
        compiler_params=_cparams(2),
        name="attn_ab_%d" % n_pieces,
    )(*args)


def _gqa_stacks(sq_ref, sink_ref):
    tq = sq_ref.shape[0]
    halves = [_split_halves(sq_ref[:, LANES * j: LANES * (j + 1)]) for j in range(4)]
    q_stacks = [jnp.concatenate([halves[j][kvh] for j in range(4)], axis=0) for kvh in range(SWA_KV_HEADS)]
    sinks = [jnp.concatenate([jnp.full((tq, 1), sink_ref[4 * kvh + j] * LOG2E, F32) for j in range(4)], axis=0)
             for kvh in range(SWA_KV_HEADS)]
    return q_stacks, sinks


def _attn_cd_prompt_kernel(sink_ref, sq_ref, sk_ref, sv_ref, nq_ref, nk_ref, nv_ref, o_ref):
    tq = sq_ref.shape[0]
    lo = _lane_lo((tq, LANES))
    sk = sk_ref[...]
    sv = sv_ref[...]
    for j in range(4):
        sl = slice(LANES * j, LANES * (j + 1))
        q_lo, q_hi = _split_halves(sq_ref[:, sl])
        o_lo = _softmax_pv([_dot_nt(q_lo, sk)], [sv], sink=sink_ref[j] * LOG2E)
        o_hi = _softmax_pv([_dot_nt(q_hi, sk)], [sv], sink=sink_ref[j + 4] * LOG2E)
        o_ref[:, sl] = jnp.where(lo, o_lo, o_hi).astype(BF16)
    for j in range(4):
        sl = slice(LANES * j, LANES * (j + 1))
        q_lo, q_hi = _split_halves(nq_ref[:, sl])
        k = nk_ref[:, sl]
        v = nv_ref[:, sl]
        o_lo = _softmax_pv([_dot_nt(q_lo, k)], [v])
        o_hi = _softmax_pv([_dot_nt(q_hi, k)], [v])
        o_ref[:, 512 + LANES * j: 512 + LANES * (j + 1)] = jnp.where(lo, o_lo, o_hi).astype(BF16)


def _attn_cd_prompt(sink, sq, sk, sv, nq, nk, nv):
    def spec(w):
        return pl.BlockSpec((SEQ, w), lambda b: (b, 0))
    return pl.pallas_call(
        _attn_cd_prompt_kernel,
        grid=(BATCH,),
        in_specs=[pl.BlockSpec(memory_space=pltpu.SMEM), spec(512), spec(128), spec(128),
                  spec(512), spec(512), spec(512)],
        out_specs=spec(1024),
        out_shape=jax.ShapeDtypeStruct((N_TOK, 1024), BF16),
        compiler_params=_cparams(1),
        name="attn_cd_prompt",
    )(sink, sq, sk, sv, nq, nk, nv)


_SWA_KEYS = TQ + 2 * SWA_WINDOW


def _attn_cd_sample_kernel(init_ref, sink_ref, sq_ref, nq_ref, sk_ref, sv_ref, nk_ref, nv_ref,
                           skc_ref, svc_ref, nkc_ref, nvc_ref, bias_ref, o_ref):
    del init_ref
    qi = pl.program_id(1)
    lo = _lane_lo((TQ, LANES))
    ks = pl.multiple_of(jnp.clip(qi * TQ - SWA_WINDOW, 0, DEC_SEQ - _SWA_KEYS), SWA_WINDOW)
    k_win = sk_ref[pl.ds(ks, _SWA_KEYS), :]
    v_win = sv_ref[pl.ds(ks, _SWA_KEYS), :]
    q_pos = qi * TQ + (lax.broadcasted_iota(jnp.int32, (4 * TQ, _SWA_KEYS), 0) & (TQ - 1))
    k_pos = ks + lax.broadcasted_iota(jnp.int32, (4 * TQ, _SWA_KEYS), 1)
    in_win = jnp.abs(q_pos - k_pos) <= SWA_WINDOW
    skc = skc_ref[...]
    svc = svc_ref[...]
    q_stacks, sinks = _gqa_stacks(sq_ref, sink_ref)

    def window_scores(kvh):
        return [_dot_nt(q_stacks[kvh], skc), jnp.where(in_win, _dot_nt(q_stacks[kvh], k_win), NEG)]

    o_kv = {}

    def window_finish(kvh, scores):
        o_kv[kvh] = _softmax_pv(scores, [svc, v_win], sink=sinks[kvh])
        if kvh == SWA_KV_HEADS - 1:
            for j in range(4):
                rows = slice(TQ * j, TQ * (j + 1))
                o_ref[:, LANES * j: LANES * (j + 1)] = jnp.where(lo, o_kv[0][rows], o_kv[1][rows]).astype(BF16)

    n_rows = DEC_SEQ // GRID_W
    r0 = jnp.clip(qi * NA_TILE_ROWS - NA_WIN_ROWS // 2, 0, n_rows - NA_KEY_ROWS)
    kn = pl.multiple_of(r0 * GRID_W, GRID_W)

    lane_lo = _lane_lo((1, LANES))
    pieces = []
    for ri in range(NA_TILE_ROWS):
        r = qi * NA_TILE_ROWS + ri
        rs = jnp.clip(r - NA_WIN_ROWS // 2, 0, n_rows - NA_WIN_ROWS)
        row = []
        for mm in range(NA_KEY_ROWS // 2):
            kr = r0 + 2 * mm
            ok = [(kr + t >= rs) & (kr + t < rs + NA_WIN_ROWS) for t in range(2)]
            mask = jnp.where(lane_lo, jnp.where(ok[0], 0.0, NEG), jnp.where(ok[1], 0.0, NEG))
            row.append((jnp.clip(kr - r + NA_WIN_ROWS, 0, 2 * NA_WIN_ROWS - 1), mask))
        pieces.append(row)

    def na_bias(hd):
        return jnp.concatenate(
            [jnp.concatenate([bias_ref[hd, d] + mask for d, mask in row], axis=1) for row in pieces], axis=0)

    def na_scores(hd):
        sl = slice(LANES * (hd // 2), LANES * (hd // 2 + 1))
        q_half = _split_halves(nq_ref[:, sl])[hd % 2]
        nk_win = nk_ref[pl.ds(kn, NA_KEY_ROWS * GRID_W), sl]
        return [_dot_nt(q_half, nkc_ref[:, sl]), _dot_nt(q_half, nk_win) + na_bias(hd)]

    pair = {}

    def na_finish(hd, scores):
        j = hd // 2
        sl = slice(LANES * j, LANES * (j + 1))
        nv_win = nv_ref[pl.ds(kn, NA_KEY_ROWS * GRID_W), sl]
        pair[hd % 2] = _softmax_pv(scores, [nvc_ref[:, sl], nv_win])
        if hd % 2 == 1:
            o_ref[:, 512 + LANES * j: 512 + LANES * (j + 1)] = jnp.where(lo, pair[0], pair[1]).astype(BF16)

    _pipelined_units([(window_scores, window_finish, kvh) for kvh in range(SWA_KV_HEADS)]
                     + [(na_scores, na_finish, hd) for hd in range(NA_HEADS)])


def _attn_cd_sample(out_init, sink, sq, sk, sv, nq, nk, nv, skc, svc, nkc, nvc, bias):
    nq_t = DEC_SEQ // TQ
    q_off = NP_TOK // TQ
    b_off = NP_TOK // DEC_SEQ

    def qspec(w):
        return pl.BlockSpec((TQ, w), lambda b, i: (q_off + b * nq_t + i, 0))

    def kspec(w):
        return pl.BlockSpec((DEC_SEQ, w), lambda b, i: (b_off + b, 0))

    def cspec(w):
        return pl.BlockSpec((PAST_LEN, w), lambda b, i: (b, 0))

    bias_spec = pl.BlockSpec((NA_HEADS, 2 * NA_WIN_ROWS, GRID_W, LANES), lambda b, i: (0, 0, 0, 0))
    return pl.pallas_call(
        _attn_cd_sample_kernel,
        grid=(DEC_BATCH, nq_t),
        in_specs=[pl.BlockSpec(memory_space=pl.ANY), pl.BlockSpec(memory_space=pltpu.SMEM),
                  qspec(512), qspec(512), kspec(128), kspec(128), kspec(512), kspec(512),
                  cspec(128), cspec(128), cspec(512), cspec(512), bias_spec],
        out_specs=pl.BlockSpec((TQ, 1024), lambda b, i: (q_off + b * nq_t + i, 0)),
        out_shape=jax.ShapeDtypeStruct((N_TOK, 1024), BF16),
        input_output_aliases={0: 0},
        compiler_params=_cparams(2),
        name="attn_cd_sample",
    )(out_init, sink, sq, nq, sk, sv, nk, nv, skc, svc, nkc, nvc, bias)


def _na_bias(rpb):
    n_dc = 2 * NA_WIN_COLS - 1
    c = np.arange(GRID_W)[:, None]
    kc = np.arange(GRID_W)[None, :]
    qs = np.clip(c - NA_WIN_COLS // 2, 0, GRID_W - NA_WIN_COLS)
    col_ok = (kc >= qs) & (kc < qs + NA_WIN_COLS)
    dc = np.clip(kc - c + NA_WIN_COLS - 1, 0, n_dc - 1)
    onehot = ((dc[None] == np.arange(n_dc)[:, None, None]) & col_ok[None]).astype(np.float32)
    blocks = jnp.einsum('hrd,dck->hrck', rpb.astype(F32) * LOG2E, onehot, precision=lax.Precision.HIGHEST)
    blocks = jnp.where(col_ok[None, None], blocks, NEG)
    none = jnp.full((NA_HEADS, 1, GRID_W, GRID_W), NEG, F32)
    return jnp.concatenate([jnp.concatenate([none, blocks], axis=1),
                            jnp.concatenate([blocks, none], axis=1)], axis=-1)


_HI_MASK = -65536


def _pack_pairs(x):
    w = x.shape[1] // 2
    r = x.astype(BF16).astype(F32)
    lo = lax.bitcast_convert_type(r[:, :w], jnp.int32)
    hi = lax.bitcast_convert_type(r[:, w:], jnp.int32)
    return (hi & _HI_MASK) | lax.shift_right_logical(lo, 16)


def _unpack_pairs(p):
    lo = lax.bitcast_convert_type(lax.shift_left(p, 16), F32)
    hi = lax.bitcast_convert_type(p & _HI_MASK, F32)
    return lo, hi


def _postmix_kernel(*refs, split_x):
    if split_x:
        o_ref, xp_ref, xs_ref = refs[:3]
        refs = refs[3:]
    else:
        o_ref, x_ref = refs[:2]
        refs = refs[2:]
    (mod_ref, g_ref, wout_ref, wr_ref, br_ref, tri_ref,
     x1_ref, h2_ref, route_ref, gate_ref, cnt_ref, run_ref) = refs
    i = pl.program_id(0)

    @pl.when(i == 0)
    def _():
        run_ref[...] = jnp.zeros_like(run_ref)

    m = mod_ref[0]
    g = g_ref[...]
    sub_logits = {}

    def project(n):
        return _dot(o_ref[_sub_rows(n), :], wout_ref[...])

    def finish(n, y):
        r = _sub_rows(n)
        x = jnp.where(i < NPT, xp_ref[r, :], xs_ref[r, :]) if split_x else x_ref[r, :]
        x1 = x + m[2:3] * _rms(y, g[1:2])
        x1_ref[r, :] = x1
        h2 = _rms(x1, g[2:3]) * (1.0 + m[4:5]) + m[3:4]
        h2_ref[r, :] = _pack_pairs(h2)
        sub_logits[n] = _dot(h2.astype(BF16), wr_ref[...]) + br_ref[...]

    _pipelined_units([(project, finish, n) for n in range(N_SUB)])

    logits = jnp.concatenate([sub_logits[n] for n in range(N_SUB)], axis=0)
    lane = lax.broadcasted_iota(jnp.int32, logits.shape, 1).astype(F32)
    cur = jnp.where(lane < N_EXPERTS, logits, -jnp.inf)
    tops, idxs = [], []
    for _ in range(TOP_K):
        mx = jnp.max(cur, axis=-1, keepdims=True)
        ix = jnp.min(jnp.where(cur == mx, lane, float(LANES)), axis=-1, keepdims=True)
        tops.append(mx)
        idxs.append(ix)
        cur = jnp.where(lane == ix, -jnp.inf, cur)
    es = [jnp.exp(t - tops[0]) for t in tops]
    inv = 1.0 / (es[0] + es[1] + es[2] + es[3])
    picked = jnp.zeros_like(logits)
    for k in range(TOP_K):
        picked = jnp.where(lane == idxs[k], 1.0, picked)
    before = _dot(tri_ref[...], picked.astype(BF16)) + run_ref[0:1, :]
    route = jnp.zeros_like(logits)
    gate_out = jnp.zeros_like(logits)
    for k in range(TOP_K):
        rank = jnp.sum(jnp.where(lane == idxs[k], before, 0.0), axis=-1, keepdims=True)
        route = jnp.where(lane == float(k), idxs[k], route)
        route = jnp.where(lane == float(TOP_K + k), rank, route)
        gate_out = jnp.where(lane == float(k), es[k] * inv, gate_out)
    route_ref[...] = route.T[:2 * TOP_K].astype(jnp.int32)
    gate_ref[...] = gate_out
    run_ref[...] = run_ref[...] + jnp.sum(picked, axis=0, keepdims=True)
    cnt_ref[...] = run_ref[...].astype(jnp.int32)


def _postmix(o_cat, xs, mod_l, g_l, w_out, w_r, b_r):
    tri = jnp.asarray(np.tril(np.ones((TM, TM), np.float32), -1), BF16)
    split_x = len(xs) == 2
    x_specs = [_PROMPT_SPEC, _SAMPLE_SPEC] if split_x else [_tok_spec(D_MODEL)]
    return pl.pallas_call(
        functools.partial(_postmix_kernel, split_x=split_x),
        grid=(N_TILES,),
        in_specs=[_tok_spec(1024)] + x_specs + [_MOD_SPEC, _const_spec((4, D_MODEL)),
                  _const_spec((1024, D_MODEL)), _const_spec((D_MODEL, LANES)), _const_spec((1, LANES)),
                  _const_spec((TM, TM))],
        out_specs=[_tok_spec(D_MODEL), _tok_spec(D_MODEL // 2), pl.BlockSpec((2 * TOP_K, TM), lambda i: (0, i)),
                   _tok_spec(LANES), _const_spec((8, LANES))],
        out_shape=[jax.ShapeDtypeStruct((N_TOK, D_MODEL), F32),
                   jax.ShapeDtypeStruct((N_TOK, D_MODEL // 2), jnp.int32),
                   jax.ShapeDtypeStruct((2 * TOP_K, N_TOK), jnp.int32), jax.ShapeDtypeStruct((N_TOK, LANES), F32),
                   jax.ShapeDtypeStruct((8, LANES), jnp.int32)],
        scratch_shapes=[pltpu.VMEM((8, LANES), F32)],
        compiler_params=_cparams(1),
        name="postmix",
    )(o_cat, *xs, mod_l, g_l, w_out, w_r, b_r, tri)


SC_WORKERS = 32
SC_ROWS = 64
ROW_WORDS = D_MODEL // 2

_SC_SCRATCH = [pltpu.VMEM((SC_ROWS, ROW_WORDS), jnp.int32), pltpu.VMEM((SC_ROWS, ROW_WORDS), jnp.int32),
               pltpu.SemaphoreType.DMA, pltpu.SemaphoreType.DMA, pltpu.SemaphoreType.DMA, pltpu.SemaphoreType.DMA]


def _sc_worker_id():
    return lax.axis_index("s") * 2 + lax.axis_index("c")


def _sc_double_buffered(n_chunks, load, store):
    for cp in load(0, 0):
        cp.start()

    @pl.loop(0, n_chunks, step=2)
    def _(g0):
        for b in range(2):
            g = g0 + b
            for cp in load(g, b):
                cp.wait()

            @pl.when(g >= 1)
            def _():
                for cp in store(g - 1, 1 - b):
                    cp.wait()

            @pl.when(g + 1 < n_chunks)
            def _():
                for cp in load(g + 1, 1 - b):
                    cp.start()

            for cp in store(g, b):
                cp.start()

    for cp in store(n_chunks - 1, (n_chunks - 1) % 2):
        cp.wait()


def _sc_dispatch(src, idx):
    n_chunks = N_TOK // (SC_WORKERS * SC_ROWS)
    assert n_chunks % 2 == 0
    mesh = plsc.VectorSubcoreMesh(core_axis_name="c", subcore_axis_name="s")

    @functools.partial(
        pl.kernel, mesh=mesh,
        out_type=jax.ShapeDtypeStruct((MOE_ROWS, ROW_WORDS), jnp.int32),
        scratch_types=[pltpu.VMEM((n_chunks * TOP_K, SC_ROWS), jnp.int32)] + _SC_SCRATCH)
    def k(src_hbm, idx_hbm, out_hbm, idx_v, buf0, buf1, in0, in1, out0, out1):
        wid = _sc_worker_id()
        pltpu.sync_copy(idx_hbm.at[wid], idx_v)
        bufs, in_sems, out_sems = (buf0, buf1), (in0, in1), (out0, out1)

        def load(g, b):
            rows = pl.ds((wid * n_chunks + g) * SC_ROWS, SC_ROWS)
            return [pltpu.make_async_copy(src_hbm.at[rows], bufs[b], in_sems[b])]

        def store(g, b):
            return [pltpu.make_async_copy(bufs[b], out_hbm.at[idx_v.at[g * TOP_K + kk]], out_sems[b])
                    for kk in range(TOP_K)]

        _sc_double_buffered(n_chunks, load, store)

    return k(src, idx)


def _sc_collect(table, idx):
    n_chunks = idx.shape[1]
    assert n_chunks % 2 == 0
    mesh = plsc.VectorSubcoreMesh(core_axis_name="c", subcore_axis_name="s")

    @functools.partial(
        pl.kernel, mesh=mesh,
        out_type=jax.ShapeDtypeStruct((SC_WORKERS * n_chunks * SC_ROWS, ROW_WORDS), jnp.int32),
        scratch_types=[pltpu.VMEM((n_chunks, SC_ROWS), jnp.int32)] + _SC_SCRATCH)
    def k(table_hbm, idx_hbm, out_hbm, idx_v, buf0, buf1, in0, in1, out0, out1):
        wid = _sc_worker_id()
        pltpu.sync_copy(idx_hbm.at[wid], idx_v)
        bufs, in_sems, out_sems = (buf0, buf1), (in0, in1), (out0, out1)

        def load(g, b):
            return [pltpu.make_async_copy(table_hbm.at[idx_v.at[g]], bufs[b], in_sems[b])]

        def store(g, b):
            rows = pl.ds((wid * n_chunks + g) * SC_ROWS, SC_ROWS)
            return [pltpu.make_async_copy(bufs[b], out_hbm.at[rows], out_sems[b])]

        _sc_double_buffered(n_chunks, load, store)

    return k(table, idx)


def _expert_rows(words, n_valid, wgu_b, wd_b, bgu, bd):
    live = lax.broadcasted_iota(jnp.int32, words.shape, 0) < n_valid
    lo, hi = _unpack_pairs(jnp.where(live, words, 0))
    x = jnp.concatenate([lo, hi], axis=1).astype(BF16)
    gu = _dot(x, wgu_b[...]) + bgu
    g = jnp.minimum(gu[:, :D_EXPERT], SWIGLU_LIMIT)
    u = jnp.clip(gu[:, D_EXPERT:], -SWIGLU_LIMIT, SWIGLU_LIMIT)
    a = g * jax.nn.sigmoid(SWIGLU_ALPHA * g) * (u + 1.0)
    return _pack_pairs(_dot(a.astype(BF16), wd_b[...]) + bd)


def _moe_kernel(blk_e_ref, blk_first_ref, blk_rows_ref, blk_slot_ref, blk_next_ref,
                x_ref, wgu_hbm, bgu_ref, wd_hbm, bd_ref, y_ref,
                wgu_f, wd_f, wgu_b, wd_b, sem, *, layer):
    i = pl.program_id(0)
    n_valid = blk_rows_ref[i]
    quantum = MOE_TM // MOE_TAIL_PARTS

    def weight_copies(e, slot):
        return (pltpu.make_async_copy(wgu_hbm.at[layer, e], wgu_f.at[slot], sem.at[0, slot]),
                pltpu.make_async_copy(wd_hbm.at[layer, e], wd_f.at[slot], sem.at[1, slot]))

    @pl.when(i == 0)
    def _():
        for cp in weight_copies(blk_e_ref[0], blk_slot_ref[0]):
            cp.start()

    @pl.when(blk_first_ref[i] == 1)
    def _():
        slot = blk_slot_ref[i]
        for cp in weight_copies(blk_e_ref[i], slot):
            cp.wait()
        nxt = blk_next_ref[i]

        @pl.when(nxt >= 0)
        def _():
            for cp in weight_copies(nxt, 1 - slot):
                cp.start()

        wgu_b[...] = wgu_f[slot].astype(BF16)
        wd_b[...] = wd_f[slot].astype(BF16)

    for parts in range(1, MOE_TAIL_PARTS + 1):
        rows = parts * quantum

        @pl.when((n_valid > rows - quantum) & (n_valid <= rows))
        def _(rows=rows):
            y_ref[:rows] = _expert_rows(x_ref[:rows], n_valid, wgu_b, wd_b, bgu_ref[0, 0], bd_ref[0, 0])
            if rows < MOE_TM:
                y_ref[rows:] = jnp.zeros((MOE_TM - rows, ROW_WORDS), jnp.int32)

    @pl.when(n_valid == 0)
    def _():
        y_ref[...] = jnp.zeros_like(y_ref)


def _moe(layer, blk_meta, xs, w_gu, b_gu, w_down, b_down):
    def row_map(i, *_):
        return (i, 0)

    def bias_map(i, e, *_):
        return (layer, e[i], 0, 0)

    grid_spec = pltpu.PrefetchScalarGridSpec(
        num_scalar_prefetch=5,
        grid=(MOE_BLOCKS,),
        in_specs=[
            pl.BlockSpec((MOE_TM, ROW_WORDS), row_map),
            pl.BlockSpec(memory_space=pl.ANY),
            pl.BlockSpec((1, 1, 1, 2 * D_EXPERT), bias_map),
            pl.BlockSpec(memory_space=pl.ANY),
            pl.BlockSpec((1, 1, 1, D_MODEL), bias_map),
        ],
        out_specs=pl.BlockSpec((MOE_TM, ROW_WORDS), row_map),
        scratch_shapes=[pltpu.VMEM((2, D_MODEL, 2 * D_EXPERT), F32), pltpu.VMEM((2, D_EXPERT, D_MODEL), F32),
                        pltpu.VMEM((D_MODEL, 2 * D_EXPERT), BF16), pltpu.VMEM((D_EXPERT, D_MODEL), BF16),
                        pltpu.SemaphoreType.DMA((2, 2))],
    )
    return pl.pallas_call(
        functools.partial(_moe_kernel, layer=layer),
        grid_spec=grid_spec,
        out_shape=jax.ShapeDtypeStruct((MOE_ROWS, ROW_WORDS), jnp.int32),
        compiler_params=_cparams(1),
        name="moe_experts",
    )(*blk_meta, xs, w_gu, b_gu.reshape(DEPTH, N_EXPERTS, 1, 2 * D_EXPERT),
      w_down, b_down.reshape(DEPTH, N_EXPERTS, 1, D_MODEL))


def _route(route, counts):
    experts = jnp.arange(N_EXPERTS, dtype=jnp.int32)
    padded = (counts + MOE_TM - 1) // MOE_TM * MOE_TM
    pad_end = jnp.cumsum(padded)
    pad_start = pad_end - padded
    e = route[0:TOP_K]
    onehot = e[:, :, None] == experts[None, None, :]
    dest = jnp.sum(jnp.where(onehot, pad_start[None, None, :], 0), axis=-1) + route[TOP_K:2 * TOP_K]
    blk_row0 = jnp.arange(MOE_BLOCKS, dtype=jnp.int32) * MOE_TM
    blk_e = jnp.minimum(jnp.sum((pad_end[None, :] <= blk_row0[:, None]).astype(jnp.int32), axis=1),
                        N_EXPERTS - 1)
    on = blk_row0 < pad_end[-1]
    n_on = jnp.sum(on.astype(jnp.int32))
    blk_onehot = blk_e[:, None] == experts[None, :]
    row_end = jnp.sum(jnp.where(blk_onehot, (pad_start + counts)[None, :], 0), axis=1)
    blk_rows = jnp.where(on, jnp.clip(row_end - blk_row0, 0, MOE_TM), 0).astype(jnp.int32)
    last_e = jnp.sum(jnp.where(jnp.arange(MOE_BLOCKS) == n_on - 1, blk_e, 0))
    blk_e = jnp.where(on, blk_e, last_e).astype(jnp.int32)
    prev = jnp.concatenate([jnp.full((1,), -1, jnp.int32), blk_e[:-1]])
    blk_first = (blk_e != prev).astype(jnp.int32)
    blk_slot = ((jnp.cumsum(blk_first) - 1) % 2).astype(jnp.int32)
    later_used = (experts[None, :] > experts[:, None]) & (counts[None, :] > 0)
    next_used = jnp.min(jnp.where(later_used, experts[None, :], N_EXPERTS), axis=1)
    next_used = jnp.where(next_used == N_EXPERTS, -1, next_used)
    blk_next = jnp.sum(jnp.where(blk_e[:, None] == experts[None, :], next_used[None, :], 0), axis=1)
    return dest, (blk_e, blk_first, blk_rows, blk_slot, blk_next.astype(jnp.int32))


def _combine_kernel(y_ref, gate_ref, x1_ref, mod_ref, g_ref, *o_refs):
    i = pl.program_id(0)
    m = mod_ref[0]
    gate = gate_ref[...]
    acc_lo = None
    for k in range(TOP_K):
        lo, hi = _unpack_pairs(y_ref[k])
        gk = gate[:, k:k + 1]
        acc_lo = gk * lo if acc_lo is None else acc_lo + gk * lo
        acc_hi = gk * hi if k == 0 else acc_hi + gk * hi
    acc = jnp.concatenate([acc_lo, acc_hi], axis=1)
    out = x1_ref[...] + m[5:6] * _rms(acc, g_ref[...][3:4])
    if len(o_refs) == 1:
        o_refs[0][...] = out
    else:
        @pl.when(i < NPT)
        def _():
            o_refs[0][...] = out

        @pl.when(i >= NPT)
        def _():
            o_refs[1][...] = out


def _combine(yg, gates, x1, mod_l, g_l, split_out):
    if split_out:
        out_specs = [_PROMPT_SPEC, _SAMPLE_SPEC]
        out_shape = [jax.ShapeDtypeStruct((NP_TOK, D_MODEL), F32), jax.ShapeDtypeStruct((NS_TOK, D_MODEL), F32)]
    else:
        out_specs = [_tok_spec(D_MODEL)]
        out_shape = [jax.ShapeDtypeStruct((N_TOK, D_MODEL), F32)]
    return pl.pallas_call(
        _combine_kernel,
        grid=(N_TILES,),
        in_specs=[pl.BlockSpec((TOP_K, TM, ROW_WORDS), lambda i: (0, i, 0)), _tok_spec(LANES),
                  _tok_spec(D_MODEL), _MOD_SPEC, _const_spec((4, D_MODEL))],
        out_specs=out_specs,
        out_shape=out_shape,
        compiler_params=_cparams(1),
        name="combine",
    )(yg, gates, x1, mod_l, g_l)


def _ffn(layer, o_cat, xs, mod_l, g_l, w_out, w_router, b_router, w_gu, b_gu, w_down, b_down, split_out):
    w_r = jnp.pad(w_router, ((0, 0), (0, LANES - N_EXPERTS))).astype(BF16)
    b_r = jnp.pad(b_router, (0, LANES - N_EXPERTS)).reshape(1, LANES)
    x1, h2p, route, gate_slab, counts = _postmix(o_cat, xs, mod_l, g_l, w_out.astype(BF16), w_r, b_r)
    dest, blk_meta = _route(route, counts[0, :N_EXPERTS])
    n_chunks = N_TOK // (SC_WORKERS * SC_ROWS)
    idx_d = dest.reshape(TOP_K, SC_WORKERS, n_chunks, SC_ROWS).transpose(1, 2, 0, 3).reshape(
        SC_WORKERS, n_chunks * TOP_K, SC_ROWS)
    rows = _sc_dispatch(h2p, idx_d)
    ys = _moe(layer, blk_meta, rows, w_gu, b_gu, w_down, b_down)
    idx_c = dest.reshape(SC_WORKERS, TOP_K * n_chunks, SC_ROWS)
    yg = _sc_collect(ys, idx_c).reshape(TOP_K, N_TOK, ROW_WORDS)
    return _combine(yg, gate_slab, x1, mod_l, g_l, split_out)


def _pad_heads(w, n_heads, width, keep):
    k = w.shape[0]
    w = w.reshape(k, n_heads, width)[:, :, :keep]
    return jnp.pad(w, ((0, 0), (0, 0), (0, LANES - keep))).reshape(k, n_heads * LANES)


def _pe_slab(x):
    return jnp.pad(x, [(0, 0)] * (x.ndim - 1) + [(MLA_NOPE, LANES - MLA_NOPE - MLA_ROPE)])


def _pair_kv_heads(w):
    g = SWA_HEADS // SWA_KV_HEADS
    return w.reshape(SWA_KV_HEADS, g, HEAD_DIM, -1).transpose(1, 0, 2, 3).reshape(w.shape)


def kernel(x_prompt, x_sample, cache_mla_ckv, cache_mla_krope, cache_diff_k, cache_diff_v, cache_swa_k, cache_swa_v, cache_na_k, cache_na_v, c, c_ctx, w_mod, b_mod, norm_g, w_in0, mla_q_norm, w_uq, mla_kv_norm, w_ukv, diff_lambda, diff_norm, w_out0, w_in1, swa_sink, na_rpb, w_out1, w_router, b_router, w_gu, b_gu, w_down, b_down):
    xs = (x_prompt.reshape(NP_TOK, D_MODEL), x_sample.reshape(NS_TOK, D_MODEL))
    cond = jnp.concatenate([c_ctx[None, :], c, jnp.zeros((16 - 1 - DEC_BATCH, D_MODEL), F32)], axis=0)
    mod = _modulation(cond, w_mod, b_mod).reshape(DEPTH, 16, 6, D_MODEL)
    t64, t32 = _rope_tables()
    states = {}
    for l in range(DEPTH):
        i = l // 2
        g_l = norm_g[l]
        mod_l = mod[l]
        if l % 2 == 0:
            lam_init = 0.8 - 0.6 * math.exp(-0.3 * l)
            wi = w_in0[i]
            w_in_p = jnp.concatenate(
                [wi[:, 0:640], wi[:, 672:2208], _pe_slab(wi[:, 640:672])], axis=1).astype(BF16)
            w_uq_p = _pad_heads(w_uq[i], MLA_HEADS, MLA_NOPE + MLA_ROPE, MLA_NOPE + MLA_ROPE).astype(BF16)
            w_k_p = _pad_heads(w_ukv[i], MLA_HEADS, MLA_NOPE + MLA_V, MLA_NOPE).astype(BF16)
            w_v = w_ukv[i].reshape(MLA_KV_RANK, MLA_HEADS, MLA_NOPE + MLA_V)[:, :, MLA_NOPE:].reshape(
                MLA_KV_RANK, MLA_HEADS * MLA_V).astype(BF16)
            (q, k, v, dq, dk, dv, ckv_st, kpe_st, dk_st, dv_st) = _premix0(
                *xs, mod_l, g_l[0:1], w_in_p, mla_q_norm[i][None, :], w_uq_p, mla_kv_norm[i][None, :],
                w_k_p, w_v, t32, t64)
            states['mla_ckv'] = ckv_st.reshape(BATCH, 1, SEQ, MLA_KV_RANK)
            states['mla_krope'] = kpe_st[:, MLA_NOPE:MLA_NOPE + MLA_ROPE].reshape(BATCH, 1, SEQ, MLA_ROPE)
            states['diff_k'] = dk_st.reshape(BATCH, 1, SEQ, DIFF_HEADS, 2 * DIFF_DH)
            states['diff_v'] = dv_st.reshape(BATCH, 1, SEQ, DIFF_HEADS, 2 * DIFF_DH)
            kc, vc = _mla_cache(cache_mla_ckv[:, i].reshape(DEC_BATCH * PAST_LEN, MLA_KV_RANK),
                                _pe_slab(cache_mla_krope[:, i].reshape(DEC_BATCH * PAST_LEN, MLA_ROPE)),
                                w_k_p, w_v)
            dkc = cache_diff_k[:, i].reshape(DEC_BATCH * PAST_LEN, 512).astype(BF16)
            dvc = cache_diff_v[:, i].reshape(DEC_BATCH * PAST_LEN, 512).astype(BF16)
            lam = diff_lambda[i]
            sub_g = diff_norm[i][None, :]
            o_p = _attn_ab(q, dq, (k, v, dk, dv), None, lam, sub_g, lam_init,
                           n_batch=BATCH, t_len=SEQ, tq=SEQ, tok_off=0)
            o_cat = _attn_ab(q, dq, (k, v, dk, dv), (kc, vc, dkc, dvc), lam, sub_g, lam_init,
                             n_batch=DEC_BATCH, t_len=DEC_SEQ, tq=TQ_AB, tok_off=NP_TOK, out_init=o_p)
            w_out = w_out0[i]
        else:
            wi = w_in1[i]
            n_sq = SWA_HEADS * HEAD_DIM
            w_in_p = jnp.concatenate([_pair_kv_heads(wi[:, :n_sq].T).T, wi[:, n_sq:]], axis=1).astype(BF16)
            (sq, sk, sv, nq, nk, nv, sk_st, sv_st, nk_st, nv_st) = _premix1(xs[0], mod_l, g_l[0:1], w_in_p, t64)
            states['swa_k'] = sk_st.reshape(BATCH, 1, SEQ, SWA_KV_HEADS, HEAD_DIM)
            states['swa_v'] = sv_st.reshape(BATCH, 1, SEQ, SWA_KV_HEADS, HEAD_DIM)
            states['na_k'] = nk_st.reshape(BATCH, 1, SEQ, NA_HEADS, HEAD_DIM)
            states['na_v'] = nv_st.reshape(BATCH, 1, SEQ, NA_HEADS, HEAD_DIM)
            skc = cache_swa_k[:, i].reshape(DEC_BATCH * PAST_LEN, 128).astype(BF16)
            svc = cache_swa_v[:, i].reshape(DEC_BATCH * PAST_LEN, 128).astype(BF16)
            nkc = cache_na_k[:, i].reshape(DEC_BATCH * PAST_LEN, 512).astype(BF16)
            nvc = cache_na_v[:, i].reshape(DEC_BATCH * PAST_LEN, 512).astype(BF16)
            sink = swa_sink[i]
            o_p = _attn_cd_prompt(sink, sq, sk, sv, nq, nk, nv)
            o_cat = _attn_cd_sample(o_p, sink, sq, sk, sv, nq, nk, nv, skc, svc, nkc, nvc, _na_bias(na_rpb[i]))
            wo = w_out1[i]
            w_out = jnp.concatenate([_pair_kv_heads(wo[:n_sq]), wo[n_sq:]], axis=0)
        xs = _ffn(l, o_cat, xs, mod_l, g_l, w_out, w_router[l], b_router[l], w_gu, b_gu, w_down, b_down,
                  split_out=(l == DEPTH - 1))
    return (xs[0].reshape(BATCH, SEQ, D_MODEL), xs[1].reshape(DEC_BATCH, DEC_SEQ, D_MODEL),
            states['mla_ckv'], states['mla_krope'], states['diff_k'], states['diff_v'],
            states['swa_k'], states['swa_v'], states['na_k'], states['na_v'])
```

```python
import functools
import math

import numpy as np
import jax
import jax.numpy as jnp
from jax import lax
from jax.experimental import pallas as pl
from jax.experimental.pallas import tpu as pltpu
from jax.experimental.pallas import tpu_sc as plsc

F32 = jnp.float32
BF16 = jnp.bfloat16

D_MODEL = 1024
BATCH = 16
SEQ = 256
DEPTH = 2
DEC_BATCH = 8
DEC_SEQ = 2048
PAST_LEN = 256
GRID_W = 64
HEAD_DIM = 64
ROPE_THETA = 10000.0
EPS = 1e-6
NEG = -1e30

MLA_HEADS = 8
MLA_Q_RANK = 384
MLA_KV_RANK = 256
MLA_NOPE = 64
MLA_ROPE = 32
MLA_V = 64
DIFF_HEADS = 4
DIFF_DH = 64
SWA_HEADS = 8
SWA_KV_HEADS = 2
SWA_WINDOW = 128
NA_HEADS = 8
NA_WIN_ROWS = 8
NA_WIN_COLS = 16
N_EXPERTS = 32
TOP_K = 4
D_EXPERT = 1024
SWIGLU_LIMIT = 7.0
SWIGLU_ALPHA = 1.702

LANES = 128
NP_TOK = BATCH * SEQ
NS_TOK = DEC_BATCH * DEC_SEQ
N_TOK = NP_TOK + NS_TOK
TM = 512
NPT = NP_TOK // TM
TILES_PER_SAMPLE = DEC_SEQ // TM
N_TILES = N_TOK // TM
N_SUB = 2
SUB_TM = TM // N_SUB
TQ = 256
TQ_AB = 256
MOE_TM = 512
MOE_TAIL_PARTS = 4
MOE_ROWS = ((N_TOK * TOP_K + N_EXPERTS * (MOE_TM - 1)) // MOE_TM + 1) * MOE_TM
MOE_BLOCKS = MOE_ROWS // MOE_TM
NA_TILE_ROWS = TQ // GRID_W
NA_KEY_ROWS = 12
VMEM_LIMIT = 56 * 1024 * 1024


def _cparams(n_axes, vmem=VMEM_LIMIT):
    return pltpu.CompilerParams(dimension_semantics=("arbitrary",) * n_axes,
                                vmem_limit_bytes=vmem)


def _rms(x, g):
    return x * lax.rsqrt(jnp.mean(x * x, axis=-1, keepdims=True) + EPS) * g


def _dot(a, b):
    return jnp.dot(a, b, preferred_element_type=F32)


def _dot_nt(a, b):
    return lax.dot_general(a, b, (((1,), (1,)), ((), ())), preferred_element_type=F32)


def _rope(x, cos, sin_a, sin_b, half):
    return (x * cos + pltpu.roll(x, LANES - half, 1) * sin_a + pltpu.roll(x, half, 1) * sin_b)


def _pipelined_units(units):
    pending = units[0][0](units[0][2])
    for n, (_, finish, arg) in enumerate(units):
        following = units[n + 1][0](units[n + 1][2]) if n + 1 < len(units) else None
        finish(arg, pending)
        pending = following


def _sub_rows(n):
    return slice(SUB_TM * n, SUB_TM * (n + 1))


def _store_head_rows(st_ref, n, hd, n_heads, x):
    st_ref[pl.ds(SUB_TM * n * n_heads + hd, SUB_TM, stride=n_heads), :] = x


def _mod_row(i):
    return jnp.where(i < NPT, 0, 1 + (i - NPT) // TILES_PER_SAMPLE)


def _rope_blk(i):
    return jnp.where(i < NPT, TILES_PER_SAMPLE, (i - NPT) % TILES_PER_SAMPLE)


def _mod_kernel(c_ref, w_ref, b_ref, o_ref):
    c = c_ref[...]
    s = (c * jax.nn.sigmoid(c)).astype(BF16)
    o_ref[0] = _dot(s, w_ref[0].astype(BF16)) + b_ref[0]


def _modulation(cond, w_mod, b_mod):
    nb = 1024
    return pl.pallas_call(
        _mod_kernel,
        grid=(DEPTH, 6 * D_MODEL // nb),
        in_specs=[
            pl.BlockSpec((16, D_MODEL), lambda l, n: (0, 0)),
            pl.BlockSpec((1, D_MODEL, nb), lambda l, n: (l, 0, n)),
            pl.BlockSpec((1, 1, nb), lambda l, n: (l, 0, n)),
        ],
        out_specs=pl.BlockSpec((1, 16, nb), lambda l, n: (l, 0, n)),
        out_shape=jax.ShapeDtypeStruct((DEPTH, 16, 6 * D_MODEL), F32),
        compiler_params=_cparams(2),
        name="modulation",
    )(cond, w_mod, b_mod.reshape(DEPTH, 1, 6 * D_MODEL))


def _rope_tables():
    t = jnp.arange(DEC_SEQ)
    rows = (t // GRID_W).astype(F32)
    cols = (t % GRID_W).astype(F32)

    def angles(r):
        n = r // 4
        inv = ROPE_THETA ** (-jnp.arange(n, dtype=F32) / n)
        return jnp.concatenate([rows[:, None] * inv[None], cols[:, None] * inv[None]], axis=-1)

    def finish(cos, sa, sb):
        ident = (jnp.ones((TM, LANES), F32), jnp.zeros((TM, LANES), F32), jnp.zeros((TM, LANES), F32))
        return tuple(jnp.concatenate([a, b], axis=0) for a, b in zip((cos, sa, sb), ident))

    a64 = angles(64)
    c, s, z = jnp.cos(a64), jnp.sin(a64), jnp.zeros_like(a64)
    t64 = finish(jnp.concatenate([c, c, c, c], -1), jnp.concatenate([-s, z, -s, z], -1),
                 jnp.concatenate([z, s, z, s], -1))
    a32 = angles(32)
    c, s, z = jnp.cos(a32), jnp.sin(a32), jnp.zeros_like(a32)
    one64 = jnp.ones((DEC_SEQ, 64), F32)
    z64 = jnp.zeros((DEC_SEQ, 64), F32)
    z32 = jnp.zeros((DEC_SEQ, 32), F32)
    t32 = finish(jnp.concatenate([one64, c, c, z32], -1), jnp.concatenate([z64, -s, z, z32], -1),
                 jnp.concatenate([z64, z, s, z32], -1))
    return t64, t32


LOG2E = math.log2(math.e)
_DIFF_COLS = DIFF_HEADS * 2 * DIFF_DH
_AB_COLS = tuple(np.cumsum([0, MLA_Q_RANK, MLA_KV_RANK, _DIFF_COLS, _DIFF_COLS, _DIFF_COLS, LANES]).tolist())
_CD_COLS = tuple(np.cumsum([0, SWA_HEADS * HEAD_DIM, SWA_KV_HEADS * HEAD_DIM, SWA_KV_HEADS * HEAD_DIM,
                            NA_HEADS * HEAD_DIM, NA_HEADS * HEAD_DIM, NA_HEADS * HEAD_DIM]).tolist())
IN_COLS = _AB_COLS[-1]
assert IN_COLS == _CD_COLS[-1]
_MLA_SCALE = (MLA_NOPE + MLA_ROPE) ** -0.5 * LOG2E
_QSCALE = HEAD_DIM ** -0.5 * LOG2E


def _premix0_kernel(xp_ref, xs_ref, mod_ref, g_ref, win_ref, qn_ref, wuq_ref, kvn_ref, wk_ref, wv_ref,
                    c32_ref, sa32_ref, sb32_ref, c64_ref, sa64_ref, sb64_ref,
                    q_ref, k_ref, v_ref, dq_ref, dk_ref, dv_ref,
                    ckv_st, kpe_st, dk_st, dv_st):
    i = pl.program_id(0)
    m = mod_ref[0]
    gain = g_ref[...] * (1.0 + m[1:2])

    def project(n):
        r = _sub_rows(n)
        x = jnp.where(i < NPT, xp_ref[r, :], xs_ref[r, :])
        h = _rms(x, gain) + m[0:1]
        return _dot(h.astype(BF16), win_ref[...])

    states = {}

    def finish(n, proj):
        r = _sub_rows(n)
        q_a, kv_a, dq, dk, dv, pe = (proj[:, a:b] for a, b in zip(_AB_COLS[:-1], _AB_COLS[1:]))
        q = _dot(_rms(q_a, qn_ref[...]).astype(BF16), wuq_ref[...])
        ckv = _rms(kv_a, kvn_ref[...])
        ckv_b = ckv.astype(BF16)
        kn = _dot(ckv_b, wk_ref[...])
        v_ref[r, :] = _dot(ckv_b, wv_ref[...]).astype(BF16)
        c32, sa32, sb32 = c32_ref[r, :], sa32_ref[r, :], sb32_ref[r, :]
        c64, sa64, sb64 = c64_ref[r, :], sa64_ref[r, :], sb64_ref[r, :]
        pe_r = _rope(pe, c32, sa32, sb32, MLA_ROPE // 2)
        for hd in range(MLA_HEADS):
            sl = slice(LANES * hd, LANES * (hd + 1))
            q_ref[r, sl] = (_rope(q[:, sl], c32, sa32, sb32, MLA_ROPE // 2) * _MLA_SCALE).astype(BF16)
            k_ref[r, sl] = (kn[:, sl] + pe_r).astype(BF16)
        for hd in range(DIFF_HEADS):
            sl = slice(LANES * hd, LANES * (hd + 1))
            dq_ref[r, sl] = (_rope(dq[:, sl], c64, sa64, sb64, DIFF_DH // 2) * _QSCALE).astype(BF16)
            dk_ref[r, sl] = _rope(dk[:, sl], c64, sa64, sb64, DIFF_DH // 2).astype(BF16)
        dv_ref[r, :] = dv.astype(BF16)
        states[n] = (ckv, pe, dk, dv)

    _pipelined_units([(project, finish, n) for n in range(N_SUB)])

    @pl.when(i < NPT)
    def _():
        for n in range(N_SUB):
            r = _sub_rows(n)
            ckv, pe, dk, dv = states[n]
            ckv_st[r, :] = ckv
            kpe_st[r, :] = pe
            for hd in range(DIFF_HEADS):
                _store_head_rows(dk_st, n, hd, DIFF_HEADS, dk[:, LANES * hd: LANES * (hd + 1)])
                _store_head_rows(dv_st, n, hd, DIFF_HEADS, dv[:, LANES * hd: LANES * (hd + 1)])


def _premix1_kernel(x_ref, mod_ref, g_ref, win_ref, c64_ref, sa64_ref, sb64_ref,
                    sq_ref, sk_ref, sv_ref, nq_ref, nk_ref, nv_ref,
                    sk_st, sv_st, nk_st, nv_st):
    i = pl.program_id(0)
    m = mod_ref[0]
    gain = g_ref[...] * (1.0 + m[1:2])

    def project(n):
        h = _rms(x_ref[_sub_rows(n), :], gain) + m[0:1]
        return _dot(h.astype(BF16), win_ref[...])

    states = {}

    def finish(n, proj):
        r = _sub_rows(n)
        sq, sk, sv, nq, nk, nv = (proj[:, a:b] for a, b in zip(_CD_COLS[:-1], _CD_COLS[1:]))
        c64, sa64, sb64 = c64_ref[r, :], sa64_ref[r, :], sb64_ref[r, :]
        for hd in range(4):
            sl = slice(LANES * hd, LANES * (hd + 1))
            sq_ref[r, sl] = (_rope(sq[:, sl], c64, sa64, sb64, HEAD_DIM // 2) * _QSCALE).astype(BF16)
        sk_ref[r, :] = _rope(sk, c64, sa64, sb64, HEAD_DIM // 2).astype(BF16)
        sv_ref[r, :] = sv.astype(BF16)
        nq_ref[r, :] = (nq * _QSCALE).astype(BF16)
        nk_ref[r, :] = nk.astype(BF16)
        nv_ref[r, :] = nv.astype(BF16)
        states[n] = (sk, sv, nk, nv)

    _pipelined_units([(project, finish, n) for n in range(N_SUB)])

    @pl.when(i < NPT)
    def _():
        for n in range(N_SUB):
            r = _sub_rows(n)
            sk, sv, nk, nv = states[n]
            sk_st[r, :] = sk
            sv_st[r, :] = sv
            for hd in range(NA_HEADS):
                _store_head_rows(nk_st, n, hd, NA_HEADS, nk[:, HEAD_DIM * hd: HEAD_DIM * (hd + 1)])
                _store_head_rows(nv_st, n, hd, NA_HEADS, nv[:, HEAD_DIM * hd: HEAD_DIM * (hd + 1)])


def _tok_spec(width):
    return pl.BlockSpec((TM, width), lambda i: (i, 0))


_PROMPT_SPEC = pl.BlockSpec((TM, D_MODEL), lambda i: (jnp.minimum(i, NPT - 1), 0))
_SAMPLE_SPEC = pl.BlockSpec((TM, D_MODEL), lambda i: (jnp.maximum(i - NPT, 0), 0))


def _state_spec(width, rows_per_token=1):
    return pl.BlockSpec((TM * rows_per_token, width), lambda i: (jnp.minimum(i, NPT - 1), 0))


def _const_spec(shape):
    return pl.BlockSpec(shape, lambda i: (0,) * len(shape))


_MOD_SPEC = pl.BlockSpec((1, 6, D_MODEL), lambda i: (_mod_row(i), 0, 0))
_ROPE_SPEC = pl.BlockSpec((TM, LANES), lambda i: (_rope_blk(i), 0))


def _premix0(xp, xs, mod_l, g0, w_in_p, q_norm, w_uq_p, kv_norm, w_k_p, w_v, t32, t64):
    outs = [(N_TOK, 1024, BF16), (N_TOK, 1024, BF16), (N_TOK, 512, BF16), (N_TOK, 512, BF16),
            (N_TOK, 512, BF16), (N_TOK, 512, BF16),
            (NP_TOK, 256, F32), (NP_TOK, 128, F32),
            (NP_TOK * DIFF_HEADS, 2 * DIFF_DH, F32), (NP_TOK * DIFF_HEADS, 2 * DIFF_DH, F32)]
    return pl.pallas_call(
        _premix0_kernel,
        grid=(N_TILES,),
        in_specs=[_PROMPT_SPEC, _SAMPLE_SPEC, _MOD_SPEC, _const_spec((1, D_MODEL)),
                  _const_spec((D_MODEL, IN_COLS)), _const_spec((1, MLA_Q_RANK)),
                  _const_spec((MLA_Q_RANK, 1024)), _const_spec((1, MLA_KV_RANK)),
                  _const_spec((MLA_KV_RANK, 1024)), _const_spec((MLA_KV_RANK, 512))]
                 + [_ROPE_SPEC] * 6,
        out_specs=([_tok_spec(w) for (_, w, _) in outs[:6]]
                   + [_state_spec(w, n // NP_TOK) for (n, w, _) in outs[6:]]),
        out_shape=[jax.ShapeDtypeStruct((n, w), dt) for (n, w, dt) in outs],
        compiler_params=_cparams(1),
        name="premix_ab",
    )(xp, xs, mod_l, g0, w_in_p, q_norm, w_uq_p, kv_norm, w_k_p, w_v, *t32, *t64)


def _premix1(x, mod_l, g0, w_in_p, t64):
    outs = [(N_TOK, 512, BF16), (N_TOK, 128, BF16), (N_TOK, 128, BF16), (N_TOK, 512, BF16),
            (N_TOK, 512, BF16), (N_TOK, 512, BF16),
            (NP_TOK, 128, F32), (NP_TOK, 128, F32),
            (NP_TOK * NA_HEADS, HEAD_DIM, F32), (NP_TOK * NA_HEADS, HEAD_DIM, F32)]
    return pl.pallas_call(
        _premix1_kernel,
        grid=(N_TILES,),
        in_specs=[_tok_spec(D_MODEL), _MOD_SPEC, _const_spec((1, D_MODEL)),
                  _const_spec((D_MODEL, IN_COLS))] + [_ROPE_SPEC] * 3,
        out_specs=([_tok_spec(w) for (_, w, _) in outs[:6]]
                   + [_state_spec(w, n // NP_TOK) for (n, w, _) in outs[6:]]),
        out_shape=[jax.ShapeDtypeStruct((n, w), dt) for (n, w, dt) in outs],
        compiler_params=_cparams(1),
        name="premix_cd",
    )(x, mod_l, g0, w_in_p, *t64)


def _mla_cache_kernel(ckv_ref, pe_ref, wk_ref, wv_ref, k_ref, v_ref):
    c = ckv_ref[...].astype(BF16)
    kn = _dot(c, wk_ref[...])
    v_ref[...] = _dot(c, wv_ref[...]).astype(BF16)
    pe = pe_ref[...]
    for hd in range(MLA_HEADS):
        sl = slice(LANES * hd, LANES * (hd + 1))
        k_ref[:, sl] = (kn[:, sl] + pe).astype(BF16)


def _mla_cache(ckv, pe_slab, w_k_p, w_v):
    n = ckv.shape[0]
    tm = 512
    return pl.pallas_call(
        _mla_cache_kernel,
        grid=(n // tm,),
        in_specs=[pl.BlockSpec((tm, MLA_KV_RANK), lambda i: (i, 0)),
                  pl.BlockSpec((tm, LANES), lambda i: (i, 0)),
                  _const_spec((MLA_KV_RANK, 1024)), _const_spec((MLA_KV_RANK, 512))],
        out_specs=[pl.BlockSpec((tm, 1024), lambda i: (i, 0)), pl.BlockSpec((tm, 512), lambda i: (i, 0))],
        out_shape=[jax.ShapeDtypeStruct((n, 1024), BF16), jax.ShapeDtypeStruct((n, 512), BF16)],
        compiler_params=_cparams(1),
        name="mla_cache",
    )(ckv, pe_slab, w_k_p, w_v)


def _softmax_pv(scores, values, sink=None):
    m = jnp.max(scores[0], axis=-1, keepdims=True)
    for s in scores[1:]:
        m = jnp.maximum(m, jnp.max(s, axis=-1, keepdims=True))
    if sink is not None:
        m = jnp.maximum(m, sink)
    l = None
    o = None
    for s, v in zip(scores, values):
        p = jnp.exp2(s - m)
        ls = jnp.sum(p, axis=-1, keepdims=True)
        os_ = _dot(p.astype(BF16), v)
        l = ls if l is None else l + ls
        o = os_ if o is None else o + os_
    if sink is not None:
        l = l + jnp.exp2(sink - m)
    return o * (1.0 / l)


def _lane_lo(shape):
    return lax.broadcasted_iota(jnp.int32, shape, 1) < (LANES // 2)


def _split_halves(qb):
    lo = _lane_lo(qb.shape)
    zero = jnp.zeros_like(qb)
    return jnp.where(lo, qb, zero), jnp.where(lo, zero, qb)


def _attn_ab_kernel(*refs, n_pieces, lam_init, aliased):
    if aliased:
        refs = refs[1:]
    q_ref, dq_ref = refs[0], refs[1]
    pieces = [refs[2 + 4 * p: 6 + 4 * p] for p in range(n_pieces)]
    lam_ref, subg_ref, o_ref = refs[2 + 4 * n_pieces:]
    lam = lam_ref[...]
    lam_full = (jnp.exp(jnp.sum(lam[0:1] * lam[1:2], axis=-1, keepdims=True))
                - jnp.exp(jnp.sum(lam[2:3] * lam[3:4], axis=-1, keepdims=True)) + lam_init)
    lo = _lane_lo((q_ref.shape[0], LANES))
    subg = subg_ref[...] * (1.0 - lam_init)
    tq = q_ref.shape[0]
    def mla_scores(hd):
        sl = slice(LANES * hd, LANES * (hd + 1))
        qh = q_ref[:, sl]
        return [_dot_nt(qh, k_ref[:, sl]) for (k_ref, _, _, _) in pieces]

    def diff_scores(hd):
        sl = slice(LANES * hd, LANES * (hd + 1))
        qq = jnp.concatenate(_split_halves(dq_ref[:, sl]), axis=0)
        return [_dot_nt(qq, dk_ref[:, sl]) for (_, _, dk_ref, _) in pieces]

    pair = {}

    def mla_finish(hd, scores):
        j = hd // 2
        vals = [v_ref[:, LANES * j: LANES * (j + 1)] for (_, v_ref, _, _) in pieces]
        pair[hd % 2] = _softmax_pv(scores, vals)
        if hd % 2 == 1:
            o_ref[:, LANES * j: LANES * (j + 1)] = jnp.where(lo, pair[0], pair[1]).astype(BF16)

    def diff_finish(hd, scores):
        sl = slice(LANES * hd, LANES * (hd + 1))
        oo = _softmax_pv(scores, [dv_ref[:, sl] for (_, _, _, dv_ref) in pieces])
        od = _rms(oo[:tq] - lam_full * oo[tq:], subg)
        o_ref[:, 512 + LANES * hd: 512 + LANES * (hd + 1)] = od.astype(BF16)

    units = []
    for j in range(DIFF_HEADS):
        units += [(diff_scores, diff_finish, j), (mla_scores, mla_finish, 2 * j),
                  (mla_scores, mla_finish, 2 * j + 1)]
    _pipelined_units(units)


def _attn_ab(q, dq, new_kv, cache_kv, lam, sub_g, lam_init, *, n_batch, t_len, tq, tok_off, out_init=None):
    nq = t_len // tq
    q_off = tok_off // tq
    b_off = tok_off // t_len
    widths = (1024, 512, 512, 512)
    in_specs = [pl.BlockSpec((tq, 1024), lambda b, i: (q_off + b * nq + i, 0)),
                pl.BlockSpec((tq, 512), lambda b, i: (q_off + b * nq + i, 0))]
    args = [q, dq]
    for w, a in zip(widths, new_kv):
        in_specs.append(pl.BlockSpec((t_len, w), lambda b, i: (b_off + b, 0)))
        args.append(a)
    n_pieces = 1
    if cache_kv is not None:
        n_pieces = 2
        for w, a in zip(widths, cache_kv):
            in_specs.append(pl.BlockSpec((PAST_LEN, w), lambda b, i: (b, 0)))
            args.append(a)
    in_specs += [pl.BlockSpec((4, DIFF_DH), lambda b, i: (0, 0)),
                 pl.BlockSpec((1, 2 * DIFF_DH), lambda b, i: (0, 0))]
    args += [lam, sub_g]
    aliases = {}
    if out_init is not None:
        in_specs = [pl.BlockSpec(memory_space=pl.ANY)] + in_specs
        args = [out_init] + args
        aliases = {0: 0}
    return pl.pallas_call(
        functools.partial(_attn_ab_kernel, n_pieces=n_pieces, lam_init=lam_init, aliased=out_init is not None),
        grid=(n_batch, nq),
        in_specs=in_specs,
        out_specs=pl.BlockSpec((tq, 1024), lambda b, i: (q_off + b * nq + i, 0)),
        out_shape=jax.ShapeDtypeStruct((N_TOK, 1024), BF16),
        input_output_aliases=aliases,
        compiler_params=_cparams(2),
        name="attn_ab_%d" % n_pieces,
    )(*args)


def _gqa_stacks(sq_ref, sink_ref):
    tq = sq_ref.shape[0]
    halves = [_split_halves(sq_ref[:, LANES * j: LANES * (j + 1)]) for j in range(4)]
    q_stacks = [jnp.concatenate([halves[j][kvh] for j in range(4)], axis=0) for kvh in range(SWA_KV_HEADS)]
    sinks = [jnp.concatenate([jnp.full((tq, 1), sink_ref[4 * kvh + j] * LOG2E, F32) for j in range(4)], axis=0)
             for kvh in range(SWA_KV_HEADS)]
    return q_stacks, sinks


def _attn_cd_prompt_kernel(sink_ref, sq_ref, sk_ref, sv_ref, nq_ref, nk_ref, nv_ref, o_ref):
    tq = sq_ref.shape[0]
    lo = _lane_lo((tq, LANES))
    sk = sk_ref[...]
    sv = sv_ref[...]
    for j in range(4):
        sl = slice(LANES * j, LANES * (j + 1))
        q_lo, q_hi = _split_halves(sq_ref[:, sl])
        o_lo = _softmax_pv([_dot_nt(q_lo, sk)], [sv], sink=sink_ref[j] * LOG2E)
        o_hi = _softmax_pv([_dot_nt(q_hi, sk)], [sv], sink=sink_ref[j + 4] * LOG2E)
        o_ref[:, sl] = jnp.where(lo, o_lo, o_hi).astype(BF16)
    for j in range(4):
        sl = slice(LANES * j, LANES * (j + 1))
        q_lo, q_hi = _split_halves(nq_ref[:, sl])
        k = nk_ref[:, sl]
        v = nv_ref[:, sl]
        o_lo = _softmax_pv([_dot_nt(q_lo, k)], [v])
        o_hi = _softmax_pv([_dot_nt(q_hi, k)], [v])
        o_ref[:, 512 + LANES * j: 512 + LANES * (j + 1)] = jnp.where(lo, o_lo, o_hi).astype(BF16)


def _attn_cd_prompt(sink, sq, sk, sv, nq, nk, nv):
    def spec(w):
        return pl.BlockSpec((SEQ, w), lambda b: (b, 0))
    return pl.pallas_call(
        _attn_cd_prompt_kernel,
        grid=(BATCH,),
        in_specs=[pl.BlockSpec(memory_space=pltpu.SMEM), spec(512), spec(128), spec(128),
                  spec(512), spec(512), spec(512)],
        out_specs=spec(1024),
        out_shape=jax.ShapeDtypeStruct((N_TOK, 1024), BF16),
        compiler_params=_cparams(1),
        name="attn_cd_prompt",
    )(sink, sq, sk, sv, nq, nk, nv)


_SWA_KEYS = TQ + 2 * SWA_WINDOW


def _attn_cd_sample_kernel(init_ref, sink_ref, sq_ref, nq_ref, sk_ref, sv_ref, nk_ref, nv_ref,
                           skc_ref, svc_ref, nkc_ref, nvc_ref, bias_ref, o_ref):
    del init_ref
    qi = pl.program_id(1)
    lo = _lane_lo((TQ, LANES))
    ks = pl.multiple_of(jnp.clip(qi * TQ - SWA_WINDOW, 0, DEC_SEQ - _SWA_KEYS), SWA_WINDOW)
    k_win = sk_ref[pl.ds(ks, _SWA_KEYS), :]
    v_win = sv_ref[pl.ds(ks, _SWA_KEYS), :]
    q_pos = qi * TQ + (lax.broadcasted_iota(jnp.int32, (4 * TQ, _SWA_KEYS), 0) & (TQ - 1))
    k_pos = ks + lax.broadcasted_iota(jnp.int32, (4 * TQ, _SWA_KEYS), 1)
    in_win = jnp.abs(q_pos - k_pos) <= SWA_WINDOW
    skc = skc_ref[...]
    svc = svc_ref[...]
    q_stacks, sinks = _gqa_stacks(sq_ref, sink_ref)

    def window_scores(kvh):
        return [_dot_nt(q_stacks[kvh], skc), jnp.where(in_win, _dot_nt(q_stacks[kvh], k_win), NEG)]

    o_kv = {}

    def window_finish(kvh, scores):
        o_kv[kvh] = _softmax_pv(scores, [svc, v_win], sink=sinks[kvh])
        if kvh == SWA_KV_HEADS - 1:
            for j in range(4):
                rows = slice(TQ * j, TQ * (j + 1))
                o_ref[:, LANES * j: LANES * (j + 1)] = jnp.where(lo, o_kv[0][rows], o_kv[1][rows]).astype(BF16)

    n_rows = DEC_SEQ // GRID_W
    r0 = jnp.clip(qi * NA_TILE_ROWS - NA_WIN_ROWS // 2, 0, n_rows - NA_KEY_ROWS)
    kn = pl.multiple_of(r0 * GRID_W, GRID_W)

    lane_lo = _lane_lo((1, LANES))
    pieces = []
    for ri in range(NA_TILE_ROWS):
        r = qi * NA_TILE_ROWS + ri
        rs = jnp.clip(r - NA_WIN_ROWS // 2, 0, n_rows - NA_WIN_ROWS)
        row = []
        for mm in range(NA_KEY_ROWS // 2):
            kr = r0 + 2 * mm
            ok = [(kr + t >= rs) & (kr + t < rs + NA_WIN_ROWS) for t in range(2)]
            mask = jnp.where(lane_lo, jnp.where(ok[0], 0.0, NEG), jnp.where(ok[1], 0.0, NEG))
            row.append((jnp.clip(kr - r + NA_WIN_ROWS, 0, 2 * NA_WIN_ROWS - 1), mask))
        pieces.append(row)

    def add_na_bias(hd, s):
        return jnp.concatenate(
            [jnp.concatenate([s[GRID_W * ri: GRID_W * (ri + 1), LANES * mm: LANES * (mm + 1)] + bias_ref[hd, d] + mask
                              for mm, (d, mask) in enumerate(row)], axis=1)
             for ri, row in enumerate(pieces)], axis=0)

    def na_scores(hd):
        sl = slice(LANES * (hd // 2), LANES * (hd // 2 + 1))
        q_half = _split_halves(nq_ref[:, sl])[hd % 2]
        nk_win = nk_ref[pl.ds(kn, NA_KEY_ROWS * GRID_W), sl]
        return [_dot_nt(q_half, nkc_ref[:, sl]), add_na_bias(hd, _dot_nt(q_half, nk_win))]

    pair = {}

    def na_finish(hd, scores):
        j = hd // 2
        sl = slice(LANES * j, LANES * (j + 1))
        nv_win = nv_ref[pl.ds(kn, NA_KEY_ROWS * GRID_W), sl]
        pair[hd % 2] = _softmax_pv(scores, [nvc_ref[:, sl], nv_win])
        if hd % 2 == 1:
            o_ref[:, 512 + LANES * j: 512 + LANES * (j + 1)] = jnp.where(lo, pair[0], pair[1]).astype(BF16)

    _pipelined_units([(window_scores, window_finish, kvh) for kvh in range(SWA_KV_HEADS)]
                     + [(na_scores, na_finish, hd) for hd in range(NA_HEADS)])


def _attn_cd_sample(out_init, sink, sq, sk, sv, nq, nk, nv, skc, svc, nkc, nvc, bias):
    nq_t = DEC_SEQ // TQ
    q_off = NP_TOK // TQ
    b_off = NP_TOK // DEC_SEQ

    def qspec(w):
        return pl.BlockSpec((TQ, w), lambda b, i: (q_off + b * nq_t + i, 0))

    def kspec(w):
        return pl.BlockSpec((DEC_SEQ, w), lambda b, i: (b_off + b, 0))

    def cspec(w):
        return pl.BlockSpec((PAST_LEN, w), lambda b, i: (b, 0))

    bias_spec = pl.BlockSpec((NA_HEADS, 2 * NA_WIN_ROWS, GRID_W, LANES), lambda b, i: (0, 0, 0, 0))
    return pl.pallas_call(
        _attn_cd_sample_kernel,
        grid=(DEC_BATCH, nq_t),
        in_specs=[pl.BlockSpec(memory_space=pl.ANY), pl.BlockSpec(memory_space=pltpu.SMEM),
                  qspec(512), qspec(512), kspec(128), kspec(128), kspec(512), kspec(512),
                  cspec(128), cspec(128), cspec(512), cspec(512), bias_spec],
        out_specs=pl.BlockSpec((TQ, 1024), lambda b, i: (q_off + b * nq_t + i, 0)),
        out_shape=jax.ShapeDtypeStruct((N_TOK, 1024), BF16),
        input_output_aliases={0: 0},
        compiler_params=_cparams(2),
        name="attn_cd_sample",
    )(out_init, sink, sq, nq, sk, sv, nk, nv, skc, svc, nkc, nvc, bias)


def _na_bias(rpb):
    n_dc = 2 * NA_WIN_COLS - 1
    c = np.arange(GRID_W)[:, None]
    kc = np.arange(GRID_W)[None, :]
    qs = np.clip(c - NA_WIN_COLS // 2, 0, GRID_W - NA_WIN_COLS)
    col_ok = (kc >= qs) & (kc < qs + NA_WIN_COLS)
    dc = np.clip(kc - c + NA_WIN_COLS - 1, 0, n_dc - 1)
    onehot = ((dc[None] == np.arange(n_dc)[:, None, None]) & col_ok[None]).astype(np.float32)
    blocks = jnp.einsum('hrd,dck->hrck', rpb.astype(F32) * LOG2E, onehot, precision=lax.Precision.HIGHEST)
    blocks = jnp.where(col_ok[None, None], blocks, NEG)
    none = jnp.full((NA_HEADS, 1, GRID_W, GRID_W), NEG, F32)
    return jnp.concatenate([jnp.concatenate([none, blocks], axis=1),
                            jnp.concatenate([blocks, none], axis=1)], axis=-1)


_HI_MASK = -65536


def _pack_pairs(x):
    w = x.shape[1] // 2
    r = x.astype(BF16).astype(F32)
    lo = lax.bitcast_convert_type(r[:, :w], jnp.int32)
    hi = lax.bitcast_convert_type(r[:, w:], jnp.int32)
    return (hi & _HI_MASK) | lax.shift_right_logical(lo, 16)


def _unpack_pairs(p):
    lo = lax.bitcast_convert_type(lax.shift_left(p, 16), F32)
    hi = lax.bitcast_convert_type(p & _HI_MASK, F32)
    return lo, hi


def _postmix_kernel(*refs, split_x):
    if split_x:
        o_ref, xp_ref, xs_ref = refs[:3]
        refs = refs[3:]
    else:
        o_ref, x_ref = refs[:2]
        refs = refs[2:]
    (mod_ref, g_ref, wout_ref, wr_ref, br_ref, tri_ref,
     x1_ref, h2_ref, route_ref, gate_ref, cnt_ref, run_ref) = refs
    i = pl.program_id(0)

    @pl.when(i == 0)
    def _():
        run_ref[...] = jnp.zeros_like(run_ref)

    m = mod_ref[0]
    g = g_ref[...]
    gate_gain = g[1:2] * m[2:3]
    ffn_gain = g[2:3] * (1.0 + m[4:5])
    sub_logits = {}

    def project(n):
        return _dot(o_ref[_sub_rows(n), :], wout_ref[...])

    def finish(n, y):
        r = _sub_rows(n)
        x = jnp.where(i < NPT, xp_ref[r, :], xs_ref[r, :]) if split_x else x_ref[r, :]
        x1 = x + _rms(y, gate_gain)
        x1_ref[r, :] = x1
        h2 = _rms(x1, ffn_gain) + m[3:4]
        h2_ref[r, :] = _pack_pairs(h2)
        sub_logits[n] = _dot(h2.astype(BF16), wr_ref[...]) + br_ref[...]

    _pipelined_units([(project, finish, n) for n in range(N_SUB)])

    logits = jnp.concatenate([sub_logits[n] for n in range(N_SUB)], axis=0)
    lane = lax.broadcasted_iota(jnp.int32, logits.shape, 1).astype(F32)
    cur = jnp.where(lane < N_EXPERTS, logits, -jnp.inf)
    tops, idxs = [], []
    for _ in range(TOP_K):
        mx = jnp.max(cur, axis=-1, keepdims=True)
        ix = jnp.min(jnp.where(cur == mx, lane, float(LANES)), axis=-1, keepdims=True)
        tops.append(mx)
        idxs.append(ix)
        cur = jnp.where(lane == ix, -jnp.inf, cur)
    es = [jnp.exp(t - tops[0]) for t in tops]
    inv = 1.0 / (es[0] + es[1] + es[2] + es[3])
    picked = jnp.zeros_like(logits)
    for k in range(TOP_K):
        picked = jnp.where(lane == idxs[k], 1.0, picked)
    before = _dot(tri_ref[...], picked.astype(BF16)) + run_ref[0:1, :]
    route = jnp.zeros_like(logits)
    gate_out = jnp.zeros_like(logits)
    for k in range(TOP_K):
        rank = jnp.sum(jnp.where(lane == idxs[k], before, 0.0), axis=-1, keepdims=True)
        route = jnp.where(lane == float(k), idxs[k], route)
        route = jnp.where(lane == float(TOP_K + k), rank, route)
        gate_out = jnp.where(lane == float(k), es[k] * inv, gate_out)
    route_ref[...] = route.T[:2 * TOP_K].astype(jnp.int32)
    gate_ref[...] = gate_out
    run_ref[...] = run_ref[...] + jnp.sum(picked, axis=0, keepdims=True)
    cnt_ref[...] = run_ref[...].astype(jnp.int32)


def _postmix(o_cat, xs, mod_l, g_l, w_out, w_r, b_r):
    tri = jnp.asarray(np.tril(np.ones((TM, TM), np.float32), -1), BF16)
    split_x = len(xs) == 2
    x_specs = [_PROMPT_SPEC, _SAMPLE_SPEC] if split_x else [_tok_spec(D_MODEL)]
    return pl.pallas_call(
        functools.partial(_postmix_kernel, split_x=split_x),
        grid=(N_TILES,),
        in_specs=[_tok_spec(1024)] + x_specs + [_MOD_SPEC, _const_spec((4, D_MODEL)),
                  _const_spec((1024, D_MODEL)), _const_spec((D_MODEL, LANES)), _const_spec((1, LANES)),
                  _const_spec((TM, TM))],
        out_specs=[_tok_spec(D_MODEL), _tok_spec(D_MODEL // 2), pl.BlockSpec((2 * TOP_K, TM), lambda i: (0, i)),
                   _tok_spec(LANES), _const_spec((8, LANES))],
        out_shape=[jax.ShapeDtypeStruct((N_TOK, D_MODEL), F32),
                   jax.ShapeDtypeStruct((N_TOK, D_MODEL // 2), jnp.int32),
                   jax.ShapeDtypeStruct((2 * TOP_K, N_TOK), jnp.int32), jax.ShapeDtypeStruct((N_TOK, LANES), F32),
                   jax.ShapeDtypeStruct((8, LANES), jnp.int32)],
        scratch_shapes=[pltpu.VMEM((8, LANES), F32)],
        compiler_params=_cparams(1),
        name="postmix",
    )(o_cat, *xs, mod_l, g_l, w_out, w_r, b_r, tri)


SC_WORKERS = 32
SC_ROWS = 64
ROW_WORDS = D_MODEL // 2

_SC_SCRATCH = [pltpu.VMEM((SC_ROWS, ROW_WORDS), jnp.int32), pltpu.VMEM((SC_ROWS, ROW_WORDS), jnp.int32),
               pltpu.SemaphoreType.DMA, pltpu.SemaphoreType.DMA, pltpu.SemaphoreType.DMA, pltpu.SemaphoreType.DMA]


def _sc_worker_id():
    return lax.axis_index("s") * 2 + lax.axis_index("c")


def _sc_double_buffered(n_chunks, load, store):
    for cp in load(0, 0):
        cp.start()

    @pl.loop(0, n_chunks, step=2)
    def _(g0):
        for b in range(2):
            g = g0 + b
            for cp in load(g, b):
                cp.wait()

            @pl.when(g >= 1)
            def _():
                for cp in store(g - 1, 1 - b):
                    cp.wait()

            @pl.when(g + 1 < n_chunks)
            def _():
                for cp in load(g + 1, 1 - b):
                    cp.start()

            for cp in store(g, b):
                cp.start()

    for cp in store(n_chunks - 1, (n_chunks - 1) % 2):
        cp.wait()


def _sc_dispatch(src, idx):
    n_chunks = N_TOK // (SC_WORKERS * SC_ROWS)
    assert n_chunks % 2 == 0
    mesh = plsc.VectorSubcoreMesh(core_axis_name="c", subcore_axis_name="s")

    @functools.partial(
        pl.kernel, mesh=mesh,
        out_type=jax.ShapeDtypeStruct((MOE_ROWS, ROW_WORDS), jnp.int32),
        scratch_types=[pltpu.VMEM((n_chunks * TOP_K, SC_ROWS), jnp.int32)] + _SC_SCRATCH)
    def k(src_hbm, idx_hbm, out_hbm, idx_v, buf0, buf1, in0, in1, out0, out1):
        wid = _sc_worker_id()
        pltpu.sync_copy(idx_hbm.at[wid], idx_v)
        bufs, in_sems, out_sems = (buf0, buf1), (in0, in1), (out0, out1)

        def load(g, b):
            rows = pl.ds((wid * n_chunks + g) * SC_ROWS, SC_ROWS)
            return [pltpu.make_async_copy(src_hbm.at[rows], bufs[b], in_sems[b])]

        def store(g, b):
            return [pltpu.make_async_copy(bufs[b], out_hbm.at[idx_v.at[g * TOP_K + kk]], out_sems[b])
                    for kk in range(TOP_K)]

        _sc_double_buffered(n_chunks, load, store)

    return k(src, idx)


def _sc_collect(table, idx):
    n_chunks = idx.shape[1]
    assert n_chunks % 2 == 0
    mesh = plsc.VectorSubcoreMesh(core_axis_name="c", subcore_axis_name="s")

    @functools.partial(
        pl.kernel, mesh=mesh,
        out_type=jax.ShapeDtypeStruct((SC_WORKERS * n_chunks * SC_ROWS, ROW_WORDS), jnp.int32),
        scratch_types=[pltpu.VMEM((n_chunks, SC_ROWS), jnp.int32)] + _SC_SCRATCH)
    def k(table_hbm, idx_hbm, out_hbm, idx_v, buf0, buf1, in0, in1, out0, out1):
        wid = _sc_worker_id()
        pltpu.sync_copy(idx_hbm.at[wid], idx_v)
        bufs, in_sems, out_sems = (buf0, buf1), (in0, in1), (out0, out1)

        def load(g, b):
            return [pltpu.make_async_copy(table_hbm.at[idx_v.at[g]], bufs[b], in_sems[b])]

        def store(g, b):
            rows = pl.ds((wid * n_chunks + g) * SC_ROWS, SC_ROWS)
            return [pltpu.make_async_copy(bufs[b], out_hbm.at[rows], out_sems[b])]

        _sc_double_buffered(n_chunks, load, store)

    return k(table, idx)


def _expert_rows(words, n_valid, wgu_b, wd_b, bgu, bd):
    live = lax.broadcasted_iota(jnp.int32, words.shape, 0) < n_valid
    lo, hi = _unpack_pairs(jnp.where(live, words, 0))
    x = jnp.concatenate([lo, hi], axis=1).astype(BF16)
    gu = _dot(x, wgu_b[...]) + bgu
    g = jnp.minimum(gu[:, :D_EXPERT], SWIGLU_LIMIT)
    u = jnp.clip(gu[:, D_EXPERT:], -SWIGLU_LIMIT, SWIGLU_LIMIT)
    a = g * jax.nn.sigmoid(SWIGLU_ALPHA * g) * (u + 1.0)
    return _pack_pairs(_dot(a.astype(BF16), wd_b[...]) + bd)


def _moe_kernel(blk_e_ref, blk_first_ref, blk_rows_ref, blk_slot_ref, blk_next_ref,
                x_ref, wgu_hbm, bgu_ref, wd_hbm, bd_ref, y_ref,
                wgu_f, wd_f, wgu_b, wd_b, sem, *, layer):
    i = pl.program_id(0)
    n_valid = blk_rows_ref[i]
    quantum = MOE_TM // MOE_TAIL_PARTS

    def weight_copies(e, slot):
        return (pltpu.make_async_copy(wgu_hbm.at[layer, e], wgu_f.at[slot], sem.at[0, slot]),
                pltpu.make_async_copy(wd_hbm.at[layer, e], wd_f.at[slot], sem.at[1, slot]))

    @pl.when(i == 0)
    def _():
        for cp in weight_copies(blk_e_ref[0], blk_slot_ref[0]):
            cp.start()

    @pl.when(blk_first_ref[i] == 1)
    def _():
        slot = blk_slot_ref[i]
        for cp in weight_copies(blk_e_ref[i], slot):
            cp.wait()
        nxt = blk_next_ref[i]

        @pl.when(nxt >= 0)
        def _():
            for cp in weight_copies(nxt, 1 - slot):
                cp.start()

        wgu_b[...] = wgu_f[slot].astype(BF16)
        wd_b[...] = wd_f[slot].astype(BF16)

    for parts in range(1, MOE_TAIL_PARTS + 1):
        rows = parts * quantum

        @pl.when((n_valid > rows - quantum) & (n_valid <= rows))
        def _(rows=rows):
            y_ref[:rows] = _expert_rows(x_ref[:rows], n_valid, wgu_b, wd_b, bgu_ref[0, 0], bd_ref[0, 0])
            if rows < MOE_TM:
                y_ref[rows:] = jnp.zeros((MOE_TM - rows, ROW_WORDS), jnp.int32)

    @pl.when(n_valid == 0)
    def _():
        y_ref[...] = jnp.zeros_like(y_ref)


def _moe(layer, blk_meta, xs, w_gu, b_gu, w_down, b_down):
    def row_map(i, *_):
        return (i, 0)

    def bias_map(i, e, *_):
        return (layer, e[i], 0, 0)

    grid_spec = pltpu.PrefetchScalarGridSpec(
        num_scalar_prefetch=5,
        grid=(MOE_BLOCKS,),
        in_specs=[
            pl.BlockSpec((MOE_TM, ROW_WORDS), row_map),
            pl.BlockSpec(memory_space=pl.ANY),
            pl.BlockSpec((1, 1, 1, 2 * D_EXPERT), bias_map),
            pl.BlockSpec(memory_space=pl.ANY),
            pl.BlockSpec((1, 1, 1, D_MODEL), bias_map),
        ],
        out_specs=pl.BlockSpec((MOE_TM, ROW_WORDS), row_map),
        scratch_shapes=[pltpu.VMEM((2, D_MODEL, 2 * D_EXPERT), F32), pltpu.VMEM((2, D_EXPERT, D_MODEL), F32),
                        pltpu.VMEM((D_MODEL, 2 * D_EXPERT), BF16), pltpu.VMEM((D_EXPERT, D_MODEL), BF16),
                        pltpu.SemaphoreType.DMA((2, 2))],
    )
    return pl.pallas_call(
        functools.partial(_moe_kernel, layer=layer),
        grid_spec=grid_spec,
        out_shape=jax.ShapeDtypeStruct((MOE_ROWS, ROW_WORDS), jnp.int32),
        compiler_params=_cparams(1),
        name="moe_experts",
    )(*blk_meta, xs, w_gu, b_gu.reshape(DEPTH, N_EXPERTS, 1, 2 * D_EXPERT),
      w_down, b_down.reshape(DEPTH, N_EXPERTS, 1, D_MODEL))


def _route(route, counts):
    experts = jnp.arange(N_EXPERTS, dtype=jnp.int32)
    padded = (counts + MOE_TM - 1) // MOE_TM * MOE_TM
    pad_end = jnp.cumsum(padded)
    pad_start = pad_end - padded
    e = route[0:TOP_K]
    onehot = e[:, :, None] == experts[None, None, :]
    dest = jnp.sum(jnp.where(onehot, pad_start[None, None, :], 0), axis=-1) + route[TOP_K:2 * TOP_K]
    blk_row0 = jnp.arange(MOE_BLOCKS, dtype=jnp.int32) * MOE_TM
    blk_e = jnp.minimum(jnp.sum((pad_end[None, :] <= blk_row0[:, None]).astype(jnp.int32), axis=1),
                        N_EXPERTS - 1)
    on = blk_row0 < pad_end[-1]
    n_on = jnp.sum(on.astype(jnp.int32))
    blk_onehot = blk_e[:, None] == experts[None, :]
    row_end = jnp.sum(jnp.where(blk_onehot, (pad_start + counts)[None, :], 0), axis=1)
    blk_rows = jnp.where(on, jnp.clip(row_end - blk_row0, 0, MOE_TM), 0).astype(jnp.int32)
    last_e = jnp.sum(jnp.where(jnp.arange(MOE_BLOCKS) == n_on - 1, blk_e, 0))
    blk_e = jnp.where(on, blk_e, last_e).astype(jnp.int32)
    prev = jnp.concatenate([jnp.full((1,), -1, jnp.int32), blk_e[:-1]])
    blk_first = (blk_e != prev).astype(jnp.int32)
    blk_slot = ((jnp.cumsum(blk_first) - 1) % 2).astype(jnp.int32)
    later_used = (experts[None, :] > experts[:, None]) & (counts[None, :] > 0)
    next_used = jnp.min(jnp.where(later_used, experts[None, :], N_EXPERTS), axis=1)
    next_used = jnp.where(next_used == N_EXPERTS, -1, next_used)
    blk_next = jnp.sum(jnp.where(blk_e[:, None] == experts[None, :], next_used[None, :], 0), axis=1)
    return dest, (blk_e, blk_first, blk_rows, blk_slot, blk_next.astype(jnp.int32))


def _combine_kernel(y_ref, gate_ref, x1_ref, mod_ref, g_ref, *o_refs):
    i = pl.program_id(0)
    m = mod_ref[0]
    gate = gate_ref[...]
    acc_lo = None
    for k in range(TOP_K):
        lo, hi = _unpack_pairs(y_ref[k])
        gk = gate[:, k:k + 1]
        acc_lo = gk * lo if acc_lo is None else acc_lo + gk * lo
        acc_hi = gk * hi if k == 0 else acc_hi + gk * hi
    acc = jnp.concatenate([acc_lo, acc_hi], axis=1)
    out = x1_ref[...] + _rms(acc, g_ref[...][3:4] * m[5:6])
    if len(o_refs) == 1:
        o_refs[0][...] = out
    else:
        @pl.when(i < NPT)
        def _():
            o_refs[0][...] = out

        @pl.when(i >= NPT)
        def _():
            o_refs[1][...] = out


def _combine(yg, gates, x1, mod_l, g_l, split_out):
    if split_out:
        out_specs = [_PROMPT_SPEC, _SAMPLE_SPEC]
        out_shape = [jax.ShapeDtypeStruct((NP_TOK, D_MODEL), F32), jax.ShapeDtypeStruct((NS_TOK, D_MODEL), F32)]
    else:
        out_specs = [_tok_spec(D_MODEL)]
        out_shape = [jax.ShapeDtypeStruct((N_TOK, D_MODEL), F32)]
    return pl.pallas_call(
        _combine_kernel,
        grid=(N_TILES,),
        in_specs=[pl.BlockSpec((TOP_K, TM, ROW_WORDS), lambda i: (0, i, 0)), _tok_spec(LANES),
                  _tok_spec(D_MODEL), _MOD_SPEC, _const_spec((4, D_MODEL))],
        out_specs=out_specs,
        out_shape=out_shape,
        compiler_params=_cparams(1),
        name="combine",
    )(yg, gates, x1, mod_l, g_l)


def _ffn(layer, o_cat, xs, mod_l, g_l, w_out, w_router, b_router, w_gu, b_gu, w_down, b_down, split_out):
    w_r = jnp.pad(w_router, ((0, 0), (0, LANES - N_EXPERTS))).astype(BF16)
    b_r = jnp.pad(b_router, (0, LANES - N_EXPERTS)).reshape(1, LANES)
    x1, h2p, route, gate_slab, counts = _postmix(o_cat, xs, mod_l, g_l, w_out.astype(BF16), w_r, b_r)
    dest, blk_meta = _route(route, counts[0, :N_EXPERTS])
    n_chunks = N_TOK // (SC_WORKERS * SC_ROWS)
    idx_d = dest.reshape(TOP_K, SC_WORKERS, n_chunks, SC_ROWS).transpose(1, 2, 0, 3).reshape(
        SC_WORKERS, n_chunks * TOP_K, SC_ROWS)
    rows = _sc_dispatch(h2p, idx_d)
    ys = _moe(layer, blk_meta, rows, w_gu, b_gu, w_down, b_down)
    idx_c = dest.reshape(SC_WORKERS, TOP_K * n_chunks, SC_ROWS)
    yg = _sc_collect(ys, idx_c).reshape(TOP_K, N_TOK, ROW_WORDS)
    return _combine(yg, gate_slab, x1, mod_l, g_l, split_out)


def _pad_heads(w, n_heads, width, keep):
    k = w.shape[0]
    w = w.reshape(k, n_heads, width)[:, :, :keep]
    return jnp.pad(w, ((0, 0), (0, 0), (0, LANES - keep))).reshape(k, n_heads * LANES)


def _pe_slab(x):
    return jnp.pad(x, [(0, 0)] * (x.ndim - 1) + [(MLA_NOPE, LANES - MLA_NOPE - MLA_ROPE)])


def _pair_kv_heads(w):
    g = SWA_HEADS // SWA_KV_HEADS
    return w.reshape(SWA_KV_HEADS, g, HEAD_DIM, -1).transpose(1, 0, 2, 3).reshape(w.shape)


def kernel(x_prompt, x_sample, cache_mla_ckv, cache_mla_krope, cache_diff_k, cache_diff_v, cache_swa_k, cache_swa_v, cache_na_k, cache_na_v, c, c_ctx, w_mod, b_mod, norm_g, w_in0, mla_q_norm, w_uq, mla_kv_norm, w_ukv, diff_lambda, diff_norm, w_out0, w_in1, swa_sink, na_rpb, w_out1, w_router, b_router, w_gu, b_gu, w_down, b_down):
    xs = (x_prompt.reshape(NP_TOK, D_MODEL), x_sample.reshape(NS_TOK, D_MODEL))
    cond = jnp.concatenate([c_ctx[None, :], c, jnp.zeros((16 - 1 - DEC_BATCH, D_MODEL), F32)], axis=0)
    mod = _modulation(cond, w_mod, b_mod).reshape(DEPTH, 16, 6, D_MODEL)
    t64, t32 = _rope_tables()
    states = {}
    for l in range(DEPTH):
        i = l // 2
        g_l = norm_g[l]
        mod_l = mod[l]
        if l % 2 == 0:
            lam_init = 0.8 - 0.6 * math.exp(-0.3 * l)
            wi = w_in0[i]
            w_in_p = jnp.concatenate(
                [wi[:, 0:640], wi[:, 672:2208], _pe_slab(wi[:, 640:672])], axis=1).astype(BF16)
            w_uq_p = _pad_heads(w_uq[i], MLA_HEADS, MLA_NOPE + MLA_ROPE, MLA_NOPE + MLA_ROPE).astype(BF16)
            w_k_p = _pad_heads(w_ukv[i], MLA_HEADS, MLA_NOPE + MLA_V, MLA_NOPE).astype(BF16)
            w_v = w_ukv[i].reshape(MLA_KV_RANK, MLA_HEADS, MLA_NOPE + MLA_V)[:, :, MLA_NOPE:].reshape(
                MLA_KV_RANK, MLA_HEADS * MLA_V).astype(BF16)
            (q, k, v, dq, dk, dv, ckv_st, kpe_st, dk_st, dv_st) = _premix0(
                *xs, mod_l, g_l[0:1], w_in_p, mla_q_norm[i][None, :], w_uq_p, mla_kv_norm[i][None, :],
                w_k_p, w_v, t32, t64)
            states['mla_ckv'] = ckv_st.reshape(BATCH, 1, SEQ, MLA_KV_RANK)
            states['mla_krope'] = kpe_st[:, MLA_NOPE:MLA_NOPE + MLA_ROPE].reshape(BATCH, 1, SEQ, MLA_ROPE)
            states['diff_k'] = dk_st.reshape(BATCH, 1, SEQ, DIFF_HEADS, 2 * DIFF_DH)
            states['diff_v'] = dv_st.reshape(BATCH, 1, SEQ, DIFF_HEADS, 2 * DIFF_DH)
            kc, vc = _mla_cache(cache_mla_ckv[:, i].reshape(DEC_BATCH * PAST_LEN, MLA_KV_RANK),
                                _pe_slab(cache_mla_krope[:, i].reshape(DEC_BATCH * PAST_LEN, MLA_ROPE)),
                                w_k_p, w_v)
            dkc = cache_diff_k[:, i].reshape(DEC_BATCH * PAST_LEN, 512).astype(BF16)
            dvc = cache_diff_v[:, i].reshape(DEC_BATCH * PAST_LEN, 512).astype(BF16)
            lam = diff_lambda[i]
            sub_g = diff_norm[i][None, :]
            o_p = _attn_ab(q, dq, (k, v, dk, dv), None, lam, sub_g, lam_init,
                           n_batch=BATCH, t_len=SEQ, tq=SEQ, tok_off=0)
            o_cat = _attn_ab(q, dq, (k, v, dk, dv), (kc, vc, dkc, dvc), lam, sub_g, lam_init,
                             n_batch=DEC_BATCH, t_len=DEC_SEQ, tq=TQ_AB, tok_off=NP_TOK, out_init=o_p)
            w_out = w_out0[i]
        else:
            wi = w_in1[i]
            n_sq = SWA_HEADS * HEAD_DIM
            w_in_p = jnp.concatenate([_pair_kv_heads(wi[:, :n_sq].T).T, wi[:, n_sq:]], axis=1).astype(BF16)
            (sq, sk, sv, nq, nk, nv, sk_st, sv_st, nk_st, nv_st) = _premix1(xs[0], mod_l, g_l[0:1], w_in_p, t64)
            states['swa_k'] = sk_st.reshape(BATCH, 1, SEQ, SWA_KV_HEADS, HEAD_DIM)
            states['swa_v'] = sv_st.reshape(BATCH, 1, SEQ, SWA_KV_HEADS, HEAD_DIM)
            states['na_k'] = nk_st.reshape(BATCH, 1, SEQ, NA_HEADS, HEAD_DIM)
            states['na_v'] = nv_st.reshape(BATCH, 1, SEQ, NA_HEADS, HEAD_DIM)
            skc = cache_swa_k[:, i].reshape(DEC_BATCH * PAST_LEN, 128).astype(BF16)
            svc = cache_swa_v[:, i].reshape(DEC_BATCH * PAST_LEN, 128).astype(BF16)
            nkc = cache_na_k[:, i].reshape(DEC_BATCH * PAST_LEN, 512).astype(BF16)
            nvc = cache_na_v[:, i].reshape(DEC_BATCH * PAST_LEN, 512).astype(BF16)
            sink = swa_sink[i]
            o_p = _attn_cd_prompt(sink, sq, sk, sv, nq, nk, nv)
            o_cat = _attn_cd_sample(o_p, sink, sq, sk, sv, nq, nk, nv, skc, svc, nkc, nvc, _na_bias(na_rpb[i]))
            wo = w_out1[i]
            w_out = jnp.concatenate([_pair_kv_heads(wo[:n_sq]), wo[n_sq:]], axis=0)
        xs = _ffn(l, o_cat, xs, mod_l, g_l, w_out, w_router[l], b_router[l], w_gu, b_gu, w_down, b_down,
                  split_out=(l == DEPTH - 1))
    return (xs[0].reshape(BATCH, SEQ, D_MODEL), xs[1].reshape(DEC_BATCH, DEC_SEQ, D_MODEL),
            states['mla_ckv'], states['mla_krope'], states['diff_k'], states['diff_v'],
            states['swa_k'], states['swa_v'], states['na_k'], states['na_v'])
```

```python
import functools
import math

import numpy as np
import jax
import jax.numpy as jnp
from jax import lax
from jax.experimental import pallas as pl
from jax.experimental.pallas import tpu as pltpu
from jax.experimental.pallas import tpu_sc as plsc

F32 = jnp.float32
BF16 = jnp.bfloat16

D_MODEL = 1024
BATCH = 16
SEQ = 256
DEPTH = 2
DEC_BATCH = 8
DEC_SEQ = 2048
PAST_LEN = 256
GRID_W = 64
HEAD_DIM = 64
ROPE_THETA = 10000.0
EPS = 1e-6
NEG = -1e30

MLA_HEADS = 8
MLA_Q_RANK = 384
MLA_KV_RANK = 256
MLA_NOPE = 64
MLA_ROPE = 32
MLA_V = 64
DIFF_HEADS = 4
DIFF_DH = 64
SWA_HEADS = 8
SWA_KV_HEADS = 2
SWA_WINDOW = 128
NA_HEADS = 8
NA_WIN_ROWS = 8
NA_WIN_COLS = 16
N_EXPERTS = 32
TOP_K = 4
D_EXPERT = 1024
SWIGLU_LIMIT = 7.0
SWIGLU_ALPHA = 1.702

LANES = 128
NP_TOK = BATCH * SEQ
NS_TOK = DEC_BATCH * DEC_SEQ
N_TOK = NP_TOK + NS_TOK
TM = 512
NPT = NP_TOK // TM
TILES_PER_SAMPLE = DEC_SEQ // TM
N_TILES = N_TOK // TM
N_SUB = 2
SUB_TM = TM // N_SUB
TQ = 256
TQ_AB = 256
MOE_TM = 1024
MOE_TAIL_PARTS = 8
MOE_ROWS = ((N_TOK * TOP_K + N_EXPERTS * (MOE_TM - 1)) // MOE_TM + 1) * MOE_TM
MOE_BLOCKS = MOE_ROWS // MOE_TM
NA_TILE_ROWS = TQ // GRID_W
NA_KEY_ROWS = 12
VMEM_LIMIT = 56 * 1024 * 1024


def _cparams(n_axes, vmem=VMEM_LIMIT):
    return pltpu.CompilerParams(dimension_semantics=("arbitrary",) * n_axes,
                                vmem_limit_bytes=vmem)


def _rms(x, g):
    return x * lax.rsqrt(jnp.mean(x * x, axis=-1, keepdims=True) + EPS) * g


def _dot(a, b):
    return jnp.dot(a, b, preferred_element_type=F32)


def _dot_nt(a, b):
    return lax.dot_general(a, b, (((1,), (1,)), ((), ())), preferred_element_type=F32)


def _rope(x, cos, sin_a, sin_b, half):
    return (x * cos + pltpu.roll(x, LANES - half, 1) * sin_a + pltpu.roll(x, half, 1) * sin_b)


def _pipelined_units(units):
    pending = units[0][0](units[0][2])
    for n, (_, finish, arg) in enumerate(units):
        following = units[n + 1][0](units[n + 1][2]) if n + 1 < len(units) else None
        finish(arg, pending)
        pending = following


def _sub_rows(n):
    return slice(SUB_TM * n, SUB_TM * (n + 1))


def _store_head_rows(st_ref, n, hd, n_heads, x):
    st_ref[pl.ds(SUB_TM * n * n_heads + hd, SUB_TM, stride=n_heads), :] = x


def _mod_row(i):
    return jnp.where(i < NPT, 0, 1 + (i - NPT) // TILES_PER_SAMPLE)


def _rope_blk(i):
    return jnp.where(i < NPT, TILES_PER_SAMPLE, (i - NPT) % TILES_PER_SAMPLE)


def _mod_kernel(c_ref, w_ref, b_ref, o_ref):
    c = c_ref[...]
    s = (c * jax.nn.sigmoid(c)).astype(BF16)
    o_ref[0] = _dot(s, w_ref[0].astype(BF16)) + b_ref[0]


def _modulation(cond, w_mod, b_mod):
    nb = 1024
    return pl.pallas_call(
        _mod_kernel,
        grid=(DEPTH, 6 * D_MODEL // nb),
        in_specs=[
            pl.BlockSpec((16, D_MODEL), lambda l, n: (0, 0)),
            pl.BlockSpec((1, D_MODEL, nb), lambda l, n: (l, 0, n)),
            pl.BlockSpec((1, 1, nb), lambda l, n: (l, 0, n)),
        ],
        out_specs=pl.BlockSpec((1, 16, nb), lambda l, n: (l, 0, n)),
        out_shape=jax.ShapeDtypeStruct((DEPTH, 16, 6 * D_MODEL), F32),
        compiler_params=_cparams(2),
        name="modulation",
    )(cond, w_mod, b_mod.reshape(DEPTH, 1, 6 * D_MODEL))


def _rope_tables():
    t = jnp.arange(DEC_SEQ)
    rows = (t // GRID_W).astype(F32)
    cols = (t % GRID_W).astype(F32)

    def angles(r):
        n = r // 4
        inv = ROPE_THETA ** (-jnp.arange(n, dtype=F32) / n)
        return jnp.concatenate([rows[:, None] * inv[None], cols[:, None] * inv[None]], axis=-1)

    def finish(cos, sa, sb):
        ident = (jnp.ones((TM, LANES), F32), jnp.zeros((TM, LANES), F32), jnp.zeros((TM, LANES), F32))
        return tuple(jnp.concatenate([a, b], axis=0) for a, b in zip((cos, sa, sb), ident))

    a64 = angles(64)
    c, s, z = jnp.cos(a64), jnp.sin(a64), jnp.zeros_like(a64)
    t64 = finish(jnp.concatenate([c, c, c, c], -1), jnp.concatenate([-s, z, -s, z], -1),
                 jnp.concatenate([z, s, z, s], -1))
    a32 = angles(32)
    c, s, z = jnp.cos(a32), jnp.sin(a32), jnp.zeros_like(a32)
    one64 = jnp.ones((DEC_SEQ, 64), F32)
    z64 = jnp.zeros((DEC_SEQ, 64), F32)
    z32 = jnp.zeros((DEC_SEQ, 32), F32)
    t32 = finish(jnp.concatenate([one64, c, c, z32], -1), jnp.concatenate([z64, -s, z, z32], -1),
                 jnp.concatenate([z64, z, s, z32], -1))
    return t64, t32


LOG2E = math.log2(math.e)
_DIFF_COLS = DIFF_HEADS * 2 * DIFF_DH
_AB_COLS = tuple(np.cumsum([0, MLA_Q_RANK, MLA_KV_RANK, _DIFF_COLS, _DIFF_COLS, _DIFF_COLS, LANES]).tolist())
_CD_COLS = tuple(np.cumsum([0, SWA_HEADS * HEAD_DIM, SWA_KV_HEADS * HEAD_DIM, SWA_KV_HEADS * HEAD_DIM,
                            NA_HEADS * HEAD_DIM, NA_HEADS * HEAD_DIM, NA_HEADS * HEAD_DIM]).tolist())
IN_COLS = _AB_COLS[-1]
assert IN_COLS == _CD_COLS[-1]
_MLA_SCALE = (MLA_NOPE + MLA_ROPE) ** -0.5 * LOG2E
_QSCALE = HEAD_DIM ** -0.5 * LOG2E


def _premix0_kernel(xp_ref, xs_ref, mod_ref, g_ref, win_ref, qn_ref, wuq_ref, kvn_ref, wk_ref, wv_ref,
                    c32_ref, sa32_ref, sb32_ref, c64_ref, sa64_ref, sb64_ref,
                    q_ref, k_ref, v_ref, dq_ref, dk_ref, dv_ref,
                    ckv_st, kpe_st, dk_st, dv_st):
    i = pl.program_id(0)
    m = mod_ref[0]
    gain = g_ref[...] * (1.0 + m[1:2])

    def project(n):
        r = _sub_rows(n)
        x = jnp.where(i < NPT, xp_ref[r, :], xs_ref[r, :])
        h = _rms(x, gain) + m[0:1]
        return _dot(h.astype(BF16), win_ref[...])

    states = {}

    def finish(n, proj):
        r = _sub_rows(n)
        q_a, kv_a, dq, dk, dv, pe = (proj[:, a:b] for a, b in zip(_AB_COLS[:-1], _AB_COLS[1:]))
        q = _dot(_rms(q_a, qn_ref[...]).astype(BF16), wuq_ref[...])
        ckv = _rms(kv_a, kvn_ref[...])
        ckv_b = ckv.astype(BF16)
        kn = _dot(ckv_b, wk_ref[...])
        v_ref[r, :] = _dot(ckv_b, wv_ref[...]).astype(BF16)
        c32, sa32, sb32 = c32_ref[r, :], sa32_ref[r, :], sb32_ref[r, :]
        c64, sa64, sb64 = c64_ref[r, :], sa64_ref[r, :], sb64_ref[r, :]
        pe_r = _rope(pe, c32, sa32, sb32, MLA_ROPE // 2)
        for hd in range(MLA_HEADS):
            sl = slice(LANES * hd, LANES * (hd + 1))
            q_ref[r, sl] = (_rope(q[:, sl], c32, sa32, sb32, MLA_ROPE // 2) * _MLA_SCALE).astype(BF16)
            k_ref[r, sl] = (kn[:, sl] + pe_r).astype(BF16)
        for hd in range(DIFF_HEADS):
            sl = slice(LANES * hd, LANES * (hd + 1))
            dq_ref[r, sl] = (_rope(dq[:, sl], c64, sa64, sb64, DIFF_DH // 2) * _QSCALE).astype(BF16)
            dk_ref[r, sl] = _rope(dk[:, sl], c64, sa64, sb64, DIFF_DH // 2).astype(BF16)
        dv_ref[r, :] = dv.astype(BF16)
        states[n] = (ckv, pe, dk, dv)

    _pipelined_units([(project, finish, n) for n in range(N_SUB)])

    @pl.when(i < NPT)
    def _():
        for n in range(N_SUB):
            r = _sub_rows(n)
            ckv, pe, dk, dv = states[n]
            ckv_st[r, :] = ckv
            kpe_st[r, :] = pe
            for hd in range(DIFF_HEADS):
                _store_head_rows(dk_st, n, hd, DIFF_HEADS, dk[:, LANES * hd: LANES * (hd + 1)])
                _store_head_rows(dv_st, n, hd, DIFF_HEADS, dv[:, LANES * hd: LANES * (hd + 1)])


def _premix1_kernel(x_ref, mod_ref, g_ref, win_ref, c64_ref, sa64_ref, sb64_ref,
                    sq_ref, sk_ref, sv_ref, nq_ref, nk_ref, nv_ref,
                    sk_st, sv_st, nk_st, nv_st):
    i = pl.program_id(0)
    m = mod_ref[0]
    gain = g_ref[...] * (1.0 + m[1:2])

    def project(n):
        h = _rms(x_ref[_sub_rows(n), :], gain) + m[0:1]
        return _dot(h.astype(BF16), win_ref[...])

    states = {}

    def finish(n, proj):
        r = _sub_rows(n)
        sq, sk, sv, nq, nk, nv = (proj[:, a:b] for a, b in zip(_CD_COLS[:-1], _CD_COLS[1:]))
        c64, sa64, sb64 = c64_ref[r, :], sa64_ref[r, :], sb64_ref[r, :]
        for hd in range(4):
            sl = slice(LANES * hd, LANES * (hd + 1))
            sq_ref[r, sl] = (_rope(sq[:, sl], c64, sa64, sb64, HEAD_DIM // 2) * _QSCALE).astype(BF16)
        sk_ref[r, :] = _rope(sk, c64, sa64, sb64, HEAD_DIM // 2).astype(BF16)
        sv_ref[r, :] = sv.astype(BF16)
        nq_ref[r, :] = (nq * _QSCALE).astype(BF16)
        nk_ref[r, :] = nk.astype(BF16)
        nv_ref[r, :] = nv.astype(BF16)
        states[n] = (sk, sv, nk, nv)

    _pipelined_units([(project, finish, n) for n in range(N_SUB)])

    @pl.when(i < NPT)
    def _():
        for n in range(N_SUB):
            r = _sub_rows(n)
            sk, sv, nk, nv = states[n]
            sk_st[r, :] = sk
            sv_st[r, :] = sv
            for hd in range(NA_HEADS):
                _store_head_rows(nk_st, n, hd, NA_HEADS, nk[:, HEAD_DIM * hd: HEAD_DIM * (hd + 1)])
                _store_head_rows(nv_st, n, hd, NA_HEADS, nv[:, HEAD_DIM * hd: HEAD_DIM * (hd + 1)])


def _tok_spec(width):
    return pl.BlockSpec((TM, width), lambda i: (i, 0))


_PROMPT_SPEC = pl.BlockSpec((TM, D_MODEL), lambda i: (jnp.minimum(i, NPT - 1), 0))
_SAMPLE_SPEC = pl.BlockSpec((TM, D_MODEL), lambda i: (jnp.maximum(i - NPT, 0), 0))


def _state_spec(width, rows_per_token=1):
    return pl.BlockSpec((TM * rows_per_token, width), lambda i: (jnp.minimum(i, NPT - 1), 0))


def _const_spec(shape):
    return pl.BlockSpec(shape, lambda i: (0,) * len(shape))


_MOD_SPEC = pl.BlockSpec((1, 6, D_MODEL), lambda i: (_mod_row(i), 0, 0))
_ROPE_SPEC = pl.BlockSpec((TM, LANES), lambda i: (_rope_blk(i), 0))


def _premix0(xp, xs, mod_l, g0, w_in_p, q_norm, w_uq_p, kv_norm, w_k_p, w_v, t32, t64):
    outs = [(N_TOK, 1024, BF16), (N_TOK, 1024, BF16), (N_TOK, 512, BF16), (N_TOK, 512, BF16),
            (N_TOK, 512, BF16), (N_TOK, 512, BF16),
            (NP_TOK, 256, F32), (NP_TOK, 128, F32),
            (NP_TOK * DIFF_HEADS, 2 * DIFF_DH, F32), (NP_TOK * DIFF_HEADS, 2 * DIFF_DH, F32)]
    return pl.pallas_call(
        _premix0_kernel,
        grid=(N_TILES,),
        in_specs=[_PROMPT_SPEC, _SAMPLE_SPEC, _MOD_SPEC, _const_spec((1, D_MODEL)),
                  _const_spec((D_MODEL, IN_COLS)), _const_spec((1, MLA_Q_RANK)),
                  _const_spec((MLA_Q_RANK, 1024)), _const_spec((1, MLA_KV_RANK)),
                  _const_spec((MLA_KV_RANK, 1024)), _const_spec((MLA_KV_RANK, 512))]
                 + [_ROPE_SPEC] * 6,
        out_specs=([_tok_spec(w) for (_, w, _) in outs[:6]]
                   + [_state_spec(w, n // NP_TOK) for (n, w, _) in outs[6:]]),
        out_shape=[jax.ShapeDtypeStruct((n, w), dt) for (n, w, dt) in outs],
        compiler_params=_cparams(1),
        name="premix_ab",
    )(xp, xs, mod_l, g0, w_in_p, q_norm, w_uq_p, kv_norm, w_k_p, w_v, *t32, *t64)


def _premix1(x, mod_l, g0, w_in_p, t64):
    outs = [(N_TOK, 512, BF16), (N_TOK, 128, BF16), (N_TOK, 128, BF16), (N_TOK, 512, BF16),
            (N_TOK, 512, BF16), (N_TOK, 512, BF16),
            (NP_TOK, 128, F32), (NP_TOK, 128, F32),
            (NP_TOK * NA_HEADS, HEAD_DIM, F32), (NP_TOK * NA_HEADS, HEAD_DIM, F32)]
    return pl.pallas_call(
        _premix1_kernel,
        grid=(N_TILES,),
        in_specs=[_tok_spec(D_MODEL), _MOD_SPEC, _const_spec((1, D_MODEL)),
                  _const_spec((D_MODEL, IN_COLS))] + [_ROPE_SPEC] * 3,
        out_specs=([_tok_spec(w) for (_, w, _) in outs[:6]]
                   + [_state_spec(w, n // NP_TOK) for (n, w, _) in outs[6:]]),
        out_shape=[jax.ShapeDtypeStruct((n, w), dt) for (n, w, dt) in outs],
        compiler_params=_cparams(1),
        name="premix_cd",
    )(x, mod_l, g0, w_in_p, *t64)


def _mla_cache_kernel(ckv_ref, pe_ref, wk_ref, wv_ref, k_ref, v_ref):
    c = ckv_ref[...].astype(BF16)
    kn = _dot(c, wk_ref[...])
    v_ref[...] = _dot(c, wv_ref[...]).astype(BF16)
    pe = pe_ref[...]
    for hd in range(MLA_HEADS):
        sl = slice(LANES * hd, LANES * (hd + 1))
        k_ref[:, sl] = (kn[:, sl] + pe).astype(BF16)


def _mla_cache(ckv, pe_slab, w_k_p, w_v):
    n = ckv.shape[0]
    tm = 512
    return pl.pallas_call(
        _mla_cache_kernel,
        grid=(n // tm,),
        in_specs=[pl.BlockSpec((tm, MLA_KV_RANK), lambda i: (i, 0)),
                  pl.BlockSpec((tm, LANES), lambda i: (i, 0)),
                  _const_spec((MLA_KV_RANK, 1024)), _const_spec((MLA_KV_RANK, 512))],
        out_specs=[pl.BlockSpec((tm, 1024), lambda i: (i, 0)), pl.BlockSpec((tm, 512), lambda i: (i, 0))],
        out_shape=[jax.ShapeDtypeStruct((n, 1024), BF16), jax.ShapeDtypeStruct((n, 512), BF16)],
        compiler_params=_cparams(1),
        name="mla_cache",
    )(ckv, pe_slab, w_k_p, w_v)


def _softmax_pv(scores, values, sink=None):
    m = jnp.max(scores[0], axis=-1, keepdims=True)
    for s in scores[1:]:
        m = jnp.maximum(m, jnp.max(s, axis=-1, keepdims=True))
    if sink is not None:
        m = jnp.maximum(m, sink)
    l = None
    o = None
    for s, v in zip(scores, values):
        p = jnp.exp2(s - m)
        ls = jnp.sum(p, axis=-1, keepdims=True)
        os_ = _dot(p.astype(BF16), v)
        l = ls if l is None else l + ls
        o = os_ if o is None else o + os_
    if sink is not None:
        l = l + jnp.exp2(sink - m)
    return o * (1.0 / l)


def _lane_lo(shape):
    return lax.broadcasted_iota(jnp.int32, shape, 1) < (LANES // 2)


def _split_halves(qb):
    lo = _lane_lo(qb.shape)
    zero = jnp.zeros_like(qb)
    return jnp.where(lo, qb, zero), jnp.where(lo, zero, qb)


def _attn_ab_kernel(*refs, n_pieces, lam_init, aliased):
    if aliased:
        refs = refs[1:]
    q_ref, dq_ref = refs[0], refs[1]
    pieces = [refs[2 + 4 * p: 6 + 4 * p] for p in range(n_pieces)]
    lam_ref, subg_ref, o_ref = refs[2 + 4 * n_pieces:]
    lam = lam_ref[...]
    lam_full = (jnp.exp(jnp.sum(lam[0:1] * lam[1:2], axis=-1, keepdims=True))
                - jnp.exp(jnp.sum(lam[2:3] * lam[3:4], axis=-1, keepdims=True)) + lam_init)
    lo = _lane_lo((q_ref.shape[0], LANES))
    subg = subg_ref[...] * (1.0 - lam_init)
    tq = q_ref.shape[0]
    def mla_scores(hd):
        sl = slice(LANES * hd, LANES * (hd + 1))
        qh = q_ref[:, sl]
        return [_dot_nt(qh, k_ref[:, sl]) for (k_ref, _, _, _) in pieces]

    def diff_scores(hd):
        sl = slice(LANES * hd, LANES * (hd + 1))
        qq = jnp.concatenate(_split_halves(dq_ref[:, sl]), axis=0)
        return [_dot_nt(qq, dk_ref[:, sl]) for (_, _, dk_ref, _) in pieces]

    pair = {}

    def mla_finish(hd, scores):
        j = hd // 2
        vals = [v_ref[:, LANES * j: LANES * (j + 1)] for (_, v_ref, _, _) in pieces]
        pair[hd % 2] = _softmax_pv(scores, vals)
        if hd % 2 == 1:
            o_ref[:, LANES * j: LANES * (j + 1)] = jnp.where(lo, pair[0], pair[1]).astype(BF16)

    def diff_finish(hd, scores):
        sl = slice(LANES * hd, LANES * (hd + 1))
        oo = _softmax_pv(scores, [dv_ref[:, sl] for (_, _, _, dv_ref) in pieces])
        od = _rms(oo[:tq] - lam_full * oo[tq:], subg)
        o_ref[:, 512 + LANES * hd: 512 + LANES * (hd + 1)] = od.astype(BF16)

    units = []
    for j in range(DIFF_HEADS):
        units += [(diff_scores, diff_finish, j), (mla_scores, mla_finish, 2 * j),
                  (mla_scores, mla_finish, 2 * j + 1)]
    _pipelined_units(units)


def _attn_ab(q, dq, new_kv, cache_kv, lam, sub_g, lam_init, *, n_batch, t_len, tq, tok_off, out_init=None):
    nq = t_len // tq
    q_off = tok_off // tq
    b_off = tok_off // t_len
    widths = (1024, 512, 512, 512)
    in_specs = [pl.BlockSpec((tq, 1024), lambda b, i: (q_off + b * nq + i, 0)),
                pl.BlockSpec((tq, 512), lambda b, i: (q_off + b * nq + i, 0))]
    args = [q, dq]
    for w, a in zip(widths, new_kv):
        in_specs.append(pl.BlockSpec((t_len, w), lambda b, i: (b_off + b, 0)))
        args.append(a)
    n_pieces = 1
    if cache_kv is not None:
        n_pieces = 2
        for w, a in zip(widths, cache_kv):
            in_specs.append(pl.BlockSpec((PAST_LEN, w), lambda b, i: (b, 0)))
            args.append(a)
    in_specs += [pl.BlockSpec((4, DIFF_DH), lambda b, i: (0, 0)),
                 pl.BlockSpec((1, 2 * DIFF_DH), lambda b, i: (0, 0))]
    args += [lam, sub_g]
    aliases = {}
    if out_init is not None:
        in_specs = [pl.BlockSpec(memory_space=pl.ANY)] + in_specs
        args = [out_init] + args
        aliases = {0: 0}
    return pl.pallas_call(
        functools.partial(_attn_ab_kernel, n_pieces=n_pieces, lam_init=lam_init, aliased=out_init is not None),
        grid=(n_batch, nq),
        in_specs=in_specs,
        out_specs=pl.BlockSpec((tq, 1024), lambda b, i: (q_off + b * nq + i, 0)),
        out_shape=jax.ShapeDtypeStruct((N_TOK, 1024), BF16),
        input_output_aliases=aliases,
        compiler_params=_cparams(2),
        name="attn_ab_%d" % n_pieces,
    )(*args)


def _gqa_stacks(sq_ref, sink_ref):
    tq = sq_ref.shape[0]
    halves = [_split_halves(sq_ref[:, LANES * j: LANES * (j + 1)]) for j in range(4)]
    q_stacks = [jnp.concatenate([halves[j][kvh] for j in range(4)], axis=0) for kvh in range(SWA_KV_HEADS)]
    sinks = [jnp.concatenate([jnp.full((tq, 1), sink_ref[4 * kvh + j] * LOG2E, F32) for j in range(4)], axis=0)
             for kvh in range(SWA_KV_HEADS)]
    return q_stacks, sinks


def _attn_cd_prompt_kernel(sink_ref, sq_ref, sk_ref, sv_ref, nq_ref, nk_ref, nv_ref, o_ref):
    tq = sq_ref.shape[0]
    lo = _lane_lo((tq, LANES))
    sk = sk_ref[...]
    sv = sv_ref[...]
    for j in range(4):
        sl = slice(LANES * j, LANES * (j + 1))
        q_lo, q_hi = _split_halves(sq_ref[:, sl])
        o_lo = _softmax_pv([_dot_nt(q_lo, sk)], [sv], sink=sink_ref[j] * LOG2E)
        o_hi = _softmax_pv([_dot_nt(q_hi, sk)], [sv], sink=sink_ref[j + 4] * LOG2E)
        o_ref[:, sl] = jnp.where(lo, o_lo, o_hi).astype(BF16)
    for j in range(4):
        sl = slice(LANES * j, LANES * (j + 1))
        q_lo, q_hi = _split_halves(nq_ref[:, sl])
        k = nk_ref[:, sl]
        v = nv_ref[:, sl]
        o_lo = _softmax_pv([_dot_nt(q_lo, k)], [v])
        o_hi = _softmax_pv([_dot_nt(q_hi, k)], [v])
        o_ref[:, 512 + LANES * j: 512 + LANES * (j + 1)] = jnp.where(lo, o_lo, o_hi).astype(BF16)


def _attn_cd_prompt(sink, sq, sk, sv, nq, nk, nv):
    def spec(w):
        return pl.BlockSpec((SEQ, w), lambda b: (b, 0))
    return pl.pallas_call(
        _attn_cd_prompt_kernel,
        grid=(BATCH,),
        in_specs=[pl.BlockSpec(memory_space=pltpu.SMEM), spec(512), spec(128), spec(128),
                  spec(512), spec(512), spec(512)],
        out_specs=spec(1024),
        out_shape=jax.ShapeDtypeStruct((N_TOK, 1024), BF16),
        compiler_params=_cparams(1),
        name="attn_cd_prompt",
    )(sink, sq, sk, sv, nq, nk, nv)


_SWA_KEYS = TQ + 2 * SWA_WINDOW


def _attn_cd_sample_kernel(init_ref, sink_ref, sq_ref, nq_ref, sk_ref, sv_ref, nk_ref, nv_ref,
                           skc_ref, svc_ref, nkc_ref, nvc_ref, bias_ref, o_ref):
    del init_ref
    qi = pl.program_id(1)
    lo = _lane_lo((TQ, LANES))
    ks = pl.multiple_of(jnp.clip(qi * TQ - SWA_WINDOW, 0, DEC_SEQ - _SWA_KEYS), SWA_WINDOW)
    k_win = sk_ref[pl.ds(ks, _SWA_KEYS), :]
    v_win = sv_ref[pl.ds(ks, _SWA_KEYS), :]
    q_pos = qi * TQ + (lax.broadcasted_iota(jnp.int32, (4 * TQ, _SWA_KEYS), 0) & (TQ - 1))
    k_pos = ks + lax.broadcasted_iota(jnp.int32, (4 * TQ, _SWA_KEYS), 1)
    in_win = jnp.abs(q_pos - k_pos) <= SWA_WINDOW
    skc = skc_ref[...]
    svc = svc_ref[...]
    q_stacks, sinks = _gqa_stacks(sq_ref, sink_ref)

    def window_scores(kvh):
        return [_dot_nt(q_stacks[kvh], skc), jnp.where(in_win, _dot_nt(q_stacks[kvh], k_win), NEG)]

    o_kv = {}

    def window_finish(kvh, scores):
        o_kv[kvh] = _softmax_pv(scores, [svc, v_win], sink=sinks[kvh])
        if kvh == SWA_KV_HEADS - 1:
            for j in range(4):
                rows = slice(TQ * j, TQ * (j + 1))
                o_ref[:, LANES * j: LANES * (j + 1)] = jnp.where(lo, o_kv[0][rows], o_kv[1][rows]).astype(BF16)

    n_rows = DEC_SEQ // GRID_W
    r0 = jnp.clip(qi * NA_TILE_ROWS - NA_WIN_ROWS // 2, 0, n_rows - NA_KEY_ROWS)
    kn = pl.multiple_of(r0 * GRID_W, GRID_W)

    lane_lo = _lane_lo((1, LANES))
    pieces = []
    for ri in range(NA_TILE_ROWS):
        r = qi * NA_TILE_ROWS + ri
        rs = jnp.clip(r - NA_WIN_ROWS // 2, 0, n_rows - NA_WIN_ROWS)
        row = []
        for mm in range(NA_KEY_ROWS // 2):
            kr = r0 + 2 * mm
            ok = [(kr + t >= rs) & (kr + t < rs + NA_WIN_ROWS) for t in range(2)]
            mask = jnp.where(lane_lo, jnp.where(ok[0], 0.0, NEG), jnp.where(ok[1], 0.0, NEG))
            row.append((jnp.clip(kr - r + NA_WIN_ROWS, 0, 2 * NA_WIN_ROWS - 1), mask))
        pieces.append(row)

    def add_na_bias(hd, s):
        return jnp.concatenate(
            [jnp.concatenate([s[GRID_W * ri: GRID_W * (ri + 1), LANES * mm: LANES * (mm + 1)] + bias_ref[hd, d] + mask
                              for mm, (d, mask) in enumerate(row)], axis=1)
             for ri, row in enumerate(pieces)], axis=0)

    def na_scores(hd):
        sl = slice(LANES * (hd // 2), LANES * (hd // 2 + 1))
        q_half = _split_halves(nq_ref[:, sl])[hd % 2]
        nk_win = nk_ref[pl.ds(kn, NA_KEY_ROWS * GRID_W), sl]
        return [_dot_nt(q_half, nkc_ref[:, sl]), add_na_bias(hd, _dot_nt(q_half, nk_win))]

    pair = {}

    def na_finish(hd, scores):
        j = hd // 2
        sl = slice(LANES * j, LANES * (j + 1))
        nv_win = nv_ref[pl.ds(kn, NA_KEY_ROWS * GRID_W), sl]
        pair[hd % 2] = _softmax_pv(scores, [nvc_ref[:, sl], nv_win])
        if hd % 2 == 1:
            o_ref[:, 512 + LANES * j: 512 + LANES * (j + 1)] = jnp.where(lo, pair[0], pair[1]).astype(BF16)

    _pipelined_units([(window_scores, window_finish, kvh) for kvh in range(SWA_KV_HEADS)]
                     + [(na_scores, na_finish, hd) for hd in range(NA_HEADS)])


def _attn_cd_sample(out_init, sink, sq, sk, sv, nq, nk, nv, skc, svc, nkc, nvc, bias):
    nq_t = DEC_SEQ // TQ
    q_off = NP_TOK // TQ
    b_off = NP_TOK // DEC_SEQ

    def qspec(w):
        return pl.BlockSpec((TQ, w), lambda b, i: (q_off + b * nq_t + i, 0))

    def kspec(w):
        return pl.BlockSpec((DEC_SEQ, w), lambda b, i: (b_off + b, 0))

    def cspec(w):
        return pl.BlockSpec((PAST_LEN, w), lambda b, i: (b, 0))

    bias_spec = pl.BlockSpec((NA_HEADS, 2 * NA_WIN_ROWS, GRID_W, LANES), lambda b, i: (0, 0, 0, 0))
    return pl.pallas_call(
        _attn_cd_sample_kernel,
        grid=(DEC_BATCH, nq_t),
        in_specs=[pl.BlockSpec(memory_space=pl.ANY), pl.BlockSpec(memory_space=pltpu.SMEM),
                  qspec(512), qspec(512), kspec(128), kspec(128), kspec(512), kspec(512),
                  cspec(128), cspec(128), cspec(512), cspec(512), bias_spec],
        out_specs=pl.BlockSpec((TQ, 1024), lambda b, i: (q_off + b * nq_t + i, 0)),
        out_shape=jax.ShapeDtypeStruct((N_TOK, 1024), BF16),
        input_output_aliases={0: 0},
        compiler_params=_cparams(2),
        name="attn_cd_sample",
    )(out_init, sink, sq, nq, sk, sv, nk, nv, skc, svc, nkc, nvc, bias)


def _na_bias(rpb):
    n_dc = 2 * NA_WIN_COLS - 1
    c = np.arange(GRID_W)[:, None]
    kc = np.arange(GRID_W)[None, :]
    qs = np.clip(c - NA_WIN_COLS // 2, 0, GRID_W - NA_WIN_COLS)
    col_ok = (kc >= qs) & (kc < qs + NA_WIN_COLS)
    dc = np.clip(kc - c + NA_WIN_COLS - 1, 0, n_dc - 1)
    onehot = ((dc[None] == np.arange(n_dc)[:, None, None]) & col_ok[None]).astype(np.float32)
    blocks = jnp.einsum('hrd,dck->hrck', rpb.astype(F32) * LOG2E, onehot, precision=lax.Precision.HIGHEST)
    blocks = jnp.where(col_ok[None, None], blocks, NEG)
    none = jnp.full((NA_HEADS, 1, GRID_W, GRID_W), NEG, F32)
    return jnp.concatenate([jnp.concatenate([none, blocks], axis=1),
                            jnp.concatenate([blocks, none], axis=1)], axis=-1)


_HI_MASK = -65536


def _pack_pairs(x):
    w = x.shape[1] // 2
    r = x.astype(BF16).astype(F32)
    lo = lax.bitcast_convert_type(r[:, :w], jnp.int32)
    hi = lax.bitcast_convert_type(r[:, w:], jnp.int32)
    return (hi & _HI_MASK) | lax.shift_right_logical(lo, 16)


def _unpack_pairs(p):
    lo = lax.bitcast_convert_type(lax.shift_left(p, 16), F32)
    hi = lax.bitcast_convert_type(p & _HI_MASK, F32)
    return lo, hi


def _postmix_kernel(*refs, split_x):
    if split_x:
        o_ref, xp_ref, xs_ref = refs[:3]
        refs = refs[3:]
    else:
        o_ref, x_ref = refs[:2]
        refs = refs[2:]
    (mod_ref, g_ref, wout_ref, wr_ref, br_ref, tri_ref,
     x1_ref, h2_ref, route_ref, gate_ref, cnt_ref, run_ref) = refs
    i = pl.program_id(0)

    @pl.when(i == 0)
    def _():
        run_ref[...] = jnp.zeros_like(run_ref)

    m = mod_ref[0]
    g = g_ref[...]
    gate_gain = g[1:2] * m[2:3]
    ffn_gain = g[2:3] * (1.0 + m[4:5])
    sub_logits = {}

    def project(n):
        return _dot(o_ref[_sub_rows(n), :], wout_ref[...])

    def finish(n, y):
        r = _sub_rows(n)
        x = jnp.where(i < NPT, xp_ref[r, :], xs_ref[r, :]) if split_x else x_ref[r, :]
        x1 = x + _rms(y, gate_gain)
        x1_ref[r, :] = x1
        h2 = _rms(x1, ffn_gain) + m[3:4]
        h2_ref[r, :] = _pack_pairs(h2)
        sub_logits[n] = _dot(h2.astype(BF16), wr_ref[...]) + br_ref[...]

    _pipelined_units([(project, finish, n) for n in range(N_SUB)])

    logits = jnp.concatenate([sub_logits[n] for n in range(N_SUB)], axis=0)
    lane = lax.broadcasted_iota(jnp.int32, logits.shape, 1).astype(F32)
    cur = jnp.where(lane < N_EXPERTS, logits, -jnp.inf)
    tops, idxs = [], []
    for _ in range(TOP_K):
        mx = jnp.max(cur, axis=-1, keepdims=True)
        ix = jnp.min(jnp.where(cur == mx, lane, float(LANES)), axis=-1, keepdims=True)
        tops.append(mx)
        idxs.append(ix)
        cur = jnp.where(lane == ix, -jnp.inf, cur)
    es = [jnp.exp(t - tops[0]) for t in tops]
    inv = 1.0 / (es[0] + es[1] + es[2] + es[3])
    picked = jnp.zeros_like(logits)
    for k in range(TOP_K):
        picked = jnp.where(lane == idxs[k], 1.0, picked)
    before = _dot(tri_ref[...], picked.astype(BF16)) + run_ref[0:1, :]
    route = jnp.zeros_like(logits)
    gate_out = jnp.zeros_like(logits)
    for k in range(TOP_K):
        rank = jnp.sum(jnp.where(lane == idxs[k], before, 0.0), axis=-1, keepdims=True)
        route = jnp.where(lane == float(k), idxs[k], route)
        route = jnp.where(lane == float(TOP_K + k), rank, route)
        gate_out = jnp.where(lane == float(k), es[k] * inv, gate_out)
    route_ref[...] = route.T[:2 * TOP_K].astype(jnp.int32)
    gate_ref[...] = gate_out
    run_ref[...] = run_ref[...] + jnp.sum(picked, axis=0, keepdims=True)
    cnt_ref[...] = run_ref[...].astype(jnp.int32)


def _postmix(o_cat, xs, mod_l, g_l, w_out, w_r, b_r):
    tri = jnp.asarray(np.tril(np.ones((TM, TM), np.float32), -1), BF16)
    split_x = len(xs) == 2
    x_specs = [_PROMPT_SPEC, _SAMPLE_SPEC] if split_x else [_tok_spec(D_MODEL)]
    return pl.pallas_call(
        functools.partial(_postmix_kernel, split_x=split_x),
        grid=(N_TILES,),
        in_specs=[_tok_spec(1024)] + x_specs + [_MOD_SPEC, _const_spec((4, D_MODEL)),
                  _const_spec((1024, D_MODEL)), _const_spec((D_MODEL, LANES)), _const_spec((1, LANES)),
                  _const_spec((TM, TM))],
        out_specs=[_tok_spec(D_MODEL), _tok_spec(D_MODEL // 2), pl.BlockSpec((2 * TOP_K, TM), lambda i: (0, i)),
                   _tok_spec(LANES), _const_spec((8, LANES))],
        out_shape=[jax.ShapeDtypeStruct((N_TOK, D_MODEL), F32),
                   jax.ShapeDtypeStruct((N_TOK, D_MODEL // 2), jnp.int32),
                   jax.ShapeDtypeStruct((2 * TOP_K, N_TOK), jnp.int32), jax.ShapeDtypeStruct((N_TOK, LANES), F32),
                   jax.ShapeDtypeStruct((8, LANES), jnp.int32)],
        scratch_shapes=[pltpu.VMEM((8, LANES), F32)],
        compiler_params=_cparams(1),
        name="postmix",
    )(o_cat, *xs, mod_l, g_l, w_out, w_r, b_r, tri)


SC_WORKERS = 32
SC_ROWS = 64
ROW_WORDS = D_MODEL // 2

_SC_SCRATCH = [pltpu.VMEM((SC_ROWS, ROW_WORDS), jnp.int32), pltpu.VMEM((SC_ROWS, ROW_WORDS), jnp.int32),
               pltpu.SemaphoreType.DMA, pltpu.SemaphoreType.DMA, pltpu.SemaphoreType.DMA, pltpu.SemaphoreType.DMA]


def _sc_worker_id():
    return lax.axis_index("s") * 2 + lax.axis_index("c")


def _sc_double_buffered(n_chunks, load, store):
    for cp in load(0, 0):
        cp.start()

    @pl.loop(0, n_chunks, step=2)
    def _(g0):
        for b in range(2):
            g = g0 + b
            for cp in load(g, b):
                cp.wait()

            @pl.when(g >= 1)
            def _():
                for cp in store(g - 1, 1 - b):
                    cp.wait()

            @pl.when(g + 1 < n_chunks)
            def _():
                for cp in load(g + 1, 1 - b):
                    cp.start()

            for cp in store(g, b):
                cp.start()

    for cp in store(n_chunks - 1, (n_chunks - 1) % 2):
        cp.wait()


def _sc_dispatch(src, idx):
    n_chunks = N_TOK // (SC_WORKERS * SC_ROWS)
    assert n_chunks % 2 == 0
    mesh = plsc.VectorSubcoreMesh(core_axis_name="c", subcore_axis_name="s")

    @functools.partial(
        pl.kernel, mesh=mesh,
        out_type=jax.ShapeDtypeStruct((MOE_ROWS, ROW_WORDS), jnp.int32),
        scratch_types=[pltpu.VMEM((n_chunks * TOP_K, SC_ROWS), jnp.int32)] + _SC_SCRATCH)
    def k(src_hbm, idx_hbm, out_hbm, idx_v, buf0, buf1, in0, in1, out0, out1):
        wid = _sc_worker_id()
        pltpu.sync_copy(idx_hbm.at[wid], idx_v)
        bufs, in_sems, out_sems = (buf0, buf1), (in0, in1), (out0, out1)

        def load(g, b):
            rows = pl.ds((wid * n_chunks + g) * SC_ROWS, SC_ROWS)
            return [pltpu.make_async_copy(src_hbm.at[rows], bufs[b], in_sems[b])]

        def store(g, b):
            return [pltpu.make_async_copy(bufs[b], out_hbm.at[idx_v.at[g * TOP_K + kk]], out_sems[b])
                    for kk in range(TOP_K)]

        _sc_double_buffered(n_chunks, load, store)

    return k(src, idx)


def _sc_collect(table, idx):
    n_chunks = idx.shape[1]
    assert n_chunks % 2 == 0
    mesh = plsc.VectorSubcoreMesh(core_axis_name="c", subcore_axis_name="s")

    @functools.partial(
        pl.kernel, mesh=mesh,
        out_type=jax.ShapeDtypeStruct((SC_WORKERS * n_chunks * SC_ROWS, ROW_WORDS), jnp.int32),
        scratch_types=[pltpu.VMEM((n_chunks, SC_ROWS), jnp.int32)] + _SC_SCRATCH)
    def k(table_hbm, idx_hbm, out_hbm, idx_v, buf0, buf1, in0, in1, out0, out1):
        wid = _sc_worker_id()
        pltpu.sync_copy(idx_hbm.at[wid], idx_v)
        bufs, in_sems, out_sems = (buf0, buf1), (in0, in1), (out0, out1)

        def load(g, b):
            return [pltpu.make_async_copy(table_hbm.at[idx_v.at[g]], bufs[b], in_sems[b])]

        def store(g, b):
            rows = pl.ds((wid * n_chunks + g) * SC_ROWS, SC_ROWS)
            return [pltpu.make_async_copy(bufs[b], out_hbm.at[rows], out_sems[b])]

        _sc_double_buffered(n_chunks, load, store)

    return k(table, idx)


def _expert_rows(words, n_valid, wgu_b, wd_b, bgu, bd):
    live = lax.broadcasted_iota(jnp.int32, words.shape, 0) < n_valid
    lo, hi = _unpack_pairs(jnp.where(live, words, 0))
    x = jnp.concatenate([lo, hi], axis=1).astype(BF16)
    gu = _dot(x, wgu_b[...]) + bgu
    g = jnp.minimum(gu[:, :D_EXPERT], SWIGLU_LIMIT)
    u = jnp.clip(gu[:, D_EXPERT:], -SWIGLU_LIMIT, SWIGLU_LIMIT)
    a = g * jax.nn.sigmoid(SWIGLU_ALPHA * g) * (u + 1.0)
    return _pack_pairs(_dot(a.astype(BF16), wd_b[...]) + bd)


def _moe_kernel(blk_e_ref, blk_first_ref, blk_rows_ref, blk_slot_ref, blk_next_ref,
                x_ref, wgu_hbm, bgu_ref, wd_hbm, bd_ref, y_ref,
                wgu_f, wd_f, wgu_b, wd_b, sem, *, layer):
    i = pl.program_id(0)
    n_valid = blk_rows_ref[i]
    quantum = MOE_TM // MOE_TAIL_PARTS

    def weight_copies(e, slot):
        return (pltpu.make_async_copy(wgu_hbm.at[layer, e], wgu_f.at[slot], sem.at[0, slot]),
                pltpu.make_async_copy(wd_hbm.at[layer, e], wd_f.at[slot], sem.at[1, slot]))

    @pl.when(i == 0)
    def _():
        for cp in weight_copies(blk_e_ref[0], blk_slot_ref[0]):
            cp.start()

    @pl.when(blk_first_ref[i] == 1)
    def _():
        slot = blk_slot_ref[i]
        for cp in weight_copies(blk_e_ref[i], slot):
            cp.wait()
        nxt = blk_next_ref[i]

        @pl.when(nxt >= 0)
        def _():
            for cp in weight_copies(nxt, 1 - slot):
                cp.start()

        wgu_b[...] = wgu_f[slot].astype(BF16)
        wd_b[...] = wd_f[slot].astype(BF16)

    for parts in range(1, MOE_TAIL_PARTS + 1):
        rows = parts * quantum

        @pl.when((n_valid > rows - quantum) & (n_valid <= rows))
        def _(rows=rows):
            y_ref[:rows] = _expert_rows(x_ref[:rows], n_valid, wgu_b, wd_b, bgu_ref[0, 0], bd_ref[0, 0])
            if rows < MOE_TM:
                y_ref[rows:] = jnp.zeros((MOE_TM - rows, ROW_WORDS), jnp.int32)

    @pl.when(n_valid == 0)
    def _():
        y_ref[...] = jnp.zeros_like(y_ref)


def _moe(layer, blk_meta, xs, w_gu, b_gu, w_down, b_down):
    def row_map(i, *_):
        return (i, 0)

    def bias_map(i, e, *_):
        return (layer, e[i], 0, 0)

    grid_spec = pltpu.PrefetchScalarGridSpec(
        num_scalar_prefetch=5,
        grid=(MOE_BLOCKS,),
        in_specs=[
            pl.BlockSpec((MOE_TM, ROW_WORDS), row_map),
            pl.BlockSpec(memory_space=pl.ANY),
            pl.BlockSpec((1, 1, 1, 2 * D_EXPERT), bias_map),
            pl.BlockSpec(memory_space=pl.ANY),
            pl.BlockSpec((1, 1, 1, D_MODEL), bias_map),
        ],
        out_specs=pl.BlockSpec((MOE_TM, ROW_WORDS), row_map),
        scratch_shapes=[pltpu.VMEM((2, D_MODEL, 2 * D_EXPERT), F32), pltpu.VMEM((2, D_EXPERT, D_MODEL), F32),
                        pltpu.VMEM((D_MODEL, 2 * D_EXPERT), BF16), pltpu.VMEM((D_EXPERT, D_MODEL), BF16),
                        pltpu.SemaphoreType.DMA((2, 2))],
    )
    return pl.pallas_call(
        functools.partial(_moe_kernel, layer=layer),
        grid_spec=grid_spec,
        out_shape=jax.ShapeDtypeStruct((MOE_ROWS, ROW_WORDS), jnp.int32),
        compiler_params=_cparams(1),
        name="moe_experts",
    )(*blk_meta, xs, w_gu, b_gu.reshape(DEPTH, N_EXPERTS, 1, 2 * D_EXPERT),
      w_down, b_down.reshape(DEPTH, N_EXPERTS, 1, D_MODEL))


def _route(route, counts):
    experts = jnp.arange(N_EXPERTS, dtype=jnp.int32)
    padded = (counts + MOE_TM - 1) // MOE_TM * MOE_TM
    pad_end = jnp.cumsum(padded)
    pad_start = pad_end - padded
    e = route[0:TOP_K]
    onehot = e[:, :, None] == experts[None, None, :]
    dest = jnp.sum(jnp.where(onehot, pad_start[None, None, :], 0), axis=-1) + route[TOP_K:2 * TOP_K]
    blk_row0 = jnp.arange(MOE_BLOCKS, dtype=jnp.int32) * MOE_TM
    blk_e = jnp.minimum(jnp.sum((pad_end[None, :] <= blk_row0[:, None]).astype(jnp.int32), axis=1),
                        N_EXPERTS - 1)
    on = blk_row0 < pad_end[-1]
    n_on = jnp.sum(on.astype(jnp.int32))
    blk_onehot = blk_e[:, None] == experts[None, :]
    row_end = jnp.sum(jnp.where(blk_onehot, (pad_start + counts)[None, :], 0), axis=1)
    blk_rows = jnp.where(on, jnp.clip(row_end - blk_row0, 0, MOE_TM), 0).astype(jnp.int32)
    last_e = jnp.sum(jnp.where(jnp.arange(MOE_BLOCKS) == n_on - 1, blk_e, 0))
    blk_e = jnp.where(on, blk_e, last_e).astype(jnp.int32)
    prev = jnp.concatenate([jnp.full((1,), -1, jnp.int32), blk_e[:-1]])
    blk_first = (blk_e != prev).astype(jnp.int32)
    blk_slot = ((jnp.cumsum(blk_first) - 1) % 2).astype(jnp.int32)
    later_used = (experts[None, :] > experts[:, None]) & (counts[None, :] > 0)
    next_used = jnp.min(jnp.where(later_used, experts[None, :], N_EXPERTS), axis=1)
    next_used = jnp.where(next_used == N_EXPERTS, -1, next_used)
    blk_next = jnp.sum(jnp.where(blk_e[:, None] == experts[None, :], next_used[None, :], 0), axis=1)
    return dest, (blk_e, blk_first, blk_rows, blk_slot, blk_next.astype(jnp.int32))


def _combine_kernel(y_ref, gate_ref, x1_ref, mod_ref, g_ref, *o_refs):
    i = pl.program_id(0)
    m = mod_ref[0]
    gate = gate_ref[...]
    acc_lo = None
    for k in range(TOP_K):
        lo, hi = _unpack_pairs(y_ref[k])
        gk = gate[:, k:k + 1]
        acc_lo = gk * lo if acc_lo is None else acc_lo + gk * lo
        acc_hi = gk * hi if k == 0 else acc_hi + gk * hi
    acc = jnp.concatenate([acc_lo, acc_hi], axis=1)
    out = x1_ref[...] + _rms(acc, g_ref[...][3:4] * m[5:6])
    if len(o_refs) == 1:
        o_refs[0][...] = out
    else:
        @pl.when(i < NPT)
        def _():
            o_refs[0][...] = out

        @pl.when(i >= NPT)
        def _():
            o_refs[1][...] = out


def _combine(yg, gates, x1, mod_l, g_l, split_out):
    if split_out:
        out_specs = [_PROMPT_SPEC, _SAMPLE_SPEC]
        out_shape = [jax.ShapeDtypeStruct((NP_TOK, D_MODEL), F32), jax.ShapeDtypeStruct((NS_TOK, D_MODEL), F32)]
    else:
        out_specs = [_tok_spec(D_MODEL)]
        out_shape = [jax.ShapeDtypeStruct((N_TOK, D_MODEL), F32)]
    return pl.pallas_call(
        _combine_kernel,
        grid=(N_TILES,),
        in_specs=[pl.BlockSpec((TOP_K, TM, ROW_WORDS), lambda i: (0, i, 0)), _tok_spec(LANES),
                  _tok_spec(D_MODEL), _MOD_SPEC, _const_spec((4, D_MODEL))],
        out_specs=out_specs,
        out_shape=out_shape,
        compiler_params=_cparams(1),
        name="combine",
    )(yg, gates, x1, mod_l, g_l)


def _ffn(layer, o_cat, xs, mod_l, g_l, w_out, w_router, b_router, w_gu, b_gu, w_down, b_down, split_out):
    w_r = jnp.pad(w_router, ((0, 0), (0, LANES - N_EXPERTS))).astype(BF16)
    b_r = jnp.pad(b_router, (0, LANES - N_EXPERTS)).reshape(1, LANES)
    x1, h2p, route, gate_slab, counts = _postmix(o_cat, xs, mod_l, g_l, w_out.astype(BF16), w_r, b_r)
    dest, blk_meta = _route(route, counts[0, :N_EXPERTS])
    n_chunks = N_TOK // (SC_WORKERS * SC_ROWS)
    idx_d = dest.reshape(TOP_K, SC_WORKERS, n_chunks, SC_ROWS).transpose(1, 2, 0, 3).reshape(
        SC_WORKERS, n_chunks * TOP_K, SC_ROWS)
    rows = _sc_dispatch(h2p, idx_d)
    ys = _moe(layer, blk_meta, rows, w_gu, b_gu, w_down, b_down)
    idx_c = dest.reshape(SC_WORKERS, TOP_K * n_chunks, SC_ROWS)
    yg = _sc_collect(ys, idx_c).reshape(TOP_K, N_TOK, ROW_WORDS)
    return _combine(yg, gate_slab, x1, mod_l, g_l, split_out)


def _pad_heads(w, n_heads, width, keep):
    k = w.shape[0]
    w = w.reshape(k, n_heads, width)[:, :, :keep]
    return jnp.pad(w, ((0, 0), (0, 0), (0, LANES - keep))).reshape(k, n_heads * LANES)


def _pe_slab(x):
    return jnp.pad(x, [(0, 0)] * (x.ndim - 1) + [(MLA_NOPE, LANES - MLA_NOPE - MLA_ROPE)])


def _pair_kv_heads(w):
    g = SWA_HEADS // SWA_KV_HEADS
    return w.reshape(SWA_KV_HEADS, g, HEAD_DIM, -1).transpose(1, 0, 2, 3).reshape(w.shape)


def kernel(x_prompt, x_sample, cache_mla_ckv, cache_mla_krope, cache_diff_k, cache_diff_v, cache_swa_k, cache_swa_v, cache_na_k, cache_na_v, c, c_ctx, w_mod, b_mod, norm_g, w_in0, mla_q_norm, w_uq, mla_kv_norm, w_ukv, diff_lambda, diff_norm, w_out0, w_in1, swa_sink, na_rpb, w_out1, w_router, b_router, w_gu, b_gu, w_down, b_down):
    xs = (x_prompt.reshape(NP_TOK, D_MODEL), x_sample.reshape(NS_TOK, D_MODEL))
    cond = jnp.concatenate([c_ctx[None, :], c, jnp.zeros((16 - 1 - DEC_BATCH, D_MODEL), F32)], axis=0)
    mod = _modulation(cond, w_mod, b_mod).reshape(DEPTH, 16, 6, D_MODEL)
    t64, t32 = _rope_tables()
    states = {}
    for l in range(DEPTH):
        i = l // 2
        g_l = norm_g[l]
        mod_l = mod[l]
        if l % 2 == 0:
            lam_init = 0.8 - 0.6 * math.exp(-0.3 * l)
            wi = w_in0[i]
            w_in_p = jnp.concatenate(
                [wi[:, 0:640], wi[:, 672:2208], _pe_slab(wi[:, 640:672])], axis=1).astype(BF16)
            w_uq_p = _pad_heads(w_uq[i], MLA_HEADS, MLA_NOPE + MLA_ROPE, MLA_NOPE + MLA_ROPE).astype(BF16)
            w_k_p = _pad_heads(w_ukv[i], MLA_HEADS, MLA_NOPE + MLA_V, MLA_NOPE).astype(BF16)
            w_v = w_ukv[i].reshape(MLA_KV_RANK, MLA_HEADS, MLA_NOPE + MLA_V)[:, :, MLA_NOPE:].reshape(
                MLA_KV_RANK, MLA_HEADS * MLA_V).astype(BF16)
            (q, k, v, dq, dk, dv, ckv_st, kpe_st, dk_st, dv_st) = _premix0(
                *xs, mod_l, g_l[0:1], w_in_p, mla_q_norm[i][None, :], w_uq_p, mla_kv_norm[i][None, :],
                w_k_p, w_v, t32, t64)
            states['mla_ckv'] = ckv_st.reshape(BATCH, 1, SEQ, MLA_KV_RANK)
            states['mla_krope'] = kpe_st[:, MLA_NOPE:MLA_NOPE + MLA_ROPE].reshape(BATCH, 1, SEQ, MLA_ROPE)
            states['diff_k'] = dk_st.reshape(BATCH, 1, SEQ, DIFF_HEADS, 2 * DIFF_DH)
            states['diff_v'] = dv_st.reshape(BATCH, 1, SEQ, DIFF_HEADS, 2 * DIFF_DH)
            kc, vc = _mla_cache(cache_mla_ckv[:, i].reshape(DEC_BATCH * PAST_LEN, MLA_KV_RANK),
                                _pe_slab(cache_mla_krope[:, i].reshape(DEC_BATCH * PAST_LEN, MLA_ROPE)),
                                w_k_p, w_v)
            dkc = cache_diff_k[:, i].reshape(DEC_BATCH * PAST_LEN, 512).astype(BF16)
            dvc = cache_diff_v[:, i].reshape(DEC_BATCH * PAST_LEN, 512).astype(BF16)
            lam = diff_lambda[i]
            sub_g = diff_norm[i][None, :]
            o_p = _attn_ab(q, dq, (k, v, dk, dv), None, lam, sub_g, lam_init,
                           n_batch=BATCH, t_len=SEQ, tq=SEQ, tok_off=0)
            o_cat = _attn_ab(q, dq, (k, v, dk, dv), (kc, vc, dkc, dvc), lam, sub_g, lam_init,
                             n_batch=DEC_BATCH, t_len=DEC_SEQ, tq=TQ_AB, tok_off=NP_TOK, out_init=o_p)
            w_out = w_out0[i]
        else:
            wi = w_in1[i]
            n_sq = SWA_HEADS * HEAD_DIM
            w_in_p = jnp.concatenate([_pair_kv_heads(wi[:, :n_sq].T).T, wi[:, n_sq:]], axis=1).astype(BF16)
            (sq, sk, sv, nq, nk, nv, sk_st, sv_st, nk_st, nv_st) = _premix1(xs[0], mod_l, g_l[0:1], w_in_p, t64)
            states['swa_k'] = sk_st.reshape(BATCH, 1, SEQ, SWA_KV_HEADS, HEAD_DIM)
            states['swa_v'] = sv_st.reshape(BATCH, 1, SEQ, SWA_KV_HEADS, HEAD_DIM)
            states['na_k'] = nk_st.reshape(BATCH, 1, SEQ, NA_HEADS, HEAD_DIM)
            states['na_v'] = nv_st.reshape(BATCH, 1, SEQ, NA_HEADS, HEAD_DIM)
            skc = cache_swa_k[:, i].reshape(DEC_BATCH * PAST_LEN, 128).astype(BF16)
            svc = cache_swa_v[:, i].reshape(DEC_BATCH * PAST_LEN, 128).astype(BF16)
            nkc = cache_na_k[:, i].reshape(DEC_BATCH * PAST_LEN, 512).astype(BF16)
            nvc = cache_na_v[:, i].reshape(DEC_BATCH * PAST_LEN, 512).astype(BF16)
            sink = swa_sink[i]
            o_p = _attn_cd_prompt(sink, sq, sk, sv, nq, nk, nv)
            o_cat = _attn_cd_sample(o_p, sink, sq, sk, sv, nq, nk, nv, skc, svc, nkc, nvc, _na_bias(na_rpb[i]))
            wo = w_out1[i]
            w_out = jnp.concatenate([_pair_kv_heads(wo[:n_sq]), wo[n_sq:]], axis=0)
        xs = _ffn(l, o_cat, xs, mod_l, g_l, w_out, w_router[l], b_router[l], w_gu, b_gu, w_down, b_down,
                  split_out=(l == DEPTH - 1))
    return (xs[0].reshape(BATCH, SEQ, D_MODEL), xs[1].reshape(DEC_BATCH, DEC_SEQ, D_MODEL),
            states['mla_ckv'], states['mla_krope'], states['diff_k'], states['diff_v'],
            states['swa_k'], states['swa_v'], states['na_k'], states['na_v'])
```

```python
import functools
import math

import numpy as np
import jax
import jax.numpy as jnp
from jax import lax
from jax.experimental import pallas as pl
from jax.experimental.pallas import tpu as pltpu
from jax.experimental.pallas import tpu_sc as plsc

F32 = jnp.float32
BF16 = jnp.bfloat16

D_MODEL = 1024
BATCH = 16
SEQ = 256
DEPTH = 2
DEC_BATCH = 8
DEC_SEQ = 2048
PAST_LEN = 256
GRID_W = 64
HEAD_DIM = 64
ROPE_THETA = 10000.0
EPS = 1e-6
NEG = -1e30

MLA_HEADS = 8
MLA_Q_RANK = 384
MLA_KV_RANK = 256
MLA_NOPE = 64
MLA_ROPE = 32
MLA_V = 64
DIFF_HEADS = 4
DIFF_DH = 64
SWA_HEADS = 8
SWA_KV_HEADS = 2
SWA_WINDOW = 128
NA_HEADS = 8
NA_WIN_ROWS = 8
NA_WIN_COLS = 16
N_EXPERTS = 32
TOP_K = 4
D_EXPERT = 1024
SWIGLU_LIMIT = 7.0
SWIGLU_ALPHA = 1.702

LANES = 128
NP_TOK = BATCH * SEQ
NS_TOK = DEC_BATCH * DEC_SEQ
N_TOK = NP_TOK + NS_TOK
TM = 512
NPT = NP_TOK // TM
TILES_PER_SAMPLE = DEC_SEQ // TM
N_TILES = N_TOK // TM
N_SUB = 2
SUB_TM = TM // N_SUB
TQ = 256
TQ_AB = 256
PROMPT_SEQS_PER_STEP = 2
MOE_TM = 1024
MOE_TAIL_PARTS = 8
MOE_ROWS = ((N_TOK * TOP_K + N_EXPERTS * (MOE_TM - 1)) // MOE_TM + 1) * MOE_TM
MOE_BLOCKS = MOE_ROWS // MOE_TM
NA_TILE_ROWS = TQ // GRID_W
NA_KEY_ROWS = 12
VMEM_LIMIT = 56 * 1024 * 1024


def _cparams(n_axes, vmem=VMEM_LIMIT):
    return pltpu.CompilerParams(dimension_semantics=("arbitrary",) * n_axes,
                                vmem_limit_bytes=vmem)


def _rms(x, g):
    return x * lax.rsqrt(jnp.mean(x * x, axis=-1, keepdims=True) + EPS) * g


def _dot(a, b):
    return jnp.dot(a, b, preferred_element_type=F32)


def _dot_nt(a, b):
    return lax.dot_general(a, b, (((1,), (1,)), ((), ())), preferred_element_type=F32)


def _rope(x, cos, sin_a, sin_b, half):
    return (x * cos + pltpu.roll(x, LANES - half, 1) * sin_a + pltpu.roll(x, half, 1) * sin_b)


def _pipelined_units(units):
    pending = units[0][0](units[0][2])
    for n, (_, finish, arg) in enumerate(units):
        following = units[n + 1][0](units[n + 1][2]) if n + 1 < len(units) else None
        finish(arg, pending)
        pending = following


def _sub_rows(n):
    return slice(SUB_TM * n, SUB_TM * (n + 1))


def _store_head_rows(st_ref, n, hd, n_heads, x):
    st_ref[pl.ds(SUB_TM * n * n_heads + hd, SUB_TM, stride=n_heads), :] = x


def _mod_row(i):
    return jnp.where(i < NPT, 0, 1 + (i - NPT) // TILES_PER_SAMPLE)


def _rope_blk(i):
    return jnp.where(i < NPT, TILES_PER_SAMPLE, (i - NPT) % TILES_PER_SAMPLE)


def _mod_kernel(c_ref, w_ref, b_ref, o_ref):
    c = c_ref[...]
    s = (c * jax.nn.sigmoid(c)).astype(BF16)
    o_ref[0] = _dot(s, w_ref[0].astype(BF16)) + b_ref[0]


def _modulation(cond, w_mod, b_mod):
    nb = 1024
    return pl.pallas_call(
        _mod_kernel,
        grid=(DEPTH, 6 * D_MODEL // nb),
        in_specs=[
            pl.BlockSpec((16, D_MODEL), lambda l, n: (0, 0)),
            pl.BlockSpec((1, D_MODEL, nb), lambda l, n: (l, 0, n)),
            pl.BlockSpec((1, 1, nb), lambda l, n: (l, 0, n)),
        ],
        out_specs=pl.BlockSpec((1, 16, nb), lambda l, n: (l, 0, n)),
        out_shape=jax.ShapeDtypeStruct((DEPTH, 16, 6 * D_MODEL), F32),
        compiler_params=_cparams(2),
        name="modulation",
    )(cond, w_mod, b_mod.reshape(DEPTH, 1, 6 * D_MODEL))


def _rope_tables():
    t = jnp.arange(DEC_SEQ)
    rows = (t // GRID_W).astype(F32)
    cols = (t % GRID_W).astype(F32)

    def angles(r):
        n = r // 4
        inv = ROPE_THETA ** (-jnp.arange(n, dtype=F32) / n)
        return jnp.concatenate([rows[:, None] * inv[None], cols[:, None] * inv[None]], axis=-1)

    def finish(cos, sa, sb):
        ident = (jnp.ones((TM, LANES), F32), jnp.zeros((TM, LANES), F32), jnp.zeros((TM, LANES), F32))
        return tuple(jnp.concatenate([a, b], axis=0) for a, b in zip((cos, sa, sb), ident))

    a64 = angles(64)
    c, s, z = jnp.cos(a64), jnp.sin(a64), jnp.zeros_like(a64)
    t64 = finish(jnp.concatenate([c, c, c, c], -1), jnp.concatenate([-s, z, -s, z], -1),
                 jnp.concatenate([z, s, z, s], -1))
    a32 = angles(32)
    c, s, z = jnp.cos(a32), jnp.sin(a32), jnp.zeros_like(a32)
    one64 = jnp.ones((DEC_SEQ, 64), F32)
    z64 = jnp.zeros((DEC_SEQ, 64), F32)
    z32 = jnp.zeros((DEC_SEQ, 32), F32)
    t32 = finish(jnp.concatenate([one64, c, c, z32], -1), jnp.concatenate([z64, -s, z, z32], -1),
                 jnp.concatenate([z64, z, s, z32], -1))
    return t64, t32


LOG2E = math.log2(math.e)
_DIFF_COLS = DIFF_HEADS * 2 * DIFF_DH
_AB_COLS = tuple(np.cumsum([0, MLA_Q_RANK, MLA_KV_RANK, _DIFF_COLS, _DIFF_COLS, _DIFF_COLS, LANES]).tolist())
_CD_COLS = tuple(np.cumsum([0, SWA_HEADS * HEAD_DIM, SWA_KV_HEADS * HEAD_DIM, SWA_KV_HEADS * HEAD_DIM,
                            NA_HEADS * HEAD_DIM, NA_HEADS * HEAD_DIM, NA_HEADS * HEAD_DIM]).tolist())
IN_COLS = _AB_COLS[-1]
assert IN_COLS == _CD_COLS[-1]
_MLA_SCALE = (MLA_NOPE + MLA_ROPE) ** -0.5 * LOG2E
_QSCALE = HEAD_DIM ** -0.5 * LOG2E


def _premix0_kernel(xp_ref, xs_ref, mod_ref, g_ref, win_ref, qn_ref, wuq_ref, kvn_ref, wk_ref, wv_ref,
                    c32_ref, sa32_ref, sb32_ref, c64_ref, sa64_ref, sb64_ref,
                    q_ref, k_ref, v_ref, dq_ref, dk_ref, dv_ref,
                    ckv_st, kpe_st, dk_st, dv_st):
    i = pl.program_id(0)
    m = mod_ref[0]
    gain = g_ref[...] * (1.0 + m[1:2])

    def project(n):
        r = _sub_rows(n)
        x = jnp.where(i < NPT, xp_ref[r, :], xs_ref[r, :])
        h = _rms(x, gain) + m[0:1]
        return _dot(h.astype(BF16), win_ref[...])

    states = {}

    def finish(n, proj):
        r = _sub_rows(n)
        q_a, kv_a, dq, dk, dv, pe = (proj[:, a:b] for a, b in zip(_AB_COLS[:-1], _AB_COLS[1:]))
        q = _dot(_rms(q_a, qn_ref[...]).astype(BF16), wuq_ref[...])
        ckv = _rms(kv_a, kvn_ref[...])
        ckv_b = ckv.astype(BF16)
        kn = _dot(ckv_b, wk_ref[...])
        v_ref[r, :] = _dot(ckv_b, wv_ref[...]).astype(BF16)
        c32, sa32, sb32 = c32_ref[r, :], sa32_ref[r, :], sb32_ref[r, :]
        c64, sa64, sb64 = c64_ref[r, :], sa64_ref[r, :], sb64_ref[r, :]
        pe_r = _rope(pe, c32, sa32, sb32, MLA_ROPE // 2)
        for hd in range(MLA_HEADS):
            sl = slice(LANES * hd, LANES * (hd + 1))
            q_ref[r, sl] = (_rope(q[:, sl], c32, sa32, sb32, MLA_ROPE // 2) * _MLA_SCALE).astype(BF16)
            k_ref[r, sl] = (kn[:, sl] + pe_r).astype(BF16)
        for hd in range(DIFF_HEADS):
            sl = slice(LANES * hd, LANES * (hd + 1))
            dq_ref[r, sl] = (_rope(dq[:, sl], c64, sa64, sb64, DIFF_DH // 2) * _QSCALE).astype(BF16)
            dk_ref[r, sl] = _rope(dk[:, sl], c64, sa64, sb64, DIFF_DH // 2).astype(BF16)
        dv_ref[r, :] = dv.astype(BF16)
        states[n] = (ckv, pe, dk, dv)

    _pipelined_units([(project, finish, n) for n in range(N_SUB)])

    @pl.when(i < NPT)
    def _():
        for n in range(N_SUB):
            r = _sub_rows(n)
            ckv, pe, dk, dv = states[n]
            ckv_st[r, :] = ckv
            kpe_st[r, :] = pe
            for hd in range(DIFF_HEADS):
                _store_head_rows(dk_st, n, hd, DIFF_HEADS, dk[:, LANES * hd: LANES * (hd + 1)])
                _store_head_rows(dv_st, n, hd, DIFF_HEADS, dv[:, LANES * hd: LANES * (hd + 1)])


def _premix1_kernel(x_ref, mod_ref, g_ref, win_ref, c64_ref, sa64_ref, sb64_ref,
                    sq_ref, sk_ref, sv_ref, nq_ref, nk_ref, nv_ref,
                    sk_st, sv_st, nk_st, nv_st):
    i = pl.program_id(0)
    m = mod_ref[0]
    gain = g_ref[...] * (1.0 + m[1:2])

    def project(n):
        h = _rms(x_ref[_sub_rows(n), :], gain) + m[0:1]
        return _dot(h.astype(BF16), win_ref[...])

    states = {}

    def finish(n, proj):
        r = _sub_rows(n)
        sq, sk, sv, nq, nk, nv = (proj[:, a:b] for a, b in zip(_CD_COLS[:-1], _CD_COLS[1:]))
        c64, sa64, sb64 = c64_ref[r, :], sa64_ref[r, :], sb64_ref[r, :]
        for hd in range(4):
            sl = slice(LANES * hd, LANES * (hd + 1))
            sq_ref[r, sl] = (_rope(sq[:, sl], c64, sa64, sb64, HEAD_DIM // 2) * _QSCALE).astype(BF16)
        sk_ref[r, :] = _rope(sk, c64, sa64, sb64, HEAD_DIM // 2).astype(BF16)
        sv_ref[r, :] = sv.astype(BF16)
        nq_ref[r, :] = (nq * _QSCALE).astype(BF16)
        nk_ref[r, :] = nk.astype(BF16)
        nv_ref[r, :] = nv.astype(BF16)
        states[n] = (sk, sv, nk, nv)

    _pipelined_units([(project, finish, n) for n in range(N_SUB)])

    @pl.when(i < NPT)
    def _():
        for n in range(N_SUB):
            r = _sub_rows(n)
            sk, sv, nk, nv = states[n]
            sk_st[r, :] = sk
            sv_st[r, :] = sv
            for hd in range(NA_HEADS):
                _store_head_rows(nk_st, n, hd, NA_HEADS, nk[:, HEAD_DIM * hd: HEAD_DIM * (hd + 1)])
                _store_head_rows(nv_st, n, hd, NA_HEADS, nv[:, HEAD_DIM * hd: HEAD_DIM * (hd + 1)])


def _tok_spec(width):
    return pl.BlockSpec((TM, width), lambda i: (i, 0))


_PROMPT_SPEC = pl.BlockSpec((TM, D_MODEL), lambda i: (jnp.minimum(i, NPT - 1), 0))
_SAMPLE_SPEC = pl.BlockSpec((TM, D_MODEL), lambda i: (jnp.maximum(i - NPT, 0), 0))


def _state_spec(width, rows_per_token=1):
    return pl.BlockSpec((TM * rows_per_token, width), lambda i: (jnp.minimum(i, NPT - 1), 0))


def _const_spec(shape):
    return pl.BlockSpec(shape, lambda i: (0,) * len(shape))


_MOD_SPEC = pl.BlockSpec((1, 6, D_MODEL), lambda i: (_mod_row(i), 0, 0))
_ROPE_SPEC = pl.BlockSpec((TM, LANES), lambda i: (_rope_blk(i), 0))


def _premix0(xp, xs, mod_l, g0, w_in_p, q_norm, w_uq_p, kv_norm, w_k_p, w_v, t32, t64):
    outs = [(N_TOK, 1024, BF16), (N_TOK, 1024, BF16), (N_TOK, 512, BF16), (N_TOK, 512, BF16),
            (N_TOK, 512, BF16), (N_TOK, 512, BF16),
            (NP_TOK, 256, F32), (NP_TOK, 128, F32),
            (NP_TOK * DIFF_HEADS, 2 * DIFF_DH, F32), (NP_TOK * DIFF_HEADS, 2 * DIFF_DH, F32)]
    return pl.pallas_call(
        _premix0_kernel,
        grid=(N_TILES,),
        in_specs=[_PROMPT_SPEC, _SAMPLE_SPEC, _MOD_SPEC, _const_spec((1, D_MODEL)),
                  _const_spec((D_MODEL, IN_COLS)), _const_spec((1, MLA_Q_RANK)),
                  _const_spec((MLA_Q_RANK, 1024)), _const_spec((1, MLA_KV_RANK)),
                  _const_spec((MLA_KV_RANK, 1024)), _const_spec((MLA_KV_RANK, 512))]
                 + [_ROPE_SPEC] * 6,
        out_specs=([_tok_spec(w) for (_, w, _) in outs[:6]]
                   + [_state_spec(w, n // NP_TOK) for (n, w, _) in outs[6:]]),
        out_shape=[jax.ShapeDtypeStruct((n, w), dt) for (n, w, dt) in outs],
        compiler_params=_cparams(1),
        name="premix_ab",
    )(xp, xs, mod_l, g0, w_in_p, q_norm, w_uq_p, kv_norm, w_k_p, w_v, *t32, *t64)


def _premix1(x, mod_l, g0, w_in_p, t64):
    outs = [(N_TOK, 512, BF16), (N_TOK, 128, BF16), (N_TOK, 128, BF16), (N_TOK, 512, BF16),
            (N_TOK, 512, BF16), (N_TOK, 512, BF16),
            (NP_TOK, 128, F32), (NP_TOK, 128, F32),
            (NP_TOK * NA_HEADS, HEAD_DIM, F32), (NP_TOK * NA_HEADS, HEAD_DIM, F32)]
    return pl.pallas_call(
        _premix1_kernel,
        grid=(N_TILES,),
        in_specs=[_tok_spec(D_MODEL), _MOD_SPEC, _const_spec((1, D_MODEL)),
                  _const_spec((D_MODEL, IN_COLS))] + [_ROPE_SPEC] * 3,
        out_specs=([_tok_spec(w) for (_, w, _) in outs[:6]]
                   + [_state_spec(w, n // NP_TOK) for (n, w, _) in outs[6:]]),
        out_shape=[jax.ShapeDtypeStruct((n, w), dt) for (n, w, dt) in outs],
        compiler_params=_cparams(1),
        name="premix_cd",
    )(x, mod_l, g0, w_in_p, *t64)


def _mla_cache_kernel(ckv_ref, pe_ref, wk_ref, wv_ref, k_ref, v_ref):
    c = ckv_ref[...].astype(BF16)
    kn = _dot(c, wk_ref[...])
    v_ref[...] = _dot(c, wv_ref[...]).astype(BF16)
    pe = pe_ref[...]
    for hd in range(MLA_HEADS):
        sl = slice(LANES * hd, LANES * (hd + 1))
        k_ref[:, sl] = (kn[:, sl] + pe).astype(BF16)


def _mla_cache(ckv, pe_slab, w_k_p, w_v):
    n = ckv.shape[0]
    tm = 512
    return pl.pallas_call(
        _mla_cache_kernel,
        grid=(n // tm,),
        in_specs=[pl.BlockSpec((tm, MLA_KV_RANK), lambda i: (i, 0)),
                  pl.BlockSpec((tm, LANES), lambda i: (i, 0)),
                  _const_spec((MLA_KV_RANK, 1024)), _const_spec((MLA_KV_RANK, 512))],
        out_specs=[pl.BlockSpec((tm, 1024), lambda i: (i, 0)), pl.BlockSpec((tm, 512), lambda i: (i, 0))],
        out_shape=[jax.ShapeDtypeStruct((n, 1024), BF16), jax.ShapeDtypeStruct((n, 512), BF16)],
        compiler_params=_cparams(1),
        name="mla_cache",
    )(ckv, pe_slab, w_k_p, w_v)


def _softmax_pv(scores, values, sink=None):
    m = jnp.max(scores[0], axis=-1, keepdims=True)
    for s in scores[1:]:
        m = jnp.maximum(m, jnp.max(s, axis=-1, keepdims=True))
    if sink is not None:
        m = jnp.maximum(m, sink)
    l = None
    o = None
    for s, v in zip(scores, values):
        p = jnp.exp2(s - m)
        ls = jnp.sum(p, axis=-1, keepdims=True)
        os_ = _dot(p.astype(BF16), v)
        l = ls if l is None else l + ls
        o = os_ if o is None else o + os_
    if sink is not None:
        l = l + jnp.exp2(sink - m)
    return o * (1.0 / l)


def _lane_lo(shape):
    return lax.broadcasted_iota(jnp.int32, shape, 1) < (LANES // 2)


def _split_halves(qb):
    lo = _lane_lo(qb.shape)
    zero = jnp.zeros_like(qb)
    return jnp.where(lo, qb, zero), jnp.where(lo, zero, qb)


def _attn_ab_kernel(*refs, n_pieces, n_seq, lam_init, aliased):
    if aliased:
        refs = refs[1:]
    q_ref, dq_ref = refs[0], refs[1]
    pieces = [refs[2 + 4 * p: 6 + 4 * p] for p in range(n_pieces)]
    lam_ref, subg_ref, o_ref = refs[2 + 4 * n_pieces:]
    lam = lam_ref[...]
    lam_full = (jnp.exp(jnp.sum(lam[0:1] * lam[1:2], axis=-1, keepdims=True))
                - jnp.exp(jnp.sum(lam[2:3] * lam[3:4], axis=-1, keepdims=True)) + lam_init)
    tq = q_ref.shape[0] // n_seq
    lo = _lane_lo((tq, LANES))
    subg = subg_ref[...] * (1.0 - lam_init)

    def seq_rows(ref, sq):
        n = ref.shape[0] // n_seq
        return slice(n * sq, n * (sq + 1))

    def mla_scores(arg):
        sq, hd = arg
        sl = slice(LANES * hd, LANES * (hd + 1))
        qh = q_ref[seq_rows(q_ref, sq), sl]
        return [_dot_nt(qh, k_ref[seq_rows(k_ref, sq), sl]) for (k_ref, _, _, _) in pieces]

    def diff_scores(arg):
        sq, hd = arg
        sl = slice(LANES * hd, LANES * (hd + 1))
        qq = jnp.concatenate(_split_halves(dq_ref[seq_rows(dq_ref, sq), sl]), axis=0)
        return [_dot_nt(qq, dk_ref[seq_rows(dk_ref, sq), sl]) for (_, _, dk_ref, _) in pieces]

    pair = {}

    def mla_finish(arg, scores):
        sq, hd = arg
        j = hd // 2
        vals = [v_ref[seq_rows(v_ref, sq), LANES * j: LANES * (j + 1)] for (_, v_ref, _, _) in pieces]
        pair[hd % 2] = _softmax_pv(scores, vals)
        if hd % 2 == 1:
            o_ref[seq_rows(o_ref, sq), LANES * j: LANES * (j + 1)] = jnp.where(lo, pair[0], pair[1]).astype(BF16)

    def diff_finish(arg, scores):
        sq, hd = arg
        sl = slice(LANES * hd, LANES * (hd + 1))
        oo = _softmax_pv(scores, [dv_ref[seq_rows(dv_ref, sq), sl] for (_, _, _, dv_ref) in pieces])
        od = _rms(oo[:tq] - lam_full * oo[tq:], subg)
        o_ref[seq_rows(o_ref, sq), 512 + LANES * hd: 512 + LANES * (hd + 1)] = od.astype(BF16)

    units = []
    for sq in range(n_seq):
        for j in range(DIFF_HEADS):
            units += [(diff_scores, diff_finish, (sq, j)), (mla_scores, mla_finish, (sq, 2 * j)),
                      (mla_scores, mla_finish, (sq, 2 * j + 1))]
    _pipelined_units(units)


def _attn_ab(q, dq, new_kv, cache_kv, lam, sub_g, lam_init, *, n_batch, t_len, tq, tok_off, out_init=None,
             n_seq=1):
    assert n_seq == 1 or (tq == t_len and cache_kv is None and n_batch % n_seq == 0)
    n_batch, t_len, tq = n_batch // n_seq, t_len * n_seq, tq * n_seq
    nq = t_len // tq
    q_off = tok_off // tq
    b_off = tok_off // t_len
    widths = (1024, 512, 512, 512)
    in_specs = [pl.BlockSpec((tq, 1024), lambda b, i: (q_off + b * nq + i, 0)),
                pl.BlockSpec((tq, 512), lambda b, i: (q_off + b * nq + i, 0))]
    args = [q, dq]
    for w, a in zip(widths, new_kv):
        in_specs.append(pl.BlockSpec((t_len, w), lambda b, i: (b_off + b, 0)))
        args.append(a)
    n_pieces = 1
    if cache_kv is not None:
        n_pieces = 2
        for w, a in zip(widths, cache_kv):
            in_specs.append(pl.BlockSpec((PAST_LEN, w), lambda b, i: (b, 0)))
            args.append(a)
    in_specs += [pl.BlockSpec((4, DIFF_DH), lambda b, i: (0, 0)),
                 pl.BlockSpec((1, 2 * DIFF_DH), lambda b, i: (0, 0))]
    args += [lam, sub_g]
    aliases = {}
    if out_init is not None:
        in_specs = [pl.BlockSpec(memory_space=pl.ANY)] + in_specs
        args = [out_init] + args
        aliases = {0: 0}
    return pl.pallas_call(
        functools.partial(_attn_ab_kernel, n_pieces=n_pieces, n_seq=n_seq, lam_init=lam_init,
                          aliased=out_init is not None),
        grid=(n_batch, nq),
        in_specs=in_specs,
        out_specs=pl.BlockSpec((tq, 1024), lambda b, i: (q_off + b * nq + i, 0)),
        out_shape=jax.ShapeDtypeStruct((N_TOK, 1024), BF16),
        input_output_aliases=aliases,
        compiler_params=_cparams(2),
        name="attn_ab_%d" % n_pieces,
    )(*args)


def _gqa_stacks(sq_ref, sink_ref):
    tq = sq_ref.shape[0]
    halves = [_split_halves(sq_ref[:, LANES * j: LANES * (j + 1)]) for j in range(4)]
    q_stacks = [jnp.concatenate([halves[j][kvh] for j in range(4)], axis=0) for kvh in range(SWA_KV_HEADS)]
    sinks = [jnp.concatenate([jnp.full((tq, 1), sink_ref[4 * kvh + j] * LOG2E, F32) for j in range(4)], axis=0)
             for kvh in range(SWA_KV_HEADS)]
    return q_stacks, sinks


def _attn_cd_prompt_kernel(sink_ref, sq_ref, sk_ref, sv_ref, nq_ref, nk_ref, nv_ref, o_ref):
    lo = _lane_lo((SEQ, LANES))
    for sq in range(PROMPT_SEQS_PER_STEP):
        r = slice(SEQ * sq, SEQ * (sq + 1))
        sk = sk_ref[r, :]
        sv = sv_ref[r, :]
        for j in range(4):
            sl = slice(LANES * j, LANES * (j + 1))
            q_lo, q_hi = _split_halves(sq_ref[r, sl])
            o_lo = _softmax_pv([_dot_nt(q_lo, sk)], [sv], sink=sink_ref[j] * LOG2E)
            o_hi = _softmax_pv([_dot_nt(q_hi, sk)], [sv], sink=sink_ref[j + 4] * LOG2E)
            o_ref[r, sl] = jnp.where(lo, o_lo, o_hi).astype(BF16)
        for j in range(4):
            sl = slice(LANES * j, LANES * (j + 1))
            q_lo, q_hi = _split_halves(nq_ref[r, sl])
            k = nk_ref[r, sl]
            v = nv_ref[r, sl]
            o_lo = _softmax_pv([_dot_nt(q_lo, k)], [v])
            o_hi = _softmax_pv([_dot_nt(q_hi, k)], [v])
            o_ref[r, 512 + LANES * j: 512 + LANES * (j + 1)] = jnp.where(lo, o_lo, o_hi).astype(BF16)


def _attn_cd_prompt(sink, sq, sk, sv, nq, nk, nv):
    def spec(w):
        return pl.BlockSpec((SEQ * PROMPT_SEQS_PER_STEP, w), lambda b: (b, 0))
    return pl.pallas_call(
        _attn_cd_prompt_kernel,
        grid=(BATCH // PROMPT_SEQS_PER_STEP,),
        in_specs=[pl.BlockSpec(memory_space=pltpu.SMEM), spec(512), spec(128), spec(128),
                  spec(512), spec(512), spec(512)],
        out_specs=spec(1024),
        out_shape=jax.ShapeDtypeStruct((N_TOK, 1024), BF16),
        compiler_params=_cparams(1),
        name="attn_cd_prompt",
    )(sink, sq, sk, sv, nq, nk, nv)


_SWA_KEYS = TQ + 2 * SWA_WINDOW


def _attn_cd_sample_kernel(init_ref, sink_ref, sq_ref, nq_ref, sk_ref, sv_ref, nk_ref, nv_ref,
                           skc_ref, svc_ref, nkc_ref, nvc_ref, bias_ref, o_ref):
    del init_ref
    qi = pl.program_id(1)
    lo = _lane_lo((TQ, LANES))
    ks = pl.multiple_of(jnp.clip(qi * TQ - SWA_WINDOW, 0, DEC_SEQ - _SWA_KEYS), SWA_WINDOW)
    k_win = sk_ref[pl.ds(ks, _SWA_KEYS), :]
    v_win = sv_ref[pl.ds(ks, _SWA_KEYS), :]
    q_pos = qi * TQ + (lax.broadcasted_iota(jnp.int32, (4 * TQ, _SWA_KEYS), 0) & (TQ - 1))
    k_pos = ks + lax.broadcasted_iota(jnp.int32, (4 * TQ, _SWA_KEYS), 1)
    in_win = jnp.abs(q_pos - k_pos) <= SWA_WINDOW
    skc = skc_ref[...]
    svc = svc_ref[...]
    q_stacks, sinks = _gqa_stacks(sq_ref, sink_ref)

    def window_scores(kvh):
        return [_dot_nt(q_stacks[kvh], skc), jnp.where(in_win, _dot_nt(q_stacks[kvh], k_win), NEG)]

    o_kv = {}

    def window_finish(kvh, scores):
        o_kv[kvh] = _softmax_pv(scores, [svc, v_win], sink=sinks[kvh])
        if kvh == SWA_KV_HEADS - 1:
            for j in range(4):
                rows = slice(TQ * j, TQ * (j + 1))
                o_ref[:, LANES * j: LANES * (j + 1)] = jnp.where(lo, o_kv[0][rows], o_kv[1][rows]).astype(BF16)

    n_rows = DEC_SEQ // GRID_W
    r0 = jnp.clip(qi * NA_TILE_ROWS - NA_WIN_ROWS // 2, 0, n_rows - NA_KEY_ROWS)
    kn = pl.multiple_of(r0 * GRID_W, GRID_W)

    lane_lo = _lane_lo((1, LANES))
    pieces = []
    for ri in range(NA_TILE_ROWS):
        r = qi * NA_TILE_ROWS + ri
        rs = jnp.clip(r - NA_WIN_ROWS // 2, 0, n_rows - NA_WIN_ROWS)
        row = []
        for mm in range(NA_KEY_ROWS // 2):
            kr = r0 + 2 * mm
            ok = [(kr + t >= rs) & (kr + t < rs + NA_WIN_ROWS) for t in range(2)]
            mask = jnp.where(lane_lo, jnp.where(ok[0], 0.0, NEG), jnp.where(ok[1], 0.0, NEG))
            row.append((jnp.clip(kr - r + NA_WIN_ROWS, 0, 2 * NA_WIN_ROWS - 1), mask))
        pieces.append(row)

    def add_na_bias(hd, s):
        return jnp.concatenate(
            [jnp.concatenate([s[GRID_W * ri: GRID_W * (ri + 1), LANES * mm: LANES * (mm + 1)] + bias_ref[hd, d] + mask
                              for mm, (d, mask) in enumerate(row)], axis=1)
             for ri, row in enumerate(pieces)], axis=0)

    def na_scores(hd):
        sl = slice(LANES * (hd // 2), LANES * (hd // 2 + 1))
        q_half = _split_halves(nq_ref[:, sl])[hd % 2]
        nk_win = nk_ref[pl.ds(kn, NA_KEY_ROWS * GRID_W), sl]
        return [_dot_nt(q_half, nkc_ref[:, sl]), add_na_bias(hd, _dot_nt(q_half, nk_win))]

    pair = {}

    def na_finish(hd, scores):
        j = hd // 2
        sl = slice(LANES * j, LANES * (j + 1))
        nv_win = nv_ref[pl.ds(kn, NA_KEY_ROWS * GRID_W), sl]
        pair[hd % 2] = _softmax_pv(scores, [nvc_ref[:, sl], nv_win])
        if hd % 2 == 1:
            o_ref[:, 512 + LANES * j: 512 + LANES * (j + 1)] = jnp.where(lo, pair[0], pair[1]).astype(BF16)

    _pipelined_units([(window_scores, window_finish, kvh) for kvh in range(SWA_KV_HEADS)]
                     + [(na_scores, na_finish, hd) for hd in range(NA_HEADS)])


def _attn_cd_sample(out_init, sink, sq, sk, sv, nq, nk, nv, skc, svc, nkc, nvc, bias):
    nq_t = DEC_SEQ // TQ
    q_off = NP_TOK // TQ
    b_off = NP_TOK // DEC_SEQ

    def qspec(w):
        return pl.BlockSpec((TQ, w), lambda b, i: (q_off + b * nq_t + i, 0))

    def kspec(w):
        return pl.BlockSpec((DEC_SEQ, w), lambda b, i: (b_off + b, 0))

    def cspec(w):
        return pl.BlockSpec((PAST_LEN, w), lambda b, i: (b, 0))

    bias_spec = pl.BlockSpec((NA_HEADS, 2 * NA_WIN_ROWS, GRID_W, LANES), lambda b, i: (0, 0, 0, 0))
    return pl.pallas_call(
        _attn_cd_sample_kernel,
        grid=(DEC_BATCH, nq_t),
        in_specs=[pl.BlockSpec(memory_space=pl.ANY), pl.BlockSpec(memory_space=pltpu.SMEM),
                  qspec(512), qspec(512), kspec(128), kspec(128), kspec(512), kspec(512),
                  cspec(128), cspec(128), cspec(512), cspec(512), bias_spec],
        out_specs=pl.BlockSpec((TQ, 1024), lambda b, i: (q_off + b * nq_t + i, 0)),
        out_shape=jax.ShapeDtypeStruct((N_TOK, 1024), BF16),
        input_output_aliases={0: 0},
        compiler_params=_cparams(2),
        name="attn_cd_sample",
    )(out_init, sink, sq, nq, sk, sv, nk, nv, skc, svc, nkc, nvc, bias)


def _na_bias(rpb):
    n_dc = 2 * NA_WIN_COLS - 1
    c = np.arange(GRID_W)[:, None]
    kc = np.arange(GRID_W)[None, :]
    qs = np.clip(c - NA_WIN_COLS // 2, 0, GRID_W - NA_WIN_COLS)
    col_ok = (kc >= qs) & (kc < qs + NA_WIN_COLS)
    dc = np.clip(kc - c + NA_WIN_COLS - 1, 0, n_dc - 1)
    onehot = ((dc[None] == np.arange(n_dc)[:, None, None]) & col_ok[None]).astype(np.float32)
    blocks = jnp.einsum('hrd,dck->hrck', rpb.astype(F32) * LOG2E, onehot, precision=lax.Precision.HIGHEST)
    blocks = jnp.where(col_ok[None, None], blocks, NEG)
    none = jnp.full((NA_HEADS, 1, GRID_W, GRID_W), NEG, F32)
    return jnp.concatenate([jnp.concatenate([none, blocks], axis=1),
                            jnp.concatenate([blocks, none], axis=1)], axis=-1)


_HI_MASK = -65536


def _pack_pairs(x):
    w = x.shape[1] // 2
    r = x.astype(BF16).astype(F32)
    lo = lax.bitcast_convert_type(r[:, :w], jnp.int32)
    hi = lax.bitcast_convert_type(r[:, w:], jnp.int32)
    return (hi & _HI_MASK) | lax.shift_right_logical(lo, 16)


def _unpack_pairs(p):
    lo = lax.bitcast_convert_type(lax.shift_left(p, 16), F32)
    hi = lax.bitcast_convert_type(p & _HI_MASK, F32)
    return lo, hi


def _postmix_kernel(*refs, split_x):
    if split_x:
        o_ref, xp_ref, xs_ref = refs[:3]
        refs = refs[3:]
    else:
        o_ref, x_ref = refs[:2]
        refs = refs[2:]
    (mod_ref, g_ref, wout_ref, wr_ref, br_ref, tri_ref,
     x1_ref, h2_ref, route_ref, gate_ref, cnt_ref, run_ref) = refs
    i = pl.program_id(0)

    @pl.when(i == 0)
    def _():
        run_ref[...] = jnp.zeros_like(run_ref)

    m = mod_ref[0]
    g = g_ref[...]
    gate_gain = g[1:2] * m[2:3]
    ffn_gain = g[2:3] * (1.0 + m[4:5])
    sub_logits = {}

    def project(n):
        return _dot(o_ref[_sub_rows(n), :], wout_ref[...])

    def finish(n, y):
        r = _sub_rows(n)
        x = jnp.where(i < NPT, xp_ref[r, :], xs_ref[r, :]) if split_x else x_ref[r, :]
        x1 = x + _rms(y, gate_gain)
        x1_ref[r, :] = x1
        h2 = _rms(x1, ffn_gain) + m[3:4]
        h2_ref[r, :] = _pack_pairs(h2)
        sub_logits[n] = _dot(h2.astype(BF16), wr_ref[...]) + br_ref[...]

    _pipelined_units([(project, finish, n) for n in range(N_SUB)])

    logits = jnp.concatenate([sub_logits[n] for n in range(N_SUB)], axis=0)
    lane = lax.broadcasted_iota(jnp.int32, logits.shape, 1).astype(F32)
    cur = jnp.where(lane < N_EXPERTS, logits, -jnp.inf)
    tops, idxs = [], []
    for _ in range(TOP_K):
        mx = jnp.max(cur, axis=-1, keepdims=True)
        ix = jnp.min(jnp.where(cur == mx, lane, float(LANES)), axis=-1, keepdims=True)
        tops.append(mx)
        idxs.append(ix)
        cur = jnp.where(lane == ix, -jnp.inf, cur)
    es = [jnp.exp(t - tops[0]) for t in tops]
    inv = 1.0 / (es[0] + es[1] + es[2] + es[3])
    picked = jnp.zeros_like(logits)
    for k in range(TOP_K):
        picked = jnp.where(lane == idxs[k], 1.0, picked)
    before = _dot(tri_ref[...], picked.astype(BF16)) + run_ref[0:1, :]
    route = jnp.zeros_like(logits)
    gate_out = jnp.zeros_like(logits)
    for k in range(TOP_K):
        rank = jnp.sum(jnp.where(lane == idxs[k], before, 0.0), axis=-1, keepdims=True)
        route = jnp.where(lane == float(k), idxs[k], route)
        route = jnp.where(lane == float(TOP_K + k), rank, route)
        gate_out = jnp.where(lane == float(k), es[k] * inv, gate_out)
    route_ref[...] = route.T[:2 * TOP_K].astype(jnp.int32)
    gate_ref[...] = gate_out
    run_ref[...] = run_ref[...] + jnp.sum(picked, axis=0, keepdims=True)
    cnt_ref[...] = run_ref[...].astype(jnp.int32)


def _postmix(o_cat, xs, mod_l, g_l, w_out, w_r, b_r):
    tri = jnp.asarray(np.tril(np.ones((TM, TM), np.float32), -1), BF16)
    split_x = len(xs) == 2
    x_specs = [_PROMPT_SPEC, _SAMPLE_SPEC] if split_x else [_tok_spec(D_MODEL)]
    return pl.pallas_call(
        functools.partial(_postmix_kernel, split_x=split_x),
        grid=(N_TILES,),
        in_specs=[_tok_spec(1024)] + x_specs + [_MOD_SPEC, _const_spec((4, D_MODEL)),
                  _const_spec((1024, D_MODEL)), _const_spec((D_MODEL, LANES)), _const_spec((1, LANES)),
                  _const_spec((TM, TM))],
        out_specs=[_tok_spec(D_MODEL), _tok_spec(D_MODEL // 2), pl.BlockSpec((2 * TOP_K, TM), lambda i: (0, i)),
                   _tok_spec(LANES), _const_spec((8, LANES))],
        out_shape=[jax.ShapeDtypeStruct((N_TOK, D_MODEL), F32),
                   jax.ShapeDtypeStruct((N_TOK, D_MODEL // 2), jnp.int32),
                   jax.ShapeDtypeStruct((2 * TOP_K, N_TOK), jnp.int32), jax.ShapeDtypeStruct((N_TOK, LANES), F32),
                   jax.ShapeDtypeStruct((8, LANES), jnp.int32)],
        scratch_shapes=[pltpu.VMEM((8, LANES), F32)],
        compiler_params=_cparams(1),
        name="postmix",
    )(o_cat, *xs, mod_l, g_l, w_out, w_r, b_r, tri)


SC_WORKERS = 32
SC_ROWS = 64
ROW_WORDS = D_MODEL // 2

_SC_SCRATCH = [pltpu.VMEM((SC_ROWS, ROW_WORDS), jnp.int32), pltpu.VMEM((SC_ROWS, ROW_WORDS), jnp.int32),
               pltpu.SemaphoreType.DMA, pltpu.SemaphoreType.DMA, pltpu.SemaphoreType.DMA, pltpu.SemaphoreType.DMA]


def _sc_worker_id():
    return lax.axis_index("s") * 2 + lax.axis_index("c")


def _sc_double_buffered(n_chunks, load, store):
    for cp in load(0, 0):
        cp.start()

    @pl.loop(0, n_chunks, step=2)
    def _(g0):
        for b in range(2):
            g = g0 + b
            for cp in load(g, b):
                cp.wait()

            @pl.when(g >= 1)
            def _():
                for cp in store(g - 1, 1 - b):
                    cp.wait()

            @pl.when(g + 1 < n_chunks)
            def _():
                for cp in load(g + 1, 1 - b):
                    cp.start()

            for cp in store(g, b):
                cp.start()

    for cp in store(n_chunks - 1, (n_chunks - 1) % 2):
        cp.wait()


def _sc_dispatch(src, idx):
    n_chunks = N_TOK // (SC_WORKERS * SC_ROWS)
    assert n_chunks % 2 == 0
    mesh = plsc.VectorSubcoreMesh(core_axis_name="c", subcore_axis_name="s")

    @functools.partial(
        pl.kernel, mesh=mesh,
        out_type=jax.ShapeDtypeStruct((MOE_ROWS, ROW_WORDS), jnp.int32),
        scratch_types=[pltpu.VMEM((n_chunks * TOP_K, SC_ROWS), jnp.int32)] + _SC_SCRATCH)
    def k(src_hbm, idx_hbm, out_hbm, idx_v, buf0, buf1, in0, in1, out0, out1):
        wid = _sc_worker_id()
        pltpu.sync_copy(idx_hbm.at[wid], idx_v)
        bufs, in_sems, out_sems = (buf0, buf1), (in0, in1), (out0, out1)

        def load(g, b):
            rows = pl.ds((wid * n_chunks + g) * SC_ROWS, SC_ROWS)
            return [pltpu.make_async_copy(src_hbm.at[rows], bufs[b], in_sems[b])]

        def store(g, b):
            return [pltpu.make_async_copy(bufs[b], out_hbm.at[idx_v.at[g * TOP_K + kk]], out_sems[b])
                    for kk in range(TOP_K)]

        _sc_double_buffered(n_chunks, load, store)

    return k(src, idx)


def _sc_collect(table, idx):
    n_chunks = idx.shape[1]
    assert n_chunks % 2 == 0
    mesh = plsc.VectorSubcoreMesh(core_axis_name="c", subcore_axis_name="s")

    @functools.partial(
        pl.kernel, mesh=mesh,
        out_type=jax.ShapeDtypeStruct((SC_WORKERS * n_chunks * SC_ROWS, ROW_WORDS), jnp.int32),
        scratch_types=[pltpu.VMEM((n_chunks, SC_ROWS), jnp.int32)] + _SC_SCRATCH)
    def k(table_hbm, idx_hbm, out_hbm, idx_v, buf0, buf1, in0, in1, out0, out1):
        wid = _sc_worker_id()
        pltpu.sync_copy(idx_hbm.at[wid], idx_v)
        bufs, in_sems, out_sems = (buf0, buf1), (in0, in1), (out0, out1)

        def load(g, b):
            return [pltpu.make_async_copy(table_hbm.at[idx_v.at[g]], bufs[b], in_sems[b])]

        def store(g, b):
            rows = pl.ds((wid * n_chunks + g) * SC_ROWS, SC_ROWS)
            return [pltpu.make_async_copy(bufs[b], out_hbm.at[rows], out_sems[b])]

        _sc_double_buffered(n_chunks, load, store)

    return k(table, idx)


def _expert_rows(words, n_valid, wgu_b, wd_b, bgu, bd):
    live = lax.broadcasted_iota(jnp.int32, words.shape, 0) < n_valid
    lo, hi = _unpack_pairs(jnp.where(live, words, 0))
    x = jnp.concatenate([lo, hi], axis=1).astype(BF16)
    gu = _dot(x, wgu_b[...]) + bgu
    g = jnp.minimum(gu[:, :D_EXPERT], SWIGLU_LIMIT)
    u = jnp.clip(gu[:, D_EXPERT:], -SWIGLU_LIMIT, SWIGLU_LIMIT)
    a = g * jax.nn.sigmoid(SWIGLU_ALPHA * g) * (u + 1.0)
    return _pack_pairs(_dot(a.astype(BF16), wd_b[...]) + bd)


def _moe_kernel(blk_e_ref, blk_first_ref, blk_rows_ref, blk_slot_ref, blk_next_ref,
                x_ref, wgu_hbm, bgu_ref, wd_hbm, bd_ref, y_ref,
                wgu_f, wd_f, wgu_b, wd_b, sem, *, layer):
    i = pl.program_id(0)
    n_valid = blk_rows_ref[i]
    quantum = MOE_TM // MOE_TAIL_PARTS

    def weight_copies(e, slot):
        return (pltpu.make_async_copy(wgu_hbm.at[layer, e], wgu_f.at[slot], sem.at[0, slot]),
                pltpu.make_async_copy(wd_hbm.at[layer, e], wd_f.at[slot], sem.at[1, slot]))

    @pl.when(i == 0)
    def _():
        for cp in weight_copies(blk_e_ref[0], blk_slot_ref[0]):
            cp.start()

    @pl.when(blk_first_ref[i] == 1)
    def _():
        slot = blk_slot_ref[i]
        for cp in weight_copies(blk_e_ref[i], slot):
            cp.wait()
        nxt = blk_next_ref[i]

        @pl.when(nxt >= 0)
        def _():
            for cp in weight_copies(nxt, 1 - slot):
                cp.start()

        wgu_b[...] = wgu_f[slot].astype(BF16)
        wd_b[...] = wd_f[slot].astype(BF16)

    for parts in range(1, MOE_TAIL_PARTS + 1):
        rows = parts * quantum

        @pl.when((n_valid > rows - quantum) & (n_valid <= rows))
        def _(rows=rows):
            y_ref[:rows] = _expert_rows(x_ref[:rows], n_valid, wgu_b, wd_b, bgu_ref[0, 0], bd_ref[0, 0])
            if rows < MOE_TM:
                y_ref[rows:] = jnp.zeros((MOE_TM - rows, ROW_WORDS), jnp.int32)

    @pl.when(n_valid == 0)
    def _():
        y_ref[...] = jnp.zeros_like(y_ref)


def _moe(layer, blk_meta, xs, w_gu, b_gu, w_down, b_down):
    def row_map(i, *_):
        return (i, 0)

    def bias_map(i, e, *_):
        return (layer, e[i], 0, 0)

    grid_spec = pltpu.PrefetchScalarGridSpec(
        num_scalar_prefetch=5,
        grid=(MOE_BLOCKS,),
        in_specs=[
            pl.BlockSpec((MOE_TM, ROW_WORDS), row_map),
            pl.BlockSpec(memory_space=pl.ANY),
            pl.BlockSpec((1, 1, 1, 2 * D_EXPERT), bias_map),
            pl.BlockSpec(memory_space=pl.ANY),
            pl.BlockSpec((1, 1, 1, D_MODEL), bias_map),
        ],
        out_specs=pl.BlockSpec((MOE_TM, ROW_WORDS), row_map),
        scratch_shapes=[pltpu.VMEM((2, D_MODEL, 2 * D_EXPERT), F32), pltpu.VMEM((2, D_EXPERT, D_MODEL), F32),
                        pltpu.VMEM((D_MODEL, 2 * D_EXPERT), BF16), pltpu.VMEM((D_EXPERT, D_MODEL), BF16),
                        pltpu.SemaphoreType.DMA((2, 2))],
    )
    return pl.pallas_call(
        functools.partial(_moe_kernel, layer=layer),
        grid_spec=grid_spec,
        out_shape=jax.ShapeDtypeStruct((MOE_ROWS, ROW_WORDS), jnp.int32),
        compiler_params=_cparams(1),
        name="moe_experts",
    )(*blk_meta, xs, w_gu, b_gu.reshape(DEPTH, N_EXPERTS, 1, 2 * D_EXPERT),
      w_down, b_down.reshape(DEPTH, N_EXPERTS, 1, D_MODEL))


def _route(route, counts):
    experts = jnp.arange(N_EXPERTS, dtype=jnp.int32)
    padded = (counts + MOE_TM - 1) // MOE_TM * MOE_TM
    pad_end = jnp.cumsum(padded)
    pad_start = pad_end - padded
    e = route[0:TOP_K]
    onehot = e[:, :, None] == experts[None, None, :]
    dest = jnp.sum(jnp.where(onehot, pad_start[None, None, :], 0), axis=-1) + route[TOP_K:2 * TOP_K]
    blk_row0 = jnp.arange(MOE_BLOCKS, dtype=jnp.int32) * MOE_TM
    blk_e = jnp.minimum(jnp.sum((pad_end[None, :] <= blk_row0[:, None]).astype(jnp.int32), axis=1),
                        N_EXPERTS - 1)
    on = blk_row0 < pad_end[-1]
    n_on = jnp.sum(on.astype(jnp.int32))
    blk_onehot = blk_e[:, None] == experts[None, :]
    row_end = jnp.sum(jnp.where(blk_onehot, (pad_start + counts)[None, :], 0), axis=1)
    blk_rows = jnp.where(on, jnp.clip(row_end - blk_row0, 0, MOE_TM), 0).astype(jnp.int32)
    last_e = jnp.sum(jnp.where(jnp.arange(MOE_BLOCKS) == n_on - 1, blk_e, 0))
    blk_e = jnp.where(on, blk_e, last_e).astype(jnp.int32)
    prev = jnp.concatenate([jnp.full((1,), -1, jnp.int32), blk_e[:-1]])
    blk_first = (blk_e != prev).astype(jnp.int32)
    blk_slot = ((jnp.cumsum(blk_first) - 1) % 2).astype(jnp.int32)
    later_used = (experts[None, :] > experts[:, None]) & (counts[None, :] > 0)
    next_used = jnp.min(jnp.where(later_used, experts[None, :], N_EXPERTS), axis=1)
    next_used = jnp.where(next_used == N_EXPERTS, -1, next_used)
    blk_next = jnp.sum(jnp.where(blk_e[:, None] == experts[None, :], next_used[None, :], 0), axis=1)
    return dest, (blk_e, blk_first, blk_rows, blk_slot, blk_next.astype(jnp.int32))


def _combine_kernel(y_ref, gate_ref, x1_ref, mod_ref, g_ref, *o_refs):
    i = pl.program_id(0)
    m = mod_ref[0]
    gate = gate_ref[...]
    acc_lo = None
    for k in range(TOP_K):
        lo, hi = _unpack_pairs(y_ref[k])
        gk = gate[:, k:k + 1]
        acc_lo = gk * lo if acc_lo is None else acc_lo + gk * lo
        acc_hi = gk * hi if k == 0 else acc_hi + gk * hi
    acc = jnp.concatenate([acc_lo, acc_hi], axis=1)
    out = x1_ref[...] + _rms(acc, g_ref[...][3:4] * m[5:6])
    if len(o_refs) == 1:
        o_refs[0][...] = out
    else:
        @pl.when(i < NPT)
        def _():
            o_refs[0][...] = out

        @pl.when(i >= NPT)
        def _():
            o_refs[1][...] = out


def _combine(yg, gates, x1, mod_l, g_l, split_out):
    if split_out:
        out_specs = [_PROMPT_SPEC, _SAMPLE_SPEC]
        out_shape = [jax.ShapeDtypeStruct((NP_TOK, D_MODEL), F32), jax.ShapeDtypeStruct((NS_TOK, D_MODEL), F32)]
    else:
        out_specs = [_tok_spec(D_MODEL)]
        out_shape = [jax.ShapeDtypeStruct((N_TOK, D_MODEL), F32)]
    return pl.pallas_call(
        _combine_kernel,
        grid=(N_TILES,),
        in_specs=[pl.BlockSpec((TOP_K, TM, ROW_WORDS), lambda i: (0, i, 0)), _tok_spec(LANES),
                  _tok_spec(D_MODEL), _MOD_SPEC, _const_spec((4, D_MODEL))],
        out_specs=out_specs,
        out_shape=out_shape,
        compiler_params=_cparams(1),
        name="combine",
    )(yg, gates, x1, mod_l, g_l)


def _ffn(layer, o_cat, xs, mod_l, g_l, w_out, w_router, b_router, w_gu, b_gu, w_down, b_down, split_out):
    w_r = jnp.pad(w_router, ((0, 0), (0, LANES - N_EXPERTS))).astype(BF16)
    b_r = jnp.pad(b_router, (0, LANES - N_EXPERTS)).reshape(1, LANES)
    x1, h2p, route, gate_slab, counts = _postmix(o_cat, xs, mod_l, g_l, w_out.astype(BF16), w_r, b_r)
    dest, blk_meta = _route(route, counts[0, :N_EXPERTS])
    n_chunks = N_TOK // (SC_WORKERS * SC_ROWS)
    idx_d = dest.reshape(TOP_K, SC_WORKERS, n_chunks, SC_ROWS).transpose(1, 2, 0, 3).reshape(
        SC_WORKERS, n_chunks * TOP_K, SC_ROWS)
    rows = _sc_dispatch(h2p, idx_d)
    ys = _moe(layer, blk_meta, rows, w_gu, b_gu, w_down, b_down)
    idx_c = dest.reshape(SC_WORKERS, TOP_K * n_chunks, SC_ROWS)
    yg = _sc_collect(ys, idx_c).reshape(TOP_K, N_TOK, ROW_WORDS)
    return _combine(yg, gate_slab, x1, mod_l, g_l, split_out)


def _pad_heads(w, n_heads, width, keep):
    k = w.shape[0]
    w = w.reshape(k, n_heads, width)[:, :, :keep]
    return jnp.pad(w, ((0, 0), (0, 0), (0, LANES - keep))).reshape(k, n_heads * LANES)


def _pe_slab(x):
    return jnp.pad(x, [(0, 0)] * (x.ndim - 1) + [(MLA_NOPE, LANES - MLA_NOPE - MLA_ROPE)])


def _pair_kv_heads(w):
    g = SWA_HEADS // SWA_KV_HEADS
    return w.reshape(SWA_KV_HEADS, g, HEAD_DIM, -1).transpose(1, 0, 2, 3).reshape(w.shape)


def kernel(x_prompt, x_sample, cache_mla_ckv, cache_mla_krope, cache_diff_k, cache_diff_v, cache_swa_k, cache_swa_v, cache_na_k, cache_na_v, c, c_ctx, w_mod, b_mod, norm_g, w_in0, mla_q_norm, w_uq, mla_kv_norm, w_ukv, diff_lambda, diff_norm, w_out0, w_in1, swa_sink, na_rpb, w_out1, w_router, b_router, w_gu, b_gu, w_down, b_down):
    xs = (x_prompt.reshape(NP_TOK, D_MODEL), x_sample.reshape(NS_TOK, D_MODEL))
    cond = jnp.concatenate([c_ctx[None, :], c, jnp.zeros((16 - 1 - DEC_BATCH, D_MODEL), F32)], axis=0)
    mod = _modulation(cond, w_mod, b_mod).reshape(DEPTH, 16, 6, D_MODEL)
    t64, t32 = _rope_tables()
    states = {}
    for l in range(DEPTH):
        i = l // 2
        g_l = norm_g[l]
        mod_l = mod[l]
        if l % 2 == 0:
            lam_init = 0.8 - 0.6 * math.exp(-0.3 * l)
            wi = w_in0[i]
            w_in_p = jnp.concatenate(
                [wi[:, 0:640], wi[:, 672:2208], _pe_slab(wi[:, 640:672])], axis=1).astype(BF16)
            w_uq_p = _pad_heads(w_uq[i], MLA_HEADS, MLA_NOPE + MLA_ROPE, MLA_NOPE + MLA_ROPE).astype(BF16)
            w_k_p = _pad_heads(w_ukv[i], MLA_HEADS, MLA_NOPE + MLA_V, MLA_NOPE).astype(BF16)
            w_v = w_ukv[i].reshape(MLA_KV_RANK, MLA_HEADS, MLA_NOPE + MLA_V)[:, :, MLA_NOPE:].reshape(
                MLA_KV_RANK, MLA_HEADS * MLA_V).astype(BF16)
            (q, k, v, dq, dk, dv, ckv_st, kpe_st, dk_st, dv_st) = _premix0(
                *xs, mod_l, g_l[0:1], w_in_p, mla_q_norm[i][None, :], w_uq_p, mla_kv_norm[i][None, :],
                w_k_p, w_v, t32, t64)
            states['mla_ckv'] = ckv_st.reshape(BATCH, 1, SEQ, MLA_KV_RANK)
            states['mla_krope'] = kpe_st[:, MLA_NOPE:MLA_NOPE + MLA_ROPE].reshape(BATCH, 1, SEQ, MLA_ROPE)
            states['diff_k'] = dk_st.reshape(BATCH, 1, SEQ, DIFF_HEADS, 2 * DIFF_DH)
            states['diff_v'] = dv_st.reshape(BATCH, 1, SEQ, DIFF_HEADS, 2 * DIFF_DH)
            kc, vc = _mla_cache(cache_mla_ckv[:, i].reshape(DEC_BATCH * PAST_LEN, MLA_KV_RANK),
                                _pe_slab(cache_mla_krope[:, i].reshape(DEC_BATCH * PAST_LEN, MLA_ROPE)),
                                w_k_p, w_v)
            dkc = cache_diff_k[:, i].reshape(DEC_BATCH * PAST_LEN, 512).astype(BF16)
            dvc = cache_diff_v[:, i].reshape(DEC_BATCH * PAST_LEN, 512).astype(BF16)
            lam = diff_lambda[i]
            sub_g = diff_norm[i][None, :]
            o_p = _attn_ab(q, dq, (k, v, dk, dv), None, lam, sub_g, lam_init,
                           n_batch=BATCH, t_len=SEQ, tq=SEQ, tok_off=0, n_seq=PROMPT_SEQS_PER_STEP)
            o_cat = _attn_ab(q, dq, (k, v, dk, dv), (kc, vc, dkc, dvc), lam, sub_g, lam_init,
                             n_batch=DEC_BATCH, t_len=DEC_SEQ, tq=TQ_AB, tok_off=NP_TOK, out_init=o_p)
            w_out = w_out0[i]
        else:
            wi = w_in1[i]
            n_sq = SWA_HEADS * HEAD_DIM
            w_in_p = jnp.concatenate([_pair_kv_heads(wi[:, :n_sq].T).T, wi[:, n_sq:]], axis=1).astype(BF16)
            (sq, sk, sv, nq, nk, nv, sk_st, sv_st, nk_st, nv_st) = _premix1(xs[0], mod_l, g_l[0:1], w_in_p, t64)
            states['swa_k'] = sk_st.reshape(BATCH, 1, SEQ, SWA_KV_HEADS, HEAD_DIM)
            states['swa_v'] = sv_st.reshape(BATCH, 1, SEQ, SWA_KV_HEADS, HEAD_DIM)
            states['na_k'] = nk_st.reshape(BATCH, 1, SEQ, NA_HEADS, HEAD_DIM)
            states['na_v'] = nv_st.reshape(BATCH, 1, SEQ, NA_HEADS, HEAD_DIM)
            skc = cache_swa_k[:, i].reshape(DEC_BATCH * PAST_LEN, 128).astype(BF16)
            svc = cache_swa_v[:, i].reshape(DEC_BATCH * PAST_LEN, 128).astype(BF16)
            nkc = cache_na_k[:, i].reshape(DEC_BATCH * PAST_LEN, 512).astype(BF16)
            nvc = cache_na_v[:, i].reshape(DEC_BATCH * PAST_LEN, 512).astype(BF16)
            sink = swa_sink[i]
            o_p = _attn_cd_prompt(sink, sq, sk, sv, nq, nk, nv)
            o_cat = _attn_cd_sample(o_p, sink, sq, sk, sv, nq, nk, nv, skc, svc, nkc, nvc, _na_bias(na_rpb[i]))
            wo = w_out1[i]
            w_out = jnp.concatenate([_pair_kv_heads(wo[:n_sq]), wo[n_sq:]], axis=0)
        xs = _ffn(l, o_cat, xs, mod_l, g_l, w_out, w_router[l], b_router[l], w_gu, b_gu, w_down, b_down,
                  split_out=(l == DEPTH - 1))
    return (xs[0].reshape(BATCH, SEQ, D_MODEL), xs[1].reshape(DEC_BATCH, DEC_SEQ, D_MODEL),
            states['mla_ckv'], states['mla_krope'], states['diff_k'], states['diff_v'],
            states['swa_k'], states['swa_v'], states['na_k'], states['na_v'])
```

```python
import functools
import math

import numpy as np
import jax
import jax.numpy as jnp
from jax import lax
from jax.experimental import pallas as pl
from jax.experimental.pallas import tpu as pltpu
from jax.experimental.pallas import tpu_sc as plsc

F32 = jnp.float32
BF16 = jnp.bfloat16

D_MODEL = 1024
BATCH = 16
SEQ = 256
DEPTH = 2
DEC_BATCH = 8
DEC_SEQ = 2048
PAST_LEN = 256
GRID_W = 64
HEAD_DIM = 64
ROPE_THETA = 10000.0
EPS = 1e-6
NEG = -1e30

MLA_HEADS = 8
MLA_Q_RANK = 384
MLA_KV_RANK = 256
MLA_NOPE = 64
MLA_ROPE = 32
MLA_V = 64
DIFF_HEADS = 4
DIFF_DH = 64
SWA_HEADS = 8
SWA_KV_HEADS = 2
SWA_WINDOW = 128
NA_HEADS = 8
NA_WIN_ROWS = 8
NA_WIN_COLS = 16
N_EXPERTS = 32
TOP_K = 4
D_EXPERT = 1024
SWIGLU_LIMIT = 7.0
SWIGLU_ALPHA = 1.702

LANES = 128
NP_TOK = BATCH * SEQ
NS_TOK = DEC_BATCH * DEC_SEQ
N_TOK = NP_TOK + NS_TOK
TM = 512
NPT = NP_TOK // TM
TILES_PER_SAMPLE = DEC_SEQ // TM
N_TILES = N_TOK // TM
TM_WIDE = 1024
N_SUB = 2
SUB_TM = TM // N_SUB
TQ = 256
TQ_AB = 256
PROMPT_SEQS_PER_STEP = 2
MOE_TM = 1024
MOE_TAIL_PARTS = 8
MOE_ROWS = ((N_TOK * TOP_K + N_EXPERTS * (MOE_TM - 1)) // MOE_TM + 1) * MOE_TM
MOE_BLOCKS = MOE_ROWS // MOE_TM
NA_TILE_ROWS = TQ // GRID_W
NA_KEY_ROWS = 12
VMEM_LIMIT = 56 * 1024 * 1024


def _cparams(n_axes, vmem=VMEM_LIMIT):
    return pltpu.CompilerParams(dimension_semantics=("arbitrary",) * n_axes,
                                vmem_limit_bytes=vmem)


def _rms(x, g):
    return x * lax.rsqrt(jnp.mean(x * x, axis=-1, keepdims=True) + EPS) * g


def _dot(a, b):
    return jnp.dot(a, b, preferred_element_type=F32)


def _dot_nt(a, b):
    return lax.dot_general(a, b, (((1,), (1,)), ((), ())), preferred_element_type=F32)


def _rope(x, cos, sin_a, sin_b, half):
    return (x * cos + pltpu.roll(x, LANES - half, 1) * sin_a + pltpu.roll(x, half, 1) * sin_b)


def _pipelined_units(units):
    pending = units[0][0](units[0][2])
    for n, (_, finish, arg) in enumerate(units):
        following = units[n + 1][0](units[n + 1][2]) if n + 1 < len(units) else None
        finish(arg, pending)
        pending = following


def _sub_rows(n):
    return slice(SUB_TM * n, SUB_TM * (n + 1))


def _store_head_rows(st_ref, n, hd, n_heads, x):
    st_ref[pl.ds(SUB_TM * n * n_heads + hd, SUB_TM, stride=n_heads), :] = x


def _mod_row(i, tm=TM):
    npt = NP_TOK // tm
    return jnp.where(i < npt, 0, 1 + (i - npt) // (DEC_SEQ // tm))


def _rope_blk(i):
    return jnp.where(i < NPT, TILES_PER_SAMPLE, (i - NPT) % TILES_PER_SAMPLE)


def _mod_kernel(c_ref, w_ref, b_ref, o_ref):
    c = c_ref[...]
    s = (c * jax.nn.sigmoid(c)).astype(BF16)
    o_ref[0] = _dot(s, w_ref[0].astype(BF16)) + b_ref[0]


def _modulation(cond, w_mod, b_mod):
    nb = 1024
    return pl.pallas_call(
        _mod_kernel,
        grid=(DEPTH, 6 * D_MODEL // nb),
        in_specs=[
            pl.BlockSpec((16, D_MODEL), lambda l, n: (0, 0)),
            pl.BlockSpec((1, D_MODEL, nb), lambda l, n: (l, 0, n)),
            pl.BlockSpec((1, 1, nb), lambda l, n: (l, 0, n)),
        ],
        out_specs=pl.BlockSpec((1, 16, nb), lambda l, n: (l, 0, n)),
        out_shape=jax.ShapeDtypeStruct((DEPTH, 16, 6 * D_MODEL), F32),
        compiler_params=_cparams(2),
        name="modulation",
    )(cond, w_mod, b_mod.reshape(DEPTH, 1, 6 * D_MODEL))


def _rope_tables():
    t = jnp.arange(DEC_SEQ)
    rows = (t // GRID_W).astype(F32)
    cols = (t % GRID_W).astype(F32)

    def angles(r):
        n = r // 4
        inv = ROPE_THETA ** (-jnp.arange(n, dtype=F32) / n)
        return jnp.concatenate([rows[:, None] * inv[None], cols[:, None] * inv[None]], axis=-1)

    def finish(cos, sa, sb):
        ident = (jnp.ones((TM, LANES), F32), jnp.zeros((TM, LANES), F32), jnp.zeros((TM, LANES), F32))
        return tuple(jnp.concatenate([a, b], axis=0) for a, b in zip((cos, sa, sb), ident))

    a64 = angles(64)
    c, s, z = jnp.cos(a64), jnp.sin(a64), jnp.zeros_like(a64)
    t64 = finish(jnp.concatenate([c, c, c, c], -1), jnp.concatenate([-s, z, -s, z], -1),
                 jnp.concatenate([z, s, z, s], -1))
    a32 = angles(32)
    c, s, z = jnp.cos(a32), jnp.sin(a32), jnp.zeros_like(a32)
    one64 = jnp.ones((DEC_SEQ, 64), F32)
    z64 = jnp.zeros((DEC_SEQ, 64), F32)
    z32 = jnp.zeros((DEC_SEQ, 32), F32)
    t32 = finish(jnp.concatenate([one64, c, c, z32], -1), jnp.concatenate([z64, -s, z, z32], -1),
                 jnp.concatenate([z64, z, s, z32], -1))
    return t64, t32


LOG2E = math.log2(math.e)
_DIFF_COLS = DIFF_HEADS * 2 * DIFF_DH
_AB_COLS = tuple(np.cumsum([0, MLA_Q_RANK, MLA_KV_RANK, _DIFF_COLS, _DIFF_COLS, _DIFF_COLS, LANES]).tolist())
_CD_COLS = tuple(np.cumsum([0, SWA_HEADS * HEAD_DIM, SWA_KV_HEADS * HEAD_DIM, SWA_KV_HEADS * HEAD_DIM,
                            NA_HEADS * HEAD_DIM, NA_HEADS * HEAD_DIM, NA_HEADS * HEAD_DIM]).tolist())
IN_COLS = _AB_COLS[-1]
assert IN_COLS == _CD_COLS[-1]
_MLA_SCALE = (MLA_NOPE + MLA_ROPE) ** -0.5 * LOG2E
_QSCALE = HEAD_DIM ** -0.5 * LOG2E


def _premix0_kernel(xp_ref, xs_ref, mod_ref, g_ref, win_ref, qn_ref, wuq_ref, kvn_ref, wk_ref, wv_ref,
                    c32_ref, sa32_ref, sb32_ref, c64_ref, sa64_ref, sb64_ref,
                    q_ref, k_ref, v_ref, dq_ref, dk_ref, dv_ref,
                    ckv_st, kpe_st, dk_st, dv_st):
    i = pl.program_id(0)
    m = mod_ref[0]
    gain = g_ref[...] * (1.0 + m[1:2])

    def project(n):
        r = _sub_rows(n)
        x = jnp.where(i < NPT, xp_ref[r, :], xs_ref[r, :])
        h = _rms(x, gain) + m[0:1]
        return _dot(h.astype(BF16), win_ref[...])

    states = {}

    def finish(n, proj):
        r = _sub_rows(n)
        q_a, kv_a, dq, dk, dv, pe = (proj[:, a:b] for a, b in zip(_AB_COLS[:-1], _AB_COLS[1:]))
        q = _dot(_rms(q_a, qn_ref[...]).astype(BF16), wuq_ref[...])
        ckv = _rms(kv_a, kvn_ref[...])
        ckv_b = ckv.astype(BF16)
        kn = _dot(ckv_b, wk_ref[...])
        v_ref[r, :] = _dot(ckv_b, wv_ref[...]).astype(BF16)
        c32, sa32, sb32 = c32_ref[r, :], sa32_ref[r, :], sb32_ref[r, :]
        c64, sa64, sb64 = c64_ref[r, :], sa64_ref[r, :], sb64_ref[r, :]
        pe_r = _rope(pe, c32, sa32, sb32, MLA_ROPE // 2)
        for hd in range(MLA_HEADS):
            sl = slice(LANES * hd, LANES * (hd + 1))
            q_ref[r, sl] = (_rope(q[:, sl], c32, sa32, sb32, MLA_ROPE // 2) * _MLA_SCALE).astype(BF16)
            k_ref[r, sl] = (kn[:, sl] + pe_r).astype(BF16)
        for hd in range(DIFF_HEADS):
            sl = slice(LANES * hd, LANES * (hd + 1))
            dq_ref[r, sl] = (_rope(dq[:, sl], c64, sa64, sb64, DIFF_DH // 2) * _QSCALE).astype(BF16)
            dk_ref[r, sl] = _rope(dk[:, sl], c64, sa64, sb64, DIFF_DH // 2).astype(BF16)
        dv_ref[r, :] = dv.astype(BF16)
        states[n] = (ckv, pe, dk, dv)

    _pipelined_units([(project, finish, n) for n in range(N_SUB)])

    @pl.when(i < NPT)
    def _():
        for n in range(N_SUB):
            r = _sub_rows(n)
            ckv, pe, dk, dv = states[n]
            ckv_st[r, :] = ckv
            kpe_st[r, :] = pe
            for hd in range(DIFF_HEADS):
                _store_head_rows(dk_st, n, hd, DIFF_HEADS, dk[:, LANES * hd: LANES * (hd + 1)])
                _store_head_rows(dv_st, n, hd, DIFF_HEADS, dv[:, LANES * hd: LANES * (hd + 1)])


def _premix1_kernel(x_ref, mod_ref, g_ref, win_ref, c64_ref, sa64_ref, sb64_ref,
                    sq_ref, sk_ref, sv_ref, nq_ref, nk_ref, nv_ref,
                    sk_st, sv_st, nk_st, nv_st):
    i = pl.program_id(0)
    m = mod_ref[0]
    gain = g_ref[...] * (1.0 + m[1:2])

    def project(n):
        h = _rms(x_ref[_sub_rows(n), :], gain) + m[0:1]
        return _dot(h.astype(BF16), win_ref[...])

    states = {}

    def finish(n, proj):
        r = _sub_rows(n)
        sq, sk, sv, nq, nk, nv = (proj[:, a:b] for a, b in zip(_CD_COLS[:-1], _CD_COLS[1:]))
        c64, sa64, sb64 = c64_ref[r, :], sa64_ref[r, :], sb64_ref[r, :]
        for hd in range(4):
            sl = slice(LANES * hd, LANES * (hd + 1))
            sq_ref[r, sl] = (_rope(sq[:, sl], c64, sa64, sb64, HEAD_DIM // 2) * _QSCALE).astype(BF16)
        sk_ref[r, :] = _rope(sk, c64, sa64, sb64, HEAD_DIM // 2).astype(BF16)
        sv_ref[r, :] = sv.astype(BF16)
        nq_ref[r, :] = (nq * _QSCALE).astype(BF16)
        nk_ref[r, :] = nk.astype(BF16)
        nv_ref[r, :] = nv.astype(BF16)
        states[n] = (sk, sv, nk, nv)

    _pipelined_units([(project, finish, n) for n in range(N_SUB)])

    @pl.when(i < NPT)
    def _():
        for n in range(N_SUB):
            r = _sub_rows(n)
            sk, sv, nk, nv = states[n]
            sk_st[r, :] = sk
            sv_st[r, :] = sv
            for hd in range(NA_HEADS):
                _store_head_rows(nk_st, n, hd, NA_HEADS, nk[:, HEAD_DIM * hd: HEAD_DIM * (hd + 1)])
                _store_head_rows(nv_st, n, hd, NA_HEADS, nv[:, HEAD_DIM * hd: HEAD_DIM * (hd + 1)])


def _tok_spec(width, tm=TM):
    return pl.BlockSpec((tm, width), lambda i: (i, 0))


def _prompt_spec(tm=TM):
    return pl.BlockSpec((tm, D_MODEL), lambda i: (jnp.minimum(i, NP_TOK // tm - 1), 0))


def _sample_spec(tm=TM):
    return pl.BlockSpec((tm, D_MODEL), lambda i: (jnp.maximum(i - NP_TOK // tm, 0), 0))


_PROMPT_SPEC = _prompt_spec()
_SAMPLE_SPEC = _sample_spec()


def _state_spec(width, rows_per_token=1):
    return pl.BlockSpec((TM * rows_per_token, width), lambda i: (jnp.minimum(i, NPT - 1), 0))


def _const_spec(shape):
    return pl.BlockSpec(shape, lambda i: (0,) * len(shape))


def _mod_spec(tm=TM):
    return pl.BlockSpec((1, 6, D_MODEL), lambda i: (_mod_row(i, tm), 0, 0))


_MOD_SPEC = _mod_spec()
_ROPE_SPEC = pl.BlockSpec((TM, LANES), lambda i: (_rope_blk(i), 0))


def _premix0(xp, xs, mod_l, g0, w_in_p, q_norm, w_uq_p, kv_norm, w_k_p, w_v, t32, t64):
    outs = [(N_TOK, 1024, BF16), (N_TOK, 1024, BF16), (N_TOK, 512, BF16), (N_TOK, 512, BF16),
            (N_TOK, 512, BF16), (N_TOK, 512, BF16),
            (NP_TOK, 256, F32), (NP_TOK, 128, F32),
            (NP_TOK * DIFF_HEADS, 2 * DIFF_DH, F32), (NP_TOK * DIFF_HEADS, 2 * DIFF_DH, F32)]
    return pl.pallas_call(
        _premix0_kernel,
        grid=(N_TILES,),
        in_specs=[_PROMPT_SPEC, _SAMPLE_SPEC, _MOD_SPEC, _const_spec((1, D_MODEL)),
                  _const_spec((D_MODEL, IN_COLS)), _const_spec((1, MLA_Q_RANK)),
                  _const_spec((MLA_Q_RANK, 1024)), _const_spec((1, MLA_KV_RANK)),
                  _const_spec((MLA_KV_RANK, 1024)), _const_spec((MLA_KV_RANK, 512))]
                 + [_ROPE_SPEC] * 6,
        out_specs=([_tok_spec(w) for (_, w, _) in outs[:6]]
                   + [_state_spec(w, n // NP_TOK) for (n, w, _) in outs[6:]]),
        out_shape=[jax.ShapeDtypeStruct((n, w), dt) for (n, w, dt) in outs],
        compiler_params=_cparams(1),
        name="premix_ab",
    )(xp, xs, mod_l, g0, w_in_p, q_norm, w_uq_p, kv_norm, w_k_p, w_v, *t32, *t64)


def _premix1(x, mod_l, g0, w_in_p, t64):
    outs = [(N_TOK, 512, BF16), (N_TOK, 128, BF16), (N_TOK, 128, BF16), (N_TOK, 512, BF16),
            (N_TOK, 512, BF16), (N_TOK, 512, BF16),
            (NP_TOK, 128, F32), (NP_TOK, 128, F32),
            (NP_TOK * NA_HEADS, HEAD_DIM, F32), (NP_TOK * NA_HEADS, HEAD_DIM, F32)]
    return pl.pallas_call(
        _premix1_kernel,
        grid=(N_TILES,),
        in_specs=[_tok_spec(D_MODEL), _MOD_SPEC, _const_spec((1, D_MODEL)),
                  _const_spec((D_MODEL, IN_COLS))] + [_ROPE_SPEC] * 3,
        out_specs=([_tok_spec(w) for (_, w, _) in outs[:6]]
                   + [_state_spec(w, n // NP_TOK) for (n, w, _) in outs[6:]]),
        out_shape=[jax.ShapeDtypeStruct((n, w), dt) for (n, w, dt) in outs],
        compiler_params=_cparams(1),
        name="premix_cd",
    )(x, mod_l, g0, w_in_p, *t64)


def _mla_cache_kernel(ckv_ref, pe_ref, wk_ref, wv_ref, k_ref, v_ref):
    c = ckv_ref[...].astype(BF16)
    kn = _dot(c, wk_ref[...])
    v_ref[...] = _dot(c, wv_ref[...]).astype(BF16)
    pe = pe_ref[...]
    for hd in range(MLA_HEADS):
        sl = slice(LANES * hd, LANES * (hd + 1))
        k_ref[:, sl] = (kn[:, sl] + pe).astype(BF16)


def _mla_cache(ckv, pe_slab, w_k_p, w_v):
    n = ckv.shape[0]
    tm = 512
    return pl.pallas_call(
        _mla_cache_kernel,
        grid=(n // tm,),
        in_specs=[pl.BlockSpec((tm, MLA_KV_RANK), lambda i: (i, 0)),
                  pl.BlockSpec((tm, LANES), lambda i: (i, 0)),
                  _const_spec((MLA_KV_RANK, 1024)), _const_spec((MLA_KV_RANK, 512))],
        out_specs=[pl.BlockSpec((tm, 1024), lambda i: (i, 0)), pl.BlockSpec((tm, 512), lambda i: (i, 0))],
        out_shape=[jax.ShapeDtypeStruct((n, 1024), BF16), jax.ShapeDtypeStruct((n, 512), BF16)],
        compiler_params=_cparams(1),
        name="mla_cache",
    )(ckv, pe_slab, w_k_p, w_v)


def _softmax_pv(scores, values, sink=None):
    m = jnp.max(scores[0], axis=-1, keepdims=True)
    for s in scores[1:]:
        m = jnp.maximum(m, jnp.max(s, axis=-1, keepdims=True))
    if sink is not None:
        m = jnp.maximum(m, sink)
    l = None
    o = None
    for s, v in zip(scores, values):
        p = jnp.exp2(s - m)
        ls = jnp.sum(p, axis=-1, keepdims=True)
        os_ = _dot(p.astype(BF16), v)
        l = ls if l is None else l + ls
        o = os_ if o is None else o + os_
    if sink is not None:
        l = l + jnp.exp2(sink - m)
    return o * (1.0 / l)


def _lane_lo(shape):
    return lax.broadcasted_iota(jnp.int32, shape, 1) < (LANES // 2)


def _split_halves(qb):
    lo = _lane_lo(qb.shape)
    zero = jnp.zeros_like(qb)
    return jnp.where(lo, qb, zero), jnp.where(lo, zero, qb)


def _attn_ab_kernel(*refs, n_pieces, n_seq, lam_init, aliased):
    if aliased:
        refs = refs[1:]
    q_ref, dq_ref = refs[0], refs[1]
    pieces = [refs[2 + 4 * p: 6 + 4 * p] for p in range(n_pieces)]
    lam_ref, subg_ref, o_ref = refs[2 + 4 * n_pieces:]
    lam = lam_ref[...]
    lam_full = (jnp.exp(jnp.sum(lam[0:1] * lam[1:2], axis=-1, keepdims=True))
                - jnp.exp(jnp.sum(lam[2:3] * lam[3:4], axis=-1, keepdims=True)) + lam_init)
    tq = q_ref.shape[0] // n_seq
    lo = _lane_lo((tq, LANES))
    subg = subg_ref[...] * (1.0 - lam_init)

    def seq_rows(ref, sq):
        n = ref.shape[0] // n_seq
        return slice(n * sq, n * (sq + 1))

    def mla_scores(arg):
        sq, hd = arg
        sl = slice(LANES * hd, LANES * (hd + 1))
        qh = q_ref[seq_rows(q_ref, sq), sl]
        return [_dot_nt(qh, k_ref[seq_rows(k_ref, sq), sl]) for (k_ref, _, _, _) in pieces]

    def diff_scores(arg):
        sq, hd = arg
        sl = slice(LANES * hd, LANES * (hd + 1))
        qq = jnp.concatenate(_split_halves(dq_ref[seq_rows(dq_ref, sq), sl]), axis=0)
        return [_dot_nt(qq, dk_ref[seq_rows(dk_ref, sq), sl]) for (_, _, dk_ref, _) in pieces]

    pair = {}

    def mla_finish(arg, scores):
        sq, hd = arg
        j = hd // 2
        vals = [v_ref[seq_rows(v_ref, sq), LANES * j: LANES * (j + 1)] for (_, v_ref, _, _) in pieces]
        pair[hd % 2] = _softmax_pv(scores, vals)
        if hd % 2 == 1:
            o_ref[seq_rows(o_ref, sq), LANES * j: LANES * (j + 1)] = jnp.where(lo, pair[0], pair[1]).astype(BF16)

    def diff_finish(arg, scores):
        sq, hd = arg
        sl = slice(LANES * hd, LANES * (hd + 1))
        oo = _softmax_pv(scores, [dv_ref[seq_rows(dv_ref, sq), sl] for (_, _, _, dv_ref) in pieces])
        od = _rms(oo[:tq] - lam_full * oo[tq:], subg)
        o_ref[seq_rows(o_ref, sq), 512 + LANES * hd: 512 + LANES * (hd + 1)] = od.astype(BF16)

    units = []
    for sq in range(n_seq):
        for j in range(DIFF_HEADS):
            units += [(diff_scores, diff_finish, (sq, j)), (mla_scores, mla_finish, (sq, 2 * j)),
                      (mla_scores, mla_finish, (sq, 2 * j + 1))]
    _pipelined_units(units)


def _attn_ab(q, dq, new_kv, cache_kv, lam, sub_g, lam_init, *, n_batch, t_len, tq, tok_off, out_init=None,
             n_seq=1):
    assert n_seq == 1 or (tq == t_len and cache_kv is None and n_batch % n_seq == 0)
    n_batch, t_len, tq = n_batch // n_seq, t_len * n_seq, tq * n_seq
    nq = t_len // tq
    q_off = tok_off // tq
    b_off = tok_off // t_len
    widths = (1024, 512, 512, 512)
    in_specs = [pl.BlockSpec((tq, 1024), lambda b, i: (q_off + b * nq + i, 0)),
                pl.BlockSpec((tq, 512), lambda b, i: (q_off + b * nq + i, 0))]
    args = [q, dq]
    for w, a in zip(widths, new_kv):
        in_specs.append(pl.BlockSpec((t_len, w), lambda b, i: (b_off + b, 0)))
        args.append(a)
    n_pieces = 1
    if cache_kv is not None:
        n_pieces = 2
        for w, a in zip(widths, cache_kv):
            in_specs.append(pl.BlockSpec((PAST_LEN, w), lambda b, i: (b, 0)))
            args.append(a)
    in_specs += [pl.BlockSpec((4, DIFF_DH), lambda b, i: (0, 0)),
                 pl.BlockSpec((1, 2 * DIFF_DH), lambda b, i: (0, 0))]
    args += [lam, sub_g]
    aliases = {}
    if out_init is not None:
        in_specs = [pl.BlockSpec(memory_space=pl.ANY)] + in_specs
        args = [out_init] + args
        aliases = {0: 0}
    return pl.pallas_call(
        functools.partial(_attn_ab_kernel, n_pieces=n_pieces, n_seq=n_seq, lam_init=lam_init,
                          aliased=out_init is not None),
        grid=(n_batch, nq),
        in_specs=in_specs,
        out_specs=pl.BlockSpec((tq, 1024), lambda b, i: (q_off + b * nq + i, 0)),
        out_shape=jax.ShapeDtypeStruct((N_TOK, 1024), BF16),
        input_output_aliases=aliases,
        compiler_params=_cparams(2),
        name="attn_ab_%d" % n_pieces,
    )(*args)


def _gqa_stacks(sq_ref, sink_ref):
    tq = sq_ref.shape[0]
    halves = [_split_halves(sq_ref[:, LANES * j: LANES * (j + 1)]) for j in range(4)]
    q_stacks = [jnp.concatenate([halves[j][kvh] for j in range(4)], axis=0) for kvh in range(SWA_KV_HEADS)]
    sinks = [jnp.concatenate([jnp.full((tq, 1), sink_ref[4 * kvh + j] * LOG2E, F32) for j in range(4)], axis=0)
             for kvh in range(SWA_KV_HEADS)]
    return q_stacks, sinks


def _attn_cd_prompt_kernel(sink_ref, sq_ref, sk_ref, sv_ref, nq_ref, nk_ref, nv_ref, o_ref):
    lo = _lane_lo((SEQ, LANES))
    for sq in range(PROMPT_SEQS_PER_STEP):
        r = slice(SEQ * sq, SEQ * (sq + 1))
        sk = sk_ref[r, :]
        sv = sv_ref[r, :]
        for j in range(4):
            sl = slice(LANES * j, LANES * (j + 1))
            q_lo, q_hi = _split_halves(sq_ref[r, sl])
            o_lo = _softmax_pv([_dot_nt(q_lo, sk)], [sv], sink=sink_ref[j] * LOG2E)
            o_hi = _softmax_pv([_dot_nt(q_hi, sk)], [sv], sink=sink_ref[j + 4] * LOG2E)
            o_ref[r, sl] = jnp.where(lo, o_lo, o_hi).astype(BF16)
        for j in range(4):
            sl = slice(LANES * j, LANES * (j + 1))
            q_lo, q_hi = _split_halves(nq_ref[r, sl])
            k = nk_ref[r, sl]
            v = nv_ref[r, sl]
            o_lo = _softmax_pv([_dot_nt(q_lo, k)], [v])
            o_hi = _softmax_pv([_dot_nt(q_hi, k)], [v])
            o_ref[r, 512 + LANES * j: 512 + LANES * (j + 1)] = jnp.where(lo, o_lo, o_hi).astype(BF16)


def _attn_cd_prompt(sink, sq, sk, sv, nq, nk, nv):
    def spec(w):
        return pl.BlockSpec((SEQ * PROMPT_SEQS_PER_STEP, w), lambda b: (b, 0))
    return pl.pallas_call(
        _attn_cd_prompt_kernel,
        grid=(BATCH // PROMPT_SEQS_PER_STEP,),
        in_specs=[pl.BlockSpec(memory_space=pltpu.SMEM), spec(512), spec(128), spec(128),
                  spec(512), spec(512), spec(512)],
        out_specs=spec(1024),
        out_shape=jax.ShapeDtypeStruct((N_TOK, 1024), BF16),
        compiler_params=_cparams(1),
        name="attn_cd_prompt",
    )(sink, sq, sk, sv, nq, nk, nv)


_SWA_KEYS = TQ + 2 * SWA_WINDOW


def _attn_cd_sample_kernel(init_ref, sink_ref, sq_ref, nq_ref, sk_ref, sv_ref, nk_ref, nv_ref,
                           skc_ref, svc_ref, nkc_ref, nvc_ref, bias_ref, o_ref):
    del init_ref
    qi = pl.program_id(1)
    lo = _lane_lo((TQ, LANES))
    ks = pl.multiple_of(jnp.clip(qi * TQ - SWA_WINDOW, 0, DEC_SEQ - _SWA_KEYS), SWA_WINDOW)
    k_win = sk_ref[pl.ds(ks, _SWA_KEYS), :]
    v_win = sv_ref[pl.ds(ks, _SWA_KEYS), :]
    q_pos = qi * TQ + (lax.broadcasted_iota(jnp.int32, (4 * TQ, _SWA_KEYS), 0) & (TQ - 1))
    k_pos = ks + lax.broadcasted_iota(jnp.int32, (4 * TQ, _SWA_KEYS), 1)
    in_win = jnp.abs(q_pos - k_pos) <= SWA_WINDOW
    skc = skc_ref[...]
    svc = svc_ref[...]
    q_stacks, sinks = _gqa_stacks(sq_ref, sink_ref)

    def window_scores(kvh):
        return [_dot_nt(q_stacks[kvh], skc), jnp.where(in_win, _dot_nt(q_stacks[kvh], k_win), NEG)]

    o_kv = {}

    def window_finish(kvh, scores):
        o_kv[kvh] = _softmax_pv(scores, [svc, v_win], sink=sinks[kvh])
        if kvh == SWA_KV_HEADS - 1:
            for j in range(4):
                rows = slice(TQ * j, TQ * (j + 1))
                o_ref[:, LANES * j: LANES * (j + 1)] = jnp.where(lo, o_kv[0][rows], o_kv[1][rows]).astype(BF16)

    n_rows = DEC_SEQ // GRID_W
    r0 = jnp.clip(qi * NA_TILE_ROWS - NA_WIN_ROWS // 2, 0, n_rows - NA_KEY_ROWS)
    kn = pl.multiple_of(r0 * GRID_W, GRID_W)

    lane_lo = _lane_lo((1, LANES))
    pieces = []
    for ri in range(NA_TILE_ROWS):
        r = qi * NA_TILE_ROWS + ri
        rs = jnp.clip(r - NA_WIN_ROWS // 2, 0, n_rows - NA_WIN_ROWS)
        row = []
        for mm in range(NA_KEY_ROWS // 2):
            kr = r0 + 2 * mm
            ok = [(kr + t >= rs) & (kr + t < rs + NA_WIN_ROWS) for t in range(2)]
            mask = jnp.where(lane_lo, jnp.where(ok[0], 0.0, NEG), jnp.where(ok[1], 0.0, NEG))
            row.append((jnp.clip(kr - r + NA_WIN_ROWS, 0, 2 * NA_WIN_ROWS - 1), mask))
        pieces.append(row)

    def add_na_bias(hd, s):
        return jnp.concatenate(
            [jnp.concatenate([s[GRID_W * ri: GRID_W * (ri + 1), LANES * mm: LANES * (mm + 1)] + bias_ref[hd, d] + mask
                              for mm, (d, mask) in enumerate(row)], axis=1)
             for ri, row in enumerate(pieces)], axis=0)

    def na_scores(hd):
        sl = slice(LANES * (hd // 2), LANES * (hd // 2 + 1))
        q_half = _split_halves(nq_ref[:, sl])[hd % 2]
        nk_win = nk_ref[pl.ds(kn, NA_KEY_ROWS * GRID_W), sl]
        return [_dot_nt(q_half, nkc_ref[:, sl]), add_na_bias(hd, _dot_nt(q_half, nk_win))]

    pair = {}

    def na_finish(hd, scores):
        j = hd // 2
        sl = slice(LANES * j, LANES * (j + 1))
        nv_win = nv_ref[pl.ds(kn, NA_KEY_ROWS * GRID_W), sl]
        pair[hd % 2] = _softmax_pv(scores, [nvc_ref[:, sl], nv_win])
        if hd % 2 == 1:
            o_ref[:, 512 + LANES * j: 512 + LANES * (j + 1)] = jnp.where(lo, pair[0], pair[1]).astype(BF16)

    _pipelined_units([(window_scores, window_finish, kvh) for kvh in range(SWA_KV_HEADS)]
                     + [(na_scores, na_finish, hd) for hd in range(NA_HEADS)])


def _attn_cd_sample(out_init, sink, sq, sk, sv, nq, nk, nv, skc, svc, nkc, nvc, bias):
    nq_t = DEC_SEQ // TQ
    q_off = NP_TOK // TQ
    b_off = NP_TOK // DEC_SEQ

    def qspec(w):
        return pl.BlockSpec((TQ, w), lambda b, i: (q_off + b * nq_t + i, 0))

    def kspec(w):
        return pl.BlockSpec((DEC_SEQ, w), lambda b, i: (b_off + b, 0))

    def cspec(w):
        return pl.BlockSpec((PAST_LEN, w), lambda b, i: (b, 0))

    bias_spec = pl.BlockSpec((NA_HEADS, 2 * NA_WIN_ROWS, GRID_W, LANES), lambda b, i: (0, 0, 0, 0))
    return pl.pallas_call(
        _attn_cd_sample_kernel,
        grid=(DEC_BATCH, nq_t),
        in_specs=[pl.BlockSpec(memory_space=pl.ANY), pl.BlockSpec(memory_space=pltpu.SMEM),
                  qspec(512), qspec(512), kspec(128), kspec(128), kspec(512), kspec(512),
                  cspec(128), cspec(128), cspec(512), cspec(512), bias_spec],
        out_specs=pl.BlockSpec((TQ, 1024), lambda b, i: (q_off + b * nq_t + i, 0)),
        out_shape=jax.ShapeDtypeStruct((N_TOK, 1024), BF16),
        input_output_aliases={0: 0},
        compiler_params=_cparams(2),
        name="attn_cd_sample",
    )(out_init, sink, sq, nq, sk, sv, nk, nv, skc, svc, nkc, nvc, bias)


def _na_bias(rpb):
    n_dc = 2 * NA_WIN_COLS - 1
    c = np.arange(GRID_W)[:, None]
    kc = np.arange(GRID_W)[None, :]
    qs = np.clip(c - NA_WIN_COLS // 2, 0, GRID_W - NA_WIN_COLS)
    col_ok = (kc >= qs) & (kc < qs + NA_WIN_COLS)
    dc = np.clip(kc - c + NA_WIN_COLS - 1, 0, n_dc - 1)
    onehot = ((dc[None] == np.arange(n_dc)[:, None, None]) & col_ok[None]).astype(np.float32)
    blocks = jnp.einsum('hrd,dck->hrck', rpb.astype(F32) * LOG2E, onehot, precision=lax.Precision.HIGHEST)
    blocks = jnp.where(col_ok[None, None], blocks, NEG)
    none = jnp.full((NA_HEADS, 1, GRID_W, GRID_W), NEG, F32)
    return jnp.concatenate([jnp.concatenate([none, blocks], axis=1),
                            jnp.concatenate([blocks, none], axis=1)], axis=-1)


_HI_MASK = -65536


def _pack_pairs(x):
    w = x.shape[1] // 2
    r = x.astype(BF16).astype(F32)
    lo = lax.bitcast_convert_type(r[:, :w], jnp.int32)
    hi = lax.bitcast_convert_type(r[:, w:], jnp.int32)
    return (hi & _HI_MASK) | lax.shift_right_logical(lo, 16)


def _unpack_pairs(p):
    lo = lax.bitcast_convert_type(lax.shift_left(p, 16), F32)
    hi = lax.bitcast_convert_type(p & _HI_MASK, F32)
    return lo, hi


def _postmix_kernel(*refs, split_x, tm):
    if split_x:
        o_ref, xp_ref, xs_ref = refs[:3]
        refs = refs[3:]
    else:
        o_ref, x_ref = refs[:2]
        refs = refs[2:]
    (mod_ref, g_ref, wout_ref, wr_ref, br_ref, tri_ref,
     x1_ref, h2_ref, route_ref, gate_ref, cnt_ref, run_ref) = refs
    i = pl.program_id(0)

    @pl.when(i == 0)
    def _():
        run_ref[...] = jnp.zeros_like(run_ref)

    m = mod_ref[0]
    g = g_ref[...]
    gate_gain = g[1:2] * m[2:3]
    ffn_gain = g[2:3] * (1.0 + m[4:5])
    sub_logits = {}
    sub = tm // N_SUB
    npt = NP_TOK // tm

    def project(n):
        return _dot(o_ref[sub * n: sub * (n + 1), :], wout_ref[...])

    def finish(n, y):
        r = slice(sub * n, sub * (n + 1))
        x = jnp.where(i < npt, xp_ref[r, :], xs_ref[r, :]) if split_x else x_ref[r, :]
        x1 = x + _rms(y, gate_gain)
        x1_ref[r, :] = x1
        h2 = _rms(x1, ffn_gain) + m[3:4]
        h2_ref[r, :] = _pack_pairs(h2)
        sub_logits[n] = _dot(h2.astype(BF16), wr_ref[...]) + br_ref[...]

    _pipelined_units([(project, finish, n) for n in range(N_SUB)])

    logits = jnp.concatenate([sub_logits[n] for n in range(N_SUB)], axis=0)
    lane = lax.broadcasted_iota(jnp.int32, logits.shape, 1).astype(F32)
    cur = jnp.where(lane < N_EXPERTS, logits, -jnp.inf)
    tops, idxs = [], []
    for _ in range(TOP_K):
        mx = jnp.max(cur, axis=-1, keepdims=True)
        ix = jnp.min(jnp.where(cur == mx, lane, float(LANES)), axis=-1, keepdims=True)
        tops.append(mx)
        idxs.append(ix)
        cur = jnp.where(lane == ix, -jnp.inf, cur)
    es = [jnp.exp(t - tops[0]) for t in tops]
    inv = 1.0 / (es[0] + es[1] + es[2] + es[3])
    picked = jnp.zeros_like(logits)
    for k in range(TOP_K):
        picked = jnp.where(lane == idxs[k], 1.0, picked)
    before = _dot(tri_ref[...], picked.astype(BF16)) + run_ref[0:1, :]
    route = jnp.zeros_like(logits)
    gate_out = jnp.zeros_like(logits)
    for k in range(TOP_K):
        rank = jnp.sum(jnp.where(lane == idxs[k], before, 0.0), axis=-1, keepdims=True)
        route = jnp.where(lane == float(k), idxs[k], route)
        route = jnp.where(lane == float(TOP_K + k), rank, route)
        gate_out = jnp.where(lane == float(k), es[k] * inv, gate_out)
    route_ref[...] = route.T[:2 * TOP_K].astype(jnp.int32)
    gate_ref[...] = gate_out
    run_ref[...] = run_ref[...] + jnp.sum(picked, axis=0, keepdims=True)
    cnt_ref[...] = run_ref[...].astype(jnp.int32)


def _postmix(o_cat, xs, mod_l, g_l, w_out, w_r, b_r):
    tm = TM_WIDE
    tri = jnp.asarray(np.tril(np.ones((tm, tm), np.float32), -1), BF16)
    split_x = len(xs) == 2
    x_specs = [_prompt_spec(tm), _sample_spec(tm)] if split_x else [_tok_spec(D_MODEL, tm)]
    return pl.pallas_call(
        functools.partial(_postmix_kernel, split_x=split_x, tm=tm),
        grid=(N_TOK // tm,),
        in_specs=[_tok_spec(1024, tm)] + x_specs + [_mod_spec(tm), _const_spec((4, D_MODEL)),
                  _const_spec((1024, D_MODEL)), _const_spec((D_MODEL, LANES)), _const_spec((1, LANES)),
                  _const_spec((tm, tm))],
        out_specs=[_tok_spec(D_MODEL, tm), _tok_spec(D_MODEL // 2, tm),
                   pl.BlockSpec((2 * TOP_K, tm), lambda i: (0, i)), _tok_spec(LANES, tm), _const_spec((8, LANES))],
        out_shape=[jax.ShapeDtypeStruct((N_TOK, D_MODEL), F32),
                   jax.ShapeDtypeStruct((N_TOK, D_MODEL // 2), jnp.int32),
                   jax.ShapeDtypeStruct((2 * TOP_K, N_TOK), jnp.int32), jax.ShapeDtypeStruct((N_TOK, LANES), F32),
                   jax.ShapeDtypeStruct((8, LANES), jnp.int32)],
        scratch_shapes=[pltpu.VMEM((8, LANES), F32)],
        compiler_params=_cparams(1),
        name="postmix",
    )(o_cat, *xs, mod_l, g_l, w_out, w_r, b_r, tri)


SC_WORKERS = 32
SC_ROWS = 64
ROW_WORDS = D_MODEL // 2

_SC_SCRATCH = [pltpu.VMEM((SC_ROWS, ROW_WORDS), jnp.int32), pltpu.VMEM((SC_ROWS, ROW_WORDS), jnp.int32),
               pltpu.SemaphoreType.DMA, pltpu.SemaphoreType.DMA, pltpu.SemaphoreType.DMA, pltpu.SemaphoreType.DMA]


def _sc_worker_id():
    return lax.axis_index("s") * 2 + lax.axis_index("c")


def _sc_double_buffered(n_chunks, load, store):
    for cp in load(0, 0):
        cp.start()

    @pl.loop(0, n_chunks, step=2)
    def _(g0):
        for b in range(2):
            g = g0 + b
            for cp in load(g, b):
                cp.wait()

            @pl.when(g >= 1)
            def _():
                for cp in store(g - 1, 1 - b):
                    cp.wait()

            @pl.when(g + 1 < n_chunks)
            def _():
                for cp in load(g + 1, 1 - b):
                    cp.start()

            for cp in store(g, b):
                cp.start()

    for cp in store(n_chunks - 1, (n_chunks - 1) % 2):
        cp.wait()


def _sc_dispatch(src, idx):
    n_chunks = N_TOK // (SC_WORKERS * SC_ROWS)
    assert n_chunks % 2 == 0
    mesh = plsc.VectorSubcoreMesh(core_axis_name="c", subcore_axis_name="s")

    @functools.partial(
        pl.kernel, mesh=mesh,
        out_type=jax.ShapeDtypeStruct((MOE_ROWS, ROW_WORDS), jnp.int32),
        scratch_types=[pltpu.VMEM((n_chunks * TOP_K, SC_ROWS), jnp.int32)] + _SC_SCRATCH)
    def k(src_hbm, idx_hbm, out_hbm, idx_v, buf0, buf1, in0, in1, out0, out1):
        wid = _sc_worker_id()
        pltpu.sync_copy(idx_hbm.at[wid], idx_v)
        bufs, in_sems, out_sems = (buf0, buf1), (in0, in1), (out0, out1)

        def load(g, b):
            rows = pl.ds((wid * n_chunks + g) * SC_ROWS, SC_ROWS)
            return [pltpu.make_async_copy(src_hbm.at[rows], bufs[b], in_sems[b])]

        def store(g, b):
            return [pltpu.make_async_copy(bufs[b], out_hbm.at[idx_v.at[g * TOP_K + kk]], out_sems[b])
                    for kk in range(TOP_K)]

        _sc_double_buffered(n_chunks, load, store)

    return k(src, idx)


def _sc_collect(table, idx):
    n_chunks = idx.shape[1]
    assert n_chunks % 2 == 0
    mesh = plsc.VectorSubcoreMesh(core_axis_name="c", subcore_axis_name="s")

    @functools.partial(
        pl.kernel, mesh=mesh,
        out_type=jax.ShapeDtypeStruct((SC_WORKERS * n_chunks * SC_ROWS, ROW_WORDS), jnp.int32),
        scratch_types=[pltpu.VMEM((n_chunks, SC_ROWS), jnp.int32)] + _SC_SCRATCH)
    def k(table_hbm, idx_hbm, out_hbm, idx_v, buf0, buf1, in0, in1, out0, out1):
        wid = _sc_worker_id()
        pltpu.sync_copy(idx_hbm.at[wid], idx_v)
        bufs, in_sems, out_sems = (buf0, buf1), (in0, in1), (out0, out1)

        def load(g, b):
            return [pltpu.make_async_copy(table_hbm.at[idx_v.at[g]], bufs[b], in_sems[b])]

        def store(g, b):
            rows = pl.ds((wid * n_chunks + g) * SC_ROWS, SC_ROWS)
            return [pltpu.make_async_copy(bufs[b], out_hbm.at[rows], out_sems[b])]

        _sc_double_buffered(n_chunks, load, store)

    return k(table, idx)


def _expert_rows(words, n_valid, wgu_b, wd_b, bgu, bd):
    live = lax.broadcasted_iota(jnp.int32, words.shape, 0) < n_valid
    lo, hi = _unpack_pairs(jnp.where(live, words, 0))
    x = jnp.concatenate([lo, hi], axis=1).astype(BF16)
    gu = _dot(x, wgu_b[...]) + bgu
    g = jnp.minimum(gu[:, :D_EXPERT], SWIGLU_LIMIT)
    u = jnp.clip(gu[:, D_EXPERT:], -SWIGLU_LIMIT, SWIGLU_LIMIT)
    a = g * jax.nn.sigmoid(SWIGLU_ALPHA * g) * (u + 1.0)
    return _pack_pairs(_dot(a.astype(BF16), wd_b[...]) + bd)


def _moe_kernel(blk_e_ref, blk_first_ref, blk_rows_ref, blk_slot_ref, blk_next_ref,
                x_ref, wgu_hbm, bgu_ref, wd_hbm, bd_ref, y_ref,
                wgu_f, wd_f, wgu_b, wd_b, sem, *, layer):
    i = pl.program_id(0)
    n_valid = blk_rows_ref[i]
    quantum = MOE_TM // MOE_TAIL_PARTS

    def weight_copies(e, slot):
        return (pltpu.make_async_copy(wgu_hbm.at[layer, e], wgu_f.at[slot], sem.at[0, slot]),
                pltpu.make_async_copy(wd_hbm.at[layer, e], wd_f.at[slot], sem.at[1, slot]))

    @pl.when(i == 0)
    def _():
        for cp in weight_copies(blk_e_ref[0], blk_slot_ref[0]):
            cp.start()

    @pl.when(blk_first_ref[i] == 1)
    def _():
        slot = blk_slot_ref[i]
        for cp in weight_copies(blk_e_ref[i], slot):
            cp.wait()
        nxt = blk_next_ref[i]

        @pl.when(nxt >= 0)
        def _():
            for cp in weight_copies(nxt, 1 - slot):
                cp.start()

        wgu_b[...] = wgu_f[slot].astype(BF16)
        wd_b[...] = wd_f[slot].astype(BF16)

    for parts in range(1, MOE_TAIL_PARTS + 1):
        rows = parts * quantum

        @pl.when((n_valid > rows - quantum) & (n_valid <= rows))
        def _(rows=rows):
            y_ref[:rows] = _expert_rows(x_ref[:rows], n_valid, wgu_b, wd_b, bgu_ref[0, 0], bd_ref[0, 0])
            if rows < MOE_TM:
                y_ref[rows:] = jnp.zeros((MOE_TM - rows, ROW_WORDS), jnp.int32)

    @pl.when(n_valid == 0)
    def _():
        y_ref[...] = jnp.zeros_like(y_ref)


def _moe(layer, blk_meta, xs, w_gu, b_gu, w_down, b_down):
    def row_map(i, *_):
        return (i, 0)

    def bias_map(i, e, *_):
        return (layer, e[i], 0, 0)

    grid_spec = pltpu.PrefetchScalarGridSpec(
        num_scalar_prefetch=5,
        grid=(MOE_BLOCKS,),
        in_specs=[
            pl.BlockSpec((MOE_TM, ROW_WORDS), row_map),
            pl.BlockSpec(memory_space=pl.ANY),
            pl.BlockSpec((1, 1, 1, 2 * D_EXPERT), bias_map),
            pl.BlockSpec(memory_space=pl.ANY),
            pl.BlockSpec((1, 1, 1, D_MODEL), bias_map),
        ],
        out_specs=pl.BlockSpec((MOE_TM, ROW_WORDS), row_map),
        scratch_shapes=[pltpu.VMEM((2, D_MODEL, 2 * D_EXPERT), F32), pltpu.VMEM((2, D_EXPERT, D_MODEL), F32),
                        pltpu.VMEM((D_MODEL, 2 * D_EXPERT), BF16), pltpu.VMEM((D_EXPERT, D_MODEL), BF16),
                        pltpu.SemaphoreType.DMA((2, 2))],
    )
    return pl.pallas_call(
        functools.partial(_moe_kernel, layer=layer),
        grid_spec=grid_spec,
        out_shape=jax.ShapeDtypeStruct((MOE_ROWS, ROW_WORDS), jnp.int32),
        compiler_params=_cparams(1),
        name="moe_experts",
    )(*blk_meta, xs, w_gu, b_gu.reshape(DEPTH, N_EXPERTS, 1, 2 * D_EXPERT),
      w_down, b_down.reshape(DEPTH, N_EXPERTS, 1, D_MODEL))


def _route(route, counts):
    experts = jnp.arange(N_EXPERTS, dtype=jnp.int32)
    padded = (counts + MOE_TM - 1) // MOE_TM * MOE_TM
    pad_end = jnp.cumsum(padded)
    pad_start = pad_end - padded
    e = route[0:TOP_K]
    onehot = e[:, :, None] == experts[None, None, :]
    dest = jnp.sum(jnp.where(onehot, pad_start[None, None, :], 0), axis=-1) + route[TOP_K:2 * TOP_K]
    blk_row0 = jnp.arange(MOE_BLOCKS, dtype=jnp.int32) * MOE_TM
    blk_e = jnp.minimum(jnp.sum((pad_end[None, :] <= blk_row0[:, None]).astype(jnp.int32), axis=1),
                        N_EXPERTS - 1)
    on = blk_row0 < pad_end[-1]
    n_on = jnp.sum(on.astype(jnp.int32))
    blk_onehot = blk_e[:, None] == experts[None, :]
    row_end = jnp.sum(jnp.where(blk_onehot, (pad_start + counts)[None, :], 0), axis=1)
    blk_rows = jnp.where(on, jnp.clip(row_end - blk_row0, 0, MOE_TM), 0).astype(jnp.int32)
    last_e = jnp.sum(jnp.where(jnp.arange(MOE_BLOCKS) == n_on - 1, blk_e, 0))
    blk_e = jnp.where(on, blk_e, last_e).astype(jnp.int32)
    prev = jnp.concatenate([jnp.full((1,), -1, jnp.int32), blk_e[:-1]])
    blk_first = (blk_e != prev).astype(jnp.int32)
    blk_slot = ((jnp.cumsum(blk_first) - 1) % 2).astype(jnp.int32)
    later_used = (experts[None, :] > experts[:, None]) & (counts[None, :] > 0)
    next_used = jnp.min(jnp.where(later_used, experts[None, :], N_EXPERTS), axis=1)
    next_used = jnp.where(next_used == N_EXPERTS, -1, next_used)
    blk_next = jnp.sum(jnp.where(blk_e[:, None] == experts[None, :], next_used[None, :], 0), axis=1)
    return dest, (blk_e, blk_first, blk_rows, blk_slot, blk_next.astype(jnp.int32))


def _combine_kernel(y_ref, gate_ref, x1_ref, mod_ref, g_ref, *o_refs, npt):
    i = pl.program_id(0)
    m = mod_ref[0]
    gate = gate_ref[...]
    acc_lo = None
    for k in range(TOP_K):
        lo, hi = _unpack_pairs(y_ref[k])
        gk = gate[:, k:k + 1]
        acc_lo = gk * lo if acc_lo is None else acc_lo + gk * lo
        acc_hi = gk * hi if k == 0 else acc_hi + gk * hi
    acc = jnp.concatenate([acc_lo, acc_hi], axis=1)
    out = x1_ref[...] + _rms(acc, g_ref[...][3:4] * m[5:6])
    if len(o_refs) == 1:
        o_refs[0][...] = out
    else:
        @pl.when(i < npt)
        def _():
            o_refs[0][...] = out

        @pl.when(i >= npt)
        def _():
            o_refs[1][...] = out


def _combine(yg, gates, x1, mod_l, g_l, split_out):
    tm = TM_WIDE
    if split_out:
        out_specs = [_prompt_spec(tm), _sample_spec(tm)]
        out_shape = [jax.ShapeDtypeStruct((NP_TOK, D_MODEL), F32), jax.ShapeDtypeStruct((NS_TOK, D_MODEL), F32)]
    else:
        out_specs = [_tok_spec(D_MODEL, tm)]
        out_shape = [jax.ShapeDtypeStruct((N_TOK, D_MODEL), F32)]
    return pl.pallas_call(
        functools.partial(_combine_kernel, npt=NP_TOK // tm),
        grid=(N_TOK // tm,),
        in_specs=[pl.BlockSpec((TOP_K, tm, ROW_WORDS), lambda i: (0, i, 0)), _tok_spec(LANES, tm),
                  _tok_spec(D_MODEL, tm), _mod_spec(tm), _const_spec((4, D_MODEL))],
        out_specs=out_specs,
        out_shape=out_shape,
        compiler_params=_cparams(1),
        name="combine",
    )(yg, gates, x1, mod_l, g_l)


def _ffn(layer, o_cat, xs, mod_l, g_l, w_out, w_router, b_router, w_gu, b_gu, w_down, b_down, split_out):
    w_r = jnp.pad(w_router, ((0, 0), (0, LANES - N_EXPERTS))).astype(BF16)
    b_r = jnp.pad(b_router, (0, LANES - N_EXPERTS)).reshape(1, LANES)
    x1, h2p, route, gate_slab, counts = _postmix(o_cat, xs, mod_l, g_l, w_out.astype(BF16), w_r, b_r)
    dest, blk_meta = _route(route, counts[0, :N_EXPERTS])
    n_chunks = N_TOK // (SC_WORKERS * SC_ROWS)
    idx_d = dest.reshape(TOP_K, SC_WORKERS, n_chunks, SC_ROWS).transpose(1, 2, 0, 3).reshape(
        SC_WORKERS, n_chunks * TOP_K, SC_ROWS)
    rows = _sc_dispatch(h2p, idx_d)
    ys = _moe(layer, blk_meta, rows, w_gu, b_gu, w_down, b_down)
    idx_c = dest.reshape(SC_WORKERS, TOP_K * n_chunks, SC_ROWS)
    yg = _sc_collect(ys, idx_c).reshape(TOP_K, N_TOK, ROW_WORDS)
    return _combine(yg, gate_slab, x1, mod_l, g_l, split_out)


def _pad_heads(w, n_heads, width, keep):
    k = w.shape[0]
    w = w.reshape(k, n_heads, width)[:, :, :keep]
    return jnp.pad(w, ((0, 0), (0, 0), (0, LANES - keep))).reshape(k, n_heads * LANES)


def _pe_slab(x):
    return jnp.pad(x, [(0, 0)] * (x.ndim - 1) + [(MLA_NOPE, LANES - MLA_NOPE - MLA_ROPE)])


def _pair_kv_heads(w):
    g = SWA_HEADS // SWA_KV_HEADS
    return w.reshape(SWA_KV_HEADS, g, HEAD_DIM, -1).transpose(1, 0, 2, 3).reshape(w.shape)


def kernel(x_prompt, x_sample, cache_mla_ckv, cache_mla_krope, cache_diff_k, cache_diff_v, cache_swa_k, cache_swa_v, cache_na_k, cache_na_v, c, c_ctx, w_mod, b_mod, norm_g, w_in0, mla_q_norm, w_uq, mla_kv_norm, w_ukv, diff_lambda, diff_norm, w_out0, w_in1, swa_sink, na_rpb, w_out1, w_router, b_router, w_gu, b_gu, w_down, b_down):
    xs = (x_prompt.reshape(NP_TOK, D_MODEL), x_sample.reshape(NS_TOK, D_MODEL))
    cond = jnp.concatenate([c_ctx[None, :], c, jnp.zeros((16 - 1 - DEC_BATCH, D_MODEL), F32)], axis=0)
    mod = _modulation(cond, w_mod, b_mod).reshape(DEPTH, 16, 6, D_MODEL)
    t64, t32 = _rope_tables()
    states = {}
    for l in range(DEPTH):
        i = l // 2
        g_l = norm_g[l]
        mod_l = mod[l]
        if l % 2 == 0:
            lam_init = 0.8 - 0.6 * math.exp(-0.3 * l)
            wi = w_in0[i]
            w_in_p = jnp.concatenate(
                [wi[:, 0:640], wi[:, 672:2208], _pe_slab(wi[:, 640:672])], axis=1).astype(BF16)
            w_uq_p = _pad_heads(w_uq[i], MLA_HEADS, MLA_NOPE + MLA_ROPE, MLA_NOPE + MLA_ROPE).astype(BF16)
            w_k_p = _pad_heads(w_ukv[i], MLA_HEADS, MLA_NOPE + MLA_V, MLA_NOPE).astype(BF16)
            w_v = w_ukv[i].reshape(MLA_KV_RANK, MLA_HEADS, MLA_NOPE + MLA_V)[:, :, MLA_NOPE:].reshape(
                MLA_KV_RANK, MLA_HEADS * MLA_V).astype(BF16)
            (q, k, v, dq, dk, dv, ckv_st, kpe_st, dk_st, dv_st) = _premix0(
                *xs, mod_l, g_l[0:1], w_in_p, mla_q_norm[i][None, :], w_uq_p, mla_kv_norm[i][None, :],
                w_k_p, w_v, t32, t64)
            states['mla_ckv'] = ckv_st.reshape(BATCH, 1, SEQ, MLA_KV_RANK)
            states['mla_krope'] = kpe_st[:, MLA_NOPE:MLA_NOPE + MLA_ROPE].reshape(BATCH, 1, SEQ, MLA_ROPE)
            states['diff_k'] = dk_st.reshape(BATCH, 1, SEQ, DIFF_HEADS, 2 * DIFF_DH)
            states['diff_v'] = dv_st.reshape(BATCH, 1, SEQ, DIFF_HEADS, 2 * DIFF_DH)
            kc, vc = _mla_cache(cache_mla_ckv[:, i].reshape(DEC_BATCH * PAST_LEN, MLA_KV_RANK),
                                _pe_slab(cache_mla_krope[:, i].reshape(DEC_BATCH * PAST_LEN, MLA_ROPE)),
                                w_k_p, w_v)
            dkc = cache_diff_k[:, i].reshape(DEC_BATCH * PAST_LEN, 512).astype(BF16)
            dvc = cache_diff_v[:, i].reshape(DEC_BATCH * PAST_LEN, 512).astype(BF16)
            lam = diff_lambda[i]
            sub_g = diff_norm[i][None, :]
            o_p = _attn_ab(q, dq, (k, v, dk, dv), None, lam, sub_g, lam_init,
                           n_batch=BATCH, t_len=SEQ, tq=SEQ, tok_off=0, n_seq=PROMPT_SEQS_PER_STEP)
            o_cat = _attn_ab(q, dq, (k, v, dk, dv), (kc, vc, dkc, dvc), lam, sub_g, lam_init,
                             n_batch=DEC_BATCH, t_len=DEC_SEQ, tq=TQ_AB, tok_off=NP_TOK, out_init=o_p)
            w_out = w_out0[i]
        else:
            wi = w_in1[i]
            n_sq = SWA_HEADS * HEAD_DIM
            w_in_p = jnp.concatenate([_pair_kv_heads(wi[:, :n_sq].T).T, wi[:, n_sq:]], axis=1).astype(BF16)
            (sq, sk, sv, nq, nk, nv, sk_st, sv_st, nk_st, nv_st) = _premix1(xs[0], mod_l, g_l[0:1], w_in_p, t64)
            states['swa_k'] = sk_st.reshape(BATCH, 1, SEQ, SWA_KV_HEADS, HEAD_DIM)
            states['swa_v'] = sv_st.reshape(BATCH, 1, SEQ, SWA_KV_HEADS, HEAD_DIM)
            states['na_k'] = nk_st.reshape(BATCH, 1, SEQ, NA_HEADS, HEAD_DIM)
            states['na_v'] = nv_st.reshape(BATCH, 1, SEQ, NA_HEADS, HEAD_DIM)
            skc = cache_swa_k[:, i].reshape(DEC_BATCH * PAST_LEN, 128).astype(BF16)
            svc = cache_swa_v[:, i].reshape(DEC_BATCH * PAST_LEN, 128).astype(BF16)
            nkc = cache_na_k[:, i].reshape(DEC_BATCH * PAST_LEN, 512).astype(BF16)
            nvc = cache_na_v[:, i].reshape(DEC_BATCH * PAST_LEN, 512).astype(BF16)
            sink = swa_sink[i]
            o_p = _attn_cd_prompt(sink, sq, sk, sv, nq, nk, nv)
            o_cat = _attn_cd_sample(o_p, sink, sq, sk, sv, nq, nk, nv, skc, svc, nkc, nvc, _na_bias(na_rpb[i]))
            wo = w_out1[i]
            w_out = jnp.concatenate([_pair_kv_heads(wo[:n_sq]), wo[n_sq:]], axis=0)
        xs = _ffn(l, o_cat, xs, mod_l, g_l, w_out, w_router[l], b_router[l], w_gu, b_gu, w_down, b_down,
                  split_out=(l == DEPTH - 1))
    return (xs[0].reshape(BATCH, SEQ, D_MODEL), xs[1].reshape(DEC_BATCH, DEC_SEQ, D_MODEL),
            states['mla_ckv'], states['mla_krope'], states['diff_k'], states['diff_v'],
            states['swa_k'], states['swa_v'], states['na_k'], states['na_v'])
```

```python
import functools
import math

import numpy as np
import jax
import jax.numpy as jnp
from jax import lax
from jax.experimental import pallas as pl
from jax.experimental.pallas import tpu as pltpu
from jax.experimental.pallas import tpu_sc as plsc

F32 = jnp.float32
BF16 = jnp.bfloat16

D_MODEL = 1024
BATCH = 16
SEQ = 256
DEPTH = 2
DEC_BATCH = 8
DEC_SEQ = 2048
PAST_LEN = 256
GRID_W = 64
HEAD_DIM = 64
ROPE_THETA = 10000.0
EPS = 1e-6
NEG = -1e30

MLA_HEADS = 8
MLA_Q_RANK = 384
MLA_KV_RANK = 256
MLA_NOPE = 64
MLA_ROPE = 32
MLA_V = 64
DIFF_HEADS = 4
DIFF_DH = 64
SWA_HEADS = 8
SWA_KV_HEADS = 2
SWA_WINDOW = 128
NA_HEADS = 8
NA_WIN_ROWS = 8
NA_WIN_COLS = 16
N_EXPERTS = 32
TOP_K = 4
D_EXPERT = 1024
SWIGLU_LIMIT = 7.0
SWIGLU_ALPHA = 1.702

LANES = 128
NP_TOK = BATCH * SEQ
NS_TOK = DEC_BATCH * DEC_SEQ
N_TOK = NP_TOK + NS_TOK
TM = 512
NPT = NP_TOK // TM
TILES_PER_SAMPLE = DEC_SEQ // TM
N_TILES = N_TOK // TM
TM_WIDE = 1024
N_SUB = 2
SUB_TM = TM // N_SUB
TQ = 256
TQ_AB = 256
PROMPT_SEQS_PER_STEP = 2
MOE_TM = 1024
MOE_TAIL_PARTS = 8
MOE_ROWS = ((N_TOK * TOP_K + N_EXPERTS * (MOE_TM - 1)) // MOE_TM + 1) * MOE_TM
MOE_BLOCKS = MOE_ROWS // MOE_TM
NA_TILE_ROWS = TQ // GRID_W
NA_KEY_ROWS = 12
VMEM_LIMIT = 56 * 1024 * 1024


def _cparams(n_axes, vmem=VMEM_LIMIT):
    return pltpu.CompilerParams(dimension_semantics=("arbitrary",) * n_axes,
                                vmem_limit_bytes=vmem)


def _rms(x, g):
    return x * lax.rsqrt(jnp.mean(x * x, axis=-1, keepdims=True) + EPS) * g


def _dot(a, b):
    return jnp.dot(a, b, preferred_element_type=F32)


def _dot_nt(a, b):
    return lax.dot_general(a, b, (((1,), (1,)), ((), ())), preferred_element_type=F32)


def _rope(x, cos, sin_a, sin_b, half):
    return (x * cos + pltpu.roll(x, LANES - half, 1) * sin_a + pltpu.roll(x, half, 1) * sin_b)


def _pipelined_units(units):
    pending = units[0][0](units[0][2])
    for n, (_, finish, arg) in enumerate(units):
        following = units[n + 1][0](units[n + 1][2]) if n + 1 < len(units) else None
        finish(arg, pending)
        pending = following


def _sub_rows(n):
    return slice(SUB_TM * n, SUB_TM * (n + 1))


def _store_head_rows(st_ref, n, hd, n_heads, x):
    st_ref[pl.ds(SUB_TM * n * n_heads + hd, SUB_TM, stride=n_heads), :] = x


def _mod_row(i, tm=TM):
    npt = NP_TOK // tm
    return jnp.where(i < npt, 0, 1 + (i - npt) // (DEC_SEQ // tm))


def _rope_blk(i):
    return jnp.where(i < NPT, TILES_PER_SAMPLE, (i - NPT) % TILES_PER_SAMPLE)


def _mod_kernel(c_ref, w_ref, b_ref, o_ref):
    c = c_ref[...]
    s = (c * jax.nn.sigmoid(c)).astype(BF16)
    o_ref[0] = _dot(s, w_ref[0].astype(BF16)) + b_ref[0]


def _modulation(cond, w_mod, b_mod):
    nb = 1024
    return pl.pallas_call(
        _mod_kernel,
        grid=(DEPTH, 6 * D_MODEL // nb),
        in_specs=[
            pl.BlockSpec((16, D_MODEL), lambda l, n: (0, 0)),
            pl.BlockSpec((1, D_MODEL, nb), lambda l, n: (l, 0, n)),
            pl.BlockSpec((1, 1, nb), lambda l, n: (l, 0, n)),
        ],
        out_specs=pl.BlockSpec((1, 16, nb), lambda l, n: (l, 0, n)),
        out_shape=jax.ShapeDtypeStruct((DEPTH, 16, 6 * D_MODEL), F32),
        compiler_params=_cparams(2),
        name="modulation",
    )(cond, w_mod, b_mod.reshape(DEPTH, 1, 6 * D_MODEL))


def _rope_tables():
    t = jnp.arange(DEC_SEQ)
    rows = (t // GRID_W).astype(F32)
    cols = (t % GRID_W).astype(F32)

    def angles(r):
        n = r // 4
        inv = ROPE_THETA ** (-jnp.arange(n, dtype=F32) / n)
        return jnp.concatenate([rows[:, None] * inv[None], cols[:, None] * inv[None]], axis=-1)

    def finish(cos, sa, sb):
        ident = (jnp.ones((TM, LANES), F32), jnp.zeros((TM, LANES), F32), jnp.zeros((TM, LANES), F32))
        return tuple(jnp.concatenate([a, b], axis=0) for a, b in zip((cos, sa, sb), ident))

    a64 = angles(64)
    c, s, z = jnp.cos(a64), jnp.sin(a64), jnp.zeros_like(a64)
    t64 = finish(jnp.concatenate([c, c, c, c], -1), jnp.concatenate([-s, z, -s, z], -1),
                 jnp.concatenate([z, s, z, s], -1))
    a32 = angles(32)
    c, s, z = jnp.cos(a32), jnp.sin(a32), jnp.zeros_like(a32)
    one64 = jnp.ones((DEC_SEQ, 64), F32)
    z64 = jnp.zeros((DEC_SEQ, 64), F32)
    z32 = jnp.zeros((DEC_SEQ, 32), F32)
    t32 = finish(jnp.concatenate([one64, c, c, z32], -1), jnp.concatenate([z64, -s, z, z32], -1),
                 jnp.concatenate([z64, z, s, z32], -1))
    return t64, t32


LOG2E = math.log2(math.e)
_DIFF_COLS = DIFF_HEADS * 2 * DIFF_DH
_AB_COLS = tuple(np.cumsum([0, MLA_Q_RANK, MLA_KV_RANK, _DIFF_COLS, _DIFF_COLS, _DIFF_COLS, LANES]).tolist())
_CD_COLS = tuple(np.cumsum([0, SWA_HEADS * HEAD_DIM, SWA_KV_HEADS * HEAD_DIM, SWA_KV_HEADS * HEAD_DIM,
                            NA_HEADS * HEAD_DIM, NA_HEADS * HEAD_DIM, NA_HEADS * HEAD_DIM]).tolist())
IN_COLS = _AB_COLS[-1]
assert IN_COLS == _CD_COLS[-1]
_MLA_SCALE = (MLA_NOPE + MLA_ROPE) ** -0.5 * LOG2E
_QSCALE = HEAD_DIM ** -0.5 * LOG2E


def _premix0_kernel(xp_ref, xs_ref, mod_ref, g_ref, win_ref, qn_ref, wuq_ref, kvn_ref, wk_ref, wv_ref,
                    c32_ref, sa32_ref, sb32_ref, c64_ref, sa64_ref, sb64_ref,
                    q_ref, k_ref, v_ref, dq_ref, dk_ref, dv_ref,
                    ckv_st, kpe_st, dk_st, dv_st):
    i = pl.program_id(0)
    m = mod_ref[0]
    gain = g_ref[...] * (1.0 + m[1:2])

    def project(n):
        r = _sub_rows(n)
        x = jnp.where(i < NPT, xp_ref[r, :], xs_ref[r, :])
        h = _rms(x, gain) + m[0:1]
        return _dot(h.astype(BF16), win_ref[...])

    states = {}

    def finish(n, proj):
        r = _sub_rows(n)
        q_a, kv_a, dq, dk, dv, pe = (proj[:, a:b] for a, b in zip(_AB_COLS[:-1], _AB_COLS[1:]))
        q = _dot(_rms(q_a, qn_ref[...]).astype(BF16), wuq_ref[...])
        ckv = _rms(kv_a, kvn_ref[...])
        ckv_b = ckv.astype(BF16)
        kn = _dot(ckv_b, wk_ref[...])
        v_ref[r, :] = _dot(ckv_b, wv_ref[...]).astype(BF16)
        c32, sa32, sb32 = c32_ref[r, :], sa32_ref[r, :], sb32_ref[r, :]
        c64, sa64, sb64 = c64_ref[r, :], sa64_ref[r, :], sb64_ref[r, :]
        pe_r = _rope(pe, c32, sa32, sb32, MLA_ROPE // 2)
        for hd in range(MLA_HEADS):
            sl = slice(LANES * hd, LANES * (hd + 1))
            q_ref[r, sl] = (_rope(q[:, sl], c32, sa32, sb32, MLA_ROPE // 2) * _MLA_SCALE).astype(BF16)
            k_ref[r, sl] = (kn[:, sl] + pe_r).astype(BF16)
        for hd in range(DIFF_HEADS):
            sl = slice(LANES * hd, LANES * (hd + 1))
            dq_ref[r, sl] = (_rope(dq[:, sl], c64, sa64, sb64, DIFF_DH // 2) * _QSCALE).astype(BF16)
            dk_ref[r, sl] = _rope(dk[:, sl], c64, sa64, sb64, DIFF_DH // 2).astype(BF16)
        dv_ref[r, :] = dv.astype(BF16)
        states[n] = (ckv, pe, dk, dv)

    _pipelined_units([(project, finish, n) for n in range(N_SUB)])

    @pl.when(i < NPT)
    def _():
        for n in range(N_SUB):
            r = _sub_rows(n)
            ckv, pe, dk, dv = states[n]
            ckv_st[r, :] = ckv
            kpe_st[r, :] = pe
            for hd in range(DIFF_HEADS):
                _store_head_rows(dk_st, n, hd, DIFF_HEADS, dk[:, LANES * hd: LANES * (hd + 1)])
                _store_head_rows(dv_st, n, hd, DIFF_HEADS, dv[:, LANES * hd: LANES * (hd + 1)])


def _premix1_kernel(x_ref, mod_ref, g_ref, win_ref, c64_ref, sa64_ref, sb64_ref,
                    sq_ref, sk_ref, sv_ref, nq_ref, nk_ref, nv_ref,
                    sk_st, sv_st, nk_st, nv_st):
    i = pl.program_id(0)
    m = mod_ref[0]
    gain = g_ref[...] * (1.0 + m[1:2])

    def project(n):
        h = _rms(x_ref[_sub_rows(n), :], gain) + m[0:1]
        return _dot(h.astype(BF16), win_ref[...])

    states = {}

    def finish(n, proj):
        r = _sub_rows(n)
        sq, sk, sv, nq, nk, nv = (proj[:, a:b] for a, b in zip(_CD_COLS[:-1], _CD_COLS[1:]))
        c64, sa64, sb64 = c64_ref[r, :], sa64_ref[r, :], sb64_ref[r, :]
        for hd in range(4):
            sl = slice(LANES * hd, LANES * (hd + 1))
            sq_ref[r, sl] = (_rope(sq[:, sl], c64, sa64, sb64, HEAD_DIM // 2) * _QSCALE).astype(BF16)
        sk_ref[r, :] = _rope(sk, c64, sa64, sb64, HEAD_DIM // 2).astype(BF16)
        sv_ref[r, :] = sv.astype(BF16)
        nq_ref[r, :] = (nq * _QSCALE).astype(BF16)
        nk_ref[r, :] = nk.astype(BF16)
        nv_ref[r, :] = nv.astype(BF16)
        states[n] = (sk, sv, nk, nv)

    _pipelined_units([(project, finish, n) for n in range(N_SUB)])

    @pl.when(i < NPT)
    def _():
        for n in range(N_SUB):
            r = _sub_rows(n)
            sk, sv, nk, nv = states[n]
            sk_st[r, :] = sk
            sv_st[r, :] = sv
            for hd in range(NA_HEADS):
                _store_head_rows(nk_st, n, hd, NA_HEADS, nk[:, HEAD_DIM * hd: HEAD_DIM * (hd + 1)])
                _store_head_rows(nv_st, n, hd, NA_HEADS, nv[:, HEAD_DIM * hd: HEAD_DIM * (hd + 1)])


def _tok_spec(width, tm=TM):
    return pl.BlockSpec((tm, width), lambda i: (i, 0))


def _prompt_spec(tm=TM):
    return pl.BlockSpec((tm, D_MODEL), lambda i: (jnp.minimum(i, NP_TOK // tm - 1), 0))


def _sample_spec(tm=TM):
    return pl.BlockSpec((tm, D_MODEL), lambda i: (jnp.maximum(i - NP_TOK // tm, 0), 0))


_PROMPT_SPEC = _prompt_spec()
_SAMPLE_SPEC = _sample_spec()


def _state_spec(width, rows_per_token=1):
    return pl.BlockSpec((TM * rows_per_token, width), lambda i: (jnp.minimum(i, NPT - 1), 0))


def _const_spec(shape):
    return pl.BlockSpec(shape, lambda i: (0,) * len(shape))


def _mod_spec(tm=TM):
    return pl.BlockSpec((1, 6, D_MODEL), lambda i: (_mod_row(i, tm), 0, 0))


_MOD_SPEC = _mod_spec()
_ROPE_SPEC = pl.BlockSpec((TM, LANES), lambda i: (_rope_blk(i), 0))


def _premix0(xp, xs, mod_l, g0, w_in_p, q_norm, w_uq_p, kv_norm, w_k_p, w_v, t32, t64):
    outs = [(N_TOK, 1024, BF16), (N_TOK, 1024, BF16), (N_TOK, 512, BF16), (N_TOK, 512, BF16),
            (N_TOK, 512, BF16), (N_TOK, 512, BF16),
            (NP_TOK, 256, F32), (NP_TOK, 128, F32),
            (NP_TOK * DIFF_HEADS, 2 * DIFF_DH, F32), (NP_TOK * DIFF_HEADS, 2 * DIFF_DH, F32)]
    return pl.pallas_call(
        _premix0_kernel,
        grid=(N_TILES,),
        in_specs=[_PROMPT_SPEC, _SAMPLE_SPEC, _MOD_SPEC, _const_spec((1, D_MODEL)),
                  _const_spec((D_MODEL, IN_COLS)), _const_spec((1, MLA_Q_RANK)),
                  _const_spec((MLA_Q_RANK, 1024)), _const_spec((1, MLA_KV_RANK)),
                  _const_spec((MLA_KV_RANK, 1024)), _const_spec((MLA_KV_RANK, 512))]
                 + [_ROPE_SPEC] * 6,
        out_specs=([_tok_spec(w) for (_, w, _) in outs[:6]]
                   + [_state_spec(w, n // NP_TOK) for (n, w, _) in outs[6:]]),
        out_shape=[jax.ShapeDtypeStruct((n, w), dt) for (n, w, dt) in outs],
        compiler_params=_cparams(1),
        name="premix_ab",
    )(xp, xs, mod_l, g0, w_in_p, q_norm, w_uq_p, kv_norm, w_k_p, w_v, *t32, *t64)


def _premix1(x, mod_l, g0, w_in_p, t64):
    outs = [(N_TOK, 512, BF16), (N_TOK, 128, BF16), (N_TOK, 128, BF16), (N_TOK, 512, BF16),
            (N_TOK, 512, BF16), (N_TOK, 512, BF16),
            (NP_TOK, 128, F32), (NP_TOK, 128, F32),
            (NP_TOK * NA_HEADS, HEAD_DIM, F32), (NP_TOK * NA_HEADS, HEAD_DIM, F32)]
    return pl.pallas_call(
        _premix1_kernel,
        grid=(N_TILES,),
        in_specs=[_tok_spec(D_MODEL), _MOD_SPEC, _const_spec((1, D_MODEL)),
                  _const_spec((D_MODEL, IN_COLS))] + [_ROPE_SPEC] * 3,
        out_specs=([_tok_spec(w) for (_, w, _) in outs[:6]]
                   + [_state_spec(w, n // NP_TOK) for (n, w, _) in outs[6:]]),
        out_shape=[jax.ShapeDtypeStruct((n, w), dt) for (n, w, dt) in outs],
        compiler_params=_cparams(1),
        name="premix_cd",
    )(x, mod_l, g0, w_in_p, *t64)


def _mla_cache_kernel(ckv_ref, pe_ref, wk_ref, wv_ref, k_ref, v_ref):
    c = ckv_ref[...].astype(BF16)
    kn = _dot(c, wk_ref[...])
    v_ref[...] = _dot(c, wv_ref[...]).astype(BF16)
    pe = pe_ref[...]
    for hd in range(MLA_HEADS):
        sl = slice(LANES * hd, LANES * (hd + 1))
        k_ref[:, sl] = (kn[:, sl] + pe).astype(BF16)


def _mla_cache(ckv, pe_slab, w_k_p, w_v):
    n = ckv.shape[0]
    tm = 512
    return pl.pallas_call(
        _mla_cache_kernel,
        grid=(n // tm,),
        in_specs=[pl.BlockSpec((tm, MLA_KV_RANK), lambda i: (i, 0)),
                  pl.BlockSpec((tm, LANES), lambda i: (i, 0)),
                  _const_spec((MLA_KV_RANK, 1024)), _const_spec((MLA_KV_RANK, 512))],
        out_specs=[pl.BlockSpec((tm, 1024), lambda i: (i, 0)), pl.BlockSpec((tm, 512), lambda i: (i, 0))],
        out_shape=[jax.ShapeDtypeStruct((n, 1024), BF16), jax.ShapeDtypeStruct((n, 512), BF16)],
        compiler_params=_cparams(1),
        name="mla_cache",
    )(ckv, pe_slab, w_k_p, w_v)


def _softmax_pv(scores, values, sink=None):
    m = jnp.max(scores[0], axis=-1, keepdims=True)
    for s in scores[1:]:
        m = jnp.maximum(m, jnp.max(s, axis=-1, keepdims=True))
    if sink is not None:
        m = jnp.maximum(m, sink)
    l = None
    o = None
    for s, v in zip(scores, values):
        p = jnp.exp2(s - m)
        ls = jnp.sum(p, axis=-1, keepdims=True)
        os_ = _dot(p.astype(BF16), v)
        l = ls if l is None else l + ls
        o = os_ if o is None else o + os_
    if sink is not None:
        l = l + jnp.exp2(sink - m)
    return o * (1.0 / l)


def _lane_lo(shape):
    return lax.broadcasted_iota(jnp.int32, shape, 1) < (LANES // 2)


def _split_halves(qb):
    lo = _lane_lo(qb.shape)
    zero = jnp.zeros_like(qb)
    return jnp.where(lo, qb, zero), jnp.where(lo, zero, qb)


def _attn_ab_kernel(*refs, n_pieces, n_seq, lam_init, aliased):
    if aliased:
        refs = refs[1:]
    q_ref, dq_ref = refs[0], refs[1]
    pieces = [refs[2 + 4 * p: 6 + 4 * p] for p in range(n_pieces)]
    lam_ref, subg_ref, o_ref = refs[2 + 4 * n_pieces:]
    lam = lam_ref[...]
    lam_full = (jnp.exp(jnp.sum(lam[0:1] * lam[1:2], axis=-1, keepdims=True))
                - jnp.exp(jnp.sum(lam[2:3] * lam[3:4], axis=-1, keepdims=True)) + lam_init)
    tq = q_ref.shape[0] // n_seq
    lo = _lane_lo((tq, LANES))
    subg = subg_ref[...] * (1.0 - lam_init)

    def seq_rows(ref, sq):
        n = ref.shape[0] // n_seq
        return slice(n * sq, n * (sq + 1))

    def mla_scores(arg):
        sq, hd = arg
        sl = slice(LANES * hd, LANES * (hd + 1))
        qh = q_ref[seq_rows(q_ref, sq), sl]
        return [_dot_nt(qh, k_ref[seq_rows(k_ref, sq), sl]) for (k_ref, _, _, _) in pieces]

    def diff_scores(arg):
        sq, hd = arg
        sl = slice(LANES * hd, LANES * (hd + 1))
        qq = jnp.concatenate(_split_halves(dq_ref[seq_rows(dq_ref, sq), sl]), axis=0)
        return [_dot_nt(qq, dk_ref[seq_rows(dk_ref, sq), sl]) for (_, _, dk_ref, _) in pieces]

    pair = {}

    def mla_finish(arg, scores):
        sq, hd = arg
        j = hd // 2
        vals = [v_ref[seq_rows(v_ref, sq), LANES * j: LANES * (j + 1)] for (_, v_ref, _, _) in pieces]
        pair[hd % 2] = _softmax_pv(scores, vals)
        if hd % 2 == 1:
            o_ref[seq_rows(o_ref, sq), LANES * j: LANES * (j + 1)] = jnp.where(lo, pair[0], pair[1]).astype(BF16)

    def diff_finish(arg, scores):
        sq, hd = arg
        sl = slice(LANES * hd, LANES * (hd + 1))
        oo = _softmax_pv(scores, [dv_ref[seq_rows(dv_ref, sq), sl] for (_, _, _, dv_ref) in pieces])
        od = _rms(oo[:tq] - lam_full * oo[tq:], subg)
        o_ref[seq_rows(o_ref, sq), 512 + LANES * hd: 512 + LANES * (hd + 1)] = od.astype(BF16)

    units = []
    for sq in range(n_seq):
        for j in range(DIFF_HEADS):
            units += [(diff_scores, diff_finish, (sq, j)), (mla_scores, mla_finish, (sq, 2 * j)),
                      (mla_scores, mla_finish, (sq, 2 * j + 1))]
    _pipelined_units(units)


def _attn_ab(q, dq, new_kv, cache_kv, lam, sub_g, lam_init, *, n_batch, t_len, tq, tok_off, out_init=None,
             n_seq=1):
    assert n_seq == 1 or (tq == t_len and cache_kv is None and n_batch % n_seq == 0)
    n_batch, t_len, tq = n_batch // n_seq, t_len * n_seq, tq * n_seq
    nq = t_len // tq
    q_off = tok_off // tq
    b_off = tok_off // t_len
    widths = (1024, 512, 512, 512)
    in_specs = [pl.BlockSpec((tq, 1024), lambda b, i: (q_off + b * nq + i, 0)),
                pl.BlockSpec((tq, 512), lambda b, i: (q_off + b * nq + i, 0))]
    args = [q, dq]
    for w, a in zip(widths, new_kv):
        in_specs.append(pl.BlockSpec((t_len, w), lambda b, i: (b_off + b, 0)))
        args.append(a)
    n_pieces = 1
    if cache_kv is not None:
        n_pieces = 2
        for w, a in zip(widths, cache_kv):
            in_specs.append(pl.BlockSpec((PAST_LEN, w), lambda b, i: (b, 0)))
            args.append(a)
    in_specs += [pl.BlockSpec((4, DIFF_DH), lambda b, i: (0, 0)),
                 pl.BlockSpec((1, 2 * DIFF_DH), lambda b, i: (0, 0))]
    args += [lam, sub_g]
    aliases = {}
    if out_init is not None:
        in_specs = [pl.BlockSpec(memory_space=pl.ANY)] + in_specs
        args = [out_init] + args
        aliases = {0: 0}
    return pl.pallas_call(
        functools.partial(_attn_ab_kernel, n_pieces=n_pieces, n_seq=n_seq, lam_init=lam_init,
                          aliased=out_init is not None),
        grid=(n_batch, nq),
        in_specs=in_specs,
        out_specs=pl.BlockSpec((tq, 1024), lambda b, i: (q_off + b * nq + i, 0)),
        out_shape=jax.ShapeDtypeStruct((N_TOK, 1024), BF16),
        input_output_aliases=aliases,
        compiler_params=_cparams(2),
        name="attn_ab_%d" % n_pieces,
    )(*args)


def _gqa_stacks(sq_ref, sink_ref):
    tq = sq_ref.shape[0]
    halves = [_split_halves(sq_ref[:, LANES * j: LANES * (j + 1)]) for j in range(4)]
    q_stacks = [jnp.concatenate([halves[j][kvh] for j in range(4)], axis=0) for kvh in range(SWA_KV_HEADS)]
    sinks = [jnp.concatenate([jnp.full((tq, 1), sink_ref[4 * kvh + j] * LOG2E, F32) for j in range(4)], axis=0)
             for kvh in range(SWA_KV_HEADS)]
    return q_stacks, sinks


def _attn_cd_prompt_kernel(sink_ref, sq_ref, sk_ref, sv_ref, nq_ref, nk_ref, nv_ref, o_ref):
    lo = _lane_lo((SEQ, LANES))
    for sq in range(PROMPT_SEQS_PER_STEP):
        r = slice(SEQ * sq, SEQ * (sq + 1))
        sk = sk_ref[r, :]
        sv = sv_ref[r, :]
        for j in range(4):
            sl = slice(LANES * j, LANES * (j + 1))
            q_lo, q_hi = _split_halves(sq_ref[r, sl])
            o_lo = _softmax_pv([_dot_nt(q_lo, sk)], [sv], sink=sink_ref[j] * LOG2E)
            o_hi = _softmax_pv([_dot_nt(q_hi, sk)], [sv], sink=sink_ref[j + 4] * LOG2E)
            o_ref[r, sl] = jnp.where(lo, o_lo, o_hi).astype(BF16)
        for j in range(4):
            sl = slice(LANES * j, LANES * (j + 1))
            q_lo, q_hi = _split_halves(nq_ref[r, sl])
            k = nk_ref[r, sl]
            v = nv_ref[r, sl]
            o_lo = _softmax_pv([_dot_nt(q_lo, k)], [v])
            o_hi = _softmax_pv([_dot_nt(q_hi, k)], [v])
            o_ref[r, 512 + LANES * j: 512 + LANES * (j + 1)] = jnp.where(lo, o_lo, o_hi).astype(BF16)


def _attn_cd_prompt(sink, sq, sk, sv, nq, nk, nv):
    def spec(w):
        return pl.BlockSpec((SEQ * PROMPT_SEQS_PER_STEP, w), lambda b: (b, 0))
    return pl.pallas_call(
        _attn_cd_prompt_kernel,
        grid=(BATCH // PROMPT_SEQS_PER_STEP,),
        in_specs=[pl.BlockSpec(memory_space=pltpu.SMEM), spec(512), spec(128), spec(128),
                  spec(512), spec(512), spec(512)],
        out_specs=spec(1024),
        out_shape=jax.ShapeDtypeStruct((N_TOK, 1024), BF16),
        compiler_params=_cparams(1),
        name="attn_cd_prompt",
    )(sink, sq, sk, sv, nq, nk, nv)


_SWA_KEYS = TQ + 2 * SWA_WINDOW


def _attn_cd_sample_kernel(init_ref, sink_ref, sq_ref, nq_ref, sk_ref, sv_ref, nk_ref, nv_ref,
                           skc_ref, svc_ref, nkc_ref, nvc_ref, bias_ref, o_ref):
    del init_ref
    qi = pl.program_id(1)
    lo = _lane_lo((TQ, LANES))
    ks = pl.multiple_of(jnp.clip(qi * TQ - SWA_WINDOW, 0, DEC_SEQ - _SWA_KEYS), SWA_WINDOW)
    k_win = sk_ref[pl.ds(ks, _SWA_KEYS), :]
    v_win = sv_ref[pl.ds(ks, _SWA_KEYS), :]
    q_pos = qi * TQ + (lax.broadcasted_iota(jnp.int32, (4 * TQ, _SWA_KEYS), 0) & (TQ - 1))
    k_pos = ks + lax.broadcasted_iota(jnp.int32, (4 * TQ, _SWA_KEYS), 1)
    in_win = jnp.abs(q_pos - k_pos) <= SWA_WINDOW
    skc = skc_ref[...]
    svc = svc_ref[...]
    q_stacks, sinks = _gqa_stacks(sq_ref, sink_ref)

    def window_scores(kvh):
        return [_dot_nt(q_stacks[kvh], skc), jnp.where(in_win, _dot_nt(q_stacks[kvh], k_win), NEG)]

    o_kv = {}

    def window_finish(kvh, scores):
        o_kv[kvh] = _softmax_pv(scores, [svc, v_win], sink=sinks[kvh])
        if kvh == SWA_KV_HEADS - 1:
            for j in range(4):
                rows = slice(TQ * j, TQ * (j + 1))
                o_ref[:, LANES * j: LANES * (j + 1)] = jnp.where(lo, o_kv[0][rows], o_kv[1][rows]).astype(BF16)

    n_rows = DEC_SEQ // GRID_W
    r0 = jnp.clip(qi * NA_TILE_ROWS - NA_WIN_ROWS // 2, 0, n_rows - NA_KEY_ROWS)
    kn = pl.multiple_of(r0 * GRID_W, GRID_W)

    lane_lo = _lane_lo((1, LANES))
    pieces = []
    for ri in range(NA_TILE_ROWS):
        r = qi * NA_TILE_ROWS + ri
        rs = jnp.clip(r - NA_WIN_ROWS // 2, 0, n_rows - NA_WIN_ROWS)
        row = []
        for mm in range(NA_KEY_ROWS // 2):
            kr = r0 + 2 * mm
            ok = [(kr + t >= rs) & (kr + t < rs + NA_WIN_ROWS) for t in range(2)]
            mask = jnp.where(lane_lo, jnp.where(ok[0], 0.0, NEG), jnp.where(ok[1], 0.0, NEG))
            row.append((jnp.clip(kr - r + NA_WIN_ROWS, 0, 2 * NA_WIN_ROWS - 1), mask))
        pieces.append(row)

    def add_na_bias(hd, s):
        return jnp.concatenate(
            [jnp.concatenate([s[GRID_W * ri: GRID_W * (ri + 1), LANES * mm: LANES * (mm + 1)] + bias_ref[hd, d] + mask
                              for mm, (d, mask) in enumerate(row)], axis=1)
             for ri, row in enumerate(pieces)], axis=0)

    def na_scores(hd):
        sl = slice(LANES * (hd // 2), LANES * (hd // 2 + 1))
        q_half = _split_halves(nq_ref[:, sl])[hd % 2]
        nk_win = nk_ref[pl.ds(kn, NA_KEY_ROWS * GRID_W), sl]
        return [_dot_nt(q_half, nkc_ref[:, sl]), add_na_bias(hd, _dot_nt(q_half, nk_win))]

    pair = {}

    def na_finish(hd, scores):
        j = hd // 2
        sl = slice(LANES * j, LANES * (j + 1))
        nv_win = nv_ref[pl.ds(kn, NA_KEY_ROWS * GRID_W), sl]
        pair[hd % 2] = _softmax_pv(scores, [nvc_ref[:, sl], nv_win])
        if hd % 2 == 1:
            o_ref[:, 512 + LANES * j: 512 + LANES * (j + 1)] = jnp.where(lo, pair[0], pair[1]).astype(BF16)

    _pipelined_units([(window_scores, window_finish, kvh) for kvh in range(SWA_KV_HEADS)]
                     + [(na_scores, na_finish, hd) for hd in range(NA_HEADS)])


def _attn_cd_sample(out_init, sink, sq, sk, sv, nq, nk, nv, skc, svc, nkc, nvc, bias):
    nq_t = DEC_SEQ // TQ
    q_off = NP_TOK // TQ
    b_off = NP_TOK // DEC_SEQ

    def qspec(w):
        return pl.BlockSpec((TQ, w), lambda b, i: (q_off + b * nq_t + i, 0))

    def kspec(w):
        return pl.BlockSpec((DEC_SEQ, w), lambda b, i: (b_off + b, 0))

    def cspec(w):
        return pl.BlockSpec((PAST_LEN, w), lambda b, i: (b, 0))

    bias_spec = pl.BlockSpec((NA_HEADS, 2 * NA_WIN_ROWS, GRID_W, LANES), lambda b, i: (0, 0, 0, 0))
    return pl.pallas_call(
        _attn_cd_sample_kernel,
        grid=(DEC_BATCH, nq_t),
        in_specs=[pl.BlockSpec(memory_space=pl.ANY), pl.BlockSpec(memory_space=pltpu.SMEM),
                  qspec(512), qspec(512), kspec(128), kspec(128), kspec(512), kspec(512),
                  cspec(128), cspec(128), cspec(512), cspec(512), bias_spec],
        out_specs=pl.BlockSpec((TQ, 1024), lambda b, i: (q_off + b * nq_t + i, 0)),
        out_shape=jax.ShapeDtypeStruct((N_TOK, 1024), BF16),
        input_output_aliases={0: 0},
        compiler_params=_cparams(2),
        name="attn_cd_sample",
    )(out_init, sink, sq, nq, sk, sv, nk, nv, skc, svc, nkc, nvc, bias)


def _na_bias(rpb):
    n_dc = 2 * NA_WIN_COLS - 1
    c = np.arange(GRID_W)[:, None]
    kc = np.arange(GRID_W)[None, :]
    qs = np.clip(c - NA_WIN_COLS // 2, 0, GRID_W - NA_WIN_COLS)
    col_ok = (kc >= qs) & (kc < qs + NA_WIN_COLS)
    dc = np.clip(kc - c + NA_WIN_COLS - 1, 0, n_dc - 1)
    onehot = ((dc[None] == np.arange(n_dc)[:, None, None]) & col_ok[None]).astype(np.float32)
    blocks = jnp.einsum('hrd,dck->hrck', rpb.astype(F32) * LOG2E, onehot, precision=lax.Precision.HIGHEST)
    blocks = jnp.where(col_ok[None, None], blocks, NEG)
    none = jnp.full((NA_HEADS, 1, GRID_W, GRID_W), NEG, F32)
    return jnp.concatenate([jnp.concatenate([none, blocks], axis=1),
                            jnp.concatenate([blocks, none], axis=1)], axis=-1)


_HI_MASK = -65536


def _pack_pairs(x):
    w = x.shape[1] // 2
    r = x.astype(BF16).astype(F32)
    lo = lax.bitcast_convert_type(r[:, :w], jnp.int32)
    hi = lax.bitcast_convert_type(r[:, w:], jnp.int32)
    return (hi & _HI_MASK) | lax.shift_right_logical(lo, 16)


def _unpack_pairs(p):
    lo = lax.bitcast_convert_type(lax.shift_left(p, 16), F32)
    hi = lax.bitcast_convert_type(p & _HI_MASK, F32)
    return lo, hi


def _postmix_kernel(*refs, split_x, tm):
    if split_x:
        o_ref, xp_ref, xs_ref = refs[:3]
        refs = refs[3:]
    else:
        o_ref, x_ref = refs[:2]
        refs = refs[2:]
    (mod_ref, g_ref, wout_ref, wr_ref, br_ref, tri_ref,
     x1_ref, h2_ref, route_ref, gate_ref, cnt_ref, run_ref) = refs
    i = pl.program_id(0)

    @pl.when(i == 0)
    def _():
        run_ref[...] = jnp.zeros_like(run_ref)

    m = mod_ref[0]
    g = g_ref[...]
    gate_gain = g[1:2] * m[2:3]
    ffn_gain = g[2:3] * (1.0 + m[4:5])
    sub_logits = {}
    sub = tm // N_SUB
    npt = NP_TOK // tm

    def project(n):
        return _dot(o_ref[sub * n: sub * (n + 1), :], wout_ref[...])

    def finish(n, y):
        r = slice(sub * n, sub * (n + 1))
        x = jnp.where(i < npt, xp_ref[r, :], xs_ref[r, :]) if split_x else x_ref[r, :]
        x1 = x + _rms(y, gate_gain)
        x1_ref[r, :] = x1
        h2 = _rms(x1, ffn_gain) + m[3:4]
        h2_ref[r, :] = _pack_pairs(h2)
        sub_logits[n] = _dot(h2.astype(BF16), wr_ref[...]) + br_ref[...]

    _pipelined_units([(project, finish, n) for n in range(N_SUB)])

    logits = jnp.concatenate([sub_logits[n] for n in range(N_SUB)], axis=0)
    lane = lax.broadcasted_iota(jnp.int32, logits.shape, 1).astype(F32)
    cur = jnp.where(lane < N_EXPERTS, logits, -jnp.inf)
    tops, idxs = [], []
    for _ in range(TOP_K):
        mx = jnp.max(cur, axis=-1, keepdims=True)
        ix = jnp.min(jnp.where(cur == mx, lane, float(LANES)), axis=-1, keepdims=True)
        tops.append(mx)
        idxs.append(ix)
        cur = jnp.where(lane == ix, -jnp.inf, cur)
    es = [jnp.exp(t - tops[0]) for t in tops]
    inv = 1.0 / (es[0] + es[1] + es[2] + es[3])
    picked = jnp.zeros_like(logits)
    for k in range(TOP_K):
        picked = jnp.where(lane == idxs[k], 1.0, picked)
    before = _dot(tri_ref[...], picked.astype(BF16)) + run_ref[0:1, :]
    route = jnp.zeros_like(logits)
    gate_out = jnp.zeros_like(logits)
    for k in range(TOP_K):
        rank = jnp.sum(jnp.where(lane == idxs[k], before, 0.0), axis=-1, keepdims=True)
        route = jnp.where(lane == float(k), idxs[k], route)
        route = jnp.where(lane == float(TOP_K + k), rank, route)
        gate_out = jnp.where(lane == float(k), es[k] * inv, gate_out)
    route_ref[...] = route.T[:2 * TOP_K].astype(jnp.int32)
    gate_ref[...] = gate_out
    run_ref[...] = run_ref[...] + jnp.sum(picked, axis=0, keepdims=True)
    cnt_ref[...] = run_ref[...].astype(jnp.int32)


def _postmix(o_cat, xs, mod_l, g_l, w_out, w_r, b_r):
    tm = TM_WIDE
    tri = jnp.asarray(np.tril(np.ones((tm, tm), np.float32), -1), BF16)
    split_x = len(xs) == 2
    x_specs = [_prompt_spec(tm), _sample_spec(tm)] if split_x else [_tok_spec(D_MODEL, tm)]
    return pl.pallas_call(
        functools.partial(_postmix_kernel, split_x=split_x, tm=tm),
        grid=(N_TOK // tm,),
        in_specs=[_tok_spec(1024, tm)] + x_specs + [_mod_spec(tm), _const_spec((4, D_MODEL)),
                  _const_spec((1024, D_MODEL)), _const_spec((D_MODEL, LANES)), _const_spec((1, LANES)),
                  _const_spec((tm, tm))],
        out_specs=[_tok_spec(D_MODEL, tm), _tok_spec(D_MODEL // 2, tm),
                   pl.BlockSpec((2 * TOP_K, tm), lambda i: (0, i)), _tok_spec(LANES, tm), _const_spec((8, LANES))],
        out_shape=[jax.ShapeDtypeStruct((N_TOK, D_MODEL), F32),
                   jax.ShapeDtypeStruct((N_TOK, D_MODEL // 2), jnp.int32),
                   jax.ShapeDtypeStruct((2 * TOP_K, N_TOK), jnp.int32), jax.ShapeDtypeStruct((N_TOK, LANES), F32),
                   jax.ShapeDtypeStruct((8, LANES), jnp.int32)],
        scratch_shapes=[pltpu.VMEM((8, LANES), F32)],
        compiler_params=_cparams(1),
        name="postmix",
    )(o_cat, *xs, mod_l, g_l, w_out, w_r, b_r, tri)


SC_WORKERS = 32
SC_ROWS = 64
ROW_WORDS = D_MODEL // 2

_SC_SCRATCH = [pltpu.VMEM((SC_ROWS, ROW_WORDS), jnp.int32), pltpu.VMEM((SC_ROWS, ROW_WORDS), jnp.int32),
               pltpu.SemaphoreType.DMA, pltpu.SemaphoreType.DMA, pltpu.SemaphoreType.DMA, pltpu.SemaphoreType.DMA]


def _sc_worker_id():
    return lax.axis_index("s") * 2 + lax.axis_index("c")


def _sc_double_buffered(n_chunks, load, store):
    for cp in load(0, 0):
        cp.start()

    @pl.loop(0, n_chunks, step=2)
    def _(g0):
        for b in range(2):
            g = g0 + b
            for cp in load(g, b):
                cp.wait()

            @pl.when(g >= 1)
            def _():
                for cp in store(g - 1, 1 - b):
                    cp.wait()

            @pl.when(g + 1 < n_chunks)
            def _():
                for cp in load(g + 1, 1 - b):
                    cp.start()

            for cp in store(g, b):
                cp.start()

    for cp in store(n_chunks - 1, (n_chunks - 1) % 2):
        cp.wait()


def _sc_dispatch(src, idx):
    n_chunks = N_TOK // (SC_WORKERS * SC_ROWS)
    assert n_chunks % 2 == 0
    mesh = plsc.VectorSubcoreMesh(core_axis_name="c", subcore_axis_name="s")

    @functools.partial(
        pl.kernel, mesh=mesh,
        out_type=jax.ShapeDtypeStruct((MOE_ROWS, ROW_WORDS), jnp.int32),
        scratch_types=[pltpu.VMEM((n_chunks * TOP_K, SC_ROWS), jnp.int32)] + _SC_SCRATCH)
    def k(src_hbm, idx_hbm, out_hbm, idx_v, buf0, buf1, in0, in1, out0, out1):
        wid = _sc_worker_id()
        pltpu.sync_copy(idx_hbm.at[wid], idx_v)
        bufs, in_sems, out_sems = (buf0, buf1), (in0, in1), (out0, out1)

        def load(g, b):
            rows = pl.ds((wid * n_chunks + g) * SC_ROWS, SC_ROWS)
            return [pltpu.make_async_copy(src_hbm.at[rows], bufs[b], in_sems[b])]

        def store(g, b):
            return [pltpu.make_async_copy(bufs[b], out_hbm.at[idx_v.at[g * TOP_K + kk]], out_sems[b])
                    for kk in range(TOP_K)]

        _sc_double_buffered(n_chunks, load, store)

    return k(src, idx)


def _sc_collect(table, idx):
    n_chunks = idx.shape[1]
    assert n_chunks % 2 == 0
    mesh = plsc.VectorSubcoreMesh(core_axis_name="c", subcore_axis_name="s")

    @functools.partial(
        pl.kernel, mesh=mesh,
        out_type=jax.ShapeDtypeStruct((SC_WORKERS * n_chunks * SC_ROWS, ROW_WORDS), jnp.int32),
        scratch_types=[pltpu.VMEM((n_chunks, SC_ROWS), jnp.int32)] + _SC_SCRATCH)
    def k(table_hbm, idx_hbm, out_hbm, idx_v, buf0, buf1, in0, in1, out0, out1):
        wid = _sc_worker_id()
        pltpu.sync_copy(idx_hbm.at[wid], idx_v)
        bufs, in_sems, out_sems = (buf0, buf1), (in0, in1), (out0, out1)

        def load(g, b):
            return [pltpu.make_async_copy(table_hbm.at[idx_v.at[g]], bufs[b], in_sems[b])]

        def store(g, b):
            rows = pl.ds((wid * n_chunks + g) * SC_ROWS, SC_ROWS)
            return [pltpu.make_async_copy(bufs[b], out_hbm.at[rows], out_sems[b])]

        _sc_double_buffered(n_chunks, load, store)

    return k(table, idx)


def _expert_rows(words, n_valid, wgu_b, wd_b, bgu, bd):
    live = lax.broadcasted_iota(jnp.int32, words.shape, 0) < n_valid
    lo, hi = _unpack_pairs(jnp.where(live, words, 0))
    x = jnp.concatenate([lo, hi], axis=1).astype(BF16)
    gu = _dot(x, wgu_b[...]) + bgu
    g = jnp.minimum(gu[:, :D_EXPERT], SWIGLU_LIMIT)
    u = jnp.clip(gu[:, D_EXPERT:], -SWIGLU_LIMIT, SWIGLU_LIMIT)
    a = g * jax.nn.sigmoid(SWIGLU_ALPHA * g) * (u + 1.0)
    return _pack_pairs(_dot(a.astype(BF16), wd_b[...]) + bd)


def _moe_kernel(blk_e_ref, blk_first_ref, blk_rows_ref, blk_slot_ref, blk_next_ref,
                x_ref, wgu_hbm, bgu_ref, wd_hbm, bd_ref, y_ref,
                wgu_f, wd_f, wgu_b, wd_b, sem, *, layer):
    i = pl.program_id(0)
    n_valid = blk_rows_ref[i]
    quantum = MOE_TM // MOE_TAIL_PARTS

    def weight_copies(e, slot):
        return (pltpu.make_async_copy(wgu_hbm.at[layer, e], wgu_f.at[slot], sem.at[0, slot]),
                pltpu.make_async_copy(wd_hbm.at[layer, e], wd_f.at[slot], sem.at[1, slot]))

    @pl.when(i == 0)
    def _():
        for cp in weight_copies(blk_e_ref[0], blk_slot_ref[0]):
            cp.start()

    @pl.when(blk_first_ref[i] == 1)
    def _():
        slot = blk_slot_ref[i]
        for cp in weight_copies(blk_e_ref[i], slot):
            cp.wait()
        nxt = blk_next_ref[i]

        @pl.when(nxt >= 0)
        def _():
            for cp in weight_copies(nxt, 1 - slot):
                cp.start()

        wgu_b[...] = wgu_f[slot].astype(BF16)
        wd_b[...] = wd_f[slot].astype(BF16)

    for parts in range(1, MOE_TAIL_PARTS + 1):
        rows = parts * quantum

        @pl.when((n_valid > rows - quantum) & (n_valid <= rows))
        def _(rows=rows):
            y_ref[:rows] = _expert_rows(x_ref[:rows], n_valid, wgu_b, wd_b, bgu_ref[0, 0], bd_ref[0, 0])
            if rows < MOE_TM:
                y_ref[rows:] = jnp.zeros((MOE_TM - rows, ROW_WORDS), jnp.int32)

    @pl.when(n_valid == 0)
    def _():
        y_ref[...] = jnp.zeros_like(y_ref)


def _moe(layer, blk_meta, xs, w_gu, b_gu, w_down, b_down):
    def row_map(i, *_):
        return (i, 0)

    def bias_map(i, e, *_):
        return (layer, e[i], 0, 0)

    grid_spec = pltpu.PrefetchScalarGridSpec(
        num_scalar_prefetch=5,
        grid=(MOE_BLOCKS,),
        in_specs=[
            pl.BlockSpec((MOE_TM, ROW_WORDS), row_map),
            pl.BlockSpec(memory_space=pl.ANY),
            pl.BlockSpec((1, 1, 1, 2 * D_EXPERT), bias_map),
            pl.BlockSpec(memory_space=pl.ANY),
            pl.BlockSpec((1, 1, 1, D_MODEL), bias_map),
        ],
        out_specs=pl.BlockSpec((MOE_TM, ROW_WORDS), row_map),
        scratch_shapes=[pltpu.VMEM((2, D_MODEL, 2 * D_EXPERT), F32), pltpu.VMEM((2, D_EXPERT, D_MODEL), F32),
                        pltpu.VMEM((D_MODEL, 2 * D_EXPERT), BF16), pltpu.VMEM((D_EXPERT, D_MODEL), BF16),
                        pltpu.SemaphoreType.DMA((2, 2))],
    )
    return pl.pallas_call(
        functools.partial(_moe_kernel, layer=layer),
        grid_spec=grid_spec,
        out_shape=jax.ShapeDtypeStruct((MOE_ROWS, ROW_WORDS), jnp.int32),
        compiler_params=_cparams(1),
        name="moe_experts",
    )(*blk_meta, xs, w_gu, b_gu.reshape(DEPTH, N_EXPERTS, 1, 2 * D_EXPERT),
      w_down, b_down.reshape(DEPTH, N_EXPERTS, 1, D_MODEL))


def _route(route, counts):
    experts = jnp.arange(N_EXPERTS, dtype=jnp.int32)
    padded = (counts + MOE_TM - 1) // MOE_TM * MOE_TM
    pad_end = jnp.cumsum(padded)
    pad_start = pad_end - padded
    e = route[0:TOP_K]
    onehot = e[:, :, None] == experts[None, None, :]
    dest = jnp.sum(jnp.where(onehot, pad_start[None, None, :], 0), axis=-1) + route[TOP_K:2 * TOP_K]
    blk_row0 = jnp.arange(MOE_BLOCKS, dtype=jnp.int32) * MOE_TM
    blk_e = jnp.minimum(jnp.sum((pad_end[None, :] <= blk_row0[:, None]).astype(jnp.int32), axis=1),
                        N_EXPERTS - 1)
    on = blk_row0 < pad_end[-1]
    n_on = jnp.sum(on.astype(jnp.int32))
    blk_onehot = blk_e[:, None] == experts[None, :]
    row_end = jnp.sum(jnp.where(blk_onehot, (pad_start + counts)[None, :], 0), axis=1)
    blk_rows = jnp.where(on, jnp.clip(row_end - blk_row0, 0, MOE_TM), 0).astype(jnp.int32)
    last_e = jnp.sum(jnp.where(jnp.arange(MOE_BLOCKS) == n_on - 1, blk_e, 0))
    blk_e = jnp.where(on, blk_e, last_e).astype(jnp.int32)
    prev = jnp.concatenate([jnp.full((1,), -1, jnp.int32), blk_e[:-1]])
    blk_first = (blk_e != prev).astype(jnp.int32)
    blk_slot = ((jnp.cumsum(blk_first) - 1) % 2).astype(jnp.int32)
    later_used = (experts[None, :] > experts[:, None]) & (counts[None, :] > 0)
    next_used = jnp.min(jnp.where(later_used, experts[None, :], N_EXPERTS), axis=1)
    next_used = jnp.where(next_used == N_EXPERTS, -1, next_used)
    blk_next = jnp.sum(jnp.where(blk_e[:, None] == experts[None, :], next_used[None, :], 0), axis=1)
    return dest, (blk_e, blk_first, blk_rows, blk_slot, blk_next.astype(jnp.int32))


def _combine_kernel(y_ref, gate_ref, x1_ref, mod_ref, g_ref, *o_refs):
    i = pl.program_id(0)
    m = mod_ref[0]
    gate = gate_ref[...]
    acc_lo = None
    for k in range(TOP_K):
        lo, hi = _unpack_pairs(y_ref[k])
        gk = gate[:, k:k + 1]
        acc_lo = gk * lo if acc_lo is None else acc_lo + gk * lo
        acc_hi = gk * hi if k == 0 else acc_hi + gk * hi
    acc = jnp.concatenate([acc_lo, acc_hi], axis=1)
    out = x1_ref[...] + _rms(acc, g_ref[...][3:4] * m[5:6])
    if len(o_refs) == 1:
        o_refs[0][...] = out
    else:
        @pl.when(i < NPT)
        def _():
            o_refs[0][...] = out

        @pl.when(i >= NPT)
        def _():
            o_refs[1][...] = out


def _combine(yg, gates, x1, mod_l, g_l, split_out):
    if split_out:
        out_specs = [_PROMPT_SPEC, _SAMPLE_SPEC]
        out_shape = [jax.ShapeDtypeStruct((NP_TOK, D_MODEL), F32), jax.ShapeDtypeStruct((NS_TOK, D_MODEL), F32)]
    else:
        out_specs = [_tok_spec(D_MODEL)]
        out_shape = [jax.ShapeDtypeStruct((N_TOK, D_MODEL), F32)]
    return pl.pallas_call(
        _combine_kernel,
        grid=(N_TILES,),
        in_specs=[pl.BlockSpec((TOP_K, TM, ROW_WORDS), lambda i: (0, i, 0)), _tok_spec(LANES),
                  _tok_spec(D_MODEL), _MOD_SPEC, _const_spec((4, D_MODEL))],
        out_specs=out_specs,
        out_shape=out_shape,
        compiler_params=_cparams(1),
        name="combine",
    )(yg, gates, x1, mod_l, g_l)


def _ffn(layer, o_cat, xs, mod_l, g_l, w_out, w_router, b_router, w_gu, b_gu, w_down, b_down, split_out):
    w_r = jnp.pad(w_router, ((0, 0), (0, LANES - N_EXPERTS))).astype(BF16)
    b_r = jnp.pad(b_router, (0, LANES - N_EXPERTS)).reshape(1, LANES)
    x1, h2p, route, gate_slab, counts = _postmix(o_cat, xs, mod_l, g_l, w_out.astype(BF16), w_r, b_r)
    dest, blk_meta = _route(route, counts[0, :N_EXPERTS])
    n_chunks = N_TOK // (SC_WORKERS * SC_ROWS)
    idx_d = dest.reshape(TOP_K, SC_WORKERS, n_chunks, SC_ROWS).transpose(1, 2, 0, 3).reshape(
        SC_WORKERS, n_chunks * TOP_K, SC_ROWS)
    rows = _sc_dispatch(h2p, idx_d)
    ys = _moe(layer, blk_meta, rows, w_gu, b_gu, w_down, b_down)
    idx_c = dest.reshape(SC_WORKERS, TOP_K * n_chunks, SC_ROWS)
    yg = _sc_collect(ys, idx_c).reshape(TOP_K, N_TOK, ROW_WORDS)
    return _combine(yg, gate_slab, x1, mod_l, g_l, split_out)


def _pad_heads(w, n_heads, width, keep):
    k = w.shape[0]
    w = w.reshape(k, n_heads, width)[:, :, :keep]
    return jnp.pad(w, ((0, 0), (0, 0), (0, LANES - keep))).reshape(k, n_heads * LANES)


def _pe_slab(x):
    return jnp.pad(x, [(0, 0)] * (x.ndim - 1) + [(MLA_NOPE, LANES - MLA_NOPE - MLA_ROPE)])


def _pair_kv_heads(w):
    g = SWA_HEADS // SWA_KV_HEADS
    return w.reshape(SWA_KV_HEADS, g, HEAD_DIM, -1).transpose(1, 0, 2, 3).reshape(w.shape)


def kernel(x_prompt, x_sample, cache_mla_ckv, cache_mla_krope, cache_diff_k, cache_diff_v, cache_swa_k, cache_swa_v, cache_na_k, cache_na_v, c, c_ctx, w_mod, b_mod, norm_g, w_in0, mla_q_norm, w_uq, mla_kv_norm, w_ukv, diff_lambda, diff_norm, w_out0, w_in1, swa_sink, na_rpb, w_out1, w_router, b_router, w_gu, b_gu, w_down, b_down):
    xs = (x_prompt.reshape(NP_TOK, D_MODEL), x_sample.reshape(NS_TOK, D_MODEL))
    cond = jnp.concatenate([c_ctx[None, :], c, jnp.zeros((16 - 1 - DEC_BATCH, D_MODEL), F32)], axis=0)
    mod = _modulation(cond, w_mod, b_mod).reshape(DEPTH, 16, 6, D_MODEL)
    t64, t32 = _rope_tables()
    states = {}
    for l in range(DEPTH):
        i = l // 2
        g_l = norm_g[l]
        mod_l = mod[l]
        if l % 2 == 0:
            lam_init = 0.8 - 0.6 * math.exp(-0.3 * l)
            wi = w_in0[i]
            w_in_p = jnp.concatenate(
                [wi[:, 0:640], wi[:, 672:2208], _pe_slab(wi[:, 640:672])], axis=1).astype(BF16)
            w_uq_p = _pad_heads(w_uq[i], MLA_HEADS, MLA_NOPE + MLA_ROPE, MLA_NOPE + MLA_ROPE).astype(BF16)
            w_k_p = _pad_heads(w_ukv[i], MLA_HEADS, MLA_NOPE + MLA_V, MLA_NOPE).astype(BF16)
            w_v = w_ukv[i].reshape(MLA_KV_RANK, MLA_HEADS, MLA_NOPE + MLA_V)[:, :, MLA_NOPE:].reshape(
                MLA_KV_RANK, MLA_HEADS * MLA_V).astype(BF16)
            (q, k, v, dq, dk, dv, ckv_st, kpe_st, dk_st, dv_st) = _premix0(
                *xs, mod_l, g_l[0:1], w_in_p, mla_q_norm[i][None, :], w_uq_p, mla_kv_norm[i][None, :],
                w_k_p, w_v, t32, t64)
            states['mla_ckv'] = ckv_st.reshape(BATCH, 1, SEQ, MLA_KV_RANK)
            states['mla_krope'] = kpe_st[:, MLA_NOPE:MLA_NOPE + MLA_ROPE].reshape(BATCH, 1, SEQ, MLA_ROPE)
            states['diff_k'] = dk_st.reshape(BATCH, 1, SEQ, DIFF_HEADS, 2 * DIFF_DH)
            states['diff_v'] = dv_st.reshape(BATCH, 1, SEQ, DIFF_HEADS, 2 * DIFF_DH)
            kc, vc = _mla_cache(cache_mla_ckv[:, i].reshape(DEC_BATCH * PAST_LEN, MLA_KV_RANK),
                                _pe_slab(cache_mla_krope[:, i].reshape(DEC_BATCH * PAST_LEN, MLA_ROPE)),
                                w_k_p, w_v)
            dkc = cache_diff_k[:, i].reshape(DEC_BATCH * PAST_LEN, 512).astype(BF16)
            dvc = cache_diff_v[:, i].reshape(DEC_BATCH * PAST_LEN, 512).astype(BF16)
            lam = diff_lambda[i]
            sub_g = diff_norm[i][None, :]
            o_p = _attn_ab(q, dq, (k, v, dk, dv), None, lam, sub_g, lam_init,
                           n_batch=BATCH, t_len=SEQ, tq=SEQ, tok_off=0, n_seq=PROMPT_SEQS_PER_STEP)
            o_cat = _attn_ab(q, dq, (k, v, dk, dv), (kc, vc, dkc, dvc), lam, sub_g, lam_init,
                             n_batch=DEC_BATCH, t_len=DEC_SEQ, tq=TQ_AB, tok_off=NP_TOK, out_init=o_p)
            w_out = w_out0[i]
        else:
            wi = w_in1[i]
            n_sq = SWA_HEADS * HEAD_DIM
            w_in_p = jnp.concatenate([_pair_kv_heads(wi[:, :n_sq].T).T, wi[:, n_sq:]], axis=1).astype(BF16)
            (sq, sk, sv, nq, nk, nv, sk_st, sv_st, nk_st, nv_st) = _premix1(xs[0], mod_l, g_l[0:1], w_in_p, t64)
            states['swa_k'] = sk_st.reshape(BATCH, 1, SEQ, SWA_KV_HEADS, HEAD_DIM)
            states['swa_v'] = sv_st.reshape(BATCH, 1, SEQ, SWA_KV_HEADS, HEAD_DIM)
            states['na_k'] = nk_st.reshape(BATCH, 1, SEQ, NA_HEADS, HEAD_DIM)
            states['na_v'] = nv_st.reshape(BATCH, 1, SEQ, NA_HEADS, HEAD_DIM)
            skc = cache_swa_k[:, i].reshape(DEC_BATCH * PAST_LEN, 128).astype(BF16)
            svc = cache_swa_v[:, i].reshape(DEC_BATCH * PAST_LEN, 128).astype(BF16)
            nkc = cache_na_k[:, i].reshape(DEC_BATCH * PAST_LEN, 512).astype(BF16)
            nvc = cache_na_v[:, i].reshape(DEC_BATCH * PAST_LEN, 512).astype(BF16)
            sink = swa_sink[i]
            o_p = _attn_cd_prompt(sink, sq, sk, sv, nq, nk, nv)
            o_cat = _attn_cd_sample(o_p, sink, sq, sk, sv, nq, nk, nv, skc, svc, nkc, nvc, _na_bias(na_rpb[i]))
            wo = w_out1[i]
            w_out = jnp.concatenate([_pair_kv_heads(wo[:n_sq]), wo[n_sq:]], axis=0)
        xs = _ffn(l, o_cat, xs, mod_l, g_l, w_out, w_router[l], b_router[l], w_gu, b_gu, w_down, b_down,
                  split_out=(l == DEPTH - 1))
    return (xs[0].reshape(BATCH, SEQ, D_MODEL), xs[1].reshape(DEC_BATCH, DEC_SEQ, D_MODEL),
            states['mla_ckv'], states['mla_krope'], states['diff_k'], states['diff_v'],
            states['swa_k'], states['swa_v'], states['na_k'], states['na_v'])
```

```python
import functools
import math

import numpy as np
import jax
import jax.numpy as jnp
from jax import lax
from jax.experimental import pallas as pl
from jax.experimental.pallas import tpu as pltpu
from jax.experimental.pallas import tpu_sc as plsc

F32 = jnp.float32
BF16 = jnp.bfloat16

D_MODEL = 1024
BATCH = 16
SEQ = 256
DEPTH = 2
DEC_BATCH = 8
DEC_SEQ = 2048
PAST_LEN = 256
GRID_W = 64
HEAD_DIM = 64
ROPE_THETA = 10000.0
EPS = 1e-6
NEG = -1e30

MLA_HEADS = 8
MLA_Q_RANK = 384
MLA_KV_RANK = 256
MLA_NOPE = 64
MLA_ROPE = 32
MLA_V = 64
DIFF_HEADS = 4
DIFF_DH = 64
SWA_HEADS = 8
SWA_KV_HEADS = 2
SWA_WINDOW = 128
NA_HEADS = 8
NA_WIN_ROWS = 8
NA_WIN_COLS = 16
N_EXPERTS = 32
TOP_K = 4
D_EXPERT = 1024
SWIGLU_LIMIT = 7.0
SWIGLU_ALPHA = 1.702

LANES = 128
NP_TOK = BATCH * SEQ
NS_TOK = DEC_BATCH * DEC_SEQ
N_TOK = NP_TOK + NS_TOK
TM = 512
NPT = NP_TOK // TM
TILES_PER_SAMPLE = DEC_SEQ // TM
N_TILES = N_TOK // TM
N_SUB = 2
SUB_TM = TM // N_SUB
TQ = 256
TQ_AB = 256
PROMPT_SEQS_PER_STEP = 2
MOE_TM = 1024
MOE_TAIL_PARTS = 8
MOE_ROWS = ((N_TOK * TOP_K + N_EXPERTS * (MOE_TM - 1)) // MOE_TM + 1) * MOE_TM
MOE_BLOCKS = MOE_ROWS // MOE_TM
NA_TILE_ROWS = TQ // GRID_W
NA_KEY_ROWS = 12
VMEM_LIMIT = 56 * 1024 * 1024


def _cparams(n_axes, vmem=VMEM_LIMIT):
    return pltpu.CompilerParams(dimension_semantics=("arbitrary",) * n_axes,
                                vmem_limit_bytes=vmem)


def _rms(x, g):
    return x * lax.rsqrt(jnp.mean(x * x, axis=-1, keepdims=True) + EPS) * g


def _dot(a, b):
    return jnp.dot(a, b, preferred_element_type=F32)


def _dot_nt(a, b):
    return lax.dot_general(a, b, (((1,), (1,)), ((), ())), preferred_element_type=F32)


def _rope(x, cos, sin_a, sin_b, half):
    return (x * cos + pltpu.roll(x, LANES - half, 1) * sin_a + pltpu.roll(x, half, 1) * sin_b)


def _pipelined_units(units):
    pending = units[0][0](units[0][2])
    for n, (_, finish, arg) in enumerate(units):
        following = units[n + 1][0](units[n + 1][2]) if n + 1 < len(units) else None
        finish(arg, pending)
        pending = following


def _sub_rows(n):
    return slice(SUB_TM * n, SUB_TM * (n + 1))


def _store_head_rows(st_ref, n, hd, n_heads, x):
    st_ref[pl.ds(SUB_TM * n * n_heads + hd, SUB_TM, stride=n_heads), :] = x


def _mod_row(i):
    return jnp.where(i < NPT, 0, 1 + (i - NPT) // TILES_PER_SAMPLE)


def _rope_blk(i):
    return jnp.where(i < NPT, TILES_PER_SAMPLE, (i - NPT) % TILES_PER_SAMPLE)


def _mod_kernel(c_ref, w_ref, b_ref, o_ref):
    c = c_ref[...]
    s = (c * jax.nn.sigmoid(c)).astype(BF16)
    o_ref[0] = _dot(s, w_ref[0].astype(BF16)) + b_ref[0]


def _modulation(cond, w_mod, b_mod):
    nb = 1024
    return pl.pallas_call(
        _mod_kernel,
        grid=(DEPTH, 6 * D_MODEL // nb),
        in_specs=[
            pl.BlockSpec((16, D_MODEL), lambda l, n: (0, 0)),
            pl.BlockSpec((1, D_MODEL, nb), lambda l, n: (l, 0, n)),
            pl.BlockSpec((1, 1, nb), lambda l, n: (l, 0, n)),
        ],
        out_specs=pl.BlockSpec((1, 16, nb), lambda l, n: (l, 0, n)),
        out_shape=jax.ShapeDtypeStruct((DEPTH, 16, 6 * D_MODEL), F32),
        compiler_params=_cparams(2),
        name="modulation",
    )(cond, w_mod, b_mod.reshape(DEPTH, 1, 6 * D_MODEL))


def _rope_tables():
    t = jnp.arange(DEC_SEQ)
    rows = (t // GRID_W).astype(F32)
    cols = (t % GRID_W).astype(F32)

    def angles(r):
        n = r // 4
        inv = ROPE_THETA ** (-jnp.arange(n, dtype=F32) / n)
        return jnp.concatenate([rows[:, None] * inv[None], cols[:, None] * inv[None]], axis=-1)

    def finish(cos, sa, sb):
        ident = (jnp.ones((TM, LANES), F32), jnp.zeros((TM, LANES), F32), jnp.zeros((TM, LANES), F32))
        return tuple(jnp.concatenate([a, b], axis=0) for a, b in zip((cos, sa, sb), ident))

    a64 = angles(64)
    c, s, z = jnp.cos(a64), jnp.sin(a64), jnp.zeros_like(a64)
    t64 = finish(jnp.concatenate([c, c, c, c], -1), jnp.concatenate([-s, z, -s, z], -1),
                 jnp.concatenate([z, s, z, s], -1))
    a32 = angles(32)
    c, s, z = jnp.cos(a32), jnp.sin(a32), jnp.zeros_like(a32)
    one64 = jnp.ones((DEC_SEQ, 64), F32)
    z64 = jnp.zeros((DEC_SEQ, 64), F32)
    z32 = jnp.zeros((DEC_SEQ, 32), F32)
    t32 = finish(jnp.concatenate([one64, c, c, z32], -1), jnp.concatenate([z64, -s, z, z32], -1),
                 jnp.concatenate([z64, z, s, z32], -1))
    return t64, t32


LOG2E = math.log2(math.e)
_DIFF_COLS = DIFF_HEADS * 2 * DIFF_DH
_AB_COLS = tuple(np.cumsum([0, MLA_Q_RANK, MLA_KV_RANK, _DIFF_COLS, _DIFF_COLS, _DIFF_COLS, LANES]).tolist())
_CD_COLS = tuple(np.cumsum([0, SWA_HEADS * HEAD_DIM, SWA_KV_HEADS * HEAD_DIM, SWA_KV_HEADS * HEAD_DIM,
                            NA_HEADS * HEAD_DIM, NA_HEADS * HEAD_DIM, NA_HEADS * HEAD_DIM]).tolist())
IN_COLS = _AB_COLS[-1]
assert IN_COLS == _CD_COLS[-1]
_MLA_SCALE = (MLA_NOPE + MLA_ROPE) ** -0.5 * LOG2E
_QSCALE = HEAD_DIM ** -0.5 * LOG2E


def _premix0_kernel(xp_ref, xs_ref, mod_ref, g_ref, win_ref, qn_ref, wuq_ref, kvn_ref, wk_ref, wv_ref,
                    c32_ref, sa32_ref, sb32_ref, c64_ref, sa64_ref, sb64_ref,
                    q_ref, k_ref, v_ref, dq_ref, dk_ref, dv_ref,
                    ckv_st, kpe_st, dk_st, dv_st):
    i = pl.program_id(0)
    m = mod_ref[0]
    gain = g_ref[...] * (1.0 + m[1:2])

    def project(n):
        r = _sub_rows(n)
        x = jnp.where(i < NPT, xp_ref[r, :], xs_ref[r, :])
        h = _rms(x, gain) + m[0:1]
        return _dot(h.astype(BF16), win_ref[...])

    states = {}

    def finish(n, proj):
        r = _sub_rows(n)
        q_a, kv_a, dq, dk, dv, pe = (proj[:, a:b] for a, b in zip(_AB_COLS[:-1], _AB_COLS[1:]))
        q = _dot(_rms(q_a, qn_ref[...]).astype(BF16), wuq_ref[...])
        ckv = _rms(kv_a, kvn_ref[...])
        ckv_b = ckv.astype(BF16)
        kn = _dot(ckv_b, wk_ref[...])
        v_ref[r, :] = _dot(ckv_b, wv_ref[...]).astype(BF16)
        c32, sa32, sb32 = c32_ref[r, :], sa32_ref[r, :], sb32_ref[r, :]
        c64, sa64, sb64 = c64_ref[r, :], sa64_ref[r, :], sb64_ref[r, :]
        pe_r = _rope(pe, c32, sa32, sb32, MLA_ROPE // 2)
        for hd in range(MLA_HEADS):
            sl = slice(LANES * hd, LANES * (hd + 1))
            q_ref[r, sl] = (_rope(q[:, sl], c32, sa32, sb32, MLA_ROPE // 2) * _MLA_SCALE).astype(BF16)
            k_ref[r, sl] = (kn[:, sl] + pe_r).astype(BF16)
        for hd in range(DIFF_HEADS):
            sl = slice(LANES * hd, LANES * (hd + 1))
            dq_ref[r, sl] = (_rope(dq[:, sl], c64, sa64, sb64, DIFF_DH // 2) * _QSCALE).astype(BF16)
            dk_ref[r, sl] = _rope(dk[:, sl], c64, sa64, sb64, DIFF_DH // 2).astype(BF16)
        dv_ref[r, :] = dv.astype(BF16)
        states[n] = (ckv, pe, dk, dv)

    _pipelined_units([(project, finish, n) for n in range(N_SUB)])

    @pl.when(i < NPT)
    def _():
        for n in range(N_SUB):
            r = _sub_rows(n)
            ckv, pe, dk, dv = states[n]
            ckv_st[r, :] = ckv
            kpe_st[r, :] = pe
            for hd in range(DIFF_HEADS):
                _store_head_rows(dk_st, n, hd, DIFF_HEADS, dk[:, LANES * hd: LANES * (hd + 1)])
                _store_head_rows(dv_st, n, hd, DIFF_HEADS, dv[:, LANES * hd: LANES * (hd + 1)])


def _premix1_kernel(x_ref, mod_ref, g_ref, win_ref, c64_ref, sa64_ref, sb64_ref,
                    sq_ref, sk_ref, sv_ref, nq_ref, nk_ref, nv_ref,
                    sk_st, sv_st, nk_st, nv_st):
    i = pl.program_id(0)
    m = mod_ref[0]
    gain = g_ref[...] * (1.0 + m[1:2])

    def project(n):
        h = _rms(x_ref[_sub_rows(n), :], gain) + m[0:1]
        return _dot(h.astype(BF16), win_ref[...])

    states = {}

    def finish(n, proj):
        r = _sub_rows(n)
        sq, sk, sv, nq, nk, nv = (proj[:, a:b] for a, b in zip(_CD_COLS[:-1], _CD_COLS[1:]))
        c64, sa64, sb64 = c64_ref[r, :], sa64_ref[r, :], sb64_ref[r, :]
        for hd in range(4):
            sl = slice(LANES * hd, LANES * (hd + 1))
            sq_ref[r, sl] = (_rope(sq[:, sl], c64, sa64, sb64, HEAD_DIM // 2) * _QSCALE).astype(BF16)
        sk_ref[r, :] = _rope(sk, c64, sa64, sb64, HEAD_DIM // 2).astype(BF16)
        sv_ref[r, :] = sv.astype(BF16)
        nq_ref[r, :] = (nq * _QSCALE).astype(BF16)
        nk_ref[r, :] = nk.astype(BF16)
        nv_ref[r, :] = nv.astype(BF16)
        states[n] = (sk, sv, nk, nv)

    _pipelined_units([(project, finish, n) for n in range(N_SUB)])

    @pl.when(i < NPT)
    def _():
        for n in range(N_SUB):
            r = _sub_rows(n)
            sk, sv, nk, nv = states[n]
            sk_st[r, :] = sk
            sv_st[r, :] = sv
            for hd in range(NA_HEADS):
                _store_head_rows(nk_st, n, hd, NA_HEADS, nk[:, HEAD_DIM * hd: HEAD_DIM * (hd + 1)])
                _store_head_rows(nv_st, n, hd, NA_HEADS, nv[:, HEAD_DIM * hd: HEAD_DIM * (hd + 1)])


def _tok_spec(width):
    return pl.BlockSpec((TM, width), lambda i: (i, 0))


_PROMPT_SPEC = pl.BlockSpec((TM, D_MODEL), lambda i: (jnp.minimum(i, NPT - 1), 0))
_SAMPLE_SPEC = pl.BlockSpec((TM, D_MODEL), lambda i: (jnp.maximum(i - NPT, 0), 0))


def _state_spec(width, rows_per_token=1):
    return pl.BlockSpec((TM * rows_per_token, width), lambda i: (jnp.minimum(i, NPT - 1), 0))


def _const_spec(shape):
    return pl.BlockSpec(shape, lambda i: (0,) * len(shape))


_MOD_SPEC = pl.BlockSpec((1, 6, D_MODEL), lambda i: (_mod_row(i), 0, 0))
_ROPE_SPEC = pl.BlockSpec((TM, LANES), lambda i: (_rope_blk(i), 0))


def _premix0(xp, xs, mod_l, g0, w_in_p, q_norm, w_uq_p, kv_norm, w_k_p, w_v, t32, t64):
    outs = [(N_TOK, 1024, BF16), (N_TOK, 1024, BF16), (N_TOK, 512, BF16), (N_TOK, 512, BF16),
            (N_TOK, 512, BF16), (N_TOK, 512, BF16),
            (NP_TOK, 256, F32), (NP_TOK, 128, F32),
            (NP_TOK * DIFF_HEADS, 2 * DIFF_DH, F32), (NP_TOK * DIFF_HEADS, 2 * DIFF_DH, F32)]
    return pl.pallas_call(
        _premix0_kernel,
        grid=(N_TILES,),
        in_specs=[_PROMPT_SPEC, _SAMPLE_SPEC, _MOD_SPEC, _const_spec((1, D_MODEL)),
                  _const_spec((D_MODEL, IN_COLS)), _const_spec((1, MLA_Q_RANK)),
                  _const_spec((MLA_Q_RANK, 1024)), _const_spec((1, MLA_KV_RANK)),
                  _const_spec((MLA_KV_RANK, 1024)), _const_spec((MLA_KV_RANK, 512))]
                 + [_ROPE_SPEC] * 6,
        out_specs=([_tok_spec(w) for (_, w, _) in outs[:6]]
                   + [_state_spec(w, n // NP_TOK) for (n, w, _) in outs[6:]]),
        out_shape=[jax.ShapeDtypeStruct((n, w), dt) for (n, w, dt) in outs],
        compiler_params=_cparams(1),
        name="premix_ab",
    )(xp, xs, mod_l, g0, w_in_p, q_norm, w_uq_p, kv_norm, w_k_p, w_v, *t32, *t64)


def _premix1(x, mod_l, g0, w_in_p, t64):
    outs = [(N_TOK, 512, BF16), (N_TOK, 128, BF16), (N_TOK, 128, BF16), (N_TOK, 512, BF16),
            (N_TOK, 512, BF16), (N_TOK, 512, BF16),
            (NP_TOK, 128, F32), (NP_TOK, 128, F32),
            (NP_TOK * NA_HEADS, HEAD_DIM, F32), (NP_TOK * NA_HEADS, HEAD_DIM, F32)]
    return pl.pallas_call(
        _premix1_kernel,
        grid=(N_TILES,),
        in_specs=[_tok_spec(D_MODEL), _MOD_SPEC, _const_spec((1, D_MODEL)),
                  _const_spec((D_MODEL, IN_COLS))] + [_ROPE_SPEC] * 3,
        out_specs=([_tok_spec(w) for (_, w, _) in outs[:6]]
                   + [_state_spec(w, n // NP_TOK) for (n, w, _) in outs[6:]]),
        out_shape=[jax.ShapeDtypeStruct((n, w), dt) for (n, w, dt) in outs],
        compiler_params=_cparams(1),
        name="premix_cd",
    )(x, mod_l, g0, w_in_p, *t64)


def _mla_cache_kernel(ckv_ref, pe_ref, wk_ref, wv_ref, k_ref, v_ref):
    c = ckv_ref[...].astype(BF16)
    kn = _dot(c, wk_ref[...])
    v_ref[...] = _dot(c, wv_ref[...]).astype(BF16)
    pe = pe_ref[...]
    for hd in range(MLA_HEADS):
        sl = slice(LANES * hd, LANES * (hd + 1))
        k_ref[:, sl] = (kn[:, sl] + pe).astype(BF16)


def _mla_cache(ckv, pe_slab, w_k_p, w_v):
    n = ckv.shape[0]
    tm = 512
    return pl.pallas_call(
        _mla_cache_kernel,
        grid=(n // tm,),
        in_specs=[pl.BlockSpec((tm, MLA_KV_RANK), lambda i: (i, 0)),
                  pl.BlockSpec((tm, LANES), lambda i: (i, 0)),
                  _const_spec((MLA_KV_RANK, 1024)), _const_spec((MLA_KV_RANK, 512))],
        out_specs=[pl.BlockSpec((tm, 1024), lambda i: (i, 0)), pl.BlockSpec((tm, 512), lambda i: (i, 0))],
        out_shape=[jax.ShapeDtypeStruct((n, 1024), BF16), jax.ShapeDtypeStruct((n, 512), BF16)],
        compiler_params=_cparams(1),
        name="mla_cache",
    )(ckv, pe_slab, w_k_p, w_v)


def _softmax_pv(scores, values, sink=None):
    m = jnp.max(scores[0], axis=-1, keepdims=True)
    for s in scores[1:]:
        m = jnp.maximum(m, jnp.max(s, axis=-1, keepdims=True))
    if sink is not None:
        m = jnp.maximum(m, sink)
    l = None
    o = None
    for s, v in zip(scores, values):
        p = jnp.exp2(s - m)
        ls = jnp.sum(p, axis=-1, keepdims=True)
        os_ = _dot(p.astype(BF16), v)
        l = ls if l is None else l + ls
        o = os_ if o is None else o + os_
    if sink is not None:
        l = l + jnp.exp2(sink - m)
    return o * (1.0 / l)


def _lane_lo(shape):
    return lax.broadcasted_iota(jnp.int32, shape, 1) < (LANES // 2)


def _split_halves(qb):
    lo = _lane_lo(qb.shape)
    zero = jnp.zeros_like(qb)
    return jnp.where(lo, qb, zero), jnp.where(lo, zero, qb)


def _attn_ab_kernel(*refs, n_pieces, n_seq, lam_init, aliased):
    if aliased:
        refs = refs[1:]
    q_ref, dq_ref = refs[0], refs[1]
    pieces = [refs[2 + 4 * p: 6 + 4 * p] for p in range(n_pieces)]
    lam_ref, subg_ref, o_ref = refs[2 + 4 * n_pieces:]
    lam = lam_ref[...]
    lam_full = (jnp.exp(jnp.sum(lam[0:1] * lam[1:2], axis=-1, keepdims=True))
                - jnp.exp(jnp.sum(lam[2:3] * lam[3:4], axis=-1, keepdims=True)) + lam_init)
    tq = q_ref.shape[0] // n_seq
    lo = _lane_lo((tq, LANES))
    subg = subg_ref[...] * (1.0 - lam_init)

    def seq_rows(ref, sq):
        n = ref.shape[0] // n_seq
        return slice(n * sq, n * (sq + 1))

    def mla_scores(arg):
        sq, hd = arg
        sl = slice(LANES * hd, LANES * (hd + 1))
        qh = q_ref[seq_rows(q_ref, sq), sl]
        return [_dot_nt(qh, k_ref[seq_rows(k_ref, sq), sl]) for (k_ref, _, _, _) in pieces]

    def diff_scores(arg):
        sq, hd = arg
        sl = slice(LANES * hd, LANES * (hd + 1))
        qq = jnp.concatenate(_split_halves(dq_ref[seq_rows(dq_ref, sq), sl]), axis=0)
        return [_dot_nt(qq, dk_ref[seq_rows(dk_ref, sq), sl]) for (_, _, dk_ref, _) in pieces]

    pair = {}

    def mla_finish(arg, scores):
        sq, hd = arg
        j = hd // 2
        vals = [v_ref[seq_rows(v_ref, sq), LANES * j: LANES * (j + 1)] for (_, v_ref, _, _) in pieces]
        pair[hd % 2] = _softmax_pv(scores, vals)
        if hd % 2 == 1:
            o_ref[seq_rows(o_ref, sq), LANES * j: LANES * (j + 1)] = jnp.where(lo, pair[0], pair[1]).astype(BF16)

    def diff_finish(arg, scores):
        sq, hd = arg
        sl = slice(LANES * hd, LANES * (hd + 1))
        oo = _softmax_pv(scores, [dv_ref[seq_rows(dv_ref, sq), sl] for (_, _, _, dv_ref) in pieces])
        od = _rms(oo[:tq] - lam_full * oo[tq:], subg)
        o_ref[seq_rows(o_ref, sq), 512 + LANES * hd: 512 + LANES * (hd + 1)] = od.astype(BF16)

    units = []
    for sq in range(n_seq):
        for j in range(DIFF_HEADS):
            units += [(diff_scores, diff_finish, (sq, j)), (mla_scores, mla_finish, (sq, 2 * j)),
                      (mla_scores, mla_finish, (sq, 2 * j + 1))]
    _pipelined_units(units)


def _attn_ab(q, dq, new_kv, cache_kv, lam, sub_g, lam_init, *, n_batch, t_len, tq, tok_off, out_init=None,
             n_seq=1):
    assert n_seq == 1 or (tq == t_len and cache_kv is None and n_batch % n_seq == 0)
    n_batch, t_len, tq = n_batch // n_seq, t_len * n_seq, tq * n_seq
    nq = t_len // tq
    q_off = tok_off // tq
    b_off = tok_off // t_len
    widths = (1024, 512, 512, 512)
    in_specs = [pl.BlockSpec((tq, 1024), lambda b, i: (q_off + b * nq + i, 0)),
                pl.BlockSpec((tq, 512), lambda b, i: (q_off + b * nq + i, 0))]
    args = [q, dq]
    for w, a in zip(widths, new_kv):
        in_specs.append(pl.BlockSpec((t_len, w), lambda b, i: (b_off + b, 0)))
        args.append(a)
    n_pieces = 1
    if cache_kv is not None:
        n_pieces = 2
        for w, a in zip(widths, cache_kv):
            in_specs.append(pl.BlockSpec((PAST_LEN, w), lambda b, i: (b, 0)))
            args.append(a)
    in_specs += [pl.BlockSpec((4, DIFF_DH), lambda b, i: (0, 0)),
                 pl.BlockSpec((1, 2 * DIFF_DH), lambda b, i: (0, 0))]
    args += [lam, sub_g]
    aliases = {}
    if out_init is not None:
        in_specs = [pl.BlockSpec(memory_space=pl.ANY)] + in_specs
        args = [out_init] + args
        aliases = {0: 0}
    return pl.pallas_call(
        functools.partial(_attn_ab_kernel, n_pieces=n_pieces, n_seq=n_seq, lam_init=lam_init,
                          aliased=out_init is not None),
        grid=(n_batch, nq),
        in_specs=in_specs,
        out_specs=pl.BlockSpec((tq, 1024), lambda b, i: (q_off + b * nq + i, 0)),
        out_shape=jax.ShapeDtypeStruct((N_TOK, 1024), BF16),
        input_output_aliases=aliases,
        compiler_params=_cparams(2),
        name="attn_ab_%d" % n_pieces,
    )(*args)


def _gqa_stacks(sq_ref, sink_ref):
    tq = sq_ref.shape[0]
    halves = [_split_halves(sq_ref[:, LANES * j: LANES * (j + 1)]) for j in range(4)]
    q_stacks = [jnp.concatenate([halves[j][kvh] for j in range(4)], axis=0) for kvh in range(SWA_KV_HEADS)]
    sinks = [jnp.concatenate([jnp.full((tq, 1), sink_ref[4 * kvh + j] * LOG2E, F32) for j in range(4)], axis=0)
             for kvh in range(SWA_KV_HEADS)]
    return q_stacks, sinks


def _attn_cd_prompt_kernel(sink_ref, sq_ref, sk_ref, sv_ref, nq_ref, nk_ref, nv_ref, o_ref):
    lo = _lane_lo((SEQ, LANES))
    for sq in range(PROMPT_SEQS_PER_STEP):
        r = slice(SEQ * sq, SEQ * (sq + 1))
        sk = sk_ref[r, :]
        sv = sv_ref[r, :]
        for j in range(4):
            sl = slice(LANES * j, LANES * (j + 1))
            q_lo, q_hi = _split_halves(sq_ref[r, sl])
            o_lo = _softmax_pv([_dot_nt(q_lo, sk)], [sv], sink=sink_ref[j] * LOG2E)
            o_hi = _softmax_pv([_dot_nt(q_hi, sk)], [sv], sink=sink_ref[j + 4] * LOG2E)
            o_ref[r, sl] = jnp.where(lo, o_lo, o_hi).astype(BF16)
        for j in range(4):
            sl = slice(LANES * j, LANES * (j + 1))
            q_lo, q_hi = _split_halves(nq_ref[r, sl])
            k = nk_ref[r, sl]
            v = nv_ref[r, sl]
            o_lo = _softmax_pv([_dot_nt(q_lo, k)], [v])
            o_hi = _softmax_pv([_dot_nt(q_hi, k)], [v])
            o_ref[r, 512 + LANES * j: 512 + LANES * (j + 1)] = jnp.where(lo, o_lo, o_hi).astype(BF16)


def _attn_cd_prompt(sink, sq, sk, sv, nq, nk, nv):
    def spec(w):
        return pl.BlockSpec((SEQ * PROMPT_SEQS_PER_STEP, w), lambda b: (b, 0))
    return pl.pallas_call(
        _attn_cd_prompt_kernel,
        grid=(BATCH // PROMPT_SEQS_PER_STEP,),
        in_specs=[pl.BlockSpec(memory_space=pltpu.SMEM), spec(512), spec(128), spec(128),
                  spec(512), spec(512), spec(512)],
        out_specs=spec(1024),
        out_shape=jax.ShapeDtypeStruct((N_TOK, 1024), BF16),
        compiler_params=_cparams(1),
        name="attn_cd_prompt",
    )(sink, sq, sk, sv, nq, nk, nv)


_SWA_KEYS = TQ + 2 * SWA_WINDOW


def _attn_cd_sample_kernel(init_ref, sink_ref, sq_ref, nq_ref, sk_ref, sv_ref, nk_ref, nv_ref,
                           skc_ref, svc_ref, nkc_ref, nvc_ref, bias_ref, o_ref):
    del init_ref
    qi = pl.program_id(1)
    lo = _lane_lo((TQ, LANES))
    ks = pl.multiple_of(jnp.clip(qi * TQ - SWA_WINDOW, 0, DEC_SEQ - _SWA_KEYS), SWA_WINDOW)
    k_win = sk_ref[pl.ds(ks, _SWA_KEYS), :]
    v_win = sv_ref[pl.ds(ks, _SWA_KEYS), :]
    q_pos = qi * TQ + (lax.broadcasted_iota(jnp.int32, (4 * TQ, _SWA_KEYS), 0) & (TQ - 1))
    k_pos = ks + lax.broadcasted_iota(jnp.int32, (4 * TQ, _SWA_KEYS), 1)
    in_win = jnp.abs(q_pos - k_pos) <= SWA_WINDOW
    skc = skc_ref[...]
    svc = svc_ref[...]
    q_stacks, sinks = _gqa_stacks(sq_ref, sink_ref)

    def window_scores(kvh):
        return [_dot_nt(q_stacks[kvh], skc), jnp.where(in_win, _dot_nt(q_stacks[kvh], k_win), NEG)]

    o_kv = {}

    def window_finish(kvh, scores):
        o_kv[kvh] = _softmax_pv(scores, [svc, v_win], sink=sinks[kvh])
        if kvh == SWA_KV_HEADS - 1:
            for j in range(4):
                rows = slice(TQ * j, TQ * (j + 1))
                o_ref[:, LANES * j: LANES * (j + 1)] = jnp.where(lo, o_kv[0][rows], o_kv[1][rows]).astype(BF16)

    n_rows = DEC_SEQ // GRID_W
    r0 = jnp.clip(qi * NA_TILE_ROWS - NA_WIN_ROWS // 2, 0, n_rows - NA_KEY_ROWS)
    kn = pl.multiple_of(r0 * GRID_W, GRID_W)

    lane_lo = _lane_lo((1, LANES))
    pieces = []
    for ri in range(NA_TILE_ROWS):
        r = qi * NA_TILE_ROWS + ri
        rs = jnp.clip(r - NA_WIN_ROWS // 2, 0, n_rows - NA_WIN_ROWS)
        row = []
        for mm in range(NA_KEY_ROWS // 2):
            kr = r0 + 2 * mm
            ok = [(kr + t >= rs) & (kr + t < rs + NA_WIN_ROWS) for t in range(2)]
            mask = jnp.where(lane_lo, jnp.where(ok[0], 0.0, NEG), jnp.where(ok[1], 0.0, NEG))
            row.append((jnp.clip(kr - r + NA_WIN_ROWS, 0, 2 * NA_WIN_ROWS - 1), mask))
        pieces.append(row)

    def add_na_bias(hd, s):
        return jnp.concatenate(
            [jnp.concatenate([s[GRID_W * ri: GRID_W * (ri + 1), LANES * mm: LANES * (mm + 1)] + bias_ref[hd, d] + mask
                              for mm, (d, mask) in enumerate(row)], axis=1)
             for ri, row in enumerate(pieces)], axis=0)

    def na_scores(hd):
        sl = slice(LANES * (hd // 2), LANES * (hd // 2 + 1))
        q_half = _split_halves(nq_ref[:, sl])[hd % 2]
        nk_win = nk_ref[pl.ds(kn, NA_KEY_ROWS * GRID_W), sl]
        return [_dot_nt(q_half, nkc_ref[:, sl]), add_na_bias(hd, _dot_nt(q_half, nk_win))]

    pair = {}

    def na_finish(hd, scores):
        j = hd // 2
        sl = slice(LANES * j, LANES * (j + 1))
        nv_win = nv_ref[pl.ds(kn, NA_KEY_ROWS * GRID_W), sl]
        pair[hd % 2] = _softmax_pv(scores, [nvc_ref[:, sl], nv_win])
        if hd % 2 == 1:
            o_ref[:, 512 + LANES * j: 512 + LANES * (j + 1)] = jnp.where(lo, pair[0], pair[1]).astype(BF16)

    _pipelined_units([(window_scores, window_finish, kvh) for kvh in range(SWA_KV_HEADS)]
                     + [(na_scores, na_finish, hd) for hd in range(NA_HEADS)])


def _attn_cd_sample(out_init, sink, sq, sk, sv, nq, nk, nv, skc, svc, nkc, nvc, bias):
    nq_t = DEC_SEQ // TQ
    q_off = NP_TOK // TQ
    b_off = NP_TOK // DEC_SEQ

    def qspec(w):
        return pl.BlockSpec((TQ, w), lambda b, i: (q_off + b * nq_t + i, 0))

    def kspec(w):
        return pl.BlockSpec((DEC_SEQ, w), lambda b, i: (b_off + b, 0))

    def cspec(w):
        return pl.BlockSpec((PAST_LEN, w), lambda b, i: (b, 0))

    bias_spec = pl.BlockSpec((NA_HEADS, 2 * NA_WIN_ROWS, GRID_W, LANES), lambda b, i: (0, 0, 0, 0))
    return pl.pallas_call(
        _attn_cd_sample_kernel,
        grid=(DEC_BATCH, nq_t),
        in_specs=[pl.BlockSpec(memory_space=pl.ANY), pl.BlockSpec(memory_space=pltpu.SMEM),
                  qspec(512), qspec(512), kspec(128), kspec(128), kspec(512), kspec(512),
                  cspec(128), cspec(128), cspec(512), cspec(512), bias_spec],
        out_specs=pl.BlockSpec((TQ, 1024), lambda b, i: (q_off + b * nq_t + i, 0)),
        out_shape=jax.ShapeDtypeStruct((N_TOK, 1024), BF16),
        input_output_aliases={0: 0},
        compiler_params=_cparams(2),
        name="attn_cd_sample",
    )(out_init, sink, sq, nq, sk, sv, nk, nv, skc, svc, nkc, nvc, bias)


def _na_bias(rpb):
    n_dc = 2 * NA_WIN_COLS - 1
    c = np.arange(GRID_W)[:, None]
    kc = np.arange(GRID_W)[None, :]
    qs = np.clip(c - NA_WIN_COLS // 2, 0, GRID_W - NA_WIN_COLS)
    col_ok = (kc >= qs) & (kc < qs + NA_WIN_COLS)
    dc = np.clip(kc - c + NA_WIN_COLS - 1, 0, n_dc - 1)
    onehot = ((dc[None] == np.arange(n_dc)[:, None, None]) & col_ok[None]).astype(np.float32)
    blocks = jnp.einsum('hrd,dck->hrck', rpb.astype(F32) * LOG2E, onehot, precision=lax.Precision.HIGHEST)
    blocks = jnp.where(col_ok[None, None], blocks, NEG)
    none = jnp.full((NA_HEADS, 1, GRID_W, GRID_W), NEG, F32)
    return jnp.concatenate([jnp.concatenate([none, blocks], axis=1),
                            jnp.concatenate([blocks, none], axis=1)], axis=-1)


_HI_MASK = -65536


def _pack_pairs(x):
    w = x.shape[1] // 2
    r = x.astype(BF16).astype(F32)
    lo = lax.bitcast_convert_type(r[:, :w], jnp.int32)
    hi = lax.bitcast_convert_type(r[:, w:], jnp.int32)
    return (hi & _HI_MASK) | lax.shift_right_logical(lo, 16)


def _unpack_pairs(p):
    lo = lax.bitcast_convert_type(lax.shift_left(p, 16), F32)
    hi = lax.bitcast_convert_type(p & _HI_MASK, F32)
    return lo, hi


def _postmix_kernel(*refs, split_x):
    if split_x:
        o_ref, xp_ref, xs_ref = refs[:3]
        refs = refs[3:]
    else:
        o_ref, x_ref = refs[:2]
        refs = refs[2:]
    (mod_ref, g_ref, wout_ref, wr_ref, br_ref, tri_ref,
     x1_ref, h2_ref, route_ref, gate_ref, cnt_ref, run_ref) = refs
    i = pl.program_id(0)

    @pl.when(i == 0)
    def _():
        run_ref[...] = jnp.zeros_like(run_ref)

    m = mod_ref[0]
    g = g_ref[...]
    gate_gain = g[1:2] * m[2:3]
    ffn_gain = g[2:3] * (1.0 + m[4:5])
    sub_logits = {}

    def project(n):
        return _dot(o_ref[_sub_rows(n), :], wout_ref[...])

    def finish(n, y):
        r = _sub_rows(n)
        x = jnp.where(i < NPT, xp_ref[r, :], xs_ref[r, :]) if split_x else x_ref[r, :]
        x1 = x + _rms(y, gate_gain)
        x1_ref[r, :] = x1
        h2 = _rms(x1, ffn_gain) + m[3:4]
        h2_ref[r, :] = _pack_pairs(h2)
        sub_logits[n] = _dot(h2.astype(BF16), wr_ref[...]) + br_ref[...]

    _pipelined_units([(project, finish, n) for n in range(N_SUB)])

    logits = jnp.concatenate([sub_logits[n] for n in range(N_SUB)], axis=0)
    lane = lax.broadcasted_iota(jnp.int32, logits.shape, 1).astype(F32)
    cur = jnp.where(lane < N_EXPERTS, logits, -jnp.inf)
    tops, idxs = [], []
    for _ in range(TOP_K):
        mx = jnp.max(cur, axis=-1, keepdims=True)
        ix = jnp.min(jnp.where(cur == mx, lane, float(LANES)), axis=-1, keepdims=True)
        tops.append(mx)
        idxs.append(ix)
        cur = jnp.where(lane == ix, -jnp.inf, cur)
    es = [jnp.exp(t - tops[0]) for t in tops]
    inv = 1.0 / (es[0] + es[1] + es[2] + es[3])
    picked = jnp.zeros_like(logits)
    for k in range(TOP_K):
        picked = jnp.where(lane == idxs[k], 1.0, picked)
    before = _dot(tri_ref[...], picked.astype(BF16)) + run_ref[0:1, :]
    route = jnp.zeros_like(logits)
    gate_out = jnp.zeros_like(logits)
    for k in range(TOP_K):
        rank = jnp.sum(jnp.where(lane == idxs[k], before, 0.0), axis=-1, keepdims=True)
        route = jnp.where(lane == float(k), idxs[k], route)
        route = jnp.where(lane == float(TOP_K + k), rank, route)
        gate_out = jnp.where(lane == float(k), es[k] * inv, gate_out)
    route_ref[...] = route.T[:2 * TOP_K].astype(jnp.int32)
    gate_ref[...] = gate_out
    run_ref[...] = run_ref[...] + jnp.sum(picked, axis=0, keepdims=True)
    cnt_ref[...] = run_ref[...].astype(jnp.int32)


def _postmix(o_cat, xs, mod_l, g_l, w_out, w_r, b_r):
    tri = jnp.asarray(np.tril(np.ones((TM, TM), np.float32), -1), BF16)
    split_x = len(xs) == 2
    x_specs = [_PROMPT_SPEC, _SAMPLE_SPEC] if split_x else [_tok_spec(D_MODEL)]
    return pl.pallas_call(
        functools.partial(_postmix_kernel, split_x=split_x),
        grid=(N_TILES,),
        in_specs=[_tok_spec(1024)] + x_specs + [_MOD_SPEC, _const_spec((4, D_MODEL)),
                  _const_spec((1024, D_MODEL)), _const_spec((D_MODEL, LANES)), _const_spec((1, LANES)),
                  _const_spec((TM, TM))],
        out_specs=[_tok_spec(D_MODEL), _tok_spec(D_MODEL // 2), pl.BlockSpec((2 * TOP_K, TM), lambda i: (0, i)),
                   _tok_spec(LANES), _const_spec((8, LANES))],
        out_shape=[jax.ShapeDtypeStruct((N_TOK, D_MODEL), F32),
                   jax.ShapeDtypeStruct((N_TOK, D_MODEL // 2), jnp.int32),
                   jax.ShapeDtypeStruct((2 * TOP_K, N_TOK), jnp.int32), jax.ShapeDtypeStruct((N_TOK, LANES), F32),
                   jax.ShapeDtypeStruct((8, LANES), jnp.int32)],
        scratch_shapes=[pltpu.VMEM((8, LANES), F32)],
        compiler_params=_cparams(1),
        name="postmix",
    )(o_cat, *xs, mod_l, g_l, w_out, w_r, b_r, tri)


SC_WORKERS = 32
SC_ROWS = 64
ROW_WORDS = D_MODEL // 2

_SC_SCRATCH = [pltpu.VMEM((SC_ROWS, ROW_WORDS), jnp.int32), pltpu.VMEM((SC_ROWS, ROW_WORDS), jnp.int32),
               pltpu.SemaphoreType.DMA, pltpu.SemaphoreType.DMA, pltpu.SemaphoreType.DMA, pltpu.SemaphoreType.DMA]


def _sc_worker_id():
    return lax.axis_index("s") * 2 + lax.axis_index("c")


def _sc_double_buffered(n_chunks, load, store):
    for cp in load(0, 0):
        cp.start()

    @pl.loop(0, n_chunks, step=2)
    def _(g0):
        for b in range(2):
            g = g0 + b
            for cp in load(g, b):
                cp.wait()

            @pl.when(g >= 1)
            def _():
                for cp in store(g - 1, 1 - b):
                    cp.wait()

            @pl.when(g + 1 < n_chunks)
            def _():
                for cp in load(g + 1, 1 - b):
                    cp.start()

            for cp in store(g, b):
                cp.start()

    for cp in store(n_chunks - 1, (n_chunks - 1) % 2):
        cp.wait()


def _sc_dispatch(src, idx):
    n_chunks = N_TOK // (SC_WORKERS * SC_ROWS)
    assert n_chunks % 2 == 0
    mesh = plsc.VectorSubcoreMesh(core_axis_name="c", subcore_axis_name="s")

    @functools.partial(
        pl.kernel, mesh=mesh,
        out_type=jax.ShapeDtypeStruct((MOE_ROWS, ROW_WORDS), jnp.int32),
        scratch_types=[pltpu.VMEM((n_chunks * TOP_K, SC_ROWS), jnp.int32)] + _SC_SCRATCH)
    def k(src_hbm, idx_hbm, out_hbm, idx_v, buf0, buf1, in0, in1, out0, out1):
        wid = _sc_worker_id()
        pltpu.sync_copy(idx_hbm.at[wid], idx_v)
        bufs, in_sems, out_sems = (buf0, buf1), (in0, in1), (out0, out1)

        def load(g, b):
            rows = pl.ds((wid * n_chunks + g) * SC_ROWS, SC_ROWS)
            return [pltpu.make_async_copy(src_hbm.at[rows], bufs[b], in_sems[b])]

        def store(g, b):
            return [pltpu.make_async_copy(bufs[b], out_hbm.at[idx_v.at[g * TOP_K + kk]], out_sems[b])
                    for kk in range(TOP_K)]

        _sc_double_buffered(n_chunks, load, store)

    return k(src, idx)


def _sc_collect(table, idx):
    n_chunks = idx.shape[1]
    assert n_chunks % 2 == 0
    mesh = plsc.VectorSubcoreMesh(core_axis_name="c", subcore_axis_name="s")

    @functools.partial(
        pl.kernel, mesh=mesh,
        out_type=jax.ShapeDtypeStruct((SC_WORKERS * n_chunks * SC_ROWS, ROW_WORDS), jnp.int32),
        scratch_types=[pltpu.VMEM((n_chunks, SC_ROWS), jnp.int32)] + _SC_SCRATCH)
    def k(table_hbm, idx_hbm, out_hbm, idx_v, buf0, buf1, in0, in1, out0, out1):
        wid = _sc_worker_id()
        pltpu.sync_copy(idx_hbm.at[wid], idx_v)
        bufs, in_sems, out_sems = (buf0, buf1), (in0, in1), (out0, out1)

        def load(g, b):
            return [pltpu.make_async_copy(table_hbm.at[idx_v.at[g]], bufs[b], in_sems[b])]

        def store(g, b):
            rows = pl.ds((wid * n_chunks + g) * SC_ROWS, SC_ROWS)
            return [pltpu.make_async_copy(bufs[b], out_hbm.at[rows], out_sems[b])]

        _sc_double_buffered(n_chunks, load, store)

    return k(table, idx)


def _expert_rows(words, n_valid, wgu_b, wd_b, bgu, bd):
    live = lax.broadcasted_iota(jnp.int32, words.shape, 0) < n_valid
    lo, hi = _unpack_pairs(jnp.where(live, words, 0))
    x = jnp.concatenate([lo, hi], axis=1).astype(BF16)
    gu = _dot(x, wgu_b[...]) + bgu
    g = jnp.minimum(gu[:, :D_EXPERT], SWIGLU_LIMIT)
    u = jnp.clip(gu[:, D_EXPERT:], -SWIGLU_LIMIT, SWIGLU_LIMIT)
    a = g * jax.nn.sigmoid(SWIGLU_ALPHA * g) * (u + 1.0)
    return _pack_pairs(_dot(a.astype(BF16), wd_b[...]) + bd)


def _moe_kernel(blk_e_ref, blk_first_ref, blk_rows_ref, blk_slot_ref, blk_next_ref, blk_io_ref,
                x_ref, wgu_hbm, bgu_ref, wd_hbm, bd_ref, y_ref,
                wgu_f, wd_f, wgu_b, wd_b, sem, *, layer):
    del blk_io_ref
    i = pl.program_id(0)
    n_valid = blk_rows_ref[i]
    quantum = MOE_TM // MOE_TAIL_PARTS

    def weight_copies(e, slot):
        return (pltpu.make_async_copy(wgu_hbm.at[layer, e], wgu_f.at[slot], sem.at[0, slot]),
                pltpu.make_async_copy(wd_hbm.at[layer, e], wd_f.at[slot], sem.at[1, slot]))

    @pl.when(i == 0)
    def _():
        for cp in weight_copies(blk_e_ref[0], blk_slot_ref[0]):
            cp.start()

    @pl.when(blk_first_ref[i] == 1)
    def _():
        slot = blk_slot_ref[i]
        for cp in weight_copies(blk_e_ref[i], slot):
            cp.wait()
        nxt = blk_next_ref[i]

        @pl.when(nxt >= 0)
        def _():
            for cp in weight_copies(nxt, 1 - slot):
                cp.start()

        wgu_b[...] = wgu_f[slot].astype(BF16)
        wd_b[...] = wd_f[slot].astype(BF16)

    for parts in range(1, MOE_TAIL_PARTS + 1):
        rows = parts * quantum

        @pl.when((n_valid > rows - quantum) & (n_valid <= rows))
        def _(rows=rows):
            y_ref[:rows] = _expert_rows(x_ref[:rows], n_valid, wgu_b, wd_b, bgu_ref[0, 0], bd_ref[0, 0])
            if rows < MOE_TM:
                y_ref[rows:] = jnp.zeros((MOE_TM - rows, ROW_WORDS), jnp.int32)


def _moe(layer, blk_meta, xs, w_gu, b_gu, w_down, b_down):
    def row_map(i, e, first, rows, slot, nxt, io):
        return (io[i], 0)

    def bias_map(i, e, *_):
        return (layer, e[i], 0, 0)

    grid_spec = pltpu.PrefetchScalarGridSpec(
        num_scalar_prefetch=6,
        grid=(MOE_BLOCKS,),
        in_specs=[
            pl.BlockSpec((MOE_TM, ROW_WORDS), row_map),
            pl.BlockSpec(memory_space=pl.ANY),
            pl.BlockSpec((1, 1, 1, 2 * D_EXPERT), bias_map),
            pl.BlockSpec(memory_space=pl.ANY),
            pl.BlockSpec((1, 1, 1, D_MODEL), bias_map),
        ],
        out_specs=pl.BlockSpec((MOE_TM, ROW_WORDS), row_map),
        scratch_shapes=[pltpu.VMEM((2, D_MODEL, 2 * D_EXPERT), F32), pltpu.VMEM((2, D_EXPERT, D_MODEL), F32),
                        pltpu.VMEM((D_MODEL, 2 * D_EXPERT), BF16), pltpu.VMEM((D_EXPERT, D_MODEL), BF16),
                        pltpu.SemaphoreType.DMA((2, 2))],
    )
    return pl.pallas_call(
        functools.partial(_moe_kernel, layer=layer),
        grid_spec=grid_spec,
        out_shape=jax.ShapeDtypeStruct((MOE_ROWS, ROW_WORDS), jnp.int32),
        compiler_params=_cparams(1),
        name="moe_experts",
    )(*blk_meta, xs, w_gu, b_gu.reshape(DEPTH, N_EXPERTS, 1, 2 * D_EXPERT),
      w_down, b_down.reshape(DEPTH, N_EXPERTS, 1, D_MODEL))


def _route(route, counts):
    experts = jnp.arange(N_EXPERTS, dtype=jnp.int32)
    padded = (counts + MOE_TM - 1) // MOE_TM * MOE_TM
    pad_end = jnp.cumsum(padded)
    pad_start = pad_end - padded
    e = route[0:TOP_K]
    onehot = e[:, :, None] == experts[None, None, :]
    dest = jnp.sum(jnp.where(onehot, pad_start[None, None, :], 0), axis=-1) + route[TOP_K:2 * TOP_K]
    blk_row0 = jnp.arange(MOE_BLOCKS, dtype=jnp.int32) * MOE_TM
    blk_e = jnp.minimum(jnp.sum((pad_end[None, :] <= blk_row0[:, None]).astype(jnp.int32), axis=1),
                        N_EXPERTS - 1)
    on = blk_row0 < pad_end[-1]
    n_on = jnp.sum(on.astype(jnp.int32))
    blk_onehot = blk_e[:, None] == experts[None, :]
    row_end = jnp.sum(jnp.where(blk_onehot, (pad_start + counts)[None, :], 0), axis=1)
    blk_rows = jnp.where(on, jnp.clip(row_end - blk_row0, 0, MOE_TM), 0).astype(jnp.int32)
    last_e = jnp.sum(jnp.where(jnp.arange(MOE_BLOCKS) == n_on - 1, blk_e, 0))
    blk_e = jnp.where(on, blk_e, last_e).astype(jnp.int32)
    prev = jnp.concatenate([jnp.full((1,), -1, jnp.int32), blk_e[:-1]])
    blk_first = (blk_e != prev).astype(jnp.int32)
    blk_slot = ((jnp.cumsum(blk_first) - 1) % 2).astype(jnp.int32)
    later_used = (experts[None, :] > experts[:, None]) & (counts[None, :] > 0)
    next_used = jnp.min(jnp.where(later_used, experts[None, :], N_EXPERTS), axis=1)
    next_used = jnp.where(next_used == N_EXPERTS, -1, next_used)
    blk_next = jnp.sum(jnp.where(blk_e[:, None] == experts[None, :], next_used[None, :], 0), axis=1)
    blk_io = jnp.minimum(jnp.arange(MOE_BLOCKS, dtype=jnp.int32), n_on - 1)
    return dest, (blk_e, blk_first, blk_rows, blk_slot, blk_next.astype(jnp.int32), blk_io)


def _combine_kernel(y_ref, gate_ref, x1_ref, mod_ref, g_ref, *o_refs):
    i = pl.program_id(0)
    m = mod_ref[0]
    gate = gate_ref[...]
    acc_lo = None
    for k in range(TOP_K):
        lo, hi = _unpack_pairs(y_ref[k])
        gk = gate[:, k:k + 1]
        acc_lo = gk * lo if acc_lo is None else acc_lo + gk * lo
        acc_hi = gk * hi if k == 0 else acc_hi + gk * hi
    acc = jnp.concatenate([acc_lo, acc_hi], axis=1)
    out = x1_ref[...] + _rms(acc, g_ref[...][3:4] * m[5:6])
    if len(o_refs) == 1:
        o_refs[0][...] = out
    else:
        @pl.when(i < NPT)
        def _():
            o_refs[0][...] = out

        @pl.when(i >= NPT)
        def _():
            o_refs[1][...] = out


def _combine(yg, gates, x1, mod_l, g_l, split_out):
    if split_out:
        out_specs = [_PROMPT_SPEC, _SAMPLE_SPEC]
        out_shape = [jax.ShapeDtypeStruct((NP_TOK, D_MODEL), F32), jax.ShapeDtypeStruct((NS_TOK, D_MODEL), F32)]
    else:
        out_specs = [_tok_spec(D_MODEL)]
        out_shape = [jax.ShapeDtypeStruct((N_TOK, D_MODEL), F32)]
    return pl.pallas_call(
        _combine_kernel,
        grid=(N_TILES,),
        in_specs=[pl.BlockSpec((TOP_K, TM, ROW_WORDS), lambda i: (0, i, 0)), _tok_spec(LANES),
                  _tok_spec(D_MODEL), _MOD_SPEC, _const_spec((4, D_MODEL))],
        out_specs=out_specs,
        out_shape=out_shape,
        compiler_params=_cparams(1),
        name="combine",
    )(yg, gates, x1, mod_l, g_l)


def _ffn(layer, o_cat, xs, mod_l, g_l, w_out, w_router, b_router, w_gu, b_gu, w_down, b_down, split_out):
    w_r = jnp.pad(w_router, ((0, 0), (0, LANES - N_EXPERTS))).astype(BF16)
    b_r = jnp.pad(b_router, (0, LANES - N_EXPERTS)).reshape(1, LANES)
    x1, h2p, route, gate_slab, counts = _postmix(o_cat, xs, mod_l, g_l, w_out.astype(BF16), w_r, b_r)
    dest, blk_meta = _route(route, counts[0, :N_EXPERTS])
    n_chunks = N_TOK // (SC_WORKERS * SC_ROWS)
    idx_d = dest.reshape(TOP_K, SC_WORKERS, n_chunks, SC_ROWS).transpose(1, 2, 0, 3).reshape(
        SC_WORKERS, n_chunks * TOP_K, SC_ROWS)
    rows = _sc_dispatch(h2p, idx_d)
    ys = _moe(layer, blk_meta, rows, w_gu, b_gu, w_down, b_down)
    idx_c = dest.reshape(SC_WORKERS, TOP_K * n_chunks, SC_ROWS)
    yg = _sc_collect(ys, idx_c).reshape(TOP_K, N_TOK, ROW_WORDS)
    return _combine(yg, gate_slab, x1, mod_l, g_l, split_out)


def _pad_heads(w, n_heads, width, keep):
    k = w.shape[0]
    w = w.reshape(k, n_heads, width)[:, :, :keep]
    return jnp.pad(w, ((0, 0), (0, 0), (0, LANES - keep))).reshape(k, n_heads * LANES)


def _pe_slab(x):
    return jnp.pad(x, [(0, 0)] * (x.ndim - 1) + [(MLA_NOPE, LANES - MLA_NOPE - MLA_ROPE)])


def _pair_kv_heads(w):
    g = SWA_HEADS // SWA_KV_HEADS
    return w.reshape(SWA_KV_HEADS, g, HEAD_DIM, -1).transpose(1, 0, 2, 3).reshape(w.shape)


def kernel(x_prompt, x_sample, cache_mla_ckv, cache_mla_krope, cache_diff_k, cache_diff_v, cache_swa_k, cache_swa_v, cache_na_k, cache_na_v, c, c_ctx, w_mod, b_mod, norm_g, w_in0, mla_q_norm, w_uq, mla_kv_norm, w_ukv, diff_lambda, diff_norm, w_out0, w_in1, swa_sink, na_rpb, w_out1, w_router, b_router, w_gu, b_gu, w_down, b_down):
    xs = (x_prompt.reshape(NP_TOK, D_MODEL), x_sample.reshape(NS_TOK, D_MODEL))
    cond = jnp.concatenate([c_ctx[None, :], c, jnp.zeros((16 - 1 - DEC_BATCH, D_MODEL), F32)], axis=0)
    mod = _modulation(cond, w_mod, b_mod).reshape(DEPTH, 16, 6, D_MODEL)
    t64, t32 = _rope_tables()
    states = {}
    for l in range(DEPTH):
        i = l // 2
        g_l = norm_g[l]
        mod_l = mod[l]
        if l % 2 == 0:
            lam_init = 0.8 - 0.6 * math.exp(-0.3 * l)
            wi = w_in0[i]
            w_in_p = jnp.concatenate(
                [wi[:, 0:640], wi[:, 672:2208], _pe_slab(wi[:, 640:672])], axis=1).astype(BF16)
            w_uq_p = _pad_heads(w_uq[i], MLA_HEADS, MLA_NOPE + MLA_ROPE, MLA_NOPE + MLA_ROPE).astype(BF16)
            w_k_p = _pad_heads(w_ukv[i], MLA_HEADS, MLA_NOPE + MLA_V, MLA_NOPE).astype(BF16)
            w_v = w_ukv[i].reshape(MLA_KV_RANK, MLA_HEADS, MLA_NOPE + MLA_V)[:, :, MLA_NOPE:].reshape(
                MLA_KV_RANK, MLA_HEADS * MLA_V).astype(BF16)
            (q, k, v, dq, dk, dv, ckv_st, kpe_st, dk_st, dv_st) = _premix0(
                *xs, mod_l, g_l[0:1], w_in_p, mla_q_norm[i][None, :], w_uq_p, mla_kv_norm[i][None, :],
                w_k_p, w_v, t32, t64)
            states['mla_ckv'] = ckv_st.reshape(BATCH, 1, SEQ, MLA_KV_RANK)
            states['mla_krope'] = kpe_st[:, MLA_NOPE:MLA_NOPE + MLA_ROPE].reshape(BATCH, 1, SEQ, MLA_ROPE)
            states['diff_k'] = dk_st.reshape(BATCH, 1, SEQ, DIFF_HEADS, 2 * DIFF_DH)
            states['diff_v'] = dv_st.reshape(BATCH, 1, SEQ, DIFF_HEADS, 2 * DIFF_DH)
            kc, vc = _mla_cache(cache_mla_ckv[:, i].reshape(DEC_BATCH * PAST_LEN, MLA_KV_RANK),
                                _pe_slab(cache_mla_krope[:, i].reshape(DEC_BATCH * PAST_LEN, MLA_ROPE)),
                                w_k_p, w_v)
            dkc = cache_diff_k[:, i].reshape(DEC_BATCH * PAST_LEN, 512).astype(BF16)
            dvc = cache_diff_v[:, i].reshape(DEC_BATCH * PAST_LEN, 512).astype(BF16)
            lam = diff_lambda[i]
            sub_g = diff_norm[i][None, :]
            o_p = _attn_ab(q, dq, (k, v, dk, dv), None, lam, sub_g, lam_init,
                           n_batch=BATCH, t_len=SEQ, tq=SEQ, tok_off=0, n_seq=PROMPT_SEQS_PER_STEP)
            o_cat = _attn_ab(q, dq, (k, v, dk, dv), (kc, vc, dkc, dvc), lam, sub_g, lam_init,
                             n_batch=DEC_BATCH, t_len=DEC_SEQ, tq=TQ_AB, tok_off=NP_TOK, out_init=o_p)
            w_out = w_out0[i]
        else:
            wi = w_in1[i]
            n_sq = SWA_HEADS * HEAD_DIM
            w_in_p = jnp.concatenate([_pair_kv_heads(wi[:, :n_sq].T).T, wi[:, n_sq:]], axis=1).astype(BF16)
            (sq, sk, sv, nq, nk, nv, sk_st, sv_st, nk_st, nv_st) = _premix1(xs[0], mod_l, g_l[0:1], w_in_p, t64)
            states['swa_k'] = sk_st.reshape(BATCH, 1, SEQ, SWA_KV_HEADS, HEAD_DIM)
            states['swa_v'] = sv_st.reshape(BATCH, 1, SEQ, SWA_KV_HEADS, HEAD_DIM)
            states['na_k'] = nk_st.reshape(BATCH, 1, SEQ, NA_HEADS, HEAD_DIM)
            states['na_v'] = nv_st.reshape(BATCH, 1, SEQ, NA_HEADS, HEAD_DIM)
            skc = cache_swa_k[:, i].reshape(DEC_BATCH * PAST_LEN, 128).astype(BF16)
            svc = cache_swa_v[:, i].reshape(DEC_BATCH * PAST_LEN, 128).astype(BF16)
            nkc = cache_na_k[:, i].reshape(DEC_BATCH * PAST_LEN, 512).astype(BF16)
            nvc = cache_na_v[:, i].reshape(DEC_BATCH * PAST_LEN, 512).astype(BF16)
            sink = swa_sink[i]
            o_p = _attn_cd_prompt(sink, sq, sk, sv, nq, nk, nv)
            o_cat = _attn_cd_sample(o_p, sink, sq, sk, sv, nq, nk, nv, skc, svc, nkc, nvc, _na_bias(na_rpb[i]))
            wo = w_out1[i]
            w_out = jnp.concatenate([_pair_kv_heads(wo[:n_sq]), wo[n_sq:]], axis=0)
        xs = _ffn(l, o_cat, xs, mod_l, g_l, w_out, w_router[l], b_router[l], w_gu, b_gu, w_down, b_down,
                  split_out=(l == DEPTH - 1))
    return (xs[0].reshape(BATCH, SEQ, D_MODEL), xs[1].reshape(DEC_BATCH, DEC_SEQ, D_MODEL),
            states['mla_ckv'], states['mla_krope'], states['diff_k'], states['diff_v'],
            states['swa_k'], states['swa_v'], states['na_k'], states['na_v'])
```

```python
import functools
import math

import numpy as np
import jax
import jax.numpy as jnp
from jax import lax
from jax.experimental import pallas as pl
from jax.experimental.pallas import tpu as pltpu
from jax.experimental.pallas import tpu_sc as plsc

F32 = jnp.float32
BF16 = jnp.bfloat16

D_MODEL = 1024
BATCH = 16
SEQ = 256
DEPTH = 2
DEC_BATCH = 8
DEC_SEQ = 2048
PAST_LEN = 256
GRID_W = 64
HEAD_DIM = 64
ROPE_THETA = 10000.0
EPS = 1e-6
NEG = -1e30

MLA_HEADS = 8
MLA_Q_RANK = 384
MLA_KV_RANK = 256
MLA_NOPE = 64
MLA_ROPE = 32
MLA_V = 64
DIFF_HEADS = 4
DIFF_DH = 64
SWA_HEADS = 8
SWA_KV_HEADS = 2
SWA_WINDOW = 128
NA_HEADS = 8
NA_WIN_ROWS = 8
NA_WIN_COLS = 16
N_EXPERTS = 32
TOP_K = 4
D_EXPERT = 1024
SWIGLU_LIMIT = 7.0
SWIGLU_ALPHA = 1.702

LANES = 128
NP_TOK = BATCH * SEQ
NS_TOK = DEC_BATCH * DEC_SEQ
N_TOK = NP_TOK + NS_TOK
TM = 512
NPT = NP_TOK // TM
TILES_PER_SAMPLE = DEC_SEQ // TM
N_TILES = N_TOK // TM
TM_WIDE = 1024
N_SUB = 2
SUB_TM = TM // N_SUB
TQ = 256
TQ_AB = 256
PROMPT_SEQS_PER_STEP = 2
MOE_TM = 1024
MOE_TAIL_PARTS = 8
MOE_ROWS = ((N_TOK * TOP_K + N_EXPERTS * (MOE_TM - 1)) // MOE_TM + 1) * MOE_TM
MOE_BLOCKS = MOE_ROWS // MOE_TM
NA_TILE_ROWS = TQ // GRID_W
NA_KEY_ROWS = 12
VMEM_LIMIT = 56 * 1024 * 1024


def _cparams(n_axes, vmem=VMEM_LIMIT):
    return pltpu.CompilerParams(dimension_semantics=("arbitrary",) * n_axes,
                                vmem_limit_bytes=vmem)


def _rms(x, g):
    return x * lax.rsqrt(jnp.mean(x * x, axis=-1, keepdims=True) + EPS) * g


def _dot(a, b):
    return jnp.dot(a, b, preferred_element_type=F32)


def _dot_nt(a, b):
    return lax.dot_general(a, b, (((1,), (1,)), ((), ())), preferred_element_type=F32)


def _rope(x, cos, sin_a, sin_b, half):
    return (x * cos + pltpu.roll(x, LANES - half, 1) * sin_a + pltpu.roll(x, half, 1) * sin_b)


def _pipelined_units(units):
    pending = units[0][0](units[0][2])
    for n, (_, finish, arg) in enumerate(units):
        following = units[n + 1][0](units[n + 1][2]) if n + 1 < len(units) else None
        finish(arg, pending)
        pending = following


def _sub_rows(n):
    return slice(SUB_TM * n, SUB_TM * (n + 1))


def _store_head_rows(st_ref, n, hd, n_heads, x):
    st_ref[pl.ds(SUB_TM * n * n_heads + hd, SUB_TM, stride=n_heads), :] = x


def _mod_row(i, tm=TM):
    npt = NP_TOK // tm
    return jnp.where(i < npt, 0, 1 + (i - npt) // (DEC_SEQ // tm))


def _rope_blk(i):
    return jnp.where(i < NPT, TILES_PER_SAMPLE, (i - NPT) % TILES_PER_SAMPLE)


def _mod_kernel(c_ref, w_ref, b_ref, o_ref):
    c = c_ref[...]
    s = (c * jax.nn.sigmoid(c)).astype(BF16)
    o_ref[0] = _dot(s, w_ref[0].astype(BF16)) + b_ref[0]


def _modulation(cond, w_mod, b_mod):
    nb = 1024
    return pl.pallas_call(
        _mod_kernel,
        grid=(DEPTH, 6 * D_MODEL // nb),
        in_specs=[
            pl.BlockSpec((16, D_MODEL), lambda l, n: (0, 0)),
            pl.BlockSpec((1, D_MODEL, nb), lambda l, n: (l, 0, n)),
            pl.BlockSpec((1, 1, nb), lambda l, n: (l, 0, n)),
        ],
        out_specs=pl.BlockSpec((1, 16, nb), lambda l, n: (l, 0, n)),
        out_shape=jax.ShapeDtypeStruct((DEPTH, 16, 6 * D_MODEL), F32),
        compiler_params=_cparams(2),
        name="modulation",
    )(cond, w_mod, b_mod.reshape(DEPTH, 1, 6 * D_MODEL))


def _rope_tables():
    t = jnp.arange(DEC_SEQ)
    rows = (t // GRID_W).astype(F32)
    cols = (t % GRID_W).astype(F32)

    def angles(r):
        n = r // 4
        inv = ROPE_THETA ** (-jnp.arange(n, dtype=F32) / n)
        return jnp.concatenate([rows[:, None] * inv[None], cols[:, None] * inv[None]], axis=-1)

    def finish(cos, sa, sb):
        ident = (jnp.ones((TM, LANES), F32), jnp.zeros((TM, LANES), F32), jnp.zeros((TM, LANES), F32))
        return tuple(jnp.concatenate([a, b], axis=0) for a, b in zip((cos, sa, sb), ident))

    a64 = angles(64)
    c, s, z = jnp.cos(a64), jnp.sin(a64), jnp.zeros_like(a64)
    t64 = finish(jnp.concatenate([c, c, c, c], -1), jnp.concatenate([-s, z, -s, z], -1),
                 jnp.concatenate([z, s, z, s], -1))
    a32 = angles(32)
    c, s, z = jnp.cos(a32), jnp.sin(a32), jnp.zeros_like(a32)
    one64 = jnp.ones((DEC_SEQ, 64), F32)
    z64 = jnp.zeros((DEC_SEQ, 64), F32)
    z32 = jnp.zeros((DEC_SEQ, 32), F32)
    t32 = finish(jnp.concatenate([one64, c, c, z32], -1), jnp.concatenate([z64, -s, z, z32], -1),
                 jnp.concatenate([z64, z, s, z32], -1))
    return t64, t32


LOG2E = math.log2(math.e)
_DIFF_COLS = DIFF_HEADS * 2 * DIFF_DH
_AB_COLS = tuple(np.cumsum([0, MLA_Q_RANK, MLA_KV_RANK, _DIFF_COLS, _DIFF_COLS, _DIFF_COLS, LANES]).tolist())
_CD_COLS = tuple(np.cumsum([0, SWA_HEADS * HEAD_DIM, SWA_KV_HEADS * HEAD_DIM, SWA_KV_HEADS * HEAD_DIM,
                            NA_HEADS * HEAD_DIM, NA_HEADS * HEAD_DIM, NA_HEADS * HEAD_DIM]).tolist())
IN_COLS = _AB_COLS[-1]
assert IN_COLS == _CD_COLS[-1]
_MLA_SCALE = (MLA_NOPE + MLA_ROPE) ** -0.5 * LOG2E
_QSCALE = HEAD_DIM ** -0.5 * LOG2E


def _premix0_kernel(xp_ref, xs_ref, mod_ref, g_ref, win_ref, qn_ref, wuq_ref, kvn_ref, wk_ref, wv_ref,
                    c32_ref, sa32_ref, sb32_ref, c64_ref, sa64_ref, sb64_ref,
                    q_ref, k_ref, v_ref, dq_ref, dk_ref, dv_ref,
                    ckv_st, kpe_st, dk_st, dv_st):
    i = pl.program_id(0)
    m = mod_ref[0]
    gain = g_ref[...] * (1.0 + m[1:2])

    def project(n):
        r = _sub_rows(n)
        x = jnp.where(i < NPT, xp_ref[r, :], xs_ref[r, :])
        h = _rms(x, gain) + m[0:1]
        return _dot(h.astype(BF16), win_ref[...])

    states = {}

    def finish(n, proj):
        r = _sub_rows(n)
        q_a, kv_a, dq, dk, dv, pe = (proj[:, a:b] for a, b in zip(_AB_COLS[:-1], _AB_COLS[1:]))
        q = _dot(_rms(q_a, qn_ref[...]).astype(BF16), wuq_ref[...])
        ckv = _rms(kv_a, kvn_ref[...])
        ckv_b = ckv.astype(BF16)
        kn = _dot(ckv_b, wk_ref[...])
        v_ref[r, :] = _dot(ckv_b, wv_ref[...]).astype(BF16)
        c32, sa32, sb32 = c32_ref[r, :], sa32_ref[r, :], sb32_ref[r, :]
        c64, sa64, sb64 = c64_ref[r, :], sa64_ref[r, :], sb64_ref[r, :]
        pe_r = _rope(pe, c32, sa32, sb32, MLA_ROPE // 2)
        for hd in range(MLA_HEADS):
            sl = slice(LANES * hd, LANES * (hd + 1))
            q_ref[r, sl] = (_rope(q[:, sl], c32, sa32, sb32, MLA_ROPE // 2) * _MLA_SCALE).astype(BF16)
            k_ref[r, sl] = (kn[:, sl] + pe_r).astype(BF16)
        for hd in range(DIFF_HEADS):
            sl = slice(LANES * hd, LANES * (hd + 1))
            dq_ref[r, sl] = (_rope(dq[:, sl], c64, sa64, sb64, DIFF_DH // 2) * _QSCALE).astype(BF16)
            dk_ref[r, sl] = _rope(dk[:, sl], c64, sa64, sb64, DIFF_DH // 2).astype(BF16)
        dv_ref[r, :] = dv.astype(BF16)
        states[n] = (ckv, pe, dk, dv)

    _pipelined_units([(project, finish, n) for n in range(N_SUB)])

    @pl.when(i < NPT)
    def _():
        for n in range(N_SUB):
            r = _sub_rows(n)
            ckv, pe, dk, dv = states[n]
            ckv_st[r, :] = ckv
            kpe_st[r, :] = pe
            for hd in range(DIFF_HEADS):
                _store_head_rows(dk_st, n, hd, DIFF_HEADS, dk[:, LANES * hd: LANES * (hd + 1)])
                _store_head_rows(dv_st, n, hd, DIFF_HEADS, dv[:, LANES * hd: LANES * (hd + 1)])


def _premix1_kernel(x_ref, mod_ref, g_ref, win_ref, c64_ref, sa64_ref, sb64_ref,
                    sq_ref, sk_ref, sv_ref, nq_ref, nk_ref, nv_ref,
                    sk_st, sv_st, nk_st, nv_st):
    i = pl.program_id(0)
    m = mod_ref[0]
    gain = g_ref[...] * (1.0 + m[1:2])

    def project(n):
        h = _rms(x_ref[_sub_rows(n), :], gain) + m[0:1]
        return _dot(h.astype(BF16), win_ref[...])

    states = {}

    def finish(n, proj):
        r = _sub_rows(n)
        sq, sk, sv, nq, nk, nv = (proj[:, a:b] for a, b in zip(_CD_COLS[:-1], _CD_COLS[1:]))
        c64, sa64, sb64 = c64_ref[r, :], sa64_ref[r, :], sb64_ref[r, :]
        for hd in range(4):
            sl = slice(LANES * hd, LANES * (hd + 1))
            sq_ref[r, sl] = (_rope(sq[:, sl], c64, sa64, sb64, HEAD_DIM // 2) * _QSCALE).astype(BF16)
        sk_ref[r, :] = _rope(sk, c64, sa64, sb64, HEAD_DIM // 2).astype(BF16)
        sv_ref[r, :] = sv.astype(BF16)
        nq_ref[r, :] = (nq * _QSCALE).astype(BF16)
        nk_ref[r, :] = nk.astype(BF16)
        nv_ref[r, :] = nv.astype(BF16)
        states[n] = (sk, sv, nk, nv)

    _pipelined_units([(project, finish, n) for n in range(N_SUB)])

    @pl.when(i < NPT)
    def _():
        for n in range(N_SUB):
            r = _sub_rows(n)
            sk, sv, nk, nv = states[n]
            sk_st[r, :] = sk
            sv_st[r, :] = sv
            for hd in range(NA_HEADS):
                _store_head_rows(nk_st, n, hd, NA_HEADS, nk[:, HEAD_DIM * hd: HEAD_DIM * (hd + 1)])
                _store_head_rows(nv_st, n, hd, NA_HEADS, nv[:, HEAD_DIM * hd: HEAD_DIM * (hd + 1)])


def _tok_spec(width, tm=TM):
    return pl.BlockSpec((tm, width), lambda i: (i, 0))


def _prompt_spec(tm=TM):
    return pl.BlockSpec((tm, D_MODEL), lambda i: (jnp.minimum(i, NP_TOK // tm - 1), 0))


def _sample_spec(tm=TM):
    return pl.BlockSpec((tm, D_MODEL), lambda i: (jnp.maximum(i - NP_TOK // tm, 0), 0))


_PROMPT_SPEC = _prompt_spec()
_SAMPLE_SPEC = _sample_spec()


def _state_spec(width, rows_per_token=1):
    return pl.BlockSpec((TM * rows_per_token, width), lambda i: (jnp.minimum(i, NPT - 1), 0))


def _const_spec(shape):
    return pl.BlockSpec(shape, lambda i: (0,) * len(shape))


def _mod_spec(tm=TM):
    return pl.BlockSpec((1, 6, D_MODEL), lambda i: (_mod_row(i, tm), 0, 0))


_MOD_SPEC = _mod_spec()
_ROPE_SPEC = pl.BlockSpec((TM, LANES), lambda i: (_rope_blk(i), 0))


def _premix0(xp, xs, mod_l, g0, w_in_p, q_norm, w_uq_p, kv_norm, w_k_p, w_v, t32, t64):
    outs = [(N_TOK, 1024, BF16), (N_TOK, 1024, BF16), (N_TOK, 512, BF16), (N_TOK, 512, BF16),
            (N_TOK, 512, BF16), (N_TOK, 512, BF16),
            (NP_TOK, 256, F32), (NP_TOK, 128, F32),
            (NP_TOK * DIFF_HEADS, 2 * DIFF_DH, F32), (NP_TOK * DIFF_HEADS, 2 * DIFF_DH, F32)]
    return pl.pallas_call(
        _premix0_kernel,
        grid=(N_TILES,),
        in_specs=[_PROMPT_SPEC, _SAMPLE_SPEC, _MOD_SPEC, _const_spec((1, D_MODEL)),
                  _const_spec((D_MODEL, IN_COLS)), _const_spec((1, MLA_Q_RANK)),
                  _const_spec((MLA_Q_RANK, 1024)), _const_spec((1, MLA_KV_RANK)),
                  _const_spec((MLA_KV_RANK, 1024)), _const_spec((MLA_KV_RANK, 512))]
                 + [_ROPE_SPEC] * 6,
        out_specs=([_tok_spec(w) for (_, w, _) in outs[:6]]
                   + [_state_spec(w, n // NP_TOK) for (n, w, _) in outs[6:]]),
        out_shape=[jax.ShapeDtypeStruct((n, w), dt) for (n, w, dt) in outs],
        compiler_params=_cparams(1),
        name="premix_ab",
    )(xp, xs, mod_l, g0, w_in_p, q_norm, w_uq_p, kv_norm, w_k_p, w_v, *t32, *t64)


def _premix1(x, mod_l, g0, w_in_p, t64):
    outs = [(N_TOK, 512, BF16), (N_TOK, 128, BF16), (N_TOK, 128, BF16), (N_TOK, 512, BF16),
            (N_TOK, 512, BF16), (N_TOK, 512, BF16),
            (NP_TOK, 128, F32), (NP_TOK, 128, F32),
            (NP_TOK * NA_HEADS, HEAD_DIM, F32), (NP_TOK * NA_HEADS, HEAD_DIM, F32)]
    return pl.pallas_call(
        _premix1_kernel,
        grid=(N_TILES,),
        in_specs=[_tok_spec(D_MODEL), _MOD_SPEC, _const_spec((1, D_MODEL)),
                  _const_spec((D_MODEL, IN_COLS))] + [_ROPE_SPEC] * 3,
        out_specs=([_tok_spec(w) for (_, w, _) in outs[:6]]
                   + [_state_spec(w, n // NP_TOK) for (n, w, _) in outs[6:]]),
        out_shape=[jax.ShapeDtypeStruct((n, w), dt) for (n, w, dt) in outs],
        compiler_params=_cparams(1),
        name="premix_cd",
    )(x, mod_l, g0, w_in_p, *t64)


def _mla_cache_kernel(ckv_ref, pe_ref, wk_ref, wv_ref, k_ref, v_ref):
    c = ckv_ref[...].astype(BF16)
    kn = _dot(c, wk_ref[...])
    v_ref[...] = _dot(c, wv_ref[...]).astype(BF16)
    pe = pe_ref[...]
    for hd in range(MLA_HEADS):
        sl = slice(LANES * hd, LANES * (hd + 1))
        k_ref[:, sl] = (kn[:, sl] + pe).astype(BF16)


def _mla_cache(ckv, pe_slab, w_k_p, w_v):
    n = ckv.shape[0]
    tm = 512
    return pl.pallas_call(
        _mla_cache_kernel,
        grid=(n // tm,),
        in_specs=[pl.BlockSpec((tm, MLA_KV_RANK), lambda i: (i, 0)),
                  pl.BlockSpec((tm, LANES), lambda i: (i, 0)),
                  _const_spec((MLA_KV_RANK, 1024)), _const_spec((MLA_KV_RANK, 512))],
        out_specs=[pl.BlockSpec((tm, 1024), lambda i: (i, 0)), pl.BlockSpec((tm, 512), lambda i: (i, 0))],
        out_shape=[jax.ShapeDtypeStruct((n, 1024), BF16), jax.ShapeDtypeStruct((n, 512), BF16)],
        compiler_params=_cparams(1),
        name="mla_cache",
    )(ckv, pe_slab, w_k_p, w_v)


def _softmax_pv(scores, values, sink=None):
    m = jnp.max(scores[0], axis=-1, keepdims=True)
    for s in scores[1:]:
        m = jnp.maximum(m, jnp.max(s, axis=-1, keepdims=True))
    if sink is not None:
        m = jnp.maximum(m, sink)
    l = None
    o = None
    for s, v in zip(scores, values):
        p = jnp.exp2(s - m)
        ls = jnp.sum(p, axis=-1, keepdims=True)
        os_ = _dot(p.astype(BF16), v)
        l = ls if l is None else l + ls
        o = os_ if o is None else o + os_
    if sink is not None:
        l = l + jnp.exp2(sink - m)
    return o * (1.0 / l)


def _lane_lo(shape):
    return lax.broadcasted_iota(jnp.int32, shape, 1) < (LANES // 2)


def _split_halves(qb):
    lo = _lane_lo(qb.shape)
    zero = jnp.zeros_like(qb)
    return jnp.where(lo, qb, zero), jnp.where(lo, zero, qb)


def _attn_ab_kernel(*refs, n_pieces, n_seq, lam_init, aliased):
    if aliased:
        refs = refs[1:]
    q_ref, dq_ref = refs[0], refs[1]
    pieces = [refs[2 + 4 * p: 6 + 4 * p] for p in range(n_pieces)]
    lam_ref, subg_ref, o_ref = refs[2 + 4 * n_pieces:]
    lam = lam_ref[...]
    lam_full = (jnp.exp(jnp.sum(lam[0:1] * lam[1:2], axis=-1, keepdims=True))
                - jnp.exp(jnp.sum(lam[2:3] * lam[3:4], axis=-1, keepdims=True)) + lam_init)
    tq = q_ref.shape[0] // n_seq
    lo = _lane_lo((tq, LANES))
    subg = subg_ref[...] * (1.0 - lam_init)

    def seq_rows(ref, sq):
        n = ref.shape[0] // n_seq
        return slice(n * sq, n * (sq + 1))

    def mla_scores(arg):
        sq, hd = arg
        sl = slice(LANES * hd, LANES * (hd + 1))
        qh = q_ref[seq_rows(q_ref, sq), sl]
        return [_dot_nt(qh, k_ref[seq_rows(k_ref, sq), sl]) for (k_ref, _, _, _) in pieces]

    def diff_scores(arg):
        sq, hd = arg
        sl = slice(LANES * hd, LANES * (hd + 1))
        qq = jnp.concatenate(_split_halves(dq_ref[seq_rows(dq_ref, sq), sl]), axis=0)
        return [_dot_nt(qq, dk_ref[seq_rows(dk_ref, sq), sl]) for (_, _, dk_ref, _) in pieces]

    pair = {}

    def mla_finish(arg, scores):
        sq, hd = arg
        j = hd // 2
        vals = [v_ref[seq_rows(v_ref, sq), LANES * j: LANES * (j + 1)] for (_, v_ref, _, _) in pieces]
        pair[hd % 2] = _softmax_pv(scores, vals)
        if hd % 2 == 1:
            o_ref[seq_rows(o_ref, sq), LANES * j: LANES * (j + 1)] = jnp.where(lo, pair[0], pair[1]).astype(BF16)

    def diff_finish(arg, scores):
        sq, hd = arg
        sl = slice(LANES * hd, LANES * (hd + 1))
        oo = _softmax_pv(scores, [dv_ref[seq_rows(dv_ref, sq), sl] for (_, _, _, dv_ref) in pieces])
        od = _rms(oo[:tq] - lam_full * oo[tq:], subg)
        o_ref[seq_rows(o_ref, sq), 512 + LANES * hd: 512 + LANES * (hd + 1)] = od.astype(BF16)

    units = []
    for sq in range(n_seq):
        for j in range(DIFF_HEADS):
            units += [(diff_scores, diff_finish, (sq, j)), (mla_scores, mla_finish, (sq, 2 * j)),
                      (mla_scores, mla_finish, (sq, 2 * j + 1))]
    _pipelined_units(units)


def _attn_ab(q, dq, new_kv, cache_kv, lam, sub_g, lam_init, *, n_batch, t_len, tq, tok_off, out_init=None,
             n_seq=1):
    assert n_seq == 1 or (tq == t_len and cache_kv is None and n_batch % n_seq == 0)
    n_batch, t_len, tq = n_batch // n_seq, t_len * n_seq, tq * n_seq
    nq = t_len // tq
    q_off = tok_off // tq
    b_off = tok_off // t_len
    widths = (1024, 512, 512, 512)
    in_specs = [pl.BlockSpec((tq, 1024), lambda b, i: (q_off + b * nq + i, 0)),
                pl.BlockSpec((tq, 512), lambda b, i: (q_off + b * nq + i, 0))]
    args = [q, dq]
    for w, a in zip(widths, new_kv):
        in_specs.append(pl.BlockSpec((t_len, w), lambda b, i: (b_off + b, 0)))
        args.append(a)
    n_pieces = 1
    if cache_kv is not None:
        n_pieces = 2
        for w, a in zip(widths, cache_kv):
            in_specs.append(pl.BlockSpec((PAST_LEN, w), lambda b, i: (b, 0)))
            args.append(a)
    in_specs += [pl.BlockSpec((4, DIFF_DH), lambda b, i: (0, 0)),
                 pl.BlockSpec((1, 2 * DIFF_DH), lambda b, i: (0, 0))]
    args += [lam, sub_g]
    aliases = {}
    if out_init is not None:
        in_specs = [pl.BlockSpec(memory_space=pl.ANY)] + in_specs
        args = [out_init] + args
        aliases = {0: 0}
    return pl.pallas_call(
        functools.partial(_attn_ab_kernel, n_pieces=n_pieces, n_seq=n_seq, lam_init=lam_init,
                          aliased=out_init is not None),
        grid=(n_batch, nq),
        in_specs=in_specs,
        out_specs=pl.BlockSpec((tq, 1024), lambda b, i: (q_off + b * nq + i, 0)),
        out_shape=jax.ShapeDtypeStruct((N_TOK, 1024), BF16),
        input_output_aliases=aliases,
        compiler_params=_cparams(2),
        name="attn_ab_%d" % n_pieces,
    )(*args)


def _gqa_stacks(sq_ref, sink_ref):
    tq = sq_ref.shape[0]
    halves = [_split_halves(sq_ref[:, LANES * j: LANES * (j + 1)]) for j in range(4)]
    q_stacks = [jnp.concatenate([halves[j][kvh] for j in range(4)], axis=0) for kvh in range(SWA_KV_HEADS)]
    sinks = [jnp.concatenate([jnp.full((tq, 1), sink_ref[4 * kvh + j] * LOG2E, F32) for j in range(4)], axis=0)
             for kvh in range(SWA_KV_HEADS)]
    return q_stacks, sinks


def _attn_cd_prompt_kernel(sink_ref, sq_ref, sk_ref, sv_ref, nq_ref, nk_ref, nv_ref, o_ref):
    lo = _lane_lo((SEQ, LANES))
    for sq in range(PROMPT_SEQS_PER_STEP):
        r = slice(SEQ * sq, SEQ * (sq + 1))
        sk = sk_ref[r, :]
        sv = sv_ref[r, :]
        for j in range(4):
            sl = slice(LANES * j, LANES * (j + 1))
            q_lo, q_hi = _split_halves(sq_ref[r, sl])
            o_lo = _softmax_pv([_dot_nt(q_lo, sk)], [sv], sink=sink_ref[j] * LOG2E)
            o_hi = _softmax_pv([_dot_nt(q_hi, sk)], [sv], sink=sink_ref[j + 4] * LOG2E)
            o_ref[r, sl] = jnp.where(lo, o_lo, o_hi).astype(BF16)
        for j in range(4):
            sl = slice(LANES * j, LANES * (j + 1))
            q_lo, q_hi = _split_halves(nq_ref[r, sl])
            k = nk_ref[r, sl]
            v = nv_ref[r, sl]
            o_lo = _softmax_pv([_dot_nt(q_lo, k)], [v])
            o_hi = _softmax_pv([_dot_nt(q_hi, k)], [v])
            o_ref[r, 512 + LANES * j: 512 + LANES * (j + 1)] = jnp.where(lo, o_lo, o_hi).astype(BF16)


def _attn_cd_prompt(sink, sq, sk, sv, nq, nk, nv):
    def spec(w):
        return pl.BlockSpec((SEQ * PROMPT_SEQS_PER_STEP, w), lambda b: (b, 0))
    return pl.pallas_call(
        _attn_cd_prompt_kernel,
        grid=(BATCH // PROMPT_SEQS_PER_STEP,),
        in_specs=[pl.BlockSpec(memory_space=pltpu.SMEM), spec(512), spec(128), spec(128),
                  spec(512), spec(512), spec(512)],
        out_specs=spec(1024),
        out_shape=jax.ShapeDtypeStruct((N_TOK, 1024), BF16),
        compiler_params=_cparams(1),
        name="attn_cd_prompt",
    )(sink, sq, sk, sv, nq, nk, nv)


_SWA_KEYS = TQ + 2 * SWA_WINDOW


def _attn_cd_sample_kernel(init_ref, sink_ref, sq_ref, nq_ref, sk_ref, sv_ref, nk_ref, nv_ref,
                           skc_ref, svc_ref, nkc_ref, nvc_ref, bias_ref, o_ref):
    del init_ref
    qi = pl.program_id(1)
    lo = _lane_lo((TQ, LANES))
    ks = pl.multiple_of(jnp.clip(qi * TQ - SWA_WINDOW, 0, DEC_SEQ - _SWA_KEYS), SWA_WINDOW)
    k_win = sk_ref[pl.ds(ks, _SWA_KEYS), :]
    v_win = sv_ref[pl.ds(ks, _SWA_KEYS), :]
    q_pos = qi * TQ + (lax.broadcasted_iota(jnp.int32, (4 * TQ, _SWA_KEYS), 0) & (TQ - 1))
    k_pos = ks + lax.broadcasted_iota(jnp.int32, (4 * TQ, _SWA_KEYS), 1)
    in_win = jnp.abs(q_pos - k_pos) <= SWA_WINDOW
    skc = skc_ref[...]
    svc = svc_ref[...]
    q_stacks, sinks = _gqa_stacks(sq_ref, sink_ref)

    def window_scores(kvh):
        return [_dot_nt(q_stacks[kvh], skc), jnp.where(in_win, _dot_nt(q_stacks[kvh], k_win), NEG)]

    o_kv = {}

    def window_finish(kvh, scores):
        o_kv[kvh] = _softmax_pv(scores, [svc, v_win], sink=sinks[kvh])
        if kvh == SWA_KV_HEADS - 1:
            for j in range(4):
                rows = slice(TQ * j, TQ * (j + 1))
                o_ref[:, LANES * j: LANES * (j + 1)] = jnp.where(lo, o_kv[0][rows], o_kv[1][rows]).astype(BF16)

    n_rows = DEC_SEQ // GRID_W
    r0 = jnp.clip(qi * NA_TILE_ROWS - NA_WIN_ROWS // 2, 0, n_rows - NA_KEY_ROWS)
    kn = pl.multiple_of(r0 * GRID_W, GRID_W)

    lane_lo = _lane_lo((1, LANES))
    pieces = []
    for ri in range(NA_TILE_ROWS):
        r = qi * NA_TILE_ROWS + ri
        rs = jnp.clip(r - NA_WIN_ROWS // 2, 0, n_rows - NA_WIN_ROWS)
        row = []
        for mm in range(NA_KEY_ROWS // 2):
            kr = r0 + 2 * mm
            ok = [(kr + t >= rs) & (kr + t < rs + NA_WIN_ROWS) for t in range(2)]
            mask = jnp.where(lane_lo, jnp.where(ok[0], 0.0, NEG), jnp.where(ok[1], 0.0, NEG))
            row.append((jnp.clip(kr - r + NA_WIN_ROWS, 0, 2 * NA_WIN_ROWS - 1), mask))
        pieces.append(row)

    def add_na_bias(hd, s):
        return jnp.concatenate(
            [jnp.concatenate([s[GRID_W * ri: GRID_W * (ri + 1), LANES * mm: LANES * (mm + 1)] + bias_ref[hd, d] + mask
                              for mm, (d, mask) in enumerate(row)], axis=1)
             for ri, row in enumerate(pieces)], axis=0)

    def na_scores(hd):
        sl = slice(LANES * (hd // 2), LANES * (hd // 2 + 1))
        q_half = _split_halves(nq_ref[:, sl])[hd % 2]
        nk_win = nk_ref[pl.ds(kn, NA_KEY_ROWS * GRID_W), sl]
        return [_dot_nt(q_half, nkc_ref[:, sl]), add_na_bias(hd, _dot_nt(q_half, nk_win))]

    pair = {}

    def na_finish(hd, scores):
        j = hd // 2
        sl = slice(LANES * j, LANES * (j + 1))
        nv_win = nv_ref[pl.ds(kn, NA_KEY_ROWS * GRID_W), sl]
        pair[hd % 2] = _softmax_pv(scores, [nvc_ref[:, sl], nv_win])
        if hd % 2 == 1:
            o_ref[:, 512 + LANES * j: 512 + LANES * (j + 1)] = jnp.where(lo, pair[0], pair[1]).astype(BF16)

    _pipelined_units([(window_scores, window_finish, kvh) for kvh in range(SWA_KV_HEADS)]
                     + [(na_scores, na_finish, hd) for hd in range(NA_HEADS)])


def _attn_cd_sample(out_init, sink, sq, sk, sv, nq, nk, nv, skc, svc, nkc, nvc, bias):
    nq_t = DEC_SEQ // TQ
    q_off = NP_TOK // TQ
    b_off = NP_TOK // DEC_SEQ

    def qspec(w):
        return pl.BlockSpec((TQ, w), lambda b, i: (q_off + b * nq_t + i, 0))

    def kspec(w):
        return pl.BlockSpec((DEC_SEQ, w), lambda b, i: (b_off + b, 0))

    def cspec(w):
        return pl.BlockSpec((PAST_LEN, w), lambda b, i: (b, 0))

    bias_spec = pl.BlockSpec((NA_HEADS, 2 * NA_WIN_ROWS, GRID_W, LANES), lambda b, i: (0, 0, 0, 0))
    return pl.pallas_call(
        _attn_cd_sample_kernel,
        grid=(DEC_BATCH, nq_t),
        in_specs=[pl.BlockSpec(memory_space=pl.ANY), pl.BlockSpec(memory_space=pltpu.SMEM),
                  qspec(512), qspec(512), kspec(128), kspec(128), kspec(512), kspec(512),
                  cspec(128), cspec(128), cspec(512), cspec(512), bias_spec],
        out_specs=pl.BlockSpec((TQ, 1024), lambda b, i: (q_off + b * nq_t + i, 0)),
        out_shape=jax.ShapeDtypeStruct((N_TOK, 1024), BF16),
        input_output_aliases={0: 0},
        compiler_params=_cparams(2),
        name="attn_cd_sample",
    )(out_init, sink, sq, nq, sk, sv, nk, nv, skc, svc, nkc, nvc, bias)


def _na_bias(rpb):
    n_dc = 2 * NA_WIN_COLS - 1
    c = np.arange(GRID_W)[:, None]
    kc = np.arange(GRID_W)[None, :]
    qs = np.clip(c - NA_WIN_COLS // 2, 0, GRID_W - NA_WIN_COLS)
    col_ok = (kc >= qs) & (kc < qs + NA_WIN_COLS)
    dc = np.clip(kc - c + NA_WIN_COLS - 1, 0, n_dc - 1)
    onehot = ((dc[None] == np.arange(n_dc)[:, None, None]) & col_ok[None]).astype(np.float32)
    blocks = jnp.einsum('hrd,dck->hrck', rpb.astype(F32) * LOG2E, onehot, precision=lax.Precision.HIGHEST)
    blocks = jnp.where(col_ok[None, None], blocks, NEG)
    none = jnp.full((NA_HEADS, 1, GRID_W, GRID_W), NEG, F32)
    return jnp.concatenate([jnp.concatenate([none, blocks], axis=1),
                            jnp.concatenate([blocks, none], axis=1)], axis=-1)


_HI_MASK = -65536


def _pack_pairs(x):
    w = x.shape[1] // 2
    r = x.astype(BF16).astype(F32)
    lo = lax.bitcast_convert_type(r[:, :w], jnp.int32)
    hi = lax.bitcast_convert_type(r[:, w:], jnp.int32)
    return (hi & _HI_MASK) | lax.shift_right_logical(lo, 16)


def _unpack_pairs(p):
    lo = lax.bitcast_convert_type(lax.shift_left(p, 16), F32)
    hi = lax.bitcast_convert_type(p & _HI_MASK, F32)
    return lo, hi


def _postmix_kernel(*refs, split_x, tm):
    if split_x:
        o_ref, xp_ref, xs_ref = refs[:3]
        refs = refs[3:]
    else:
        o_ref, x_ref = refs[:2]
        refs = refs[2:]
    (mod_ref, g_ref, wout_ref, wr_ref, br_ref, tri_ref,
     x1_ref, h2_ref, route_ref, gate_ref, cnt_ref, run_ref) = refs
    i = pl.program_id(0)

    @pl.when(i == 0)
    def _():
        run_ref[...] = jnp.zeros_like(run_ref)

    m = mod_ref[0]
    g = g_ref[...]
    gate_gain = g[1:2] * m[2:3]
    ffn_gain = g[2:3] * (1.0 + m[4:5])
    sub_logits = {}
    sub = tm // N_SUB
    npt = NP_TOK // tm

    def project(n):
        return _dot(o_ref[sub * n: sub * (n + 1), :], wout_ref[...])

    def finish(n, y):
        r = slice(sub * n, sub * (n + 1))
        x = jnp.where(i < npt, xp_ref[r, :], xs_ref[r, :]) if split_x else x_ref[r, :]
        x1 = x + _rms(y, gate_gain)
        x1_ref[r, :] = x1
        h2 = _rms(x1, ffn_gain) + m[3:4]
        h2_ref[r, :] = _pack_pairs(h2)
        sub_logits[n] = _dot(h2.astype(BF16), wr_ref[...]) + br_ref[...]

    _pipelined_units([(project, finish, n) for n in range(N_SUB)])

    logits = jnp.concatenate([sub_logits[n] for n in range(N_SUB)], axis=0)
    lane = lax.broadcasted_iota(jnp.int32, logits.shape, 1).astype(F32)
    cur = jnp.where(lane < N_EXPERTS, logits, -jnp.inf)
    tops, idxs = [], []
    for _ in range(TOP_K):
        mx = jnp.max(cur, axis=-1, keepdims=True)
        ix = jnp.min(jnp.where(cur == mx, lane, float(LANES)), axis=-1, keepdims=True)
        tops.append(mx)
        idxs.append(ix)
        cur = jnp.where(lane == ix, -jnp.inf, cur)
    es = [jnp.exp(t - tops[0]) for t in tops]
    inv = 1.0 / (es[0] + es[1] + es[2] + es[3])
    picked = jnp.zeros_like(logits)
    for k in range(TOP_K):
        picked = jnp.where(lane == idxs[k], 1.0, picked)
    before = _dot(tri_ref[...], picked.astype(BF16)) + run_ref[0:1, :]
    route = jnp.zeros_like(logits)
    gate_out = jnp.zeros_like(logits)
    for k in range(TOP_K):
        rank = jnp.sum(jnp.where(lane == idxs[k], before, 0.0), axis=-1, keepdims=True)
        route = jnp.where(lane == float(k), idxs[k], route)
        route = jnp.where(lane == float(TOP_K + k), rank, route)
        gate_out = jnp.where(lane == float(k), es[k] * inv, gate_out)
    route_ref[...] = route.T[:2 * TOP_K].astype(jnp.int32)
    gate_ref[...] = gate_out
    run_ref[...] = run_ref[...] + jnp.sum(picked, axis=0, keepdims=True)
    cnt_ref[...] = run_ref[...].astype(jnp.int32)


def _postmix(o_cat, xs, mod_l, g_l, w_out, w_r, b_r):
    tm = TM_WIDE
    tri = jnp.asarray(np.tril(np.ones((tm, tm), np.float32), -1), BF16)
    split_x = len(xs) == 2
    x_specs = [_prompt_spec(tm), _sample_spec(tm)] if split_x else [_tok_spec(D_MODEL, tm)]
    return pl.pallas_call(
        functools.partial(_postmix_kernel, split_x=split_x, tm=tm),
        grid=(N_TOK // tm,),
        in_specs=[_tok_spec(1024, tm)] + x_specs + [_mod_spec(tm), _const_spec((4, D_MODEL)),
                  _const_spec((1024, D_MODEL)), _const_spec((D_MODEL, LANES)), _const_spec((1, LANES)),
                  _const_spec((tm, tm))],
        out_specs=[_tok_spec(D_MODEL, tm), _tok_spec(D_MODEL // 2, tm),
                   pl.BlockSpec((2 * TOP_K, tm), lambda i: (0, i)), _tok_spec(LANES, tm), _const_spec((8, LANES))],
        out_shape=[jax.ShapeDtypeStruct((N_TOK, D_MODEL), F32),
                   jax.ShapeDtypeStruct((N_TOK, D_MODEL // 2), jnp.int32),
                   jax.ShapeDtypeStruct((2 * TOP_K, N_TOK), jnp.int32), jax.ShapeDtypeStruct((N_TOK, LANES), F32),
                   jax.ShapeDtypeStruct((8, LANES), jnp.int32)],
        scratch_shapes=[pltpu.VMEM((8, LANES), F32)],
        compiler_params=_cparams(1),
        name="postmix",
    )(o_cat, *xs, mod_l, g_l, w_out, w_r, b_r, tri)


SC_WORKERS = 32
SC_ROWS = 64
ROW_WORDS = D_MODEL // 2

_SC_SCRATCH = [pltpu.VMEM((SC_ROWS, ROW_WORDS), jnp.int32), pltpu.VMEM((SC_ROWS, ROW_WORDS), jnp.int32),
               pltpu.SemaphoreType.DMA, pltpu.SemaphoreType.DMA, pltpu.SemaphoreType.DMA, pltpu.SemaphoreType.DMA]


def _sc_worker_id():
    return lax.axis_index("s") * 2 + lax.axis_index("c")


def _sc_double_buffered(n_chunks, load, store):
    for cp in load(0, 0):
        cp.start()

    @pl.loop(0, n_chunks, step=2)
    def _(g0):
        for b in range(2):
            g = g0 + b
            for cp in load(g, b):
                cp.wait()

            @pl.when(g >= 1)
            def _():
                for cp in store(g - 1, 1 - b):
                    cp.wait()

            @pl.when(g + 1 < n_chunks)
            def _():
                for cp in load(g + 1, 1 - b):
                    cp.start()

            for cp in store(g, b):
                cp.start()

    for cp in store(n_chunks - 1, (n_chunks - 1) % 2):
        cp.wait()


def _sc_dispatch(src, idx):
    n_chunks = N_TOK // (SC_WORKERS * SC_ROWS)
    assert n_chunks % 2 == 0
    mesh = plsc.VectorSubcoreMesh(core_axis_name="c", subcore_axis_name="s")

    @functools.partial(
        pl.kernel, mesh=mesh,
        out_type=jax.ShapeDtypeStruct((MOE_ROWS, ROW_WORDS), jnp.int32),
        scratch_types=[pltpu.VMEM((n_chunks * TOP_K, SC_ROWS), jnp.int32)] + _SC_SCRATCH)
    def k(src_hbm, idx_hbm, out_hbm, idx_v, buf0, buf1, in0, in1, out0, out1):
        wid = _sc_worker_id()
        pltpu.sync_copy(idx_hbm.at[wid], idx_v)
        bufs, in_sems, out_sems = (buf0, buf1), (in0, in1), (out0, out1)

        def load(g, b):
            rows = pl.ds((wid * n_chunks + g) * SC_ROWS, SC_ROWS)
            return [pltpu.make_async_copy(src_hbm.at[rows], bufs[b], in_sems[b])]

        def store(g, b):
            return [pltpu.make_async_copy(bufs[b], out_hbm.at[idx_v.at[g * TOP_K + kk]], out_sems[b])
                    for kk in range(TOP_K)]

        _sc_double_buffered(n_chunks, load, store)

    return k(src, idx)


def _sc_collect(table, idx):
    n_chunks = idx.shape[1]
    assert n_chunks % 2 == 0
    mesh = plsc.VectorSubcoreMesh(core_axis_name="c", subcore_axis_name="s")

    @functools.partial(
        pl.kernel, mesh=mesh,
        out_type=jax.ShapeDtypeStruct((SC_WORKERS * n_chunks * SC_ROWS, ROW_WORDS), jnp.int32),
        scratch_types=[pltpu.VMEM((n_chunks, SC_ROWS), jnp.int32)] + _SC_SCRATCH)
    def k(table_hbm, idx_hbm, out_hbm, idx_v, buf0, buf1, in0, in1, out0, out1):
        wid = _sc_worker_id()
        pltpu.sync_copy(idx_hbm.at[wid], idx_v)
        bufs, in_sems, out_sems = (buf0, buf1), (in0, in1), (out0, out1)

        def load(g, b):
            return [pltpu.make_async_copy(table_hbm.at[idx_v.at[g]], bufs[b], in_sems[b])]

        def store(g, b):
            rows = pl.ds((wid * n_chunks + g) * SC_ROWS, SC_ROWS)
            return [pltpu.make_async_copy(bufs[b], out_hbm.at[rows], out_sems[b])]

        _sc_double_buffered(n_chunks, load, store)

    return k(table, idx)


def _expert_rows(words, n_valid, wgu_b, wd_b, bgu, bd):
    live = lax.broadcasted_iota(jnp.int32, words.shape, 0) < n_valid
    lo, hi = _unpack_pairs(jnp.where(live, words, 0))
    x = jnp.concatenate([lo, hi], axis=1).astype(BF16)
    gu = _dot(x, wgu_b[...]) + bgu
    g = jnp.minimum(gu[:, :D_EXPERT], SWIGLU_LIMIT)
    u = jnp.clip(gu[:, D_EXPERT:], -SWIGLU_LIMIT, SWIGLU_LIMIT)
    a = g * jax.nn.sigmoid(SWIGLU_ALPHA * g) * (u + 1.0)
    return _pack_pairs(_dot(a.astype(BF16), wd_b[...]) + bd)


def _moe_kernel(blk_e_ref, blk_first_ref, blk_rows_ref, blk_slot_ref, blk_next_ref, blk_io_ref,
                x_ref, wgu_hbm, bgu_ref, wd_hbm, bd_ref, y_ref,
                wgu_f, wd_f, wgu_b, wd_b, sem, *, layer):
    del blk_io_ref
    i = pl.program_id(0)
    n_valid = blk_rows_ref[i]
    quantum = MOE_TM // MOE_TAIL_PARTS

    def weight_copies(e, slot):
        return (pltpu.make_async_copy(wgu_hbm.at[layer, e], wgu_f.at[slot], sem.at[0, slot]),
                pltpu.make_async_copy(wd_hbm.at[layer, e], wd_f.at[slot], sem.at[1, slot]))

    @pl.when(i == 0)
    def _():
        for cp in weight_copies(blk_e_ref[0], blk_slot_ref[0]):
            cp.start()

    @pl.when(blk_first_ref[i] == 1)
    def _():
        slot = blk_slot_ref[i]
        for cp in weight_copies(blk_e_ref[i], slot):
            cp.wait()
        nxt = blk_next_ref[i]

        @pl.when(nxt >= 0)
        def _():
            for cp in weight_copies(nxt, 1 - slot):
                cp.start()

        wgu_b[...] = wgu_f[slot].astype(BF16)
        wd_b[...] = wd_f[slot].astype(BF16)

    for parts in range(1, MOE_TAIL_PARTS + 1):
        rows = parts * quantum

        @pl.when((n_valid > rows - quantum) & (n_valid <= rows))
        def _(rows=rows):
            y_ref[:rows] = _expert_rows(x_ref[:rows], n_valid, wgu_b, wd_b, bgu_ref[0, 0], bd_ref[0, 0])
            if rows < MOE_TM:
                y_ref[rows:] = jnp.zeros((MOE_TM - rows, ROW_WORDS), jnp.int32)


def _moe(layer, blk_meta, xs, w_gu, b_gu, w_down, b_down):
    def row_map(i, e, first, rows, slot, nxt, io):
        return (io[i], 0)

    def bias_map(i, e, *_):
        return (layer, e[i], 0, 0)

    grid_spec = pltpu.PrefetchScalarGridSpec(
        num_scalar_prefetch=6,
        grid=(MOE_BLOCKS,),
        in_specs=[
            pl.BlockSpec((MOE_TM, ROW_WORDS), row_map),
            pl.BlockSpec(memory_space=pl.ANY),
            pl.BlockSpec((1, 1, 1, 2 * D_EXPERT), bias_map),
            pl.BlockSpec(memory_space=pl.ANY),
            pl.BlockSpec((1, 1, 1, D_MODEL), bias_map),
        ],
        out_specs=pl.BlockSpec((MOE_TM, ROW_WORDS), row_map),
        scratch_shapes=[pltpu.VMEM((2, D_MODEL, 2 * D_EXPERT), F32), pltpu.VMEM((2, D_EXPERT, D_MODEL), F32),
                        pltpu.VMEM((D_MODEL, 2 * D_EXPERT), BF16), pltpu.VMEM((D_EXPERT, D_MODEL), BF16),
                        pltpu.SemaphoreType.DMA((2, 2))],
    )
    return pl.pallas_call(
        functools.partial(_moe_kernel, layer=layer),
        grid_spec=grid_spec,
        out_shape=jax.ShapeDtypeStruct((MOE_ROWS, ROW_WORDS), jnp.int32),
        compiler_params=_cparams(1),
        name="moe_experts",
    )(*blk_meta, xs, w_gu, b_gu.reshape(DEPTH, N_EXPERTS, 1, 2 * D_EXPERT),
      w_down, b_down.reshape(DEPTH, N_EXPERTS, 1, D_MODEL))


def _route(route, counts):
    experts = jnp.arange(N_EXPERTS, dtype=jnp.int32)
    padded = (counts + MOE_TM - 1) // MOE_TM * MOE_TM
    pad_end = jnp.cumsum(padded)
    pad_start = pad_end - padded
    e = route[0:TOP_K]
    onehot = e[:, :, None] == experts[None, None, :]
    dest = jnp.sum(jnp.where(onehot, pad_start[None, None, :], 0), axis=-1) + route[TOP_K:2 * TOP_K]
    blk_row0 = jnp.arange(MOE_BLOCKS, dtype=jnp.int32) * MOE_TM
    blk_e = jnp.minimum(jnp.sum((pad_end[None, :] <= blk_row0[:, None]).astype(jnp.int32), axis=1),
                        N_EXPERTS - 1)
    on = blk_row0 < pad_end[-1]
    n_on = jnp.sum(on.astype(jnp.int32))
    blk_onehot = blk_e[:, None] == experts[None, :]
    row_end = jnp.sum(jnp.where(blk_onehot, (pad_start + counts)[None, :], 0), axis=1)
    blk_rows = jnp.where(on, jnp.clip(row_end - blk_row0, 0, MOE_TM), 0).astype(jnp.int32)
    last_e = jnp.sum(jnp.where(jnp.arange(MOE_BLOCKS) == n_on - 1, blk_e, 0))
    blk_e = jnp.where(on, blk_e, last_e).astype(jnp.int32)
    prev = jnp.concatenate([jnp.full((1,), -1, jnp.int32), blk_e[:-1]])
    blk_first = (blk_e != prev).astype(jnp.int32)
    blk_slot = ((jnp.cumsum(blk_first) - 1) % 2).astype(jnp.int32)
    later_used = (experts[None, :] > experts[:, None]) & (counts[None, :] > 0)
    next_used = jnp.min(jnp.where(later_used, experts[None, :], N_EXPERTS), axis=1)
    next_used = jnp.where(next_used == N_EXPERTS, -1, next_used)
    blk_next = jnp.sum(jnp.where(blk_e[:, None] == experts[None, :], next_used[None, :], 0), axis=1)
    blk_io = jnp.minimum(jnp.arange(MOE_BLOCKS, dtype=jnp.int32), n_on - 1)
    return dest, (blk_e, blk_first, blk_rows, blk_slot, blk_next.astype(jnp.int32), blk_io)


def _combine_kernel(y_ref, gate_ref, x1_ref, mod_ref, g_ref, *o_refs, npt):
    i = pl.program_id(0)
    m = mod_ref[0]
    gate = gate_ref[...]
    acc_lo = None
    for k in range(TOP_K):
        lo, hi = _unpack_pairs(y_ref[k])
        gk = gate[:, k:k + 1]
        acc_lo = gk * lo if acc_lo is None else acc_lo + gk * lo
        acc_hi = gk * hi if k == 0 else acc_hi + gk * hi
    acc = jnp.concatenate([acc_lo, acc_hi], axis=1)
    out = x1_ref[...] + _rms(acc, g_ref[...][3:4] * m[5:6])
    if len(o_refs) == 1:
        o_refs[0][...] = out
    else:
        @pl.when(i < npt)
        def _():
            o_refs[0][...] = out

        @pl.when(i >= npt)
        def _():
            o_refs[1][...] = out


def _combine(yg, gates, x1, mod_l, g_l, split_out):
    tm = TM_WIDE
    if split_out:
        out_specs = [_prompt_spec(tm), _sample_spec(tm)]
        out_shape = [jax.ShapeDtypeStruct((NP_TOK, D_MODEL), F32), jax.ShapeDtypeStruct((NS_TOK, D_MODEL), F32)]
    else:
        out_specs = [_tok_spec(D_MODEL, tm)]
        out_shape = [jax.ShapeDtypeStruct((N_TOK, D_MODEL), F32)]
    return pl.pallas_call(
        functools.partial(_combine_kernel, npt=NP_TOK // tm),
        grid=(N_TOK // tm,),
        in_specs=[pl.BlockSpec((TOP_K, tm, ROW_WORDS), lambda i: (0, i, 0)), _tok_spec(LANES, tm),
                  _tok_spec(D_MODEL, tm), _mod_spec(tm), _const_spec((4, D_MODEL))],
        out_specs=out_specs,
        out_shape=out_shape,
        compiler_params=_cparams(1),
        name="combine",
    )(yg, gates, x1, mod_l, g_l)


def _ffn(layer, o_cat, xs, mod_l, g_l, w_out, w_router, b_router, w_gu, b_gu, w_down, b_down, split_out):
    w_r = jnp.pad(w_router, ((0, 0), (0, LANES - N_EXPERTS))).astype(BF16)
    b_r = jnp.pad(b_router, (0, LANES - N_EXPERTS)).reshape(1, LANES)
    x1, h2p, route, gate_slab, counts = _postmix(o_cat, xs, mod_l, g_l, w_out.astype(BF16), w_r, b_r)
    dest, blk_meta = _route(route, counts[0, :N_EXPERTS])
    n_chunks = N_TOK // (SC_WORKERS * SC_ROWS)
    idx_d = dest.reshape(TOP_K, SC_WORKERS, n_chunks, SC_ROWS).transpose(1, 2, 0, 3).reshape(
        SC_WORKERS, n_chunks * TOP_K, SC_ROWS)
    rows = _sc_dispatch(h2p, idx_d)
    ys = _moe(layer, blk_meta, rows, w_gu, b_gu, w_down, b_down)
    idx_c = dest.reshape(SC_WORKERS, TOP_K * n_chunks, SC_ROWS)
    yg = _sc_collect(ys, idx_c).reshape(TOP_K, N_TOK, ROW_WORDS)
    return _combine(yg, gate_slab, x1, mod_l, g_l, split_out)


def _pad_heads(w, n_heads, width, keep):
    k = w.shape[0]
    w = w.reshape(k, n_heads, width)[:, :, :keep]
    return jnp.pad(w, ((0, 0), (0, 0), (0, LANES - keep))).reshape(k, n_heads * LANES)


def _pe_slab(x):
    return jnp.pad(x, [(0, 0)] * (x.ndim - 1) + [(MLA_NOPE, LANES - MLA_NOPE - MLA_ROPE)])


def _pair_kv_heads(w):
    g = SWA_HEADS // SWA_KV_HEADS
    return w.reshape(SWA_KV_HEADS, g, HEAD_DIM, -1).transpose(1, 0, 2, 3).reshape(w.shape)


def kernel(x_prompt, x_sample, cache_mla_ckv, cache_mla_krope, cache_diff_k, cache_diff_v, cache_swa_k, cache_swa_v, cache_na_k, cache_na_v, c, c_ctx, w_mod, b_mod, norm_g, w_in0, mla_q_norm, w_uq, mla_kv_norm, w_ukv, diff_lambda, diff_norm, w_out0, w_in1, swa_sink, na_rpb, w_out1, w_router, b_router, w_gu, b_gu, w_down, b_down):
    xs = (x_prompt.reshape(NP_TOK, D_MODEL), x_sample.reshape(NS_TOK, D_MODEL))
    cond = jnp.concatenate([c_ctx[None, :], c, jnp.zeros((16 - 1 - DEC_BATCH, D_MODEL), F32)], axis=0)
    mod = _modulation(cond, w_mod, b_mod).reshape(DEPTH, 16, 6, D_MODEL)
    t64, t32 = _rope_tables()
    states = {}
    for l in range(DEPTH):
        i = l // 2
        g_l = norm_g[l]
        mod_l = mod[l]
        if l % 2 == 0:
            lam_init = 0.8 - 0.6 * math.exp(-0.3 * l)
            wi = w_in0[i]
            w_in_p = jnp.concatenate(
                [wi[:, 0:640], wi[:, 672:2208], _pe_slab(wi[:, 640:672])], axis=1).astype(BF16)
            w_uq_p = _pad_heads(w_uq[i], MLA_HEADS, MLA_NOPE + MLA_ROPE, MLA_NOPE + MLA_ROPE).astype(BF16)
            w_k_p = _pad_heads(w_ukv[i], MLA_HEADS, MLA_NOPE + MLA_V, MLA_NOPE).astype(BF16)
            w_v = w_ukv[i].reshape(MLA_KV_RANK, MLA_HEADS, MLA_NOPE + MLA_V)[:, :, MLA_NOPE:].reshape(
                MLA_KV_RANK, MLA_HEADS * MLA_V).astype(BF16)
            (q, k, v, dq, dk, dv, ckv_st, kpe_st, dk_st, dv_st) = _premix0(
                *xs, mod_l, g_l[0:1], w_in_p, mla_q_norm[i][None, :], w_uq_p, mla_kv_norm[i][None, :],
                w_k_p, w_v, t32, t64)
            states['mla_ckv'] = ckv_st.reshape(BATCH, 1, SEQ, MLA_KV_RANK)
            states['mla_krope'] = kpe_st[:, MLA_NOPE:MLA_NOPE + MLA_ROPE].reshape(BATCH, 1, SEQ, MLA_ROPE)
            states['diff_k'] = dk_st.reshape(BATCH, 1, SEQ, DIFF_HEADS, 2 * DIFF_DH)
            states['diff_v'] = dv_st.reshape(BATCH, 1, SEQ, DIFF_HEADS, 2 * DIFF_DH)
            kc, vc = _mla_cache(cache_mla_ckv[:, i].reshape(DEC_BATCH * PAST_LEN, MLA_KV_RANK),
                                _pe_slab(cache_mla_krope[:, i].reshape(DEC_BATCH * PAST_LEN, MLA_ROPE)),
                                w_k_p, w_v)
            dkc = cache_diff_k[:, i].reshape(DEC_BATCH * PAST_LEN, 512).astype(BF16)
            dvc = cache_diff_v[:, i].reshape(DEC_BATCH * PAST_LEN, 512).astype(BF16)
            lam = diff_lambda[i]
            sub_g = diff_norm[i][None, :]
            o_p = _attn_ab(q, dq, (k, v, dk, dv), None, lam, sub_g, lam_init,
                           n_batch=BATCH, t_len=SEQ, tq=SEQ, tok_off=0, n_seq=PROMPT_SEQS_PER_STEP)
            o_cat = _attn_ab(q, dq, (k, v, dk, dv), (kc, vc, dkc, dvc), lam, sub_g, lam_init,
                             n_batch=DEC_BATCH, t_len=DEC_SEQ, tq=TQ_AB, tok_off=NP_TOK, out_init=o_p)
            w_out = w_out0[i]
        else:
            wi = w_in1[i]
            n_sq = SWA_HEADS * HEAD_DIM
            w_in_p = jnp.concatenate([_pair_kv_heads(wi[:, :n_sq].T).T, wi[:, n_sq:]], axis=1).astype(BF16)
            (sq, sk, sv, nq, nk, nv, sk_st, sv_st, nk_st, nv_st) = _premix1(xs[0], mod_l, g_l[0:1], w_in_p, t64)
            states['swa_k'] = sk_st.reshape(BATCH, 1, SEQ, SWA_KV_HEADS, HEAD_DIM)
            states['swa_v'] = sv_st.reshape(BATCH, 1, SEQ, SWA_KV_HEADS, HEAD_DIM)
            states['na_k'] = nk_st.reshape(BATCH, 1, SEQ, NA_HEADS, HEAD_DIM)
            states['na_v'] = nv_st.reshape(BATCH, 1, SEQ, NA_HEADS, HEAD_DIM)
            skc = cache_swa_k[:, i].reshape(DEC_BATCH * PAST_LEN, 128).astype(BF16)
            svc = cache_swa_v[:, i].reshape(DEC_BATCH * PAST_LEN, 128).astype(BF16)
            nkc = cache_na_k[:, i].reshape(DEC_BATCH * PAST_LEN, 512).astype(BF16)
            nvc = cache_na_v[:, i].reshape(DEC_BATCH * PAST_LEN, 512).astype(BF16)
            sink = swa_sink[i]
            o_p = _attn_cd_prompt(sink, sq, sk, sv, nq, nk, nv)
            o_cat = _attn_cd_sample(o_p, sink, sq, sk, sv, nq, nk, nv, skc, svc, nkc, nvc, _na_bias(na_rpb[i]))
            wo = w_out1[i]
            w_out = jnp.concatenate([_pair_kv_heads(wo[:n_sq]), wo[n_sq:]], axis=0)
        xs = _ffn(l, o_cat, xs, mod_l, g_l, w_out, w_router[l], b_router[l], w_gu, b_gu, w_down, b_down,
                  split_out=(l == DEPTH - 1))
    return (xs[0].reshape(BATCH, SEQ, D_MODEL), xs[1].reshape(DEC_BATCH, DEC_SEQ, D_MODEL),
            states['mla_ckv'], states['mla_krope'], states['diff_k'], states['diff_v'],
            states['swa_k'], states['swa_v'], states['na_k'], states['na_v'])
```

```python
import functools
import math

import numpy as np
import jax
import jax.numpy as jnp
from jax import lax
from jax.experimental import pallas as pl
from jax.experimental.pallas import tpu as pltpu
from jax.experimental.pallas import tpu_sc as plsc

F32 = jnp.float32
BF16 = jnp.bfloat16

D_MODEL = 1024
BATCH = 16
SEQ = 256
DEPTH = 2
DEC_BATCH = 8
DEC_SEQ = 2048
PAST_LEN = 256
GRID_W = 64
HEAD_DIM = 64
ROPE_THETA = 10000.0
EPS = 1e-6
NEG = -1e30

MLA_HEADS = 8
MLA_Q_RANK = 384
MLA_KV_RANK = 256
MLA_NOPE = 64
MLA_ROPE = 32
MLA_V = 64
DIFF_HEADS = 4
DIFF_DH = 64
SWA_HEADS = 8
SWA_KV_HEADS = 2
SWA_WINDOW = 128
NA_HEADS = 8
NA_WIN_ROWS = 8
NA_WIN_COLS = 16
N_EXPERTS = 32
TOP_K = 4
D_EXPERT = 1024
SWIGLU_LIMIT = 7.0
SWIGLU_ALPHA = 1.702

LANES = 128
NP_TOK = BATCH * SEQ
NS_TOK = DEC_BATCH * DEC_SEQ
N_TOK = NP_TOK + NS_TOK
TM = 512
NPT = NP_TOK // TM
TILES_PER_SAMPLE = DEC_SEQ // TM
N_TILES = N_TOK // TM
N_SUB = 2
SUB_TM = TM // N_SUB
TQ = 256
TQ_AB = 256
PROMPT_SEQS_PER_STEP = 2
MOE_TM = 1024
MOE_TAIL_PARTS = 8
MOE_ROWS = ((N_TOK * TOP_K + N_EXPERTS * (MOE_TM - 1)) // MOE_TM + 1) * MOE_TM
MOE_BLOCKS = MOE_ROWS // MOE_TM
NA_TILE_ROWS = TQ // GRID_W
NA_KEY_ROWS = 12
VMEM_LIMIT = 56 * 1024 * 1024


def _cparams(n_axes, vmem=VMEM_LIMIT):
    return pltpu.CompilerParams(dimension_semantics=("arbitrary",) * n_axes,
                                vmem_limit_bytes=vmem)


def _rms(x, g):
    return x * lax.rsqrt(jnp.mean(x * x, axis=-1, keepdims=True) + EPS) * g


def _dot(a, b):
    return jnp.dot(a, b, preferred_element_type=F32)


def _dot_nt(a, b):
    return lax.dot_general(a, b, (((1,), (1,)), ((), ())), preferred_element_type=F32)


def _rope(x, cos, sin_a, sin_b, half):
    return (x * cos + pltpu.roll(x, LANES - half, 1) * sin_a + pltpu.roll(x, half, 1) * sin_b)


def _pipelined_units(units):
    pending = units[0][0](units[0][2])
    for n, (_, finish, arg) in enumerate(units):
        following = units[n + 1][0](units[n + 1][2]) if n + 1 < len(units) else None
        finish(arg, pending)
        pending = following


def _sub_rows(n):
    return slice(SUB_TM * n, SUB_TM * (n + 1))


def _store_head_rows(st_ref, n, hd, n_heads, x):
    st_ref[pl.ds(SUB_TM * n * n_heads + hd, SUB_TM, stride=n_heads), :] = x


def _mod_row(i):
    return jnp.where(i < NPT, 0, 1 + (i - NPT) // TILES_PER_SAMPLE)


def _rope_blk(i):
    return jnp.where(i < NPT, TILES_PER_SAMPLE, (i - NPT) % TILES_PER_SAMPLE)


def _mod_kernel(c_ref, w_ref, b_ref, o_ref):
    c = c_ref[...]
    s = (c * jax.nn.sigmoid(c)).astype(BF16)
    o_ref[0] = _dot(s, w_ref[0].astype(BF16)) + b_ref[0]


def _modulation(cond, w_mod, b_mod):
    nb = 1024
    return pl.pallas_call(
        _mod_kernel,
        grid=(DEPTH, 6 * D_MODEL // nb),
        in_specs=[
            pl.BlockSpec((16, D_MODEL), lambda l, n: (0, 0)),
            pl.BlockSpec((1, D_MODEL, nb), lambda l, n: (l, 0, n)),
            pl.BlockSpec((1, 1, nb), lambda l, n: (l, 0, n)),
        ],
        out_specs=pl.BlockSpec((1, 16, nb), lambda l, n: (l, 0, n)),
        out_shape=jax.ShapeDtypeStruct((DEPTH, 16, 6 * D_MODEL), F32),
        compiler_params=_cparams(2),
        name="modulation",
    )(cond, w_mod, b_mod.reshape(DEPTH, 1, 6 * D_MODEL))


def _rope_tables():
    t = jnp.arange(DEC_SEQ)
    rows = (t // GRID_W).astype(F32)
    cols = (t % GRID_W).astype(F32)

    def angles(r):
        n = r // 4
        inv = ROPE_THETA ** (-jnp.arange(n, dtype=F32) / n)
        return jnp.concatenate([rows[:, None] * inv[None], cols[:, None] * inv[None]], axis=-1)

    def finish(cos, sa, sb):
        ident = (jnp.ones((TM, LANES), F32), jnp.zeros((TM, LANES), F32), jnp.zeros((TM, LANES), F32))
        return tuple(jnp.concatenate([a, b], axis=0) for a, b in zip((cos, sa, sb), ident))

    a64 = angles(64)
    c, s, z = jnp.cos(a64), jnp.sin(a64), jnp.zeros_like(a64)
    t64 = finish(jnp.concatenate([c, c, c, c], -1), jnp.concatenate([-s, z, -s, z], -1),
                 jnp.concatenate([z, s, z, s], -1))
    a32 = angles(32)
    c, s, z = jnp.cos(a32), jnp.sin(a32), jnp.zeros_like(a32)
    one64 = jnp.ones((DEC_SEQ, 64), F32)
    z64 = jnp.zeros((DEC_SEQ, 64), F32)
    z32 = jnp.zeros((DEC_SEQ, 32), F32)
    t32 = finish(jnp.concatenate([one64, c, c, z32], -1), jnp.concatenate([z64, -s, z, z32], -1),
                 jnp.concatenate([z64, z, s, z32], -1))
    return t64, t32


LOG2E = math.log2(math.e)
_DIFF_COLS = DIFF_HEADS * 2 * DIFF_DH
_AB_COLS = tuple(np.cumsum([0, MLA_Q_RANK, MLA_KV_RANK, _DIFF_COLS, _DIFF_COLS, _DIFF_COLS, LANES]).tolist())
_CD_COLS = tuple(np.cumsum([0, SWA_HEADS * HEAD_DIM, SWA_KV_HEADS * HEAD_DIM, SWA_KV_HEADS * HEAD_DIM,
                            NA_HEADS * HEAD_DIM, NA_HEADS * HEAD_DIM, NA_HEADS * HEAD_DIM]).tolist())
IN_COLS = _AB_COLS[-1]
assert IN_COLS == _CD_COLS[-1]
_MLA_SCALE = (MLA_NOPE + MLA_ROPE) ** -0.5 * LOG2E
_QSCALE = HEAD_DIM ** -0.5 * LOG2E


def _premix0_kernel(xp_ref, xs_ref, mod_ref, g_ref, win_ref, qn_ref, wuq_ref, kvn_ref, wk_ref, wv_ref,
                    c32_ref, sa32_ref, sb32_ref, c64_ref, sa64_ref, sb64_ref,
                    q_ref, k_ref, v_ref, dq_ref, dk_ref, dv_ref,
                    ckv_st, kpe_st, dk_st, dv_st):
    i = pl.program_id(0)
    m = mod_ref[0]
    gain = g_ref[...] * (1.0 + m[1:2])

    def project(n):
        r = _sub_rows(n)
        x = jnp.where(i < NPT, xp_ref[r, :], xs_ref[r, :])
        h = _rms(x, gain) + m[0:1]
        return _dot(h.astype(BF16), win_ref[...])

    states = {}

    def finish(n, proj):
        r = _sub_rows(n)
        q_a, kv_a, dq, dk, dv, pe = (proj[:, a:b] for a, b in zip(_AB_COLS[:-1], _AB_COLS[1:]))
        q = _dot(_rms(q_a, qn_ref[...]).astype(BF16), wuq_ref[...])
        ckv = _rms(kv_a, kvn_ref[...])
        ckv_b = ckv.astype(BF16)
        kn = _dot(ckv_b, wk_ref[...])
        v_ref[r, :] = _dot(ckv_b, wv_ref[...]).astype(BF16)
        c32, sa32, sb32 = c32_ref[r, :], sa32_ref[r, :], sb32_ref[r, :]
        c64, sa64, sb64 = c64_ref[r, :], sa64_ref[r, :], sb64_ref[r, :]
        pe_r = _rope(pe, c32, sa32, sb32, MLA_ROPE // 2)
        for hd in range(MLA_HEADS):
            sl = slice(LANES * hd, LANES * (hd + 1))
            q_ref[r, sl] = (_rope(q[:, sl], c32, sa32, sb32, MLA_ROPE // 2) * _MLA_SCALE).astype(BF16)
            k_ref[r, sl] = (kn[:, sl] + pe_r).astype(BF16)
        for hd in range(DIFF_HEADS):
            sl = slice(LANES * hd, LANES * (hd + 1))
            dq_ref[r, sl] = (_rope(dq[:, sl], c64, sa64, sb64, DIFF_DH // 2) * _QSCALE).astype(BF16)
            dk_ref[r, sl] = _rope(dk[:, sl], c64, sa64, sb64, DIFF_DH // 2).astype(BF16)
        dv_ref[r, :] = dv.astype(BF16)
        states[n] = (ckv, pe, dk, dv)

    _pipelined_units([(project, finish, n) for n in range(N_SUB)])

    @pl.when(i < NPT)
    def _():
        for n in range(N_SUB):
            r = _sub_rows(n)
            ckv, pe, dk, dv = states[n]
            ckv_st[r, :] = ckv
            kpe_st[r, :] = pe
            for hd in range(DIFF_HEADS):
                _store_head_rows(dk_st, n, hd, DIFF_HEADS, dk[:, LANES * hd: LANES * (hd + 1)])
                _store_head_rows(dv_st, n, hd, DIFF_HEADS, dv[:, LANES * hd: LANES * (hd + 1)])


def _premix1_kernel(x_ref, mod_ref, g_ref, win_ref, c64_ref, sa64_ref, sb64_ref,
                    sq_ref, sk_ref, sv_ref, nq_ref, nk_ref, nv_ref,
                    sk_st, sv_st, nk_st, nv_st):
    i = pl.program_id(0)
    m = mod_ref[0]
    gain = g_ref[...] * (1.0 + m[1:2])

    def project(n):
        h = _rms(x_ref[_sub_rows(n), :], gain) + m[0:1]
        return _dot(h.astype(BF16), win_ref[...])

    states = {}

    def finish(n, proj):
        r = _sub_rows(n)
        sq, sk, sv, nq, nk, nv = (proj[:, a:b] for a, b in zip(_CD_COLS[:-1], _CD_COLS[1:]))
        c64, sa64, sb64 = c64_ref[r, :], sa64_ref[r, :], sb64_ref[r, :]
        for hd in range(4):
            sl = slice(LANES * hd, LANES * (hd + 1))
            sq_ref[r, sl] = (_rope(sq[:, sl], c64, sa64, sb64, HEAD_DIM // 2) * _QSCALE).astype(BF16)
        sk_ref[r, :] = _rope(sk, c64, sa64, sb64, HEAD_DIM // 2).astype(BF16)
        sv_ref[r, :] = sv.astype(BF16)
        nq_ref[r, :] = (nq * _QSCALE).astype(BF16)
        nk_ref[r, :] = nk.astype(BF16)
        nv_ref[r, :] = nv.astype(BF16)
        states[n] = (sk, sv, nk, nv)

    _pipelined_units([(project, finish, n) for n in range(N_SUB)])

    @pl.when(i < NPT)
    def _():
        for n in range(N_SUB):
            r = _sub_rows(n)
            sk, sv, nk, nv = states[n]
            sk_st[r, :] = sk
            sv_st[r, :] = sv
            for hd in range(NA_HEADS):
                _store_head_rows(nk_st, n, hd, NA_HEADS, nk[:, HEAD_DIM * hd: HEAD_DIM * (hd + 1)])
                _store_head_rows(nv_st, n, hd, NA_HEADS, nv[:, HEAD_DIM * hd: HEAD_DIM * (hd + 1)])


def _tok_spec(width):
    return pl.BlockSpec((TM, width), lambda i: (i, 0))


_PROMPT_SPEC = pl.BlockSpec((TM, D_MODEL), lambda i: (jnp.minimum(i, NPT - 1), 0))
_SAMPLE_SPEC = pl.BlockSpec((TM, D_MODEL), lambda i: (jnp.maximum(i - NPT, 0), 0))


def _state_spec(width, rows_per_token=1):
    return pl.BlockSpec((TM * rows_per_token, width), lambda i: (jnp.minimum(i, NPT - 1), 0))


def _const_spec(shape):
    return pl.BlockSpec(shape, lambda i: (0,) * len(shape))


_MOD_SPEC = pl.BlockSpec((1, 6, D_MODEL), lambda i: (_mod_row(i), 0, 0))
_ROPE_SPEC = pl.BlockSpec((TM, LANES), lambda i: (_rope_blk(i), 0))


def _premix0(xp, xs, mod_l, g0, w_in_p, q_norm, w_uq_p, kv_norm, w_k_p, w_v, t32, t64):
    outs = [(N_TOK, 1024, BF16), (N_TOK, 1024, BF16), (N_TOK, 512, BF16), (N_TOK, 512, BF16),
            (N_TOK, 512, BF16), (N_TOK, 512, BF16),
            (NP_TOK, 256, F32), (NP_TOK, 128, F32),
            (NP_TOK * DIFF_HEADS, 2 * DIFF_DH, F32), (NP_TOK * DIFF_HEADS, 2 * DIFF_DH, F32)]
    return pl.pallas_call(
        _premix0_kernel,
        grid=(N_TILES,),
        in_specs=[_PROMPT_SPEC, _SAMPLE_SPEC, _MOD_SPEC, _const_spec((1, D_MODEL)),
                  _const_spec((D_MODEL, IN_COLS)), _const_spec((1, MLA_Q_RANK)),
                  _const_spec((MLA_Q_RANK, 1024)), _const_spec((1, MLA_KV_RANK)),
                  _const_spec((MLA_KV_RANK, 1024)), _const_spec((MLA_KV_RANK, 512))]
                 + [_ROPE_SPEC] * 6,
        out_specs=([_tok_spec(w) for (_, w, _) in outs[:6]]
                   + [_state_spec(w, n // NP_TOK) for (n, w, _) in outs[6:]]),
        out_shape=[jax.ShapeDtypeStruct((n, w), dt) for (n, w, dt) in outs],
        compiler_params=_cparams(1),
        name="premix_ab",
    )(xp, xs, mod_l, g0, w_in_p, q_norm, w_uq_p, kv_norm, w_k_p, w_v, *t32, *t64)


def _premix1(x, mod_l, g0, w_in_p, t64):
    outs = [(N_TOK, 512, BF16), (N_TOK, 128, BF16), (N_TOK, 128, BF16), (N_TOK, 512, BF16),
            (N_TOK, 512, BF16), (N_TOK, 512, BF16),
            (NP_TOK, 128, F32), (NP_TOK, 128, F32),
            (NP_TOK * NA_HEADS, HEAD_DIM, F32), (NP_TOK * NA_HEADS, HEAD_DIM, F32)]
    return pl.pallas_call(
        _premix1_kernel,
        grid=(N_TILES,),
        in_specs=[_tok_spec(D_MODEL), _MOD_SPEC, _const_spec((1, D_MODEL)),
                  _const_spec((D_MODEL, IN_COLS))] + [_ROPE_SPEC] * 3,
        out_specs=([_tok_spec(w) for (_, w, _) in outs[:6]]
                   + [_state_spec(w, n // NP_TOK) for (n, w, _) in outs[6:]]),
        out_shape=[jax.ShapeDtypeStruct((n, w), dt) for (n, w, dt) in outs],
        compiler_params=_cparams(1),
        name="premix_cd",
    )(x, mod_l, g0, w_in_p, *t64)


def _mla_cache_kernel(ckv_ref, pe_ref, wk_ref, wv_ref, k_ref, v_ref):
    c = ckv_ref[...].astype(BF16)
    kn = _dot(c, wk_ref[...])
    v_ref[...] = _dot(c, wv_ref[...]).astype(BF16)
    pe = pe_ref[...]
    for hd in range(MLA_HEADS):
        sl = slice(LANES * hd, LANES * (hd + 1))
        k_ref[:, sl] = (kn[:, sl] + pe).astype(BF16)


def _mla_cache(ckv, pe_slab, w_k_p, w_v):
    n = ckv.shape[0]
    tm = 512
    return pl.pallas_call(
        _mla_cache_kernel,
        grid=(n // tm,),
        in_specs=[pl.BlockSpec((tm, MLA_KV_RANK), lambda i: (i, 0)),
                  pl.BlockSpec((tm, LANES), lambda i: (i, 0)),
                  _const_spec((MLA_KV_RANK, 1024)), _const_spec((MLA_KV_RANK, 512))],
        out_specs=[pl.BlockSpec((tm, 1024), lambda i: (i, 0)), pl.BlockSpec((tm, 512), lambda i: (i, 0))],
        out_shape=[jax.ShapeDtypeStruct((n, 1024), BF16), jax.ShapeDtypeStruct((n, 512), BF16)],
        compiler_params=_cparams(1),
        name="mla_cache",
    )(ckv, pe_slab, w_k_p, w_v)


def _softmax_pv(scores, values, sink=None):
    m = jnp.max(scores[0], axis=-1, keepdims=True)
    for s in scores[1:]:
        m = jnp.maximum(m, jnp.max(s, axis=-1, keepdims=True))
    if sink is not None:
        m = jnp.maximum(m, sink)
    l = None
    o = None
    for s, v in zip(scores, values):
        p = jnp.exp2(s - m)
        ls = jnp.sum(p, axis=-1, keepdims=True)
        os_ = _dot(p.astype(BF16), v)
        l = ls if l is None else l + ls
        o = os_ if o is None else o + os_
    if sink is not None:
        l = l + jnp.exp2(sink - m)
    return o * (1.0 / l)


def _lane_lo(shape):
    return lax.broadcasted_iota(jnp.int32, shape, 1) < (LANES // 2)


def _split_halves(qb):
    lo = _lane_lo(qb.shape)
    zero = jnp.zeros_like(qb)
    return jnp.where(lo, qb, zero), jnp.where(lo, zero, qb)


def _attn_ab_kernel(*refs, n_pieces, n_seq, lam_init, aliased):
    if aliased:
        refs = refs[1:]
    q_ref, dq_ref = refs[0], refs[1]
    pieces = [refs[2 + 4 * p: 6 + 4 * p] for p in range(n_pieces)]
    lam_ref, subg_ref, o_ref = refs[2 + 4 * n_pieces:]
    lam = lam_ref[...]
    lam_full = (jnp.exp(jnp.sum(lam[0:1] * lam[1:2], axis=-1, keepdims=True))
                - jnp.exp(jnp.sum(lam[2:3] * lam[3:4], axis=-1, keepdims=True)) + lam_init)
    tq = q_ref.shape[0] // n_seq
    lo = _lane_lo((tq, LANES))
    subg = subg_ref[...] * (1.0 - lam_init)

    def seq_rows(ref, sq):
        n = ref.shape[0] // n_seq
        return slice(n * sq, n * (sq + 1))

    def mla_scores(arg):
        sq, hd = arg
        sl = slice(LANES * hd, LANES * (hd + 1))
        qh = q_ref[seq_rows(q_ref, sq), sl]
        return [_dot_nt(qh, k_ref[seq_rows(k_ref, sq), sl]) for (k_ref, _, _, _) in pieces]

    def diff_scores(arg):
        sq, hd = arg
        sl = slice(LANES * hd, LANES * (hd + 1))
        qq = jnp.concatenate(_split_halves(dq_ref[seq_rows(dq_ref, sq), sl]), axis=0)
        return [_dot_nt(qq, dk_ref[seq_rows(dk_ref, sq), sl]) for (_, _, dk_ref, _) in pieces]

    pair = {}

    def mla_finish(arg, scores):
        sq, hd = arg
        j = hd // 2
        vals = [v_ref[seq_rows(v_ref, sq), LANES * j: LANES * (j + 1)] for (_, v_ref, _, _) in pieces]
        pair[hd % 2] = _softmax_pv(scores, vals)
        if hd % 2 == 1:
            o_ref[seq_rows(o_ref, sq), LANES * j: LANES * (j + 1)] = jnp.where(lo, pair[0], pair[1]).astype(BF16)

    def diff_finish(arg, scores):
        sq, hd = arg
        sl = slice(LANES * hd, LANES * (hd + 1))
        oo = _softmax_pv(scores, [dv_ref[seq_rows(dv_ref, sq), sl] for (_, _, _, dv_ref) in pieces])
        od = _rms(oo[:tq] - lam_full * oo[tq:], subg)
        o_ref[seq_rows(o_ref, sq), 512 + LANES * hd: 512 + LANES * (hd + 1)] = od.astype(BF16)

    units = []
    for sq in range(n_seq):
        for j in range(DIFF_HEADS):
            units += [(diff_scores, diff_finish, (sq, j)), (mla_scores, mla_finish, (sq, 2 * j)),
                      (mla_scores, mla_finish, (sq, 2 * j + 1))]
    _pipelined_units(units)


def _attn_ab(q, dq, new_kv, cache_kv, lam, sub_g, lam_init, *, n_batch, t_len, tq, tok_off, out_init=None,
             n_seq=1):
    assert n_seq == 1 or (tq == t_len and cache_kv is None and n_batch % n_seq == 0)
    n_batch, t_len, tq = n_batch // n_seq, t_len * n_seq, tq * n_seq
    nq = t_len // tq
    q_off = tok_off // tq
    b_off = tok_off // t_len
    widths = (1024, 512, 512, 512)
    in_specs = [pl.BlockSpec((tq, 1024), lambda b, i: (q_off + b * nq + i, 0)),
                pl.BlockSpec((tq, 512), lambda b, i: (q_off + b * nq + i, 0))]
    args = [q, dq]
    for w, a in zip(widths, new_kv):
        in_specs.append(pl.BlockSpec((t_len, w), lambda b, i: (b_off + b, 0)))
        args.append(a)
    n_pieces = 1
    if cache_kv is not None:
        n_pieces = 2
        for w, a in zip(widths, cache_kv):
            in_specs.append(pl.BlockSpec((PAST_LEN, w), lambda b, i: (b, 0)))
            args.append(a)
    in_specs += [pl.BlockSpec((4, DIFF_DH), lambda b, i: (0, 0)),
                 pl.BlockSpec((1, 2 * DIFF_DH), lambda b, i: (0, 0))]
    args += [lam, sub_g]
    aliases = {}
    if out_init is not None:
        in_specs = [pl.BlockSpec(memory_space=pl.ANY)] + in_specs
        args = [out_init] + args
        aliases = {0: 0}
    return pl.pallas_call(
        functools.partial(_attn_ab_kernel, n_pieces=n_pieces, n_seq=n_seq, lam_init=lam_init,
                          aliased=out_init is not None),
        grid=(n_batch, nq),
        in_specs=in_specs,
        out_specs=pl.BlockSpec((tq, 1024), lambda b, i: (q_off + b * nq + i, 0)),
        out_shape=jax.ShapeDtypeStruct((N_TOK, 1024), BF16),
        input_output_aliases=aliases,
        compiler_params=_cparams(2),
        name="attn_ab_%d" % n_pieces,
    )(*args)


def _gqa_stacks(sq_ref, sink_ref):
    tq = sq_ref.shape[0]
    halves = [_split_halves(sq_ref[:, LANES * j: LANES * (j + 1)]) for j in range(4)]
    q_stacks = [jnp.concatenate([halves[j][kvh] for j in range(4)], axis=0) for kvh in range(SWA_KV_HEADS)]
    sinks = [jnp.concatenate([jnp.full((tq, 1), sink_ref[4 * kvh + j] * LOG2E, F32) for j in range(4)], axis=0)
             for kvh in range(SWA_KV_HEADS)]
    return q_stacks, sinks


def _attn_cd_prompt_kernel(sink_ref, sq_ref, sk_ref, sv_ref, nq_ref, nk_ref, nv_ref, o_ref):
    lo = _lane_lo((SEQ, LANES))
    for sq in range(PROMPT_SEQS_PER_STEP):
        r = slice(SEQ * sq, SEQ * (sq + 1))
        sk = sk_ref[r, :]
        sv = sv_ref[r, :]
        for j in range(4):
            sl = slice(LANES * j, LANES * (j + 1))
            q_lo, q_hi = _split_halves(sq_ref[r, sl])
            o_lo = _softmax_pv([_dot_nt(q_lo, sk)], [sv], sink=sink_ref[j] * LOG2E)
            o_hi = _softmax_pv([_dot_nt(q_hi, sk)], [sv], sink=sink_ref[j + 4] * LOG2E)
            o_ref[r, sl] = jnp.where(lo, o_lo, o_hi).astype(BF16)
        for j in range(4):
            sl = slice(LANES * j, LANES * (j + 1))
            q_lo, q_hi = _split_halves(nq_ref[r, sl])
            k = nk_ref[r, sl]
            v = nv_ref[r, sl]
            o_lo = _softmax_pv([_dot_nt(q_lo, k)], [v])
            o_hi = _softmax_pv([_dot_nt(q_hi, k)], [v])
            o_ref[r, 512 + LANES * j: 512 + LANES * (j + 1)] = jnp.where(lo, o_lo, o_hi).astype(BF16)


def _attn_cd_prompt(sink, sq, sk, sv, nq, nk, nv):
    def spec(w):
        return pl.BlockSpec((SEQ * PROMPT_SEQS_PER_STEP, w), lambda b: (b, 0))
    return pl.pallas_call(
        _attn_cd_prompt_kernel,
        grid=(BATCH // PROMPT_SEQS_PER_STEP,),
        in_specs=[pl.BlockSpec(memory_space=pltpu.SMEM), spec(512), spec(128), spec(128),
                  spec(512), spec(512), spec(512)],
        out_specs=spec(1024),
        out_shape=jax.ShapeDtypeStruct((N_TOK, 1024), BF16),
        compiler_params=_cparams(1),
        name="attn_cd_prompt",
    )(sink, sq, sk, sv, nq, nk, nv)


_SWA_KEYS = TQ + 2 * SWA_WINDOW


def _attn_cd_sample_kernel(init_ref, sink_ref, sq_ref, nq_ref, sk_ref, sv_ref, nk_ref, nv_ref,
                           skc_ref, svc_ref, nkc_ref, nvc_ref, bias_ref, o_ref):
    del init_ref
    qi = pl.program_id(1)
    lo = _lane_lo((TQ, LANES))
    ks = pl.multiple_of(jnp.clip(qi * TQ - SWA_WINDOW, 0, DEC_SEQ - _SWA_KEYS), SWA_WINDOW)
    k_win = sk_ref[pl.ds(ks, _SWA_KEYS), :]
    v_win = sv_ref[pl.ds(ks, _SWA_KEYS), :]
    q_pos = qi * TQ + (lax.broadcasted_iota(jnp.int32, (4 * TQ, _SWA_KEYS), 0) & (TQ - 1))
    k_pos = ks + lax.broadcasted_iota(jnp.int32, (4 * TQ, _SWA_KEYS), 1)
    in_win = jnp.abs(q_pos - k_pos) <= SWA_WINDOW
    skc = skc_ref[...]
    svc = svc_ref[...]
    q_stacks, sinks = _gqa_stacks(sq_ref, sink_ref)

    def window_scores(kvh):
        return [_dot_nt(q_stacks[kvh], skc), jnp.where(in_win, _dot_nt(q_stacks[kvh], k_win), NEG)]

    o_kv = {}

    def window_finish(kvh, scores):
        o_kv[kvh] = _softmax_pv(scores, [svc, v_win], sink=sinks[kvh])
        if kvh == SWA_KV_HEADS - 1:
            for j in range(4):
                rows = slice(TQ * j, TQ * (j + 1))
                o_ref[:, LANES * j: LANES * (j + 1)] = jnp.where(lo, o_kv[0][rows], o_kv[1][rows]).astype(BF16)

    n_rows = DEC_SEQ // GRID_W
    r0 = jnp.clip(qi * NA_TILE_ROWS - NA_WIN_ROWS // 2, 0, n_rows - NA_KEY_ROWS)
    kn = pl.multiple_of(r0 * GRID_W, GRID_W)

    lane_lo = _lane_lo((1, LANES))
    pieces = []
    for ri in range(NA_TILE_ROWS):
        r = qi * NA_TILE_ROWS + ri
        rs = jnp.clip(r - NA_WIN_ROWS // 2, 0, n_rows - NA_WIN_ROWS)
        row = []
        for mm in range(NA_KEY_ROWS // 2):
            kr = r0 + 2 * mm
            ok = [(kr + t >= rs) & (kr + t < rs + NA_WIN_ROWS) for t in range(2)]
            mask = jnp.where(lane_lo, jnp.where(ok[0], 0.0, NEG), jnp.where(ok[1], 0.0, NEG))
            row.append((jnp.clip(kr - r + NA_WIN_ROWS, 0, 2 * NA_WIN_ROWS - 1), mask))
        pieces.append(row)

    def add_na_bias(hd, s):
        return jnp.concatenate(
            [jnp.concatenate([s[GRID_W * ri: GRID_W * (ri + 1), LANES * mm: LANES * (mm + 1)] + bias_ref[hd, d] + mask
                              for mm, (d, mask) in enumerate(row)], axis=1)
             for ri, row in enumerate(pieces)], axis=0)

    def na_scores(hd):
        sl = slice(LANES * (hd // 2), LANES * (hd // 2 + 1))
        q_half = _split_halves(nq_ref[:, sl])[hd % 2]
        nk_win = nk_ref[pl.ds(kn, NA_KEY_ROWS * GRID_W), sl]
        return [_dot_nt(q_half, nkc_ref[:, sl]), add_na_bias(hd, _dot_nt(q_half, nk_win))]

    pair = {}

    def na_finish(hd, scores):
        j = hd // 2
        sl = slice(LANES * j, LANES * (j + 1))
        nv_win = nv_ref[pl.ds(kn, NA_KEY_ROWS * GRID_W), sl]
        pair[hd % 2] = _softmax_pv(scores, [nvc_ref[:, sl], nv_win])
        if hd % 2 == 1:
            o_ref[:, 512 + LANES * j: 512 + LANES * (j + 1)] = jnp.where(lo, pair[0], pair[1]).astype(BF16)

    _pipelined_units([(window_scores, window_finish, kvh) for kvh in range(SWA_KV_HEADS)]
                     + [(na_scores, na_finish, hd) for hd in range(NA_HEADS)])


def _attn_cd_sample(out_init, sink, sq, sk, sv, nq, nk, nv, skc, svc, nkc, nvc, bias):
    nq_t = DEC_SEQ // TQ
    q_off = NP_TOK // TQ
    b_off = NP_TOK // DEC_SEQ

    def qspec(w):
        return pl.BlockSpec((TQ, w), lambda b, i: (q_off + b * nq_t + i, 0))

    def kspec(w):
        return pl.BlockSpec((DEC_SEQ, w), lambda b, i: (b_off + b, 0))

    def cspec(w):
        return pl.BlockSpec((PAST_LEN, w), lambda b, i: (b, 0))

    bias_spec = pl.BlockSpec((NA_HEADS, 2 * NA_WIN_ROWS, GRID_W, LANES), lambda b, i: (0, 0, 0, 0))
    return pl.pallas_call(
        _attn_cd_sample_kernel,
        grid=(DEC_BATCH, nq_t),
        in_specs=[pl.BlockSpec(memory_space=pl.ANY), pl.BlockSpec(memory_space=pltpu.SMEM),
                  qspec(512), qspec(512), kspec(128), kspec(128), kspec(512), kspec(512),
                  cspec(128), cspec(128), cspec(512), cspec(512), bias_spec],
        out_specs=pl.BlockSpec((TQ, 1024), lambda b, i: (q_off + b * nq_t + i, 0)),
        out_shape=jax.ShapeDtypeStruct((N_TOK, 1024), BF16),
        input_output_aliases={0: 0},
        compiler_params=_cparams(2),
        name="attn_cd_sample",
    )(out_init, sink, sq, nq, sk, sv, nk, nv, skc, svc, nkc, nvc, bias)


def _na_bias(rpb):
    n_dc = 2 * NA_WIN_COLS - 1
    c = np.arange(GRID_W)[:, None]
    kc = np.arange(GRID_W)[None, :]
    qs = np.clip(c - NA_WIN_COLS // 2, 0, GRID_W - NA_WIN_COLS)
    col_ok = (kc >= qs) & (kc < qs + NA_WIN_COLS)
    dc = np.clip(kc - c + NA_WIN_COLS - 1, 0, n_dc - 1)
    onehot = ((dc[None] == np.arange(n_dc)[:, None, None]) & col_ok[None]).astype(np.float32)
    blocks = jnp.einsum('hrd,dck->hrck', rpb.astype(F32) * LOG2E, onehot, precision=lax.Precision.HIGHEST)
    blocks = jnp.where(col_ok[None, None], blocks, NEG)
    none = jnp.full((NA_HEADS, 1, GRID_W, GRID_W), NEG, F32)
    return jnp.concatenate([jnp.concatenate([none, blocks], axis=1),
                            jnp.concatenate([blocks, none], axis=1)], axis=-1)


_HI_MASK = -65536


def _pack_pairs(x):
    w = x.shape[1] // 2
    r = x.astype(BF16).astype(F32)
    lo = lax.bitcast_convert_type(r[:, :w], jnp.int32)
    hi = lax.bitcast_convert_type(r[:, w:], jnp.int32)
    return (hi & _HI_MASK) | lax.shift_right_logical(lo, 16)


def _unpack_pairs(p):
    lo = lax.bitcast_convert_type(lax.shift_left(p, 16), F32)
    hi = lax.bitcast_convert_type(p & _HI_MASK, F32)
    return lo, hi


def _postmix_kernel(*refs, split_x):
    if split_x:
        o_ref, xp_ref, xs_ref = refs[:3]
        refs = refs[3:]
    else:
        o_ref, x_ref = refs[:2]
        refs = refs[2:]
    (mod_ref, g_ref, wout_ref, wr_ref, br_ref, tri_ref,
     x1_ref, h2_ref, route_ref, gate_ref, cnt_ref, run_ref) = refs
    i = pl.program_id(0)

    @pl.when(i == 0)
    def _():
        run_ref[...] = jnp.zeros_like(run_ref)

    m = mod_ref[0]
    g = g_ref[...]
    gate_gain = g[1:2] * m[2:3]
    ffn_gain = g[2:3] * (1.0 + m[4:5])
    sub_logits = {}

    def project(n):
        return _dot(o_ref[_sub_rows(n), :], wout_ref[...])

    def finish(n, y):
        r = _sub_rows(n)
        x = jnp.where(i < NPT, xp_ref[r, :], xs_ref[r, :]) if split_x else x_ref[r, :]
        x1 = x + _rms(y, gate_gain)
        x1_ref[r, :] = x1
        h2 = _rms(x1, ffn_gain) + m[3:4]
        h2_ref[r, :] = _pack_pairs(h2)
        sub_logits[n] = _dot(h2.astype(BF16), wr_ref[...]) + br_ref[...]

    _pipelined_units([(project, finish, n) for n in range(N_SUB)])

    logits = jnp.concatenate([sub_logits[n] for n in range(N_SUB)], axis=0)
    lane = lax.broadcasted_iota(jnp.int32, logits.shape, 1).astype(F32)
    cur = jnp.where(lane < N_EXPERTS, logits, -jnp.inf)
    tops, idxs = [], []
    for _ in range(TOP_K):
        mx = jnp.max(cur, axis=-1, keepdims=True)
        ix = jnp.min(jnp.where(cur == mx, lane, float(LANES)), axis=-1, keepdims=True)
        tops.append(mx)
        idxs.append(ix)
        cur = jnp.where(lane == ix, -jnp.inf, cur)
    es = [jnp.exp(t - tops[0]) for t in tops]
    inv = 1.0 / (es[0] + es[1] + es[2] + es[3])
    picked = jnp.zeros_like(logits)
    for k in range(TOP_K):
        picked = jnp.where(lane == idxs[k], 1.0, picked)
    before = _dot(tri_ref[...], picked.astype(BF16)) + run_ref[0:1, :]
    route = jnp.zeros_like(logits)
    gate_out = jnp.zeros_like(logits)
    for k in range(TOP_K):
        rank = jnp.sum(jnp.where(lane == idxs[k], before, 0.0), axis=-1, keepdims=True)
        route = jnp.where(lane == float(k), idxs[k], route)
        route = jnp.where(lane == float(TOP_K + k), rank, route)
        gate_out = jnp.where(lane == float(k), es[k] * inv, gate_out)
    route_ref[...] = route.T[:2 * TOP_K].astype(jnp.int32)
    gate_ref[...] = gate_out
    run_ref[...] = run_ref[...] + jnp.sum(picked, axis=0, keepdims=True)
    cnt_ref[...] = run_ref[...].astype(jnp.int32)


def _postmix(o_cat, xs, mod_l, g_l, w_out, w_r, b_r):
    tri = jnp.asarray(np.tril(np.ones((TM, TM), np.float32), -1), BF16)
    split_x = len(xs) == 2
    x_specs = [_PROMPT_SPEC, _SAMPLE_SPEC] if split_x else [_tok_spec(D_MODEL)]
    return pl.pallas_call(
        functools.partial(_postmix_kernel, split_x=split_x),
        grid=(N_TILES,),
        in_specs=[_tok_spec(1024)] + x_specs + [_MOD_SPEC, _const_spec((4, D_MODEL)),
                  _const_spec((1024, D_MODEL)), _const_spec((D_MODEL, LANES)), _const_spec((1, LANES)),
                  _const_spec((TM, TM))],
        out_specs=[_tok_spec(D_MODEL), _tok_spec(D_MODEL // 2), pl.BlockSpec((2 * TOP_K, TM), lambda i: (0, i)),
                   _tok_spec(LANES), _const_spec((8, LANES))],
        out_shape=[jax.ShapeDtypeStruct((N_TOK, D_MODEL), F32),
                   jax.ShapeDtypeStruct((N_TOK, D_MODEL // 2), jnp.int32),
                   jax.ShapeDtypeStruct((2 * TOP_K, N_TOK), jnp.int32), jax.ShapeDtypeStruct((N_TOK, LANES), F32),
                   jax.ShapeDtypeStruct((8, LANES), jnp.int32)],
        scratch_shapes=[pltpu.VMEM((8, LANES), F32)],
        compiler_params=_cparams(1),
        name="postmix",
    )(o_cat, *xs, mod_l, g_l, w_out, w_r, b_r, tri)


SC_WORKERS = 32
SC_ROWS = 64
ROW_WORDS = D_MODEL // 2

_SC_SCRATCH = [pltpu.VMEM((SC_ROWS, ROW_WORDS), jnp.int32), pltpu.VMEM((SC_ROWS, ROW_WORDS), jnp.int32),
               pltpu.SemaphoreType.DMA, pltpu.SemaphoreType.DMA, pltpu.SemaphoreType.DMA, pltpu.SemaphoreType.DMA]


def _sc_worker_id():
    return lax.axis_index("s") * 2 + lax.axis_index("c")


def _sc_double_buffered(n_chunks, load, store):
    for cp in load(0, 0):
        cp.start()

    @pl.loop(0, n_chunks, step=2)
    def _(g0):
        for b in range(2):
            g = g0 + b
            for cp in load(g, b):
                cp.wait()

            @pl.when(g >= 1)
            def _():
                for cp in store(g - 1, 1 - b):
                    cp.wait()

            @pl.when(g + 1 < n_chunks)
            def _():
                for cp in load(g + 1, 1 - b):
                    cp.start()

            for cp in store(g, b):
                cp.start()

    for cp in store(n_chunks - 1, (n_chunks - 1) % 2):
        cp.wait()


def _sc_dispatch(src, idx):
    n_chunks = N_TOK // (SC_WORKERS * SC_ROWS)
    assert n_chunks % 2 == 0
    mesh = plsc.VectorSubcoreMesh(core_axis_name="c", subcore_axis_name="s")

    @functools.partial(
        pl.kernel, mesh=mesh,
        out_type=jax.ShapeDtypeStruct((MOE_ROWS, ROW_WORDS), jnp.int32),
        scratch_types=[pltpu.VMEM((n_chunks * TOP_K, SC_ROWS), jnp.int32)] + _SC_SCRATCH)
    def k(src_hbm, idx_hbm, out_hbm, idx_v, buf0, buf1, in0, in1, out0, out1):
        wid = _sc_worker_id()
        pltpu.sync_copy(idx_hbm.at[wid], idx_v)
        bufs, in_sems, out_sems = (buf0, buf1), (in0, in1), (out0, out1)

        def load(g, b):
            rows = pl.ds((wid * n_chunks + g) * SC_ROWS, SC_ROWS)
            return [pltpu.make_async_copy(src_hbm.at[rows], bufs[b], in_sems[b])]

        def store(g, b):
            return [pltpu.make_async_copy(bufs[b], out_hbm.at[idx_v.at[g * TOP_K + kk]], out_sems[b])
                    for kk in range(TOP_K)]

        _sc_double_buffered(n_chunks, load, store)

    return k(src, idx)


def _sc_collect(table, idx):
    n_chunks = idx.shape[1]
    assert n_chunks % 2 == 0
    mesh = plsc.VectorSubcoreMesh(core_axis_name="c", subcore_axis_name="s")

    @functools.partial(
        pl.kernel, mesh=mesh,
        out_type=jax.ShapeDtypeStruct((SC_WORKERS * n_chunks * SC_ROWS, ROW_WORDS), jnp.int32),
        scratch_types=[pltpu.VMEM((n_chunks, SC_ROWS), jnp.int32)] + _SC_SCRATCH)
    def k(table_hbm, idx_hbm, out_hbm, idx_v, buf0, buf1, in0, in1, out0, out1):
        wid = _sc_worker_id()
        pltpu.sync_copy(idx_hbm.at[wid], idx_v)
        bufs, in_sems, out_sems = (buf0, buf1), (in0, in1), (out0, out1)

        def load(g, b):
            return [pltpu.make_async_copy(table_hbm.at[idx_v.at[g]], bufs[b], in_sems[b])]

        def store(g, b):
            rows = pl.ds((wid * n_chunks + g) * SC_ROWS, SC_ROWS)
            return [pltpu.make_async_copy(bufs[b], out_hbm.at[rows], out_sems[b])]

        _sc_double_buffered(n_chunks, load, store)

    return k(table, idx)


def _expert_rows(words, n_valid, gu_weights, down_weights, bgu, bd):
    live = lax.broadcasted_iota(jnp.int32, words.shape, 0) < n_valid
    lo, hi = _unpack_pairs(jnp.where(live, words, 0))
    x = jnp.concatenate([lo, hi], axis=1).astype(BF16)
    gu = _dot(x, gu_weights()) + bgu
    w_down = down_weights()
    g = jnp.minimum(gu[:, :D_EXPERT], SWIGLU_LIMIT)
    u = jnp.clip(gu[:, D_EXPERT:], -SWIGLU_LIMIT, SWIGLU_LIMIT)
    a = g * jax.nn.sigmoid(SWIGLU_ALPHA * g) * (u + 1.0)
    return _pack_pairs(_dot(a.astype(BF16), w_down) + bd)


def _moe_kernel(blk_e_ref, blk_first_ref, blk_rows_ref, blk_slot_ref, blk_next_ref, blk_io_ref,
                x_ref, wgu_hbm, bgu_ref, wd_hbm, bd_ref, y_ref,
                wgu_f, wd_f, wgu_b, wd_b, sem, *, layer):
    del blk_io_ref
    i = pl.program_id(0)
    n_valid = blk_rows_ref[i]
    quantum = MOE_TM // MOE_TAIL_PARTS

    def weight_copies(e, slot):
        return (pltpu.make_async_copy(wgu_hbm.at[layer, e], wgu_f.at[slot], sem.at[0, slot]),
                pltpu.make_async_copy(wd_hbm.at[layer, e], wd_f.at[slot], sem.at[1, slot]))

    @pl.when(i == 0)
    def _():
        for cp in weight_copies(blk_e_ref[0], blk_slot_ref[0]):
            cp.start()

    first = blk_first_ref[i] == 1

    @pl.when(first)
    def _():
        slot = blk_slot_ref[i]
        for cp in weight_copies(blk_e_ref[i], slot):
            cp.wait()
        nxt = blk_next_ref[i]

        @pl.when(nxt >= 0)
        def _():
            for cp in weight_copies(nxt, 1 - slot):
                cp.start()

        def gu_weights():
            w = wgu_f[slot].astype(BF16)
            wgu_b[...] = w
            return w

        def down_weights():
            w = wd_f[slot].astype(BF16)
            wd_b[...] = w
            return w

        y_ref[...] = _expert_rows(x_ref[...], n_valid, gu_weights, down_weights, bgu_ref[0, 0], bd_ref[0, 0])

    for parts in range(1, MOE_TAIL_PARTS + 1):
        rows = parts * quantum

        @pl.when(jnp.logical_not(first) & (n_valid > rows - quantum) & (n_valid <= rows))
        def _(rows=rows):
            y_ref[:rows] = _expert_rows(x_ref[:rows], n_valid, lambda: wgu_b[...], lambda: wd_b[...],
                                        bgu_ref[0, 0], bd_ref[0, 0])
            if rows < MOE_TM:
                y_ref[rows:] = jnp.zeros((MOE_TM - rows, ROW_WORDS), jnp.int32)


def _moe(layer, blk_meta, xs, w_gu, b_gu, w_down, b_down):
    def row_map(i, e, first, rows, slot, nxt, io):
        return (io[i], 0)

    def bias_map(i, e, *_):
        return (layer, e[i], 0, 0)

    grid_spec = pltpu.PrefetchScalarGridSpec(
        num_scalar_prefetch=6,
        grid=(MOE_BLOCKS,),
        in_specs=[
            pl.BlockSpec((MOE_TM, ROW_WORDS), row_map),
            pl.BlockSpec(memory_space=pl.ANY),
            pl.BlockSpec((1, 1, 1, 2 * D_EXPERT), bias_map),
            pl.BlockSpec(memory_space=pl.ANY),
            pl.BlockSpec((1, 1, 1, D_MODEL), bias_map),
        ],
        out_specs=pl.BlockSpec((MOE_TM, ROW_WORDS), row_map),
        scratch_shapes=[pltpu.VMEM((2, D_MODEL, 2 * D_EXPERT), F32), pltpu.VMEM((2, D_EXPERT, D_MODEL), F32),
                        pltpu.VMEM((D_MODEL, 2 * D_EXPERT), BF16), pltpu.VMEM((D_EXPERT, D_MODEL), BF16),
                        pltpu.SemaphoreType.DMA((2, 2))],
    )
    return pl.pallas_call(
        functools.partial(_moe_kernel, layer=layer),
        grid_spec=grid_spec,
        out_shape=jax.ShapeDtypeStruct((MOE_ROWS, ROW_WORDS), jnp.int32),
        compiler_params=_cparams(1),
        name="moe_experts",
    )(*blk_meta, xs, w_gu, b_gu.reshape(DEPTH, N_EXPERTS, 1, 2 * D_EXPERT),
      w_down, b_down.reshape(DEPTH, N_EXPERTS, 1, D_MODEL))


def _route(route, counts):
    experts = jnp.arange(N_EXPERTS, dtype=jnp.int32)
    padded = (counts + MOE_TM - 1) // MOE_TM * MOE_TM
    pad_end = jnp.cumsum(padded)
    pad_start = pad_end - padded
    e = route[0:TOP_K]
    onehot = e[:, :, None] == experts[None, None, :]
    dest = jnp.sum(jnp.where(onehot, pad_start[None, None, :], 0), axis=-1) + route[TOP_K:2 * TOP_K]
    blk_row0 = jnp.arange(MOE_BLOCKS, dtype=jnp.int32) * MOE_TM
    blk_e = jnp.minimum(jnp.sum((pad_end[None, :] <= blk_row0[:, None]).astype(jnp.int32), axis=1),
                        N_EXPERTS - 1)
    on = blk_row0 < pad_end[-1]
    n_on = jnp.sum(on.astype(jnp.int32))
    blk_onehot = blk_e[:, None] == experts[None, :]
    row_end = jnp.sum(jnp.where(blk_onehot, (pad_start + counts)[None, :], 0), axis=1)
    blk_rows = jnp.where(on, jnp.clip(row_end - blk_row0, 0, MOE_TM), 0).astype(jnp.int32)
    last_e = jnp.sum(jnp.where(jnp.arange(MOE_BLOCKS) == n_on - 1, blk_e, 0))
    blk_e = jnp.where(on, blk_e, last_e).astype(jnp.int32)
    prev = jnp.concatenate([jnp.full((1,), -1, jnp.int32), blk_e[:-1]])
    blk_first = (blk_e != prev).astype(jnp.int32)
    blk_slot = ((jnp.cumsum(blk_first) - 1) % 2).astype(jnp.int32)
    later_used = (experts[None, :] > experts[:, None]) & (counts[None, :] > 0)
    next_used = jnp.min(jnp.where(later_used, experts[None, :], N_EXPERTS), axis=1)
    next_used = jnp.where(next_used == N_EXPERTS, -1, next_used)
    blk_next = jnp.sum(jnp.where(blk_e[:, None] == experts[None, :], next_used[None, :], 0), axis=1)
    blk_io = jnp.minimum(jnp.arange(MOE_BLOCKS, dtype=jnp.int32), n_on - 1)
    return dest, (blk_e, blk_first, blk_rows, blk_slot, blk_next.astype(jnp.int32), blk_io)


def _combine_kernel(y_ref, gate_ref, x1_ref, mod_ref, g_ref, *o_refs):
    i = pl.program_id(0)
    m = mod_ref[0]
    gate = gate_ref[...]
    acc_lo = None
    for k in range(TOP_K):
        lo, hi = _unpack_pairs(y_ref[k])
        gk = gate[:, k:k + 1]
        acc_lo = gk * lo if acc_lo is None else acc_lo + gk * lo
        acc_hi = gk * hi if k == 0 else acc_hi + gk * hi
    acc = jnp.concatenate([acc_lo, acc_hi], axis=1)
    out = x1_ref[...] + _rms(acc, g_ref[...][3:4] * m[5:6])
    if len(o_refs) == 1:
        o_refs[0][...] = out
    else:
        @pl.when(i < NPT)
        def _():
            o_refs[0][...] = out

        @pl.when(i >= NPT)
        def _():
            o_refs[1][...] = out


def _combine(yg, gates, x1, mod_l, g_l, split_out):
    if split_out:
        out_specs = [_PROMPT_SPEC, _SAMPLE_SPEC]
        out_shape = [jax.ShapeDtypeStruct((NP_TOK, D_MODEL), F32), jax.ShapeDtypeStruct((NS_TOK, D_MODEL), F32)]
    else:
        out_specs = [_tok_spec(D_MODEL)]
        out_shape = [jax.ShapeDtypeStruct((N_TOK, D_MODEL), F32)]
    return pl.pallas_call(
        _combine_kernel,
        grid=(N_TILES,),
        in_specs=[pl.BlockSpec((TOP_K, TM, ROW_WORDS), lambda i: (0, i, 0)), _tok_spec(LANES),
                  _tok_spec(D_MODEL), _MOD_SPEC, _const_spec((4, D_MODEL))],
        out_specs=out_specs,
        out_shape=out_shape,
        compiler_params=_cparams(1),
        name="combine",
    )(yg, gates, x1, mod_l, g_l)


def _ffn(layer, o_cat, xs, mod_l, g_l, w_out, w_router, b_router, w_gu, b_gu, w_down, b_down, split_out):
    w_r = jnp.pad(w_router, ((0, 0), (0, LANES - N_EXPERTS))).astype(BF16)
    b_r = jnp.pad(b_router, (0, LANES - N_EXPERTS)).reshape(1, LANES)
    x1, h2p, route, gate_slab, counts = _postmix(o_cat, xs, mod_l, g_l, w_out.astype(BF16), w_r, b_r)
    dest, blk_meta = _route(route, counts[0, :N_EXPERTS])
    n_chunks = N_TOK // (SC_WORKERS * SC_ROWS)
    idx_d = dest.reshape(TOP_K, SC_WORKERS, n_chunks, SC_ROWS).transpose(1, 2, 0, 3).reshape(
        SC_WORKERS, n_chunks * TOP_K, SC_ROWS)
    rows = _sc_dispatch(h2p, idx_d)
    ys = _moe(layer, blk_meta, rows, w_gu, b_gu, w_down, b_down)
    idx_c = dest.reshape(SC_WORKERS, TOP_K * n_chunks, SC_ROWS)
    yg = _sc_collect(ys, idx_c).reshape(TOP_K, N_TOK, ROW_WORDS)
    return _combine(yg, gate_slab, x1, mod_l, g_l, split_out)


def _pad_heads(w, n_heads, width, keep):
    k = w.shape[0]
    w = w.reshape(k, n_heads, width)[:, :, :keep]
    return jnp.pad(w, ((0, 0), (0, 0), (0, LANES - keep))).reshape(k, n_heads * LANES)


def _pe_slab(x):
    return jnp.pad(x, [(0, 0)] * (x.ndim - 1) + [(MLA_NOPE, LANES - MLA_NOPE - MLA_ROPE)])


def _pair_kv_heads(w):
    g = SWA_HEADS // SWA_KV_HEADS
    return w.reshape(SWA_KV_HEADS, g, HEAD_DIM, -1).transpose(1, 0, 2, 3).reshape(w.shape)


def kernel(x_prompt, x_sample, cache_mla_ckv, cache_mla_krope, cache_diff_k, cache_diff_v, cache_swa_k, cache_swa_v, cache_na_k, cache_na_v, c, c_ctx, w_mod, b_mod, norm_g, w_in0, mla_q_norm, w_uq, mla_kv_norm, w_ukv, diff_lambda, diff_norm, w_out0, w_in1, swa_sink, na_rpb, w_out1, w_router, b_router, w_gu, b_gu, w_down, b_down):
    xs = (x_prompt.reshape(NP_TOK, D_MODEL), x_sample.reshape(NS_TOK, D_MODEL))
    cond = jnp.concatenate([c_ctx[None, :], c, jnp.zeros((16 - 1 - DEC_BATCH, D_MODEL), F32)], axis=0)
    mod = _modulation(cond, w_mod, b_mod).reshape(DEPTH, 16, 6, D_MODEL)
    t64, t32 = _rope_tables()
    states = {}
    for l in range(DEPTH):
        i = l // 2
        g_l = norm_g[l]
        mod_l = mod[l]
        if l % 2 == 0:
            lam_init = 0.8 - 0.6 * math.exp(-0.3 * l)
            wi = w_in0[i]
            w_in_p = jnp.concatenate(
                [wi[:, 0:640], wi[:, 672:2208], _pe_slab(wi[:, 640:672])], axis=1).astype(BF16)
            w_uq_p = _pad_heads(w_uq[i], MLA_HEADS, MLA_NOPE + MLA_ROPE, MLA_NOPE + MLA_ROPE).astype(BF16)
            w_k_p = _pad_heads(w_ukv[i], MLA_HEADS, MLA_NOPE + MLA_V, MLA_NOPE).astype(BF16)
            w_v = w_ukv[i].reshape(MLA_KV_RANK, MLA_HEADS, MLA_NOPE + MLA_V)[:, :, MLA_NOPE:].reshape(
                MLA_KV_RANK, MLA_HEADS * MLA_V).astype(BF16)
            (q, k, v, dq, dk, dv, ckv_st, kpe_st, dk_st, dv_st) = _premix0(
                *xs, mod_l, g_l[0:1], w_in_p, mla_q_norm[i][None, :], w_uq_p, mla_kv_norm[i][None, :],
                w_k_p, w_v, t32, t64)
            states['mla_ckv'] = ckv_st.reshape(BATCH, 1, SEQ, MLA_KV_RANK)
            states['mla_krope'] = kpe_st[:, MLA_NOPE:MLA_NOPE + MLA_ROPE].reshape(BATCH, 1, SEQ, MLA_ROPE)
            states['diff_k'] = dk_st.reshape(BATCH, 1, SEQ, DIFF_HEADS, 2 * DIFF_DH)
            states['diff_v'] = dv_st.reshape(BATCH, 1, SEQ, DIFF_HEADS, 2 * DIFF_DH)
            kc, vc = _mla_cache(cache_mla_ckv[:, i].reshape(DEC_BATCH * PAST_LEN, MLA_KV_RANK),
                                _pe_slab(cache_mla_krope[:, i].reshape(DEC_BATCH * PAST_LEN, MLA_ROPE)),
                                w_k_p, w_v)
            dkc = cache_diff_k[:, i].reshape(DEC_BATCH * PAST_LEN, 512).astype(BF16)
            dvc = cache_diff_v[:, i].reshape(DEC_BATCH * PAST_LEN, 512).astype(BF16)
            lam = diff_lambda[i]
            sub_g = diff_norm[i][None, :]
            o_p = _attn_ab(q, dq, (k, v, dk, dv), None, lam, sub_g, lam_init,
                           n_batch=BATCH, t_len=SEQ, tq=SEQ, tok_off=0, n_seq=PROMPT_SEQS_PER_STEP)
            o_cat = _attn_ab(q, dq, (k, v, dk, dv), (kc, vc, dkc, dvc), lam, sub_g, lam_init,
                             n_batch=DEC_BATCH, t_len=DEC_SEQ, tq=TQ_AB, tok_off=NP_TOK, out_init=o_p)
            w_out = w_out0[i]
        else:
            wi = w_in1[i]
            n_sq = SWA_HEADS * HEAD_DIM
            w_in_p = jnp.concatenate([_pair_kv_heads(wi[:, :n_sq].T).T, wi[:, n_sq:]], axis=1).astype(BF16)
            (sq, sk, sv, nq, nk, nv, sk_st, sv_st, nk_st, nv_st) = _premix1(xs[0], mod_l, g_l[0:1], w_in_p, t64)
            states['swa_k'] = sk_st.reshape(BATCH, 1, SEQ, SWA_KV_HEADS, HEAD_DIM)
            states['swa_v'] = sv_st.reshape(BATCH, 1, SEQ, SWA_KV_HEADS, HEAD_DIM)
            states['na_k'] = nk_st.reshape(BATCH, 1, SEQ, NA_HEADS, HEAD_DIM)
            states['na_v'] = nv_st.reshape(BATCH, 1, SEQ, NA_HEADS, HEAD_DIM)
            skc = cache_swa_k[:, i].reshape(DEC_BATCH * PAST_LEN, 128).astype(BF16)
            svc = cache_swa_v[:, i].reshape(DEC_BATCH * PAST_LEN, 128).astype(BF16)
            nkc = cache_na_k[:, i].reshape(DEC_BATCH * PAST_LEN, 512).astype(BF16)
            nvc = cache_na_v[:, i].reshape(DEC_BATCH * PAST_LEN, 512).astype(BF16)
            sink = swa_sink[i]
            o_p = _attn_cd_prompt(sink, sq, sk, sv, nq, nk, nv)
            o_cat = _attn_cd_sample(o_p, sink, sq, sk, sv, nq, nk, nv, skc, svc, nkc, nvc, _na_bias(na_rpb[i]))
            wo = w_out1[i]
            w_out = jnp.concatenate([_pair_kv_heads(wo[:n_sq]), wo[n_sq:]], axis=0)
        xs = _ffn(l, o_cat, xs, mod_l, g_l, w_out, w_router[l], b_router[l], w_gu, b_gu, w_down, b_down,
                  split_out=(l == DEPTH - 1))
    return (xs[0].reshape(BATCH, SEQ, D_MODEL), xs[1].reshape(DEC_BATCH, DEC_SEQ, D_MODEL),
            states['mla_ckv'], states['mla_krope'], states['diff_k'], states['diff_v'],
            states['swa_k'], states['swa_v'], states['na_k'], states['na_v'])
```

```python
import functools
import math

import numpy as np
import jax
import jax.numpy as jnp
from jax import lax
from jax.experimental import pallas as pl
from jax.experimental.pallas import tpu as pltpu
from jax.experimental.pallas import tpu_sc as plsc

F32 = jnp.float32
BF16 = jnp.bfloat16

D_MODEL = 1024
BATCH = 16
SEQ = 256
DEPTH = 2
DEC_BATCH = 8
DEC_SEQ = 2048
PAST_LEN = 256
GRID_W = 64
HEAD_DIM = 64
ROPE_THETA = 10000.0
EPS = 1e-6
NEG = -1e30

MLA_HEADS = 8
MLA_Q_RANK = 384
MLA_KV_RANK = 256
MLA_NOPE = 64
MLA_ROPE = 32
MLA_V = 64
DIFF_HEADS = 4
DIFF_DH = 64
SWA_HEADS = 8
SWA_KV_HEADS = 2
SWA_WINDOW = 128
NA_HEADS = 8
NA_WIN_ROWS = 8
NA_WIN_COLS = 16
N_EXPERTS = 32
TOP_K = 4
D_EXPERT = 1024
SWIGLU_LIMIT = 7.0
SWIGLU_ALPHA = 1.702

LANES = 128
NP_TOK = BATCH * SEQ
NS_TOK = DEC_BATCH * DEC_SEQ
N_TOK = NP_TOK + NS_TOK
TM = 512
NPT = NP_TOK // TM
TILES_PER_SAMPLE = DEC_SEQ // TM
N_TILES = N_TOK // TM
N_SUB = 2
SUB_TM = TM // N_SUB
TQ = 256
TQ_AB = 256
PROMPT_SEQS_PER_STEP = 2
MOE_TM = 1024
MOE_TAIL_PARTS = 8
MOE_ROWS = ((N_TOK * TOP_K + N_EXPERTS * (MOE_TM - 1)) // MOE_TM + 1) * MOE_TM
MOE_BLOCKS = MOE_ROWS // MOE_TM
NA_TILE_ROWS = TQ // GRID_W
NA_KEY_ROWS = 12
VMEM_LIMIT = 56 * 1024 * 1024


def _cparams(n_axes, vmem=VMEM_LIMIT):
    return pltpu.CompilerParams(dimension_semantics=("arbitrary",) * n_axes,
                                vmem_limit_bytes=vmem)


def _rms(x, g):
    return x * lax.rsqrt(jnp.mean(x * x, axis=-1, keepdims=True) + EPS) * g


def _dot(a, b):
    return jnp.dot(a, b, preferred_element_type=F32)


def _dot_nt(a, b):
    return lax.dot_general(a, b, (((1,), (1,)), ((), ())), preferred_element_type=F32)


def _rope(x, cos, sin_a, sin_b, half):
    return (x * cos + pltpu.roll(x, LANES - half, 1) * sin_a + pltpu.roll(x, half, 1) * sin_b)


def _pipelined_units(units):
    pending = units[0][0](units[0][2])
    for n, (_, finish, arg) in enumerate(units):
        following = units[n + 1][0](units[n + 1][2]) if n + 1 < len(units) else None
        finish(arg, pending)
        pending = following


def _sub_rows(n):
    return slice(SUB_TM * n, SUB_TM * (n + 1))


def _store_head_rows(st_ref, n, hd, n_heads, x):
    st_ref[pl.ds(SUB_TM * n * n_heads + hd, SUB_TM, stride=n_heads), :] = x


def _mod_row(i):
    return jnp.where(i < NPT, 0, 1 + (i - NPT) // TILES_PER_SAMPLE)


def _rope_blk(i):
    return jnp.where(i < NPT, TILES_PER_SAMPLE, (i - NPT) % TILES_PER_SAMPLE)


def _mod_kernel(c_ref, w_ref, b_ref, o_ref):
    c = c_ref[...]
    s = (c * jax.nn.sigmoid(c)).astype(BF16)
    o_ref[0] = _dot(s, w_ref[0].astype(BF16)) + b_ref[0]


def _modulation(cond, w_mod, b_mod):
    nb = 1024
    return pl.pallas_call(
        _mod_kernel,
        grid=(DEPTH, 6 * D_MODEL // nb),
        in_specs=[
            pl.BlockSpec((16, D_MODEL), lambda l, n: (0, 0)),
            pl.BlockSpec((1, D_MODEL, nb), lambda l, n: (l, 0, n)),
            pl.BlockSpec((1, 1, nb), lambda l, n: (l, 0, n)),
        ],
        out_specs=pl.BlockSpec((1, 16, nb), lambda l, n: (l, 0, n)),
        out_shape=jax.ShapeDtypeStruct((DEPTH, 16, 6 * D_MODEL), F32),
        compiler_params=_cparams(2),
        name="modulation",
    )(cond, w_mod, b_mod.reshape(DEPTH, 1, 6 * D_MODEL))


def _rope_tables():
    t = jnp.arange(DEC_SEQ)
    rows = (t // GRID_W).astype(F32)
    cols = (t % GRID_W).astype(F32)

    def angles(r):
        n = r // 4
        inv = ROPE_THETA ** (-jnp.arange(n, dtype=F32) / n)
        return jnp.concatenate([rows[:, None] * inv[None], cols[:, None] * inv[None]], axis=-1)

    def finish(cos, sa, sb):
        ident = (jnp.ones((TM, LANES), F32), jnp.zeros((TM, LANES), F32), jnp.zeros((TM, LANES), F32))
        return tuple(jnp.concatenate([a, b], axis=0) for a, b in zip((cos, sa, sb), ident))

    a64 = angles(64)
    c, s, z = jnp.cos(a64), jnp.sin(a64), jnp.zeros_like(a64)
    t64 = finish(jnp.concatenate([c, c, c, c], -1), jnp.concatenate([-s, z, -s, z], -1),
                 jnp.concatenate([z, s, z, s], -1))
    a32 = angles(32)
    c, s, z = jnp.cos(a32), jnp.sin(a32), jnp.zeros_like(a32)
    one64 = jnp.ones((DEC_SEQ, 64), F32)
    z64 = jnp.zeros((DEC_SEQ, 64), F32)
    z32 = jnp.zeros((DEC_SEQ, 32), F32)
    t32 = finish(jnp.concatenate([one64, c, c, z32], -1), jnp.concatenate([z64, -s, z, z32], -1),
                 jnp.concatenate([z64, z, s, z32], -1))
    return t64, t32


LOG2E = math.log2(math.e)
_DIFF_COLS = DIFF_HEADS * 2 * DIFF_DH
_AB_COLS = tuple(np.cumsum([0, MLA_Q_RANK, MLA_KV_RANK, _DIFF_COLS, _DIFF_COLS, _DIFF_COLS, LANES]).tolist())
_CD_COLS = tuple(np.cumsum([0, SWA_HEADS * HEAD_DIM, SWA_KV_HEADS * HEAD_DIM, SWA_KV_HEADS * HEAD_DIM,
                            NA_HEADS * HEAD_DIM, NA_HEADS * HEAD_DIM, NA_HEADS * HEAD_DIM]).tolist())
IN_COLS = _AB_COLS[-1]
assert IN_COLS == _CD_COLS[-1]
_MLA_SCALE = (MLA_NOPE + MLA_ROPE) ** -0.5 * LOG2E
_QSCALE = HEAD_DIM ** -0.5 * LOG2E


def _premix0_kernel(xp_ref, xs_ref, mod_ref, g_ref, win_ref, qn_ref, wuq_ref, kvn_ref, wk_ref, wv_ref,
                    c32_ref, sa32_ref, sb32_ref, c64_ref, sa64_ref, sb64_ref,
                    q_ref, k_ref, v_ref, dq_ref, dk_ref, dv_ref,
                    ckv_st, kpe_st, dk_st, dv_st):
    i = pl.program_id(0)
    m = mod_ref[0]
    gain = g_ref[...] * (1.0 + m[1:2])

    def project(n):
        r = _sub_rows(n)
        x = jnp.where(i < NPT, xp_ref[r, :], xs_ref[r, :])
        h = _rms(x, gain) + m[0:1]
        return _dot(h.astype(BF16), win_ref[...])

    states = {}

    def finish(n, proj):
        r = _sub_rows(n)
        q_a, kv_a, dq, dk, dv, pe = (proj[:, a:b] for a, b in zip(_AB_COLS[:-1], _AB_COLS[1:]))
        q = _dot(_rms(q_a, qn_ref[...]).astype(BF16), wuq_ref[...])
        ckv = _rms(kv_a, kvn_ref[...])
        ckv_b = ckv.astype(BF16)
        kn = _dot(ckv_b, wk_ref[...])
        v_ref[r, :] = _dot(ckv_b, wv_ref[...]).astype(BF16)
        c32, sa32, sb32 = c32_ref[r, :], sa32_ref[r, :], sb32_ref[r, :]
        c64, sa64, sb64 = c64_ref[r, :], sa64_ref[r, :], sb64_ref[r, :]
        pe_r = _rope(pe, c32, sa32, sb32, MLA_ROPE // 2)
        for hd in range(MLA_HEADS):
            sl = slice(LANES * hd, LANES * (hd + 1))
            q_ref[r, sl] = (_rope(q[:, sl], c32, sa32, sb32, MLA_ROPE // 2) * _MLA_SCALE).astype(BF16)
            k_ref[r, sl] = (kn[:, sl] + pe_r).astype(BF16)
        for hd in range(DIFF_HEADS):
            sl = slice(LANES * hd, LANES * (hd + 1))
            dq_ref[r, sl] = (_rope(dq[:, sl], c64, sa64, sb64, DIFF_DH // 2) * _QSCALE).astype(BF16)
            dk_ref[r, sl] = _rope(dk[:, sl], c64, sa64, sb64, DIFF_DH // 2).astype(BF16)
        dv_ref[r, :] = dv.astype(BF16)
        states[n] = (ckv, pe, dk, dv)

    _pipelined_units([(project, finish, n) for n in range(N_SUB)])

    @pl.when(i < NPT)
    def _():
        for n in range(N_SUB):
            r = _sub_rows(n)
            ckv, pe, dk, dv = states[n]
            ckv_st[r, :] = ckv
            kpe_st[r, :] = pe
            for hd in range(DIFF_HEADS):
                _store_head_rows(dk_st, n, hd, DIFF_HEADS, dk[:, LANES * hd: LANES * (hd + 1)])
                _store_head_rows(dv_st, n, hd, DIFF_HEADS, dv[:, LANES * hd: LANES * (hd + 1)])


def _premix1_kernel(x_ref, mod_ref, g_ref, win_ref, c64_ref, sa64_ref, sb64_ref,
                    sq_ref, sk_ref, sv_ref, nq_ref, nk_ref, nv_ref,
                    sk_st, sv_st, nk_st, nv_st):
    i = pl.program_id(0)
    m = mod_ref[0]
    gain = g_ref[...] * (1.0 + m[1:2])

    def project(n):
        h = _rms(x_ref[_sub_rows(n), :], gain) + m[0:1]
        return _dot(h.astype(BF16), win_ref[...])

    states = {}

    def finish(n, proj):
        r = _sub_rows(n)
        sq, sk, sv, nq, nk, nv = (proj[:, a:b] for a, b in zip(_CD_COLS[:-1], _CD_COLS[1:]))
        c64, sa64, sb64 = c64_ref[r, :], sa64_ref[r, :], sb64_ref[r, :]
        for hd in range(4):
            sl = slice(LANES * hd, LANES * (hd + 1))
            sq_ref[r, sl] = (_rope(sq[:, sl], c64, sa64, sb64, HEAD_DIM // 2) * _QSCALE).astype(BF16)
        sk_ref[r, :] = _rope(sk, c64, sa64, sb64, HEAD_DIM // 2).astype(BF16)
        sv_ref[r, :] = sv.astype(BF16)
        nq_ref[r, :] = (nq * _QSCALE).astype(BF16)
        nk_ref[r, :] = nk.astype(BF16)
        nv_ref[r, :] = nv.astype(BF16)
        states[n] = (sk, sv, nk, nv)

    _pipelined_units([(project, finish, n) for n in range(N_SUB)])

    @pl.when(i < NPT)
    def _():
        for n in range(N_SUB):
            r = _sub_rows(n)
            sk, sv, nk, nv = states[n]
            sk_st[r, :] = sk
            sv_st[r, :] = sv
            for hd in range(NA_HEADS):
                _store_head_rows(nk_st, n, hd, NA_HEADS, nk[:, HEAD_DIM * hd: HEAD_DIM * (hd + 1)])
                _store_head_rows(nv_st, n, hd, NA_HEADS, nv[:, HEAD_DIM * hd: HEAD_DIM * (hd + 1)])


def _tok_spec(width):
    return pl.BlockSpec((TM, width), lambda i: (i, 0))


_PROMPT_SPEC = pl.BlockSpec((TM, D_MODEL), lambda i: (jnp.minimum(i, NPT - 1), 0))
_SAMPLE_SPEC = pl.BlockSpec((TM, D_MODEL), lambda i: (jnp.maximum(i - NPT, 0), 0))


def _state_spec(width, rows_per_token=1):
    return pl.BlockSpec((TM * rows_per_token, width), lambda i: (jnp.minimum(i, NPT - 1), 0))


def _const_spec(shape):
    return pl.BlockSpec(shape, lambda i: (0,) * len(shape))


_MOD_SPEC = pl.BlockSpec((1, 6, D_MODEL), lambda i: (_mod_row(i), 0, 0))
_ROPE_SPEC = pl.BlockSpec((TM, LANES), lambda i: (_rope_blk(i), 0))


def _premix0(xp, xs, mod_l, g0, w_in_p, q_norm, w_uq_p, kv_norm, w_k_p, w_v, t32, t64):
    outs = [(N_TOK, 1024, BF16), (N_TOK, 1024, BF16), (N_TOK, 512, BF16), (N_TOK, 512, BF16),
            (N_TOK, 512, BF16), (N_TOK, 512, BF16),
            (NP_TOK, 256, F32), (NP_TOK, 128, F32),
            (NP_TOK * DIFF_HEADS, 2 * DIFF_DH, F32), (NP_TOK * DIFF_HEADS, 2 * DIFF_DH, F32)]
    return pl.pallas_call(
        _premix0_kernel,
        grid=(N_TILES,),
        in_specs=[_PROMPT_SPEC, _SAMPLE_SPEC, _MOD_SPEC, _const_spec((1, D_MODEL)),
                  _const_spec((D_MODEL, IN_COLS)), _const_spec((1, MLA_Q_RANK)),
                  _const_spec((MLA_Q_RANK, 1024)), _const_spec((1, MLA_KV_RANK)),
                  _const_spec((MLA_KV_RANK, 1024)), _const_spec((MLA_KV_RANK, 512))]
                 + [_ROPE_SPEC] * 6,
        out_specs=([_tok_spec(w) for (_, w, _) in outs[:6]]
                   + [_state_spec(w, n // NP_TOK) for (n, w, _) in outs[6:]]),
        out_shape=[jax.ShapeDtypeStruct((n, w), dt) for (n, w, dt) in outs],
        compiler_params=_cparams(1),
        name="premix_ab",
    )(xp, xs, mod_l, g0, w_in_p, q_norm, w_uq_p, kv_norm, w_k_p, w_v, *t32, *t64)


def _premix1(x, mod_l, g0, w_in_p, t64):
    outs = [(N_TOK, 512, BF16), (N_TOK, 128, BF16), (N_TOK, 128, BF16), (N_TOK, 512, BF16),
            (N_TOK, 512, BF16), (N_TOK, 512, BF16),
            (NP_TOK, 128, F32), (NP_TOK, 128, F32),
            (NP_TOK * NA_HEADS, HEAD_DIM, F32), (NP_TOK * NA_HEADS, HEAD_DIM, F32)]
    return pl.pallas_call(
        _premix1_kernel,
        grid=(N_TILES,),
        in_specs=[_tok_spec(D_MODEL), _MOD_SPEC, _const_spec((1, D_MODEL)),
                  _const_spec((D_MODEL, IN_COLS))] + [_ROPE_SPEC] * 3,
        out_specs=([_tok_spec(w) for (_, w, _) in outs[:6]]
                   + [_state_spec(w, n // NP_TOK) for (n, w, _) in outs[6:]]),
        out_shape=[jax.ShapeDtypeStruct((n, w), dt) for (n, w, dt) in outs],
        compiler_params=_cparams(1),
        name="premix_cd",
    )(x, mod_l, g0, w_in_p, *t64)


def _mla_cache_kernel(ckv_ref, pe_ref, wk_ref, wv_ref, k_ref, v_ref):
    c = ckv_ref[...].astype(BF16)
    kn = _dot(c, wk_ref[...])
    v_ref[...] = _dot(c, wv_ref[...]).astype(BF16)
    pe = pe_ref[...]
    for hd in range(MLA_HEADS):
        sl = slice(LANES * hd, LANES * (hd + 1))
        k_ref[:, sl] = (kn[:, sl] + pe).astype(BF16)


def _mla_cache(ckv, pe_slab, w_k_p, w_v):
    n = ckv.shape[0]
    tm = 512
    return pl.pallas_call(
        _mla_cache_kernel,
        grid=(n // tm,),
        in_specs=[pl.BlockSpec((tm, MLA_KV_RANK), lambda i: (i, 0)),
                  pl.BlockSpec((tm, LANES), lambda i: (i, 0)),
                  _const_spec((MLA_KV_RANK, 1024)), _const_spec((MLA_KV_RANK, 512))],
        out_specs=[pl.BlockSpec((tm, 1024), lambda i: (i, 0)), pl.BlockSpec((tm, 512), lambda i: (i, 0))],
        out_shape=[jax.ShapeDtypeStruct((n, 1024), BF16), jax.ShapeDtypeStruct((n, 512), BF16)],
        compiler_params=_cparams(1),
        name="mla_cache",
    )(ckv, pe_slab, w_k_p, w_v)


def _softmax_pv(scores, values, sink=None):
    m = jnp.max(scores[0], axis=-1, keepdims=True)
    for s in scores[1:]:
        m = jnp.maximum(m, jnp.max(s, axis=-1, keepdims=True))
    if sink is not None:
        m = jnp.maximum(m, sink)
    l = None
    o = None
    for s, v in zip(scores, values):
        p = jnp.exp2(s - m)
        ls = jnp.sum(p, axis=-1, keepdims=True)
        os_ = _dot(p.astype(BF16), v)
        l = ls if l is None else l + ls
        o = os_ if o is None else o + os_
    if sink is not None:
        l = l + jnp.exp2(sink - m)
    return o * (1.0 / l)


def _lane_lo(shape):
    return lax.broadcasted_iota(jnp.int32, shape, 1) < (LANES // 2)


def _split_halves(qb):
    lo = _lane_lo(qb.shape)
    zero = jnp.zeros_like(qb)
    return jnp.where(lo, qb, zero), jnp.where(lo, zero, qb)


def _attn_ab_kernel(*refs, n_pieces, n_seq, lam_init, aliased):
    if aliased:
        refs = refs[1:]
    q_ref, dq_ref = refs[0], refs[1]
    pieces = [refs[2 + 4 * p: 6 + 4 * p] for p in range(n_pieces)]
    lam_ref, subg_ref, o_ref = refs[2 + 4 * n_pieces:]
    lam = lam_ref[...]
    lam_full = (jnp.exp(jnp.sum(lam[0:1] * lam[1:2], axis=-1, keepdims=True))
                - jnp.exp(jnp.sum(lam[2:3] * lam[3:4], axis=-1, keepdims=True)) + lam_init)
    tq = q_ref.shape[0] // n_seq
    lo = _lane_lo((tq, LANES))
    subg = subg_ref[...] * (1.0 - lam_init)

    def seq_rows(ref, sq):
        n = ref.shape[0] // n_seq
        return slice(n * sq, n * (sq + 1))

    def mla_scores(arg):
        sq, hd = arg
        sl = slice(LANES * hd, LANES * (hd + 1))
        qh = q_ref[seq_rows(q_ref, sq), sl]
        return [_dot_nt(qh, k_ref[seq_rows(k_ref, sq), sl]) for (k_ref, _, _, _) in pieces]

    def diff_scores(arg):
        sq, hd = arg
        sl = slice(LANES * hd, LANES * (hd + 1))
        qq = jnp.concatenate(_split_halves(dq_ref[seq_rows(dq_ref, sq), sl]), axis=0)
        return [_dot_nt(qq, dk_ref[seq_rows(dk_ref, sq), sl]) for (_, _, dk_ref, _) in pieces]

    pair = {}

    def mla_finish(arg, scores):
        sq, hd = arg
        j = hd // 2
        vals = [v_ref[seq_rows(v_ref, sq), LANES * j: LANES * (j + 1)] for (_, v_ref, _, _) in pieces]
        pair[hd % 2] = _softmax_pv(scores, vals)
        if hd % 2 == 1:
            o_ref[seq_rows(o_ref, sq), LANES * j: LANES * (j + 1)] = jnp.where(lo, pair[0], pair[1]).astype(BF16)

    def diff_finish(arg, scores):
        sq, hd = arg
        sl = slice(LANES * hd, LANES * (hd + 1))
        oo = _softmax_pv(scores, [dv_ref[seq_rows(dv_ref, sq), sl] for (_, _, _, dv_ref) in pieces])
        od = _rms(oo[:tq] - lam_full * oo[tq:], subg)
        o_ref[seq_rows(o_ref, sq), 512 + LANES * hd: 512 + LANES * (hd + 1)] = od.astype(BF16)

    units = []
    for sq in range(n_seq):
        for j in range(DIFF_HEADS):
            units += [(diff_scores, diff_finish, (sq, j)), (mla_scores, mla_finish, (sq, 2 * j)),
                      (mla_scores, mla_finish, (sq, 2 * j + 1))]
    _pipelined_units(units)


def _attn_ab(q, dq, new_kv, cache_kv, lam, sub_g, lam_init, *, n_batch, t_len, tq, tok_off, out_init=None,
             n_seq=1):
    assert n_seq == 1 or (tq == t_len and cache_kv is None and n_batch % n_seq == 0)
    n_batch, t_len, tq = n_batch // n_seq, t_len * n_seq, tq * n_seq
    nq = t_len // tq
    q_off = tok_off // tq
    b_off = tok_off // t_len
    widths = (1024, 512, 512, 512)
    in_specs = [pl.BlockSpec((tq, 1024), lambda b, i: (q_off + b * nq + i, 0)),
                pl.BlockSpec((tq, 512), lambda b, i: (q_off + b * nq + i, 0))]
    args = [q, dq]
    for w, a in zip(widths, new_kv):
        in_specs.append(pl.BlockSpec((t_len, w), lambda b, i: (b_off + b, 0)))
        args.append(a)
    n_pieces = 1
    if cache_kv is not None:
        n_pieces = 2
        for w, a in zip(widths, cache_kv):
            in_specs.append(pl.BlockSpec((PAST_LEN, w), lambda b, i: (b, 0)))
            args.append(a)
    in_specs += [pl.BlockSpec((4, DIFF_DH), lambda b, i: (0, 0)),
                 pl.BlockSpec((1, 2 * DIFF_DH), lambda b, i: (0, 0))]
    args += [lam, sub_g]
    aliases = {}
    if out_init is not None:
        in_specs = [pl.BlockSpec(memory_space=pl.ANY)] + in_specs
        args = [out_init] + args
        aliases = {0: 0}
    return pl.pallas_call(
        functools.partial(_attn_ab_kernel, n_pieces=n_pieces, n_seq=n_seq, lam_init=lam_init,
                          aliased=out_init is not None),
        grid=(n_batch, nq),
        in_specs=in_specs,
        out_specs=pl.BlockSpec((tq, 1024), lambda b, i: (q_off + b * nq + i, 0)),
        out_shape=jax.ShapeDtypeStruct((N_TOK, 1024), BF16),
        input_output_aliases=aliases,
        compiler_params=_cparams(2),
        name="attn_ab_%d" % n_pieces,
    )(*args)


def _gqa_stacks(sq_ref, sink_ref):
    tq = sq_ref.shape[0]
    halves = [_split_halves(sq_ref[:, LANES * j: LANES * (j + 1)]) for j in range(4)]
    q_stacks = [jnp.concatenate([halves[j][kvh] for j in range(4)], axis=0) for kvh in range(SWA_KV_HEADS)]
    sinks = [jnp.concatenate([jnp.full((tq, 1), sink_ref[4 * kvh + j] * LOG2E, F32) for j in range(4)], axis=0)
             for kvh in range(SWA_KV_HEADS)]
    return q_stacks, sinks


def _attn_cd_prompt_kernel(sink_ref, sq_ref, sk_ref, sv_ref, nq_ref, nk_ref, nv_ref, o_ref):
    lo = _lane_lo((SEQ, LANES))
    for sq in range(PROMPT_SEQS_PER_STEP):
        r = slice(SEQ * sq, SEQ * (sq + 1))
        sk = sk_ref[r, :]
        sv = sv_ref[r, :]
        for j in range(4):
            sl = slice(LANES * j, LANES * (j + 1))
            q_lo, q_hi = _split_halves(sq_ref[r, sl])
            o_lo = _softmax_pv([_dot_nt(q_lo, sk)], [sv], sink=sink_ref[j] * LOG2E)
            o_hi = _softmax_pv([_dot_nt(q_hi, sk)], [sv], sink=sink_ref[j + 4] * LOG2E)
            o_ref[r, sl] = jnp.where(lo, o_lo, o_hi).astype(BF16)
        for j in range(4):
            sl = slice(LANES * j, LANES * (j + 1))
            q_lo, q_hi = _split_halves(nq_ref[r, sl])
            k = nk_ref[r, sl]
            v = nv_ref[r, sl]
            o_lo = _softmax_pv([_dot_nt(q_lo, k)], [v])
            o_hi = _softmax_pv([_dot_nt(q_hi, k)], [v])
            o_ref[r, 512 + LANES * j: 512 + LANES * (j + 1)] = jnp.where(lo, o_lo, o_hi).astype(BF16)


def _attn_cd_prompt(sink, sq, sk, sv, nq, nk, nv):
    def spec(w):
        return pl.BlockSpec((SEQ * PROMPT_SEQS_PER_STEP, w), lambda b: (b, 0))
    return pl.pallas_call(
        _attn_cd_prompt_kernel,
        grid=(BATCH // PROMPT_SEQS_PER_STEP,),
        in_specs=[pl.BlockSpec(memory_space=pltpu.SMEM), spec(512), spec(128), spec(128),
                  spec(512), spec(512), spec(512)],
        out_specs=spec(1024),
        out_shape=jax.ShapeDtypeStruct((N_TOK, 1024), BF16),
        compiler_params=_cparams(1),
        name="attn_cd_prompt",
    )(sink, sq, sk, sv, nq, nk, nv)


_SWA_KEYS = TQ + 2 * SWA_WINDOW


def _attn_cd_sample_kernel(init_ref, sink_ref, sq_ref, nq_ref, sk_ref, sv_ref, nk_ref, nv_ref,
                           skc_ref, svc_ref, nkc_ref, nvc_ref, bias_ref, o_ref):
    del init_ref
    qi = pl.program_id(1)
    lo = _lane_lo((TQ, LANES))
    ks = pl.multiple_of(jnp.clip(qi * TQ - SWA_WINDOW, 0, DEC_SEQ - _SWA_KEYS), SWA_WINDOW)
    k_win = sk_ref[pl.ds(ks, _SWA_KEYS), :]
    v_win = sv_ref[pl.ds(ks, _SWA_KEYS), :]
    q_pos = qi * TQ + lax.broadcasted_iota(jnp.int32, (TQ, _SWA_KEYS), 0)
    k_pos = ks + lax.broadcasted_iota(jnp.int32, (TQ, _SWA_KEYS), 1)
    in_win = jnp.abs(q_pos - k_pos) <= SWA_WINDOW
    skc = skc_ref[...]
    svc = svc_ref[...]
    q_stacks, sinks = _gqa_stacks(sq_ref, sink_ref)

    def window_scores(kvh):
        s = _dot_nt(q_stacks[kvh], k_win)
        s = jnp.concatenate([jnp.where(in_win, s[TQ * j: TQ * (j + 1)], NEG) for j in range(4)], axis=0)
        return [_dot_nt(q_stacks[kvh], skc), s]

    o_kv = {}

    def window_finish(kvh, scores):
        o_kv[kvh] = _softmax_pv(scores, [svc, v_win], sink=sinks[kvh])
        if kvh == SWA_KV_HEADS - 1:
            for j in range(4):
                rows = slice(TQ * j, TQ * (j + 1))
                o_ref[:, LANES * j: LANES * (j + 1)] = jnp.where(lo, o_kv[0][rows], o_kv[1][rows]).astype(BF16)

    n_rows = DEC_SEQ // GRID_W
    r0 = jnp.clip(qi * NA_TILE_ROWS - NA_WIN_ROWS // 2, 0, n_rows - NA_KEY_ROWS)
    kn = pl.multiple_of(r0 * GRID_W, GRID_W)

    lane_lo = _lane_lo((1, LANES))
    pieces = []
    for ri in range(NA_TILE_ROWS):
        r = qi * NA_TILE_ROWS + ri
        rs = jnp.clip(r - NA_WIN_ROWS // 2, 0, n_rows - NA_WIN_ROWS)
        row = []
        for mm in range(NA_KEY_ROWS // 2):
            kr = r0 + 2 * mm
            ok = [(kr + t >= rs) & (kr + t < rs + NA_WIN_ROWS) for t in range(2)]
            mask = jnp.where(lane_lo, jnp.where(ok[0], 0.0, NEG), jnp.where(ok[1], 0.0, NEG))
            row.append((jnp.clip(kr - r + NA_WIN_ROWS, 0, 2 * NA_WIN_ROWS - 1), mask))
        pieces.append(row)

    def add_na_bias(hd, s):
        return jnp.concatenate(
            [jnp.concatenate([s[GRID_W * ri: GRID_W * (ri + 1), LANES * mm: LANES * (mm + 1)] + bias_ref[hd, d] + mask
                              for mm, (d, mask) in enumerate(row)], axis=1)
             for ri, row in enumerate(pieces)], axis=0)

    def na_scores(hd):
        sl = slice(LANES * (hd // 2), LANES * (hd // 2 + 1))
        q_half = _split_halves(nq_ref[:, sl])[hd % 2]
        nk_win = nk_ref[pl.ds(kn, NA_KEY_ROWS * GRID_W), sl]
        return [_dot_nt(q_half, nkc_ref[:, sl]), add_na_bias(hd, _dot_nt(q_half, nk_win))]

    pair = {}

    def na_finish(hd, scores):
        j = hd // 2
        sl = slice(LANES * j, LANES * (j + 1))
        nv_win = nv_ref[pl.ds(kn, NA_KEY_ROWS * GRID_W), sl]
        pair[hd % 2] = _softmax_pv(scores, [nvc_ref[:, sl], nv_win])
        if hd % 2 == 1:
            o_ref[:, 512 + LANES * j: 512 + LANES * (j + 1)] = jnp.where(lo, pair[0], pair[1]).astype(BF16)

    _pipelined_units([(window_scores, window_finish, kvh) for kvh in range(SWA_KV_HEADS)]
                     + [(na_scores, na_finish, hd) for hd in range(NA_HEADS)])


def _attn_cd_sample(out_init, sink, sq, sk, sv, nq, nk, nv, skc, svc, nkc, nvc, bias):
    nq_t = DEC_SEQ // TQ
    q_off = NP_TOK // TQ
    b_off = NP_TOK // DEC_SEQ

    def qspec(w):
        return pl.BlockSpec((TQ, w), lambda b, i: (q_off + b * nq_t + i, 0))

    def kspec(w):
        return pl.BlockSpec((DEC_SEQ, w), lambda b, i: (b_off + b, 0))

    def cspec(w):
        return pl.BlockSpec((PAST_LEN, w), lambda b, i: (b, 0))

    bias_spec = pl.BlockSpec((NA_HEADS, 2 * NA_WIN_ROWS, GRID_W, LANES), lambda b, i: (0, 0, 0, 0))
    return pl.pallas_call(
        _attn_cd_sample_kernel,
        grid=(DEC_BATCH, nq_t),
        in_specs=[pl.BlockSpec(memory_space=pl.ANY), pl.BlockSpec(memory_space=pltpu.SMEM),
                  qspec(512), qspec(512), kspec(128), kspec(128), kspec(512), kspec(512),
                  cspec(128), cspec(128), cspec(512), cspec(512), bias_spec],
        out_specs=pl.BlockSpec((TQ, 1024), lambda b, i: (q_off + b * nq_t + i, 0)),
        out_shape=jax.ShapeDtypeStruct((N_TOK, 1024), BF16),
        input_output_aliases={0: 0},
        compiler_params=_cparams(2),
        name="attn_cd_sample",
    )(out_init, sink, sq, nq, sk, sv, nk, nv, skc, svc, nkc, nvc, bias)


def _na_bias(rpb):
    n_dc = 2 * NA_WIN_COLS - 1
    c = np.arange(GRID_W)[:, None]
    kc = np.arange(GRID_W)[None, :]
    qs = np.clip(c - NA_WIN_COLS // 2, 0, GRID_W - NA_WIN_COLS)
    col_ok = (kc >= qs) & (kc < qs + NA_WIN_COLS)
    dc = np.clip(kc - c + NA_WIN_COLS - 1, 0, n_dc - 1)
    onehot = ((dc[None] == np.arange(n_dc)[:, None, None]) & col_ok[None]).astype(np.float32)
    blocks = jnp.einsum('hrd,dck->hrck', rpb.astype(F32) * LOG2E, onehot, precision=lax.Precision.HIGHEST)
    blocks = jnp.where(col_ok[None, None], blocks, NEG)
    none = jnp.full((NA_HEADS, 1, GRID_W, GRID_W), NEG, F32)
    return jnp.concatenate([jnp.concatenate([none, blocks], axis=1),
                            jnp.concatenate([blocks, none], axis=1)], axis=-1)


_HI_MASK = -65536


def _pack_pairs(x):
    w = x.shape[1] // 2
    r = x.astype(BF16).astype(F32)
    lo = lax.bitcast_convert_type(r[:, :w], jnp.int32)
    hi = lax.bitcast_convert_type(r[:, w:], jnp.int32)
    return (hi & _HI_MASK) | lax.shift_right_logical(lo, 16)


def _unpack_pairs(p):
    lo = lax.bitcast_convert_type(lax.shift_left(p, 16), F32)
    hi = lax.bitcast_convert_type(p & _HI_MASK, F32)
    return lo, hi


def _postmix_kernel(*refs, split_x):
    if split_x:
        o_ref, xp_ref, xs_ref = refs[:3]
        refs = refs[3:]
    else:
        o_ref, x_ref = refs[:2]
        refs = refs[2:]
    (mod_ref, g_ref, wout_ref, wr_ref, br_ref, tri_ref,
     x1_ref, h2_ref, route_ref, gate_ref, cnt_ref, run_ref) = refs
    i = pl.program_id(0)

    @pl.when(i == 0)
    def _():
        run_ref[...] = jnp.zeros_like(run_ref)

    m = mod_ref[0]
    g = g_ref[...]
    gate_gain = g[1:2] * m[2:3]
    ffn_gain = g[2:3] * (1.0 + m[4:5])
    sub_logits = {}

    def project(n):
        return _dot(o_ref[_sub_rows(n), :], wout_ref[...])

    def finish(n, y):
        r = _sub_rows(n)
        x = jnp.where(i < NPT, xp_ref[r, :], xs_ref[r, :]) if split_x else x_ref[r, :]
        x1 = x + _rms(y, gate_gain)
        x1_ref[r, :] = x1
        h2 = _rms(x1, ffn_gain) + m[3:4]
        h2_ref[r, :] = _pack_pairs(h2)
        sub_logits[n] = _dot(h2.astype(BF16), wr_ref[...]) + br_ref[...]

    _pipelined_units([(project, finish, n) for n in range(N_SUB)])

    logits = jnp.concatenate([sub_logits[n] for n in range(N_SUB)], axis=0)
    lane = lax.broadcasted_iota(jnp.int32, logits.shape, 1).astype(F32)
    cur = jnp.where(lane < N_EXPERTS, logits, -jnp.inf)
    tops, idxs = [], []
    for _ in range(TOP_K):
        mx = jnp.max(cur, axis=-1, keepdims=True)
        ix = jnp.min(jnp.where(cur == mx, lane, float(LANES)), axis=-1, keepdims=True)
        tops.append(mx)
        idxs.append(ix)
        cur = jnp.where(lane == ix, -jnp.inf, cur)
    es = [jnp.exp(t - tops[0]) for t in tops]
    inv = 1.0 / (es[0] + es[1] + es[2] + es[3])
    picked = jnp.zeros_like(logits)
    for k in range(TOP_K):
        picked = jnp.where(lane == idxs[k], 1.0, picked)
    before = _dot(tri_ref[...], picked.astype(BF16)) + run_ref[0:1, :]
    route = jnp.zeros_like(logits)
    gate_out = jnp.zeros_like(logits)
    for k in range(TOP_K):
        rank = jnp.sum(jnp.where(lane == idxs[k], before, 0.0), axis=-1, keepdims=True)
        route = jnp.where(lane == float(k), idxs[k], route)
        route = jnp.where(lane == float(TOP_K + k), rank, route)
        gate_out = jnp.where(lane == float(k), es[k] * inv, gate_out)
    route_ref[...] = route.T[:2 * TOP_K].astype(jnp.int32)
    gate_ref[...] = gate_out
    run_ref[...] = run_ref[...] + jnp.sum(picked, axis=0, keepdims=True)
    cnt_ref[...] = run_ref[...].astype(jnp.int32)


def _postmix(o_cat, xs, mod_l, g_l, w_out, w_r, b_r):
    tri = jnp.asarray(np.tril(np.ones((TM, TM), np.float32), -1), BF16)
    split_x = len(xs) == 2
    x_specs = [_PROMPT_SPEC, _SAMPLE_SPEC] if split_x else [_tok_spec(D_MODEL)]
    return pl.pallas_call(
        functools.partial(_postmix_kernel, split_x=split_x),
        grid=(N_TILES,),
        in_specs=[_tok_spec(1024)] + x_specs + [_MOD_SPEC, _const_spec((4, D_MODEL)),
                  _const_spec((1024, D_MODEL)), _const_spec((D_MODEL, LANES)), _const_spec((1, LANES)),
                  _const_spec((TM, TM))],
        out_specs=[_tok_spec(D_MODEL), _tok_spec(D_MODEL // 2), pl.BlockSpec((2 * TOP_K, TM), lambda i: (0, i)),
                   _tok_spec(LANES), _const_spec((8, LANES))],
        out_shape=[jax.ShapeDtypeStruct((N_TOK, D_MODEL), F32),
                   jax.ShapeDtypeStruct((N_TOK, D_MODEL // 2), jnp.int32),
                   jax.ShapeDtypeStruct((2 * TOP_K, N_TOK), jnp.int32), jax.ShapeDtypeStruct((N_TOK, LANES), F32),
                   jax.ShapeDtypeStruct((8, LANES), jnp.int32)],
        scratch_shapes=[pltpu.VMEM((8, LANES), F32)],
        compiler_params=_cparams(1),
        name="postmix",
    )(o_cat, *xs, mod_l, g_l, w_out, w_r, b_r, tri)


SC_WORKERS = 32
SC_ROWS = 64
ROW_WORDS = D_MODEL // 2

_SC_SCRATCH = [pltpu.VMEM((SC_ROWS, ROW_WORDS), jnp.int32), pltpu.VMEM((SC_ROWS, ROW_WORDS), jnp.int32),
               pltpu.SemaphoreType.DMA, pltpu.SemaphoreType.DMA, pltpu.SemaphoreType.DMA, pltpu.SemaphoreType.DMA]


def _sc_worker_id():
    return lax.axis_index("s") * 2 + lax.axis_index("c")


def _sc_double_buffered(n_chunks, load, store):
    for cp in load(0, 0):
        cp.start()

    @pl.loop(0, n_chunks, step=2)
    def _(g0):
        for b in range(2):
            g = g0 + b
            for cp in load(g, b):
                cp.wait()

            @pl.when(g >= 1)
            def _():
                for cp in store(g - 1, 1 - b):
                    cp.wait()

            @pl.when(g + 1 < n_chunks)
            def _():
                for cp in load(g + 1, 1 - b):
                    cp.start()

            for cp in store(g, b):
                cp.start()

    for cp in store(n_chunks - 1, (n_chunks - 1) % 2):
        cp.wait()


def _sc_dispatch(src, idx):
    n_chunks = N_TOK // (SC_WORKERS * SC_ROWS)
    assert n_chunks % 2 == 0
    mesh = plsc.VectorSubcoreMesh(core_axis_name="c", subcore_axis_name="s")

    @functools.partial(
        pl.kernel, mesh=mesh,
        out_type=jax.ShapeDtypeStruct((MOE_ROWS, ROW_WORDS), jnp.int32),
        scratch_types=[pltpu.VMEM((n_chunks * TOP_K, SC_ROWS), jnp.int32)] + _SC_SCRATCH)
    def k(src_hbm, idx_hbm, out_hbm, idx_v, buf0, buf1, in0, in1, out0, out1):
        wid = _sc_worker_id()
        pltpu.sync_copy(idx_hbm.at[wid], idx_v)
        bufs, in_sems, out_sems = (buf0, buf1), (in0, in1), (out0, out1)

        def load(g, b):
            rows = pl.ds((wid * n_chunks + g) * SC_ROWS, SC_ROWS)
            return [pltpu.make_async_copy(src_hbm.at[rows], bufs[b], in_sems[b])]

        def store(g, b):
            return [pltpu.make_async_copy(bufs[b], out_hbm.at[idx_v.at[g * TOP_K + kk]], out_sems[b])
                    for kk in range(TOP_K)]

        _sc_double_buffered(n_chunks, load, store)

    return k(src, idx)


def _sc_collect(table, idx):
    n_chunks = idx.shape[1]
    assert n_chunks % 2 == 0
    mesh = plsc.VectorSubcoreMesh(core_axis_name="c", subcore_axis_name="s")

    @functools.partial(
        pl.kernel, mesh=mesh,
        out_type=jax.ShapeDtypeStruct((SC_WORKERS * n_chunks * SC_ROWS, ROW_WORDS), jnp.int32),
        scratch_types=[pltpu.VMEM((n_chunks, SC_ROWS), jnp.int32)] + _SC_SCRATCH)
    def k(table_hbm, idx_hbm, out_hbm, idx_v, buf0, buf1, in0, in1, out0, out1):
        wid = _sc_worker_id()
        pltpu.sync_copy(idx_hbm.at[wid], idx_v)
        bufs, in_sems, out_sems = (buf0, buf1), (in0, in1), (out0, out1)

        def load(g, b):
            return [pltpu.make_async_copy(table_hbm.at[idx_v.at[g]], bufs[b], in_sems[b])]

        def store(g, b):
            rows = pl.ds((wid * n_chunks + g) * SC_ROWS, SC_ROWS)
            return [pltpu.make_async_copy(bufs[b], out_hbm.at[rows], out_sems[b])]

        _sc_double_buffered(n_chunks, load, store)

    return k(table, idx)


def _expert_rows(words, n_valid, wgu_b, wd_b, bgu, bd):
    live = lax.broadcasted_iota(jnp.int32, words.shape, 0) < n_valid
    lo, hi = _unpack_pairs(jnp.where(live, words, 0))
    x = jnp.concatenate([lo, hi], axis=1).astype(BF16)
    gu = _dot(x, wgu_b[...]) + bgu
    g = jnp.minimum(gu[:, :D_EXPERT], SWIGLU_LIMIT)
    u = jnp.clip(gu[:, D_EXPERT:], -SWIGLU_LIMIT, SWIGLU_LIMIT)
    a = g * jax.nn.sigmoid(SWIGLU_ALPHA * g) * (u + 1.0)
    return _pack_pairs(_dot(a.astype(BF16), wd_b[...]) + bd)


def _moe_kernel(blk_e_ref, blk_first_ref, blk_rows_ref, blk_slot_ref, blk_next_ref, blk_io_ref,
                x_ref, wgu_hbm, bgu_ref, wd_hbm, bd_ref, y_ref,
                wgu_f, wd_f, wgu_b, wd_b, sem, *, layer):
    del blk_io_ref
    i = pl.program_id(0)
    n_valid = blk_rows_ref[i]
    quantum = MOE_TM // MOE_TAIL_PARTS

    def weight_copies(e, slot):
        return (pltpu.make_async_copy(wgu_hbm.at[layer, e], wgu_f.at[slot], sem.at[0, slot]),
                pltpu.make_async_copy(wd_hbm.at[layer, e], wd_f.at[slot], sem.at[1, slot]))

    @pl.when(i == 0)
    def _():
        for cp in weight_copies(blk_e_ref[0], blk_slot_ref[0]):
            cp.start()

    @pl.when(blk_first_ref[i] == 1)
    def _():
        slot = blk_slot_ref[i]
        for cp in weight_copies(blk_e_ref[i], slot):
            cp.wait()
        nxt = blk_next_ref[i]

        @pl.when(nxt >= 0)
        def _():
            for cp in weight_copies(nxt, 1 - slot):
                cp.start()

        wgu_b[...] = wgu_f[slot].astype(BF16)
        wd_b[...] = wd_f[slot].astype(BF16)

    for parts in range(1, MOE_TAIL_PARTS + 1):
        rows = parts * quantum

        @pl.when((n_valid > rows - quantum) & (n_valid <= rows))
        def _(rows=rows):
            y_ref[:rows] = _expert_rows(x_ref[:rows], n_valid, wgu_b, wd_b, bgu_ref[0, 0], bd_ref[0, 0])
            if rows < MOE_TM:
                y_ref[rows:] = jnp.zeros((MOE_TM - rows, ROW_WORDS), jnp.int32)


def _moe(layer, blk_meta, xs, w_gu, b_gu, w_down, b_down):
    def row_map(i, e, first, rows, slot, nxt, io):
        return (io[i], 0)

    def bias_map(i, e, *_):
        return (layer, e[i], 0, 0)

    grid_spec = pltpu.PrefetchScalarGridSpec(
        num_scalar_prefetch=6,
        grid=(MOE_BLOCKS,),
        in_specs=[
            pl.BlockSpec((MOE_TM, ROW_WORDS), row_map),
            pl.BlockSpec(memory_space=pl.ANY),
            pl.BlockSpec((1, 1, 1, 2 * D_EXPERT), bias_map),
            pl.BlockSpec(memory_space=pl.ANY),
            pl.BlockSpec((1, 1, 1, D_MODEL), bias_map),
        ],
        out_specs=pl.BlockSpec((MOE_TM, ROW_WORDS), row_map),
        scratch_shapes=[pltpu.VMEM((2, D_MODEL, 2 * D_EXPERT), F32), pltpu.VMEM((2, D_EXPERT, D_MODEL), F32),
                        pltpu.VMEM((D_MODEL, 2 * D_EXPERT), BF16), pltpu.VMEM((D_EXPERT, D_MODEL), BF16),
                        pltpu.SemaphoreType.DMA((2, 2))],
    )
    return pl.pallas_call(
        functools.partial(_moe_kernel, layer=layer),
        grid_spec=grid_spec,
        out_shape=jax.ShapeDtypeStruct((MOE_ROWS, ROW_WORDS), jnp.int32),
        compiler_params=_cparams(1),
        name="moe_experts",
    )(*blk_meta, xs, w_gu, b_gu.reshape(DEPTH, N_EXPERTS, 1, 2 * D_EXPERT),
      w_down, b_down.reshape(DEPTH, N_EXPERTS, 1, D_MODEL))


def _route(route, counts):
    experts = jnp.arange(N_EXPERTS, dtype=jnp.int32)
    padded = (counts + MOE_TM - 1) // MOE_TM * MOE_TM
    pad_end = jnp.cumsum(padded)
    pad_start = pad_end - padded
    e = route[0:TOP_K]
    onehot = e[:, :, None] == experts[None, None, :]
    dest = jnp.sum(jnp.where(onehot, pad_start[None, None, :], 0), axis=-1) + route[TOP_K:2 * TOP_K]
    blk_row0 = jnp.arange(MOE_BLOCKS, dtype=jnp.int32) * MOE_TM
    blk_e = jnp.minimum(jnp.sum((pad_end[None, :] <= blk_row0[:, None]).astype(jnp.int32), axis=1),
                        N_EXPERTS - 1)
    on = blk_row0 < pad_end[-1]
    n_on = jnp.sum(on.astype(jnp.int32))
    blk_onehot = blk_e[:, None] == experts[None, :]
    row_end = jnp.sum(jnp.where(blk_onehot, (pad_start + counts)[None, :], 0), axis=1)
    blk_rows = jnp.where(on, jnp.clip(row_end - blk_row0, 0, MOE_TM), 0).astype(jnp.int32)
    last_e = jnp.sum(jnp.where(jnp.arange(MOE_BLOCKS) == n_on - 1, blk_e, 0))
    blk_e = jnp.where(on, blk_e, last_e).astype(jnp.int32)
    prev = jnp.concatenate([jnp.full((1,), -1, jnp.int32), blk_e[:-1]])
    blk_first = (blk_e != prev).astype(jnp.int32)
    blk_slot = ((jnp.cumsum(blk_first) - 1) % 2).astype(jnp.int32)
    later_used = (experts[None, :] > experts[:, None]) & (counts[None, :] > 0)
    next_used = jnp.min(jnp.where(later_used, experts[None, :], N_EXPERTS), axis=1)
    next_used = jnp.where(next_used == N_EXPERTS, -1, next_used)
    blk_next = jnp.sum(jnp.where(blk_e[:, None] == experts[None, :], next_used[None, :], 0), axis=1)
    blk_io = jnp.minimum(jnp.arange(MOE_BLOCKS, dtype=jnp.int32), n_on - 1)
    return dest, (blk_e, blk_first, blk_rows, blk_slot, blk_next.astype(jnp.int32), blk_io)


def _combine_kernel(y_ref, gate_ref, x1_ref, mod_ref, g_ref, *o_refs):
    i = pl.program_id(0)
    m = mod_ref[0]
    gate = gate_ref[...]
    acc_lo = None
    for k in range(TOP_K):
        lo, hi = _unpack_pairs(y_ref[k])
        gk = gate[:, k:k + 1]
        acc_lo = gk * lo if acc_lo is None else acc_lo + gk * lo
        acc_hi = gk * hi if k == 0 else acc_hi + gk * hi
    acc = jnp.concatenate([acc_lo, acc_hi], axis=1)
    out = x1_ref[...] + _rms(acc, g_ref[...][3:4] * m[5:6])
    if len(o_refs) == 1:
        o_refs[0][...] = out
    else:
        @pl.when(i < NPT)
        def _():
            o_refs[0][...] = out

        @pl.when(i >= NPT)
        def _():
            o_refs[1][...] = out


def _combine(yg, gates, x1, mod_l, g_l, split_out):
    if split_out:
        out_specs = [_PROMPT_SPEC, _SAMPLE_SPEC]
        out_shape = [jax.ShapeDtypeStruct((NP_TOK, D_MODEL), F32), jax.ShapeDtypeStruct((NS_TOK, D_MODEL), F32)]
    else:
        out_specs = [_tok_spec(D_MODEL)]
        out_shape = [jax.ShapeDtypeStruct((N_TOK, D_MODEL), F32)]
    return pl.pallas_call(
        _combine_kernel,
        grid=(N_TILES,),
        in_specs=[pl.BlockSpec((TOP_K, TM, ROW_WORDS), lambda i: (0, i, 0)), _tok_spec(LANES),
                  _tok_spec(D_MODEL), _MOD_SPEC, _const_spec((4, D_MODEL))],
        out_specs=out_specs,
        out_shape=out_shape,
        compiler_params=_cparams(1),
        name="combine",
    )(yg, gates, x1, mod_l, g_l)


def _ffn(layer, o_cat, xs, mod_l, g_l, w_out, w_router, b_router, w_gu, b_gu, w_down, b_down, split_out):
    w_r = jnp.pad(w_router, ((0, 0), (0, LANES - N_EXPERTS))).astype(BF16)
    b_r = jnp.pad(b_router, (0, LANES - N_EXPERTS)).reshape(1, LANES)
    x1, h2p, route, gate_slab, counts = _postmix(o_cat, xs, mod_l, g_l, w_out.astype(BF16), w_r, b_r)
    dest, blk_meta = _route(route, counts[0, :N_EXPERTS])
    n_chunks = N_TOK // (SC_WORKERS * SC_ROWS)
    idx_d = dest.reshape(TOP_K, SC_WORKERS, n_chunks, SC_ROWS).transpose(1, 2, 0, 3).reshape(
        SC_WORKERS, n_chunks * TOP_K, SC_ROWS)
    rows = _sc_dispatch(h2p, idx_d)
    ys = _moe(layer, blk_meta, rows, w_gu, b_gu, w_down, b_down)
    idx_c = dest.reshape(SC_WORKERS, TOP_K * n_chunks, SC_ROWS)
    yg = _sc_collect(ys, idx_c).reshape(TOP_K, N_TOK, ROW_WORDS)
    return _combine(yg, gate_slab, x1, mod_l, g_l, split_out)


def _pad_heads(w, n_heads, width, keep):
    k = w.shape[0]
    w = w.reshape(k, n_heads, width)[:, :, :keep]
    return jnp.pad(w, ((0, 0), (0, 0), (0, LANES - keep))).reshape(k, n_heads * LANES)


def _pe_slab(x):
    return jnp.pad(x, [(0, 0)] * (x.ndim - 1) + [(MLA_NOPE, LANES - MLA_NOPE - MLA_ROPE)])


def _pair_kv_heads(w):
    g = SWA_HEADS // SWA_KV_HEADS
    return w.reshape(SWA_KV_HEADS, g, HEAD_DIM, -1).transpose(1, 0, 2, 3).reshape(w.shape)


def kernel(x_prompt, x_sample, cache_mla_ckv, cache_mla_krope, cache_diff_k, cache_diff_v, cache_swa_k, cache_swa_v, cache_na_k, cache_na_v, c, c_ctx, w_mod, b_mod, norm_g, w_in0, mla_q_norm, w_uq, mla_kv_norm, w_ukv, diff_lambda, diff_norm, w_out0, w_in1, swa_sink, na_rpb, w_out1, w_router, b_router, w_gu, b_gu, w_down, b_down):
    xs = (x_prompt.reshape(NP_TOK, D_MODEL), x_sample.reshape(NS_TOK, D_MODEL))
    cond = jnp.concatenate([c_ctx[None, :], c, jnp.zeros((16 - 1 - DEC_BATCH, D_MODEL), F32)], axis=0)
    mod = _modulation(cond, w_mod, b_mod).reshape(DEPTH, 16, 6, D_MODEL)
    t64, t32 = _rope_tables()
    states = {}
    for l in range(DEPTH):
        i = l // 2
        g_l = norm_g[l]
        mod_l = mod[l]
        if l % 2 == 0:
            lam_init = 0.8 - 0.6 * math.exp(-0.3 * l)
            wi = w_in0[i]
            w_in_p = jnp.concatenate(
                [wi[:, 0:640], wi[:, 672:2208], _pe_slab(wi[:, 640:672])], axis=1).astype(BF16)
            w_uq_p = _pad_heads(w_uq[i], MLA_HEADS, MLA_NOPE + MLA_ROPE, MLA_NOPE + MLA_ROPE).astype(BF16)
            w_k_p = _pad_heads(w_ukv[i], MLA_HEADS, MLA_NOPE + MLA_V, MLA_NOPE).astype(BF16)
            w_v = w_ukv[i].reshape(MLA_KV_RANK, MLA_HEADS, MLA_NOPE + MLA_V)[:, :, MLA_NOPE:].reshape(
                MLA_KV_RANK, MLA_HEADS * MLA_V).astype(BF16)
            (q, k, v, dq, dk, dv, ckv_st, kpe_st, dk_st, dv_st) = _premix0(
                *xs, mod_l, g_l[0:1], w_in_p, mla_q_norm[i][None, :], w_uq_p, mla_kv_norm[i][None, :],
                w_k_p, w_v, t32, t64)
            states['mla_ckv'] = ckv_st.reshape(BATCH, 1, SEQ, MLA_KV_RANK)
            states['mla_krope'] = kpe_st[:, MLA_NOPE:MLA_NOPE + MLA_ROPE].reshape(BATCH, 1, SEQ, MLA_ROPE)
            states['diff_k'] = dk_st.reshape(BATCH, 1, SEQ, DIFF_HEADS, 2 * DIFF_DH)
            states['diff_v'] = dv_st.reshape(BATCH, 1, SEQ, DIFF_HEADS, 2 * DIFF_DH)
            kc, vc = _mla_cache(cache_mla_ckv[:, i].reshape(DEC_BATCH * PAST_LEN, MLA_KV_RANK),
                                _pe_slab(cache_mla_krope[:, i].reshape(DEC_BATCH * PAST_LEN, MLA_ROPE)),
                                w_k_p, w_v)
            dkc = cache_diff_k[:, i].reshape(DEC_BATCH * PAST_LEN, 512).astype(BF16)
            dvc = cache_diff_v[:, i].reshape(DEC_BATCH * PAST_LEN, 512).astype(BF16)
            lam = diff_lambda[i]
            sub_g = diff_norm[i][None, :]
            o_p = _attn_ab(q, dq, (k, v, dk, dv), None, lam, sub_g, lam_init,
                           n_batch=BATCH, t_len=SEQ, tq=SEQ, tok_off=0, n_seq=PROMPT_SEQS_PER_STEP)
            o_cat = _attn_ab(q, dq, (k, v, dk, dv), (kc, vc, dkc, dvc), lam, sub_g, lam_init,
                             n_batch=DEC_BATCH, t_len=DEC_SEQ, tq=TQ_AB, tok_off=NP_TOK, out_init=o_p)
            w_out = w_out0[i]
        else:
            wi = w_in1[i]
            n_sq = SWA_HEADS * HEAD_DIM
            w_in_p = jnp.concatenate([_pair_kv_heads(wi[:, :n_sq].T).T, wi[:, n_sq:]], axis=1).astype(BF16)
            (sq, sk, sv, nq, nk, nv, sk_st, sv_st, nk_st, nv_st) = _premix1(xs[0], mod_l, g_l[0:1], w_in_p, t64)
            states['swa_k'] = sk_st.reshape(BATCH, 1, SEQ, SWA_KV_HEADS, HEAD_DIM)
            states['swa_v'] = sv_st.reshape(BATCH, 1, SEQ, SWA_KV_HEADS, HEAD_DIM)
            states['na_k'] = nk_st.reshape(BATCH, 1, SEQ, NA_HEADS, HEAD_DIM)
            states['na_v'] = nv_st.reshape(BATCH, 1, SEQ, NA_HEADS, HEAD_DIM)
            skc = cache_swa_k[:, i].reshape(DEC_BATCH * PAST_LEN, 128).astype(BF16)
            svc = cache_swa_v[:, i].reshape(DEC_BATCH * PAST_LEN, 128).astype(BF16)
            nkc = cache_na_k[:, i].reshape(DEC_BATCH * PAST_LEN, 512).astype(BF16)
            nvc = cache_na_v[:, i].reshape(DEC_BATCH * PAST_LEN, 512).astype(BF16)
            sink = swa_sink[i]
            o_p = _attn_cd_prompt(sink, sq, sk, sv, nq, nk, nv)
            o_cat = _attn_cd_sample(o_p, sink, sq, sk, sv, nq, nk, nv, skc, svc, nkc, nvc, _na_bias(na_rpb[i]))
            wo = w_out1[i]
            w_out = jnp.concatenate([_pair_kv_heads(wo[:n_sq]), wo[n_sq:]], axis=0)
        xs = _ffn(l, o_cat, xs, mod_l, g_l, w_out, w_router[l], b_router[l], w_gu, b_gu, w_down, b_down,
                  split_out=(l == DEPTH - 1))
    return (xs[0].reshape(BATCH, SEQ, D_MODEL), xs[1].reshape(DEC_BATCH, DEC_SEQ, D_MODEL),
            states['mla_ckv'], states['mla_krope'], states['diff_k'], states['diff_v'],
            states['swa_k'], states['swa_v'], states['na_k'], states['na_v'])
```

```python
import functools
import math

import numpy as np
import jax
import jax.numpy as jnp
from jax import lax
from jax.experimental import pallas as pl
from jax.experimental.pallas import tpu as pltpu
from jax.experimental.pallas import tpu_sc as plsc

F32 = jnp.float32
BF16 = jnp.bfloat16

D_MODEL = 1024
BATCH = 16
SEQ = 256
DEPTH = 2
DEC_BATCH = 8
DEC_SEQ = 2048
PAST_LEN = 256
GRID_W = 64
HEAD_DIM = 64
ROPE_THETA = 10000.0
EPS = 1e-6
NEG = -1e30

MLA_HEADS = 8
MLA_Q_RANK = 384
MLA_KV_RANK = 256
MLA_NOPE = 64
MLA_ROPE = 32
MLA_V = 64
DIFF_HEADS = 4
DIFF_DH = 64
SWA_HEADS = 8
SWA_KV_HEADS = 2
SWA_WINDOW = 128
NA_HEADS = 8
NA_WIN_ROWS = 8
NA_WIN_COLS = 16
N_EXPERTS = 32
TOP_K = 4
D_EXPERT = 1024
SWIGLU_LIMIT = 7.0
SWIGLU_ALPHA = 1.702

LANES = 128
NP_TOK = BATCH * SEQ
NS_TOK = DEC_BATCH * DEC_SEQ
N_TOK = NP_TOK + NS_TOK
TM = 512
NPT = NP_TOK // TM
TILES_PER_SAMPLE = DEC_SEQ // TM
N_TILES = N_TOK // TM
N_SUB = 2
SUB_TM = TM // N_SUB
TQ = 256
TQ_AB = 256
PROMPT_SEQS_PER_STEP = 4
MOE_TM = 1024
MOE_TAIL_PARTS = 8
MOE_ROWS = ((N_TOK * TOP_K + N_EXPERTS * (MOE_TM - 1)) // MOE_TM + 1) * MOE_TM
MOE_BLOCKS = MOE_ROWS // MOE_TM
NA_TILE_ROWS = TQ // GRID_W
NA_KEY_ROWS = 12
VMEM_LIMIT = 56 * 1024 * 1024


def _cparams(n_axes, vmem=VMEM_LIMIT):
    return pltpu.CompilerParams(dimension_semantics=("arbitrary",) * n_axes,
                                vmem_limit_bytes=vmem)


def _rms(x, g):
    return x * lax.rsqrt(jnp.mean(x * x, axis=-1, keepdims=True) + EPS) * g


def _dot(a, b):
    return jnp.dot(a, b, preferred_element_type=F32)


def _dot_nt(a, b):
    return lax.dot_general(a, b, (((1,), (1,)), ((), ())), preferred_element_type=F32)


def _rope(x, cos, sin_a, sin_b, half):
    return (x * cos + pltpu.roll(x, LANES - half, 1) * sin_a + pltpu.roll(x, half, 1) * sin_b)


def _pipelined_units(units):
    pending = units[0][0](units[0][2])
    for n, (_, finish, arg) in enumerate(units):
        following = units[n + 1][0](units[n + 1][2]) if n + 1 < len(units) else None
        finish(arg, pending)
        pending = following


def _sub_rows(n):
    return slice(SUB_TM * n, SUB_TM * (n + 1))


def _store_head_rows(st_ref, n, hd, n_heads, x):
    st_ref[pl.ds(SUB_TM * n * n_heads + hd, SUB_TM, stride=n_heads), :] = x


def _mod_row(i):
    return jnp.where(i < NPT, 0, 1 + (i - NPT) // TILES_PER_SAMPLE)


def _rope_blk(i):
    return jnp.where(i < NPT, TILES_PER_SAMPLE, (i - NPT) % TILES_PER_SAMPLE)


def _mod_kernel(c_ref, w_ref, b_ref, o_ref):
    c = c_ref[...]
    s = (c * jax.nn.sigmoid(c)).astype(BF16)
    o_ref[0] = _dot(s, w_ref[0].astype(BF16)) + b_ref[0]


def _modulation(cond, w_mod, b_mod):
    nb = 1024
    return pl.pallas_call(
        _mod_kernel,
        grid=(DEPTH, 6 * D_MODEL // nb),
        in_specs=[
            pl.BlockSpec((16, D_MODEL), lambda l, n: (0, 0)),
            pl.BlockSpec((1, D_MODEL, nb), lambda l, n: (l, 0, n)),
            pl.BlockSpec((1, 1, nb), lambda l, n: (l, 0, n)),
        ],
        out_specs=pl.BlockSpec((1, 16, nb), lambda l, n: (l, 0, n)),
        out_shape=jax.ShapeDtypeStruct((DEPTH, 16, 6 * D_MODEL), F32),
        compiler_params=_cparams(2),
        name="modulation",
    )(cond, w_mod, b_mod.reshape(DEPTH, 1, 6 * D_MODEL))


def _rope_tables():
    t = jnp.arange(DEC_SEQ)
    rows = (t // GRID_W).astype(F32)
    cols = (t % GRID_W).astype(F32)

    def angles(r):
        n = r // 4
        inv = ROPE_THETA ** (-jnp.arange(n, dtype=F32) / n)
        return jnp.concatenate([rows[:, None] * inv[None], cols[:, None] * inv[None]], axis=-1)

    def finish(cos, sa, sb):
        ident = (jnp.ones((TM, LANES), F32), jnp.zeros((TM, LANES), F32), jnp.zeros((TM, LANES), F32))
        return tuple(jnp.concatenate([a, b], axis=0) for a, b in zip((cos, sa, sb), ident))

    a64 = angles(64)
    c, s, z = jnp.cos(a64), jnp.sin(a64), jnp.zeros_like(a64)
    t64 = finish(jnp.concatenate([c, c, c, c], -1), jnp.concatenate([-s, z, -s, z], -1),
                 jnp.concatenate([z, s, z, s], -1))
    a32 = angles(32)
    c, s, z = jnp.cos(a32), jnp.sin(a32), jnp.zeros_like(a32)
    one64 = jnp.ones((DEC_SEQ, 64), F32)
    z64 = jnp.zeros((DEC_SEQ, 64), F32)
    z32 = jnp.zeros((DEC_SEQ, 32), F32)
    t32 = finish(jnp.concatenate([one64, c, c, z32], -1), jnp.concatenate([z64, -s, z, z32], -1),
                 jnp.concatenate([z64, z, s, z32], -1))
    return t64, t32


LOG2E = math.log2(math.e)
_DIFF_COLS = DIFF_HEADS * 2 * DIFF_DH
_AB_COLS = tuple(np.cumsum([0, MLA_Q_RANK, MLA_KV_RANK, _DIFF_COLS, _DIFF_COLS, _DIFF_COLS, LANES]).tolist())
_CD_COLS = tuple(np.cumsum([0, SWA_HEADS * HEAD_DIM, SWA_KV_HEADS * HEAD_DIM, SWA_KV_HEADS * HEAD_DIM,
                            NA_HEADS * HEAD_DIM, NA_HEADS * HEAD_DIM, NA_HEADS * HEAD_DIM]).tolist())
IN_COLS = _AB_COLS[-1]
assert IN_COLS == _CD_COLS[-1]
_MLA_SCALE = (MLA_NOPE + MLA_ROPE) ** -0.5 * LOG2E
_QSCALE = HEAD_DIM ** -0.5 * LOG2E


def _premix0_kernel(xp_ref, xs_ref, mod_ref, g_ref, win_ref, qn_ref, wuq_ref, kvn_ref, wk_ref, wv_ref,
                    c32_ref, sa32_ref, sb32_ref, c64_ref, sa64_ref, sb64_ref,
                    q_ref, k_ref, v_ref, dq_ref, dk_ref, dv_ref,
                    ckv_st, kpe_st, dk_st, dv_st):
    i = pl.program_id(0)
    m = mod_ref[0]
    gain = g_ref[...] * (1.0 + m[1:2])

    def project(n):
        r = _sub_rows(n)
        x = jnp.where(i < NPT, xp_ref[r, :], xs_ref[r, :])
        h = _rms(x, gain) + m[0:1]
        return _dot(h.astype(BF16), win_ref[...])

    states = {}

    def finish(n, proj):
        r = _sub_rows(n)
        q_a, kv_a, dq, dk, dv, pe = (proj[:, a:b] for a, b in zip(_AB_COLS[:-1], _AB_COLS[1:]))
        q = _dot(_rms(q_a, qn_ref[...]).astype(BF16), wuq_ref[...])
        ckv = _rms(kv_a, kvn_ref[...])
        ckv_b = ckv.astype(BF16)
        kn = _dot(ckv_b, wk_ref[...])
        v_ref[r, :] = _dot(ckv_b, wv_ref[...]).astype(BF16)
        c32, sa32, sb32 = c32_ref[r, :], sa32_ref[r, :], sb32_ref[r, :]
        c64, sa64, sb64 = c64_ref[r, :], sa64_ref[r, :], sb64_ref[r, :]
        pe_r = _rope(pe, c32, sa32, sb32, MLA_ROPE // 2)
        for hd in range(MLA_HEADS):
            sl = slice(LANES * hd, LANES * (hd + 1))
            q_ref[r, sl] = (_rope(q[:, sl], c32, sa32, sb32, MLA_ROPE // 2) * _MLA_SCALE).astype(BF16)
            k_ref[r, sl] = (kn[:, sl] + pe_r).astype(BF16)
        for hd in range(DIFF_HEADS):
            sl = slice(LANES * hd, LANES * (hd + 1))
            dq_ref[r, sl] = (_rope(dq[:, sl], c64, sa64, sb64, DIFF_DH // 2) * _QSCALE).astype(BF16)
            dk_ref[r, sl] = _rope(dk[:, sl], c64, sa64, sb64, DIFF_DH // 2).astype(BF16)
        dv_ref[r, :] = dv.astype(BF16)
        states[n] = (ckv, pe, dk, dv)

    _pipelined_units([(project, finish, n) for n in range(N_SUB)])

    @pl.when(i < NPT)
    def _():
        for n in range(N_SUB):
            r = _sub_rows(n)
            ckv, pe, dk, dv = states[n]
            ckv_st[r, :] = ckv
            kpe_st[r, :] = pe
            for hd in range(DIFF_HEADS):
                _store_head_rows(dk_st, n, hd, DIFF_HEADS, dk[:, LANES * hd: LANES * (hd + 1)])
                _store_head_rows(dv_st, n, hd, DIFF_HEADS, dv[:, LANES * hd: LANES * (hd + 1)])


def _premix1_kernel(x_ref, mod_ref, g_ref, win_ref, c64_ref, sa64_ref, sb64_ref,
                    sq_ref, sk_ref, sv_ref, nq_ref, nk_ref, nv_ref,
                    sk_st, sv_st, nk_st, nv_st):
    i = pl.program_id(0)
    m = mod_ref[0]
    gain = g_ref[...] * (1.0 + m[1:2])

    def project(n):
        h = _rms(x_ref[_sub_rows(n), :], gain) + m[0:1]
        return _dot(h.astype(BF16), win_ref[...])

    states = {}

    def finish(n, proj):
        r = _sub_rows(n)
        sq, sk, sv, nq, nk, nv = (proj[:, a:b] for a, b in zip(_CD_COLS[:-1], _CD_COLS[1:]))
        c64, sa64, sb64 = c64_ref[r, :], sa64_ref[r, :], sb64_ref[r, :]
        for hd in range(4):
            sl = slice(LANES * hd, LANES * (hd + 1))
            sq_ref[r, sl] = (_rope(sq[:, sl], c64, sa64, sb64, HEAD_DIM // 2) * _QSCALE).astype(BF16)
        sk_ref[r, :] = _rope(sk, c64, sa64, sb64, HEAD_DIM // 2).astype(BF16)
        sv_ref[r, :] = sv.astype(BF16)
        nq_ref[r, :] = (nq * _QSCALE).astype(BF16)
        nk_ref[r, :] = nk.astype(BF16)
        nv_ref[r, :] = nv.astype(BF16)
        states[n] = (sk, sv, nk, nv)

    _pipelined_units([(project, finish, n) for n in range(N_SUB)])

    @pl.when(i < NPT)
    def _():
        for n in range(N_SUB):
            r = _sub_rows(n)
            sk, sv, nk, nv = states[n]
            sk_st[r, :] = sk
            sv_st[r, :] = sv
            for hd in range(NA_HEADS):
                _store_head_rows(nk_st, n, hd, NA_HEADS, nk[:, HEAD_DIM * hd: HEAD_DIM * (hd + 1)])
                _store_head_rows(nv_st, n, hd, NA_HEADS, nv[:, HEAD_DIM * hd: HEAD_DIM * (hd + 1)])


def _tok_spec(width):
    return pl.BlockSpec((TM, width), lambda i: (i, 0))


_PROMPT_SPEC = pl.BlockSpec((TM, D_MODEL), lambda i: (jnp.minimum(i, NPT - 1), 0))
_SAMPLE_SPEC = pl.BlockSpec((TM, D_MODEL), lambda i: (jnp.maximum(i - NPT, 0), 0))


def _state_spec(width, rows_per_token=1):
    return pl.BlockSpec((TM * rows_per_token, width), lambda i: (jnp.minimum(i, NPT - 1), 0))


def _const_spec(shape):
    return pl.BlockSpec(shape, lambda i: (0,) * len(shape))


_MOD_SPEC = pl.BlockSpec((1, 6, D_MODEL), lambda i: (_mod_row(i), 0, 0))
_ROPE_SPEC = pl.BlockSpec((TM, LANES), lambda i: (_rope_blk(i), 0))


def _premix0(xp, xs, mod_l, g0, w_in_p, q_norm, w_uq_p, kv_norm, w_k_p, w_v, t32, t64):
    outs = [(N_TOK, 1024, BF16), (N_TOK, 1024, BF16), (N_TOK, 512, BF16), (N_TOK, 512, BF16),
            (N_TOK, 512, BF16), (N_TOK, 512, BF16),
            (NP_TOK, 256, F32), (NP_TOK, 128, F32),
            (NP_TOK * DIFF_HEADS, 2 * DIFF_DH, F32), (NP_TOK * DIFF_HEADS, 2 * DIFF_DH, F32)]
    return pl.pallas_call(
        _premix0_kernel,
        grid=(N_TILES,),
        in_specs=[_PROMPT_SPEC, _SAMPLE_SPEC, _MOD_SPEC, _const_spec((1, D_MODEL)),
                  _const_spec((D_MODEL, IN_COLS)), _const_spec((1, MLA_Q_RANK)),
                  _const_spec((MLA_Q_RANK, 1024)), _const_spec((1, MLA_KV_RANK)),
                  _const_spec((MLA_KV_RANK, 1024)), _const_spec((MLA_KV_RANK, 512))]
                 + [_ROPE_SPEC] * 6,
        out_specs=([_tok_spec(w) for (_, w, _) in outs[:6]]
                   + [_state_spec(w, n // NP_TOK) for (n, w, _) in outs[6:]]),
        out_shape=[jax.ShapeDtypeStruct((n, w), dt) for (n, w, dt) in outs],
        compiler_params=_cparams(1),
        name="premix_ab",
    )(xp, xs, mod_l, g0, w_in_p, q_norm, w_uq_p, kv_norm, w_k_p, w_v, *t32, *t64)


def _premix1(x, mod_l, g0, w_in_p, t64):
    outs = [(N_TOK, 512, BF16), (N_TOK, 128, BF16), (N_TOK, 128, BF16), (N_TOK, 512, BF16),
            (N_TOK, 512, BF16), (N_TOK, 512, BF16),
            (NP_TOK, 128, F32), (NP_TOK, 128, F32),
            (NP_TOK * NA_HEADS, HEAD_DIM, F32), (NP_TOK * NA_HEADS, HEAD_DIM, F32)]
    return pl.pallas_call(
        _premix1_kernel,
        grid=(N_TILES,),
        in_specs=[_tok_spec(D_MODEL), _MOD_SPEC, _const_spec((1, D_MODEL)),
                  _const_spec((D_MODEL, IN_COLS))] + [_ROPE_SPEC] * 3,
        out_specs=([_tok_spec(w) for (_, w, _) in outs[:6]]
                   + [_state_spec(w, n // NP_TOK) for (n, w, _) in outs[6:]]),
        out_shape=[jax.ShapeDtypeStruct((n, w), dt) for (n, w, dt) in outs],
        compiler_params=_cparams(1),
        name="premix_cd",
    )(x, mod_l, g0, w_in_p, *t64)


def _mla_cache_kernel(ckv_ref, pe_ref, wk_ref, wv_ref, k_ref, v_ref):
    c = ckv_ref[...].astype(BF16)
    kn = _dot(c, wk_ref[...])
    v_ref[...] = _dot(c, wv_ref[...]).astype(BF16)
    pe = pe_ref[...]
    for hd in range(MLA_HEADS):
        sl = slice(LANES * hd, LANES * (hd + 1))
        k_ref[:, sl] = (kn[:, sl] + pe).astype(BF16)


def _mla_cache(ckv, pe_slab, w_k_p, w_v):
    n = ckv.shape[0]
    tm = 512
    return pl.pallas_call(
        _mla_cache_kernel,
        grid=(n // tm,),
        in_specs=[pl.BlockSpec((tm, MLA_KV_RANK), lambda i: (i, 0)),
                  pl.BlockSpec((tm, LANES), lambda i: (i, 0)),
                  _const_spec((MLA_KV_RANK, 1024)), _const_spec((MLA_KV_RANK, 512))],
        out_specs=[pl.BlockSpec((tm, 1024), lambda i: (i, 0)), pl.BlockSpec((tm, 512), lambda i: (i, 0))],
        out_shape=[jax.ShapeDtypeStruct((n, 1024), BF16), jax.ShapeDtypeStruct((n, 512), BF16)],
        compiler_params=_cparams(1),
        name="mla_cache",
    )(ckv, pe_slab, w_k_p, w_v)


def _softmax_pv(scores, values, sink=None):
    m = jnp.max(scores[0], axis=-1, keepdims=True)
    for s in scores[1:]:
        m = jnp.maximum(m, jnp.max(s, axis=-1, keepdims=True))
    if sink is not None:
        m = jnp.maximum(m, sink)
    l = None
    o = None
    for s, v in zip(scores, values):
        p = jnp.exp2(s - m)
        ls = jnp.sum(p, axis=-1, keepdims=True)
        os_ = _dot(p.astype(BF16), v)
        l = ls if l is None else l + ls
        o = os_ if o is None else o + os_
    if sink is not None:
        l = l + jnp.exp2(sink - m)
    return o * (1.0 / l)


def _lane_lo(shape):
    return lax.broadcasted_iota(jnp.int32, shape, 1) < (LANES // 2)


def _split_halves(qb):
    lo = _lane_lo(qb.shape)
    zero = jnp.zeros_like(qb)
    return jnp.where(lo, qb, zero), jnp.where(lo, zero, qb)


def _attn_ab_kernel(*refs, n_pieces, n_seq, lam_init, aliased):
    if aliased:
        refs = refs[1:]
    q_ref, dq_ref = refs[0], refs[1]
    pieces = [refs[2 + 4 * p: 6 + 4 * p] for p in range(n_pieces)]
    lam_ref, subg_ref, o_ref = refs[2 + 4 * n_pieces:]
    lam = lam_ref[...]
    lam_full = (jnp.exp(jnp.sum(lam[0:1] * lam[1:2], axis=-1, keepdims=True))
                - jnp.exp(jnp.sum(lam[2:3] * lam[3:4], axis=-1, keepdims=True)) + lam_init)
    tq = q_ref.shape[0] // n_seq
    lo = _lane_lo((tq, LANES))
    subg = subg_ref[...] * (1.0 - lam_init)

    def seq_rows(ref, sq):
        n = ref.shape[0] // n_seq
        return slice(n * sq, n * (sq + 1))

    def mla_scores(arg):
        sq, hd = arg
        sl = slice(LANES * hd, LANES * (hd + 1))
        qh = q_ref[seq_rows(q_ref, sq), sl]
        return [_dot_nt(qh, k_ref[seq_rows(k_ref, sq), sl]) for (k_ref, _, _, _) in pieces]

    def diff_scores(arg):
        sq, hd = arg
        sl = slice(LANES * hd, LANES * (hd + 1))
        qq = jnp.concatenate(_split_halves(dq_ref[seq_rows(dq_ref, sq), sl]), axis=0)
        return [_dot_nt(qq, dk_ref[seq_rows(dk_ref, sq), sl]) for (_, _, dk_ref, _) in pieces]

    pair = {}

    def mla_finish(arg, scores):
        sq, hd = arg
        j = hd // 2
        vals = [v_ref[seq_rows(v_ref, sq), LANES * j: LANES * (j + 1)] for (_, v_ref, _, _) in pieces]
        pair[hd % 2] = _softmax_pv(scores, vals)
        if hd % 2 == 1:
            o_ref[seq_rows(o_ref, sq), LANES * j: LANES * (j + 1)] = jnp.where(lo, pair[0], pair[1]).astype(BF16)

    def diff_finish(arg, scores):
        sq, hd = arg
        sl = slice(LANES * hd, LANES * (hd + 1))
        oo = _softmax_pv(scores, [dv_ref[seq_rows(dv_ref, sq), sl] for (_, _, _, dv_ref) in pieces])
        od = _rms(oo[:tq] - lam_full * oo[tq:], subg)
        o_ref[seq_rows(o_ref, sq), 512 + LANES * hd: 512 + LANES * (hd + 1)] = od.astype(BF16)

    units = []
    for sq in range(n_seq):
        for j in range(DIFF_HEADS):
            units += [(diff_scores, diff_finish, (sq, j)), (mla_scores, mla_finish, (sq, 2 * j)),
                      (mla_scores, mla_finish, (sq, 2 * j + 1))]
    _pipelined_units(units)


def _attn_ab(q, dq, new_kv, cache_kv, lam, sub_g, lam_init, *, n_batch, t_len, tq, tok_off, out_init=None,
             n_seq=1):
    assert n_seq == 1 or (tq == t_len and cache_kv is None and n_batch % n_seq == 0)
    n_batch, t_len, tq = n_batch // n_seq, t_len * n_seq, tq * n_seq
    nq = t_len // tq
    q_off = tok_off // tq
    b_off = tok_off // t_len
    widths = (1024, 512, 512, 512)
    in_specs = [pl.BlockSpec((tq, 1024), lambda b, i: (q_off + b * nq + i, 0)),
                pl.BlockSpec((tq, 512), lambda b, i: (q_off + b * nq + i, 0))]
    args = [q, dq]
    for w, a in zip(widths, new_kv):
        in_specs.append(pl.BlockSpec((t_len, w), lambda b, i: (b_off + b, 0)))
        args.append(a)
    n_pieces = 1
    if cache_kv is not None:
        n_pieces = 2
        for w, a in zip(widths, cache_kv):
            in_specs.append(pl.BlockSpec((PAST_LEN, w), lambda b, i: (b, 0)))
            args.append(a)
    in_specs += [pl.BlockSpec((4, DIFF_DH), lambda b, i: (0, 0)),
                 pl.BlockSpec((1, 2 * DIFF_DH), lambda b, i: (0, 0))]
    args += [lam, sub_g]
    aliases = {}
    if out_init is not None:
        in_specs = [pl.BlockSpec(memory_space=pl.ANY)] + in_specs
        args = [out_init] + args
        aliases = {0: 0}
    return pl.pallas_call(
        functools.partial(_attn_ab_kernel, n_pieces=n_pieces, n_seq=n_seq, lam_init=lam_init,
                          aliased=out_init is not None),
        grid=(n_batch, nq),
        in_specs=in_specs,
        out_specs=pl.BlockSpec((tq, 1024), lambda b, i: (q_off + b * nq + i, 0)),
        out_shape=jax.ShapeDtypeStruct((N_TOK, 1024), BF16),
        input_output_aliases=aliases,
        compiler_params=_cparams(2),
        name="attn_ab_%d" % n_pieces,
    )(*args)


def _gqa_stacks(sq_ref, sink_ref):
    tq = sq_ref.shape[0]
    halves = [_split_halves(sq_ref[:, LANES * j: LANES * (j + 1)]) for j in range(4)]
    q_stacks = [jnp.concatenate([halves[j][kvh] for j in range(4)], axis=0) for kvh in range(SWA_KV_HEADS)]
    sinks = [jnp.concatenate([jnp.full((tq, 1), sink_ref[4 * kvh + j] * LOG2E, F32) for j in range(4)], axis=0)
             for kvh in range(SWA_KV_HEADS)]
    return q_stacks, sinks


def _attn_cd_prompt_kernel(sink_ref, sq_ref, sk_ref, sv_ref, nq_ref, nk_ref, nv_ref, o_ref):
    lo = _lane_lo((SEQ, LANES))
    for sq in range(PROMPT_SEQS_PER_STEP):
        r = slice(SEQ * sq, SEQ * (sq + 1))
        sk = sk_ref[r, :]
        sv = sv_ref[r, :]
        for j in range(4):
            sl = slice(LANES * j, LANES * (j + 1))
            q_lo, q_hi = _split_halves(sq_ref[r, sl])
            o_lo = _softmax_pv([_dot_nt(q_lo, sk)], [sv], sink=sink_ref[j] * LOG2E)
            o_hi = _softmax_pv([_dot_nt(q_hi, sk)], [sv], sink=sink_ref[j + 4] * LOG2E)
            o_ref[r, sl] = jnp.where(lo, o_lo, o_hi).astype(BF16)
        for j in range(4):
            sl = slice(LANES * j, LANES * (j + 1))
            q_lo, q_hi = _split_halves(nq_ref[r, sl])
            k = nk_ref[r, sl]
            v = nv_ref[r, sl]
            o_lo = _softmax_pv([_dot_nt(q_lo, k)], [v])
            o_hi = _softmax_pv([_dot_nt(q_hi, k)], [v])
            o_ref[r, 512 + LANES * j: 512 + LANES * (j + 1)] = jnp.where(lo, o_lo, o_hi).astype(BF16)


def _attn_cd_prompt(sink, sq, sk, sv, nq, nk, nv):
    def spec(w):
        return pl.BlockSpec((SEQ * PROMPT_SEQS_PER_STEP, w), lambda b: (b, 0))
    return pl.pallas_call(
        _attn_cd_prompt_kernel,
        grid=(BATCH // PROMPT_SEQS_PER_STEP,),
        in_specs=[pl.BlockSpec(memory_space=pltpu.SMEM), spec(512), spec(128), spec(128),
                  spec(512), spec(512), spec(512)],
        out_specs=spec(1024),
        out_shape=jax.ShapeDtypeStruct((N_TOK, 1024), BF16),
        compiler_params=_cparams(1),
        name="attn_cd_prompt",
    )(sink, sq, sk, sv, nq, nk, nv)


_SWA_KEYS = TQ + 2 * SWA_WINDOW


def _attn_cd_sample_kernel(init_ref, sink_ref, sq_ref, nq_ref, sk_ref, sv_ref, nk_ref, nv_ref,
                           skc_ref, svc_ref, nkc_ref, nvc_ref, bias_ref, o_ref):
    del init_ref
    qi = pl.program_id(1)
    lo = _lane_lo((TQ, LANES))
    ks = pl.multiple_of(jnp.clip(qi * TQ - SWA_WINDOW, 0, DEC_SEQ - _SWA_KEYS), SWA_WINDOW)
    k_win = sk_ref[pl.ds(ks, _SWA_KEYS), :]
    v_win = sv_ref[pl.ds(ks, _SWA_KEYS), :]
    q_pos = qi * TQ + (lax.broadcasted_iota(jnp.int32, (4 * TQ, _SWA_KEYS), 0) & (TQ - 1))
    k_pos = ks + lax.broadcasted_iota(jnp.int32, (4 * TQ, _SWA_KEYS), 1)
    in_win = jnp.abs(q_pos - k_pos) <= SWA_WINDOW
    skc = skc_ref[...]
    svc = svc_ref[...]
    q_stacks, sinks = _gqa_stacks(sq_ref, sink_ref)

    def window_scores(kvh):
        return [_dot_nt(q_stacks[kvh], skc), jnp.where(in_win, _dot_nt(q_stacks[kvh], k_win), NEG)]

    o_kv = {}

    def window_finish(kvh, scores):
        o_kv[kvh] = _softmax_pv(scores, [svc, v_win], sink=sinks[kvh])
        if kvh == SWA_KV_HEADS - 1:
            for j in range(4):
                rows = slice(TQ * j, TQ * (j + 1))
                o_ref[:, LANES * j: LANES * (j + 1)] = jnp.where(lo, o_kv[0][rows], o_kv[1][rows]).astype(BF16)

    n_rows = DEC_SEQ // GRID_W
    r0 = jnp.clip(qi * NA_TILE_ROWS - NA_WIN_ROWS // 2, 0, n_rows - NA_KEY_ROWS)
    kn = pl.multiple_of(r0 * GRID_W, GRID_W)

    lane_lo = _lane_lo((1, LANES))
    pieces = []
    for ri in range(NA_TILE_ROWS):
        r = qi * NA_TILE_ROWS + ri
        rs = jnp.clip(r - NA_WIN_ROWS // 2, 0, n_rows - NA_WIN_ROWS)
        row = []
        for mm in range(NA_KEY_ROWS // 2):
            kr = r0 + 2 * mm
            ok = [(kr + t >= rs) & (kr + t < rs + NA_WIN_ROWS) for t in range(2)]
            mask = jnp.where(lane_lo, jnp.where(ok[0], 0.0, NEG), jnp.where(ok[1], 0.0, NEG))
            row.append((jnp.clip(kr - r + NA_WIN_ROWS, 0, 2 * NA_WIN_ROWS - 1), mask))
        pieces.append(row)

    def add_na_bias(hd, s):
        return jnp.concatenate(
            [jnp.concatenate([s[GRID_W * ri: GRID_W * (ri + 1), LANES * mm: LANES * (mm + 1)] + bias_ref[hd, d] + mask
                              for mm, (d, mask) in enumerate(row)], axis=1)
             for ri, row in enumerate(pieces)], axis=0)

    def na_scores(hd):
        sl = slice(LANES * (hd // 2), LANES * (hd // 2 + 1))
        q_half = _split_halves(nq_ref[:, sl])[hd % 2]
        nk_win = nk_ref[pl.ds(kn, NA_KEY_ROWS * GRID_W), sl]
        return [_dot_nt(q_half, nkc_ref[:, sl]), add_na_bias(hd, _dot_nt(q_half, nk_win))]

    pair = {}

    def na_finish(hd, scores):
        j = hd // 2
        sl = slice(LANES * j, LANES * (j + 1))
        nv_win = nv_ref[pl.ds(kn, NA_KEY_ROWS * GRID_W), sl]
        pair[hd % 2] = _softmax_pv(scores, [nvc_ref[:, sl], nv_win])
        if hd % 2 == 1:
            o_ref[:, 512 + LANES * j: 512 + LANES * (j + 1)] = jnp.where(lo, pair[0], pair[1]).astype(BF16)

    _pipelined_units([(window_scores, window_finish, kvh) for kvh in range(SWA_KV_HEADS)]
                     + [(na_scores, na_finish, hd) for hd in range(NA_HEADS)])


def _attn_cd_sample(out_init, sink, sq, sk, sv, nq, nk, nv, skc, svc, nkc, nvc, bias):
    nq_t = DEC_SEQ // TQ
    q_off = NP_TOK // TQ
    b_off = NP_TOK // DEC_SEQ

    def qspec(w):
        return pl.BlockSpec((TQ, w), lambda b, i: (q_off + b * nq_t + i, 0))

    def kspec(w):
        return pl.BlockSpec((DEC_SEQ, w), lambda b, i: (b_off + b, 0))

    def cspec(w):
        return pl.BlockSpec((PAST_LEN, w), lambda b, i: (b, 0))

    bias_spec = pl.BlockSpec((NA_HEADS, 2 * NA_WIN_ROWS, GRID_W, LANES), lambda b, i: (0, 0, 0, 0))
    return pl.pallas_call(
        _attn_cd_sample_kernel,
        grid=(DEC_BATCH, nq_t),
        in_specs=[pl.BlockSpec(memory_space=pl.ANY), pl.BlockSpec(memory_space=pltpu.SMEM),
                  qspec(512), qspec(512), kspec(128), kspec(128), kspec(512), kspec(512),
                  cspec(128), cspec(128), cspec(512), cspec(512), bias_spec],
        out_specs=pl.BlockSpec((TQ, 1024), lambda b, i: (q_off + b * nq_t + i, 0)),
        out_shape=jax.ShapeDtypeStruct((N_TOK, 1024), BF16),
        input_output_aliases={0: 0},
        compiler_params=_cparams(2),
        name="attn_cd_sample",
    )(out_init, sink, sq, nq, sk, sv, nk, nv, skc, svc, nkc, nvc, bias)


def _na_bias(rpb):
    n_dc = 2 * NA_WIN_COLS - 1
    c = np.arange(GRID_W)[:, None]
    kc = np.arange(GRID_W)[None, :]
    qs = np.clip(c - NA_WIN_COLS // 2, 0, GRID_W - NA_WIN_COLS)
    col_ok = (kc >= qs) & (kc < qs + NA_WIN_COLS)
    dc = np.clip(kc - c + NA_WIN_COLS - 1, 0, n_dc - 1)
    onehot = ((dc[None] == np.arange(n_dc)[:, None, None]) & col_ok[None]).astype(np.float32)
    blocks = jnp.einsum('hrd,dck->hrck', rpb.astype(F32) * LOG2E, onehot, precision=lax.Precision.HIGHEST)
    blocks = jnp.where(col_ok[None, None], blocks, NEG)
    none = jnp.full((NA_HEADS, 1, GRID_W, GRID_W), NEG, F32)
    return jnp.concatenate([jnp.concatenate([none, blocks], axis=1),
                            jnp.concatenate([blocks, none], axis=1)], axis=-1)


_HI_MASK = -65536


def _pack_pairs(x):
    w = x.shape[1] // 2
    r = x.astype(BF16).astype(F32)
    lo = lax.bitcast_convert_type(r[:, :w], jnp.int32)
    hi = lax.bitcast_convert_type(r[:, w:], jnp.int32)
    return (hi & _HI_MASK) | lax.shift_right_logical(lo, 16)


def _unpack_pairs(p):
    lo = lax.bitcast_convert_type(lax.shift_left(p, 16), F32)
    hi = lax.bitcast_convert_type(p & _HI_MASK, F32)
    return lo, hi


def _postmix_kernel(*refs, split_x):
    if split_x:
        o_ref, xp_ref, xs_ref = refs[:3]
        refs = refs[3:]
    else:
        o_ref, x_ref = refs[:2]
        refs = refs[2:]
    (mod_ref, g_ref, wout_ref, wr_ref, br_ref, tri_ref,
     x1_ref, h2_ref, route_ref, gate_ref, cnt_ref, run_ref) = refs
    i = pl.program_id(0)

    @pl.when(i == 0)
    def _():
        run_ref[...] = jnp.zeros_like(run_ref)

    m = mod_ref[0]
    g = g_ref[...]
    gate_gain = g[1:2] * m[2:3]
    ffn_gain = g[2:3] * (1.0 + m[4:5])
    sub_logits = {}

    def project(n):
        return _dot(o_ref[_sub_rows(n), :], wout_ref[...])

    def finish(n, y):
        r = _sub_rows(n)
        x = jnp.where(i < NPT, xp_ref[r, :], xs_ref[r, :]) if split_x else x_ref[r, :]
        x1 = x + _rms(y, gate_gain)
        x1_ref[r, :] = x1
        h2 = _rms(x1, ffn_gain) + m[3:4]
        h2_ref[r, :] = _pack_pairs(h2)
        sub_logits[n] = _dot(h2.astype(BF16), wr_ref[...]) + br_ref[...]

    _pipelined_units([(project, finish, n) for n in range(N_SUB)])

    logits = jnp.concatenate([sub_logits[n] for n in range(N_SUB)], axis=0)
    lane = lax.broadcasted_iota(jnp.int32, logits.shape, 1).astype(F32)
    cur = jnp.where(lane < N_EXPERTS, logits, -jnp.inf)
    tops, idxs = [], []
    for _ in range(TOP_K):
        mx = jnp.max(cur, axis=-1, keepdims=True)
        ix = jnp.min(jnp.where(cur == mx, lane, float(LANES)), axis=-1, keepdims=True)
        tops.append(mx)
        idxs.append(ix)
        cur = jnp.where(lane == ix, -jnp.inf, cur)
    es = [jnp.exp(t - tops[0]) for t in tops]
    inv = 1.0 / (es[0] + es[1] + es[2] + es[3])
    picked = jnp.zeros_like(logits)
    for k in range(TOP_K):
        picked = jnp.where(lane == idxs[k], 1.0, picked)
    before = _dot(tri_ref[...], picked.astype(BF16)) + run_ref[0:1, :]
    route = jnp.zeros_like(logits)
    gate_out = jnp.zeros_like(logits)
    for k in range(TOP_K):
        rank = jnp.sum(jnp.where(lane == idxs[k], before, 0.0), axis=-1, keepdims=True)
        route = jnp.where(lane == float(k), idxs[k], route)
        route = jnp.where(lane == float(TOP_K + k), rank, route)
        gate_out = jnp.where(lane == float(k), es[k] * inv, gate_out)
    route_ref[...] = route.T[:2 * TOP_K].astype(jnp.int32)
    gate_ref[...] = gate_out
    run_ref[...] = run_ref[...] + jnp.sum(picked, axis=0, keepdims=True)
    cnt_ref[...] = run_ref[...].astype(jnp.int32)


def _postmix(o_cat, xs, mod_l, g_l, w_out, w_r, b_r):
    tri = jnp.asarray(np.tril(np.ones((TM, TM), np.float32), -1), BF16)
    split_x = len(xs) == 2
    x_specs = [_PROMPT_SPEC, _SAMPLE_SPEC] if split_x else [_tok_spec(D_MODEL)]
    return pl.pallas_call(
        functools.partial(_postmix_kernel, split_x=split_x),
        grid=(N_TILES,),
        in_specs=[_tok_spec(1024)] + x_specs + [_MOD_SPEC, _const_spec((4, D_MODEL)),
                  _const_spec((1024, D_MODEL)), _const_spec((D_MODEL, LANES)), _const_spec((1, LANES)),
                  _const_spec((TM, TM))],
        out_specs=[_tok_spec(D_MODEL), _tok_spec(D_MODEL // 2), pl.BlockSpec((2 * TOP_K, TM), lambda i: (0, i)),
                   _tok_spec(LANES), _const_spec((8, LANES))],
        out_shape=[jax.ShapeDtypeStruct((N_TOK, D_MODEL), F32),
                   jax.ShapeDtypeStruct((N_TOK, D_MODEL // 2), jnp.int32),
                   jax.ShapeDtypeStruct((2 * TOP_K, N_TOK), jnp.int32), jax.ShapeDtypeStruct((N_TOK, LANES), F32),
                   jax.ShapeDtypeStruct((8, LANES), jnp.int32)],
        scratch_shapes=[pltpu.VMEM((8, LANES), F32)],
        compiler_params=_cparams(1),
        name="postmix",
    )(o_cat, *xs, mod_l, g_l, w_out, w_r, b_r, tri)


SC_WORKERS = 32
SC_ROWS = 64
ROW_WORDS = D_MODEL // 2

_SC_SCRATCH = [pltpu.VMEM((SC_ROWS, ROW_WORDS), jnp.int32), pltpu.VMEM((SC_ROWS, ROW_WORDS), jnp.int32),
               pltpu.SemaphoreType.DMA, pltpu.SemaphoreType.DMA, pltpu.SemaphoreType.DMA, pltpu.SemaphoreType.DMA]


def _sc_worker_id():
    return lax.axis_index("s") * 2 + lax.axis_index("c")


def _sc_double_buffered(n_chunks, load, store):
    for cp in load(0, 0):
        cp.start()

    @pl.loop(0, n_chunks, step=2)
    def _(g0):
        for b in range(2):
            g = g0 + b
            for cp in load(g, b):
                cp.wait()

            @pl.when(g >= 1)
            def _():
                for cp in store(g - 1, 1 - b):
                    cp.wait()

            @pl.when(g + 1 < n_chunks)
            def _():
                for cp in load(g + 1, 1 - b):
                    cp.start()

            for cp in store(g, b):
                cp.start()

    for cp in store(n_chunks - 1, (n_chunks - 1) % 2):
        cp.wait()


def _sc_dispatch(src, idx):
    n_chunks = N_TOK // (SC_WORKERS * SC_ROWS)
    assert n_chunks % 2 == 0
    mesh = plsc.VectorSubcoreMesh(core_axis_name="c", subcore_axis_name="s")

    @functools.partial(
        pl.kernel, mesh=mesh,
        out_type=jax.ShapeDtypeStruct((MOE_ROWS, ROW_WORDS), jnp.int32),
        scratch_types=[pltpu.VMEM((n_chunks * TOP_K, SC_ROWS), jnp.int32)] + _SC_SCRATCH)
    def k(src_hbm, idx_hbm, out_hbm, idx_v, buf0, buf1, in0, in1, out0, out1):
        wid = _sc_worker_id()
        pltpu.sync_copy(idx_hbm.at[wid], idx_v)
        bufs, in_sems, out_sems = (buf0, buf1), (in0, in1), (out0, out1)

        def load(g, b):
            rows = pl.ds((wid * n_chunks + g) * SC_ROWS, SC_ROWS)
            return [pltpu.make_async_copy(src_hbm.at[rows], bufs[b], in_sems[b])]

        def store(g, b):
            return [pltpu.make_async_copy(bufs[b], out_hbm.at[idx_v.at[g * TOP_K + kk]], out_sems[b])
                    for kk in range(TOP_K)]

        _sc_double_buffered(n_chunks, load, store)

    return k(src, idx)


def _sc_collect(table, idx):
    n_chunks = idx.shape[1]
    assert n_chunks % 2 == 0
    mesh = plsc.VectorSubcoreMesh(core_axis_name="c", subcore_axis_name="s")

    @functools.partial(
        pl.kernel, mesh=mesh,
        out_type=jax.ShapeDtypeStruct((SC_WORKERS * n_chunks * SC_ROWS, ROW_WORDS), jnp.int32),
        scratch_types=[pltpu.VMEM((n_chunks, SC_ROWS), jnp.int32)] + _SC_SCRATCH)
    def k(table_hbm, idx_hbm, out_hbm, idx_v, buf0, buf1, in0, in1, out0, out1):
        wid = _sc_worker_id()
        pltpu.sync_copy(idx_hbm.at[wid], idx_v)
        bufs, in_sems, out_sems = (buf0, buf1), (in0, in1), (out0, out1)

        def load(g, b):
            return [pltpu.make_async_copy(table_hbm.at[idx_v.at[g]], bufs[b], in_sems[b])]

        def store(g, b):
            rows = pl.ds((wid * n_chunks + g) * SC_ROWS, SC_ROWS)
            return [pltpu.make_async_copy(bufs[b], out_hbm.at[rows], out_sems[b])]

        _sc_double_buffered(n_chunks, load, store)

    return k(table, idx)


def _expert_rows(words, n_valid, wgu_b, wd_b, bgu, bd):
    live = lax.broadcasted_iota(jnp.int32, words.shape, 0) < n_valid
    lo, hi = _unpack_pairs(jnp.where(live, words, 0))
    x = jnp.concatenate([lo, hi], axis=1).astype(BF16)
    gu = _dot(x, wgu_b[...]) + bgu
    g = jnp.minimum(gu[:, :D_EXPERT], SWIGLU_LIMIT)
    u = jnp.clip(gu[:, D_EXPERT:], -SWIGLU_LIMIT, SWIGLU_LIMIT)
    a = g * jax.nn.sigmoid(SWIGLU_ALPHA * g) * (u + 1.0)
    return _pack_pairs(_dot(a.astype(BF16), wd_b[...]) + bd)


def _moe_kernel(blk_e_ref, blk_first_ref, blk_rows_ref, blk_slot_ref, blk_next_ref, blk_io_ref,
                x_ref, wgu_hbm, bgu_ref, wd_hbm, bd_ref, y_ref,
                wgu_f, wd_f, wgu_b, wd_b, sem, *, layer):
    del blk_io_ref
    i = pl.program_id(0)
    n_valid = blk_rows_ref[i]
    quantum = MOE_TM // MOE_TAIL_PARTS

    def weight_copies(e, slot):
        return (pltpu.make_async_copy(wgu_hbm.at[layer, e], wgu_f.at[slot], sem.at[0, slot]),
                pltpu.make_async_copy(wd_hbm.at[layer, e], wd_f.at[slot], sem.at[1, slot]))

    @pl.when(i == 0)
    def _():
        for cp in weight_copies(blk_e_ref[0], blk_slot_ref[0]):
            cp.start()

    @pl.when(blk_first_ref[i] == 1)
    def _():
        slot = blk_slot_ref[i]
        for cp in weight_copies(blk_e_ref[i], slot):
            cp.wait()
        nxt = blk_next_ref[i]

        @pl.when(nxt >= 0)
        def _():
            for cp in weight_copies(nxt, 1 - slot):
                cp.start()

        wgu_b[...] = wgu_f[slot].astype(BF16)
        wd_b[...] = wd_f[slot].astype(BF16)

    for parts in range(1, MOE_TAIL_PARTS + 1):
        rows = parts * quantum

        @pl.when((n_valid > rows - quantum) & (n_valid <= rows))
        def _(rows=rows):
            y_ref[:rows] = _expert_rows(x_ref[:rows], n_valid, wgu_b, wd_b, bgu_ref[0, 0], bd_ref[0, 0])
            if rows < MOE_TM:
                y_ref[rows:] = jnp.zeros((MOE_TM - rows, ROW_WORDS), jnp.int32)


def _moe(layer, blk_meta, xs, w_gu, b_gu, w_down, b_down):
    def row_map(i, e, first, rows, slot, nxt, io):
        return (io[i], 0)

    def bias_map(i, e, *_):
        return (layer, e[i], 0, 0)

    grid_spec = pltpu.PrefetchScalarGridSpec(
        num_scalar_prefetch=6,
        grid=(MOE_BLOCKS,),
        in_specs=[
            pl.BlockSpec((MOE_TM, ROW_WORDS), row_map),
            pl.BlockSpec(memory_space=pl.ANY),
            pl.BlockSpec((1, 1, 1, 2 * D_EXPERT), bias_map),
            pl.BlockSpec(memory_space=pl.ANY),
            pl.BlockSpec((1, 1, 1, D_MODEL), bias_map),
        ],
        out_specs=pl.BlockSpec((MOE_TM, ROW_WORDS), row_map),
        scratch_shapes=[pltpu.VMEM((2, D_MODEL, 2 * D_EXPERT), F32), pltpu.VMEM((2, D_EXPERT, D_MODEL), F32),
                        pltpu.VMEM((D_MODEL, 2 * D_EXPERT), BF16), pltpu.VMEM((D_EXPERT, D_MODEL), BF16),
                        pltpu.SemaphoreType.DMA((2, 2))],
    )
    return pl.pallas_call(
        functools.partial(_moe_kernel, layer=layer),
        grid_spec=grid_spec,
        out_shape=jax.ShapeDtypeStruct((MOE_ROWS, ROW_WORDS), jnp.int32),
        compiler_params=_cparams(1),
        name="moe_experts",
    )(*blk_meta, xs, w_gu, b_gu.reshape(DEPTH, N_EXPERTS, 1, 2 * D_EXPERT),
      w_down, b_down.reshape(DEPTH, N_EXPERTS, 1, D_MODEL))


def _route(route, counts):
    experts = jnp.arange(N_EXPERTS, dtype=jnp.int32)
    padded = (counts + MOE_TM - 1) // MOE_TM * MOE_TM
    pad_end = jnp.cumsum(padded)
    pad_start = pad_end - padded
    e = route[0:TOP_K]
    onehot = e[:, :, None] == experts[None, None, :]
    dest = jnp.sum(jnp.where(onehot, pad_start[None, None, :], 0), axis=-1) + route[TOP_K:2 * TOP_K]
    blk_row0 = jnp.arange(MOE_BLOCKS, dtype=jnp.int32) * MOE_TM
    blk_e = jnp.minimum(jnp.sum((pad_end[None, :] <= blk_row0[:, None]).astype(jnp.int32), axis=1),
                        N_EXPERTS - 1)
    on = blk_row0 < pad_end[-1]
    n_on = jnp.sum(on.astype(jnp.int32))
    blk_onehot = blk_e[:, None] == experts[None, :]
    row_end = jnp.sum(jnp.where(blk_onehot, (pad_start + counts)[None, :], 0), axis=1)
    blk_rows = jnp.where(on, jnp.clip(row_end - blk_row0, 0, MOE_TM), 0).astype(jnp.int32)
    last_e = jnp.sum(jnp.where(jnp.arange(MOE_BLOCKS) == n_on - 1, blk_e, 0))
    blk_e = jnp.where(on, blk_e, last_e).astype(jnp.int32)
    prev = jnp.concatenate([jnp.full((1,), -1, jnp.int32), blk_e[:-1]])
    blk_first = (blk_e != prev).astype(jnp.int32)
    blk_slot = ((jnp.cumsum(blk_first) - 1) % 2).astype(jnp.int32)
    later_used = (experts[None, :] > experts[:, None]) & (counts[None, :] > 0)
    next_used = jnp.min(jnp.where(later_used, experts[None, :], N_EXPERTS), axis=1)
    next_used = jnp.where(next_used == N_EXPERTS, -1, next_used)
    blk_next = jnp.sum(jnp.where(blk_e[:, None] == experts[None, :], next_used[None, :], 0), axis=1)
    blk_io = jnp.minimum(jnp.arange(MOE_BLOCKS, dtype=jnp.int32), n_on - 1)
    return dest, (blk_e, blk_first, blk_rows, blk_slot, blk_next.astype(jnp.int32), blk_io)


def _combine_kernel(y_ref, gate_ref, x1_ref, mod_ref, g_ref, *o_refs):
    i = pl.program_id(0)
    m = mod_ref[0]
    gate = gate_ref[...]
    acc_lo = None
    for k in range(TOP_K):
        lo, hi = _unpack_pairs(y_ref[k])
        gk = gate[:, k:k + 1]
        acc_lo = gk * lo if acc_lo is None else acc_lo + gk * lo
        acc_hi = gk * hi if k == 0 else acc_hi + gk * hi
    acc = jnp.concatenate([acc_lo, acc_hi], axis=1)
    out = x1_ref[...] + _rms(acc, g_ref[...][3:4] * m[5:6])
    if len(o_refs) == 1:
        o_refs[0][...] = out
    else:
        @pl.when(i < NPT)
        def _():
            o_refs[0][...] = out

        @pl.when(i >= NPT)
        def _():
            o_refs[1][...] = out


def _combine(yg, gates, x1, mod_l, g_l, split_out):
    if split_out:
        out_specs = [_PROMPT_SPEC, _SAMPLE_SPEC]
        out_shape = [jax.ShapeDtypeStruct((NP_TOK, D_MODEL), F32), jax.ShapeDtypeStruct((NS_TOK, D_MODEL), F32)]
    else:
        out_specs = [_tok_spec(D_MODEL)]
        out_shape = [jax.ShapeDtypeStruct((N_TOK, D_MODEL), F32)]
    return pl.pallas_call(
        _combine_kernel,
        grid=(N_TILES,),
        in_specs=[pl.BlockSpec((TOP_K, TM, ROW_WORDS), lambda i: (0, i, 0)), _tok_spec(LANES),
                  _tok_spec(D_MODEL), _MOD_SPEC, _const_spec((4, D_MODEL))],
        out_specs=out_specs,
        out_shape=out_shape,
        compiler_params=_cparams(1),
        name="combine",
    )(yg, gates, x1, mod_l, g_l)


def _ffn(layer, o_cat, xs, mod_l, g_l, w_out, w_router, b_router, w_gu, b_gu, w_down, b_down, split_out):
    w_r = jnp.pad(w_router, ((0, 0), (0, LANES - N_EXPERTS))).astype(BF16)
    b_r = jnp.pad(b_router, (0, LANES - N_EXPERTS)).reshape(1, LANES)
    x1, h2p, route, gate_slab, counts = _postmix(o_cat, xs, mod_l, g_l, w_out.astype(BF16), w_r, b_r)
    dest, blk_meta = _route(route, counts[0, :N_EXPERTS])
    n_chunks = N_TOK // (SC_WORKERS * SC_ROWS)
    idx_d = dest.reshape(TOP_K, SC_WORKERS, n_chunks, SC_ROWS).transpose(1, 2, 0, 3).reshape(
        SC_WORKERS, n_chunks * TOP_K, SC_ROWS)
    rows = _sc_dispatch(h2p, idx_d)
    ys = _moe(layer, blk_meta, rows, w_gu, b_gu, w_down, b_down)
    idx_c = dest.reshape(SC_WORKERS, TOP_K * n_chunks, SC_ROWS)
    yg = _sc_collect(ys, idx_c).reshape(TOP_K, N_TOK, ROW_WORDS)
    return _combine(yg, gate_slab, x1, mod_l, g_l, split_out)


def _pad_heads(w, n_heads, width, keep):
    k = w.shape[0]
    w = w.reshape(k, n_heads, width)[:, :, :keep]
    return jnp.pad(w, ((0, 0), (0, 0), (0, LANES - keep))).reshape(k, n_heads * LANES)


def _pe_slab(x):
    return jnp.pad(x, [(0, 0)] * (x.ndim - 1) + [(MLA_NOPE, LANES - MLA_NOPE - MLA_ROPE)])


def _pair_kv_heads(w):
    g = SWA_HEADS // SWA_KV_HEADS
    return w.reshape(SWA_KV_HEADS, g, HEAD_DIM, -1).transpose(1, 0, 2, 3).reshape(w.shape)


def kernel(x_prompt, x_sample, cache_mla_ckv, cache_mla_krope, cache_diff_k, cache_diff_v, cache_swa_k, cache_swa_v, cache_na_k, cache_na_v, c, c_ctx, w_mod, b_mod, norm_g, w_in0, mla_q_norm, w_uq, mla_kv_norm, w_ukv, diff_lambda, diff_norm, w_out0, w_in1, swa_sink, na_rpb, w_out1, w_router, b_router, w_gu, b_gu, w_down, b_down):
    xs = (x_prompt.reshape(NP_TOK, D_MODEL), x_sample.reshape(NS_TOK, D_MODEL))
    cond = jnp.concatenate([c_ctx[None, :], c, jnp.zeros((16 - 1 - DEC_BATCH, D_MODEL), F32)], axis=0)
    mod = _modulation(cond, w_mod, b_mod).reshape(DEPTH, 16, 6, D_MODEL)
    t64, t32 = _rope_tables()
    states = {}
    for l in range(DEPTH):
        i = l // 2
        g_l = norm_g[l]
        mod_l = mod[l]
        if l % 2 == 0:
            lam_init = 0.8 - 0.6 * math.exp(-0.3 * l)
            wi = w_in0[i]
            w_in_p = jnp.concatenate(
                [wi[:, 0:640], wi[:, 672:2208], _pe_slab(wi[:, 640:672])], axis=1).astype(BF16)
            w_uq_p = _pad_heads(w_uq[i], MLA_HEADS, MLA_NOPE + MLA_ROPE, MLA_NOPE + MLA_ROPE).astype(BF16)
            w_k_p = _pad_heads(w_ukv[i], MLA_HEADS, MLA_NOPE + MLA_V, MLA_NOPE).astype(BF16)
            w_v = w_ukv[i].reshape(MLA_KV_RANK, MLA_HEADS, MLA_NOPE + MLA_V)[:, :, MLA_NOPE:].reshape(
                MLA_KV_RANK, MLA_HEADS * MLA_V).astype(BF16)
            (q, k, v, dq, dk, dv, ckv_st, kpe_st, dk_st, dv_st) = _premix0(
                *xs, mod_l, g_l[0:1], w_in_p, mla_q_norm[i][None, :], w_uq_p, mla_kv_norm[i][None, :],
                w_k_p, w_v, t32, t64)
            states['mla_ckv'] = ckv_st.reshape(BATCH, 1, SEQ, MLA_KV_RANK)
            states['mla_krope'] = kpe_st[:, MLA_NOPE:MLA_NOPE + MLA_ROPE].reshape(BATCH, 1, SEQ, MLA_ROPE)
            states['diff_k'] = dk_st.reshape(BATCH, 1, SEQ, DIFF_HEADS, 2 * DIFF_DH)
            states['diff_v'] = dv_st.reshape(BATCH, 1, SEQ, DIFF_HEADS, 2 * DIFF_DH)
            kc, vc = _mla_cache(cache_mla_ckv[:, i].reshape(DEC_BATCH * PAST_LEN, MLA_KV_RANK),
                                _pe_slab(cache_mla_krope[:, i].reshape(DEC_BATCH * PAST_LEN, MLA_ROPE)),
                                w_k_p, w_v)
            dkc = cache_diff_k[:, i].reshape(DEC_BATCH * PAST_LEN, 512).astype(BF16)
            dvc = cache_diff_v[:, i].reshape(DEC_BATCH * PAST_LEN, 512).astype(BF16)
            lam = diff_lambda[i]
            sub_g = diff_norm[i][None, :]
            o_p = _attn_ab(q, dq, (k, v, dk, dv), None, lam, sub_g, lam_init,
                           n_batch=BATCH, t_len=SEQ, tq=SEQ, tok_off=0, n_seq=PROMPT_SEQS_PER_STEP)
            o_cat = _attn_ab(q, dq, (k, v, dk, dv), (kc, vc, dkc, dvc), lam, sub_g, lam_init,
                             n_batch=DEC_BATCH, t_len=DEC_SEQ, tq=TQ_AB, tok_off=NP_TOK, out_init=o_p)
            w_out = w_out0[i]
        else:
            wi = w_in1[i]
            n_sq = SWA_HEADS * HEAD_DIM
            w_in_p = jnp.concatenate([_pair_kv_heads(wi[:, :n_sq].T).T, wi[:, n_sq:]], axis=1).astype(BF16)
            (sq, sk, sv, nq, nk, nv, sk_st, sv_st, nk_st, nv_st) = _premix1(xs[0], mod_l, g_l[0:1], w_in_p, t64)
            states['swa_k'] = sk_st.reshape(BATCH, 1, SEQ, SWA_KV_HEADS, HEAD_DIM)
            states['swa_v'] = sv_st.reshape(BATCH, 1, SEQ, SWA_KV_HEADS, HEAD_DIM)
            states['na_k'] = nk_st.reshape(BATCH, 1, SEQ, NA_HEADS, HEAD_DIM)
            states['na_v'] = nv_st.reshape(BATCH, 1, SEQ, NA_HEADS, HEAD_DIM)
            skc = cache_swa_k[:, i].reshape(DEC_BATCH * PAST_LEN, 128).astype(BF16)
            svc = cache_swa_v[:, i].reshape(DEC_BATCH * PAST_LEN, 128).astype(BF16)
            nkc = cache_na_k[:, i].reshape(DEC_BATCH * PAST_LEN, 512).astype(BF16)
            nvc = cache_na_v[:, i].reshape(DEC_BATCH * PAST_LEN, 512).astype(BF16)
            sink = swa_sink[i]
            o_p = _attn_cd_prompt(sink, sq, sk, sv, nq, nk, nv)
            o_cat = _attn_cd_sample(o_p, sink, sq, sk, sv, nq, nk, nv, skc, svc, nkc, nvc, _na_bias(na_rpb[i]))
            wo = w_out1[i]
            w_out = jnp.concatenate([_pair_kv_heads(wo[:n_sq]), wo[n_sq:]], axis=0)
        xs = _ffn(l, o_cat, xs, mod_l, g_l, w_out, w_router[l], b_router[l], w_gu, b_gu, w_down, b_down,
                  split_out=(l == DEPTH - 1))
    return (xs[0].reshape(BATCH, SEQ, D_MODEL), xs[1].reshape(DEC_BATCH, DEC_SEQ, D_MODEL),
            states['mla_ckv'], states['mla_krope'], states['diff_k'], states['diff_v'],
            states['swa_k'], states['swa_v'], states['na_k'], states['na_v'])
```

```python
import functools
import math

import numpy as np
import jax
import jax.numpy as jnp
from jax import lax
from jax.experimental import pallas as pl
from jax.experimental.pallas import tpu as pltpu
from jax.experimental.pallas import tpu_sc as plsc

F32 = jnp.float32
BF16 = jnp.bfloat16

D_MODEL = 1024
BATCH = 16
SEQ = 256
DEPTH = 2
DEC_BATCH = 8
DEC_SEQ = 2048
PAST_LEN = 256
GRID_W = 64
HEAD_DIM = 64
ROPE_THETA = 10000.0
EPS = 1e-6
NEG = -1e30

MLA_HEADS = 8
MLA_Q_RANK = 384
MLA_KV_RANK = 256
MLA_NOPE = 64
MLA_ROPE = 32
MLA_V = 64
DIFF_HEADS = 4
DIFF_DH = 64
SWA_HEADS = 8
SWA_KV_HEADS = 2
SWA_WINDOW = 128
NA_HEADS = 8
NA_WIN_ROWS = 8
NA_WIN_COLS = 16
N_EXPERTS = 32
TOP_K = 4
D_EXPERT = 1024
SWIGLU_LIMIT = 7.0
SWIGLU_ALPHA = 1.702

LANES = 128
NP_TOK = BATCH * SEQ
NS_TOK = DEC_BATCH * DEC_SEQ
N_TOK = NP_TOK + NS_TOK
TM = 512
NPT = NP_TOK // TM
TILES_PER_SAMPLE = DEC_SEQ // TM
N_TILES = N_TOK // TM
TM_WIDE = 1024
N_SUB = 2
SUB_TM = TM // N_SUB
TQ = 256
TQ_AB = 256
PROMPT_SEQS_PER_STEP = 2
MOE_TM = 1024
MOE_TAIL_PARTS = 8
MOE_ROWS = ((N_TOK * TOP_K + N_EXPERTS * (MOE_TM - 1)) // MOE_TM + 1) * MOE_TM
MOE_BLOCKS = MOE_ROWS // MOE_TM
NA_TILE_ROWS = TQ // GRID_W
NA_KEY_ROWS = 12
VMEM_LIMIT = 56 * 1024 * 1024


def _cparams(n_axes, vmem=VMEM_LIMIT):
    return pltpu.CompilerParams(dimension_semantics=("arbitrary",) * n_axes,
                                vmem_limit_bytes=vmem)


def _rms(x, g):
    return x * lax.rsqrt(jnp.mean(x * x, axis=-1, keepdims=True) + EPS) * g


def _dot(a, b):
    return jnp.dot(a, b, preferred_element_type=F32)


def _dot_nt(a, b):
    return lax.dot_general(a, b, (((1,), (1,)), ((), ())), preferred_element_type=F32)


def _rope(x, cos, sin_a, sin_b, half):
    return (x * cos + pltpu.roll(x, LANES - half, 1) * sin_a + pltpu.roll(x, half, 1) * sin_b)


def _pipelined_units(units):
    pending = units[0][0](units[0][2])
    for n, (_, finish, arg) in enumerate(units):
        following = units[n + 1][0](units[n + 1][2]) if n + 1 < len(units) else None
        finish(arg, pending)
        pending = following


def _sub_rows(n):
    return slice(SUB_TM * n, SUB_TM * (n + 1))


def _store_head_rows(st_ref, n, hd, n_heads, x):
    st_ref[pl.ds(SUB_TM * n * n_heads + hd, SUB_TM, stride=n_heads), :] = x


def _mod_row(i):
    return jnp.where(i < NPT, 0, 1 + (i - NPT) // TILES_PER_SAMPLE)


def _rope_blk(i):
    return jnp.where(i < NPT, TILES_PER_SAMPLE, (i - NPT) % TILES_PER_SAMPLE)


def _mod_kernel(c_ref, w_ref, b_ref, o_ref):
    c = c_ref[...]
    s = (c * jax.nn.sigmoid(c)).astype(BF16)
    o_ref[0] = _dot(s, w_ref[0].astype(BF16)) + b_ref[0]


def _modulation(cond, w_mod, b_mod):
    nb = 1024
    return pl.pallas_call(
        _mod_kernel,
        grid=(DEPTH, 6 * D_MODEL // nb),
        in_specs=[
            pl.BlockSpec((16, D_MODEL), lambda l, n: (0, 0)),
            pl.BlockSpec((1, D_MODEL, nb), lambda l, n: (l, 0, n)),
            pl.BlockSpec((1, 1, nb), lambda l, n: (l, 0, n)),
        ],
        out_specs=pl.BlockSpec((1, 16, nb), lambda l, n: (l, 0, n)),
        out_shape=jax.ShapeDtypeStruct((DEPTH, 16, 6 * D_MODEL), F32),
        compiler_params=_cparams(2),
        name="modulation",
    )(cond, w_mod, b_mod.reshape(DEPTH, 1, 6 * D_MODEL))


def _rope_tables():
    t = jnp.arange(DEC_SEQ)
    rows = (t // GRID_W).astype(F32)
    cols = (t % GRID_W).astype(F32)

    def angles(r):
        n = r // 4
        inv = ROPE_THETA ** (-jnp.arange(n, dtype=F32) / n)
        return jnp.concatenate([rows[:, None] * inv[None], cols[:, None] * inv[None]], axis=-1)

    def finish(cos, sa, sb):
        ident = (jnp.ones((TM, LANES), F32), jnp.zeros((TM, LANES), F32), jnp.zeros((TM, LANES), F32))
        return tuple(jnp.concatenate([a, b], axis=0) for a, b in zip((cos, sa, sb), ident))

    a64 = angles(64)
    c, s, z = jnp.cos(a64), jnp.sin(a64), jnp.zeros_like(a64)
    t64 = finish(jnp.concatenate([c, c, c, c], -1), jnp.concatenate([-s, z, -s, z], -1),
                 jnp.concatenate([z, s, z, s], -1))
    a32 = angles(32)
    c, s, z = jnp.cos(a32), jnp.sin(a32), jnp.zeros_like(a32)
    one64 = jnp.ones((DEC_SEQ, 64), F32)
    z64 = jnp.zeros((DEC_SEQ, 64), F32)
    z32 = jnp.zeros((DEC_SEQ, 32), F32)
    t32 = finish(jnp.concatenate([one64, c, c, z32], -1), jnp.concatenate([z64, -s, z, z32], -1),
                 jnp.concatenate([z64, z, s, z32], -1))
    return t64, t32


LOG2E = math.log2(math.e)
_DIFF_COLS = DIFF_HEADS * 2 * DIFF_DH
_AB_COLS = tuple(np.cumsum([0, MLA_Q_RANK, MLA_KV_RANK, _DIFF_COLS, _DIFF_COLS, _DIFF_COLS, LANES]).tolist())
_CD_COLS = tuple(np.cumsum([0, SWA_HEADS * HEAD_DIM, SWA_KV_HEADS * HEAD_DIM, SWA_KV_HEADS * HEAD_DIM,
                            NA_HEADS * HEAD_DIM, NA_HEADS * HEAD_DIM, NA_HEADS * HEAD_DIM]).tolist())
IN_COLS = _AB_COLS[-1]
assert IN_COLS == _CD_COLS[-1]
_MLA_SCALE = (MLA_NOPE + MLA_ROPE) ** -0.5 * LOG2E
_QSCALE = HEAD_DIM ** -0.5 * LOG2E


def _premix0_kernel(xp_ref, xs_ref, mod_ref, g_ref, win_ref, qn_ref, wuq_ref, kvn_ref, wk_ref, wv_ref,
                    c32_ref, sa32_ref, sb32_ref, c64_ref, sa64_ref, sb64_ref,
                    q_ref, k_ref, v_ref, dq_ref, dk_ref, dv_ref,
                    ckv_st, kpe_st, dk_st, dv_st):
    i = pl.program_id(0)
    m = mod_ref[0]
    gain = g_ref[...] * (1.0 + m[1:2])

    def project(n):
        r = _sub_rows(n)
        x = jnp.where(i < NPT, xp_ref[r, :], xs_ref[r, :])
        h = _rms(x, gain) + m[0:1]
        return _dot(h.astype(BF16), win_ref[...])

    states = {}

    def finish(n, proj):
        r = _sub_rows(n)
        q_a, kv_a, dq, dk, dv, pe = (proj[:, a:b] for a, b in zip(_AB_COLS[:-1], _AB_COLS[1:]))
        q = _dot(_rms(q_a, qn_ref[...]).astype(BF16), wuq_ref[...])
        ckv = _rms(kv_a, kvn_ref[...])
        ckv_b = ckv.astype(BF16)
        kn = _dot(ckv_b, wk_ref[...])
        v_ref[r, :] = _dot(ckv_b, wv_ref[...]).astype(BF16)
        c32, sa32, sb32 = c32_ref[r, :], sa32_ref[r, :], sb32_ref[r, :]
        c64, sa64, sb64 = c64_ref[r, :], sa64_ref[r, :], sb64_ref[r, :]
        pe_r = _rope(pe, c32, sa32, sb32, MLA_ROPE // 2)
        for hd in range(MLA_HEADS):
            sl = slice(LANES * hd, LANES * (hd + 1))
            q_ref[r, sl] = (_rope(q[:, sl], c32, sa32, sb32, MLA_ROPE // 2) * _MLA_SCALE).astype(BF16)
            k_ref[r, sl] = (kn[:, sl] + pe_r).astype(BF16)
        for hd in range(DIFF_HEADS):
            sl = slice(LANES * hd, LANES * (hd + 1))
            dq_ref[r, sl] = (_rope(dq[:, sl], c64, sa64, sb64, DIFF_DH // 2) * _QSCALE).astype(BF16)
            dk_ref[r, sl] = _rope(dk[:, sl], c64, sa64, sb64, DIFF_DH // 2).astype(BF16)
        dv_ref[r, :] = dv.astype(BF16)
        states[n] = (ckv, pe, dk, dv)

    _pipelined_units([(project, finish, n) for n in range(N_SUB)])

    @pl.when(i < NPT)
    def _():
        for n in range(N_SUB):
            r = _sub_rows(n)
            ckv, pe, dk, dv = states[n]
            ckv_st[r, :] = ckv
            kpe_st[r, :] = pe
            for hd in range(DIFF_HEADS):
                _store_head_rows(dk_st, n, hd, DIFF_HEADS, dk[:, LANES * hd: LANES * (hd + 1)])
                _store_head_rows(dv_st, n, hd, DIFF_HEADS, dv[:, LANES * hd: LANES * (hd + 1)])


def _premix1_kernel(x_ref, mod_ref, g_ref, win_ref, c64_ref, sa64_ref, sb64_ref,
                    sq_ref, sk_ref, sv_ref, nq_ref, nk_ref, nv_ref,
                    sk_st, sv_st, nk_st, nv_st):
    i = pl.program_id(0)
    m = mod_ref[0]
    gain = g_ref[...] * (1.0 + m[1:2])

    def project(n):
        h = _rms(x_ref[_sub_rows(n), :], gain) + m[0:1]
        return _dot(h.astype(BF16), win_ref[...])

    states = {}

    def finish(n, proj):
        r = _sub_rows(n)
        sq, sk, sv, nq, nk, nv = (proj[:, a:b] for a, b in zip(_CD_COLS[:-1], _CD_COLS[1:]))
        c64, sa64, sb64 = c64_ref[r, :], sa64_ref[r, :], sb64_ref[r, :]
        for hd in range(4):
            sl = slice(LANES * hd, LANES * (hd + 1))
            sq_ref[r, sl] = (_rope(sq[:, sl], c64, sa64, sb64, HEAD_DIM // 2) * _QSCALE).astype(BF16)
        sk_ref[r, :] = _rope(sk, c64, sa64, sb64, HEAD_DIM // 2).astype(BF16)
        sv_ref[r, :] = sv.astype(BF16)
        nq_ref[r, :] = (nq * _QSCALE).astype(BF16)
        nk_ref[r, :] = nk.astype(BF16)
        nv_ref[r, :] = nv.astype(BF16)
        states[n] = (sk, sv, nk, nv)

    _pipelined_units([(project, finish, n) for n in range(N_SUB)])

    @pl.when(i < NPT)
    def _():
        for n in range(N_SUB):
            r = _sub_rows(n)
            sk, sv, nk, nv = states[n]
            sk_st[r, :] = sk
            sv_st[r, :] = sv
            for hd in range(NA_HEADS):
                _store_head_rows(nk_st, n, hd, NA_HEADS, nk[:, HEAD_DIM * hd: HEAD_DIM * (hd + 1)])
                _store_head_rows(nv_st, n, hd, NA_HEADS, nv[:, HEAD_DIM * hd: HEAD_DIM * (hd + 1)])


def _tok_spec(width):
    return pl.BlockSpec((TM, width), lambda i: (i, 0))


_PROMPT_SPEC = pl.BlockSpec((TM, D_MODEL), lambda i: (jnp.minimum(i, NPT - 1), 0))
_SAMPLE_SPEC = pl.BlockSpec((TM, D_MODEL), lambda i: (jnp.maximum(i - NPT, 0), 0))


def _state_spec(width, rows_per_token=1):
    return pl.BlockSpec((TM * rows_per_token, width), lambda i: (jnp.minimum(i, NPT - 1), 0))


def _const_spec(shape):
    return pl.BlockSpec(shape, lambda i: (0,) * len(shape))


_MOD_SPEC = pl.BlockSpec((1, 6, D_MODEL), lambda i: (_mod_row(i), 0, 0))
_ROPE_SPEC = pl.BlockSpec((TM, LANES), lambda i: (_rope_blk(i), 0))


def _premix0(xp, xs, mod_l, g0, w_in_p, q_norm, w_uq_p, kv_norm, w_k_p, w_v, t32, t64):
    outs = [(N_TOK, 1024, BF16), (N_TOK, 1024, BF16), (N_TOK, 512, BF16), (N_TOK, 512, BF16),
            (N_TOK, 512, BF16), (N_TOK, 512, BF16),
            (NP_TOK, 256, F32), (NP_TOK, 128, F32),
            (NP_TOK * DIFF_HEADS, 2 * DIFF_DH, F32), (NP_TOK * DIFF_HEADS, 2 * DIFF_DH, F32)]
    return pl.pallas_call(
        _premix0_kernel,
        grid=(N_TILES,),
        in_specs=[_PROMPT_SPEC, _SAMPLE_SPEC, _MOD_SPEC, _const_spec((1, D_MODEL)),
                  _const_spec((D_MODEL, IN_COLS)), _const_spec((1, MLA_Q_RANK)),
                  _const_spec((MLA_Q_RANK, 1024)), _const_spec((1, MLA_KV_RANK)),
                  _const_spec((MLA_KV_RANK, 1024)), _const_spec((MLA_KV_RANK, 512))]
                 + [_ROPE_SPEC] * 6,
        out_specs=([_tok_spec(w) for (_, w, _) in outs[:6]]
                   + [_state_spec(w, n // NP_TOK) for (n, w, _) in outs[6:]]),
        out_shape=[jax.ShapeDtypeStruct((n, w), dt) for (n, w, dt) in outs],
        compiler_params=_cparams(1),
        name="premix_ab",
    )(xp, xs, mod_l, g0, w_in_p, q_norm, w_uq_p, kv_norm, w_k_p, w_v, *t32, *t64)


def _premix1(x, mod_l, g0, w_in_p, t64):
    outs = [(N_TOK, 512, BF16), (N_TOK, 128, BF16), (N_TOK, 128, BF16), (N_TOK, 512, BF16),
            (N_TOK, 512, BF16), (N_TOK, 512, BF16),
            (NP_TOK, 128, F32), (NP_TOK, 128, F32),
            (NP_TOK * NA_HEADS, HEAD_DIM, F32), (NP_TOK * NA_HEADS, HEAD_DIM, F32)]
    return pl.pallas_call(
        _premix1_kernel,
        grid=(N_TILES,),
        in_specs=[_tok_spec(D_MODEL), _MOD_SPEC, _const_spec((1, D_MODEL)),
                  _const_spec((D_MODEL, IN_COLS))] + [_ROPE_SPEC] * 3,
        out_specs=([_tok_spec(w) for (_, w, _) in outs[:6]]
                   + [_state_spec(w, n // NP_TOK) for (n, w, _) in outs[6:]]),
        out_shape=[jax.ShapeDtypeStruct((n, w), dt) for (n, w, dt) in outs],
        compiler_params=_cparams(1),
        name="premix_cd",
    )(x, mod_l, g0, w_in_p, *t64)


def _mla_cache_kernel(ckv_ref, pe_ref, wk_ref, wv_ref, k_ref, v_ref):
    c = ckv_ref[...].astype(BF16)
    kn = _dot(c, wk_ref[...])
    v_ref[...] = _dot(c, wv_ref[...]).astype(BF16)
    pe = pe_ref[...]
    for hd in range(MLA_HEADS):
        sl = slice(LANES * hd, LANES * (hd + 1))
        k_ref[:, sl] = (kn[:, sl] + pe).astype(BF16)


def _mla_cache(ckv, pe_slab, w_k_p, w_v):
    n = ckv.shape[0]
    tm = 512
    return pl.pallas_call(
        _mla_cache_kernel,
        grid=(n // tm,),
        in_specs=[pl.BlockSpec((tm, MLA_KV_RANK), lambda i: (i, 0)),
                  pl.BlockSpec((tm, LANES), lambda i: (i, 0)),
                  _const_spec((MLA_KV_RANK, 1024)), _const_spec((MLA_KV_RANK, 512))],
        out_specs=[pl.BlockSpec((tm, 1024), lambda i: (i, 0)), pl.BlockSpec((tm, 512), lambda i: (i, 0))],
        out_shape=[jax.ShapeDtypeStruct((n, 1024), BF16), jax.ShapeDtypeStruct((n, 512), BF16)],
        compiler_params=_cparams(1),
        name="mla_cache",
    )(ckv, pe_slab, w_k_p, w_v)


def _softmax_pv(scores, values, sink=None):
    m = jnp.max(scores[0], axis=-1, keepdims=True)
    for s in scores[1:]:
        m = jnp.maximum(m, jnp.max(s, axis=-1, keepdims=True))
    if sink is not None:
        m = jnp.maximum(m, sink)
    l = None
    o = None
    for s, v in zip(scores, values):
        p = jnp.exp2(s - m)
        ls = jnp.sum(p, axis=-1, keepdims=True)
        os_ = _dot(p.astype(BF16), v)
        l = ls if l is None else l + ls
        o = os_ if o is None else o + os_
    if sink is not None:
        l = l + jnp.exp2(sink - m)
    return o * (1.0 / l)


def _lane_lo(shape):
    return lax.broadcasted_iota(jnp.int32, shape, 1) < (LANES // 2)


def _split_halves(qb):
    lo = _lane_lo(qb.shape)
    zero = jnp.zeros_like(qb)
    return jnp.where(lo, qb, zero), jnp.where(lo, zero, qb)


def _attn_ab_kernel(*refs, n_pieces, n_seq, lam_init, aliased):
    if aliased:
        refs = refs[1:]
    q_ref, dq_ref = refs[0], refs[1]
    pieces = [refs[2 + 4 * p: 6 + 4 * p] for p in range(n_pieces)]
    lam_ref, subg_ref, o_ref = refs[2 + 4 * n_pieces:]
    lam = lam_ref[...]
    lam_full = (jnp.exp(jnp.sum(lam[0:1] * lam[1:2], axis=-1, keepdims=True))
                - jnp.exp(jnp.sum(lam[2:3] * lam[3:4], axis=-1, keepdims=True)) + lam_init)
    tq = q_ref.shape[0] // n_seq
    lo = _lane_lo((tq, LANES))
    subg = subg_ref[...] * (1.0 - lam_init)

    def seq_rows(ref, sq):
        n = ref.shape[0] // n_seq
        return slice(n * sq, n * (sq + 1))

    def mla_scores(arg):
        sq, hd = arg
        sl = slice(LANES * hd, LANES * (hd + 1))
        qh = q_ref[seq_rows(q_ref, sq), sl]
        return [_dot_nt(qh, k_ref[seq_rows(k_ref, sq), sl]) for (k_ref, _, _, _) in pieces]

    def diff_scores(arg):
        sq, hd = arg
        sl = slice(LANES * hd, LANES * (hd + 1))
        qq = jnp.concatenate(_split_halves(dq_ref[seq_rows(dq_ref, sq), sl]), axis=0)
        return [_dot_nt(qq, dk_ref[seq_rows(dk_ref, sq), sl]) for (_, _, dk_ref, _) in pieces]

    pair = {}

    def mla_finish(arg, scores):
        sq, hd = arg
        j = hd // 2
        vals = [v_ref[seq_rows(v_ref, sq), LANES * j: LANES * (j + 1)] for (_, v_ref, _, _) in pieces]
        pair[hd % 2] = _softmax_pv(scores, vals)
        if hd % 2 == 1:
            o_ref[seq_rows(o_ref, sq), LANES * j: LANES * (j + 1)] = jnp.where(lo, pair[0], pair[1]).astype(BF16)

    def diff_finish(arg, scores):
        sq, hd = arg
        sl = slice(LANES * hd, LANES * (hd + 1))
        oo = _softmax_pv(scores, [dv_ref[seq_rows(dv_ref, sq), sl] for (_, _, _, dv_ref) in pieces])
        od = _rms(oo[:tq] - lam_full * oo[tq:], subg)
        o_ref[seq_rows(o_ref, sq), 512 + LANES * hd: 512 + LANES * (hd + 1)] = od.astype(BF16)

    units = []
    for sq in range(n_seq):
        for j in range(DIFF_HEADS):
            units += [(diff_scores, diff_finish, (sq, j)), (mla_scores, mla_finish, (sq, 2 * j)),
                      (mla_scores, mla_finish, (sq, 2 * j + 1))]
    _pipelined_units(units)


def _attn_ab(q, dq, new_kv, cache_kv, lam, sub_g, lam_init, *, n_batch, t_len, tq, tok_off, out_init=None,
             n_seq=1):
    assert n_seq == 1 or (tq == t_len and cache_kv is None and n_batch % n_seq == 0)
    n_batch, t_len, tq = n_batch // n_seq, t_len * n_seq, tq * n_seq
    nq = t_len // tq
    q_off = tok_off // tq
    b_off = tok_off // t_len
    widths = (1024, 512, 512, 512)
    in_specs = [pl.BlockSpec((tq, 1024), lambda b, i: (q_off + b * nq + i, 0)),
                pl.BlockSpec((tq, 512), lambda b, i: (q_off + b * nq + i, 0))]
    args = [q, dq]
    for w, a in zip(widths, new_kv):
        in_specs.append(pl.BlockSpec((t_len, w), lambda b, i: (b_off + b, 0)))
        args.append(a)
    n_pieces = 1
    if cache_kv is not None:
        n_pieces = 2
        for w, a in zip(widths, cache_kv):
            in_specs.append(pl.BlockSpec((PAST_LEN, w), lambda b, i: (b, 0)))
            args.append(a)
    in_specs += [pl.BlockSpec((4, DIFF_DH), lambda b, i: (0, 0)),
                 pl.BlockSpec((1, 2 * DIFF_DH), lambda b, i: (0, 0))]
    args += [lam, sub_g]
    aliases = {}
    if out_init is not None:
        in_specs = [pl.BlockSpec(memory_space=pl.ANY)] + in_specs
        args = [out_init] + args
        aliases = {0: 0}
    return pl.pallas_call(
        functools.partial(_attn_ab_kernel, n_pieces=n_pieces, n_seq=n_seq, lam_init=lam_init,
                          aliased=out_init is not None),
        grid=(n_batch, nq),
        in_specs=in_specs,
        out_specs=pl.BlockSpec((tq, 1024), lambda b, i: (q_off + b * nq + i, 0)),
        out_shape=jax.ShapeDtypeStruct((N_TOK, 1024), BF16),
        input_output_aliases=aliases,
        compiler_params=_cparams(2),
        name="attn_ab_%d" % n_pieces,
    )(*args)


def _gqa_stacks(sq_ref, sink_ref):
    tq = sq_ref.shape[0]
    halves = [_split_halves(sq_ref[:, LANES * j: LANES * (j + 1)]) for j in range(4)]
    q_stacks = [jnp.concatenate([halves[j][kvh] for j in range(4)], axis=0) for kvh in range(SWA_KV_HEADS)]
    sinks = [jnp.concatenate([jnp.full((tq, 1), sink_ref[4 * kvh + j] * LOG2E, F32) for j in range(4)], axis=0)
             for kvh in range(SWA_KV_HEADS)]
    return q_stacks, sinks


def _attn_cd_prompt_kernel(sink_ref, sq_ref, sk_ref, sv_ref, nq_ref, nk_ref, nv_ref, o_ref):
    lo = _lane_lo((SEQ, LANES))
    for sq in range(PROMPT_SEQS_PER_STEP):
        r = slice(SEQ * sq, SEQ * (sq + 1))
        sk = sk_ref[r, :]
        sv = sv_ref[r, :]
        for j in range(4):
            sl = slice(LANES * j, LANES * (j + 1))
            q_lo, q_hi = _split_halves(sq_ref[r, sl])
            o_lo = _softmax_pv([_dot_nt(q_lo, sk)], [sv], sink=sink_ref[j] * LOG2E)
            o_hi = _softmax_pv([_dot_nt(q_hi, sk)], [sv], sink=sink_ref[j + 4] * LOG2E)
            o_ref[r, sl] = jnp.where(lo, o_lo, o_hi).astype(BF16)
        for j in range(4):
            sl = slice(LANES * j, LANES * (j + 1))
            q_lo, q_hi = _split_halves(nq_ref[r, sl])
            k = nk_ref[r, sl]
            v = nv_ref[r, sl]
            o_lo = _softmax_pv([_dot_nt(q_lo, k)], [v])
            o_hi = _softmax_pv([_dot_nt(q_hi, k)], [v])
            o_ref[r, 512 + LANES * j: 512 + LANES * (j + 1)] = jnp.where(lo, o_lo, o_hi).astype(BF16)


def _attn_cd_prompt(sink, sq, sk, sv, nq, nk, nv):
    def spec(w):
        return pl.BlockSpec((SEQ * PROMPT_SEQS_PER_STEP, w), lambda b: (b, 0))
    return pl.pallas_call(
        _attn_cd_prompt_kernel,
        grid=(BATCH // PROMPT_SEQS_PER_STEP,),
        in_specs=[pl.BlockSpec(memory_space=pltpu.SMEM), spec(512), spec(128), spec(128),
                  spec(512), spec(512), spec(512)],
        out_specs=spec(1024),
        out_shape=jax.ShapeDtypeStruct((N_TOK, 1024), BF16),
        compiler_params=_cparams(1),
        name="attn_cd_prompt",
    )(sink, sq, sk, sv, nq, nk, nv)


_SWA_KEYS = TQ + 2 * SWA_WINDOW


def _attn_cd_sample_kernel(init_ref, sink_ref, sq_ref, nq_ref, sk_ref, sv_ref, nk_ref, nv_ref,
                           skc_ref, svc_ref, nkc_ref, nvc_ref, bias_ref, o_ref):
    del init_ref
    qi = pl.program_id(1)
    lo = _lane_lo((TQ, LANES))
    ks = pl.multiple_of(jnp.clip(qi * TQ - SWA_WINDOW, 0, DEC_SEQ - _SWA_KEYS), SWA_WINDOW)
    k_win = sk_ref[pl.ds(ks, _SWA_KEYS), :]
    v_win = sv_ref[pl.ds(ks, _SWA_KEYS), :]
    q_pos = qi * TQ + (lax.broadcasted_iota(jnp.int32, (4 * TQ, _SWA_KEYS), 0) & (TQ - 1))
    k_pos = ks + lax.broadcasted_iota(jnp.int32, (4 * TQ, _SWA_KEYS), 1)
    in_win = jnp.abs(q_pos - k_pos) <= SWA_WINDOW
    skc = skc_ref[...]
    svc = svc_ref[...]
    q_stacks, sinks = _gqa_stacks(sq_ref, sink_ref)

    def window_scores(kvh):
        return [_dot_nt(q_stacks[kvh], skc), jnp.where(in_win, _dot_nt(q_stacks[kvh], k_win), NEG)]

    o_kv = {}

    def window_finish(kvh, scores):
        o_kv[kvh] = _softmax_pv(scores, [svc, v_win], sink=sinks[kvh])
        if kvh == SWA_KV_HEADS - 1:
            for j in range(4):
                rows = slice(TQ * j, TQ * (j + 1))
                o_ref[:, LANES * j: LANES * (j + 1)] = jnp.where(lo, o_kv[0][rows], o_kv[1][rows]).astype(BF16)

    n_rows = DEC_SEQ // GRID_W
    r0 = jnp.clip(qi * NA_TILE_ROWS - NA_WIN_ROWS // 2, 0, n_rows - NA_KEY_ROWS)
    kn = pl.multiple_of(r0 * GRID_W, GRID_W)

    lane_lo = _lane_lo((1, LANES))
    pieces = []
    for ri in range(NA_TILE_ROWS):
        r = qi * NA_TILE_ROWS + ri
        rs = jnp.clip(r - NA_WIN_ROWS // 2, 0, n_rows - NA_WIN_ROWS)
        row = []
        for mm in range(NA_KEY_ROWS // 2):
            kr = r0 + 2 * mm
            ok = [(kr + t >= rs) & (kr + t < rs + NA_WIN_ROWS) for t in range(2)]
            mask = jnp.where(lane_lo, jnp.where(ok[0], 0.0, NEG), jnp.where(ok[1], 0.0, NEG))
            row.append((jnp.clip(kr - r + NA_WIN_ROWS, 0, 2 * NA_WIN_ROWS - 1), mask))
        pieces.append(row)

    def add_na_bias(hd, s):
        return jnp.concatenate(
            [jnp.concatenate([s[GRID_W * ri: GRID_W * (ri + 1), LANES * mm: LANES * (mm + 1)] + bias_ref[hd, d] + mask
                              for mm, (d, mask) in enumerate(row)], axis=1)
             for ri, row in enumerate(pieces)], axis=0)

    def na_scores(hd):
        sl = slice(LANES * (hd // 2), LANES * (hd // 2 + 1))
        q_half = _split_halves(nq_ref[:, sl])[hd % 2]
        nk_win = nk_ref[pl.ds(kn, NA_KEY_ROWS * GRID_W), sl]
        return [_dot_nt(q_half, nkc_ref[:, sl]), add_na_bias(hd, _dot_nt(q_half, nk_win))]

    pair = {}

    def na_finish(hd, scores):
        j = hd // 2
        sl = slice(LANES * j, LANES * (j + 1))
        nv_win = nv_ref[pl.ds(kn, NA_KEY_ROWS * GRID_W), sl]
        pair[hd % 2] = _softmax_pv(scores, [nvc_ref[:, sl], nv_win])
        if hd % 2 == 1:
            o_ref[:, 512 + LANES * j: 512 + LANES * (j + 1)] = jnp.where(lo, pair[0], pair[1]).astype(BF16)

    _pipelined_units([(window_scores, window_finish, kvh) for kvh in range(SWA_KV_HEADS)]
                     + [(na_scores, na_finish, hd) for hd in range(NA_HEADS)])


def _attn_cd_sample(out_init, sink, sq, sk, sv, nq, nk, nv, skc, svc, nkc, nvc, bias):
    nq_t = DEC_SEQ // TQ
    q_off = NP_TOK // TQ
    b_off = NP_TOK // DEC_SEQ

    def qspec(w):
        return pl.BlockSpec((TQ, w), lambda b, i: (q_off + b * nq_t + i, 0))

    def kspec(w):
        return pl.BlockSpec((DEC_SEQ, w), lambda b, i: (b_off + b, 0))

    def cspec(w):
        return pl.BlockSpec((PAST_LEN, w), lambda b, i: (b, 0))

    bias_spec = pl.BlockSpec((NA_HEADS, 2 * NA_WIN_ROWS, GRID_W, LANES), lambda b, i: (0, 0, 0, 0))
    return pl.pallas_call(
        _attn_cd_sample_kernel,
        grid=(DEC_BATCH, nq_t),
        in_specs=[pl.BlockSpec(memory_space=pl.ANY), pl.BlockSpec(memory_space=pltpu.SMEM),
                  qspec(512), qspec(512), kspec(128), kspec(128), kspec(512), kspec(512),
                  cspec(128), cspec(128), cspec(512), cspec(512), bias_spec],
        out_specs=pl.BlockSpec((TQ, 1024), lambda b, i: (q_off + b * nq_t + i, 0)),
        out_shape=jax.ShapeDtypeStruct((N_TOK, 1024), BF16),
        input_output_aliases={0: 0},
        compiler_params=_cparams(2),
        name="attn_cd_sample",
    )(out_init, sink, sq, nq, sk, sv, nk, nv, skc, svc, nkc, nvc, bias)


def _na_bias(rpb):
    n_dc = 2 * NA_WIN_COLS - 1
    c = np.arange(GRID_W)[:, None]
    kc = np.arange(GRID_W)[None, :]
    qs = np.clip(c - NA_WIN_COLS // 2, 0, GRID_W - NA_WIN_COLS)
    col_ok = (kc >= qs) & (kc < qs + NA_WIN_COLS)
    dc = np.clip(kc - c + NA_WIN_COLS - 1, 0, n_dc - 1)
    onehot = ((dc[None] == np.arange(n_dc)[:, None, None]) & col_ok[None]).astype(np.float32)
    blocks = jnp.einsum('hrd,dck->hrck', rpb.astype(F32) * LOG2E, onehot, precision=lax.Precision.HIGHEST)
    blocks = jnp.where(col_ok[None, None], blocks, NEG)
    none = jnp.full((NA_HEADS, 1, GRID_W, GRID_W), NEG, F32)
    return jnp.concatenate([jnp.concatenate([none, blocks], axis=1),
                            jnp.concatenate([blocks, none], axis=1)], axis=-1)


_HI_MASK = -65536


def _pack_pairs(x):
    w = x.shape[1] // 2
    r = x.astype(BF16).astype(F32)
    lo = lax.bitcast_convert_type(r[:, :w], jnp.int32)
    hi = lax.bitcast_convert_type(r[:, w:], jnp.int32)
    return (hi & _HI_MASK) | lax.shift_right_logical(lo, 16)


def _unpack_pairs(p):
    lo = lax.bitcast_convert_type(lax.shift_left(p, 16), F32)
    hi = lax.bitcast_convert_type(p & _HI_MASK, F32)
    return lo, hi


def _postmix_kernel(*refs, split_x):
    if split_x:
        o_ref, xp_ref, xs_ref = refs[:3]
        refs = refs[3:]
    else:
        o_ref, x_ref = refs[:2]
        refs = refs[2:]
    (mod_ref, g_ref, wout_ref, wr_ref, br_ref, tri_ref,
     x1_ref, h2_ref, route_ref, gate_ref, cnt_ref, run_ref) = refs
    i = pl.program_id(0)

    @pl.when(i == 0)
    def _():
        run_ref[...] = jnp.zeros_like(run_ref)

    m = mod_ref[0]
    g = g_ref[...]
    gate_gain = g[1:2] * m[2:3]
    ffn_gain = g[2:3] * (1.0 + m[4:5])
    sub_logits = {}

    def project(n):
        return _dot(o_ref[_sub_rows(n), :], wout_ref[...])

    def finish(n, y):
        r = _sub_rows(n)
        x = jnp.where(i < NPT, xp_ref[r, :], xs_ref[r, :]) if split_x else x_ref[r, :]
        x1 = x + _rms(y, gate_gain)
        x1_ref[r, :] = x1
        h2 = _rms(x1, ffn_gain) + m[3:4]
        h2_ref[r, :] = _pack_pairs(h2)
        sub_logits[n] = _dot(h2.astype(BF16), wr_ref[...]) + br_ref[...]

    _pipelined_units([(project, finish, n) for n in range(N_SUB)])

    logits = jnp.concatenate([sub_logits[n] for n in range(N_SUB)], axis=0)
    lane = lax.broadcasted_iota(jnp.int32, logits.shape, 1).astype(F32)
    cur = jnp.where(lane < N_EXPERTS, logits, -jnp.inf)
    tops, idxs = [], []
    for _ in range(TOP_K):
        mx = jnp.max(cur, axis=-1, keepdims=True)
        ix = jnp.min(jnp.where(cur == mx, lane, float(LANES)), axis=-1, keepdims=True)
        tops.append(mx)
        idxs.append(ix)
        cur = jnp.where(lane == ix, -jnp.inf, cur)
    es = [jnp.exp(t - tops[0]) for t in tops]
    inv = 1.0 / (es[0] + es[1] + es[2] + es[3])
    picked = jnp.zeros_like(logits)
    for k in range(TOP_K):
        picked = jnp.where(lane == idxs[k], 1.0, picked)
    before = _dot(tri_ref[...], picked.astype(BF16)) + run_ref[0:1, :]
    route = jnp.zeros_like(logits)
    gate_out = jnp.zeros_like(logits)
    for k in range(TOP_K):
        rank = jnp.sum(jnp.where(lane == idxs[k], before, 0.0), axis=-1, keepdims=True)
        route = jnp.where(lane == float(k), idxs[k], route)
        route = jnp.where(lane == float(TOP_K + k), rank, route)
        gate_out = jnp.where(lane == float(k), es[k] * inv, gate_out)
    route_ref[...] = route.T[:2 * TOP_K].astype(jnp.int32)
    gate_ref[...] = gate_out
    run_ref[...] = run_ref[...] + jnp.sum(picked, axis=0, keepdims=True)
    cnt_ref[...] = run_ref[...].astype(jnp.int32)


def _postmix(o_cat, xs, mod_l, g_l, w_out, w_r, b_r):
    tri = jnp.asarray(np.tril(np.ones((TM, TM), np.float32), -1), BF16)
    split_x = len(xs) == 2
    x_specs = [_PROMPT_SPEC, _SAMPLE_SPEC] if split_x else [_tok_spec(D_MODEL)]
    return pl.pallas_call(
        functools.partial(_postmix_kernel, split_x=split_x),
        grid=(N_TILES,),
        in_specs=[_tok_spec(1024)] + x_specs + [_MOD_SPEC, _const_spec((4, D_MODEL)),
                  _const_spec((1024, D_MODEL)), _const_spec((D_MODEL, LANES)), _const_spec((1, LANES)),
                  _const_spec((TM, TM))],
        out_specs=[_tok_spec(D_MODEL), _tok_spec(D_MODEL // 2), pl.BlockSpec((2 * TOP_K, TM), lambda i: (0, i)),
                   _tok_spec(LANES), _const_spec((8, LANES))],
        out_shape=[jax.ShapeDtypeStruct((N_TOK, D_MODEL), F32),
                   jax.ShapeDtypeStruct((N_TOK, D_MODEL // 2), jnp.int32),
                   jax.ShapeDtypeStruct((2 * TOP_K, N_TOK), jnp.int32), jax.ShapeDtypeStruct((N_TOK, LANES), F32),
                   jax.ShapeDtypeStruct((8, LANES), jnp.int32)],
        scratch_shapes=[pltpu.VMEM((8, LANES), F32)],
        compiler_params=_cparams(1),
        name="postmix",
    )(o_cat, *xs, mod_l, g_l, w_out, w_r, b_r, tri)


SC_WORKERS = 32
SC_ROWS = 64
ROW_WORDS = D_MODEL // 2

_SC_SCRATCH = [pltpu.VMEM((SC_ROWS, ROW_WORDS), jnp.int32), pltpu.VMEM((SC_ROWS, ROW_WORDS), jnp.int32),
               pltpu.SemaphoreType.DMA, pltpu.SemaphoreType.DMA, pltpu.SemaphoreType.DMA, pltpu.SemaphoreType.DMA]


def _sc_worker_id():
    return lax.axis_index("s") * 2 + lax.axis_index("c")


def _sc_double_buffered(n_chunks, load, store):
    for cp in load(0, 0):
        cp.start()

    @pl.loop(0, n_chunks, step=2)
    def _(g0):
        for b in range(2):
            g = g0 + b
            for cp in load(g, b):
                cp.wait()

            @pl.when(g >= 1)
            def _():
                for cp in store(g - 1, 1 - b):
                    cp.wait()

            @pl.when(g + 1 < n_chunks)
            def _():
                for cp in load(g + 1, 1 - b):
                    cp.start()

            for cp in store(g, b):
                cp.start()

    for cp in store(n_chunks - 1, (n_chunks - 1) % 2):
        cp.wait()


def _sc_dispatch(src, idx):
    n_chunks = N_TOK // (SC_WORKERS * SC_ROWS)
    assert n_chunks % 2 == 0
    mesh = plsc.VectorSubcoreMesh(core_axis_name="c", subcore_axis_name="s")

    @functools.partial(
        pl.kernel, mesh=mesh,
        out_type=jax.ShapeDtypeStruct((MOE_ROWS, ROW_WORDS), jnp.int32),
        scratch_types=[pltpu.VMEM((n_chunks * TOP_K, SC_ROWS), jnp.int32)] + _SC_SCRATCH)
    def k(src_hbm, idx_hbm, out_hbm, idx_v, buf0, buf1, in0, in1, out0, out1):
        wid = _sc_worker_id()
        pltpu.sync_copy(idx_hbm.at[wid], idx_v)
        bufs, in_sems, out_sems = (buf0, buf1), (in0, in1), (out0, out1)

        def load(g, b):
            rows = pl.ds((wid * n_chunks + g) * SC_ROWS, SC_ROWS)
            return [pltpu.make_async_copy(src_hbm.at[rows], bufs[b], in_sems[b])]

        def store(g, b):
            return [pltpu.make_async_copy(bufs[b], out_hbm.at[idx_v.at[g * TOP_K + kk]], out_sems[b])
                    for kk in range(TOP_K)]

        _sc_double_buffered(n_chunks, load, store)

    return k(src, idx)


def _sc_collect(table, idx):
    n_chunks = idx.shape[1]
    assert n_chunks % 2 == 0
    mesh = plsc.VectorSubcoreMesh(core_axis_name="c", subcore_axis_name="s")

    @functools.partial(
        pl.kernel, mesh=mesh,
        out_type=jax.ShapeDtypeStruct((SC_WORKERS * n_chunks * SC_ROWS, ROW_WORDS), jnp.int32),
        scratch_types=[pltpu.VMEM((n_chunks, SC_ROWS), jnp.int32)] + _SC_SCRATCH)
    def k(table_hbm, idx_hbm, out_hbm, idx_v, buf0, buf1, in0, in1, out0, out1):
        wid = _sc_worker_id()
        pltpu.sync_copy(idx_hbm.at[wid], idx_v)
        bufs, in_sems, out_sems = (buf0, buf1), (in0, in1), (out0, out1)

        def load(g, b):
            return [pltpu.make_async_copy(table_hbm.at[idx_v.at[g]], bufs[b], in_sems[b])]

        def store(g, b):
            rows = pl.ds((wid * n_chunks + g) * SC_ROWS, SC_ROWS)
            return [pltpu.make_async_copy(bufs[b], out_hbm.at[rows], out_sems[b])]

        _sc_double_buffered(n_chunks, load, store)

    return k(table, idx)


def _expert_rows(words, n_valid, wgu_b, wd_b, bgu, bd):
    live = lax.broadcasted_iota(jnp.int32, words.shape, 0) < n_valid
    lo, hi = _unpack_pairs(jnp.where(live, words, 0))
    x = jnp.concatenate([lo, hi], axis=1).astype(BF16)
    gu = _dot(x, wgu_b[...]) + bgu
    g = jnp.minimum(gu[:, :D_EXPERT], SWIGLU_LIMIT)
    u = jnp.clip(gu[:, D_EXPERT:], -SWIGLU_LIMIT, SWIGLU_LIMIT)
    a = g * jax.nn.sigmoid(SWIGLU_ALPHA * g) * (u + 1.0)
    return _pack_pairs(_dot(a.astype(BF16), wd_b[...]) + bd)


def _moe_kernel(blk_e_ref, blk_first_ref, blk_rows_ref, blk_slot_ref, blk_next_ref, blk_io_ref,
                x_ref, wgu_hbm, bgu_ref, wd_hbm, bd_ref, y_ref,
                wgu_f, wd_f, wgu_b, wd_b, sem, *, layer):
    del blk_io_ref
    i = pl.program_id(0)
    n_valid = blk_rows_ref[i]
    quantum = MOE_TM // MOE_TAIL_PARTS

    def weight_copies(e, slot):
        return (pltpu.make_async_copy(wgu_hbm.at[layer, e], wgu_f.at[slot], sem.at[0, slot]),
                pltpu.make_async_copy(wd_hbm.at[layer, e], wd_f.at[slot], sem.at[1, slot]))

    @pl.when(i == 0)
    def _():
        for cp in weight_copies(blk_e_ref[0], blk_slot_ref[0]):
            cp.start()

    @pl.when(blk_first_ref[i] == 1)
    def _():
        slot = blk_slot_ref[i]
        for cp in weight_copies(blk_e_ref[i], slot):
            cp.wait()
        nxt = blk_next_ref[i]

        @pl.when(nxt >= 0)
        def _():
            for cp in weight_copies(nxt, 1 - slot):
                cp.start()

        wgu_b[...] = wgu_f[slot].astype(BF16)
        wd_b[...] = wd_f[slot].astype(BF16)

    for parts in range(1, MOE_TAIL_PARTS + 1):
        rows = parts * quantum

        @pl.when((n_valid > rows - quantum) & (n_valid <= rows))
        def _(rows=rows):
            y_ref[:rows] = _expert_rows(x_ref[:rows], n_valid, wgu_b, wd_b, bgu_ref[0, 0], bd_ref[0, 0])
            if rows < MOE_TM:
                y_ref[rows:] = jnp.zeros((MOE_TM - rows, ROW_WORDS), jnp.int32)


def _moe(layer, blk_meta, xs, w_gu, b_gu, w_down, b_down):
    def row_map(i, e, first, rows, slot, nxt, io):
        return (io[i], 0)

    def bias_map(i, e, *_):
        return (layer, e[i], 0, 0)

    grid_spec = pltpu.PrefetchScalarGridSpec(
        num_scalar_prefetch=6,
        grid=(MOE_BLOCKS,),
        in_specs=[
            pl.BlockSpec((MOE_TM, ROW_WORDS), row_map),
            pl.BlockSpec(memory_space=pl.ANY),
            pl.BlockSpec((1, 1, 1, 2 * D_EXPERT), bias_map),
            pl.BlockSpec(memory_space=pl.ANY),
            pl.BlockSpec((1, 1, 1, D_MODEL), bias_map),
        ],
        out_specs=pl.BlockSpec((MOE_TM, ROW_WORDS), row_map),
        scratch_shapes=[pltpu.VMEM((2, D_MODEL, 2 * D_EXPERT), F32), pltpu.VMEM((2, D_EXPERT, D_MODEL), F32),
                        pltpu.VMEM((D_MODEL, 2 * D_EXPERT), BF16), pltpu.VMEM((D_EXPERT, D_MODEL), BF16),
                        pltpu.SemaphoreType.DMA((2, 2))],
    )
    return pl.pallas_call(
        functools.partial(_moe_kernel, layer=layer),
        grid_spec=grid_spec,
        out_shape=jax.ShapeDtypeStruct((MOE_ROWS, ROW_WORDS), jnp.int32),
        compiler_params=_cparams(1),
        name="moe_experts",
    )(*blk_meta, xs, w_gu, b_gu.reshape(DEPTH, N_EXPERTS, 1, 2 * D_EXPERT),
      w_down, b_down.reshape(DEPTH, N_EXPERTS, 1, D_MODEL))


def _route(route, counts):
    experts = jnp.arange(N_EXPERTS, dtype=jnp.int32)
    padded = (counts + MOE_TM - 1) // MOE_TM * MOE_TM
    pad_end = jnp.cumsum(padded)
    pad_start = pad_end - padded
    e = route[0:TOP_K]
    onehot = e[:, :, None] == experts[None, None, :]
    dest = jnp.sum(jnp.where(onehot, pad_start[None, None, :], 0), axis=-1) + route[TOP_K:2 * TOP_K]
    blk_row0 = jnp.arange(MOE_BLOCKS, dtype=jnp.int32) * MOE_TM
    blk_e = jnp.minimum(jnp.sum((pad_end[None, :] <= blk_row0[:, None]).astype(jnp.int32), axis=1),
                        N_EXPERTS - 1)
    on = blk_row0 < pad_end[-1]
    n_on = jnp.sum(on.astype(jnp.int32))
    blk_onehot = blk_e[:, None] == experts[None, :]
    row_end = jnp.sum(jnp.where(blk_onehot, (pad_start + counts)[None, :], 0), axis=1)
    blk_rows = jnp.where(on, jnp.clip(row_end - blk_row0, 0, MOE_TM), 0).astype(jnp.int32)
    last_e = jnp.sum(jnp.where(jnp.arange(MOE_BLOCKS) == n_on - 1, blk_e, 0))
    blk_e = jnp.where(on, blk_e, last_e).astype(jnp.int32)
    prev = jnp.concatenate([jnp.full((1,), -1, jnp.int32), blk_e[:-1]])
    blk_first = (blk_e != prev).astype(jnp.int32)
    blk_slot = ((jnp.cumsum(blk_first) - 1) % 2).astype(jnp.int32)
    later_used = (experts[None, :] > experts[:, None]) & (counts[None, :] > 0)
    next_used = jnp.min(jnp.where(later_used, experts[None, :], N_EXPERTS), axis=1)
    next_used = jnp.where(next_used == N_EXPERTS, -1, next_used)
    blk_next = jnp.sum(jnp.where(blk_e[:, None] == experts[None, :], next_used[None, :], 0), axis=1)
    blk_io = jnp.minimum(jnp.arange(MOE_BLOCKS, dtype=jnp.int32), n_on - 1)
    return dest, (blk_e, blk_first, blk_rows, blk_slot, blk_next.astype(jnp.int32), blk_io)


def _combine_kernel(y_ref, gate_ref, x1_ref, mod_ref, g_ref, *o_refs):
    npt = NP_TOK // TM_WIDE
    i = pl.program_id(0)
    m = mod_ref[0]
    gate = gate_ref[...]
    acc_lo = None
    for k in range(TOP_K):
        lo, hi = _unpack_pairs(y_ref[k])
        gk = gate[:, k:k + 1]
        acc_lo = gk * lo if acc_lo is None else acc_lo + gk * lo
        acc_hi = gk * hi if k == 0 else acc_hi + gk * hi
    acc = jnp.concatenate([acc_lo, acc_hi], axis=1)
    out = x1_ref[...] + _rms(acc, g_ref[...][3:4] * m[5:6])
    if len(o_refs) == 1:
        o_refs[0][...] = out
    else:
        @pl.when(i < npt)
        def _():
            o_refs[0][...] = out

        @pl.when(i >= npt)
        def _():
            o_refs[1][...] = out


def _combine(yg, gates, x1, mod_l, g_l, split_out):
    tm = TM_WIDE
    npt = NP_TOK // tm

    def tok(w):
        return pl.BlockSpec((tm, w), lambda i: (i, 0))

    if split_out:
        out_specs = [pl.BlockSpec((tm, D_MODEL), lambda i: (jnp.minimum(i, npt - 1), 0)),
                     pl.BlockSpec((tm, D_MODEL), lambda i: (jnp.maximum(i - npt, 0), 0))]
        out_shape = [jax.ShapeDtypeStruct((NP_TOK, D_MODEL), F32), jax.ShapeDtypeStruct((NS_TOK, D_MODEL), F32)]
    else:
        out_specs = [tok(D_MODEL)]
        out_shape = [jax.ShapeDtypeStruct((N_TOK, D_MODEL), F32)]
    mod_spec = pl.BlockSpec((1, 6, D_MODEL), lambda i: (jnp.where(i < npt, 0, 1 + (i - npt) // (DEC_SEQ // tm)), 0, 0))
    return pl.pallas_call(
        _combine_kernel,
        grid=(N_TOK // tm,),
        in_specs=[pl.BlockSpec((TOP_K, tm, ROW_WORDS), lambda i: (0, i, 0)), tok(LANES),
                  tok(D_MODEL), mod_spec, _const_spec((4, D_MODEL))],
        out_specs=out_specs,
        out_shape=out_shape,
        compiler_params=_cparams(1),
        name="combine",
    )(yg, gates, x1, mod_l, g_l)


def _ffn(layer, o_cat, xs, mod_l, g_l, w_out, w_router, b_router, w_gu, b_gu, w_down, b_down, split_out):
    w_r = jnp.pad(w_router, ((0, 0), (0, LANES - N_EXPERTS))).astype(BF16)
    b_r = jnp.pad(b_router, (0, LANES - N_EXPERTS)).reshape(1, LANES)
    x1, h2p, route, gate_slab, counts = _postmix(o_cat, xs, mod_l, g_l, w_out.astype(BF16), w_r, b_r)
    dest, blk_meta = _route(route, counts[0, :N_EXPERTS])
    n_chunks = N_TOK // (SC_WORKERS * SC_ROWS)
    idx_d = dest.reshape(TOP_K, SC_WORKERS, n_chunks, SC_ROWS).transpose(1, 2, 0, 3).reshape(
        SC_WORKERS, n_chunks * TOP_K, SC_ROWS)
    rows = _sc_dispatch(h2p, idx_d)
    ys = _moe(layer, blk_meta, rows, w_gu, b_gu, w_down, b_down)
    idx_c = dest.reshape(SC_WORKERS, TOP_K * n_chunks, SC_ROWS)
    yg = _sc_collect(ys, idx_c).reshape(TOP_K, N_TOK, ROW_WORDS)
    return _combine(yg, gate_slab, x1, mod_l, g_l, split_out)


def _pad_heads(w, n_heads, width, keep):
    k = w.shape[0]
    w = w.reshape(k, n_heads, width)[:, :, :keep]
    return jnp.pad(w, ((0, 0), (0, 0), (0, LANES - keep))).reshape(k, n_heads * LANES)


def _pe_slab(x):
    return jnp.pad(x, [(0, 0)] * (x.ndim - 1) + [(MLA_NOPE, LANES - MLA_NOPE - MLA_ROPE)])


def _pair_kv_heads(w):
    g = SWA_HEADS // SWA_KV_HEADS
    return w.reshape(SWA_KV_HEADS, g, HEAD_DIM, -1).transpose(1, 0, 2, 3).reshape(w.shape)


def kernel(x_prompt, x_sample, cache_mla_ckv, cache_mla_krope, cache_diff_k, cache_diff_v, cache_swa_k, cache_swa_v, cache_na_k, cache_na_v, c, c_ctx, w_mod, b_mod, norm_g, w_in0, mla_q_norm, w_uq, mla_kv_norm, w_ukv, diff_lambda, diff_norm, w_out0, w_in1, swa_sink, na_rpb, w_out1, w_router, b_router, w_gu, b_gu, w_down, b_down):
    xs = (x_prompt.reshape(NP_TOK, D_MODEL), x_sample.reshape(NS_TOK, D_MODEL))
    cond = jnp.concatenate([c_ctx[None, :], c, jnp.zeros((16 - 1 - DEC_BATCH, D_MODEL), F32)], axis=0)
    mod = _modulation(cond, w_mod, b_mod).reshape(DEPTH, 16, 6, D_MODEL)
    t64, t32 = _rope_tables()
    states = {}
    for l in range(DEPTH):
        i = l // 2
        g_l = norm_g[l]
        mod_l = mod[l]
        if l % 2 == 0:
            lam_init = 0.8 - 0.6 * math.exp(-0.3 * l)
            wi = w_in0[i]
            w_in_p = jnp.concatenate(
                [wi[:, 0:640], wi[:, 672:2208], _pe_slab(wi[:, 640:672])], axis=1).astype(BF16)
            w_uq_p = _pad_heads(w_uq[i], MLA_HEADS, MLA_NOPE + MLA_ROPE, MLA_NOPE + MLA_ROPE).astype(BF16)
            w_k_p = _pad_heads(w_ukv[i], MLA_HEADS, MLA_NOPE + MLA_V, MLA_NOPE).astype(BF16)
            w_v = w_ukv[i].reshape(MLA_KV_RANK, MLA_HEADS, MLA_NOPE + MLA_V)[:, :, MLA_NOPE:].reshape(
                MLA_KV_RANK, MLA_HEADS * MLA_V).astype(BF16)
            (q, k, v, dq, dk, dv, ckv_st, kpe_st, dk_st, dv_st) = _premix0(
                *xs, mod_l, g_l[0:1], w_in_p, mla_q_norm[i][None, :], w_uq_p, mla_kv_norm[i][None, :],
                w_k_p, w_v, t32, t64)
            states['mla_ckv'] = ckv_st.reshape(BATCH, 1, SEQ, MLA_KV_RANK)
            states['mla_krope'] = kpe_st[:, MLA_NOPE:MLA_NOPE + MLA_ROPE].reshape(BATCH, 1, SEQ, MLA_ROPE)
            states['diff_k'] = dk_st.reshape(BATCH, 1, SEQ, DIFF_HEADS, 2 * DIFF_DH)
            states['diff_v'] = dv_st.reshape(BATCH, 1, SEQ, DIFF_HEADS, 2 * DIFF_DH)
            kc, vc = _mla_cache(cache_mla_ckv[:, i].reshape(DEC_BATCH * PAST_LEN, MLA_KV_RANK),
                                _pe_slab(cache_mla_krope[:, i].reshape(DEC_BATCH * PAST_LEN, MLA_ROPE)),
                                w_k_p, w_v)
            dkc = cache_diff_k[:, i].reshape(DEC_BATCH * PAST_LEN, 512).astype(BF16)
            dvc = cache_diff_v[:, i].reshape(DEC_BATCH * PAST_LEN, 512).astype(BF16)
            lam = diff_lambda[i]
            sub_g = diff_norm[i][None, :]
            o_p = _attn_ab(q, dq, (k, v, dk, dv), None, lam, sub_g, lam_init,
                           n_batch=BATCH, t_len=SEQ, tq=SEQ, tok_off=0, n_seq=PROMPT_SEQS_PER_STEP)
            o_cat = _attn_ab(q, dq, (k, v, dk, dv), (kc, vc, dkc, dvc), lam, sub_g, lam_init,
                             n_batch=DEC_BATCH, t_len=DEC_SEQ, tq=TQ_AB, tok_off=NP_TOK, out_init=o_p)
            w_out = w_out0[i]
        else:
            wi = w_in1[i]
            n_sq = SWA_HEADS * HEAD_DIM
            w_in_p = jnp.concatenate([_pair_kv_heads(wi[:, :n_sq].T).T, wi[:, n_sq:]], axis=1).astype(BF16)
            (sq, sk, sv, nq, nk, nv, sk_st, sv_st, nk_st, nv_st) = _premix1(xs[0], mod_l, g_l[0:1], w_in_p, t64)
            states['swa_k'] = sk_st.reshape(BATCH, 1, SEQ, SWA_KV_HEADS, HEAD_DIM)
            states['swa_v'] = sv_st.reshape(BATCH, 1, SEQ, SWA_KV_HEADS, HEAD_DIM)
            states['na_k'] = nk_st.reshape(BATCH, 1, SEQ, NA_HEADS, HEAD_DIM)
            states['na_v'] = nv_st.reshape(BATCH, 1, SEQ, NA_HEADS, HEAD_DIM)
            skc = cache_swa_k[:, i].reshape(DEC_BATCH * PAST_LEN, 128).astype(BF16)
            svc = cache_swa_v[:, i].reshape(DEC_BATCH * PAST_LEN, 128).astype(BF16)
            nkc = cache_na_k[:, i].reshape(DEC_BATCH * PAST_LEN, 512).astype(BF16)
            nvc = cache_na_v[:, i].reshape(DEC_BATCH * PAST_LEN, 512).astype(BF16)
            sink = swa_sink[i]
            o_p = _attn_cd_prompt(sink, sq, sk, sv, nq, nk, nv)
            o_cat = _attn_cd_sample(o_p, sink, sq, sk, sv, nq, nk, nv, skc, svc, nkc, nvc, _na_bias(na_rpb[i]))
            wo = w_out1[i]
            w_out = jnp.concatenate([_pair_kv_heads(wo[:n_sq]), wo[n_sq:]], axis=0)
        xs = _ffn(l, o_cat, xs, mod_l, g_l, w_out, w_router[l], b_router[l], w_gu, b_gu, w_down, b_down,
                  split_out=(l == DEPTH - 1))
    return (xs[0].reshape(BATCH, SEQ, D_MODEL), xs[1].reshape(DEC_BATCH, DEC_SEQ, D_MODEL),
            states['mla_ckv'], states['mla_krope'], states['diff_k'], states['diff_v'],
            states['swa_k'], states['swa_v'], states['na_k'], states['na_v'])
```

```python
import functools
import math

import numpy as np
import jax
import jax.numpy as jnp
from jax import lax
from jax.experimental import pallas as pl
from jax.experimental.pallas import tpu as pltpu
from jax.experimental.pallas import tpu_sc as plsc

F32 = jnp.float32
BF16 = jnp.bfloat16

D_MODEL = 1024
BATCH = 16
SEQ = 256
DEPTH = 2
DEC_BATCH = 8
DEC_SEQ = 2048
PAST_LEN = 256
GRID_W = 64
HEAD_DIM = 64
ROPE_THETA = 10000.0
EPS = 1e-6
NEG = -1e30

MLA_HEADS = 8
MLA_Q_RANK = 384
MLA_KV_RANK = 256
MLA_NOPE = 64
MLA_ROPE = 32
MLA_V = 64
DIFF_HEADS = 4
DIFF_DH = 64
SWA_HEADS = 8
SWA_KV_HEADS = 2
SWA_WINDOW = 128
NA_HEADS = 8
NA_WIN_ROWS = 8
NA_WIN_COLS = 16
N_EXPERTS = 32
TOP_K = 4
D_EXPERT = 1024
SWIGLU_LIMIT = 7.0
SWIGLU_ALPHA = 1.702

LANES = 128
NP_TOK = BATCH * SEQ
NS_TOK = DEC_BATCH * DEC_SEQ
N_TOK = NP_TOK + NS_TOK
TM = 512
NPT = NP_TOK // TM
TILES_PER_SAMPLE = DEC_SEQ // TM
N_TILES = N_TOK // TM
TM_WIDE = 1024
N_SUB = 2
SUB_TM = TM // N_SUB
TQ = 256
TQ_AB = 256
PROMPT_SEQS_PER_STEP = 2
MOE_TM = 1024
MOE_TAIL_PARTS = 8
MOE_ROWS = ((N_TOK * TOP_K + N_EXPERTS * (MOE_TM - 1)) // MOE_TM + 1) * MOE_TM
MOE_BLOCKS = MOE_ROWS // MOE_TM
NA_TILE_ROWS = TQ // GRID_W
NA_KEY_ROWS = 12
VMEM_LIMIT = 56 * 1024 * 1024


def _cparams(n_axes, vmem=VMEM_LIMIT):
    return pltpu.CompilerParams(dimension_semantics=("arbitrary",) * n_axes,
                                vmem_limit_bytes=vmem)


def _rms(x, g):
    return x * lax.rsqrt(jnp.mean(x * x, axis=-1, keepdims=True) + EPS) * g


def _dot(a, b):
    return jnp.dot(a, b, preferred_element_type=F32)


def _dot_nt(a, b):
    return lax.dot_general(a, b, (((1,), (1,)), ((), ())), preferred_element_type=F32)


def _rope(x, cos, sin_a, sin_b, half):
    return (x * cos + pltpu.roll(x, LANES - half, 1) * sin_a + pltpu.roll(x, half, 1) * sin_b)


def _pipelined_units(units):
    pending = units[0][0](units[0][2])
    for n, (_, finish, arg) in enumerate(units):
        following = units[n + 1][0](units[n + 1][2]) if n + 1 < len(units) else None
        finish(arg, pending)
        pending = following


def _sub_rows(n):
    return slice(SUB_TM * n, SUB_TM * (n + 1))


def _store_head_rows(st_ref, n, hd, n_heads, x):
    st_ref[pl.ds(SUB_TM * n * n_heads + hd, SUB_TM, stride=n_heads), :] = x


def _mod_row(i):
    return jnp.where(i < NPT, 0, 1 + (i - NPT) // TILES_PER_SAMPLE)


def _rope_blk(i):
    return jnp.where(i < NPT, TILES_PER_SAMPLE, (i - NPT) % TILES_PER_SAMPLE)


def _mod_kernel(c_ref, w_ref, b_ref, o_ref):
    c = c_ref[...]
    s = (c * jax.nn.sigmoid(c)).astype(BF16)
    o_ref[0] = _dot(s, w_ref[0].astype(BF16)) + b_ref[0]


def _modulation(cond, w_mod, b_mod):
    nb = 1024
    return pl.pallas_call(
        _mod_kernel,
        grid=(DEPTH, 6 * D_MODEL // nb),
        in_specs=[
            pl.BlockSpec((16, D_MODEL), lambda l, n: (0, 0)),
            pl.BlockSpec((1, D_MODEL, nb), lambda l, n: (l, 0, n)),
            pl.BlockSpec((1, 1, nb), lambda l, n: (l, 0, n)),
        ],
        out_specs=pl.BlockSpec((1, 16, nb), lambda l, n: (l, 0, n)),
        out_shape=jax.ShapeDtypeStruct((DEPTH, 16, 6 * D_MODEL), F32),
        compiler_params=_cparams(2),
        name="modulation",
    )(cond, w_mod, b_mod.reshape(DEPTH, 1, 6 * D_MODEL))


def _rope_tables():
    t = jnp.arange(DEC_SEQ)
    rows = (t // GRID_W).astype(F32)
    cols = (t % GRID_W).astype(F32)

    def angles(r):
        n = r // 4
        inv = ROPE_THETA ** (-jnp.arange(n, dtype=F32) / n)
        return jnp.concatenate([rows[:, None] * inv[None], cols[:, None] * inv[None]], axis=-1)

    def finish(cos, sa, sb):
        ident = (jnp.ones((TM, LANES), F32), jnp.zeros((TM, LANES), F32), jnp.zeros((TM, LANES), F32))
        return tuple(jnp.concatenate([a, b], axis=0) for a, b in zip((cos, sa, sb), ident))

    a64 = angles(64)
    c, s, z = jnp.cos(a64), jnp.sin(a64), jnp.zeros_like(a64)
    t64 = finish(jnp.concatenate([c, c, c, c], -1), jnp.concatenate([-s, z, -s, z], -1),
                 jnp.concatenate([z, s, z, s], -1))
    a32 = angles(32)
    c, s, z = jnp.cos(a32), jnp.sin(a32), jnp.zeros_like(a32)
    one64 = jnp.ones((DEC_SEQ, 64), F32)
    z64 = jnp.zeros((DEC_SEQ, 64), F32)
    z32 = jnp.zeros((DEC_SEQ, 32), F32)
    t32 = finish(jnp.concatenate([one64, c, c, z32], -1), jnp.concatenate([z64, -s, z, z32], -1),
                 jnp.concatenate([z64, z, s, z32], -1))
    return t64, t32


LOG2E = math.log2(math.e)
_DIFF_COLS = DIFF_HEADS * 2 * DIFF_DH
_AB_COLS = tuple(np.cumsum([0, MLA_Q_RANK, MLA_KV_RANK, _DIFF_COLS, _DIFF_COLS, _DIFF_COLS, LANES]).tolist())
_CD_COLS = tuple(np.cumsum([0, SWA_HEADS * HEAD_DIM, SWA_KV_HEADS * HEAD_DIM, SWA_KV_HEADS * HEAD_DIM,
                            NA_HEADS * HEAD_DIM, NA_HEADS * HEAD_DIM, NA_HEADS * HEAD_DIM]).tolist())
IN_COLS = _AB_COLS[-1]
assert IN_COLS == _CD_COLS[-1]
_MLA_SCALE = (MLA_NOPE + MLA_ROPE) ** -0.5 * LOG2E
_QSCALE = HEAD_DIM ** -0.5 * LOG2E


def _premix0_kernel(xp_ref, xs_ref, mod_ref, g_ref, win_ref, qn_ref, wuq_ref, kvn_ref, wk_ref, wv_ref,
                    c32_ref, sa32_ref, sb32_ref, c64_ref, sa64_ref, sb64_ref,
                    q_ref, k_ref, v_ref, dq_ref, dk_ref, dv_ref,
                    ckv_st, kpe_st, dk_st, dv_st):
    i = pl.program_id(0)
    m = mod_ref[0]
    gain = g_ref[...] * (1.0 + m[1:2])

    def project(n):
        r = _sub_rows(n)
        x = jnp.where(i < NPT, xp_ref[r, :], xs_ref[r, :])
        h = _rms(x, gain) + m[0:1]
        return _dot(h.astype(BF16), win_ref[...])

    states = {}

    def finish(n, proj):
        r = _sub_rows(n)
        q_a, kv_a, dq, dk, dv, pe = (proj[:, a:b] for a, b in zip(_AB_COLS[:-1], _AB_COLS[1:]))
        q = _dot(_rms(q_a, qn_ref[...]).astype(BF16), wuq_ref[...])
        ckv = _rms(kv_a, kvn_ref[...])
        ckv_b = ckv.astype(BF16)
        kn = _dot(ckv_b, wk_ref[...])
        v_ref[r, :] = _dot(ckv_b, wv_ref[...]).astype(BF16)
        c32, sa32, sb32 = c32_ref[r, :], sa32_ref[r, :], sb32_ref[r, :]
        c64, sa64, sb64 = c64_ref[r, :], sa64_ref[r, :], sb64_ref[r, :]
        pe_r = _rope(pe, c32, sa32, sb32, MLA_ROPE // 2)
        for hd in range(MLA_HEADS):
            sl = slice(LANES * hd, LANES * (hd + 1))
            q_ref[r, sl] = (_rope(q[:, sl], c32, sa32, sb32, MLA_ROPE // 2) * _MLA_SCALE).astype(BF16)
            k_ref[r, sl] = (kn[:, sl] + pe_r).astype(BF16)
        for hd in range(DIFF_HEADS):
            sl = slice(LANES * hd, LANES * (hd + 1))
            dq_ref[r, sl] = (_rope(dq[:, sl], c64, sa64, sb64, DIFF_DH // 2) * _QSCALE).astype(BF16)
            dk_ref[r, sl] = _rope(dk[:, sl], c64, sa64, sb64, DIFF_DH // 2).astype(BF16)
        dv_ref[r, :] = dv.astype(BF16)
        states[n] = (ckv, pe, dk, dv)

    _pipelined_units([(project, finish, n) for n in range(N_SUB)])

    @pl.when(i < NPT)
    def _():
        for n in range(N_SUB):
            r = _sub_rows(n)
            ckv, pe, dk, dv = states[n]
            ckv_st[r, :] = ckv
            kpe_st[r, :] = pe
            for hd in range(DIFF_HEADS):
                _store_head_rows(dk_st, n, hd, DIFF_HEADS, dk[:, LANES * hd: LANES * (hd + 1)])
                _store_head_rows(dv_st, n, hd, DIFF_HEADS, dv[:, LANES * hd: LANES * (hd + 1)])


def _premix1_kernel(x_ref, mod_ref, g_ref, win_ref, c64_ref, sa64_ref, sb64_ref,
                    sq_ref, sk_ref, sv_ref, nq_ref, nk_ref, nv_ref,
                    sk_st, sv_st, nk_st, nv_st):
    i = pl.program_id(0)
    m = mod_ref[0]
    gain = g_ref[...] * (1.0 + m[1:2])

    def project(n):
        h = _rms(x_ref[_sub_rows(n), :], gain) + m[0:1]
        return _dot(h.astype(BF16), win_ref[...])

    states = {}

    def finish(n, proj):
        r = _sub_rows(n)
        sq, sk, sv, nq, nk, nv = (proj[:, a:b] for a, b in zip(_CD_COLS[:-1], _CD_COLS[1:]))
        c64, sa64, sb64 = c64_ref[r, :], sa64_ref[r, :], sb64_ref[r, :]
        for hd in range(4):
            sl = slice(LANES * hd, LANES * (hd + 1))
            sq_ref[r, sl] = (_rope(sq[:, sl], c64, sa64, sb64, HEAD_DIM // 2) * _QSCALE).astype(BF16)
        sk_ref[r, :] = _rope(sk, c64, sa64, sb64, HEAD_DIM // 2).astype(BF16)
        sv_ref[r, :] = sv.astype(BF16)
        nq_ref[r, :] = (nq * _QSCALE).astype(BF16)
        nk_ref[r, :] = nk.astype(BF16)
        nv_ref[r, :] = nv.astype(BF16)
        states[n] = (sk, sv, nk, nv)

    _pipelined_units([(project, finish, n) for n in range(N_SUB)])

    @pl.when(i < NPT)
    def _():
        for n in range(N_SUB):
            r = _sub_rows(n)
            sk, sv, nk, nv = states[n]
            sk_st[r, :] = sk
            sv_st[r, :] = sv
            for hd in range(NA_HEADS):
                _store_head_rows(nk_st, n, hd, NA_HEADS, nk[:, HEAD_DIM * hd: HEAD_DIM * (hd + 1)])
                _store_head_rows(nv_st, n, hd, NA_HEADS, nv[:, HEAD_DIM * hd: HEAD_DIM * (hd + 1)])


def _tok_spec(width):
    return pl.BlockSpec((TM, width), lambda i: (i, 0))


_PROMPT_SPEC = pl.BlockSpec((TM, D_MODEL), lambda i: (jnp.minimum(i, NPT - 1), 0))
_SAMPLE_SPEC = pl.BlockSpec((TM, D_MODEL), lambda i: (jnp.maximum(i - NPT, 0), 0))


def _state_spec(width, rows_per_token=1):
    return pl.BlockSpec((TM * rows_per_token, width), lambda i: (jnp.minimum(i, NPT - 1), 0))


def _const_spec(shape):
    return pl.BlockSpec(shape, lambda i: (0,) * len(shape))


_MOD_SPEC = pl.BlockSpec((1, 6, D_MODEL), lambda i: (_mod_row(i), 0, 0))
_ROPE_SPEC = pl.BlockSpec((TM, LANES), lambda i: (_rope_blk(i), 0))


def _premix0(xp, xs, mod_l, g0, w_in_p, q_norm, w_uq_p, kv_norm, w_k_p, w_v, t32, t64):
    outs = [(N_TOK, 1024, BF16), (N_TOK, 1024, BF16), (N_TOK, 512, BF16), (N_TOK, 512, BF16),
            (N_TOK, 512, BF16), (N_TOK, 512, BF16),
            (NP_TOK, 256, F32), (NP_TOK, 128, F32),
            (NP_TOK * DIFF_HEADS, 2 * DIFF_DH, F32), (NP_TOK * DIFF_HEADS, 2 * DIFF_DH, F32)]
    return pl.pallas_call(
        _premix0_kernel,
        grid=(N_TILES,),
        in_specs=[_PROMPT_SPEC, _SAMPLE_SPEC, _MOD_SPEC, _const_spec((1, D_MODEL)),
                  _const_spec((D_MODEL, IN_COLS)), _const_spec((1, MLA_Q_RANK)),
                  _const_spec((MLA_Q_RANK, 1024)), _const_spec((1, MLA_KV_RANK)),
                  _const_spec((MLA_KV_RANK, 1024)), _const_spec((MLA_KV_RANK, 512))]
                 + [_ROPE_SPEC] * 6,
        out_specs=([_tok_spec(w) for (_, w, _) in outs[:6]]
                   + [_state_spec(w, n // NP_TOK) for (n, w, _) in outs[6:]]),
        out_shape=[jax.ShapeDtypeStruct((n, w), dt) for (n, w, dt) in outs],
        compiler_params=_cparams(1),
        name="premix_ab",
    )(xp, xs, mod_l, g0, w_in_p, q_norm, w_uq_p, kv_norm, w_k_p, w_v, *t32, *t64)


def _premix1(x, mod_l, g0, w_in_p, t64):
    outs = [(N_TOK, 512, BF16), (N_TOK, 128, BF16), (N_TOK, 128, BF16), (N_TOK, 512, BF16),
            (N_TOK, 512, BF16), (N_TOK, 512, BF16),
            (NP_TOK, 128, F32), (NP_TOK, 128, F32),
            (NP_TOK * NA_HEADS, HEAD_DIM, F32), (NP_TOK * NA_HEADS, HEAD_DIM, F32)]
    return pl.pallas_call(
        _premix1_kernel,
        grid=(N_TILES,),
        in_specs=[_tok_spec(D_MODEL), _MOD_SPEC, _const_spec((1, D_MODEL)),
                  _const_spec((D_MODEL, IN_COLS))] + [_ROPE_SPEC] * 3,
        out_specs=([_tok_spec(w) for (_, w, _) in outs[:6]]
                   + [_state_spec(w, n // NP_TOK) for (n, w, _) in outs[6:]]),
        out_shape=[jax.ShapeDtypeStruct((n, w), dt) for (n, w, dt) in outs],
        compiler_params=_cparams(1),
        name="premix_cd",
    )(x, mod_l, g0, w_in_p, *t64)


def _mla_cache_kernel(ckv_ref, pe_ref, wk_ref, wv_ref, k_ref, v_ref):
    c = ckv_ref[...].astype(BF16)
    kn = _dot(c, wk_ref[...])
    v_ref[...] = _dot(c, wv_ref[...]).astype(BF16)
    pe = pe_ref[...]
    for hd in range(MLA_HEADS):
        sl = slice(LANES * hd, LANES * (hd + 1))
        k_ref[:, sl] = (kn[:, sl] + pe).astype(BF16)


def _mla_cache(ckv, pe_slab, w_k_p, w_v):
    n = ckv.shape[0]
    tm = 512
    return pl.pallas_call(
        _mla_cache_kernel,
        grid=(n // tm,),
        in_specs=[pl.BlockSpec((tm, MLA_KV_RANK), lambda i: (i, 0)),
                  pl.BlockSpec((tm, LANES), lambda i: (i, 0)),
                  _const_spec((MLA_KV_RANK, 1024)), _const_spec((MLA_KV_RANK, 512))],
        out_specs=[pl.BlockSpec((tm, 1024), lambda i: (i, 0)), pl.BlockSpec((tm, 512), lambda i: (i, 0))],
        out_shape=[jax.ShapeDtypeStruct((n, 1024), BF16), jax.ShapeDtypeStruct((n, 512), BF16)],
        compiler_params=_cparams(1),
        name="mla_cache",
    )(ckv, pe_slab, w_k_p, w_v)


def _softmax_pv(scores, values, sink=None):
    m = jnp.max(scores[0], axis=-1, keepdims=True)
    for s in scores[1:]:
        m = jnp.maximum(m, jnp.max(s, axis=-1, keepdims=True))
    if sink is not None:
        m = jnp.maximum(m, sink)
    l = None
    o = None
    for s, v in zip(scores, values):
        p = jnp.exp2(s - m)
        ls = jnp.sum(p, axis=-1, keepdims=True)
        os_ = _dot(p.astype(BF16), v)
        l = ls if l is None else l + ls
        o = os_ if o is None else o + os_
    if sink is not None:
        l = l + jnp.exp2(sink - m)
    return o * (1.0 / l)


def _lane_lo(shape):
    return lax.broadcasted_iota(jnp.int32, shape, 1) < (LANES // 2)


def _split_halves(qb):
    lo = _lane_lo(qb.shape)
    zero = jnp.zeros_like(qb)
    return jnp.where(lo, qb, zero), jnp.where(lo, zero, qb)


def _attn_ab_kernel(*refs, n_pieces, n_seq, lam_init, aliased):
    if aliased:
        refs = refs[1:]
    q_ref, dq_ref = refs[0], refs[1]
    pieces = [refs[2 + 4 * p: 6 + 4 * p] for p in range(n_pieces)]
    lam_ref, subg_ref, o_ref = refs[2 + 4 * n_pieces:]
    lam = lam_ref[...]
    lam_full = (jnp.exp(jnp.sum(lam[0:1] * lam[1:2], axis=-1, keepdims=True))
                - jnp.exp(jnp.sum(lam[2:3] * lam[3:4], axis=-1, keepdims=True)) + lam_init)
    tq = q_ref.shape[0] // n_seq
    lo = _lane_lo((tq, LANES))
    subg = subg_ref[...] * (1.0 - lam_init)

    def seq_rows(ref, sq):
        n = ref.shape[0] // n_seq
        return slice(n * sq, n * (sq + 1))

    def mla_scores(arg):
        sq, hd = arg
        sl = slice(LANES * hd, LANES * (hd + 1))
        qh = q_ref[seq_rows(q_ref, sq), sl]
        return [_dot_nt(qh, k_ref[seq_rows(k_ref, sq), sl]) for (k_ref, _, _, _) in pieces]

    def diff_scores(arg):
        sq, hd = arg
        sl = slice(LANES * hd, LANES * (hd + 1))
        qq = jnp.concatenate(_split_halves(dq_ref[seq_rows(dq_ref, sq), sl]), axis=0)
        return [_dot_nt(qq, dk_ref[seq_rows(dk_ref, sq), sl]) for (_, _, dk_ref, _) in pieces]

    pair = {}

    def mla_finish(arg, scores):
        sq, hd = arg
        j = hd // 2
        vals = [v_ref[seq_rows(v_ref, sq), LANES * j: LANES * (j + 1)] for (_, v_ref, _, _) in pieces]
        pair[hd % 2] = _softmax_pv(scores, vals)
        if hd % 2 == 1:
            o_ref[seq_rows(o_ref, sq), LANES * j: LANES * (j + 1)] = jnp.where(lo, pair[0], pair[1]).astype(BF16)

    def diff_finish(arg, scores):
        sq, hd = arg
        sl = slice(LANES * hd, LANES * (hd + 1))
        oo = _softmax_pv(scores, [dv_ref[seq_rows(dv_ref, sq), sl] for (_, _, _, dv_ref) in pieces])
        od = _rms(oo[:tq] - lam_full * oo[tq:], subg)
        o_ref[seq_rows(o_ref, sq), 512 + LANES * hd: 512 + LANES * (hd + 1)] = od.astype(BF16)

    units = []
    for sq in range(n_seq):
        for j in range(DIFF_HEADS):
            units += [(diff_scores, diff_finish, (sq, j)), (mla_scores, mla_finish, (sq, 2 * j)),
                      (mla_scores, mla_finish, (sq, 2 * j + 1))]
    _pipelined_units(units)


def _attn_ab(q, dq, new_kv, cache_kv, lam, sub_g, lam_init, *, n_batch, t_len, tq, tok_off, out_init=None,
             n_seq=1):
    assert n_seq == 1 or (tq == t_len and cache_kv is None and n_batch % n_seq == 0)
    n_batch, t_len, tq = n_batch // n_seq, t_len * n_seq, tq * n_seq
    nq = t_len // tq
    q_off = tok_off // tq
    b_off = tok_off // t_len
    widths = (1024, 512, 512, 512)
    in_specs = [pl.BlockSpec((tq, 1024), lambda b, i: (q_off + b * nq + i, 0)),
                pl.BlockSpec((tq, 512), lambda b, i: (q_off + b * nq + i, 0))]
    args = [q, dq]
    for w, a in zip(widths, new_kv):
        in_specs.append(pl.BlockSpec((t_len, w), lambda b, i: (b_off + b, 0)))
        args.append(a)
    n_pieces = 1
    if cache_kv is not None:
        n_pieces = 2
        for w, a in zip(widths, cache_kv):
            in_specs.append(pl.BlockSpec((PAST_LEN, w), lambda b, i: (b, 0)))
            args.append(a)
    in_specs += [pl.BlockSpec((4, DIFF_DH), lambda b, i: (0, 0)),
                 pl.BlockSpec((1, 2 * DIFF_DH), lambda b, i: (0, 0))]
    args += [lam, sub_g]
    aliases = {}
    if out_init is not None:
        in_specs = [pl.BlockSpec(memory_space=pl.ANY)] + in_specs
        args = [out_init] + args
        aliases = {0: 0}
    return pl.pallas_call(
        functools.partial(_attn_ab_kernel, n_pieces=n_pieces, n_seq=n_seq, lam_init=lam_init,
                          aliased=out_init is not None),
        grid=(n_batch, nq),
        in_specs=in_specs,
        out_specs=pl.BlockSpec((tq, 1024), lambda b, i: (q_off + b * nq + i, 0)),
        out_shape=jax.ShapeDtypeStruct((N_TOK, 1024), BF16),
        input_output_aliases=aliases,
        compiler_params=_cparams(2),
        name="attn_ab_%d" % n_pieces,
    )(*args)


def _gqa_stacks(sq_ref, sink_ref):
    tq = sq_ref.shape[0]
    halves = [_split_halves(sq_ref[:, LANES * j: LANES * (j + 1)]) for j in range(4)]
    q_stacks = [jnp.concatenate([halves[j][kvh] for j in range(4)], axis=0) for kvh in range(SWA_KV_HEADS)]
    sinks = [jnp.concatenate([jnp.full((tq, 1), sink_ref[4 * kvh + j] * LOG2E, F32) for j in range(4)], axis=0)
             for kvh in range(SWA_KV_HEADS)]
    return q_stacks, sinks


def _attn_cd_prompt_kernel(sink_ref, sq_ref, sk_ref, sv_ref, nq_ref, nk_ref, nv_ref, o_ref):
    lo = _lane_lo((SEQ, LANES))
    for sq in range(PROMPT_SEQS_PER_STEP):
        r = slice(SEQ * sq, SEQ * (sq + 1))
        sk = sk_ref[r, :]
        sv = sv_ref[r, :]
        for j in range(4):
            sl = slice(LANES * j, LANES * (j + 1))
            q_lo, q_hi = _split_halves(sq_ref[r, sl])
            o_lo = _softmax_pv([_dot_nt(q_lo, sk)], [sv], sink=sink_ref[j] * LOG2E)
            o_hi = _softmax_pv([_dot_nt(q_hi, sk)], [sv], sink=sink_ref[j + 4] * LOG2E)
            o_ref[r, sl] = jnp.where(lo, o_lo, o_hi).astype(BF16)
        for j in range(4):
            sl = slice(LANES * j, LANES * (j + 1))
            q_lo, q_hi = _split_halves(nq_ref[r, sl])
            k = nk_ref[r, sl]
            v = nv_ref[r, sl]
            o_lo = _softmax_pv([_dot_nt(q_lo, k)], [v])
            o_hi = _softmax_pv([_dot_nt(q_hi, k)], [v])
            o_ref[r, 512 + LANES * j: 512 + LANES * (j + 1)] = jnp.where(lo, o_lo, o_hi).astype(BF16)


def _attn_cd_prompt(sink, sq, sk, sv, nq, nk, nv):
    def spec(w):
        return pl.BlockSpec((SEQ * PROMPT_SEQS_PER_STEP, w), lambda b: (b, 0))
    return pl.pallas_call(
        _attn_cd_prompt_kernel,
        grid=(BATCH // PROMPT_SEQS_PER_STEP,),
        in_specs=[pl.BlockSpec(memory_space=pltpu.SMEM), spec(512), spec(128), spec(128),
                  spec(512), spec(512), spec(512)],
        out_specs=spec(1024),
        out_shape=jax.ShapeDtypeStruct((N_TOK, 1024), BF16),
        compiler_params=_cparams(1),
        name="attn_cd_prompt",
    )(sink, sq, sk, sv, nq, nk, nv)


_SWA_KEYS = TQ + 2 * SWA_WINDOW


def _attn_cd_sample_kernel(init_ref, sink_ref, sq_ref, nq_ref, sk_ref, sv_ref, nk_ref, nv_ref,
                           skc_ref, svc_ref, nkc_ref, nvc_ref, bias_ref, o_ref):
    del init_ref
    qi = pl.program_id(1)
    lo = _lane_lo((TQ, LANES))
    ks = pl.multiple_of(jnp.clip(qi * TQ - SWA_WINDOW, 0, DEC_SEQ - _SWA_KEYS), SWA_WINDOW)
    k_win = sk_ref[pl.ds(ks, _SWA_KEYS), :]
    v_win = sv_ref[pl.ds(ks, _SWA_KEYS), :]
    q_pos = qi * TQ + (lax.broadcasted_iota(jnp.int32, (4 * TQ, _SWA_KEYS), 0) & (TQ - 1))
    k_pos = ks + lax.broadcasted_iota(jnp.int32, (4 * TQ, _SWA_KEYS), 1)
    in_win = jnp.abs(q_pos - k_pos) <= SWA_WINDOW
    skc = skc_ref[...]
    svc = svc_ref[...]
    q_stacks, sinks = _gqa_stacks(sq_ref, sink_ref)

    def window_scores(kvh):
        return [_dot_nt(q_stacks[kvh], skc), jnp.where(in_win, _dot_nt(q_stacks[kvh], k_win), NEG)]

    o_kv = {}

    def window_finish(kvh, scores):
        o_kv[kvh] = _softmax_pv(scores, [svc, v_win], sink=sinks[kvh])
        if kvh == SWA_KV_HEADS - 1:
            for j in range(4):
                rows = slice(TQ * j, TQ * (j + 1))
                o_ref[:, LANES * j: LANES * (j + 1)] = jnp.where(lo, o_kv[0][rows], o_kv[1][rows]).astype(BF16)

    n_rows = DEC_SEQ // GRID_W
    r0 = jnp.clip(qi * NA_TILE_ROWS - NA_WIN_ROWS // 2, 0, n_rows - NA_KEY_ROWS)
    kn = pl.multiple_of(r0 * GRID_W, GRID_W)

    lane_lo = _lane_lo((1, LANES))
    pieces = []
    for ri in range(NA_TILE_ROWS):
        r = qi * NA_TILE_ROWS + ri
        rs = jnp.clip(r - NA_WIN_ROWS // 2, 0, n_rows - NA_WIN_ROWS)
        row = []
        for mm in range(NA_KEY_ROWS // 2):
            kr = r0 + 2 * mm
            ok = [(kr + t >= rs) & (kr + t < rs + NA_WIN_ROWS) for t in range(2)]
            mask = jnp.where(lane_lo, jnp.where(ok[0], 0.0, NEG), jnp.where(ok[1], 0.0, NEG))
            row.append((jnp.clip(kr - r + NA_WIN_ROWS, 0, 2 * NA_WIN_ROWS - 1), mask))
        pieces.append(row)

    def add_na_bias(hd, s):
        return jnp.concatenate(
            [jnp.concatenate([s[GRID_W * ri: GRID_W * (ri + 1), LANES * mm: LANES * (mm + 1)] + bias_ref[hd, d] + mask
                              for mm, (d, mask) in enumerate(row)], axis=1)
             for ri, row in enumerate(pieces)], axis=0)

    def na_scores(hd):
        sl = slice(LANES * (hd // 2), LANES * (hd // 2 + 1))
        q_half = _split_halves(nq_ref[:, sl])[hd % 2]
        nk_win = nk_ref[pl.ds(kn, NA_KEY_ROWS * GRID_W), sl]
        return [_dot_nt(q_half, nkc_ref[:, sl]), add_na_bias(hd, _dot_nt(q_half, nk_win))]

    pair = {}

    def na_finish(hd, scores):
        j = hd // 2
        sl = slice(LANES * j, LANES * (j + 1))
        nv_win = nv_ref[pl.ds(kn, NA_KEY_ROWS * GRID_W), sl]
        pair[hd % 2] = _softmax_pv(scores, [nvc_ref[:, sl], nv_win])
        if hd % 2 == 1:
            o_ref[:, 512 + LANES * j: 512 + LANES * (j + 1)] = jnp.where(lo, pair[0], pair[1]).astype(BF16)

    _pipelined_units([(window_scores, window_finish, kvh) for kvh in range(SWA_KV_HEADS)]
                     + [(na_scores, na_finish, hd) for hd in range(NA_HEADS)])


def _attn_cd_sample(out_init, sink, sq, sk, sv, nq, nk, nv, skc, svc, nkc, nvc, bias):
    nq_t = DEC_SEQ // TQ
    q_off = NP_TOK // TQ
    b_off = NP_TOK // DEC_SEQ

    def qspec(w):
        return pl.BlockSpec((TQ, w), lambda b, i: (q_off + b * nq_t + i, 0))

    def kspec(w):
        return pl.BlockSpec((DEC_SEQ, w), lambda b, i: (b_off + b, 0))

    def cspec(w):
        return pl.BlockSpec((PAST_LEN, w), lambda b, i: (b, 0))

    bias_spec = pl.BlockSpec((NA_HEADS, 2 * NA_WIN_ROWS, GRID_W, LANES), lambda b, i: (0, 0, 0, 0))
    return pl.pallas_call(
        _attn_cd_sample_kernel,
        grid=(DEC_BATCH, nq_t),
        in_specs=[pl.BlockSpec(memory_space=pl.ANY), pl.BlockSpec(memory_space=pltpu.SMEM),
                  qspec(512), qspec(512), kspec(128), kspec(128), kspec(512), kspec(512),
                  cspec(128), cspec(128), cspec(512), cspec(512), bias_spec],
        out_specs=pl.BlockSpec((TQ, 1024), lambda b, i: (q_off + b * nq_t + i, 0)),
        out_shape=jax.ShapeDtypeStruct((N_TOK, 1024), BF16),
        input_output_aliases={0: 0},
        compiler_params=_cparams(2),
        name="attn_cd_sample",
    )(out_init, sink, sq, nq, sk, sv, nk, nv, skc, svc, nkc, nvc, bias)


def _na_bias(rpb):
    n_dc = 2 * NA_WIN_COLS - 1
    c = np.arange(GRID_W)[:, None]
    kc = np.arange(GRID_W)[None, :]
    qs = np.clip(c - NA_WIN_COLS // 2, 0, GRID_W - NA_WIN_COLS)
    col_ok = (kc >= qs) & (kc < qs + NA_WIN_COLS)
    dc = np.clip(kc - c + NA_WIN_COLS - 1, 0, n_dc - 1)
    onehot = ((dc[None] == np.arange(n_dc)[:, None, None]) & col_ok[None]).astype(np.float32)
    blocks = jnp.einsum('hrd,dck->hrck', rpb.astype(F32) * LOG2E, onehot, precision=lax.Precision.HIGHEST)
    blocks = jnp.where(col_ok[None, None], blocks, NEG)
    none = jnp.full((NA_HEADS, 1, GRID_W, GRID_W), NEG, F32)
    return jnp.concatenate([jnp.concatenate([none, blocks], axis=1),
                            jnp.concatenate([blocks, none], axis=1)], axis=-1)


_HI_MASK = -65536


def _pack_pairs(x):
    w = x.shape[1] // 2
    r = x.astype(BF16).astype(F32)
    lo = lax.bitcast_convert_type(r[:, :w], jnp.int32)
    hi = lax.bitcast_convert_type(r[:, w:], jnp.int32)
    return (hi & _HI_MASK) | lax.shift_right_logical(lo, 16)


def _unpack_pairs(p):
    lo = lax.bitcast_convert_type(lax.shift_left(p, 16), F32)
    hi = lax.bitcast_convert_type(p & _HI_MASK, F32)
    return lo, hi


def _postmix_kernel(*refs, split_x):
    if split_x:
        o_ref, xp_ref, xs_ref = refs[:3]
        refs = refs[3:]
    else:
        o_ref, x_ref = refs[:2]
        refs = refs[2:]
    (mod_ref, g_ref, wout_ref, wr_ref, br_ref, tri_ref,
     x1_ref, h2_ref, route_ref, gate_ref, cnt_ref, run_ref) = refs
    i = pl.program_id(0)

    @pl.when(i == 0)
    def _():
        run_ref[...] = jnp.zeros_like(run_ref)

    m = mod_ref[0]
    g = g_ref[...]
    gate_gain = g[1:2] * m[2:3]
    ffn_gain = g[2:3] * (1.0 + m[4:5])
    sub_logits = {}

    def project(n):
        return _dot(o_ref[_sub_rows(n), :], wout_ref[...])

    def finish(n, y):
        r = _sub_rows(n)
        x = jnp.where(i < NPT, xp_ref[r, :], xs_ref[r, :]) if split_x else x_ref[r, :]
        x1 = x + _rms(y, gate_gain)
        x1_ref[r, :] = x1
        h2 = _rms(x1, ffn_gain) + m[3:4]
        h2_ref[r, :] = _pack_pairs(h2)
        sub_logits[n] = _dot(h2.astype(BF16), wr_ref[...]) + br_ref[...]

    _pipelined_units([(project, finish, n) for n in range(N_SUB)])

    logits = jnp.concatenate([sub_logits[n] for n in range(N_SUB)], axis=0)
    lane = lax.broadcasted_iota(jnp.int32, logits.shape, 1).astype(F32)
    cur = jnp.where(lane < N_EXPERTS, logits, -jnp.inf)
    tops, idxs = [], []
    for _ in range(TOP_K):
        mx = jnp.max(cur, axis=-1, keepdims=True)
        ix = jnp.min(jnp.where(cur == mx, lane, float(LANES)), axis=-1, keepdims=True)
        tops.append(mx)
        idxs.append(ix)
        cur = jnp.where(lane == ix, -jnp.inf, cur)
    es = [jnp.exp(t - tops[0]) for t in tops]
    inv = 1.0 / (es[0] + es[1] + es[2] + es[3])
    picked = jnp.zeros_like(logits)
    for k in range(TOP_K):
        picked = jnp.where(lane == idxs[k], 1.0, picked)
    before = _dot(tri_ref[...], picked.astype(BF16)) + run_ref[0:1, :]
    route = jnp.zeros_like(logits)
    gate_out = jnp.zeros_like(logits)
    for k in range(TOP_K):
        rank = jnp.sum(jnp.where(lane == idxs[k], before, 0.0), axis=-1, keepdims=True)
        route = jnp.where(lane == float(k), idxs[k], route)
        route = jnp.where(lane == float(TOP_K + k), rank, route)
        gate_out = jnp.where(lane == float(k), es[k] * inv, gate_out)
    route_ref[...] = route.T[:2 * TOP_K].astype(jnp.int32)
    gate_ref[...] = gate_out
    run_ref[...] = run_ref[...] + jnp.sum(picked, axis=0, keepdims=True)
    cnt_ref[...] = run_ref[...].astype(jnp.int32)


def _postmix(o_cat, xs, mod_l, g_l, w_out, w_r, b_r):
    tri = jnp.asarray(np.tril(np.ones((TM, TM), np.float32), -1), BF16)
    split_x = len(xs) == 2
    x_specs = [_PROMPT_SPEC, _SAMPLE_SPEC] if split_x else [_tok_spec(D_MODEL)]
    return pl.pallas_call(
        functools.partial(_postmix_kernel, split_x=split_x),
        grid=(N_TILES,),
        in_specs=[_tok_spec(1024)] + x_specs + [_MOD_SPEC, _const_spec((4, D_MODEL)),
                  _const_spec((1024, D_MODEL)), _const_spec((D_MODEL, LANES)), _const_spec((1, LANES)),
                  _const_spec((TM, TM))],
        out_specs=[_tok_spec(D_MODEL), _tok_spec(D_MODEL // 2), pl.BlockSpec((2 * TOP_K, TM), lambda i: (0, i)),
                   _tok_spec(LANES), _const_spec((8, LANES))],
        out_shape=[jax.ShapeDtypeStruct((N_TOK, D_MODEL), F32),
                   jax.ShapeDtypeStruct((N_TOK, D_MODEL // 2), jnp.int32),
                   jax.ShapeDtypeStruct((2 * TOP_K, N_TOK), jnp.int32), jax.ShapeDtypeStruct((N_TOK, LANES), F32),
                   jax.ShapeDtypeStruct((8, LANES), jnp.int32)],
        scratch_shapes=[pltpu.VMEM((8, LANES), F32)],
        compiler_params=_cparams(1),
        name="postmix",
    )(o_cat, *xs, mod_l, g_l, w_out, w_r, b_r, tri)


SC_WORKERS = 32
SC_ROWS = 64
ROW_WORDS = D_MODEL // 2

_SC_SCRATCH = [pltpu.VMEM((SC_ROWS, ROW_WORDS), jnp.int32), pltpu.VMEM((SC_ROWS, ROW_WORDS), jnp.int32),
               pltpu.SemaphoreType.DMA, pltpu.SemaphoreType.DMA, pltpu.SemaphoreType.DMA, pltpu.SemaphoreType.DMA]


def _sc_worker_id():
    return lax.axis_index("s") * 2 + lax.axis_index("c")


def _sc_double_buffered(n_chunks, load, store):
    for cp in load(0, 0):
        cp.start()

    @pl.loop(0, n_chunks, step=2)
    def _(g0):
        for b in range(2):
            g = g0 + b
            for cp in load(g, b):
                cp.wait()

            @pl.when(g >= 1)
            def _():
                for cp in store(g - 1, 1 - b):
                    cp.wait()

            @pl.when(g + 1 < n_chunks)
            def _():
                for cp in load(g + 1, 1 - b):
                    cp.start()

            for cp in store(g, b):
                cp.start()

    for cp in store(n_chunks - 1, (n_chunks - 1) % 2):
        cp.wait()


def _sc_dispatch(src, idx):
    n_chunks = N_TOK // (SC_WORKERS * SC_ROWS)
    assert n_chunks % 2 == 0
    mesh = plsc.VectorSubcoreMesh(core_axis_name="c", subcore_axis_name="s")

    @functools.partial(
        pl.kernel, mesh=mesh,
        out_type=jax.ShapeDtypeStruct((MOE_ROWS, ROW_WORDS), jnp.int32),
        scratch_types=[pltpu.VMEM((n_chunks * TOP_K, SC_ROWS), jnp.int32)] + _SC_SCRATCH)
    def k(src_hbm, idx_hbm, out_hbm, idx_v, buf0, buf1, in0, in1, out0, out1):
        wid = _sc_worker_id()
        pltpu.sync_copy(idx_hbm.at[wid], idx_v)
        bufs, in_sems, out_sems = (buf0, buf1), (in0, in1), (out0, out1)

        def load(g, b):
            rows = pl.ds((wid * n_chunks + g) * SC_ROWS, SC_ROWS)
            return [pltpu.make_async_copy(src_hbm.at[rows], bufs[b], in_sems[b])]

        def store(g, b):
            return [pltpu.make_async_copy(bufs[b], out_hbm.at[idx_v.at[g * TOP_K + kk]], out_sems[b])
                    for kk in range(TOP_K)]

        _sc_double_buffered(n_chunks, load, store)

    return k(src, idx)


def _sc_collect(table, idx):
    n_chunks = idx.shape[1]
    assert n_chunks % 2 == 0
    mesh = plsc.VectorSubcoreMesh(core_axis_name="c", subcore_axis_name="s")

    @functools.partial(
        pl.kernel, mesh=mesh,
        out_type=jax.ShapeDtypeStruct((SC_WORKERS * n_chunks * SC_ROWS, ROW_WORDS), jnp.int32),
        scratch_types=[pltpu.VMEM((n_chunks, SC_ROWS), jnp.int32)] + _SC_SCRATCH)
    def k(table_hbm, idx_hbm, out_hbm, idx_v, buf0, buf1, in0, in1, out0, out1):
        wid = _sc_worker_id()
        pltpu.sync_copy(idx_hbm.at[wid], idx_v)
        bufs, in_sems, out_sems = (buf0, buf1), (in0, in1), (out0, out1)

        def load(g, b):
            return [pltpu.make_async_copy(table_hbm.at[idx_v.at[g]], bufs[b], in_sems[b])]

        def store(g, b):
            rows = pl.ds((wid * n_chunks + g) * SC_ROWS, SC_ROWS)
            return [pltpu.make_async_copy(bufs[b], out_hbm.at[rows], out_sems[b])]

        _sc_double_buffered(n_chunks, load, store)

    return k(table, idx)


def _expert_rows(words, n_valid, wgu_b, wd_b, bgu, bd):
    live = lax.broadcasted_iota(jnp.int32, words.shape, 0) < n_valid
    lo, hi = _unpack_pairs(jnp.where(live, words, 0))
    x = jnp.concatenate([lo, hi], axis=1).astype(BF16)
    gu = _dot(x, wgu_b[...]) + bgu
    g = jnp.minimum(gu[:, :D_EXPERT], SWIGLU_LIMIT)
    u = jnp.clip(gu[:, D_EXPERT:], -SWIGLU_LIMIT, SWIGLU_LIMIT)
    a = g * jax.nn.sigmoid(SWIGLU_ALPHA * g) * (u + 1.0)
    return _pack_pairs(_dot(a.astype(BF16), wd_b[...]) + bd)


def _moe_kernel(blk_e_ref, blk_first_ref, blk_rows_ref, blk_slot_ref, blk_next_ref, blk_io_ref,
                x_ref, wgu_hbm, bgu_ref, wd_hbm, bd_ref, y_ref,
                wgu_f, wd_f, wgu_b, wd_b, sem, *, layer):
    del blk_io_ref
    i = pl.program_id(0)
    n_valid = blk_rows_ref[i]
    quantum = MOE_TM // MOE_TAIL_PARTS

    def weight_copies(e, slot):
        return (pltpu.make_async_copy(wgu_hbm.at[layer, e], wgu_f.at[slot], sem.at[0, slot]),
                pltpu.make_async_copy(wd_hbm.at[layer, e], wd_f.at[slot], sem.at[1, slot]))

    @pl.when(i == 0)
    def _():
        for n, cp in enumerate(weight_copies(blk_e_ref[0], blk_slot_ref[0])):
            cp.start(priority=n)

    @pl.when(blk_first_ref[i] == 1)
    def _():
        slot = blk_slot_ref[i]
        for cp in weight_copies(blk_e_ref[i], slot):
            cp.wait()
        nxt = blk_next_ref[i]

        @pl.when(nxt >= 0)
        def _():
            for cp in weight_copies(nxt, 1 - slot):
                cp.start(priority=1)

        wgu_b[...] = wgu_f[slot].astype(BF16)
        wd_b[...] = wd_f[slot].astype(BF16)

    for parts in range(1, MOE_TAIL_PARTS + 1):
        rows = parts * quantum

        @pl.when((n_valid > rows - quantum) & (n_valid <= rows))
        def _(rows=rows):
            y_ref[:rows] = _expert_rows(x_ref[:rows], n_valid, wgu_b, wd_b, bgu_ref[0, 0], bd_ref[0, 0])
            if rows < MOE_TM:
                y_ref[rows:] = jnp.zeros((MOE_TM - rows, ROW_WORDS), jnp.int32)


def _moe(layer, blk_meta, xs, w_gu, b_gu, w_down, b_down):
    def row_map(i, e, first, rows, slot, nxt, io):
        return (io[i], 0)

    def bias_map(i, e, *_):
        return (layer, e[i], 0, 0)

    grid_spec = pltpu.PrefetchScalarGridSpec(
        num_scalar_prefetch=6,
        grid=(MOE_BLOCKS,),
        in_specs=[
            pl.BlockSpec((MOE_TM, ROW_WORDS), row_map),
            pl.BlockSpec(memory_space=pl.ANY),
            pl.BlockSpec((1, 1, 1, 2 * D_EXPERT), bias_map),
            pl.BlockSpec(memory_space=pl.ANY),
            pl.BlockSpec((1, 1, 1, D_MODEL), bias_map),
        ],
        out_specs=pl.BlockSpec((MOE_TM, ROW_WORDS), row_map),
        scratch_shapes=[pltpu.VMEM((2, D_MODEL, 2 * D_EXPERT), F32), pltpu.VMEM((2, D_EXPERT, D_MODEL), F32),
                        pltpu.VMEM((D_MODEL, 2 * D_EXPERT), BF16), pltpu.VMEM((D_EXPERT, D_MODEL), BF16),
                        pltpu.SemaphoreType.DMA((2, 2))],
    )
    return pl.pallas_call(
        functools.partial(_moe_kernel, layer=layer),
        grid_spec=grid_spec,
        out_shape=jax.ShapeDtypeStruct((MOE_ROWS, ROW_WORDS), jnp.int32),
        compiler_params=_cparams(1),
        name="moe_experts",
    )(*blk_meta, xs, w_gu, b_gu.reshape(DEPTH, N_EXPERTS, 1, 2 * D_EXPERT),
      w_down, b_down.reshape(DEPTH, N_EXPERTS, 1, D_MODEL))


def _route(route, counts):
    experts = jnp.arange(N_EXPERTS, dtype=jnp.int32)
    padded = (counts + MOE_TM - 1) // MOE_TM * MOE_TM
    pad_end = jnp.cumsum(padded)
    pad_start = pad_end - padded
    e = route[0:TOP_K]
    onehot = e[:, :, None] == experts[None, None, :]
    dest = jnp.sum(jnp.where(onehot, pad_start[None, None, :], 0), axis=-1) + route[TOP_K:2 * TOP_K]
    blk_row0 = jnp.arange(MOE_BLOCKS, dtype=jnp.int32) * MOE_TM
    blk_e = jnp.minimum(jnp.sum((pad_end[None, :] <= blk_row0[:, None]).astype(jnp.int32), axis=1),
                        N_EXPERTS - 1)
    on = blk_row0 < pad_end[-1]
    n_on = jnp.sum(on.astype(jnp.int32))
    blk_onehot = blk_e[:, None] == experts[None, :]
    row_end = jnp.sum(jnp.where(blk_onehot, (pad_start + counts)[None, :], 0), axis=1)
    blk_rows = jnp.where(on, jnp.clip(row_end - blk_row0, 0, MOE_TM), 0).astype(jnp.int32)
    last_e = jnp.sum(jnp.where(jnp.arange(MOE_BLOCKS) == n_on - 1, blk_e, 0))
    blk_e = jnp.where(on, blk_e, last_e).astype(jnp.int32)
    prev = jnp.concatenate([jnp.full((1,), -1, jnp.int32), blk_e[:-1]])
    blk_first = (blk_e != prev).astype(jnp.int32)
    blk_slot = ((jnp.cumsum(blk_first) - 1) % 2).astype(jnp.int32)
    later_used = (experts[None, :] > experts[:, None]) & (counts[None, :] > 0)
    next_used = jnp.min(jnp.where(later_used, experts[None, :], N_EXPERTS), axis=1)
    next_used = jnp.where(next_used == N_EXPERTS, -1, next_used)
    blk_next = jnp.sum(jnp.where(blk_e[:, None] == experts[None, :], next_used[None, :], 0), axis=1)
    blk_io = jnp.minimum(jnp.arange(MOE_BLOCKS, dtype=jnp.int32), n_on - 1)
    return dest, (blk_e, blk_first, blk_rows, blk_slot, blk_next.astype(jnp.int32), blk_io)


def _combine_kernel(y_ref, gate_ref, x1_ref, mod_ref, g_ref, *o_refs):
    npt = NP_TOK // TM_WIDE
    i = pl.program_id(0)
    m = mod_ref[0]
    gate = gate_ref[...]
    acc_lo = None
    for k in range(TOP_K):
        lo, hi = _unpack_pairs(y_ref[k])
        gk = gate[:, k:k + 1]
        acc_lo = gk * lo if acc_lo is None else acc_lo + gk * lo
        acc_hi = gk * hi if k == 0 else acc_hi + gk * hi
    acc = jnp.concatenate([acc_lo, acc_hi], axis=1)
    out = x1_ref[...] + _rms(acc, g_ref[...][3:4] * m[5:6])
    if len(o_refs) == 1:
        o_refs[0][...] = out
    else:
        @pl.when(i < npt)
        def _():
            o_refs[0][...] = out

        @pl.when(i >= npt)
        def _():
            o_refs[1][...] = out


def _combine(yg, gates, x1, mod_l, g_l, split_out):
    tm = TM_WIDE
    npt = NP_TOK // tm

    def tok(w):
        return pl.BlockSpec((tm, w), lambda i: (i, 0))

    if split_out:
        out_specs = [pl.BlockSpec((tm, D_MODEL), lambda i: (jnp.minimum(i, npt - 1), 0)),
                     pl.BlockSpec((tm, D_MODEL), lambda i: (jnp.maximum(i - npt, 0), 0))]
        out_shape = [jax.ShapeDtypeStruct((NP_TOK, D_MODEL), F32), jax.ShapeDtypeStruct((NS_TOK, D_MODEL), F32)]
    else:
        out_specs = [tok(D_MODEL)]
        out_shape = [jax.ShapeDtypeStruct((N_TOK, D_MODEL), F32)]
    mod_spec = pl.BlockSpec((1, 6, D_MODEL), lambda i: (jnp.where(i < npt, 0, 1 + (i - npt) // (DEC_SEQ // tm)), 0, 0))
    return pl.pallas_call(
        _combine_kernel,
        grid=(N_TOK // tm,),
        in_specs=[pl.BlockSpec((TOP_K, tm, ROW_WORDS), lambda i: (0, i, 0)), tok(LANES),
                  tok(D_MODEL), mod_spec, _const_spec((4, D_MODEL))],
        out_specs=out_specs,
        out_shape=out_shape,
        compiler_params=_cparams(1),
        name="combine",
    )(yg, gates, x1, mod_l, g_l)


def _ffn(layer, o_cat, xs, mod_l, g_l, w_out, w_router, b_router, w_gu, b_gu, w_down, b_down, split_out):
    w_r = jnp.pad(w_router, ((0, 0), (0, LANES - N_EXPERTS))).astype(BF16)
    b_r = jnp.pad(b_router, (0, LANES - N_EXPERTS)).reshape(1, LANES)
    x1, h2p, route, gate_slab, counts = _postmix(o_cat, xs, mod_l, g_l, w_out.astype(BF16), w_r, b_r)
    dest, blk_meta = _route(route, counts[0, :N_EXPERTS])
    n_chunks = N_TOK // (SC_WORKERS * SC_ROWS)
    idx_d = dest.reshape(TOP_K, SC_WORKERS, n_chunks, SC_ROWS).transpose(1, 2, 0, 3).reshape(
        SC_WORKERS, n_chunks * TOP_K, SC_ROWS)
    rows = _sc_dispatch(h2p, idx_d)
    ys = _moe(layer, blk_meta, rows, w_gu, b_gu, w_down, b_down)
    idx_c = dest.reshape(SC_WORKERS, TOP_K * n_chunks, SC_ROWS)
    yg = _sc_collect(ys, idx_c).reshape(TOP_K, N_TOK, ROW_WORDS)
    return _combine(yg, gate_slab, x1, mod_l, g_l, split_out)


def _pad_heads(w, n_heads, width, keep):
    k = w.shape[0]
    w = w.reshape(k, n_heads, width)[:, :, :keep]
    return jnp.pad(w, ((0, 0), (0, 0), (0, LANES - keep))).reshape(k, n_heads * LANES)


def _pe_slab(x):
    return jnp.pad(x, [(0, 0)] * (x.ndim - 1) + [(MLA_NOPE, LANES - MLA_NOPE - MLA_ROPE)])


def _pair_kv_heads(w):
    g = SWA_HEADS // SWA_KV_HEADS
    return w.reshape(SWA_KV_HEADS, g, HEAD_DIM, -1).transpose(1, 0, 2, 3).reshape(w.shape)


def kernel(x_prompt, x_sample, cache_mla_ckv, cache_mla_krope, cache_diff_k, cache_diff_v, cache_swa_k, cache_swa_v, cache_na_k, cache_na_v, c, c_ctx, w_mod, b_mod, norm_g, w_in0, mla_q_norm, w_uq, mla_kv_norm, w_ukv, diff_lambda, diff_norm, w_out0, w_in1, swa_sink, na_rpb, w_out1, w_router, b_router, w_gu, b_gu, w_down, b_down):
    xs = (x_prompt.reshape(NP_TOK, D_MODEL), x_sample.reshape(NS_TOK, D_MODEL))
    cond = jnp.concatenate([c_ctx[None, :], c, jnp.zeros((16 - 1 - DEC_BATCH, D_MODEL), F32)], axis=0)
    mod = _modulation(cond, w_mod, b_mod).reshape(DEPTH, 16, 6, D_MODEL)
    t64, t32 = _rope_tables()
    states = {}
    for l in range(DEPTH):
        i = l // 2
        g_l = norm_g[l]
        mod_l = mod[l]
        if l % 2 == 0:
            lam_init = 0.8 - 0.6 * math.exp(-0.3 * l)
            wi = w_in0[i]
            w_in_p = jnp.concatenate(
                [wi[:, 0:640], wi[:, 672:2208], _pe_slab(wi[:, 640:672])], axis=1).astype(BF16)
            w_uq_p = _pad_heads(w_uq[i], MLA_HEADS, MLA_NOPE + MLA_ROPE, MLA_NOPE + MLA_ROPE).astype(BF16)
            w_k_p = _pad_heads(w_ukv[i], MLA_HEADS, MLA_NOPE + MLA_V, MLA_NOPE).astype(BF16)
            w_v = w_ukv[i].reshape(MLA_KV_RANK, MLA_HEADS, MLA_NOPE + MLA_V)[:, :, MLA_NOPE:].reshape(
                MLA_KV_RANK, MLA_HEADS * MLA_V).astype(BF16)
            (q, k, v, dq, dk, dv, ckv_st, kpe_st, dk_st, dv_st) = _premix0(
                *xs, mod_l, g_l[0:1], w_in_p, mla_q_norm[i][None, :], w_uq_p, mla_kv_norm[i][None, :],
                w_k_p, w_v, t32, t64)
            states['mla_ckv'] = ckv_st.reshape(BATCH, 1, SEQ, MLA_KV_RANK)
            states['mla_krope'] = kpe_st[:, MLA_NOPE:MLA_NOPE + MLA_ROPE].reshape(BATCH, 1, SEQ, MLA_ROPE)
            states['diff_k'] = dk_st.reshape(BATCH, 1, SEQ, DIFF_HEADS, 2 * DIFF_DH)
            states['diff_v'] = dv_st.reshape(BATCH, 1, SEQ, DIFF_HEADS, 2 * DIFF_DH)
            kc, vc = _mla_cache(cache_mla_ckv[:, i].reshape(DEC_BATCH * PAST_LEN, MLA_KV_RANK),
                                _pe_slab(cache_mla_krope[:, i].reshape(DEC_BATCH * PAST_LEN, MLA_ROPE)),
                                w_k_p, w_v)
            dkc = cache_diff_k[:, i].reshape(DEC_BATCH * PAST_LEN, 512).astype(BF16)
            dvc = cache_diff_v[:, i].reshape(DEC_BATCH * PAST_LEN, 512).astype(BF16)
            lam = diff_lambda[i]
            sub_g = diff_norm[i][None, :]
            o_p = _attn_ab(q, dq, (k, v, dk, dv), None, lam, sub_g, lam_init,
                           n_batch=BATCH, t_len=SEQ, tq=SEQ, tok_off=0, n_seq=PROMPT_SEQS_PER_STEP)
            o_cat = _attn_ab(q, dq, (k, v, dk, dv), (kc, vc, dkc, dvc), lam, sub_g, lam_init,
                             n_batch=DEC_BATCH, t_len=DEC_SEQ, tq=TQ_AB, tok_off=NP_TOK, out_init=o_p)
            w_out = w_out0[i]
        else:
            wi = w_in1[i]
            n_sq = SWA_HEADS * HEAD_DIM
            w_in_p = jnp.concatenate([_pair_kv_heads(wi[:, :n_sq].T).T, wi[:, n_sq:]], axis=1).astype(BF16)
            (sq, sk, sv, nq, nk, nv, sk_st, sv_st, nk_st, nv_st) = _premix1(xs[0], mod_l, g_l[0:1], w_in_p, t64)
            states['swa_k'] = sk_st.reshape(BATCH, 1, SEQ, SWA_KV_HEADS, HEAD_DIM)
            states['swa_v'] = sv_st.reshape(BATCH, 1, SEQ, SWA_KV_HEADS, HEAD_DIM)
            states['na_k'] = nk_st.reshape(BATCH, 1, SEQ, NA_HEADS, HEAD_DIM)
            states['na_v'] = nv_st.reshape(BATCH, 1, SEQ, NA_HEADS, HEAD_DIM)
            skc = cache_swa_k[:, i].reshape(DEC_BATCH * PAST_LEN, 128).astype(BF16)
            svc = cache_swa_v[:, i].reshape(DEC_BATCH * PAST_LEN, 128).astype(BF16)
            nkc = cache_na_k[:, i].reshape(DEC_BATCH * PAST_LEN, 512).astype(BF16)
            nvc = cache_na_v[:, i].reshape(DEC_BATCH * PAST_LEN, 512).astype(BF16)
            sink = swa_sink[i]
            o_p = _attn_cd_prompt(sink, sq, sk, sv, nq, nk, nv)
            o_cat = _attn_cd_sample(o_p, sink, sq, sk, sv, nq, nk, nv, skc, svc, nkc, nvc, _na_bias(na_rpb[i]))
            wo = w_out1[i]
            w_out = jnp.concatenate([_pair_kv_heads(wo[:n_sq]), wo[n_sq:]], axis=0)
        xs = _ffn(l, o_cat, xs, mod_l, g_l, w_out, w_router[l], b_router[l], w_gu, b_gu, w_down, b_down,
                  split_out=(l == DEPTH - 1))
    return (xs[0].reshape(BATCH, SEQ, D_MODEL), xs[1].reshape(DEC_BATCH, DEC_SEQ, D_MODEL),
            states['mla_ckv'], states['mla_krope'], states['diff_k'], states['diff_v'],
            states['swa_k'], states['swa_v'], states['na_k'], states['na_v'])
```

```python
import functools
import math

import numpy as np
import jax
import jax.numpy as jnp
from jax import lax
from jax.experimental import pallas as pl
from jax.experimental.pallas import tpu as pltpu
from jax.experimental.pallas import tpu_sc as plsc

F32 = jnp.float32
BF16 = jnp.bfloat16

D_MODEL = 1024
BATCH = 16
SEQ = 256
DEPTH = 2
DEC_BATCH = 8
DEC_SEQ = 2048
PAST_LEN = 256
GRID_W = 64
HEAD_DIM = 64
ROPE_THETA = 10000.0
EPS = 1e-6
NEG = -1e30

MLA_HEADS = 8
MLA_Q_RANK = 384
MLA_KV_RANK = 256
MLA_NOPE = 64
MLA_ROPE = 32
MLA_V = 64
DIFF_HEADS = 4
DIFF_DH = 64
SWA_HEADS = 8
SWA_KV_HEADS = 2
SWA_WINDOW = 128
NA_HEADS = 8
NA_WIN_ROWS = 8
NA_WIN_COLS = 16
N_EXPERTS = 32
TOP_K = 4
D_EXPERT = 1024
SWIGLU_LIMIT = 7.0
SWIGLU_ALPHA = 1.702

LANES = 128
NP_TOK = BATCH * SEQ
NS_TOK = DEC_BATCH * DEC_SEQ
N_TOK = NP_TOK + NS_TOK
TM = 512
NPT = NP_TOK // TM
TILES_PER_SAMPLE = DEC_SEQ // TM
N_TILES = N_TOK // TM
TM_WIDE = 1024
N_SUB = 2
SUB_TM = TM // N_SUB
TQ = 256
TQ_AB = 256
PROMPT_SEQS_PER_STEP = 2
MOE_TM = 1024
MOE_TAIL_PARTS = 8
MOE_ROWS = ((N_TOK * TOP_K + N_EXPERTS * (MOE_TM - 1)) // MOE_TM + 1) * MOE_TM
MOE_BLOCKS = MOE_ROWS // MOE_TM
NA_TILE_ROWS = TQ // GRID_W
NA_KEY_ROWS = 12
VMEM_LIMIT = 56 * 1024 * 1024


def _cparams(n_axes, vmem=VMEM_LIMIT):
    return pltpu.CompilerParams(dimension_semantics=("arbitrary",) * n_axes,
                                vmem_limit_bytes=vmem)


def _rms(x, g):
    return x * lax.rsqrt(jnp.mean(x * x, axis=-1, keepdims=True) + EPS) * g


def _dot(a, b):
    return jnp.dot(a, b, preferred_element_type=F32)


def _dot_nt(a, b):
    return lax.dot_general(a, b, (((1,), (1,)), ((), ())), preferred_element_type=F32)


def _rope(x, cos, sin_a, sin_b, half):
    return (x * cos + pltpu.roll(x, LANES - half, 1) * sin_a + pltpu.roll(x, half, 1) * sin_b)


def _pipelined_units(units):
    pending = units[0][0](units[0][2])
    for n, (_, finish, arg) in enumerate(units):
        following = units[n + 1][0](units[n + 1][2]) if n + 1 < len(units) else None
        finish(arg, pending)
        pending = following


def _sub_rows(n):
    return slice(SUB_TM * n, SUB_TM * (n + 1))


def _store_head_rows(st_ref, n, hd, n_heads, x):
    st_ref[pl.ds(SUB_TM * n * n_heads + hd, SUB_TM, stride=n_heads), :] = x


def _mod_row(i):
    return jnp.where(i < NPT, 0, 1 + (i - NPT) // TILES_PER_SAMPLE)


def _rope_blk(i):
    return jnp.where(i < NPT, TILES_PER_SAMPLE, (i - NPT) % TILES_PER_SAMPLE)


def _mod_kernel(c_ref, w_ref, b_ref, o_ref):
    c = c_ref[...]
    s = (c * jax.nn.sigmoid(c)).astype(BF16)
    o_ref[0] = _dot(s, w_ref[0].astype(BF16)) + b_ref[0]


def _modulation(cond, w_mod, b_mod):
    nb = 1024
    return pl.pallas_call(
        _mod_kernel,
        grid=(DEPTH, 6 * D_MODEL // nb),
        in_specs=[
            pl.BlockSpec((16, D_MODEL), lambda l, n: (0, 0)),
            pl.BlockSpec((1, D_MODEL, nb), lambda l, n: (l, 0, n)),
            pl.BlockSpec((1, 1, nb), lambda l, n: (l, 0, n)),
        ],
        out_specs=pl.BlockSpec((1, 16, nb), lambda l, n: (l, 0, n)),
        out_shape=jax.ShapeDtypeStruct((DEPTH, 16, 6 * D_MODEL), F32),
        compiler_params=_cparams(2),
        name="modulation",
    )(cond, w_mod, b_mod.reshape(DEPTH, 1, 6 * D_MODEL))


def _rope_tables():
    t = jnp.arange(DEC_SEQ)
    rows = (t // GRID_W).astype(F32)
    cols = (t % GRID_W).astype(F32)

    def angles(r):
        n = r // 4
        inv = ROPE_THETA ** (-jnp.arange(n, dtype=F32) / n)
        return jnp.concatenate([rows[:, None] * inv[None], cols[:, None] * inv[None]], axis=-1)

    def finish(cos, sa, sb):
        ident = (jnp.ones((TM, LANES), F32), jnp.zeros((TM, LANES), F32), jnp.zeros((TM, LANES), F32))
        return tuple(jnp.concatenate([a, b], axis=0) for a, b in zip((cos, sa, sb), ident))

    a64 = angles(64)
    c, s, z = jnp.cos(a64), jnp.sin(a64), jnp.zeros_like(a64)
    t64 = finish(jnp.concatenate([c, c, c, c], -1), jnp.concatenate([-s, z, -s, z], -1),
                 jnp.concatenate([z, s, z, s], -1))
    a32 = angles(32)
    c, s, z = jnp.cos(a32), jnp.sin(a32), jnp.zeros_like(a32)
    one64 = jnp.ones((DEC_SEQ, 64), F32)
    z64 = jnp.zeros((DEC_SEQ, 64), F32)
    z32 = jnp.zeros((DEC_SEQ, 32), F32)
    t32 = finish(jnp.concatenate([one64, c, c, z32], -1), jnp.concatenate([z64, -s, z, z32], -1),
                 jnp.concatenate([z64, z, s, z32], -1))
    return t64, t32


LOG2E = math.log2(math.e)
_DIFF_COLS = DIFF_HEADS * 2 * DIFF_DH
_AB_COLS = tuple(np.cumsum([0, MLA_Q_RANK, MLA_KV_RANK, _DIFF_COLS, _DIFF_COLS, _DIFF_COLS, LANES]).tolist())
_CD_COLS = tuple(np.cumsum([0, SWA_HEADS * HEAD_DIM, SWA_KV_HEADS * HEAD_DIM, SWA_KV_HEADS * HEAD_DIM,
                            NA_HEADS * HEAD_DIM, NA_HEADS * HEAD_DIM, NA_HEADS * HEAD_DIM]).tolist())
IN_COLS = _AB_COLS[-1]
assert IN_COLS == _CD_COLS[-1]
_MLA_SCALE = (MLA_NOPE + MLA_ROPE) ** -0.5 * LOG2E
_QSCALE = HEAD_DIM ** -0.5 * LOG2E


def _premix0_kernel(xp_ref, xs_ref, mod_ref, g_ref, win_ref, qn_ref, wuq_ref, kvn_ref, wk_ref, wv_ref,
                    c32_ref, sa32_ref, sb32_ref, c64_ref, sa64_ref, sb64_ref,
                    q_ref, k_ref, v_ref, dq_ref, dk_ref, dv_ref,
                    ckv_st, kpe_st, dk_st, dv_st):
    i = pl.program_id(0)
    m = mod_ref[0]
    gain = g_ref[...] * (1.0 + m[1:2])

    def project(n):
        r = _sub_rows(n)
        x = jnp.where(i < NPT, xp_ref[r, :], xs_ref[r, :])
        h = _rms(x, gain) + m[0:1]
        return _dot(h.astype(BF16), win_ref[...])

    states = {}

    def finish(n, proj):
        r = _sub_rows(n)
        q_a, kv_a, dq, dk, dv, pe = (proj[:, a:b] for a, b in zip(_AB_COLS[:-1], _AB_COLS[1:]))
        q = _dot(_rms(q_a, qn_ref[...]).astype(BF16), wuq_ref[...])
        ckv = _rms(kv_a, kvn_ref[...])
        ckv_b = ckv.astype(BF16)
        kn = _dot(ckv_b, wk_ref[...])
        v_ref[r, :] = _dot(ckv_b, wv_ref[...]).astype(BF16)
        c32, sa32, sb32 = c32_ref[r, :], sa32_ref[r, :], sb32_ref[r, :]
        c64, sa64, sb64 = c64_ref[r, :], sa64_ref[r, :], sb64_ref[r, :]
        pe_r = _rope(pe, c32, sa32, sb32, MLA_ROPE // 2)
        for hd in range(MLA_HEADS):
            sl = slice(LANES * hd, LANES * (hd + 1))
            q_ref[r, sl] = (_rope(q[:, sl], c32, sa32, sb32, MLA_ROPE // 2) * _MLA_SCALE).astype(BF16)
            k_ref[r, sl] = (kn[:, sl] + pe_r).astype(BF16)
        for hd in range(DIFF_HEADS):
            sl = slice(LANES * hd, LANES * (hd + 1))
            dq_ref[r, sl] = (_rope(dq[:, sl], c64, sa64, sb64, DIFF_DH // 2) * _QSCALE).astype(BF16)
            dk_ref[r, sl] = _rope(dk[:, sl], c64, sa64, sb64, DIFF_DH // 2).astype(BF16)
        dv_ref[r, :] = dv.astype(BF16)
        states[n] = (ckv, pe, dk, dv)

    _pipelined_units([(project, finish, n) for n in range(N_SUB)])

    @pl.when(i < NPT)
    def _():
        for n in range(N_SUB):
            r = _sub_rows(n)
            ckv, pe, dk, dv = states[n]
            ckv_st[r, :] = ckv
            kpe_st[r, :] = pe
            for hd in range(DIFF_HEADS):
                _store_head_rows(dk_st, n, hd, DIFF_HEADS, dk[:, LANES * hd: LANES * (hd + 1)])
                _store_head_rows(dv_st, n, hd, DIFF_HEADS, dv[:, LANES * hd: LANES * (hd + 1)])


def _premix1_kernel(x_ref, mod_ref, g_ref, win_ref, c64_ref, sa64_ref, sb64_ref,
                    sq_ref, sk_ref, sv_ref, nq_ref, nk_ref, nv_ref,
                    sk_st, sv_st, nk_st, nv_st):
    i = pl.program_id(0)
    m = mod_ref[0]
    gain = g_ref[...] * (1.0 + m[1:2])

    def project(n):
        h = _rms(x_ref[_sub_rows(n), :], gain) + m[0:1]
        return _dot(h.astype(BF16), win_ref[...])

    states = {}

    def finish(n, proj):
        r = _sub_rows(n)
        sq, sk, sv, nq, nk, nv = (proj[:, a:b] for a, b in zip(_CD_COLS[:-1], _CD_COLS[1:]))
        c64, sa64, sb64 = c64_ref[r, :], sa64_ref[r, :], sb64_ref[r, :]
        for hd in range(4):
            sl = slice(LANES * hd, LANES * (hd + 1))
            sq_ref[r, sl] = (_rope(sq[:, sl], c64, sa64, sb64, HEAD_DIM // 2) * _QSCALE).astype(BF16)
        sk_ref[r, :] = _rope(sk, c64, sa64, sb64, HEAD_DIM // 2).astype(BF16)
        sv_ref[r, :] = sv.astype(BF16)
        nq_ref[r, :] = (nq * _QSCALE).astype(BF16)
        nk_ref[r, :] = nk.astype(BF16)
        nv_ref[r, :] = nv.astype(BF16)
        states[n] = (sk, sv, nk, nv)

    _pipelined_units([(project, finish, n) for n in range(N_SUB)])

    @pl.when(i < NPT)
    def _():
        for n in range(N_SUB):
            r = _sub_rows(n)
            sk, sv, nk, nv = states[n]
            sk_st[r, :] = sk
            sv_st[r, :] = sv
            for hd in range(NA_HEADS):
                _store_head_rows(nk_st, n, hd, NA_HEADS, nk[:, HEAD_DIM * hd: HEAD_DIM * (hd + 1)])
                _store_head_rows(nv_st, n, hd, NA_HEADS, nv[:, HEAD_DIM * hd: HEAD_DIM * (hd + 1)])


def _tok_spec(width):
    return pl.BlockSpec((TM, width), lambda i: (i, 0))


_PROMPT_SPEC = pl.BlockSpec((TM, D_MODEL), lambda i: (jnp.minimum(i, NPT - 1), 0))
_SAMPLE_SPEC = pl.BlockSpec((TM, D_MODEL), lambda i: (jnp.maximum(i - NPT, 0), 0))


def _state_spec(width, rows_per_token=1):
    return pl.BlockSpec((TM * rows_per_token, width), lambda i: (jnp.minimum(i, NPT - 1), 0))


def _const_spec(shape):
    return pl.BlockSpec(shape, lambda i: (0,) * len(shape))


_MOD_SPEC = pl.BlockSpec((1, 6, D_MODEL), lambda i: (_mod_row(i), 0, 0))
_ROPE_SPEC = pl.BlockSpec((TM, LANES), lambda i: (_rope_blk(i), 0))


def _premix0(xp, xs, mod_l, g0, w_in_p, q_norm, w_uq_p, kv_norm, w_k_p, w_v, t32, t64):
    outs = [(N_TOK, 1024, BF16), (N_TOK, 1024, BF16), (N_TOK, 512, BF16), (N_TOK, 512, BF16),
            (N_TOK, 512, BF16), (N_TOK, 512, BF16),
            (NP_TOK, 256, F32), (NP_TOK, 128, F32),
            (NP_TOK * DIFF_HEADS, 2 * DIFF_DH, F32), (NP_TOK * DIFF_HEADS, 2 * DIFF_DH, F32)]
    return pl.pallas_call(
        _premix0_kernel,
        grid=(N_TILES,),
        in_specs=[_PROMPT_SPEC, _SAMPLE_SPEC, _MOD_SPEC, _const_spec((1, D_MODEL)),
                  _const_spec((D_MODEL, IN_COLS)), _const_spec((1, MLA_Q_RANK)),
                  _const_spec((MLA_Q_RANK, 1024)), _const_spec((1, MLA_KV_RANK)),
                  _const_spec((MLA_KV_RANK, 1024)), _const_spec((MLA_KV_RANK, 512))]
                 + [_ROPE_SPEC] * 6,
        out_specs=([_tok_spec(w) for (_, w, _) in outs[:6]]
                   + [_state_spec(w, n // NP_TOK) for (n, w, _) in outs[6:]]),
        out_shape=[jax.ShapeDtypeStruct((n, w), dt) for (n, w, dt) in outs],
        compiler_params=_cparams(1),
        name="premix_ab",
    )(xp, xs, mod_l, g0, w_in_p, q_norm, w_uq_p, kv_norm, w_k_p, w_v, *t32, *t64)


def _premix1(x, mod_l, g0, w_in_p, t64):
    outs = [(N_TOK, 512, BF16), (N_TOK, 128, BF16), (N_TOK, 128, BF16), (N_TOK, 512, BF16),
            (N_TOK, 512, BF16), (N_TOK, 512, BF16),
            (NP_TOK, 128, F32), (NP_TOK, 128, F32),
            (NP_TOK * NA_HEADS, HEAD_DIM, F32), (NP_TOK * NA_HEADS, HEAD_DIM, F32)]
    return pl.pallas_call(
        _premix1_kernel,
        grid=(N_TILES,),
        in_specs=[_tok_spec(D_MODEL), _MOD_SPEC, _const_spec((1, D_MODEL)),
                  _const_spec((D_MODEL, IN_COLS))] + [_ROPE_SPEC] * 3,
        out_specs=([_tok_spec(w) for (_, w, _) in outs[:6]]
                   + [_state_spec(w, n // NP_TOK) for (n, w, _) in outs[6:]]),
        out_shape=[jax.ShapeDtypeStruct((n, w), dt) for (n, w, dt) in outs],
        compiler_params=_cparams(1),
        name="premix_cd",
    )(x, mod_l, g0, w_in_p, *t64)


def _mla_cache_kernel(ckv_ref, pe_ref, wk_ref, wv_ref, k_ref, v_ref):
    c = ckv_ref[...].astype(BF16)
    kn = _dot(c, wk_ref[...])
    v_ref[...] = _dot(c, wv_ref[...]).astype(BF16)
    pe = pe_ref[...]
    for hd in range(MLA_HEADS):
        sl = slice(LANES * hd, LANES * (hd + 1))
        k_ref[:, sl] = (kn[:, sl] + pe).astype(BF16)


def _mla_cache(ckv, pe_slab, w_k_p, w_v):
    n = ckv.shape[0]
    tm = 512
    return pl.pallas_call(
        _mla_cache_kernel,
        grid=(n // tm,),
        in_specs=[pl.BlockSpec((tm, MLA_KV_RANK), lambda i: (i, 0)),
                  pl.BlockSpec((tm, LANES), lambda i: (i, 0)),
                  _const_spec((MLA_KV_RANK, 1024)), _const_spec((MLA_KV_RANK, 512))],
        out_specs=[pl.BlockSpec((tm, 1024), lambda i: (i, 0)), pl.BlockSpec((tm, 512), lambda i: (i, 0))],
        out_shape=[jax.ShapeDtypeStruct((n, 1024), BF16), jax.ShapeDtypeStruct((n, 512), BF16)],
        compiler_params=_cparams(1),
        name="mla_cache",
    )(ckv, pe_slab, w_k_p, w_v)


def _softmax_pv(scores, values, sink=None):
    m = jnp.max(scores[0], axis=-1, keepdims=True)
    for s in scores[1:]:
        m = jnp.maximum(m, jnp.max(s, axis=-1, keepdims=True))
    if sink is not None:
        m = jnp.maximum(m, sink)
    l = None
    o = None
    for s, v in zip(scores, values):
        p = jnp.exp2(s - m)
        ls = jnp.sum(p, axis=-1, keepdims=True)
        os_ = _dot(p.astype(BF16), v)
        l = ls if l is None else l + ls
        o = os_ if o is None else o + os_
    if sink is not None:
        l = l + jnp.exp2(sink - m)
    return o * (1.0 / l)


def _lane_lo(shape):
    return lax.broadcasted_iota(jnp.int32, shape, 1) < (LANES // 2)


def _split_halves(qb):
    lo = _lane_lo(qb.shape)
    zero = jnp.zeros_like(qb)
    return jnp.where(lo, qb, zero), jnp.where(lo, zero, qb)


def _attn_ab_kernel(*refs, n_pieces, n_seq, lam_init, aliased):
    if aliased:
        refs = refs[1:]
    q_ref, dq_ref = refs[0], refs[1]
    pieces = [refs[2 + 4 * p: 6 + 4 * p] for p in range(n_pieces)]
    lam_ref, subg_ref, o_ref = refs[2 + 4 * n_pieces:]
    lam = lam_ref[...]
    lam_full = (jnp.exp(jnp.sum(lam[0:1] * lam[1:2], axis=-1, keepdims=True))
                - jnp.exp(jnp.sum(lam[2:3] * lam[3:4], axis=-1, keepdims=True)) + lam_init)
    tq = q_ref.shape[0] // n_seq
    lo = _lane_lo((tq, LANES))
    subg = subg_ref[...] * (1.0 - lam_init)

    def seq_rows(ref, sq):
        n = ref.shape[0] // n_seq
        return slice(n * sq, n * (sq + 1))

    def mla_scores(arg):
        sq, hd = arg
        sl = slice(LANES * hd, LANES * (hd + 1))
        qh = q_ref[seq_rows(q_ref, sq), sl]
        return [_dot_nt(qh, k_ref[seq_rows(k_ref, sq), sl]) for (k_ref, _, _, _) in pieces]

    def diff_scores(arg):
        sq, hd = arg
        sl = slice(LANES * hd, LANES * (hd + 1))
        qq = jnp.concatenate(_split_halves(dq_ref[seq_rows(dq_ref, sq), sl]), axis=0)
        return [_dot_nt(qq, dk_ref[seq_rows(dk_ref, sq), sl]) for (_, _, dk_ref, _) in pieces]

    pair = {}

    def mla_finish(arg, scores):
        sq, hd = arg
        j = hd // 2
        vals = [v_ref[seq_rows(v_ref, sq), LANES * j: LANES * (j + 1)] for (_, v_ref, _, _) in pieces]
        pair[hd % 2] = _softmax_pv(scores, vals)
        if hd % 2 == 1:
            o_ref[seq_rows(o_ref, sq), LANES * j: LANES * (j + 1)] = jnp.where(lo, pair[0], pair[1]).astype(BF16)

    def diff_finish(arg, scores):
        sq, hd = arg
        sl = slice(LANES * hd, LANES * (hd + 1))
        oo = _softmax_pv(scores, [dv_ref[seq_rows(dv_ref, sq), sl] for (_, _, _, dv_ref) in pieces])
        od = _rms(oo[:tq] - lam_full * oo[tq:], subg)
        o_ref[seq_rows(o_ref, sq), 512 + LANES * hd: 512 + LANES * (hd + 1)] = od.astype(BF16)

    units = []
    for sq in range(n_seq):
        for j in range(DIFF_HEADS):
            units += [(diff_scores, diff_finish, (sq, j)), (mla_scores, mla_finish, (sq, 2 * j)),
                      (mla_scores, mla_finish, (sq, 2 * j + 1))]
    _pipelined_units(units)


def _attn_ab(q, dq, new_kv, cache_kv, lam, sub_g, lam_init, *, n_batch, t_len, tq, tok_off, out_init=None,
             n_seq=1):
    assert n_seq == 1 or (tq == t_len and cache_kv is None and n_batch % n_seq == 0)
    n_batch, t_len, tq = n_batch // n_seq, t_len * n_seq, tq * n_seq
    nq = t_len // tq
    q_off = tok_off // tq
    b_off = tok_off // t_len
    widths = (1024, 512, 512, 512)
    in_specs = [pl.BlockSpec((tq, 1024), lambda b, i: (q_off + b * nq + i, 0)),
                pl.BlockSpec((tq, 512), lambda b, i: (q_off + b * nq + i, 0))]
    args = [q, dq]
    for w, a in zip(widths, new_kv):
        in_specs.append(pl.BlockSpec((t_len, w), lambda b, i: (b_off + b, 0)))
        args.append(a)
    n_pieces = 1
    if cache_kv is not None:
        n_pieces = 2
        for w, a in zip(widths, cache_kv):
            in_specs.append(pl.BlockSpec((PAST_LEN, w), lambda b, i: (b, 0)))
            args.append(a)
    in_specs += [pl.BlockSpec((4, DIFF_DH), lambda b, i: (0, 0)),
                 pl.BlockSpec((1, 2 * DIFF_DH), lambda b, i: (0, 0))]
    args += [lam, sub_g]
    aliases = {}
    if out_init is not None:
        in_specs = [pl.BlockSpec(memory_space=pl.ANY)] + in_specs
        args = [out_init] + args
        aliases = {0: 0}
    return pl.pallas_call(
        functools.partial(_attn_ab_kernel, n_pieces=n_pieces, n_seq=n_seq, lam_init=lam_init,
                          aliased=out_init is not None),
        grid=(n_batch, nq),
        in_specs=in_specs,
        out_specs=pl.BlockSpec((tq, 1024), lambda b, i: (q_off + b * nq + i, 0)),
        out_shape=jax.ShapeDtypeStruct((N_TOK, 1024), BF16),
        input_output_aliases=aliases,
        compiler_params=_cparams(2),
        name="attn_ab_%d" % n_pieces,
    )(*args)


def _gqa_stacks(sq_ref, sink_ref):
    tq = sq_ref.shape[0]
    halves = [_split_halves(sq_ref[:, LANES * j: LANES * (j + 1)]) for j in range(4)]
    q_stacks = [jnp.concatenate([halves[j][kvh] for j in range(4)], axis=0) for kvh in range(SWA_KV_HEADS)]
    sinks = [jnp.concatenate([jnp.full((tq, 1), sink_ref[4 * kvh + j] * LOG2E, F32) for j in range(4)], axis=0)
             for kvh in range(SWA_KV_HEADS)]
    return q_stacks, sinks


def _attn_cd_prompt_kernel(sink_ref, sq_ref, sk_ref, sv_ref, nq_ref, nk_ref, nv_ref, o_ref):
    lo = _lane_lo((SEQ, LANES))
    for sq in range(PROMPT_SEQS_PER_STEP):
        r = slice(SEQ * sq, SEQ * (sq + 1))
        sk = sk_ref[r, :]
        sv = sv_ref[r, :]
        for j in range(4):
            sl = slice(LANES * j, LANES * (j + 1))
            q_lo, q_hi = _split_halves(sq_ref[r, sl])
            o_lo = _softmax_pv([_dot_nt(q_lo, sk)], [sv], sink=sink_ref[j] * LOG2E)
            o_hi = _softmax_pv([_dot_nt(q_hi, sk)], [sv], sink=sink_ref[j + 4] * LOG2E)
            o_ref[r, sl] = jnp.where(lo, o_lo, o_hi).astype(BF16)
        for j in range(4):
            sl = slice(LANES * j, LANES * (j + 1))
            q_lo, q_hi = _split_halves(nq_ref[r, sl])
            k = nk_ref[r, sl]
            v = nv_ref[r, sl]
            o_lo = _softmax_pv([_dot_nt(q_lo, k)], [v])
            o_hi = _softmax_pv([_dot_nt(q_hi, k)], [v])
            o_ref[r, 512 + LANES * j: 512 + LANES * (j + 1)] = jnp.where(lo, o_lo, o_hi).astype(BF16)


def _attn_cd_prompt(sink, sq, sk, sv, nq, nk, nv):
    def spec(w):
        return pl.BlockSpec((SEQ * PROMPT_SEQS_PER_STEP, w), lambda b: (b, 0))
    return pl.pallas_call(
        _attn_cd_prompt_kernel,
        grid=(BATCH // PROMPT_SEQS_PER_STEP,),
        in_specs=[pl.BlockSpec(memory_space=pltpu.SMEM), spec(512), spec(128), spec(128),
                  spec(512), spec(512), spec(512)],
        out_specs=spec(1024),
        out_shape=jax.ShapeDtypeStruct((N_TOK, 1024), BF16),
        compiler_params=_cparams(1),
        name="attn_cd_prompt",
    )(sink, sq, sk, sv, nq, nk, nv)


_SWA_KEYS = TQ + 2 * SWA_WINDOW


def _attn_cd_sample_kernel(init_ref, sink_ref, sq_ref, nq_ref, sk_ref, sv_ref, nk_ref, nv_ref,
                           skc_ref, svc_ref, nkc_ref, nvc_ref, bias_ref, o_ref):
    del init_ref
    qi = pl.program_id(1)
    lo = _lane_lo((TQ, LANES))
    ks = pl.multiple_of(jnp.clip(qi * TQ - SWA_WINDOW, 0, DEC_SEQ - _SWA_KEYS), SWA_WINDOW)
    k_win = sk_ref[pl.ds(ks, _SWA_KEYS), :]
    v_win = sv_ref[pl.ds(ks, _SWA_KEYS), :]
    q_pos = qi * TQ + (lax.broadcasted_iota(jnp.int32, (4 * TQ, _SWA_KEYS), 0) & (TQ - 1))
    k_pos = ks + lax.broadcasted_iota(jnp.int32, (4 * TQ, _SWA_KEYS), 1)
    in_win = jnp.abs(q_pos - k_pos) <= SWA_WINDOW
    skc = skc_ref[...]
    svc = svc_ref[...]
    q_stacks, sinks = _gqa_stacks(sq_ref, sink_ref)

    def window_scores(kvh):
        return [_dot_nt(q_stacks[kvh], skc), jnp.where(in_win, _dot_nt(q_stacks[kvh], k_win), NEG)]

    o_kv = {}

    def window_finish(kvh, scores):
        o_kv[kvh] = _softmax_pv(scores, [svc, v_win], sink=sinks[kvh])
        if kvh == SWA_KV_HEADS - 1:
            for j in range(4):
                rows = slice(TQ * j, TQ * (j + 1))
                o_ref[:, LANES * j: LANES * (j + 1)] = jnp.where(lo, o_kv[0][rows], o_kv[1][rows]).astype(BF16)

    n_rows = DEC_SEQ // GRID_W
    r0 = jnp.clip(qi * NA_TILE_ROWS - NA_WIN_ROWS // 2, 0, n_rows - NA_KEY_ROWS)
    kn = pl.multiple_of(r0 * GRID_W, GRID_W)

    lane_lo = _lane_lo((1, LANES))
    pieces = []
    for ri in range(NA_TILE_ROWS):
        r = qi * NA_TILE_ROWS + ri
        rs = jnp.clip(r - NA_WIN_ROWS // 2, 0, n_rows - NA_WIN_ROWS)
        row = []
        for mm in range(NA_KEY_ROWS // 2):
            kr = r0 + 2 * mm
            ok = [(kr + t >= rs) & (kr + t < rs + NA_WIN_ROWS) for t in range(2)]
            mask = jnp.where(lane_lo, jnp.where(ok[0], 0.0, NEG), jnp.where(ok[1], 0.0, NEG))
            row.append((jnp.clip(kr - r + NA_WIN_ROWS, 0, 2 * NA_WIN_ROWS - 1), mask))
        pieces.append(row)

    def add_na_bias(hd, s):
        return jnp.concatenate(
            [jnp.concatenate([s[GRID_W * ri: GRID_W * (ri + 1), LANES * mm: LANES * (mm + 1)] + bias_ref[hd, d] + mask
                              for mm, (d, mask) in enumerate(row)], axis=1)
             for ri, row in enumerate(pieces)], axis=0)

    def na_scores(hd):
        sl = slice(LANES * (hd // 2), LANES * (hd // 2 + 1))
        q_half = _split_halves(nq_ref[:, sl])[hd % 2]
        nk_win = nk_ref[pl.ds(kn, NA_KEY_ROWS * GRID_W), sl]
        return [_dot_nt(q_half, nkc_ref[:, sl]), add_na_bias(hd, _dot_nt(q_half, nk_win))]

    pair = {}

    def na_finish(hd, scores):
        j = hd // 2
        sl = slice(LANES * j, LANES * (j + 1))
        nv_win = nv_ref[pl.ds(kn, NA_KEY_ROWS * GRID_W), sl]
        pair[hd % 2] = _softmax_pv(scores, [nvc_ref[:, sl], nv_win])
        if hd % 2 == 1:
            o_ref[:, 512 + LANES * j: 512 + LANES * (j + 1)] = jnp.where(lo, pair[0], pair[1]).astype(BF16)

    _pipelined_units([(window_scores, window_finish, kvh) for kvh in range(SWA_KV_HEADS)]
                     + [(na_scores, na_finish, hd) for hd in range(NA_HEADS)])


def _attn_cd_sample(out_init, sink, sq, sk, sv, nq, nk, nv, skc, svc, nkc, nvc, bias):
    nq_t = DEC_SEQ // TQ
    q_off = NP_TOK // TQ
    b_off = NP_TOK // DEC_SEQ

    def qspec(w):
        return pl.BlockSpec((TQ, w), lambda b, i: (q_off + b * nq_t + i, 0))

    def kspec(w):
        return pl.BlockSpec((DEC_SEQ, w), lambda b, i: (b_off + b, 0))

    def cspec(w):
        return pl.BlockSpec((PAST_LEN, w), lambda b, i: (b, 0))

    bias_spec = pl.BlockSpec((NA_HEADS, 2 * NA_WIN_ROWS, GRID_W, LANES), lambda b, i: (0, 0, 0, 0))
    return pl.pallas_call(
        _attn_cd_sample_kernel,
        grid=(DEC_BATCH, nq_t),
        in_specs=[pl.BlockSpec(memory_space=pl.ANY), pl.BlockSpec(memory_space=pltpu.SMEM),
                  qspec(512), qspec(512), kspec(128), kspec(128), kspec(512), kspec(512),
                  cspec(128), cspec(128), cspec(512), cspec(512), bias_spec],
        out_specs=pl.BlockSpec((TQ, 1024), lambda b, i: (q_off + b * nq_t + i, 0)),
        out_shape=jax.ShapeDtypeStruct((N_TOK, 1024), BF16),
        input_output_aliases={0: 0},
        compiler_params=_cparams(2),
        name="attn_cd_sample",
    )(out_init, sink, sq, nq, sk, sv, nk, nv, skc, svc, nkc, nvc, bias)


def _na_bias(rpb):
    n_dc = 2 * NA_WIN_COLS - 1
    c = np.arange(GRID_W)[:, None]
    kc = np.arange(GRID_W)[None, :]
    qs = np.clip(c - NA_WIN_COLS // 2, 0, GRID_W - NA_WIN_COLS)
    col_ok = (kc >= qs) & (kc < qs + NA_WIN_COLS)
    dc = np.clip(kc - c + NA_WIN_COLS - 1, 0, n_dc - 1)
    onehot = ((dc[None] == np.arange(n_dc)[:, None, None]) & col_ok[None]).astype(np.float32)
    blocks = jnp.einsum('hrd,dck->hrck', rpb.astype(F32) * LOG2E, onehot, precision=lax.Precision.HIGHEST)
    blocks = jnp.where(col_ok[None, None], blocks, NEG)
    none = jnp.full((NA_HEADS, 1, GRID_W, GRID_W), NEG, F32)
    return jnp.concatenate([jnp.concatenate([none, blocks], axis=1),
                            jnp.concatenate([blocks, none], axis=1)], axis=-1)


_HI_MASK = -65536


def _pack_pairs(x):
    w = x.shape[1] // 2
    r = x.astype(BF16).astype(F32)
    lo = lax.bitcast_convert_type(r[:, :w], jnp.int32)
    hi = lax.bitcast_convert_type(r[:, w:], jnp.int32)
    return (hi & _HI_MASK) | lax.shift_right_logical(lo, 16)


def _unpack_pairs(p):
    lo = lax.bitcast_convert_type(lax.shift_left(p, 16), F32)
    hi = lax.bitcast_convert_type(p & _HI_MASK, F32)
    return lo, hi


def _postmix_kernel(*refs, split_x):
    if split_x:
        o_ref, xp_ref, xs_ref = refs[:3]
        refs = refs[3:]
    else:
        o_ref, x_ref = refs[:2]
        refs = refs[2:]
    (mod_ref, g_ref, wout_ref, wr_ref, br_ref, tri_ref,
     x1_ref, h2_ref, route_ref, gate_ref, cnt_ref, run_ref) = refs
    i = pl.program_id(0)

    @pl.when(i == 0)
    def _():
        run_ref[...] = jnp.zeros_like(run_ref)

    m = mod_ref[0]
    g = g_ref[...]
    gate_gain = g[1:2] * m[2:3]
    ffn_gain = g[2:3] * (1.0 + m[4:5])
    sub_logits = {}

    def project(n):
        return _dot(o_ref[_sub_rows(n), :], wout_ref[...])

    def finish(n, y):
        r = _sub_rows(n)
        x = jnp.where(i < NPT, xp_ref[r, :], xs_ref[r, :]) if split_x else x_ref[r, :]
        x1 = x + _rms(y, gate_gain)
        x1_ref[r, :] = x1
        h2 = _rms(x1, ffn_gain) + m[3:4]
        h2_ref[r, :] = _pack_pairs(h2)
        sub_logits[n] = _dot(h2.astype(BF16), wr_ref[...]) + br_ref[...]

    _pipelined_units([(project, finish, n) for n in range(N_SUB)])

    logits = jnp.concatenate([sub_logits[n] for n in range(N_SUB)], axis=0)
    lane = lax.broadcasted_iota(jnp.int32, logits.shape, 1).astype(F32)
    cur = jnp.where(lane < N_EXPERTS, logits, -jnp.inf)
    tops, idxs = [], []
    for _ in range(TOP_K):
        mx = jnp.max(cur, axis=-1, keepdims=True)
        ix = jnp.min(jnp.where(cur == mx, lane, float(LANES)), axis=-1, keepdims=True)
        tops.append(mx)
        idxs.append(ix)
        cur = jnp.where(lane == ix, -jnp.inf, cur)
    es = [jnp.exp(t - tops[0]) for t in tops]
    inv = 1.0 / (es[0] + es[1] + es[2] + es[3])
    picked = jnp.zeros_like(logits)
    for k in range(TOP_K):
        picked = jnp.where(lane == idxs[k], 1.0, picked)
    before = _dot(tri_ref[...], picked.astype(BF16)) + run_ref[0:1, :]
    route = jnp.zeros_like(logits)
    gate_out = jnp.zeros_like(logits)
    for k in range(TOP_K):
        rank = jnp.sum(jnp.where(lane == idxs[k], before, 0.0), axis=-1, keepdims=True)
        route = jnp.where(lane == float(k), idxs[k], route)
        route = jnp.where(lane == float(TOP_K + k), rank, route)
        gate_out = jnp.where(lane == float(k), es[k] * inv, gate_out)
    route_ref[...] = route.T[:2 * TOP_K].astype(jnp.int32)
    gate_ref[...] = gate_out
    run_ref[...] = run_ref[...] + jnp.sum(picked, axis=0, keepdims=True)
    cnt_ref[...] = run_ref[...].astype(jnp.int32)


def _postmix(o_cat, xs, mod_l, g_l, w_out, w_r, b_r):
    tri = jnp.asarray(np.tril(np.ones((TM, TM), np.float32), -1), BF16)
    split_x = len(xs) == 2
    x_specs = [_PROMPT_SPEC, _SAMPLE_SPEC] if split_x else [_tok_spec(D_MODEL)]
    return pl.pallas_call(
        functools.partial(_postmix_kernel, split_x=split_x),
        grid=(N_TILES,),
        in_specs=[_tok_spec(1024)] + x_specs + [_MOD_SPEC, _const_spec((4, D_MODEL)),
                  _const_spec((1024, D_MODEL)), _const_spec((D_MODEL, LANES)), _const_spec((1, LANES)),
                  _const_spec((TM, TM))],
        out_specs=[_tok_spec(D_MODEL), _tok_spec(D_MODEL // 2), pl.BlockSpec((2 * TOP_K, TM), lambda i: (0, i)),
                   _tok_spec(LANES), _const_spec((8, LANES))],
        out_shape=[jax.ShapeDtypeStruct((N_TOK, D_MODEL), F32),
                   jax.ShapeDtypeStruct((N_TOK, D_MODEL // 2), jnp.int32),
                   jax.ShapeDtypeStruct((2 * TOP_K, N_TOK), jnp.int32), jax.ShapeDtypeStruct((N_TOK, LANES), F32),
                   jax.ShapeDtypeStruct((8, LANES), jnp.int32)],
        scratch_shapes=[pltpu.VMEM((8, LANES), F32)],
        compiler_params=_cparams(1),
        name="postmix",
    )(o_cat, *xs, mod_l, g_l, w_out, w_r, b_r, tri)


SC_WORKERS = 32
SC_ROWS = 64
ROW_WORDS = D_MODEL // 2

_SC_SCRATCH = [pltpu.VMEM((SC_ROWS, ROW_WORDS), jnp.int32), pltpu.VMEM((SC_ROWS, ROW_WORDS), jnp.int32),
               pltpu.SemaphoreType.DMA, pltpu.SemaphoreType.DMA, pltpu.SemaphoreType.DMA, pltpu.SemaphoreType.DMA]


def _sc_worker_id():
    return lax.axis_index("s") * 2 + lax.axis_index("c")


def _sc_double_buffered(n_chunks, load, store):
    for cp in load(0, 0):
        cp.start()

    @pl.loop(0, n_chunks, step=2)
    def _(g0):
        for b in range(2):
            g = g0 + b
            for cp in load(g, b):
                cp.wait()

            @pl.when(g >= 1)
            def _():
                for cp in store(g - 1, 1 - b):
                    cp.wait()

            @pl.when(g + 1 < n_chunks)
            def _():
                for cp in load(g + 1, 1 - b):
                    cp.start()

            for cp in store(g, b):
                cp.start()

    for cp in store(n_chunks - 1, (n_chunks - 1) % 2):
        cp.wait()


def _sc_ring_buffered(n_chunks, depth, load, store):
    def chunk(g, b, static):
        when = (lambda c: (lambda f: f() if c else None)) if static else pl.when
        freed = (b + depth - 1) % depth
        for cp in load(g, b):
            cp.wait()

        @when(g >= 1)
        def _():
            for cp in store(g - 1, freed):
                cp.wait()

        @when(g + depth - 1 < n_chunks)
        def _():
            for cp in load(g + depth - 1, freed):
                cp.start()

        for cp in store(g, b):
            cp.start()

    for g in range(depth - 1):
        for cp in load(g, g):
            cp.start()
    whole = n_chunks // depth * depth

    @pl.loop(0, whole, step=depth)
    def _(g0):
        for b in range(depth):
            chunk(g0 + b, b, False)

    for g in range(whole, n_chunks):
        chunk(g, g % depth, True)
    for cp in store(n_chunks - 1, (n_chunks - 1) % depth):
        cp.wait()


def _sc_dispatch(src, idx):
    n_chunks = N_TOK // (SC_WORKERS * SC_ROWS)
    assert n_chunks % 2 == 0
    mesh = plsc.VectorSubcoreMesh(core_axis_name="c", subcore_axis_name="s")

    @functools.partial(
        pl.kernel, mesh=mesh,
        out_type=jax.ShapeDtypeStruct((MOE_ROWS, ROW_WORDS), jnp.int32),
        scratch_types=[pltpu.VMEM((n_chunks * TOP_K, SC_ROWS), jnp.int32)] + _SC_SCRATCH)
    def k(src_hbm, idx_hbm, out_hbm, idx_v, buf0, buf1, in0, in1, out0, out1):
        wid = _sc_worker_id()
        pltpu.sync_copy(idx_hbm.at[wid], idx_v)
        bufs, in_sems, out_sems = (buf0, buf1), (in0, in1), (out0, out1)

        def load(g, b):
            rows = pl.ds((wid * n_chunks + g) * SC_ROWS, SC_ROWS)
            return [pltpu.make_async_copy(src_hbm.at[rows], bufs[b], in_sems[b])]

        def store(g, b):
            return [pltpu.make_async_copy(bufs[b], out_hbm.at[idx_v.at[g * TOP_K + kk]], out_sems[b])
                    for kk in range(TOP_K)]

        _sc_double_buffered(n_chunks, load, store)

    return k(src, idx)


def _sc_collect(table, idx):
    n_chunks = idx.shape[1]
    depth = 3
    assert n_chunks >= depth
    mesh = plsc.VectorSubcoreMesh(core_axis_name="c", subcore_axis_name="s")

    @functools.partial(
        pl.kernel, mesh=mesh,
        out_type=jax.ShapeDtypeStruct((SC_WORKERS * n_chunks * SC_ROWS, ROW_WORDS), jnp.int32),
        scratch_types=[pltpu.VMEM((n_chunks, SC_ROWS), jnp.int32)]
        + [pltpu.VMEM((SC_ROWS, ROW_WORDS), jnp.int32)] * depth + [pltpu.SemaphoreType.DMA] * (2 * depth))
    def k(table_hbm, idx_hbm, out_hbm, idx_v, *scratch):
        wid = _sc_worker_id()
        pltpu.sync_copy(idx_hbm.at[wid], idx_v)
        bufs, in_sems, out_sems = scratch[:depth], scratch[depth:2 * depth], scratch[2 * depth:]

        def load(g, b):
            return [pltpu.make_async_copy(table_hbm.at[idx_v.at[g]], bufs[b], in_sems[b])]

        def store(g, b):
            rows = pl.ds((wid * n_chunks + g) * SC_ROWS, SC_ROWS)
            return [pltpu.make_async_copy(bufs[b], out_hbm.at[rows], out_sems[b])]

        _sc_ring_buffered(n_chunks, depth, load, store)

    return k(table, idx)


def _expert_rows(words, n_valid, wgu_b, wd_b, bgu, bd):
    live = lax.broadcasted_iota(jnp.int32, words.shape, 0) < n_valid
    lo, hi = _unpack_pairs(jnp.where(live, words, 0))
    x = jnp.concatenate([lo, hi], axis=1).astype(BF16)
    gu = _dot(x, wgu_b[...]) + bgu
    g = jnp.minimum(gu[:, :D_EXPERT], SWIGLU_LIMIT)
    u = jnp.clip(gu[:, D_EXPERT:], -SWIGLU_LIMIT, SWIGLU_LIMIT)
    a = g * jax.nn.sigmoid(SWIGLU_ALPHA * g) * (u + 1.0)
    return _pack_pairs(_dot(a.astype(BF16), wd_b[...]) + bd)


def _moe_kernel(blk_e_ref, blk_first_ref, blk_rows_ref, blk_slot_ref, blk_next_ref, blk_io_ref,
                x_ref, wgu_hbm, bgu_ref, wd_hbm, bd_ref, y_ref,
                wgu_f, wd_f, wgu_b, wd_b, sem, *, layer):
    del blk_io_ref
    i = pl.program_id(0)
    n_valid = blk_rows_ref[i]
    quantum = MOE_TM // MOE_TAIL_PARTS

    def weight_copies(e, slot):
        return (pltpu.make_async_copy(wgu_hbm.at[layer, e], wgu_f.at[slot], sem.at[0, slot]),
                pltpu.make_async_copy(wd_hbm.at[layer, e], wd_f.at[slot], sem.at[1, slot]))

    @pl.when(i == 0)
    def _():
        for cp in weight_copies(blk_e_ref[0], blk_slot_ref[0]):
            cp.start()

    @pl.when(blk_first_ref[i] == 1)
    def _():
        slot = blk_slot_ref[i]
        for cp in weight_copies(blk_e_ref[i], slot):
            cp.wait()
        nxt = blk_next_ref[i]

        @pl.when(nxt >= 0)
        def _():
            for cp in weight_copies(nxt, 1 - slot):
                cp.start()

        wgu_b[...] = wgu_f[slot].astype(BF16)
        wd_b[...] = wd_f[slot].astype(BF16)

    for parts in range(1, MOE_TAIL_PARTS + 1):
        rows = parts * quantum

        @pl.when((n_valid > rows - quantum) & (n_valid <= rows))
        def _(rows=rows):
            y_ref[:rows] = _expert_rows(x_ref[:rows], n_valid, wgu_b, wd_b, bgu_ref[0, 0], bd_ref[0, 0])
            if rows < MOE_TM:
                y_ref[rows:] = jnp.zeros((MOE_TM - rows, ROW_WORDS), jnp.int32)


def _moe(layer, blk_meta, xs, w_gu, b_gu, w_down, b_down):
    def row_map(i, e, first, rows, slot, nxt, io):
        return (io[i], 0)

    def bias_map(i, e, *_):
        return (layer, e[i], 0, 0)

    grid_spec = pltpu.PrefetchScalarGridSpec(
        num_scalar_prefetch=6,
        grid=(MOE_BLOCKS,),
        in_specs=[
            pl.BlockSpec((MOE_TM, ROW_WORDS), row_map),
            pl.BlockSpec(memory_space=pl.ANY),
            pl.BlockSpec((1, 1, 1, 2 * D_EXPERT), bias_map),
            pl.BlockSpec(memory_space=pl.ANY),
            pl.BlockSpec((1, 1, 1, D_MODEL), bias_map),
        ],
        out_specs=pl.BlockSpec((MOE_TM, ROW_WORDS), row_map),
        scratch_shapes=[pltpu.VMEM((2, D_MODEL, 2 * D_EXPERT), F32), pltpu.VMEM((2, D_EXPERT, D_MODEL), F32),
                        pltpu.VMEM((D_MODEL, 2 * D_EXPERT), BF16), pltpu.VMEM((D_EXPERT, D_MODEL), BF16),
                        pltpu.SemaphoreType.DMA((2, 2))],
    )
    return pl.pallas_call(
        functools.partial(_moe_kernel, layer=layer),
        grid_spec=grid_spec,
        out_shape=jax.ShapeDtypeStruct((MOE_ROWS, ROW_WORDS), jnp.int32),
        compiler_params=_cparams(1),
        name="moe_experts",
    )(*blk_meta, xs, w_gu, b_gu.reshape(DEPTH, N_EXPERTS, 1, 2 * D_EXPERT),
      w_down, b_down.reshape(DEPTH, N_EXPERTS, 1, D_MODEL))


def _route(route, counts):
    experts = jnp.arange(N_EXPERTS, dtype=jnp.int32)
    padded = (counts + MOE_TM - 1) // MOE_TM * MOE_TM
    pad_end = jnp.cumsum(padded)
    pad_start = pad_end - padded
    e = route[0:TOP_K]
    onehot = e[:, :, None] == experts[None, None, :]
    dest = jnp.sum(jnp.where(onehot, pad_start[None, None, :], 0), axis=-1) + route[TOP_K:2 * TOP_K]
    blk_row0 = jnp.arange(MOE_BLOCKS, dtype=jnp.int32) * MOE_TM
    blk_e = jnp.minimum(jnp.sum((pad_end[None, :] <= blk_row0[:, None]).astype(jnp.int32), axis=1),
                        N_EXPERTS - 1)
    on = blk_row0 < pad_end[-1]
    n_on = jnp.sum(on.astype(jnp.int32))
    blk_onehot = blk_e[:, None] == experts[None, :]
    row_end = jnp.sum(jnp.where(blk_onehot, (pad_start + counts)[None, :], 0), axis=1)
    blk_rows = jnp.where(on, jnp.clip(row_end - blk_row0, 0, MOE_TM), 0).astype(jnp.int32)
    last_e = jnp.sum(jnp.where(jnp.arange(MOE_BLOCKS) == n_on - 1, blk_e, 0))
    blk_e = jnp.where(on, blk_e, last_e).astype(jnp.int32)
    prev = jnp.concatenate([jnp.full((1,), -1, jnp.int32), blk_e[:-1]])
    blk_first = (blk_e != prev).astype(jnp.int32)
    blk_slot = ((jnp.cumsum(blk_first) - 1) % 2).astype(jnp.int32)
    later_used = (experts[None, :] > experts[:, None]) & (counts[None, :] > 0)
    next_used = jnp.min(jnp.where(later_used, experts[None, :], N_EXPERTS), axis=1)
    next_used = jnp.where(next_used == N_EXPERTS, -1, next_used)
    blk_next = jnp.sum(jnp.where(blk_e[:, None] == experts[None, :], next_used[None, :], 0), axis=1)
    blk_io = jnp.minimum(jnp.arange(MOE_BLOCKS, dtype=jnp.int32), n_on - 1)
    return dest, (blk_e, blk_first, blk_rows, blk_slot, blk_next.astype(jnp.int32), blk_io)


def _combine_kernel(y_ref, gate_ref, x1_ref, mod_ref, g_ref, *o_refs):
    npt = NP_TOK // TM_WIDE
    i = pl.program_id(0)
    m = mod_ref[0]
    gate = gate_ref[...]
    acc_lo = None
    for k in range(TOP_K):
        lo, hi = _unpack_pairs(y_ref[k])
        gk = gate[:, k:k + 1]
        acc_lo = gk * lo if acc_lo is None else acc_lo + gk * lo
        acc_hi = gk * hi if k == 0 else acc_hi + gk * hi
    acc = jnp.concatenate([acc_lo, acc_hi], axis=1)
    out = x1_ref[...] + _rms(acc, g_ref[...][3:4] * m[5:6])
    if len(o_refs) == 1:
        o_refs[0][...] = out
    else:
        @pl.when(i < npt)
        def _():
            o_refs[0][...] = out

        @pl.when(i >= npt)
        def _():
            o_refs[1][...] = out


def _combine(yg, gates, x1, mod_l, g_l, split_out):
    tm = TM_WIDE
    npt = NP_TOK // tm

    def tok(w):
        return pl.BlockSpec((tm, w), lambda i: (i, 0))

    if split_out:
        out_specs = [pl.BlockSpec((tm, D_MODEL), lambda i: (jnp.minimum(i, npt - 1), 0)),
                     pl.BlockSpec((tm, D_MODEL), lambda i: (jnp.maximum(i - npt, 0), 0))]
        out_shape = [jax.ShapeDtypeStruct((NP_TOK, D_MODEL), F32), jax.ShapeDtypeStruct((NS_TOK, D_MODEL), F32)]
    else:
        out_specs = [tok(D_MODEL)]
        out_shape = [jax.ShapeDtypeStruct((N_TOK, D_MODEL), F32)]
    mod_spec = pl.BlockSpec((1, 6, D_MODEL), lambda i: (jnp.where(i < npt, 0, 1 + (i - npt) // (DEC_SEQ // tm)), 0, 0))
    return pl.pallas_call(
        _combine_kernel,
        grid=(N_TOK // tm,),
        in_specs=[pl.BlockSpec((TOP_K, tm, ROW_WORDS), lambda i: (0, i, 0)), tok(LANES),
                  tok(D_MODEL), mod_spec, _const_spec((4, D_MODEL))],
        out_specs=out_specs,
        out_shape=out_shape,
        compiler_params=_cparams(1),
        name="combine",
    )(yg, gates, x1, mod_l, g_l)


def _ffn(layer, o_cat, xs, mod_l, g_l, w_out, w_router, b_router, w_gu, b_gu, w_down, b_down, split_out):
    w_r = jnp.pad(w_router, ((0, 0), (0, LANES - N_EXPERTS))).astype(BF16)
    b_r = jnp.pad(b_router, (0, LANES - N_EXPERTS)).reshape(1, LANES)
    x1, h2p, route, gate_slab, counts = _postmix(o_cat, xs, mod_l, g_l, w_out.astype(BF16), w_r, b_r)
    dest, blk_meta = _route(route, counts[0, :N_EXPERTS])
    n_chunks = N_TOK // (SC_WORKERS * SC_ROWS)
    idx_d = dest.reshape(TOP_K, SC_WORKERS, n_chunks, SC_ROWS).transpose(1, 2, 0, 3).reshape(
        SC_WORKERS, n_chunks * TOP_K, SC_ROWS)
    rows = _sc_dispatch(h2p, idx_d)
    ys = _moe(layer, blk_meta, rows, w_gu, b_gu, w_down, b_down)
    idx_c = dest.reshape(SC_WORKERS, TOP_K * n_chunks, SC_ROWS)
    yg = _sc_collect(ys, idx_c).reshape(TOP_K, N_TOK, ROW_WORDS)
    return _combine(yg, gate_slab, x1, mod_l, g_l, split_out)


def _pad_heads(w, n_heads, width, keep):
    k = w.shape[0]
    w = w.reshape(k, n_heads, width)[:, :, :keep]
    return jnp.pad(w, ((0, 0), (0, 0), (0, LANES - keep))).reshape(k, n_heads * LANES)


def _pe_slab(x):
    return jnp.pad(x, [(0, 0)] * (x.ndim - 1) + [(MLA_NOPE, LANES - MLA_NOPE - MLA_ROPE)])


def _pair_kv_heads(w):
    g = SWA_HEADS // SWA_KV_HEADS
    return w.reshape(SWA_KV_HEADS, g, HEAD_DIM, -1).transpose(1, 0, 2, 3).reshape(w.shape)


def kernel(x_prompt, x_sample, cache_mla_ckv, cache_mla_krope, cache_diff_k, cache_diff_v, cache_swa_k, cache_swa_v, cache_na_k, cache_na_v, c, c_ctx, w_mod, b_mod, norm_g, w_in0, mla_q_norm, w_uq, mla_kv_norm, w_ukv, diff_lambda, diff_norm, w_out0, w_in1, swa_sink, na_rpb, w_out1, w_router, b_router, w_gu, b_gu, w_down, b_down):
    xs = (x_prompt.reshape(NP_TOK, D_MODEL), x_sample.reshape(NS_TOK, D_MODEL))
    cond = jnp.concatenate([c_ctx[None, :], c, jnp.zeros((16 - 1 - DEC_BATCH, D_MODEL), F32)], axis=0)
    mod = _modulation(cond, w_mod, b_mod).reshape(DEPTH, 16, 6, D_MODEL)
    t64, t32 = _rope_tables()
    states = {}
    for l in range(DEPTH):
        i = l // 2
        g_l = norm_g[l]
        mod_l = mod[l]
        if l % 2 == 0:
            lam_init = 0.8 - 0.6 * math.exp(-0.3 * l)
            wi = w_in0[i]
            w_in_p = jnp.concatenate(
                [wi[:, 0:640], wi[:, 672:2208], _pe_slab(wi[:, 640:672])], axis=1).astype(BF16)
            w_uq_p = _pad_heads(w_uq[i], MLA_HEADS, MLA_NOPE + MLA_ROPE, MLA_NOPE + MLA_ROPE).astype(BF16)
            w_k_p = _pad_heads(w_ukv[i], MLA_HEADS, MLA_NOPE + MLA_V, MLA_NOPE).astype(BF16)
            w_v = w_ukv[i].reshape(MLA_KV_RANK, MLA_HEADS, MLA_NOPE + MLA_V)[:, :, MLA_NOPE:].reshape(
                MLA_KV_RANK, MLA_HEADS * MLA_V).astype(BF16)
            (q, k, v, dq, dk, dv, ckv_st, kpe_st, dk_st, dv_st) = _premix0(
                *xs, mod_l, g_l[0:1], w_in_p, mla_q_norm[i][None, :], w_uq_p, mla_kv_norm[i][None, :],
                w_k_p, w_v, t32, t64)
            states['mla_ckv'] = ckv_st.reshape(BATCH, 1, SEQ, MLA_KV_RANK)
            states['mla_krope'] = kpe_st[:, MLA_NOPE:MLA_NOPE + MLA_ROPE].reshape(BATCH, 1, SEQ, MLA_ROPE)
            states['diff_k'] = dk_st.reshape(BATCH, 1, SEQ, DIFF_HEADS, 2 * DIFF_DH)
            states['diff_v'] = dv_st.reshape(BATCH, 1, SEQ, DIFF_HEADS, 2 * DIFF_DH)
            kc, vc = _mla_cache(cache_mla_ckv[:, i].reshape(DEC_BATCH * PAST_LEN, MLA_KV_RANK),
                                _pe_slab(cache_mla_krope[:, i].reshape(DEC_BATCH * PAST_LEN, MLA_ROPE)),
                                w_k_p, w_v)
            dkc = cache_diff_k[:, i].reshape(DEC_BATCH * PAST_LEN, 512).astype(BF16)
            dvc = cache_diff_v[:, i].reshape(DEC_BATCH * PAST_LEN, 512).astype(BF16)
            lam = diff_lambda[i]
            sub_g = diff_norm[i][None, :]
            o_p = _attn_ab(q, dq, (k, v, dk, dv), None, lam, sub_g, lam_init,
                           n_batch=BATCH, t_len=SEQ, tq=SEQ, tok_off=0, n_seq=PROMPT_SEQS_PER_STEP)
            o_cat = _attn_ab(q, dq, (k, v, dk, dv), (kc, vc, dkc, dvc), lam, sub_g, lam_init,
                             n_batch=DEC_BATCH, t_len=DEC_SEQ, tq=TQ_AB, tok_off=NP_TOK, out_init=o_p)
            w_out = w_out0[i]
        else:
            wi = w_in1[i]
            n_sq = SWA_HEADS * HEAD_DIM
            w_in_p = jnp.concatenate([_pair_kv_heads(wi[:, :n_sq].T).T, wi[:, n_sq:]], axis=1).astype(BF16)
            (sq, sk, sv, nq, nk, nv, sk_st, sv_st, nk_st, nv_st) = _premix1(xs[0], mod_l, g_l[0:1], w_in_p, t64)
            states['swa_k'] = sk_st.reshape(BATCH, 1, SEQ, SWA_KV_HEADS, HEAD_DIM)
            states['swa_v'] = sv_st.reshape(BATCH, 1, SEQ, SWA_KV_HEADS, HEAD_DIM)
            states['na_k'] = nk_st.reshape(BATCH, 1, SEQ, NA_HEADS, HEAD_DIM)
            states['na_v'] = nv_st.reshape(BATCH, 1, SEQ, NA_HEADS, HEAD_DIM)
            skc = cache_swa_k[:, i].reshape(DEC_BATCH * PAST_LEN, 128).astype(BF16)
            svc = cache_swa_v[:, i].reshape(DEC_BATCH * PAST_LEN, 128).astype(BF16)
            nkc = cache_na_k[:, i].reshape(DEC_BATCH * PAST_LEN, 512).astype(BF16)
            nvc = cache_na_v[:, i].reshape(DEC_BATCH * PAST_LEN, 512).astype(BF16)
            sink = swa_sink[i]
            o_p = _attn_cd_prompt(sink, sq, sk, sv, nq, nk, nv)
            o_cat = _attn_cd_sample(o_p, sink, sq, sk, sv, nq, nk, nv, skc, svc, nkc, nvc, _na_bias(na_rpb[i]))
            wo = w_out1[i]
            w_out = jnp.concatenate([_pair_kv_heads(wo[:n_sq]), wo[n_sq:]], axis=0)
        xs = _ffn(l, o_cat, xs, mod_l, g_l, w_out, w_router[l], b_router[l], w_gu, b_gu, w_down, b_down,
                  split_out=(l == DEPTH - 1))
    return (xs[0].reshape(BATCH, SEQ, D_MODEL), xs[1].reshape(DEC_BATCH, DEC_SEQ, D_MODEL),
            states['mla_ckv'], states['mla_krope'], states['diff_k'], states['diff_v'],
            states['swa_k'], states['swa_v'], states['na_k'], states['na_v'])
```
